```python
import jax, jax.numpy as jnp
from jax import lax
import numpy as np

D_MODEL = 1024
BATCH = 16
SEQ = 256
DEPTH = 1
DEC_BATCH = 4
DEC_SEQ = 4096
PAST_LEN = 512

GRID_W = 64
ATT_WIDTH = D_MODEL // 2
HEAD_DIM = 64
N_HEADS = ATT_WIDTH // HEAD_DIM
KV_HEADS = 2
GROUP = N_HEADS // KV_HEADS
ATT_SCALE = HEAD_DIM ** -0.5
ROPE_AXIS_DIM = HEAD_DIM // 2
ROPE_THETA = 10000.0
Q_BLOCK = 128
GLA_WIDTH = D_MODEL - ATT_WIDTH
GLA_HEADS = 4
GLA_DK = 64
GLA_DV = GLA_WIDTH // GLA_HEADS
GLA_GATE_RANK = 16
GLA_TAU = 16.0
GLA_CHUNK = 64
IN_SPLITS = [N_HEADS * HEAD_DIM, KV_HEADS * HEAD_DIM, KV_HEADS * HEAD_DIM,
             GLA_HEADS * GLA_DK, GLA_HEADS * GLA_DK, GLA_WIDTH, GLA_WIDTH,
             GLA_GATE_RANK, GLA_GATE_RANK]
IN_COLS = sum(IN_SPLITS)
N_EXPERTS = 256
TOP_K = 8
EXPERT_FF = 256
SHARED_FF = 256
ROUTED_SCALE = 2.5
MOE_BLOCK = 128
ALPHA = (2.0 * DEPTH) ** 0.25
BETA = (8.0 * DEPTH) ** -0.25
EPS = 1e-6

kernel_name = 'hybrid_gqa_gla_moe_diffusion_step'


def rms_norm(x, g):
    xf = x.astype(jnp.float32)
    y = xf * lax.rsqrt(jnp.mean(xf * xf, axis=-1, keepdims=True) + EPS)
    return (y * g.astype(jnp.float32)).astype(x.dtype)


def layer_norm(x, g, b):
    xf = x.astype(jnp.float32)
    mu = jnp.mean(xf, axis=-1, keepdims=True)
    var = jnp.mean(jnp.square(xf - mu), axis=-1, keepdims=True)
    y = (xf - mu) * lax.rsqrt(var + EPS)
    return (y * g.astype(jnp.float32) + b.astype(jnp.float32)).astype(x.dtype)


def rope_tables(n_tok):
    rows = n_tok // GRID_W
    row_idx = jnp.repeat(jnp.arange(rows, dtype=jnp.float32), GRID_W)
    col_idx = jnp.tile(jnp.arange(GRID_W, dtype=jnp.float32), rows)
    inv_freq = 1.0 / (ROPE_THETA ** (jnp.arange(0, ROPE_AXIS_DIM, 2, dtype=jnp.float32) / ROPE_AXIS_DIM))
    ang_r = row_idx[:, None] * inv_freq[None, :]
    ang_c = col_idx[:, None] * inv_freq[None, :]
    ang = jnp.concatenate([ang_r, ang_r, ang_c, ang_c], axis=-1)
    return jnp.cos(ang), jnp.sin(ang)


def apply_rope(x, cos, sin):
    r1, r2, c1, c2 = jnp.split(x, 4, axis=-1)
    rot = jnp.concatenate([-r2, r1, -c2, c1], axis=-1)
    cos = cos[None, :, None, :].astype(x.dtype)
    sin = sin[None, :, None, :].astype(x.dtype)
    return x * cos + rot * sin


def attention_blocks(q, k, v):
    b, t = q.shape[0], q.shape[1]
    nb = t // Q_BLOCK
    qb = q.reshape(b, nb, Q_BLOCK, KV_HEADS, GROUP, HEAD_DIM).transpose(1, 0, 2, 3, 4, 5)

    def one_block(qi):
        s = jnp.einsum('bqkgd,bskd->bkgqs', qi, k, preferred_element_type=jnp.float32) * ATT_SCALE
        p = jax.nn.softmax(s, axis=-1)
        return jnp.einsum('bkgqs,bskd->bqkgd', p.astype(v.dtype), v)

    o = lax.map(one_block, qb)
    return o.transpose(1, 0, 2, 3, 4, 5).reshape(b, t, N_HEADS * HEAD_DIM)


def gla_scan(q, k, v, g, s0):
    dt = v.dtype
    b, t = q.shape[0], q.shape[1]
    n = t // GLA_CHUNK

    def chunks(a):
        return jnp.moveaxis(a.astype(jnp.float32).reshape(b, n, GLA_CHUNK, *a.shape[2:]), 1, 0)

    mask = jnp.tril(jnp.ones((GLA_CHUNK, GLA_CHUNK), dtype=bool))[None, :, :, None, None]

    def step(s, inp):
        qc, kc, vc, gc = inp
        bcum = jnp.cumsum(gc, axis=1)
        inter = jnp.einsum('bihk,bhkv->bihv', qc * jnp.exp(bcum), s)
        diff = bcum[:, :, None] - bcum[:, None, :]
        decay = jnp.exp(jnp.where(mask, diff, -jnp.inf))
        a = jnp.einsum('bihk,bjhk,bijhk->bhij', qc, kc, decay)
        intra = jnp.einsum('bhij,bjhv->bihv', a, vc)
        blast = bcum[:, -1]
        s_new = s * jnp.exp(blast)[..., None] + jnp.einsum(
            'bjhk,bjhv->bhkv', kc * jnp.exp(blast[:, None] - bcum), vc)
        return s_new, inter + intra

    s_fin, o = lax.scan(step, s0.astype(jnp.float32), (chunks(q), chunks(k), chunks(v), chunks(g)))
    o = jnp.moveaxis(o, 0, 1).reshape(b, t, GLA_HEADS, GLA_DV)
    return o.astype(dt), s_fin.astype(dt)


def swiglu(x, w1, w3, w2):
    return (jax.nn.silu(x @ w1) * (x @ w3)) @ w2


def moe(u, w_router, router_bias, w1, w3, w2, sw1, sw3, sw2):
    b, t, d = u.shape
    xt = u.reshape(-1, d)
    n = xt.shape[0]
    scores = jax.nn.sigmoid(jnp.dot(xt, w_router, preferred_element_type=jnp.float32))
    _, idx = lax.top_k(scores + router_bias.astype(jnp.float32), TOP_K)
    s_sel = jnp.take_along_axis(scores, idx, axis=1)
    wts = (s_sel / jnp.sum(s_sel, axis=-1, keepdims=True) * ROUTED_SCALE).astype(u.dtype)
    n_assign = n * TOP_K
    flat_e = idx.reshape(-1)
    flat_tok = jnp.repeat(jnp.arange(n, dtype=jnp.int32), TOP_K)
    flat_w = wts.reshape(-1)
    order = jnp.argsort(flat_e)
    e_s, tok_s, w_s = flat_e[order], flat_tok[order], flat_w[order]
    counts = jnp.bincount(flat_e, length=N_EXPERTS)
    starts = jnp.cumsum(counts) - counts
    pcounts = (counts + MOE_BLOCK - 1) // MOE_BLOCK * MOE_BLOCK
    pends = jnp.cumsum(pcounts)
    pstarts = pends - pcounts
    n_blocks = n_assign // MOE_BLOCK + N_EXPERTS
    dest = pstarts[e_s] + jnp.arange(n_assign, dtype=jnp.int32) - starts[e_s]
    row_tok = jnp.full((n_blocks * MOE_BLOCK,), n, dtype=jnp.int32).at[dest].set(tok_s)
    row_w = jnp.zeros((n_blocks * MOE_BLOCK,), dtype=u.dtype).at[dest].set(w_s)
    block_exp = jnp.clip(jnp.searchsorted(pends, jnp.arange(n_blocks, dtype=jnp.int32) * MOE_BLOCK,
                                          side='right'), 0, N_EXPERTS - 1).astype(jnp.int32)
    x_pad = jnp.concatenate([xt, jnp.zeros((1, d), dtype=xt.dtype)], axis=0)

    def run_block(blk):
        toks, e = blk
        return swiglu(x_pad[toks], w1[e], w3[e], w2[e])

    out = lax.map(run_block, (row_tok.reshape(n_blocks, MOE_BLOCK), block_exp))
    y = jnp.zeros((n + 1, d), dtype=u.dtype).at[row_tok].add(out.reshape(-1, d) * row_w[:, None])[:n]
    return (y + swiglu(xt, sw1, sw3, sw2)).reshape(b, t, d)


def trunk_layer(x, mod, rope, ctx_kv, s0_fwd, s0_bwd, w_in, q_norm, k_norm, wa_fwd, ba_fwd,
                wa_bwd, ba_bwd, gla_norm, w_out, ln1_g, ln1_b, ln2_g, ln2_b, w_router, router_bias,
                w1, w3, w2, sw1, sw3, sw2):
    b, t, _ = x.shape
    shift1, scale1, gate1, shift2, scale2, gate2 = [m[:, None, :] for m in jnp.split(mod, 6, axis=-1)]
    u = x * (1 + scale1) + shift1
    proj = u @ w_in
    aq, ak, av, gq, gk, gv, gg, ra_f, ra_b = jnp.split(proj, list(np.cumsum(IN_SPLITS)[:-1]), axis=-1)
    q = rms_norm(aq.reshape(b, t, N_HEADS, HEAD_DIM), q_norm)
    k = rms_norm(ak.reshape(b, t, KV_HEADS, HEAD_DIM), k_norm)
    v = av.reshape(b, t, KV_HEADS, HEAD_DIM)
    if rope is not None:
        q = apply_rope(q, *rope)
        k_rot = apply_rope(k, *rope)
    else:
        k_rot = k
    if ctx_kv is not None:
        k_all = jnp.concatenate([k_rot, ctx_kv[0].astype(k.dtype)], axis=1)
        v_all = jnp.concatenate([v, ctx_kv[1].astype(v.dtype)], axis=1)
    else:
        k_all, v_all = k_rot, v
    att = attention_blocks(q, k_all, v_all)
    qg = gq.reshape(b, t, GLA_HEADS, GLA_DK) * (GLA_DK ** -0.5)
    kg = gk.reshape(b, t, GLA_HEADS, GLA_DK)
    vg = gv.reshape(b, t, GLA_HEADS, GLA_DV)
    la_f = (jax.nn.log_sigmoid((ra_f @ wa_fwd + ba_fwd).astype(jnp.float32)) / GLA_TAU).reshape(b, t, GLA_HEADS, GLA_DK)
    la_b = (jax.nn.log_sigmoid((ra_b @ wa_bwd + ba_bwd).astype(jnp.float32)) / GLA_TAU).reshape(b, t, GLA_HEADS, GLA_DK)
    o_f, s_f = gla_scan(qg, kg, vg, la_f, s0_fwd)
    o_b, s_b = gla_scan(qg[:, ::-1], kg[:, ::-1], vg[:, ::-1], la_b[:, ::-1], s0_bwd)
    o_gla = rms_norm(o_f + o_b[:, ::-1], gla_norm).reshape(b, t, GLA_WIDTH) * jax.nn.silu(gg)
    h = jnp.concatenate([att, o_gla], axis=-1) @ w_out
    x = layer_norm(ALPHA * x + gate1 * h, ln1_g, ln1_b)
    u2 = x * (1 + scale2) + shift2
    f = moe(u2, w_router, router_bias, w1, w3, w2, sw1, sw3, sw2)
    x = layer_norm(ALPHA * x + gate2 * f, ln2_g, ln2_b)
    return x, k, v, s_f, s_b


def setup_inputs(seed: int = 0) -> dict:
    key = jax.random.key(seed)
    ks = iter(jax.random.split(key, 40))
    nrm = lambda shape, s: jax.random.normal(next(ks), shape, dtype=jnp.float32) * s
    d = D_MODEL
    return {
        'x_prompt': nrm((BATCH, SEQ, d), 1.0),
        'x_sample': nrm((DEC_BATCH, DEC_SEQ, d), 1.0),
        'cache_k': nrm((DEC_BATCH, DEPTH, PAST_LEN, KV_HEADS, HEAD_DIM), 1.0),
        'cache_v': nrm((DEC_BATCH, DEPTH, PAST_LEN, KV_HEADS, HEAD_DIM), 1.0),
        'state_gla_fwd': nrm((DEC_BATCH, DEPTH, GLA_HEADS, GLA_DK, GLA_DV), 0.5),
        'state_gla_bwd': nrm((DEC_BATCH, DEPTH, GLA_HEADS, GLA_DK, GLA_DV), 0.5),
        'c': nrm((DEC_BATCH, d), 1.0),
        'c_ctx': nrm((d,), 1.0),
        'w_ada': nrm((DEPTH, d, 6 * d), 0.5 * d ** -0.5),
        'b_ada': nrm((DEPTH, 6 * d), 0.02),
        'w_in': nrm((DEPTH, d, IN_COLS), d ** -0.5),
        'q_norm': 1.0 + nrm((DEPTH, HEAD_DIM), 0.02),
        'k_norm': 1.0 + nrm((DEPTH, HEAD_DIM), 0.02),
        'gla_wa_fwd': nrm((DEPTH, GLA_GATE_RANK, GLA_HEADS * GLA_DK), GLA_GATE_RANK ** -0.5),
        'gla_ba_fwd': nrm((DEPTH, GLA_HEADS * GLA_DK), 0.1),
        'gla_wa_bwd': nrm((DEPTH, GLA_GATE_RANK, GLA_HEADS * GLA_DK), GLA_GATE_RANK ** -0.5),
        'gla_ba_bwd': nrm((DEPTH, GLA_HEADS * GLA_DK), 0.1),
        'gla_norm': 1.0 + nrm((DEPTH, GLA_DV), 0.02),
        'w_out': nrm((DEPTH, d, d), BETA * d ** -0.5),
        'ln1_g': 1.0 + nrm((DEPTH, d), 0.02),
        'ln1_b': nrm((DEPTH, d), 0.02),
        'ln2_g': 1.0 + nrm((DEPTH, d), 0.02),
        'ln2_b': nrm((DEPTH, d), 0.02),
        'w_router': nrm((DEPTH, d, N_EXPERTS), d ** -0.5),
        'router_bias': nrm((DEPTH, N_EXPERTS), 0.01),
        'exp_w1': nrm((DEPTH, N_EXPERTS, d, EXPERT_FF), d ** -0.5),
        'exp_w3': nrm((DEPTH, N_EXPERTS, d, EXPERT_FF), d ** -0.5),
        'exp_w2': nrm((DEPTH, N_EXPERTS, EXPERT_FF, d), BETA * EXPERT_FF ** -0.5),
        'sh_w1': nrm((DEPTH, d, SHARED_FF), d ** -0.5),
        'sh_w3': nrm((DEPTH, d, SHARED_FF), d ** -0.5),
        'sh_w2': nrm((DEPTH, SHARED_FF, d), BETA * SHARED_FF ** -0.5),
    }


def reference(x_prompt, x_sample, cache_k, cache_v, state_gla_fwd, state_gla_bwd, c, c_ctx,
              w_ada, b_ada, w_in, q_norm, k_norm, gla_wa_fwd, gla_ba_fwd, gla_wa_bwd, gla_ba_bwd,
              gla_norm, w_out, ln1_g, ln1_b, ln2_g, ln2_b, w_router, router_bias,
              exp_w1, exp_w3, exp_w2, sh_w1, sh_w3, sh_w2):
    n_ctx_b = x_prompt.shape[0]
    rope_lat = rope_tables(x_sample.shape[1])
    s_zero = jnp.zeros((n_ctx_b, GLA_HEADS, GLA_DK, GLA_DV), dtype=x_prompt.dtype)
    yp, ys = x_prompt, x_sample
    ks_new, vs_new, sf_new, sb_new = [], [], [], []
    for l in range(DEPTH):
        weights = (w_in[l], q_norm[l], k_norm[l], gla_wa_fwd[l], gla_ba_fwd[l], gla_wa_bwd[l],
                   gla_ba_bwd[l], gla_norm[l], w_out[l], ln1_g[l], ln1_b[l], ln2_g[l], ln2_b[l],
                   w_router[l], router_bias[l], exp_w1[l], exp_w3[l], exp_w2[l],
                   sh_w1[l], sh_w3[l], sh_w2[l])
        mod_ctx = jnp.broadcast_to(jax.nn.silu(c_ctx) @ w_ada[l] + b_ada[l], (n_ctx_b, 6 * D_MODEL))
        mod_lat = jax.nn.silu(c) @ w_ada[l] + b_ada[l]
        yp, k_c, v_c, s_f, s_b = trunk_layer(yp, mod_ctx, None, None, s_zero, s_zero, *weights)
        ks_new.append(k_c)
        vs_new.append(v_c)
        sf_new.append(s_f)
        sb_new.append(s_b)
        ys, _, _, _, _ = trunk_layer(ys, mod_lat, rope_lat, (cache_k[:, l], cache_v[:, l]),
                                     state_gla_fwd[:, l], state_gla_bwd[:, l], *weights)
    new_cache_k = jnp.stack(ks_new, axis=1)
    new_cache_v = jnp.stack(vs_new, axis=1)
    new_state_gla_fwd = jnp.stack(sf_new, axis=1)
    new_state_gla_bwd = jnp.stack(sb_new, axis=1)
    return (yp, ys, new_cache_k, new_cache_v, new_state_gla_fwd, new_state_gla_bwd)
```

```python
import functools

import numpy as np
import jax
import jax.numpy as jnp
from jax import lax
from jax.experimental import pallas as pl
from jax.experimental.pallas import tpu as pltpu

F32 = jnp.float32
BF16 = jnp.bfloat16
I32 = jnp.int32

D_MODEL = 1024
GRID_W = 64
HEAD_DIM = 64
N_HEADS = 8
KV_HEADS = 2
ATT_WIDTH = N_HEADS * HEAD_DIM
ATT_SCALE = HEAD_DIM ** -0.5
ROPE_AXIS_DIM = HEAD_DIM // 2
ROPE_THETA = 10000.0
GLA_HEADS = 4
GLA_DK = 64
GLA_DV = 128
GLA_WIDTH = GLA_HEADS * GLA_DV
GLA_KW = GLA_HEADS * GLA_DK
GLA_GATE_RANK = 16
GLA_TAU = 16.0
N_EXPERTS = 256
TOP_K = 8
EXPERT_FF = 256
SHARED_FF = 256
ROUTED_SCALE = 2.5
DEPTH = 1
ALPHA = (2.0 * DEPTH) ** 0.25
EPS = 1e-6

LANES = 128
SUBLANES = 8
ROW_CHUNKS = D_MODEL // LANES
VMEM_LIMIT = 56 * 1024 * 1024

TOK_TILE = 512
ATT_TQ = 256
GLA_CHUNK = 128
GLA_LEVELS = ((32, 128), (8, 32), (2, 8), (1, 2))
MOE_ROWS = 256
DISP_TILE = 256
COMB_TILE = 128
HIGHEST = lax.Precision.HIGHEST


def _cparams(n_axes):
    return pltpu.CompilerParams(dimension_semantics=("arbitrary",) * n_axes,
                                vmem_limit_bytes=VMEM_LIMIT)


def _silu(x):
    return x * jax.nn.sigmoid(x)


def _log_sigmoid(x):
    return jnp.minimum(x, 0.0) - jnp.log(1.0 + jnp.exp(-jnp.abs(x)))


def _layer_norm(z, g, b):
    mu = jnp.mean(z, axis=-1, keepdims=True)
    zc = z - mu
    var = jnp.mean(zc * zc, axis=-1, keepdims=True)
    return zc * lax.rsqrt(var + EPS) * g + b


def _mod_kernel(c_ref, w_ref, b_ref, o_ref):
    s = _silu(c_ref[...]).astype(BF16)
    o_ref[...] = jnp.dot(s, w_ref[...].astype(BF16), preferred_element_type=F32) + b_ref[...]


def _modulation(c_rows, w_ada, b_ada):
    n_cols = w_ada.shape[1]
    tn = 512
    return pl.pallas_call(
        _mod_kernel,
        grid=(n_cols // tn,),
        in_specs=[pl.BlockSpec((SUBLANES, D_MODEL), lambda j: (0, 0)),
                  pl.BlockSpec((D_MODEL, tn), lambda j: (0, j)),
                  pl.BlockSpec((1, tn), lambda j: (0, j))],
        out_specs=pl.BlockSpec((SUBLANES, tn), lambda j: (0, j)),
        out_shape=jax.ShapeDtypeStruct((SUBLANES, n_cols), F32),
        compiler_params=_cparams(1),
        name="modulation",
    )(c_rows, w_ada, b_ada)


_C_Q = 0
_C_K = _C_Q + ATT_WIDTH
_C_V = _C_K + 2 * LANES
_C_GQ = _C_V + 2 * LANES
_C_GV = _C_GQ + GLA_KW
_C_GG = _C_GV + GLA_WIDTH
_C_RA = _C_GG + GLA_WIDTH
_C_END = _C_RA + LANES


def _inproj_kernel(x_ref, mod_ref, w_ref, wt_ref, qn_ref, kn_ref, cos_ref, sa_ref, sb_ref,
                   seg_ref, wa_ref, ba_ref, wat_ref, bat_ref,
                   q_ref, k_ref, v_ref, k32_ref, v32_ref, gq_ref, gv_ref, gg_ref,
                   la_ref, gkt_ref, lat_ref, *, n_ctx_tiles):
    i = pl.program_id(0)
    m = mod_ref[...]
    shift1 = m[:, 0:D_MODEL]
    scale1 = m[:, D_MODEL:2 * D_MODEL]
    u = (x_ref[...] * (1.0 + scale1) + shift1).astype(BF16)

    cos = cos_ref[...]
    sin_a = sa_ref[...]
    sin_b = sb_ref[...]
    seg = seg_ref[...]
    lane = lax.broadcasted_iota(I32, (u.shape[0], LANES), 1)
    low = lane < HEAD_DIM

    def proj(c0, c1):
        return jnp.dot(u, w_ref[:, c0:c1], preferred_element_type=F32)

    def head_norm(blk, gain):
        ss = jnp.dot((blk * blk).astype(BF16), seg, preferred_element_type=F32) * (1.0 / HEAD_DIM)
        return blk * lax.rsqrt(ss + EPS) * gain

    def rope(blk):
        return (blk * cos + pltpu.roll(blk, LANES - ROPE_AXIS_DIM // 2, 1) * sin_a
                + pltpu.roll(blk, ROPE_AXIS_DIM // 2, 1) * sin_b)

    pq = proj(_C_Q, _C_K)
    for j in range(ATT_WIDTH // LANES):
        qn = head_norm(pq[:, j * LANES:(j + 1) * LANES], qn_ref[...])
        q_ref[:, j * LANES:(j + 1) * LANES] = (rope(qn) * ATT_SCALE).astype(BF16)

    pk = proj(_C_K, _C_V)
    kn = [head_norm(pk[:, j * LANES:(j + 1) * LANES], kn_ref[...]) for j in range(KV_HEADS)]
    for j in range(KV_HEADS):
        k_ref[:, j * LANES:(j + 1) * LANES] = rope(kn[j]).astype(BF16)
    pv = proj(_C_V, _C_GQ)
    v_ref[...] = pv.astype(BF16)

    @pl.when(i < n_ctx_tiles)
    def _():
        k32_ref[...] = jnp.where(low, kn[0], kn[1])
        v32_ref[...] = jnp.where(low, pv[:, 0:LANES], pv[:, LANES:2 * LANES])

    gq_ref[...] = proj(_C_GQ, _C_GV) * (GLA_DK ** -0.5)
    gv_ref[...] = proj(_C_GV, _C_GG).astype(BF16)
    gg_ref[...] = proj(_C_GG, _C_RA).astype(BF16)

    ra = proj(_C_RA, _C_END)
    pre = jnp.dot(ra, wa_ref[...], precision=HIGHEST, preferred_element_type=F32) + ba_ref[...]
    la_ref[...] = _log_sigmoid(pre) * (1.0 / GLA_TAU)

    pt = lax.dot_general(wt_ref[...], u, (((1,), (1,)), ((), ())), preferred_element_type=F32)
    gkt_ref[...] = pt[0:GLA_KW, :]
    rat = pt[GLA_KW:GLA_KW + 2 * GLA_GATE_RANK, :]
    pre_t = jnp.dot(wat_ref[...], rat, precision=HIGHEST, preferred_element_type=F32) + bat_ref[...]
    lat_ref[...] = _log_sigmoid(pre_t) * (1.0 / GLA_TAU)


def _in_projection(x_all, mod3, w_tok, w_tr, qn, kn, cos_t, sa_t, sb_t, seg, wa, ba, wat, bat,
                   n_ctx, n_seq_tiles):
    n = x_all.shape[0]
    tb = TOK_TILE
    n_ctx_tiles = n_ctx // tb
    n_tiles = n // tb
    n_rope_blocks = cos_t.shape[0] // tb - 1

    def mod_idx(i):
        return (jnp.where(i < n_ctx_tiles, 0, 1 + (i - n_ctx_tiles) // n_seq_tiles), 0, 0)

    def rope_idx(i):
        return (jnp.where(i < n_ctx_tiles, n_rope_blocks, (i - n_ctx_tiles) % n_seq_tiles), 0)

    def ctx_idx(i):
        return (jnp.minimum(i, n_ctx_tiles - 1), 0)

    tok = lambda w: pl.BlockSpec((tb, w), lambda i: (i, 0))
    full = lambda a: pl.BlockSpec(a.shape, lambda i: (0,) * a.ndim)
    tr = lambda r: pl.BlockSpec((r, tb), lambda i: (0, i))
    out_shapes = (
        jax.ShapeDtypeStruct((n, ATT_WIDTH), BF16),
        jax.ShapeDtypeStruct((n, 2 * LANES), BF16),
        jax.ShapeDtypeStruct((n, 2 * LANES), BF16),
        jax.ShapeDtypeStruct((n_ctx, LANES), F32),
        jax.ShapeDtypeStruct((n_ctx, LANES), F32),
        jax.ShapeDtypeStruct((n, GLA_KW), F32),
        jax.ShapeDtypeStruct((n, GLA_WIDTH), BF16),
        jax.ShapeDtypeStruct((n, GLA_WIDTH), BF16),
        jax.ShapeDtypeStruct((n, 2 * GLA_KW), F32),
        jax.ShapeDtypeStruct((GLA_KW, n), F32),
        jax.ShapeDtypeStruct((2 * GLA_KW, n), F32),
    )
    out_specs = (tok(ATT_WIDTH), tok(2 * LANES), tok(2 * LANES),
                 pl.BlockSpec((tb, LANES), ctx_idx), pl.BlockSpec((tb, LANES), ctx_idx),
                 tok(GLA_KW), tok(GLA_WIDTH), tok(GLA_WIDTH), tok(2 * GLA_KW),
                 tr(GLA_KW), tr(2 * GLA_KW))
    in_specs = [tok(D_MODEL),
                pl.BlockSpec((None, 1, mod3.shape[2]), mod_idx),
                full(w_tok), full(w_tr), full(qn), full(kn),
                pl.BlockSpec((tb, LANES), rope_idx), pl.BlockSpec((tb, LANES), rope_idx),
                pl.BlockSpec((tb, LANES), rope_idx),
                full(seg), full(wa), full(ba), full(wat), full(bat)]
    return pl.pallas_call(
        functools.partial(_inproj_kernel, n_ctx_tiles=n_ctx_tiles),
        grid=(n_tiles,), in_specs=in_specs, out_specs=out_specs, out_shape=out_shapes,
        compiler_params=_cparams(1), name="in_projection",
    )(x_all, mod3, w_tok, w_tr, qn, kn, cos_t, sa_t, sb_t, seg, wa, ba, wat, bat)


def _attention_kernel(*refs, n_kv_parts):
    q_ref = refs[0]
    k_refs = refs[1:1 + n_kv_parts]
    v_refs = refs[1 + n_kv_parts:1 + 2 * n_kv_parts]
    o_ref = refs[1 + 2 * n_kv_parts]
    tq = q_ref.shape[0]
    lane = lax.broadcasted_iota(I32, (tq, LANES), 1)
    low = lane < HEAD_DIM
    for p in range(ATT_WIDTH // LANES):
        kv = p // (ATT_WIDTH // LANES // KV_HEADS)
        qp = q_ref[:, p * LANES:(p + 1) * LANES]
        outs = []
        for half in range(2):
            qm = jnp.where(low if half == 0 else jnp.logical_not(low), qp, jnp.zeros_like(qp))
            s = [lax.dot_general(qm, k[:, kv * LANES:(kv + 1) * LANES], (((1,), (1,)), ((), ())),
                                 preferred_element_type=F32) for k in k_refs]
            mx = functools.reduce(jnp.maximum, [jnp.max(x, axis=-1, keepdims=True) for x in s])
            pr = [jnp.exp(x - mx) for x in s]
            den = functools.reduce(jnp.add, [jnp.sum(x, axis=-1, keepdims=True) for x in pr])
            acc = functools.reduce(jnp.add, [
                jnp.dot(x.astype(BF16), v[:, kv * LANES:(kv + 1) * LANES], preferred_element_type=F32)
                for x, v in zip(pr, v_refs)])
            outs.append(acc / den)
        o_ref[:, p * LANES:(p + 1) * LANES] = jnp.where(low, outs[0], outs[1]).astype(BF16)


def _attention(q, k, v, extra_kv, row0, n_batch, seq):
    tq = min(ATT_TQ, seq)
    n_q = seq // tq
    q_blk0 = row0 // tq
    kv_blk0 = row0 // seq
    in_specs = [pl.BlockSpec((tq, ATT_WIDTH), lambda b, i: (q_blk0 + b * n_q + i, 0))]
    kv_spec = pl.BlockSpec((seq, 2 * LANES), lambda b, i: (kv_blk0 + b, 0))
    args_k, args_v, specs_k, specs_v = [k], [v], [kv_spec], [kv_spec]
    if extra_kv is not None:
        ck, cv = extra_kv
        c_spec = pl.BlockSpec((None, ck.shape[1], 2 * LANES), lambda b, i: (b, 0, 0))
        args_k.append(ck); args_v.append(cv); specs_k.append(c_spec); specs_v.append(c_spec)
    return pl.pallas_call(
        functools.partial(_attention_kernel, n_kv_parts=len(args_k)),
        grid=(n_batch, n_q),
        in_specs=in_specs + specs_k + specs_v,
        out_specs=pl.BlockSpec((tq, ATT_WIDTH), lambda b, i: (b * n_q + i, 0)),
        out_shape=jax.ShapeDtypeStruct((n_batch * seq, ATT_WIDTH), BF16),
        compiler_params=_cparams(2), name="attention",
    )(q, *args_k, *args_v)


def _gla_constants():
    c = GLA_CHUNK
    idx = np.arange(c)
    q_mats, k_mats, masks, levels_of = [], [], [], []
    for li, (s, p) in enumerate(GLA_LEVELS):
        start = (idx // s) * s
        end = start + s - 1
        k_mats.append(((idx[None, :] > idx[:, None]) & (idx[None, :] <= end[:, None])))
        for d in range(p // s - 1):
            lo = np.maximum(start - d * s, 0)
            q_mats.append((idx[None, :] >= lo[:, None]) & (idx[None, :] <= idx[:, None]))
            masks.append((idx[:, None] // p == idx[None, :] // p)
                         & (idx[:, None] // s - idx[None, :] // s - 1 == d))
            levels_of.append(li)
    masks.append(np.eye(c, dtype=bool))
    levels_of.append(len(GLA_LEVELS) - 1)
    q_mats.append(idx[None, :] <= idx[:, None])
    k_mats = k_mats[:-1]
    k_mats.append(idx[None, :] > idx[:, None])
    k_mats.append(np.ones((c, c), bool))
    out = {}
    for name, flip in (("f", False), ("b", True)):
        f = (lambda a: a[::-1, ::-1]) if flip else (lambda a: a)
        lq = np.concatenate([f(a) for a in q_mats], axis=0).astype(np.float32)
        lkt = np.concatenate([f(a).T for a in k_mats], axis=1).astype(np.float32)
        mk = np.stack([np.tile(f(a), (1, GLA_HEADS)) for a in masks]).astype(np.float32)
        out[name] = (np.concatenate([lq, lq], axis=1), np.concatenate([lkt, lkt], axis=0), mk)
    return out, tuple(levels_of)


def _gla_direction(q, g, gkt, gt, v, lq2, lkt2, masks_ref, bd, vbd, s_ref, levels_of):
    c = GLA_CHUNK
    n_var = len(levels_of)
    n_lev = len(GLA_LEVELS)
    g_hi = g.astype(BF16)
    g_lo = (g - g_hi.astype(F32)).astype(BF16)
    fq = jnp.dot(lq2, jnp.concatenate([g_hi, g_lo], axis=0), preferred_element_type=F32)
    gt_hi = gt.astype(BF16)
    gt_lo = (gt - gt_hi.astype(F32)).astype(BF16)
    fk = jnp.dot(jnp.concatenate([gt_hi, gt_lo], axis=1), lkt2, preferred_element_type=F32)

    def key_factor(f):
        return gkt * jnp.exp(fk[:, f * c:(f + 1) * c])

    q_var = [(q * jnp.exp(fq[vi * c:(vi + 1) * c, :])).astype(BF16) for vi in range(n_var - 1)]
    q_var.append(q.astype(BF16))
    a = jnp.zeros((c, GLA_HEADS * c), F32)
    for li in range(n_lev):
        kt = (key_factor(li) if li < n_lev - 1 else gkt).astype(BF16)
        xt = jnp.concatenate([kt] * GLA_HEADS, axis=1) * bd
        vis = [vi for vi in range(n_var) if levels_of[vi] == li]
        res = jnp.dot(jnp.concatenate([q_var[vi] for vi in vis], axis=0), xt,
                      preferred_element_type=F32)
        for r, vi in enumerate(vis):
            a = a + masks_ref[vi] * res[r * c:(r + 1) * c, :]
    q_in = (q * jnp.exp(fq[(n_var - 1) * c:n_var * c, :])).astype(BF16)
    state = s_ref[...]
    v_bd = jnp.concatenate([v] * GLA_HEADS, axis=0) * vbd
    o = (jnp.dot(q_in, state.astype(BF16), preferred_element_type=F32)
         + jnp.dot(a.astype(BF16), v_bd, preferred_element_type=F32))
    k_out = key_factor(n_lev - 1).astype(BF16)
    e_tot = jnp.exp(fk[:, n_lev * c:(n_lev + 1) * c])
    upd = jnp.dot(k_out, v, preferred_element_type=F32)
    s_ref[...] = (state * jnp.concatenate([e_tot] * (GLA_WIDTH // c), axis=1)
                  + upd * bd.astype(F32))
    return o


def _gla_kernel(gq_f, la_f, gkt_f, lat_f, gv_f, gq_b, la_b, gkt_b, lat_b, gv_b,
                s0f_ref, s0b_ref, lq2f, lkt2f, mkf, lq2b, lkt2b, mkb, bd_ref, vbd_ref,
                of_ref, ob_ref, sf_ref, sb_ref, st_f, st_b, *, levels_of):
    n = pl.program_id(1)

    @pl.when(n == 0)
    def _():
        st_f[...] = jnp.zeros_like(st_f)
        st_b[...] = jnp.zeros_like(st_b)
        for h in range(GLA_HEADS):
            rows = slice(h * GLA_DK, (h + 1) * GLA_DK)
            cols = slice(h * GLA_DV, (h + 1) * GLA_DV)
            st_f[rows, cols] = s0f_ref[h]
            st_b[rows, cols] = s0b_ref[h]

    bd = bd_ref[...]
    vbd = vbd_ref[...]
    of_ref[...] = _gla_direction(gq_f[...], la_f[...], gkt_f[...], lat_f[...], gv_f[...],
                                 lq2f[...], lkt2f[...], mkf, bd, vbd, st_f, levels_of)
    ob_ref[...] = _gla_direction(gq_b[...], la_b[...], gkt_b[...], lat_b[...], gv_b[...],
                                 lq2b[...], lkt2b[...], mkb, bd, vbd, st_b, levels_of)

    @pl.when(n == pl.num_programs(1) - 1)
    def _():
        for h in range(GLA_HEADS):
            rows = slice(h * GLA_DK, (h + 1) * GLA_DK)
            cols = slice(h * GLA_DV, (h + 1) * GLA_DV)
            sf_ref[h] = st_f[rows, cols]
            sb_ref[h] = st_b[rows, cols]


def _gla(gq, la, gkt, lat, gv, s0f, s0b, consts, row0, n_batch, seq):
    (cf, cb), levels_of, bd, vbd = consts
    c = GLA_CHUNK
    nc = seq // c
    blk0 = row0 // c
    n_la_blocks_b = 1
    fwd = lambda b, n: blk0 + b * nc + n
    bwd = lambda b, n: blk0 + b * nc + (nc - 1 - n)

    def tok(w, which, col=0):
        return pl.BlockSpec((c, w), lambda b, n: (which(b, n), col))

    def tr(r, which, row=0):
        return pl.BlockSpec((r, c), lambda b, n: (row, which(b, n)))

    full = lambda a: pl.BlockSpec(a.shape, lambda b, n: (0,) * a.ndim)
    st_spec = pl.BlockSpec((None, GLA_HEADS, GLA_DK, GLA_DV), lambda b, n: (b, 0, 0, 0))
    in_specs = [tok(GLA_KW, fwd), tok(GLA_KW, fwd, 0), tr(GLA_KW, fwd), tr(GLA_KW, fwd, 0),
                tok(GLA_WIDTH, fwd),
                tok(GLA_KW, bwd), tok(GLA_KW, bwd, n_la_blocks_b), tr(GLA_KW, bwd),
                tr(GLA_KW, bwd, 1), tok(GLA_WIDTH, bwd),
                st_spec, st_spec,
                full(cf[0]), full(cf[1]), full(cf[2]), full(cb[0]), full(cb[1]), full(cb[2]),
                full(bd), full(vbd)]
    out_specs = (pl.BlockSpec((c, GLA_WIDTH), lambda b, n: (b * nc + n, 0)),
                 pl.BlockSpec((c, GLA_WIDTH), lambda b, n: (b * nc + (nc - 1 - n), 0)),
                 st_spec, st_spec)
    out_shape = (jax.ShapeDtypeStruct((n_batch * seq, GLA_WIDTH), F32),
                 jax.ShapeDtypeStruct((n_batch * seq, GLA_WIDTH), F32),
                 jax.ShapeDtypeStruct((n_batch, GLA_HEADS, GLA_DK, GLA_DV), F32),
                 jax.ShapeDtypeStruct((n_batch, GLA_HEADS, GLA_DK, GLA_DV), F32))
    return pl.pallas_call(
        functools.partial(_gla_kernel, levels_of=levels_of),
        grid=(n_batch, nc), in_specs=in_specs, out_specs=out_specs, out_shape=out_shape,
        scratch_shapes=[pltpu.VMEM((GLA_KW, GLA_WIDTH), F32), pltpu.VMEM((GLA_KW, GLA_WIDTH), F32)],
        compiler_params=_cparams(2), name="gla",
    )(gq, la, gkt, lat, gv, gq, la, gkt, lat, gv, s0f, s0b,
      cf[0], cf[1], cf[2], cb[0], cb[1], cb[2], bd, vbd)


def _outproj_kernel(att_ref, of_ref, ob_ref, gg_ref, x_ref, mod_ref, wo_ref, gn_ref, l1g_ref, l1b_ref,
                    wrt_ref, sw13_ref, sw2_ref, base_ref, u2_ref, lg_ref):
    m = mod_ref[...]
    gate1 = m[:, 2 * D_MODEL:3 * D_MODEL]
    shift2 = m[:, 3 * D_MODEL:4 * D_MODEL]
    scale2 = m[:, 4 * D_MODEL:5 * D_MODEL]
    gate2 = m[:, 5 * D_MODEL:6 * D_MODEL]
    og = of_ref[...] + ob_ref[...]
    gg = gg_ref[...].astype(F32)
    parts = [att_ref[...]]
    for h in range(GLA_HEADS):
        blk = og[:, h * GLA_DV:(h + 1) * GLA_DV]
        ms = jnp.mean(blk * blk, axis=-1, keepdims=True)
        nb = blk * lax.rsqrt(ms + EPS) * gn_ref[...]
        parts.append((nb * _silu(gg[:, h * GLA_DV:(h + 1) * GLA_DV])).astype(BF16))
    hcat = jnp.concatenate(parts, axis=1)
    hmix = jnp.dot(hcat, wo_ref[...], preferred_element_type=F32)
    x1 = _layer_norm(ALPHA * x_ref[...] + gate1 * hmix, l1g_ref[...], l1b_ref[...])
    u2 = x1 * (1.0 + scale2) + shift2
    u2b = u2.astype(BF16)
    lg_ref[...] = lax.dot_general(wrt_ref[...], u2b, (((1,), (1,)), ((), ())),
                                  preferred_element_type=F32)
    ab = jnp.dot(u2b, sw13_ref[...], preferred_element_type=F32)
    hid = (_silu(ab[:, 0:SHARED_FF]) * ab[:, SHARED_FF:2 * SHARED_FF]).astype(BF16)
    shared = jnp.dot(hid, sw2_ref[...], preferred_element_type=F32)
    base_ref[...] = ALPHA * x1 + gate2 * shared
    for cidx in range(ROW_CHUNKS):
        u2_ref[:, cidx, :] = u2[:, cidx * LANES:(cidx + 1) * LANES]


def _out_projection(att, o_f, o_b, gg, x_all, mod3, wo, gn, l1g, l1b, wrt, sw13, sw2,
                    n_ctx, n_seq_tiles):
    n = x_all.shape[0]
    tb = TOK_TILE
    n_ctx_tiles = n_ctx // tb

    def mod_idx(i):
        return (jnp.where(i < n_ctx_tiles, 0, 1 + (i - n_ctx_tiles) // n_seq_tiles), 0, 0)

    tok = lambda w: pl.BlockSpec((tb, w), lambda i: (i, 0))
    full = lambda a: pl.BlockSpec(a.shape, lambda i: (0,) * a.ndim)
    return pl.pallas_call(
        _outproj_kernel,
        grid=(n // tb,),
        in_specs=[tok(ATT_WIDTH), tok(GLA_WIDTH), tok(GLA_WIDTH), tok(GLA_WIDTH), tok(D_MODEL),
                  pl.BlockSpec((None, 1, mod3.shape[2]), mod_idx),
                  full(wo), full(gn), full(l1g), full(l1b), full(wrt), full(sw13), full(sw2)],
        out_specs=(tok(D_MODEL),
                   pl.BlockSpec((tb, ROW_CHUNKS, LANES), lambda i: (i, 0, 0)),
                   pl.BlockSpec((N_EXPERTS, tb), lambda i: (0, i))),
        out_shape=(jax.ShapeDtypeStruct((n, D_MODEL), F32),
                   jax.ShapeDtypeStruct((n, ROW_CHUNKS, LANES), F32),
                   jax.ShapeDtypeStruct((N_EXPERTS, n), F32)),
        compiler_params=_cparams(1), name="out_projection",
    )(att, o_f, o_b, gg, x_all, mod3, wo, gn, l1g, l1b, wrt, sw13, sw2)


def _route_kernel(lg_ref, bias_ref, upper_ref, idx_ref, w_ref, pos_ref, cnt_ref, run_ref):
    i = pl.program_id(0)

    @pl.when(i == 0)
    def _():
        run_ref[...] = jnp.zeros_like(run_ref)

    s = jax.nn.sigmoid(lg_ref[...])
    work = s + bias_ref[...]
    rows = lax.broadcasted_iota(I32, s.shape, 0).astype(F32)
    sel = jnp.zeros(s.shape, F32)
    idxs, vals = [], []
    for _ in range(TOP_K):
        mx = jnp.max(work, axis=0, keepdims=True)
        idx = jnp.min(jnp.where(work == mx, rows, float(N_EXPERTS)), axis=0, keepdims=True)
        hit = rows == idx
        vals.append(jnp.sum(jnp.where(hit, s, 0.0), axis=0, keepdims=True))
        idxs.append(idx)
        sel = jnp.where(hit, 1.0, sel)
        work = jnp.where(hit, -jnp.inf, work)
    den = functools.reduce(jnp.add, vals)
    rank = jnp.dot(sel.astype(BF16), upper_ref[...], preferred_element_type=F32) + run_ref[:, 0:1]
    for k in range(TOP_K):
        idx_ref[k:k + 1, :] = idxs[k].astype(I32)
        w_ref[k:k + 1, :] = vals[k] / den * ROUTED_SCALE
        pos_ref[k:k + 1, :] = jnp.sum(jnp.where(rows == idxs[k], rank, 0.0), axis=0,
                                      keepdims=True).astype(I32)
    run_ref[...] = run_ref[...] + jnp.sum(sel, axis=1, keepdims=True)
    cnt_ref[...] = run_ref[...]


def _route(logits_t, bias_col, upper):
    n = logits_t.shape[1]
    tt = TOK_TILE
    row = lambda dt: jax.ShapeDtypeStruct((TOP_K, n), dt)
    blk = pl.BlockSpec((TOP_K, tt), lambda i: (0, i))
    return pl.pallas_call(
        _route_kernel,
        grid=(n // tt,),
        in_specs=[pl.BlockSpec((N_EXPERTS, tt), lambda i: (0, i)),
                  pl.BlockSpec((N_EXPERTS, 1), lambda i: (0, 0)),
                  pl.BlockSpec((tt, tt), lambda i: (0, 0))],
        out_specs=(blk, blk, blk, pl.BlockSpec((N_EXPERTS, LANES), lambda i: (0, 0))),
        out_shape=(row(I32), row(F32), row(I32), jax.ShapeDtypeStruct((N_EXPERTS, LANES), F32)),
        scratch_shapes=[pltpu.VMEM((N_EXPERTS, LANES), F32)],
        compiler_params=_cparams(1), name="route",
    )(logits_t, bias_col, upper)


def _dest_kernel(cnt_ref, lower_ref, idx_ref, pos_ref, dest_ref, bexp_ref, bval_ref, nused_ref):
    cnt = cnt_ref[...]
    nblk = jnp.floor((cnt + (MOE_ROWS - 1)) * (1.0 / MOE_ROWS))
    bstart = jnp.dot(lower_ref[...], nblk, precision=HIGHEST, preferred_element_type=F32)
    bend = bstart + nblk
    pstart = bstart[:, 0:1] * MOE_ROWS
    rows = lax.broadcasted_iota(I32, (N_EXPERTS, idx_ref.shape[1]), 0)
    for k in range(TOP_K):
        hit = rows == idx_ref[k:k + 1, :]
        dest_ref[k:k + 1, :] = (jnp.sum(jnp.where(hit, pstart, 0.0), axis=0, keepdims=True)
                                .astype(I32) + pos_ref[k:k + 1, :])

    @pl.when(pl.program_id(0) == 0)
    def _():
        nb = bexp_ref.shape[1]
        bid = lax.broadcasted_iota(I32, (N_EXPERTS, nb), 1).astype(F32)
        inside = jnp.logical_and(bid >= bstart[:, 0:1], bid < bend[:, 0:1])
        erow = lax.broadcasted_iota(I32, (N_EXPERTS, nb), 0).astype(F32)
        bexp_ref[...] = jnp.sum(jnp.where(inside, erow, 0.0), axis=0, keepdims=True).astype(I32)
        valid = jnp.clip(cnt[:, 0:1] - (bid - bstart[:, 0:1]) * MOE_ROWS, 0.0, float(MOE_ROWS))
        bval_ref[...] = jnp.sum(jnp.where(inside, valid, 0.0), axis=0, keepdims=True).astype(I32)
        nused_ref[...] = jnp.max(bend, axis=0, keepdims=True).astype(I32)


def _destinations(counts, lower, idx_t, pos_t, n_blocks_pad):
    n = idx_t.shape[1]
    tt = TOK_TILE
    blk = pl.BlockSpec((TOP_K, tt), lambda i: (0, i))
    one = lambda w: pl.BlockSpec((1, w), lambda i: (0, 0))
    return pl.pallas_call(
        _dest_kernel,
        grid=(n // tt,),
        in_specs=[pl.BlockSpec((N_EXPERTS, LANES), lambda i: (0, 0)),
                  pl.BlockSpec((N_EXPERTS, N_EXPERTS), lambda i: (0, 0)), blk, blk],
        out_specs=(blk, one(n_blocks_pad), one(n_blocks_pad), one(LANES)),
        out_shape=(jax.ShapeDtypeStruct((TOP_K, n), I32),
                   jax.ShapeDtypeStruct((1, n_blocks_pad), I32),
                   jax.ShapeDtypeStruct((1, n_blocks_pad), I32),
                   jax.ShapeDtypeStruct((1, LANES), I32)),
        compiler_params=_cparams(1), name="destinations",
    )(counts, lower, idx_t, pos_t)


def _dispatch_kernel(dest_hbm, x_ref, xs_hbm, dest_smem, sem_idx, sem_rows):
    i = pl.program_id(0)
    n_idx = dest_smem.shape[0]
    cp = pltpu.make_async_copy(dest_hbm.at[pl.ds(i * n_idx, n_idx)], dest_smem, sem_idx)
    cp.start()
    cp.wait()

    def issue(t, carry):
        for k in range(TOP_K):
            pltpu.make_async_copy(x_ref.at[t], xs_hbm.at[dest_smem[t * TOP_K + k]], sem_rows).start()
        return carry

    lax.fori_loop(0, x_ref.shape[0], issue, 0)

    def drain(t, carry):
        for k in range(TOP_K):
            pltpu.make_async_copy(x_ref.at[0], xs_hbm.at[0], sem_rows).wait()
        return carry

    lax.fori_loop(0, x_ref.shape[0], drain, 0)


def _dispatch(dest_flat, u2_rows, n_rows):
    n = u2_rows.shape[0]
    td = DISP_TILE
    return pl.pallas_call(
        _dispatch_kernel,
        grid=(n // td,),
        in_specs=[pl.BlockSpec(memory_space=pl.ANY),
                  pl.BlockSpec((td, ROW_CHUNKS, LANES), lambda i: (i, 0, 0))],
        out_specs=pl.BlockSpec(memory_space=pl.ANY),
        out_shape=jax.ShapeDtypeStruct((n_rows, ROW_CHUNKS, LANES), F32),
        scratch_shapes=[pltpu.SMEM((td * TOP_K,), I32), pltpu.SemaphoreType.DMA,
                        pltpu.SemaphoreType.DMA],
        compiler_params=_cparams(1), name="dispatch",
    )(dest_flat, u2_rows)


def _moe_kernel(bexp_ref, bval_ref, nused_ref, xs_ref, w1_ref, w3_ref, w2_ref, ys_ref,
                w13_s, w2_s):
    b = pl.program_id(0)

    @pl.when(b < nused_ref[0])
    def _():
        e = bexp_ref[b]
        prev = bexp_ref[jnp.maximum(b - 1, 0)]

        @pl.when(jnp.logical_or(b == 0, e != prev))
        def _():
            w13_s[:, 0:EXPERT_FF] = w1_ref[...].astype(BF16)
            w13_s[:, EXPERT_FF:2 * EXPERT_FF] = w3_ref[...].astype(BF16)
            w2_s[...] = w2_ref[...].astype(BF16)

        x = jnp.concatenate([xs_ref[:, cidx, :] for cidx in range(ROW_CHUNKS)], axis=1)
        rows = lax.broadcasted_iota(I32, x.shape, 0)
        x = jnp.where(rows < bval_ref[b], x, 0.0).astype(BF16)
        ab = jnp.dot(x, w13_s[...], preferred_element_type=F32)
        hid = (_silu(ab[:, 0:EXPERT_FF]) * ab[:, EXPERT_FF:2 * EXPERT_FF]).astype(BF16)
        y = jnp.dot(hid, w2_s[...], preferred_element_type=F32)
        for cidx in range(ROW_CHUNKS):
            ys_ref[:, cidx, :] = y[:, cidx * LANES:(cidx + 1) * LANES]


def _moe_experts(bexp, bval, nused, xs, w1, w3, w2, n_blocks):
    br = MOE_ROWS

    def row_idx(b, bexp, bval, nused):
        return (jnp.minimum(b, nused[0] - 1), 0, 0)

    def w_idx(b, bexp, bval, nused):
        return (bexp[jnp.minimum(b, nused[0] - 1)], 0, 0)

    grid_spec = pltpu.PrefetchScalarGridSpec(
        num_scalar_prefetch=3, grid=(n_blocks,),
        in_specs=[pl.BlockSpec((br, ROW_CHUNKS, LANES), row_idx),
                  pl.BlockSpec((None, D_MODEL, EXPERT_FF), w_idx),
                  pl.BlockSpec((None, D_MODEL, EXPERT_FF), w_idx),
                  pl.BlockSpec((None, EXPERT_FF, D_MODEL), w_idx)],
        out_specs=pl.BlockSpec((br, ROW_CHUNKS, LANES), row_idx),
        scratch_shapes=[pltpu.VMEM((D_MODEL, 2 * EXPERT_FF), BF16),
                        pltpu.VMEM((EXPERT_FF, D_MODEL), BF16)])
    return pl.pallas_call(
        _moe_kernel, grid_spec=grid_spec,
        out_shape=jax.ShapeDtypeStruct(xs.shape, F32),
        compiler_params=_cparams(1), name="moe_experts",
    )(bexp, bval, nused, xs, w1, w3, w2)


def _combine_kernel(dest_hbm, w_hbm, ys_hbm, base_ref, mod_ref, g_ref, b_ref, y_ref,
                    dest_smem, w_smem, rows_buf, acc_buf, sem_idx, sem_w, sem_rows):
    i = pl.program_id(0)
    n_idx = dest_smem.shape[0]
    cp_d = pltpu.make_async_copy(dest_hbm.at[pl.ds(i * n_idx, n_idx)], dest_smem, sem_idx)
    cp_w = pltpu.make_async_copy(w_hbm.at[pl.ds(i * n_idx, n_idx)], w_smem, sem_w)
    cp_d.start()
    cp_w.start()
    cp_d.wait()
    cp_w.wait()
    n_tok = acc_buf.shape[0]

    def issue(t, carry):
        for k in range(TOP_K):
            j = t * TOP_K + k
            pltpu.make_async_copy(ys_hbm.at[dest_smem[j]], rows_buf.at[j], sem_rows).start()
        return carry

    lax.fori_loop(0, n_tok, issue, 0)

    def drain(t, carry):
        for k in range(TOP_K):
            pltpu.make_async_copy(ys_hbm.at[0], rows_buf.at[0], sem_rows).wait()
        return carry

    lax.fori_loop(0, n_tok, drain, 0)

    def reduce(t, carry):
        acc = w_smem[t * TOP_K] * rows_buf[t * TOP_K]
        for k in range(1, TOP_K):
            acc = acc + w_smem[t * TOP_K + k] * rows_buf[t * TOP_K + k]
        acc_buf[t] = acc
        return carry

    lax.fori_loop(0, n_tok, reduce, 0)
    moe = jnp.concatenate([acc_buf[:, cidx, :] for cidx in range(ROW_CHUNKS)], axis=1)
    gate2 = mod_ref[:, 5 * D_MODEL:6 * D_MODEL]
    y_ref[...] = _layer_norm(base_ref[...] + gate2 * moe, g_ref[...], b_ref[...])


def _combine(dest_flat, w_flat, ys, base, mod3, l2g, l2b, n_ctx, seq_tokens):
    n = base.shape[0]
    tc = COMB_TILE
    n_ctx_tiles = n_ctx // tc
    n_seq_tiles = seq_tokens // tc

    def mod_idx(i):
        return (jnp.where(i < n_ctx_tiles, 0, 1 + (i - n_ctx_tiles) // n_seq_tiles), 0, 0)

    full = lambda a: pl.BlockSpec(a.shape, lambda i: (0,) * a.ndim)
    return pl.pallas_call(
        _combine_kernel,
        grid=(n // tc,),
        in_specs=[pl.BlockSpec(memory_space=pl.ANY), pl.BlockSpec(memory_space=pl.ANY),
                  pl.BlockSpec(memory_space=pl.ANY),
                  pl.BlockSpec((tc, D_MODEL), lambda i: (i, 0)),
                  pl.BlockSpec((None, 1, mod3.shape[2]), mod_idx), full(l2g), full(l2b)],
        out_specs=pl.BlockSpec((tc, D_MODEL), lambda i: (i, 0)),
        out_shape=jax.ShapeDtypeStruct((n, D_MODEL), F32),
        scratch_shapes=[pltpu.SMEM((tc * TOP_K,), I32), pltpu.SMEM((tc * TOP_K,), F32),
                        pltpu.VMEM((tc * TOP_K, ROW_CHUNKS, LANES), F32),
                        pltpu.VMEM((tc, ROW_CHUNKS, LANES), F32),
                        pltpu.SemaphoreType.DMA, pltpu.SemaphoreType.DMA, pltpu.SemaphoreType.DMA],
        compiler_params=_cparams(1), name="combine",
    )(dest_flat, w_flat, ys, base, mod3, l2g, l2b)


def _rope_tables(n_tok, tile):
    rows = n_tok // GRID_W
    row_idx = jnp.repeat(jnp.arange(rows, dtype=F32), GRID_W)
    col_idx = jnp.tile(jnp.arange(GRID_W, dtype=F32), rows)
    inv_freq = 1.0 / (ROPE_THETA ** (jnp.arange(0, ROPE_AXIS_DIM, 2, dtype=F32) / ROPE_AXIS_DIM))
    ang_r = row_idx[:, None] * inv_freq[None, :]
    ang_c = col_idx[:, None] * inv_freq[None, :]
    ang = jnp.concatenate([ang_r, ang_r, ang_c, ang_c], axis=-1)
    cos, sin = jnp.cos(ang), jnp.sin(ang)
    quarter = (jnp.arange(HEAD_DIM) // (ROPE_AXIS_DIM // 2)) % 2
    sin_a = jnp.where(quarter == 0, -sin, 0.0)
    sin_b = jnp.where(quarter == 1, sin, 0.0)
    rep = LANES // HEAD_DIM
    ident = lambda v: jnp.full((tile, LANES), v, F32)
    cos_t = jnp.concatenate([jnp.tile(cos, (1, rep)), ident(1.0)], axis=0)
    sa_t = jnp.concatenate([jnp.tile(sin_a, (1, rep)), ident(0.0)], axis=0)
    sb_t = jnp.concatenate([jnp.tile(sin_b, (1, rep)), ident(0.0)], axis=0)
    return cos_t, sa_t, sb_t


def _dup_heads(a):
    parts = []
    for h in range(KV_HEADS):
        blk = a[..., h * HEAD_DIM:(h + 1) * HEAD_DIM]
        parts += [blk] * (LANES // HEAD_DIM)
    return jnp.concatenate(parts, axis=-1)


def kernel(x_prompt, x_sample, cache_k, cache_v, state_gla_fwd, state_gla_bwd, c, c_ctx, w_ada, b_ada, w_in, q_norm, k_norm, gla_wa_fwd, gla_ba_fwd, gla_wa_bwd, gla_ba_bwd, gla_norm, w_out, ln1_g, ln1_b, ln2_g, ln2_b, w_router, router_bias, exp_w1, exp_w3, exp_w2, sh_w1, sh_w3, sh_w2):
    n_ctx_b, ctx_seq, _ = x_prompt.shape
    n_lat_b, lat_seq, _ = x_sample.shape
    n_ctx = n_ctx_b * ctx_seq
    n_lat = n_lat_b * lat_seq
    n = n_ctx + n_lat
    l = 0

    x_all = jnp.concatenate([x_prompt.reshape(n_ctx, D_MODEL), x_sample.reshape(n_lat, D_MODEL)], axis=0)

    c_rows = jnp.zeros((SUBLANES, D_MODEL), F32).at[0].set(c_ctx).at[1:1 + n_lat_b].set(c)
    mod = _modulation(c_rows, w_ada[l], b_ada[l][None, :])
    mod3 = mod.reshape(SUBLANES, 1, 6 * D_MODEL)

    wi = w_in[l]
    o_q, o_k, o_v, o_gq, o_gk, o_gv, o_gg, o_rf, o_rb, o_end = np.cumsum(
        [0, ATT_WIDTH, KV_HEADS * HEAD_DIM, KV_HEADS * HEAD_DIM, GLA_KW, GLA_KW, GLA_WIDTH, GLA_WIDTH,
         GLA_GATE_RANK, GLA_GATE_RANK])
    w_tok = jnp.concatenate([
        wi[:, o_q:o_k], _dup_heads(wi[:, o_k:o_v]), _dup_heads(wi[:, o_v:o_gq]), wi[:, o_gq:o_gk],
        wi[:, o_gv:o_gg], wi[:, o_gg:o_rf], wi[:, o_rf:o_end],
        jnp.zeros((D_MODEL, LANES - 2 * GLA_GATE_RANK), F32)], axis=1).astype(BF16)
    w_tr = jnp.concatenate([wi[:, o_gk:o_gv], wi[:, o_rf:o_end]], axis=1).T.astype(BF16)
    rep = LANES // HEAD_DIM
    qn = jnp.tile(q_norm[l], rep)[None, :]
    kn = jnp.tile(k_norm[l], rep)[None, :]
    seg = jnp.asarray(np.kron(np.eye(rep), np.ones((HEAD_DIM, HEAD_DIM))), BF16)
    wa = jnp.zeros((LANES, 2 * GLA_KW), F32)
    wa = wa.at[0:GLA_GATE_RANK, 0:GLA_KW].set(gla_wa_fwd[l])
    wa = wa.at[GLA_GATE_RANK:2 * GLA_GATE_RANK, GLA_KW:].set(gla_wa_bwd[l])
    ba = jnp.concatenate([gla_ba_fwd[l], gla_ba_bwd[l]])[None, :]
    wat = wa[0:2 * GLA_GATE_RANK, :].T
    bat = ba.T
    cos_t, sa_t, sb_t = _rope_tables(lat_seq, TOK_TILE)

    (q, k_dup, v_dup, k32, v32, gq, gv, gg, la, gkt, lat) = _in_projection(
        x_all, mod3, w_tok, w_tr, qn, kn, cos_t, sa_t, sb_t, seg, wa, ba, wat, bat,
        n_ctx, lat_seq // TOK_TILE)

    ck = _dup_heads(cache_k[:, l].reshape(n_lat_b, -1, KV_HEADS * HEAD_DIM)).astype(BF16)
    cv = _dup_heads(cache_v[:, l].reshape(n_lat_b, -1, KV_HEADS * HEAD_DIM)).astype(BF16)
    att_ctx = _attention(q, k_dup, v_dup, None, 0, n_ctx_b, ctx_seq)
    att_lat = _attention(q, k_dup, v_dup, (ck, cv), n_ctx, n_lat_b, lat_seq)
    att = jnp.concatenate([att_ctx, att_lat], axis=0)

    gconst, levels_of = _gla_constants()
    to_dev = lambda t: (jnp.asarray(t[0], BF16), jnp.asarray(t[1], BF16), jnp.asarray(t[2], F32))
    bd = jnp.asarray(np.kron(np.eye(GLA_HEADS), np.ones((GLA_DK, GLA_DV))), BF16)
    vbd = jnp.asarray(np.kron(np.eye(GLA_HEADS), np.ones((GLA_CHUNK, GLA_DV))), BF16)
    consts = ((to_dev(gconst["f"]), to_dev(gconst["b"])), levels_of, bd, vbd)
    s_zero = jnp.zeros((n_ctx_b, GLA_HEADS, GLA_DK, GLA_DV), F32)
    of_c, ob_c, sf_new, sb_new = _gla(gq, la, gkt, lat, gv, s_zero, s_zero, consts, 0, n_ctx_b, ctx_seq)
    of_l, ob_l, _, _ = _gla(gq, la, gkt, lat, gv, state_gla_fwd[:, l], state_gla_bwd[:, l], consts,
                            n_ctx, n_lat_b, lat_seq)
    o_f = jnp.concatenate([of_c, of_l], axis=0)
    o_b = jnp.concatenate([ob_c, ob_l], axis=0)

    sw13 = jnp.concatenate([sh_w1[l], sh_w3[l]], axis=1).astype(BF16)
    base, u2_rows, logits_t = _out_projection(
        att, o_f, o_b, gg, x_all, mod3, w_out[l].astype(BF16), gla_norm[l][None, :],
        ln1_g[l][None, :], ln1_b[l][None, :], w_router[l].T.astype(BF16), sw13,
        sh_w2[l].astype(BF16), n_ctx, lat_seq // TOK_TILE)

    upper = jnp.asarray(np.triu(np.ones((TOK_TILE, TOK_TILE)), 1), BF16)
    idx_t, w_t, pos_t, counts = _route(logits_t, router_bias[l][:, None], upper)
    n_blocks = n * TOP_K // MOE_ROWS + N_EXPERTS
    n_blocks_pad = -(-n_blocks // LANES) * LANES
    lower = jnp.asarray(np.tril(np.ones((N_EXPERTS, N_EXPERTS)), -1), F32)
    dest_t, bexp, bval, nused = _destinations(counts, lower, idx_t, pos_t, n_blocks_pad)
    dest_flat = dest_t.T.reshape(-1)
    w_flat = w_t.T.reshape(-1)

    xs = _dispatch(dest_flat, u2_rows, n_blocks * MOE_ROWS)
    ys = _moe_experts(bexp.reshape(-1), bval.reshape(-1), nused.reshape(-1)[0:1], xs,
                      exp_w1[l], exp_w3[l], exp_w2[l], n_blocks)
    y_all = _combine(dest_flat, w_flat, ys, base, mod3, ln2_g[l][None, :], ln2_b[l][None, :],
                     n_ctx, lat_seq)

    y_prompt = y_all[:n_ctx].reshape(n_ctx_b, ctx_seq, D_MODEL)
    y_sample = y_all[n_ctx:].reshape(n_lat_b, lat_seq, D_MODEL)
    new_cache_k = k32.reshape(n_ctx_b, 1, ctx_seq, KV_HEADS, HEAD_DIM)
    new_cache_v = v32.reshape(n_ctx_b, 1, ctx_seq, KV_HEADS, HEAD_DIM)
    return (y_prompt, y_sample, new_cache_k, new_cache_v, sf_new[:, None], sb_new[:, None])
```

```python
import functools

import numpy as np
import jax
import jax.numpy as jnp
from jax import lax
from jax.experimental import pallas as pl
from jax.experimental.pallas import tpu as pltpu

F32 = jnp.float32
BF16 = jnp.bfloat16
I32 = jnp.int32

D_MODEL = 1024
GRID_W = 64
HEAD_DIM = 64
N_HEADS = 8
KV_HEADS = 2
ATT_WIDTH = N_HEADS * HEAD_DIM
ATT_SCALE = HEAD_DIM ** -0.5
ROPE_AXIS_DIM = HEAD_DIM // 2
ROPE_THETA = 10000.0
GLA_HEADS = 4
GLA_DK = 64
GLA_DV = 128
GLA_WIDTH = GLA_HEADS * GLA_DV
GLA_KW = GLA_HEADS * GLA_DK
GLA_GATE_RANK = 16
GLA_TAU = 16.0
N_EXPERTS = 256
TOP_K = 8
EXPERT_FF = 256
SHARED_FF = 256
ROUTED_SCALE = 2.5
DEPTH = 1
ALPHA = (2.0 * DEPTH) ** 0.25
EPS = 1e-6

LANES = 128
SUBLANES = 8
ROW_CHUNKS = D_MODEL // LANES
VMEM_LIMIT = 56 * 1024 * 1024

TOK_TILE = 512
ATT_TQ = 256
GLA_CHUNK = 128
GLA_LEVELS = ((32, 128), (8, 32), (2, 8), (1, 2))
MOE_ROWS = 256
DISP_TILE = 256
COMB_TILE = 128
HIGHEST = lax.Precision.HIGHEST


def _cparams(n_axes):
    return pltpu.CompilerParams(dimension_semantics=("arbitrary",) * n_axes,
                                vmem_limit_bytes=VMEM_LIMIT)


def _silu(x):
    return x * jax.nn.sigmoid(x)


def _log_sigmoid(x):
    return jnp.minimum(x, 0.0) - jnp.log(1.0 + jnp.exp(-jnp.abs(x)))


def _load_row_tiles(ref, n_rows, row0=0):
    return jnp.concatenate(
        [ref[pl.ds(row0 * ROW_CHUNKS + cidx, n_rows, stride=ROW_CHUNKS), :] for cidx in range(ROW_CHUNKS)],
        axis=1)


def _store_row_tiles(ref, x):
    for cidx in range(ROW_CHUNKS):
        ref[pl.ds(cidx, x.shape[0], stride=ROW_CHUNKS), :] = x[:, cidx * LANES:(cidx + 1) * LANES]


def _layer_norm(z, g, b):
    mu = jnp.mean(z, axis=-1, keepdims=True)
    zc = z - mu
    var = jnp.mean(zc * zc, axis=-1, keepdims=True)
    return zc * lax.rsqrt(var + EPS) * g + b


def _mod_kernel(c_ref, w_ref, b_ref, o_ref):
    s = _silu(c_ref[...]).astype(BF16)
    o_ref[...] = jnp.dot(s, w_ref[...].astype(BF16), preferred_element_type=F32) + b_ref[...]


def _modulation(c_rows, w_ada, b_ada):
    n_cols = w_ada.shape[1]
    tn = 512
    return pl.pallas_call(
        _mod_kernel,
        grid=(n_cols // tn,),
        in_specs=[pl.BlockSpec((SUBLANES, D_MODEL), lambda j: (0, 0)),
                  pl.BlockSpec((D_MODEL, tn), lambda j: (0, j)),
                  pl.BlockSpec((1, tn), lambda j: (0, j))],
        out_specs=pl.BlockSpec((SUBLANES, tn), lambda j: (0, j)),
        out_shape=jax.ShapeDtypeStruct((SUBLANES, n_cols), F32),
        compiler_params=_cparams(1),
        name="modulation",
    )(c_rows, w_ada, b_ada)


_C_Q = 0
_C_K = _C_Q + ATT_WIDTH
_C_V = _C_K + 2 * LANES
_C_GQ = _C_V + 2 * LANES
_C_GV = _C_GQ + GLA_KW
_C_GG = _C_GV + GLA_WIDTH
_C_RA = _C_GG + GLA_WIDTH
_C_END = _C_RA + LANES


def _inproj_kernel(x_ref, mod_ref, w_ref, wt_ref, qn_ref, kn_ref, cos_ref, sa_ref, sb_ref,
                   seg_ref, wa_ref, ba_ref, wat_ref, bat_ref,
                   q_ref, k_ref, v_ref, k32_ref, v32_ref, gq_ref, gv_ref, gg_ref,
                   la_ref, gkt_ref, lat_ref, *, n_ctx_tiles):
    i = pl.program_id(0)
    m = mod_ref[...]
    shift1 = m[:, 0:D_MODEL]
    scale1 = m[:, D_MODEL:2 * D_MODEL]
    u = (x_ref[...] * (1.0 + scale1) + shift1).astype(BF16)

    cos = cos_ref[...]
    sin_a = sa_ref[...]
    sin_b = sb_ref[...]
    seg = seg_ref[...]
    lane = lax.broadcasted_iota(I32, (u.shape[0], LANES), 1)
    low = lane < HEAD_DIM

    def proj(c0, c1):
        return jnp.dot(u, w_ref[:, c0:c1], preferred_element_type=F32)

    def head_norm(blk, gain):
        ss = jnp.dot((blk * blk).astype(BF16), seg, preferred_element_type=F32) * (1.0 / HEAD_DIM)
        return blk * lax.rsqrt(ss + EPS) * gain

    def rope(blk):
        return (blk * cos + pltpu.roll(blk, LANES - ROPE_AXIS_DIM // 2, 1) * sin_a
                + pltpu.roll(blk, ROPE_AXIS_DIM // 2, 1) * sin_b)

    pq = proj(_C_Q, _C_K)
    for j in range(ATT_WIDTH // LANES):
        qn = head_norm(pq[:, j * LANES:(j + 1) * LANES], qn_ref[...])
        q_ref[:, j * LANES:(j + 1) * LANES] = (rope(qn) * ATT_SCALE).astype(BF16)

    pk = proj(_C_K, _C_V)
    kn = [head_norm(pk[:, j * LANES:(j + 1) * LANES], kn_ref[...]) for j in range(KV_HEADS)]
    for j in range(KV_HEADS):
        k_ref[:, j * LANES:(j + 1) * LANES] = rope(kn[j]).astype(BF16)
    pv = proj(_C_V, _C_GQ)
    v_ref[...] = pv.astype(BF16)

    @pl.when(i < n_ctx_tiles)
    def _():
        k32_ref[...] = jnp.where(low, kn[0], kn[1])
        v32_ref[...] = jnp.where(low, pv[:, 0:LANES], pv[:, LANES:2 * LANES])

    gq_ref[...] = proj(_C_GQ, _C_GV) * (GLA_DK ** -0.5)
    gv_ref[...] = proj(_C_GV, _C_GG).astype(BF16)
    gg_ref[...] = proj(_C_GG, _C_RA).astype(BF16)

    ra = proj(_C_RA, _C_END)
    pre = jnp.dot(ra, wa_ref[...], precision=HIGHEST, preferred_element_type=F32) + ba_ref[...]
    la_ref[...] = _log_sigmoid(pre) * (1.0 / GLA_TAU)

    pt = lax.dot_general(wt_ref[...], u, (((1,), (1,)), ((), ())), preferred_element_type=F32)
    gkt_ref[...] = pt[0:GLA_KW, :]
    rat = pt[GLA_KW:GLA_KW + 2 * GLA_GATE_RANK, :]
    pre_t = jnp.dot(wat_ref[...], rat, precision=HIGHEST, preferred_element_type=F32) + bat_ref[...]
    lat_ref[...] = _log_sigmoid(pre_t) * (1.0 / GLA_TAU)


def _in_projection(x_all, mod3, w_tok, w_tr, qn, kn, cos_t, sa_t, sb_t, seg, wa, ba, wat, bat,
                   n_ctx, n_seq_tiles):
    n = x_all.shape[0]
    tb = TOK_TILE
    n_ctx_tiles = n_ctx // tb
    n_tiles = n // tb
    n_rope_blocks = cos_t.shape[0] // tb - 1

    def mod_idx(i):
        return (jnp.where(i < n_ctx_tiles, 0, 1 + (i - n_ctx_tiles) // n_seq_tiles), 0, 0)

    def rope_idx(i):
        return (jnp.where(i < n_ctx_tiles, n_rope_blocks, (i - n_ctx_tiles) % n_seq_tiles), 0)

    def ctx_idx(i):
        return (jnp.minimum(i, n_ctx_tiles - 1), 0)

    tok = lambda w: pl.BlockSpec((tb, w), lambda i: (i, 0))
    full = lambda a: pl.BlockSpec(a.shape, lambda i: (0,) * a.ndim)
    tr = lambda r: pl.BlockSpec((r, tb), lambda i: (0, i))
    out_shapes = (
        jax.ShapeDtypeStruct((n, ATT_WIDTH), BF16),
        jax.ShapeDtypeStruct((n, 2 * LANES), BF16),
        jax.ShapeDtypeStruct((n, 2 * LANES), BF16),
        jax.ShapeDtypeStruct((n_ctx, LANES), F32),
        jax.ShapeDtypeStruct((n_ctx, LANES), F32),
        jax.ShapeDtypeStruct((n, GLA_KW), F32),
        jax.ShapeDtypeStruct((n, GLA_WIDTH), BF16),
        jax.ShapeDtypeStruct((n, GLA_WIDTH), BF16),
        jax.ShapeDtypeStruct((n, 2 * GLA_KW), F32),
        jax.ShapeDtypeStruct((GLA_KW, n), F32),
        jax.ShapeDtypeStruct((2 * GLA_KW, n), F32),
    )
    out_specs = (tok(ATT_WIDTH), tok(2 * LANES), tok(2 * LANES),
                 pl.BlockSpec((tb, LANES), ctx_idx), pl.BlockSpec((tb, LANES), ctx_idx),
                 tok(GLA_KW), tok(GLA_WIDTH), tok(GLA_WIDTH), tok(2 * GLA_KW),
                 tr(GLA_KW), tr(2 * GLA_KW))
    in_specs = [tok(D_MODEL),
                pl.BlockSpec((None, 1, mod3.shape[2]), mod_idx),
                full(w_tok), full(w_tr), full(qn), full(kn),
                pl.BlockSpec((tb, LANES), rope_idx), pl.BlockSpec((tb, LANES), rope_idx),
                pl.BlockSpec((tb, LANES), rope_idx),
                full(seg), full(wa), full(ba), full(wat), full(bat)]
    return pl.pallas_call(
        functools.partial(_inproj_kernel, n_ctx_tiles=n_ctx_tiles),
        grid=(n_tiles,), in_specs=in_specs, out_specs=out_specs, out_shape=out_shapes,
        compiler_params=_cparams(1), name="in_projection",
    )(x_all, mod3, w_tok, w_tr, qn, kn, cos_t, sa_t, sb_t, seg, wa, ba, wat, bat)


def _attention_kernel(*refs, n_kv_parts):
    q_ref = refs[0]
    k_refs = refs[1:1 + n_kv_parts]
    v_refs = refs[1 + n_kv_parts:1 + 2 * n_kv_parts]
    o_ref = refs[1 + 2 * n_kv_parts]
    tq = q_ref.shape[0]
    lane = lax.broadcasted_iota(I32, (tq, LANES), 1)
    low = lane < HEAD_DIM
    for p in range(ATT_WIDTH // LANES):
        kv = p // (ATT_WIDTH // LANES // KV_HEADS)
        qp = q_ref[:, p * LANES:(p + 1) * LANES]
        outs = []
        for half in range(2):
            qm = jnp.where(low if half == 0 else jnp.logical_not(low), qp, jnp.zeros_like(qp))
            s = [lax.dot_general(qm, k[:, kv * LANES:(kv + 1) * LANES], (((1,), (1,)), ((), ())),
                                 preferred_element_type=F32) for k in k_refs]
            mx = functools.reduce(jnp.maximum, [jnp.max(x, axis=-1, keepdims=True) for x in s])
            pr = [jnp.exp(x - mx) for x in s]
            den = functools.reduce(jnp.add, [jnp.sum(x, axis=-1, keepdims=True) for x in pr])
            acc = functools.reduce(jnp.add, [
                jnp.dot(x.astype(BF16), v[:, kv * LANES:(kv + 1) * LANES], preferred_element_type=F32)
                for x, v in zip(pr, v_refs)])
            outs.append(acc / den)
        o_ref[:, p * LANES:(p + 1) * LANES] = jnp.where(low, outs[0], outs[1]).astype(BF16)


def _attention(q, k, v, extra_kv, row0, n_batch, seq):
    tq = min(ATT_TQ, seq)
    n_q = seq // tq
    q_blk0 = row0 // tq
    kv_blk0 = row0 // seq
    in_specs = [pl.BlockSpec((tq, ATT_WIDTH), lambda b, i: (q_blk0 + b * n_q + i, 0))]
    kv_spec = pl.BlockSpec((seq, 2 * LANES), lambda b, i: (kv_blk0 + b, 0))
    args_k, args_v, specs_k, specs_v = [k], [v], [kv_spec], [kv_spec]
    if extra_kv is not None:
        ck, cv = extra_kv
        c_spec = pl.BlockSpec((None, ck.shape[1], 2 * LANES), lambda b, i: (b, 0, 0))
        args_k.append(ck); args_v.append(cv); specs_k.append(c_spec); specs_v.append(c_spec)
    return pl.pallas_call(
        functools.partial(_attention_kernel, n_kv_parts=len(args_k)),
        grid=(n_batch, n_q),
        in_specs=in_specs + specs_k + specs_v,
        out_specs=pl.BlockSpec((tq, ATT_WIDTH), lambda b, i: (b * n_q + i, 0)),
        out_shape=jax.ShapeDtypeStruct((n_batch * seq, ATT_WIDTH), BF16),
        compiler_params=_cparams(2), name="attention",
    )(q, *args_k, *args_v)


def _gla_constants():
    c = GLA_CHUNK
    idx = np.arange(c)
    q_mats, k_mats, masks, levels_of = [], [], [], []
    for li, (s, p) in enumerate(GLA_LEVELS):
        start = (idx // s) * s
        end = start + s - 1
        k_mats.append(((idx[None, :] > idx[:, None]) & (idx[None, :] <= end[:, None])))
        for d in range(p // s - 1):
            lo = np.maximum(start - d * s, 0)
            q_mats.append((idx[None, :] >= lo[:, None]) & (idx[None, :] <= idx[:, None]))
            masks.append((idx[:, None] // p == idx[None, :] // p)
                         & (idx[:, None] // s - idx[None, :] // s - 1 == d))
            levels_of.append(li)
    masks.append(np.eye(c, dtype=bool))
    levels_of.append(len(GLA_LEVELS) - 1)
    q_mats.append(idx[None, :] <= idx[:, None])
    k_mats = k_mats[:-1]
    k_mats.append(idx[None, :] > idx[:, None])
    k_mats.append(np.ones((c, c), bool))
    out = {}
    for name, flip in (("f", False), ("b", True)):
        f = (lambda a: a[::-1, ::-1]) if flip else (lambda a: a)
        lq = np.concatenate([f(a) for a in q_mats], axis=0).astype(np.float32)
        lkt = np.concatenate([f(a).T for a in k_mats], axis=1).astype(np.float32)
        mk = np.stack([np.tile(f(a), (1, GLA_HEADS)) for a in masks]).astype(np.float32)
        out[name] = (np.concatenate([lq, lq], axis=1), np.concatenate([lkt, lkt], axis=0), mk)
    return out, tuple(levels_of)


def _gla_direction(q, g, gkt, gt, v, lq2, lkt2, masks_ref, bd, vbd, s_ref, levels_of):
    c = GLA_CHUNK
    n_var = len(levels_of)
    n_lev = len(GLA_LEVELS)
    g_hi = g.astype(BF16)
    g_lo = (g - g_hi.astype(F32)).astype(BF16)
    fq = jnp.dot(lq2, jnp.concatenate([g_hi, g_lo], axis=0), preferred_element_type=F32)
    gt_hi = gt.astype(BF16)
    gt_lo = (gt - gt_hi.astype(F32)).astype(BF16)
    fk = jnp.dot(jnp.concatenate([gt_hi, gt_lo], axis=1), lkt2, preferred_element_type=F32)

    def key_factor(f):
        return gkt * jnp.exp(fk[:, f * c:(f + 1) * c])

    q_var = [(q * jnp.exp(fq[vi * c:(vi + 1) * c, :])).astype(BF16) for vi in range(n_var - 1)]
    q_var.append(q.astype(BF16))
    a = jnp.zeros((c, GLA_HEADS * c), F32)
    for li in range(n_lev):
        kt = (key_factor(li) if li < n_lev - 1 else gkt).astype(BF16)
        xt = jnp.concatenate([kt] * GLA_HEADS, axis=1) * bd
        vis = [vi for vi in range(n_var) if levels_of[vi] == li]
        res = jnp.dot(jnp.concatenate([q_var[vi] for vi in vis], axis=0), xt,
                      preferred_element_type=F32)
        for r, vi in enumerate(vis):
            a = a + masks_ref[vi] * res[r * c:(r + 1) * c, :]
    q_in = (q * jnp.exp(fq[(n_var - 1) * c:n_var * c, :])).astype(BF16)
    state = s_ref[...]
    v_bd = jnp.concatenate([v] * GLA_HEADS, axis=0) * vbd
    o = (jnp.dot(q_in, state.astype(BF16), preferred_element_type=F32)
         + jnp.dot(a.astype(BF16), v_bd, preferred_element_type=F32))
    k_out = key_factor(n_lev - 1).astype(BF16)
    e_tot = jnp.exp(fk[:, n_lev * c:(n_lev + 1) * c])
    upd = jnp.dot(k_out, v, preferred_element_type=F32)
    s_ref[...] = (state * jnp.concatenate([e_tot] * (GLA_WIDTH // c), axis=1)
                  + upd * bd.astype(F32))
    return o


def _gla_kernel(gq_f, la_f, gkt_f, lat_f, gv_f, gq_b, la_b, gkt_b, lat_b, gv_b,
                s0f_ref, s0b_ref, lq2f, lkt2f, mkf, lq2b, lkt2b, mkb, bd_ref, vbd_ref,
                of_ref, ob_ref, sf_ref, sb_ref, st_f, st_b, *, levels_of):
    n = pl.program_id(1)

    @pl.when(n == 0)
    def _():
        st_f[...] = jnp.zeros_like(st_f)
        st_b[...] = jnp.zeros_like(st_b)
        for h in range(GLA_HEADS):
            rows = slice(h * GLA_DK, (h + 1) * GLA_DK)
            cols = slice(h * GLA_DV, (h + 1) * GLA_DV)
            st_f[rows, cols] = s0f_ref[h]
            st_b[rows, cols] = s0b_ref[h]

    bd = bd_ref[...]
    vbd = vbd_ref[...]
    of_ref[...] = _gla_direction(gq_f[...], la_f[...], gkt_f[...], lat_f[...], gv_f[...],
                                 lq2f[...], lkt2f[...], mkf, bd, vbd, st_f, levels_of)
    ob_ref[...] = _gla_direction(gq_b[...], la_b[...], gkt_b[...], lat_b[...], gv_b[...],
                                 lq2b[...], lkt2b[...], mkb, bd, vbd, st_b, levels_of)

    @pl.when(n == pl.num_programs(1) - 1)
    def _():
        for h in range(GLA_HEADS):
            rows = slice(h * GLA_DK, (h + 1) * GLA_DK)
            cols = slice(h * GLA_DV, (h + 1) * GLA_DV)
            sf_ref[h] = st_f[rows, cols]
            sb_ref[h] = st_b[rows, cols]


def _gla(gq, la, gkt, lat, gv, s0f, s0b, consts, row0, n_batch, seq):
    (cf, cb), levels_of, bd, vbd = consts
    c = GLA_CHUNK
    nc = seq // c
    blk0 = row0 // c
    n_la_blocks_b = 1
    fwd = lambda b, n: blk0 + b * nc + n
    bwd = lambda b, n: blk0 + b * nc + (nc - 1 - n)

    def tok(w, which, col=0):
        return pl.BlockSpec((c, w), lambda b, n: (which(b, n), col))

    def tr(r, which, row=0):
        return pl.BlockSpec((r, c), lambda b, n: (row, which(b, n)))

    full = lambda a: pl.BlockSpec(a.shape, lambda b, n: (0,) * a.ndim)
    st_spec = pl.BlockSpec((None, GLA_HEADS, GLA_DK, GLA_DV), lambda b, n: (b, 0, 0, 0))
    in_specs = [tok(GLA_KW, fwd), tok(GLA_KW, fwd, 0), tr(GLA_KW, fwd), tr(GLA_KW, fwd, 0),
                tok(GLA_WIDTH, fwd),
                tok(GLA_KW, bwd), tok(GLA_KW, bwd, n_la_blocks_b), tr(GLA_KW, bwd),
                tr(GLA_KW, bwd, 1), tok(GLA_WIDTH, bwd),
                st_spec, st_spec,
                full(cf[0]), full(cf[1]), full(cf[2]), full(cb[0]), full(cb[1]), full(cb[2]),
                full(bd), full(vbd)]
    out_specs = (pl.BlockSpec((c, GLA_WIDTH), lambda b, n: (b * nc + n, 0)),
                 pl.BlockSpec((c, GLA_WIDTH), lambda b, n: (b * nc + (nc - 1 - n), 0)),
                 st_spec, st_spec)
    out_shape = (jax.ShapeDtypeStruct((n_batch * seq, GLA_WIDTH), F32),
                 jax.ShapeDtypeStruct((n_batch * seq, GLA_WIDTH), F32),
                 jax.ShapeDtypeStruct((n_batch, GLA_HEADS, GLA_DK, GLA_DV), F32),
                 jax.ShapeDtypeStruct((n_batch, GLA_HEADS, GLA_DK, GLA_DV), F32))
    return pl.pallas_call(
        functools.partial(_gla_kernel, levels_of=levels_of),
        grid=(n_batch, nc), in_specs=in_specs, out_specs=out_specs, out_shape=out_shape,
        scratch_shapes=[pltpu.VMEM((GLA_KW, GLA_WIDTH), F32), pltpu.VMEM((GLA_KW, GLA_WIDTH), F32)],
        compiler_params=_cparams(2), name="gla",
    )(gq, la, gkt, lat, gv, gq, la, gkt, lat, gv, s0f, s0b,
      cf[0], cf[1], cf[2], cb[0], cb[1], cb[2], bd, vbd)


def _outproj_kernel(att_ref, of_ref, ob_ref, gg_ref, x_ref, mod_ref, wo_ref, gn_ref, l1g_ref, l1b_ref,
                    wrt_ref, sw13_ref, sw2_ref, base_ref, u2_ref, lg_ref):
    m = mod_ref[...]
    gate1 = m[:, 2 * D_MODEL:3 * D_MODEL]
    shift2 = m[:, 3 * D_MODEL:4 * D_MODEL]
    scale2 = m[:, 4 * D_MODEL:5 * D_MODEL]
    gate2 = m[:, 5 * D_MODEL:6 * D_MODEL]
    og = of_ref[...] + ob_ref[...]
    gg = gg_ref[...].astype(F32)
    parts = [att_ref[...]]
    for h in range(GLA_HEADS):
        blk = og[:, h * GLA_DV:(h + 1) * GLA_DV]
        ms = jnp.mean(blk * blk, axis=-1, keepdims=True)
        nb = blk * lax.rsqrt(ms + EPS) * gn_ref[...]
        parts.append((nb * _silu(gg[:, h * GLA_DV:(h + 1) * GLA_DV])).astype(BF16))
    hcat = jnp.concatenate(parts, axis=1)
    hmix = jnp.dot(hcat, wo_ref[...], preferred_element_type=F32)
    x1 = _layer_norm(ALPHA * x_ref[...] + gate1 * hmix, l1g_ref[...], l1b_ref[...])
    u2 = x1 * (1.0 + scale2) + shift2
    u2b = u2.astype(BF16)
    lg_ref[...] = lax.dot_general(wrt_ref[...], u2b, (((1,), (1,)), ((), ())),
                                  preferred_element_type=F32)
    ab = jnp.dot(u2b, sw13_ref[...], preferred_element_type=F32)
    hid = (_silu(ab[:, 0:SHARED_FF]) * ab[:, SHARED_FF:2 * SHARED_FF]).astype(BF16)
    shared = jnp.dot(hid, sw2_ref[...], preferred_element_type=F32)
    base_ref[...] = ALPHA * x1 + gate2 * shared
    _store_row_tiles(u2_ref, u2)


def _out_projection(att, o_f, o_b, gg, x_all, mod3, wo, gn, l1g, l1b, wrt, sw13, sw2,
                    n_ctx, n_seq_tiles):
    n = x_all.shape[0]
    tb = TOK_TILE
    n_ctx_tiles = n_ctx // tb

    def mod_idx(i):
        return (jnp.where(i < n_ctx_tiles, 0, 1 + (i - n_ctx_tiles) // n_seq_tiles), 0, 0)

    tok = lambda w: pl.BlockSpec((tb, w), lambda i: (i, 0))
    full = lambda a: pl.BlockSpec(a.shape, lambda i: (0,) * a.ndim)
    return pl.pallas_call(
        _outproj_kernel,
        grid=(n // tb,),
        in_specs=[tok(ATT_WIDTH), tok(GLA_WIDTH), tok(GLA_WIDTH), tok(GLA_WIDTH), tok(D_MODEL),
                  pl.BlockSpec((None, 1, mod3.shape[2]), mod_idx),
                  full(wo), full(gn), full(l1g), full(l1b), full(wrt), full(sw13), full(sw2)],
        out_specs=(tok(D_MODEL),
                   pl.BlockSpec((tb * ROW_CHUNKS, LANES), lambda i: (i, 0)),
                   pl.BlockSpec((N_EXPERTS, tb), lambda i: (0, i))),
        out_shape=(jax.ShapeDtypeStruct((n, D_MODEL), F32),
                   jax.ShapeDtypeStruct((n * ROW_CHUNKS, LANES), F32),
                   jax.ShapeDtypeStruct((N_EXPERTS, n), F32)),
        compiler_params=_cparams(1), name="out_projection",
    )(att, o_f, o_b, gg, x_all, mod3, wo, gn, l1g, l1b, wrt, sw13, sw2)


def _route_kernel(lg_ref, bias_ref, upper_ref, idx_ref, w_ref, pos_ref, cnt_ref, run_ref):
    i = pl.program_id(0)

    @pl.when(i == 0)
    def _():
        run_ref[...] = jnp.zeros_like(run_ref)

    s = jax.nn.sigmoid(lg_ref[...])
    work = s + bias_ref[...]
    rows = lax.broadcasted_iota(I32, s.shape, 0).astype(F32)
    sel = jnp.zeros(s.shape, F32)
    idxs, vals = [], []
    for _ in range(TOP_K):
        mx = jnp.max(work, axis=0, keepdims=True)
        idx = jnp.min(jnp.where(work == mx, rows, float(N_EXPERTS)), axis=0, keepdims=True)
        hit = rows == idx
        vals.append(jnp.sum(jnp.where(hit, s, 0.0), axis=0, keepdims=True))
        idxs.append(idx)
        sel = jnp.where(hit, 1.0, sel)
        work = jnp.where(hit, -jnp.inf, work)
    den = functools.reduce(jnp.add, vals)
    rank = jnp.dot(sel.astype(BF16), upper_ref[...], preferred_element_type=F32) + run_ref[:, 0:1]
    for k in range(TOP_K):
        idx_ref[k:k + 1, :] = idxs[k].astype(I32)
        w_ref[k:k + 1, :] = vals[k] / den * ROUTED_SCALE
        pos_ref[k:k + 1, :] = jnp.sum(jnp.where(rows == idxs[k], rank, 0.0), axis=0,
                                      keepdims=True).astype(I32)
    run_ref[...] = run_ref[...] + jnp.sum(sel, axis=1, keepdims=True)
    cnt_ref[...] = run_ref[...]


def _route(logits_t, bias_col, upper):
    n = logits_t.shape[1]
    tt = TOK_TILE
    row = lambda dt: jax.ShapeDtypeStruct((TOP_K, n), dt)
    blk = pl.BlockSpec((TOP_K, tt), lambda i: (0, i))
    return pl.pallas_call(
        _route_kernel,
        grid=(n // tt,),
        in_specs=[pl.BlockSpec((N_EXPERTS, tt), lambda i: (0, i)),
                  pl.BlockSpec((N_EXPERTS, 1), lambda i: (0, 0)),
                  pl.BlockSpec((tt, tt), lambda i: (0, 0))],
        out_specs=(blk, blk, blk, pl.BlockSpec((N_EXPERTS, LANES), lambda i: (0, 0))),
        out_shape=(row(I32), row(F32), row(I32), jax.ShapeDtypeStruct((N_EXPERTS, LANES), F32)),
        scratch_shapes=[pltpu.VMEM((N_EXPERTS, LANES), F32)],
        compiler_params=_cparams(1), name="route",
    )(logits_t, bias_col, upper)


def _dest_kernel(cnt_ref, lower_ref, idx_ref, pos_ref, dest_ref, bexp_ref, bval_ref, nused_ref):
    cnt = cnt_ref[...]
    nblk = jnp.floor((cnt + (MOE_ROWS - 1)) * (1.0 / MOE_ROWS))
    bstart = jnp.dot(lower_ref[...], nblk, precision=HIGHEST, preferred_element_type=F32)
    bend = bstart + nblk
    pstart = bstart[:, 0:1] * MOE_ROWS
    rows = lax.broadcasted_iota(I32, (N_EXPERTS, idx_ref.shape[1]), 0)
    for k in range(TOP_K):
        hit = rows == idx_ref[k:k + 1, :]
        dest_ref[k:k + 1, :] = (jnp.sum(jnp.where(hit, pstart, 0.0), axis=0, keepdims=True)
                                .astype(I32) + pos_ref[k:k + 1, :])

    @pl.when(pl.program_id(0) == 0)
    def _():
        nb = bexp_ref.shape[1]
        bid = lax.broadcasted_iota(I32, (N_EXPERTS, nb), 1).astype(F32)
        inside = jnp.logical_and(bid >= bstart[:, 0:1], bid < bend[:, 0:1])
        erow = lax.broadcasted_iota(I32, (N_EXPERTS, nb), 0).astype(F32)
        bexp_ref[...] = jnp.sum(jnp.where(inside, erow, 0.0), axis=0, keepdims=True).astype(I32)
        valid = jnp.clip(cnt[:, 0:1] - (bid - bstart[:, 0:1]) * MOE_ROWS, 0.0, float(MOE_ROWS))
        bval_ref[...] = jnp.sum(jnp.where(inside, valid, 0.0), axis=0, keepdims=True).astype(I32)
        nused_ref[...] = jnp.max(bend, axis=0, keepdims=True).astype(I32)


def _destinations(counts, lower, idx_t, pos_t, n_blocks_pad):
    n = idx_t.shape[1]
    tt = TOK_TILE
    blk = pl.BlockSpec((TOP_K, tt), lambda i: (0, i))
    one = lambda w: pl.BlockSpec((1, w), lambda i: (0, 0))
    return pl.pallas_call(
        _dest_kernel,
        grid=(n // tt,),
        in_specs=[pl.BlockSpec((N_EXPERTS, LANES), lambda i: (0, 0)),
                  pl.BlockSpec((N_EXPERTS, N_EXPERTS), lambda i: (0, 0)), blk, blk],
        out_specs=(blk, one(n_blocks_pad), one(n_blocks_pad), one(LANES)),
        out_shape=(jax.ShapeDtypeStruct((TOP_K, n), I32),
                   jax.ShapeDtypeStruct((1, n_blocks_pad), I32),
                   jax.ShapeDtypeStruct((1, n_blocks_pad), I32),
                   jax.ShapeDtypeStruct((1, LANES), I32)),
        compiler_params=_cparams(1), name="destinations",
    )(counts, lower, idx_t, pos_t)


def _dispatch_kernel(dest_hbm, x_ref, xs_hbm, dest_smem, sem_idx, sem_rows):
    i = pl.program_id(0)
    n_idx = dest_smem.shape[0]
    cp = pltpu.make_async_copy(dest_hbm.at[pl.ds(i * n_idx, n_idx)], dest_smem, sem_idx)
    cp.start()
    cp.wait()

    n_tok = x_ref.shape[0] // ROW_CHUNKS

    def issue(t, carry):
        src = x_ref.at[pl.ds(pl.multiple_of(t * ROW_CHUNKS, ROW_CHUNKS), ROW_CHUNKS)]
        for k in range(TOP_K):
            row = pl.multiple_of(dest_smem[t * TOP_K + k] * ROW_CHUNKS, ROW_CHUNKS)
            pltpu.make_async_copy(src, xs_hbm.at[pl.ds(row, ROW_CHUNKS)], sem_rows).start()
        return carry

    lax.fori_loop(0, n_tok, issue, 0)
    for k in range(TOP_K):
        pltpu.make_async_copy(x_ref, xs_hbm.at[pl.ds(0, x_ref.shape[0])], sem_rows).wait()


def _dispatch(dest_flat, u2_rows, n_rows):
    n = u2_rows.shape[0] // ROW_CHUNKS
    td = DISP_TILE
    return pl.pallas_call(
        _dispatch_kernel,
        grid=(n // td,),
        in_specs=[pl.BlockSpec(memory_space=pl.ANY),
                  pl.BlockSpec((td * ROW_CHUNKS, LANES), lambda i: (i, 0))],
        out_specs=pl.BlockSpec(memory_space=pl.ANY),
        out_shape=jax.ShapeDtypeStruct((n_rows * ROW_CHUNKS, LANES), F32),
        scratch_shapes=[pltpu.SMEM((td * TOP_K,), I32), pltpu.SemaphoreType.DMA,
                        pltpu.SemaphoreType.DMA],
        compiler_params=_cparams(1), name="dispatch",
    )(dest_flat, u2_rows)


def _moe_kernel(bexp_ref, bval_ref, nused_ref, xs_ref, w1_ref, w3_ref, w2_ref, ys_ref,
                w13_s, w2_s):
    b = pl.program_id(0)

    @pl.when(b < nused_ref[0])
    def _():
        e = bexp_ref[b]
        prev = bexp_ref[jnp.maximum(b - 1, 0)]

        @pl.when(jnp.logical_or(b == 0, e != prev))
        def _():
            w13_s[:, 0:EXPERT_FF] = w1_ref[...].astype(BF16)
            w13_s[:, EXPERT_FF:2 * EXPERT_FF] = w3_ref[...].astype(BF16)
            w2_s[...] = w2_ref[...].astype(BF16)

        x = _load_row_tiles(xs_ref, MOE_ROWS)
        rows = lax.broadcasted_iota(I32, x.shape, 0)
        x = jnp.where(rows < bval_ref[b], x, 0.0).astype(BF16)
        ab = jnp.dot(x, w13_s[...], preferred_element_type=F32)
        hid = (_silu(ab[:, 0:EXPERT_FF]) * ab[:, EXPERT_FF:2 * EXPERT_FF]).astype(BF16)
        _store_row_tiles(ys_ref, jnp.dot(hid, w2_s[...], preferred_element_type=F32))


def _moe_experts(bexp, bval, nused, xs, w1, w3, w2, n_blocks):
    br = MOE_ROWS

    def row_idx(b, bexp, bval, nused):
        return (jnp.minimum(b, nused[0] - 1), 0)

    def w_idx(b, bexp, bval, nused):
        return (bexp[jnp.minimum(b, nused[0] - 1)], 0, 0)

    grid_spec = pltpu.PrefetchScalarGridSpec(
        num_scalar_prefetch=3, grid=(n_blocks,),
        in_specs=[pl.BlockSpec((br * ROW_CHUNKS, LANES), row_idx),
                  pl.BlockSpec((None, D_MODEL, EXPERT_FF), w_idx),
                  pl.BlockSpec((None, D_MODEL, EXPERT_FF), w_idx),
                  pl.BlockSpec((None, EXPERT_FF, D_MODEL), w_idx)],
        out_specs=pl.BlockSpec((br * ROW_CHUNKS, LANES), row_idx),
        scratch_shapes=[pltpu.VMEM((D_MODEL, 2 * EXPERT_FF), BF16),
                        pltpu.VMEM((EXPERT_FF, D_MODEL), BF16)])
    return pl.pallas_call(
        _moe_kernel, grid_spec=grid_spec,
        out_shape=jax.ShapeDtypeStruct(xs.shape, F32),
        compiler_params=_cparams(1), name="moe_experts",
    )(bexp, bval, nused, xs, w1, w3, w2)


COMB_IDX_SLOTS = 3
COMB_ROW_SLOTS = 2


def _combine_kernel(dest_hbm, w_hbm, ys_hbm, base_ref, mod_ref, g_ref, b_ref, y_ref,
                    dest_smem, w_smem, rows_buf, acc_buf, sem_idx, sem_w, sem_rows):
    i = pl.program_id(0)
    n_steps = pl.num_programs(0)
    n_tok = acc_buf.shape[0] // ROW_CHUNKS
    n_idx = n_tok * TOP_K
    n_buf_rows = n_idx * ROW_CHUNKS

    def idx_copies(tile):
        slot = lax.rem(tile, COMB_IDX_SLOTS) * n_idx
        return (pltpu.make_async_copy(dest_hbm.at[pl.ds(tile * n_idx, n_idx)],
                                      dest_smem.at[pl.ds(slot, n_idx)], sem_idx),
                pltpu.make_async_copy(w_hbm.at[pl.ds(tile * n_idx, n_idx)],
                                      w_smem.at[pl.ds(slot, n_idx)], sem_w))

    def issue_token(tile, t):
        islot = lax.rem(tile, COMB_IDX_SLOTS) * n_idx
        rslot = lax.rem(tile, COMB_ROW_SLOTS)
        for k in range(TOP_K):
            j = t * TOP_K + k
            src = pl.multiple_of(dest_smem[islot + j] * ROW_CHUNKS, ROW_CHUNKS)
            dst = pl.multiple_of(rslot * n_buf_rows + j * ROW_CHUNKS, ROW_CHUNKS)
            pltpu.make_async_copy(ys_hbm.at[pl.ds(src, ROW_CHUNKS)], rows_buf.at[pl.ds(dst, ROW_CHUNKS)],
                                  sem_rows.at[rslot]).start()

    def reduce_token(tile, t):
        islot = lax.rem(tile, COMB_IDX_SLOTS) * n_idx
        rbase = lax.rem(tile, COMB_ROW_SLOTS) * n_buf_rows
        acc = None
        for k in range(TOP_K):
            j = t * TOP_K + k
            row = pl.multiple_of(rbase + j * ROW_CHUNKS, ROW_CHUNKS)
            term = w_smem[islot + j] * rows_buf[pl.ds(row, ROW_CHUNKS), :]
            acc = term if acc is None else acc + term
        acc_buf[pl.ds(pl.multiple_of(t * ROW_CHUNKS, ROW_CHUNKS), ROW_CHUNKS), :] = acc

    @pl.when(i == 0)
    def _():
        for cp in idx_copies(i):
            cp.start()
        for cp in idx_copies(i):
            cp.wait()

        @pl.when(n_steps > 1)
        def _():
            for cp in idx_copies(i + 1):
                cp.start()

        lax.fori_loop(0, n_tok, lambda t, c: (issue_token(i, t), c)[1], 0)

    @pl.when(i + 1 < n_steps)
    def _():
        for cp in idx_copies(i + 1):
            cp.wait()

    @pl.when(i + 2 < n_steps)
    def _():
        for cp in idx_copies(i + 2):
            cp.start()

    rslot = lax.rem(i, COMB_ROW_SLOTS)
    pltpu.make_async_copy(
        ys_hbm.at[pl.ds(0, n_buf_rows)],
        rows_buf.at[pl.ds(pl.multiple_of(rslot * n_buf_rows, ROW_CHUNKS), n_buf_rows)],
        sem_rows.at[rslot]).wait()

    @pl.when(i + 1 < n_steps)
    def _():
        def both(t, c):
            issue_token(i + 1, t)
            reduce_token(i, t)
            return c
        lax.fori_loop(0, n_tok, both, 0)

    @pl.when(i + 1 == n_steps)
    def _():
        lax.fori_loop(0, n_tok, lambda t, c: (reduce_token(i, t), c)[1], 0)

    moe = _load_row_tiles(acc_buf, n_tok)
    gate2 = mod_ref[:, 5 * D_MODEL:6 * D_MODEL]
    y_ref[...] = _layer_norm(base_ref[...] + gate2 * moe, g_ref[...], b_ref[...])


def _combine(dest_flat, w_flat, ys, base, mod3, l2g, l2b, n_ctx, seq_tokens):
    n = base.shape[0]
    tc = COMB_TILE
    n_ctx_tiles = n_ctx // tc
    n_seq_tiles = seq_tokens // tc

    def mod_idx(i):
        return (jnp.where(i < n_ctx_tiles, 0, 1 + (i - n_ctx_tiles) // n_seq_tiles), 0, 0)

    full = lambda a: pl.BlockSpec(a.shape, lambda i: (0,) * a.ndim)
    return pl.pallas_call(
        _combine_kernel,
        grid=(n // tc,),
        in_specs=[pl.BlockSpec(memory_space=pl.ANY), pl.BlockSpec(memory_space=pl.ANY),
                  pl.BlockSpec(memory_space=pl.ANY),
                  pl.BlockSpec((tc, D_MODEL), lambda i: (i, 0)),
                  pl.BlockSpec((None, 1, mod3.shape[2]), mod_idx), full(l2g), full(l2b)],
        out_specs=pl.BlockSpec((tc, D_MODEL), lambda i: (i, 0)),
        out_shape=jax.ShapeDtypeStruct((n, D_MODEL), F32),
        scratch_shapes=[pltpu.SMEM((COMB_IDX_SLOTS * tc * TOP_K,), I32),
                        pltpu.SMEM((COMB_IDX_SLOTS * tc * TOP_K,), F32),
                        pltpu.VMEM((COMB_ROW_SLOTS * tc * TOP_K * ROW_CHUNKS, LANES), F32),
                        pltpu.VMEM((tc * ROW_CHUNKS, LANES), F32),
                        pltpu.SemaphoreType.DMA, pltpu.SemaphoreType.DMA,
                        pltpu.SemaphoreType.DMA((COMB_ROW_SLOTS,))],
        compiler_params=_cparams(1), name="combine",
    )(dest_flat, w_flat, ys, base, mod3, l2g, l2b)


def _rope_tables(n_tok, tile):
    rows = n_tok // GRID_W
    row_idx = jnp.repeat(jnp.arange(rows, dtype=F32), GRID_W)
    col_idx = jnp.tile(jnp.arange(GRID_W, dtype=F32), rows)
    inv_freq = 1.0 / (ROPE_THETA ** (jnp.arange(0, ROPE_AXIS_DIM, 2, dtype=F32) / ROPE_AXIS_DIM))
    ang_r = row_idx[:, None] * inv_freq[None, :]
    ang_c = col_idx[:, None] * inv_freq[None, :]
    ang = jnp.concatenate([ang_r, ang_r, ang_c, ang_c], axis=-1)
    cos, sin = jnp.cos(ang), jnp.sin(ang)
    quarter = (jnp.arange(HEAD_DIM) // (ROPE_AXIS_DIM // 2)) % 2
    sin_a = jnp.where(quarter == 0, -sin, 0.0)
    sin_b = jnp.where(quarter == 1, sin, 0.0)
    rep = LANES // HEAD_DIM
    ident = lambda v: jnp.full((tile, LANES), v, F32)
    cos_t = jnp.concatenate([jnp.tile(cos, (1, rep)), ident(1.0)], axis=0)
    sa_t = jnp.concatenate([jnp.tile(sin_a, (1, rep)), ident(0.0)], axis=0)
    sb_t = jnp.concatenate([jnp.tile(sin_b, (1, rep)), ident(0.0)], axis=0)
    return cos_t, sa_t, sb_t


def _dup_heads(a):
    parts = []
    for h in range(KV_HEADS):
        blk = a[..., h * HEAD_DIM:(h + 1) * HEAD_DIM]
        parts += [blk] * (LANES // HEAD_DIM)
    return jnp.concatenate(parts, axis=-1)


def kernel(x_prompt, x_sample, cache_k, cache_v, state_gla_fwd, state_gla_bwd, c, c_ctx, w_ada, b_ada, w_in, q_norm, k_norm, gla_wa_fwd, gla_ba_fwd, gla_wa_bwd, gla_ba_bwd, gla_norm, w_out, ln1_g, ln1_b, ln2_g, ln2_b, w_router, router_bias, exp_w1, exp_w3, exp_w2, sh_w1, sh_w3, sh_w2):
    n_ctx_b, ctx_seq, _ = x_prompt.shape
    n_lat_b, lat_seq, _ = x_sample.shape
    n_ctx = n_ctx_b * ctx_seq
    n_lat = n_lat_b * lat_seq
    n = n_ctx + n_lat
    l = 0

    x_all = jnp.concatenate([x_prompt.reshape(n_ctx, D_MODEL), x_sample.reshape(n_lat, D_MODEL)], axis=0)

    c_rows = jnp.zeros((SUBLANES, D_MODEL), F32).at[0].set(c_ctx).at[1:1 + n_lat_b].set(c)
    mod = _modulation(c_rows, w_ada[l], b_ada[l][None, :])
    mod3 = mod.reshape(SUBLANES, 1, 6 * D_MODEL)

    wi = w_in[l]
    o_q, o_k, o_v, o_gq, o_gk, o_gv, o_gg, o_rf, o_rb, o_end = np.cumsum(
        [0, ATT_WIDTH, KV_HEADS * HEAD_DIM, KV_HEADS * HEAD_DIM, GLA_KW, GLA_KW, GLA_WIDTH, GLA_WIDTH,
         GLA_GATE_RANK, GLA_GATE_RANK])
    w_tok = jnp.concatenate([
        wi[:, o_q:o_k], _dup_heads(wi[:, o_k:o_v]), _dup_heads(wi[:, o_v:o_gq]), wi[:, o_gq:o_gk],
        wi[:, o_gv:o_gg], wi[:, o_gg:o_rf], wi[:, o_rf:o_end],
        jnp.zeros((D_MODEL, LANES - 2 * GLA_GATE_RANK), F32)], axis=1).astype(BF16)
    w_tr = jnp.concatenate([wi[:, o_gk:o_gv], wi[:, o_rf:o_end]], axis=1).T.astype(BF16)
    rep = LANES // HEAD_DIM
    qn = jnp.tile(q_norm[l], rep)[None, :]
    kn = jnp.tile(k_norm[l], rep)[None, :]
    seg = jnp.asarray(np.kron(np.eye(rep), np.ones((HEAD_DIM, HEAD_DIM))), BF16)
    wa = jnp.zeros((LANES, 2 * GLA_KW), F32)
    wa = wa.at[0:GLA_GATE_RANK, 0:GLA_KW].set(gla_wa_fwd[l])
    wa = wa.at[GLA_GATE_RANK:2 * GLA_GATE_RANK, GLA_KW:].set(gla_wa_bwd[l])
    ba = jnp.concatenate([gla_ba_fwd[l], gla_ba_bwd[l]])[None, :]
    wat = wa[0:2 * GLA_GATE_RANK, :].T
    bat = ba.T
    cos_t, sa_t, sb_t = _rope_tables(lat_seq, TOK_TILE)

    (q, k_dup, v_dup, k32, v32, gq, gv, gg, la, gkt, lat) = _in_projection(
        x_all, mod3, w_tok, w_tr, qn, kn, cos_t, sa_t, sb_t, seg, wa, ba, wat, bat,
        n_ctx, lat_seq // TOK_TILE)

    ck = _dup_heads(cache_k[:, l].reshape(n_lat_b, -1, KV_HEADS * HEAD_DIM)).astype(BF16)
    cv = _dup_heads(cache_v[:, l].reshape(n_lat_b, -1, KV_HEADS * HEAD_DIM)).astype(BF16)
    att_ctx = _attention(q, k_dup, v_dup, None, 0, n_ctx_b, ctx_seq)
    att_lat = _attention(q, k_dup, v_dup, (ck, cv), n_ctx, n_lat_b, lat_seq)
    att = jnp.concatenate([att_ctx, att_lat], axis=0)

    gconst, levels_of = _gla_constants()
    to_dev = lambda t: (jnp.asarray(t[0], BF16), jnp.asarray(t[1], BF16), jnp.asarray(t[2], F32))
    bd = jnp.asarray(np.kron(np.eye(GLA_HEADS), np.ones((GLA_DK, GLA_DV))), BF16)
    vbd = jnp.asarray(np.kron(np.eye(GLA_HEADS), np.ones((GLA_CHUNK, GLA_DV))), BF16)
    consts = ((to_dev(gconst["f"]), to_dev(gconst["b"])), levels_of, bd, vbd)
    s_zero = jnp.zeros((n_ctx_b, GLA_HEADS, GLA_DK, GLA_DV), F32)
    of_c, ob_c, sf_new, sb_new = _gla(gq, la, gkt, lat, gv, s_zero, s_zero, consts, 0, n_ctx_b, ctx_seq)
    of_l, ob_l, _, _ = _gla(gq, la, gkt, lat, gv, state_gla_fwd[:, l], state_gla_bwd[:, l], consts,
                            n_ctx, n_lat_b, lat_seq)
    o_f = jnp.concatenate([of_c, of_l], axis=0)
    o_b = jnp.concatenate([ob_c, ob_l], axis=0)

    sw13 = jnp.concatenate([sh_w1[l], sh_w3[l]], axis=1).astype(BF16)
    base, u2_rows, logits_t = _out_projection(
        att, o_f, o_b, gg, x_all, mod3, w_out[l].astype(BF16), gla_norm[l][None, :],
        ln1_g[l][None, :], ln1_b[l][None, :], w_router[l].T.astype(BF16), sw13,
        sh_w2[l].astype(BF16), n_ctx, lat_seq // TOK_TILE)

    upper = jnp.asarray(np.triu(np.ones((TOK_TILE, TOK_TILE)), 1), BF16)
    idx_t, w_t, pos_t, counts = _route(logits_t, router_bias[l][:, None], upper)
    n_blocks = n * TOP_K // MOE_ROWS + N_EXPERTS
    n_blocks_pad = -(-n_blocks // LANES) * LANES
    lower = jnp.asarray(np.tril(np.ones((N_EXPERTS, N_EXPERTS)), -1), F32)
    dest_t, bexp, bval, nused = _destinations(counts, lower, idx_t, pos_t, n_blocks_pad)
    dest_flat = dest_t.T.reshape(-1)
    w_flat = w_t.T.reshape(-1)

    xs = _dispatch(dest_flat, u2_rows, n_blocks * MOE_ROWS)
    ys = _moe_experts(bexp.reshape(-1), bval.reshape(-1), nused.reshape(-1)[0:1], xs,
                      exp_w1[l], exp_w3[l], exp_w2[l], n_blocks)
    y_all = _combine(dest_flat, w_flat, ys, base, mod3, ln2_g[l][None, :], ln2_b[l][None, :],
                     n_ctx, lat_seq)

    y_prompt = y_all[:n_ctx].reshape(n_ctx_b, ctx_seq, D_MODEL)
    y_sample = y_all[n_ctx:].reshape(n_lat_b, lat_seq, D_MODEL)
    new_cache_k = k32.reshape(n_ctx_b, 1, ctx_seq, KV_HEADS, HEAD_DIM)
    new_cache_v = v32.reshape(n_ctx_b, 1, ctx_seq, KV_HEADS, HEAD_DIM)
    return (y_prompt, y_sample, new_cache_k, new_cache_v, sf_new[:, None], sb_new[:, None])
```

```python
import functools

import numpy as np
import jax
import jax.numpy as jnp
from jax import lax
from jax.experimental import pallas as pl
from jax.experimental.pallas import tpu as pltpu

F32 = jnp.float32
BF16 = jnp.bfloat16
I32 = jnp.int32

D_MODEL = 1024
GRID_W = 64
HEAD_DIM = 64
N_HEADS = 8
KV_HEADS = 2
ATT_WIDTH = N_HEADS * HEAD_DIM
ATT_SCALE = HEAD_DIM ** -0.5
LOG2_E = 1.4426950408889634
ROPE_AXIS_DIM = HEAD_DIM // 2
ROPE_THETA = 10000.0
GLA_HEADS = 4
GLA_DK = 64
GLA_DV = 128
GLA_WIDTH = GLA_HEADS * GLA_DV
GLA_KW = GLA_HEADS * GLA_DK
GLA_GATE_RANK = 16
GLA_TAU = 16.0
N_EXPERTS = 256
TOP_K = 8
EXPERT_FF = 256
SHARED_FF = 256
ROUTED_SCALE = 2.5
DEPTH = 1
ALPHA = (2.0 * DEPTH) ** 0.25
EPS = 1e-6

LANES = 128
SUBLANES = 8
ROW_CHUNKS = D_MODEL // LANES
VMEM_LIMIT = 56 * 1024 * 1024

TOK_TILE = 512
ATT_TQ = 128
GLA_CHUNK = 128
GLA_LEVELS = ((32, 128), (8, 32), (2, 8), (1, 2))
MOE_ROWS = 256
DISP_TILE = 256
COMB_TILE = 128
HIGHEST = lax.Precision.HIGHEST


def _cparams(n_axes):
    return pltpu.CompilerParams(dimension_semantics=("arbitrary",) * n_axes,
                                vmem_limit_bytes=VMEM_LIMIT)


def _silu(x):
    return x * jax.nn.sigmoid(x)


def _log_sigmoid(x):
    return jnp.minimum(x, 0.0) - jnp.log(1.0 + jnp.exp(-jnp.abs(x)))


def _load_row_tiles(ref, n_rows, row0=0):
    return jnp.concatenate(
        [ref[pl.ds(row0 * ROW_CHUNKS + cidx, n_rows, stride=ROW_CHUNKS), :] for cidx in range(ROW_CHUNKS)],
        axis=1)


def _store_row_tiles(ref, x):
    for cidx in range(ROW_CHUNKS):
        ref[pl.ds(cidx, x.shape[0], stride=ROW_CHUNKS), :] = x[:, cidx * LANES:(cidx + 1) * LANES]


def _layer_norm(z, g, b):
    mu = jnp.mean(z, axis=-1, keepdims=True)
    zc = z - mu
    var = jnp.mean(zc * zc, axis=-1, keepdims=True)
    return zc * lax.rsqrt(var + EPS) * g + b


def _mod_kernel(c_ref, w_ref, b_ref, o_ref):
    s = _silu(c_ref[...]).astype(BF16)
    o_ref[...] = jnp.dot(s, w_ref[...].astype(BF16), preferred_element_type=F32) + b_ref[...]


def _modulation(c_rows, w_ada, b_ada):
    n_cols = w_ada.shape[1]
    tn = 512
    return pl.pallas_call(
        _mod_kernel,
        grid=(n_cols // tn,),
        in_specs=[pl.BlockSpec((SUBLANES, D_MODEL), lambda j: (0, 0)),
                  pl.BlockSpec((D_MODEL, tn), lambda j: (0, j)),
                  pl.BlockSpec((1, tn), lambda j: (0, j))],
        out_specs=pl.BlockSpec((SUBLANES, tn), lambda j: (0, j)),
        out_shape=jax.ShapeDtypeStruct((SUBLANES, n_cols), F32),
        compiler_params=_cparams(1),
        name="modulation",
    )(c_rows, w_ada, b_ada)


_C_K = 0
_C_V = _C_K + 2 * LANES
_C_GQ = _C_V + KV_HEADS * HEAD_DIM
_C_GV = _C_GQ + GLA_KW
_C_GG = _C_GV + GLA_WIDTH
_C_RA = _C_GG + GLA_WIDTH
_C_END = _C_RA + LANES
_R_Q = 0
_R_V = _R_Q + ATT_WIDTH
_R_GK = _R_V + KV_HEADS * HEAD_DIM
_R_RA = _R_GK + GLA_KW
_R_END = _R_RA + 2 * GLA_GATE_RANK


def _inproj_kernel(xc_ref, xl_ref, mod_ref, w_ref, wt_ref, qn_ref, kn_ref, cos_ref, sa_ref, sb_ref,
                   cost_ref, sint_ref, seg_ref, wa_ref, ba_ref, wat_ref, bat_ref,
                   qt_ref, k_ref, vt_ref, k32_ref, v32_ref, gq_ref, gv_ref, gg_ref,
                   la_ref, gkt_ref, lat_ref, *, n_ctx_tiles):
    i = pl.program_id(0)
    m = mod_ref[...]
    shift1 = m[:, 0:D_MODEL]
    scale1 = m[:, D_MODEL:2 * D_MODEL]
    x = jnp.where(i < n_ctx_tiles, xc_ref[...], xl_ref[...])
    u = (x * (1.0 + scale1) + shift1).astype(BF16)

    cos = cos_ref[...]
    sin_a = sa_ref[...]
    sin_b = sb_ref[...]
    seg = seg_ref[...]
    lane = lax.broadcasted_iota(I32, (u.shape[0], LANES), 1)
    low = lane < HEAD_DIM

    def proj(c0, c1):
        return jnp.dot(u, w_ref[:, c0:c1], preferred_element_type=F32)

    def head_norm(blk, gain):
        ss = jnp.dot((blk * blk).astype(BF16), seg, preferred_element_type=F32) * (1.0 / HEAD_DIM)
        return blk * lax.rsqrt(ss + EPS) * gain

    def rope(blk):
        return (blk * cos + pltpu.roll(blk, LANES - ROPE_AXIS_DIM // 2, 1) * sin_a
                + pltpu.roll(blk, ROPE_AXIS_DIM // 2, 1) * sin_b)

    pk = proj(_C_K, _C_V)
    kn = [head_norm(pk[:, j * LANES:(j + 1) * LANES], kn_ref[...]) for j in range(KV_HEADS)]
    for j in range(KV_HEADS):
        k_ref[:, j * LANES:(j + 1) * LANES] = rope(kn[j]).astype(BF16)

    @pl.when(i < n_ctx_tiles)
    def _():
        k32_ref[...] = jnp.where(low, kn[0], kn[1])
        v32_ref[...] = proj(_C_V, _C_GQ)

    gq_ref[...] = proj(_C_GQ, _C_GV) * (GLA_DK ** -0.5)
    gv_ref[...] = proj(_C_GV, _C_GG).astype(BF16)
    gg_ref[...] = proj(_C_GG, _C_RA).astype(BF16)

    ra = proj(_C_RA, _C_END)
    pre = jnp.dot(ra, wa_ref[...], precision=HIGHEST, preferred_element_type=F32) + ba_ref[...]
    la_ref[...] = _log_sigmoid(pre) * (1.0 / GLA_TAU)

    pt = lax.dot_general(wt_ref[...], u, (((1,), (1,)), ((), ())), preferred_element_type=F32)
    cos_t = cost_ref[...]
    sin_t = sint_ref[...]
    quarter = ROPE_AXIS_DIM // 2
    for h in range(N_HEADS):
        blk = pt[_R_Q + h * HEAD_DIM:_R_Q + (h + 1) * HEAD_DIM, :]
        ms = jnp.mean(blk * blk, axis=0, keepdims=True)
        qn = blk * lax.rsqrt(ms + EPS) * qn_ref[...]
        rot = jnp.concatenate([-qn[quarter:2 * quarter], qn[0:quarter],
                               -qn[3 * quarter:4 * quarter], qn[2 * quarter:3 * quarter]], axis=0)
        qt_ref[h * HEAD_DIM:(h + 1) * HEAD_DIM, :] = (
            (qn * cos_t + rot * sin_t) * (ATT_SCALE * LOG2_E)).astype(BF16)
    vt_ref[...] = pt[_R_V:_R_GK, :].astype(BF16)
    gkt_ref[...] = pt[_R_GK:_R_RA, :]
    rat = pt[_R_RA:_R_END, :]
    pre_t = jnp.dot(wat_ref[...], rat, precision=HIGHEST, preferred_element_type=F32) + bat_ref[...]
    lat_ref[...] = _log_sigmoid(pre_t) * (1.0 / GLA_TAU)


def _in_projection(x_c, x_l, mod3, w_tok, w_tr, qn, kn, cos_t, sa_t, sb_t, cos_tr, sin_tr, seg, wa, ba,
                   wat, bat, n_seq_tiles):
    n_ctx = x_c.shape[0]
    n = n_ctx + x_l.shape[0]
    tb = TOK_TILE
    n_ctx_tiles = n_ctx // tb
    n_tiles = n // tb
    n_rope_blocks = cos_t.shape[0] // tb - 1

    def mod_idx(i):
        return (jnp.where(i < n_ctx_tiles, 0, 1 + (i - n_ctx_tiles) // n_seq_tiles), 0, 0)

    def rope_blk(i):
        return jnp.where(i < n_ctx_tiles, n_rope_blocks, (i - n_ctx_tiles) % n_seq_tiles)

    def rope_idx(i):
        return (rope_blk(i), 0)

    def ctx_idx(i):
        return (jnp.minimum(i, n_ctx_tiles - 1), 0)

    tok = lambda w: pl.BlockSpec((tb, w), lambda i: (i, 0))
    full = lambda a: pl.BlockSpec(a.shape, lambda i: (0,) * a.ndim)
    tr = lambda r: pl.BlockSpec((r, tb), lambda i: (0, i))
    rope_tr = pl.BlockSpec((HEAD_DIM, tb), lambda i: (0, rope_blk(i)))
    out_shapes = (
        jax.ShapeDtypeStruct((ATT_WIDTH, n), BF16),
        jax.ShapeDtypeStruct((n, 2 * LANES), BF16),
        jax.ShapeDtypeStruct((KV_HEADS * HEAD_DIM, n), BF16),
        jax.ShapeDtypeStruct((n_ctx, LANES), F32),
        jax.ShapeDtypeStruct((n_ctx, LANES), F32),
        jax.ShapeDtypeStruct((n, GLA_KW), F32),
        jax.ShapeDtypeStruct((n, GLA_WIDTH), BF16),
        jax.ShapeDtypeStruct((n, GLA_WIDTH), BF16),
        jax.ShapeDtypeStruct((n, 2 * GLA_KW), F32),
        jax.ShapeDtypeStruct((GLA_KW, n), F32),
        jax.ShapeDtypeStruct((2 * GLA_KW, n), F32),
    )
    out_specs = (tr(ATT_WIDTH), tok(2 * LANES), tr(KV_HEADS * HEAD_DIM),
                 pl.BlockSpec((tb, LANES), ctx_idx), pl.BlockSpec((tb, LANES), ctx_idx),
                 tok(GLA_KW), tok(GLA_WIDTH), tok(GLA_WIDTH), tok(2 * GLA_KW),
                 tr(GLA_KW), tr(2 * GLA_KW))
    in_specs = [pl.BlockSpec((tb, D_MODEL), ctx_idx),
                pl.BlockSpec((tb, D_MODEL), lambda i: (jnp.maximum(i - n_ctx_tiles, 0), 0)),
                pl.BlockSpec((None, 1, mod3.shape[2]), mod_idx),
                full(w_tok), full(w_tr), full(qn), full(kn),
                pl.BlockSpec((tb, LANES), rope_idx), pl.BlockSpec((tb, LANES), rope_idx),
                pl.BlockSpec((tb, LANES), rope_idx), rope_tr, rope_tr,
                full(seg), full(wa), full(ba), full(wat), full(bat)]
    return pl.pallas_call(
        functools.partial(_inproj_kernel, n_ctx_tiles=n_ctx_tiles),
        grid=(n_tiles,), in_specs=in_specs, out_specs=out_specs, out_shape=out_shapes,
        compiler_params=_cparams(1), name="in_projection",
    )(x_c, x_l, mod3, w_tok, w_tr, qn, kn, cos_t, sa_t, sb_t, cos_tr, sin_tr, seg, wa, ba, wat, bat)


def _attention_kernel(*refs, n_kv_parts):
    qt_ref = refs[0]
    k_refs = refs[1:1 + n_kv_parts]
    vt_refs = refs[1 + n_kv_parts:1 + 2 * n_kv_parts]
    o_ref = refs[1 + 2 * n_kv_parts]
    tq = qt_ref.shape[1]
    group = N_HEADS // KV_HEADS
    for kv in range(KV_HEADS):
        heads = range(kv * group, (kv + 1) * group)
        q_grp = jnp.concatenate([qt_ref[h * HEAD_DIM:(h + 1) * HEAD_DIM, :] for h in heads], axis=1)
        rhs = jnp.concatenate([q_grp, jnp.zeros_like(q_grp)], axis=0)
        s = [jnp.dot(k[:, kv * LANES:(kv + 1) * LANES], rhs, preferred_element_type=F32)
             for k in k_refs]
        mx = functools.reduce(jnp.maximum, [jnp.max(x, axis=0, keepdims=True) for x in s])
        pr = [jnp.exp2(x - mx) for x in s]
        den = functools.reduce(jnp.add, [jnp.sum(x, axis=0, keepdims=True) for x in pr])
        acc = functools.reduce(jnp.add, [
            jnp.dot(vt[kv * HEAD_DIM:(kv + 1) * HEAD_DIM, :], x.astype(BF16),
                    preferred_element_type=F32) for x, vt in zip(pr, vt_refs)])
        out = (acc / den).astype(BF16)
        for j, h in enumerate(heads):
            o_ref[h * HEAD_DIM:(h + 1) * HEAD_DIM, :] = out[:, j * tq:(j + 1) * tq]


def _attention(qt, k, vt, extra_kv, row0, n_batch, seq):
    tq = ATT_TQ
    n_q = seq // tq
    q_blk0 = row0 // tq
    kv_blk0 = row0 // seq
    in_specs = [pl.BlockSpec((ATT_WIDTH, tq), lambda b, i: (0, q_blk0 + b * n_q + i))]
    k_spec = pl.BlockSpec((seq, 2 * LANES), lambda b, i: (kv_blk0 + b, 0))
    vt_spec = pl.BlockSpec((KV_HEADS * HEAD_DIM, seq), lambda b, i: (0, kv_blk0 + b))
    args_k, args_v, specs_k, specs_v = [k], [vt], [k_spec], [vt_spec]
    if extra_kv is not None:
        ck, cvt = extra_kv
        args_k.append(ck)
        args_v.append(cvt)
        specs_k.append(pl.BlockSpec((None, ck.shape[1], 2 * LANES), lambda b, i: (b, 0, 0)))
        specs_v.append(pl.BlockSpec((None, KV_HEADS * HEAD_DIM, cvt.shape[2]), lambda b, i: (b, 0, 0)))
    return pl.pallas_call(
        functools.partial(_attention_kernel, n_kv_parts=len(args_k)),
        grid=(n_batch, n_q),
        in_specs=in_specs + specs_k + specs_v,
        out_specs=pl.BlockSpec((ATT_WIDTH, tq), lambda b, i: (0, b * n_q + i)),
        out_shape=jax.ShapeDtypeStruct((ATT_WIDTH, n_batch * seq), BF16),
        compiler_params=_cparams(2), name="attention",
    )(qt, *args_k, *args_v)


def _gla_constants():
    c = GLA_CHUNK
    idx = np.arange(c)
    q_mats, k_mats, masks, levels_of = [], [], [], []
    for li, (s, p) in enumerate(GLA_LEVELS):
        start = (idx // s) * s
        end = start + s - 1
        k_mats.append(((idx[None, :] > idx[:, None]) & (idx[None, :] <= end[:, None])))
        for d in range(p // s - 1):
            lo = np.maximum(start - d * s, 0)
            q_mats.append((idx[None, :] >= lo[:, None]) & (idx[None, :] <= idx[:, None]))
            masks.append((idx[:, None] // p == idx[None, :] // p)
                         & (idx[:, None] // s - idx[None, :] // s - 1 == d))
            levels_of.append(li)
    masks.append(np.eye(c, dtype=bool))
    levels_of.append(len(GLA_LEVELS) - 1)
    q_mats.append(idx[None, :] <= idx[:, None])
    k_mats = k_mats[:-1]
    k_mats.append(idx[None, :] > idx[:, None])
    k_mats.append(np.ones((c, c), bool))
    out = {}
    for name, flip in (("f", False), ("b", True)):
        f = (lambda a: a[::-1, ::-1]) if flip else (lambda a: a)
        lq = np.concatenate([f(a) for a in q_mats], axis=0).astype(np.float32)
        lkt = np.concatenate([f(a).T for a in k_mats], axis=1).astype(np.float32)
        mk = np.stack([np.tile(f(a), (1, GLA_HEADS)) for a in masks]).astype(np.float32)
        out[name] = (np.concatenate([lq, lq], axis=1), np.concatenate([lkt, lkt], axis=0), mk)
    return out, tuple(levels_of)


def _gla_direction(q, g, gkt, gt, v, lq2, lkt2, masks_ref, bd, vbd, s_ref, levels_of):
    c = GLA_CHUNK
    n_var = len(levels_of)
    n_lev = len(GLA_LEVELS)
    g_hi = g.astype(BF16)
    g_lo = (g - g_hi.astype(F32)).astype(BF16)
    fq = jnp.dot(lq2, jnp.concatenate([g_hi, g_lo], axis=0), preferred_element_type=F32)
    gt_hi = gt.astype(BF16)
    gt_lo = (gt - gt_hi.astype(F32)).astype(BF16)
    fk = jnp.dot(jnp.concatenate([gt_hi, gt_lo], axis=1), lkt2, preferred_element_type=F32)

    def key_factor(f):
        return gkt * jnp.exp(fk[:, f * c:(f + 1) * c])

    q_var = [(q * jnp.exp(fq[vi * c:(vi + 1) * c, :])).astype(BF16) for vi in range(n_var - 1)]
    q_var.append(q.astype(BF16))
    a = jnp.zeros((c, GLA_HEADS * c), F32)
    for li in range(n_lev):
        kt = (key_factor(li) if li < n_lev - 1 else gkt).astype(BF16)
        xt = jnp.concatenate([kt] * GLA_HEADS, axis=1) * bd
        vis = [vi for vi in range(n_var) if levels_of[vi] == li]
        res = jnp.dot(jnp.concatenate([q_var[vi] for vi in vis], axis=0), xt,
                      preferred_element_type=F32)
        for r, vi in enumerate(vis):
            a = a + masks_ref[vi] * res[r * c:(r + 1) * c, :]
    q_in = (q * jnp.exp(fq[(n_var - 1) * c:n_var * c, :])).astype(BF16)
    state = s_ref[...]
    v_bd = jnp.concatenate([v] * GLA_HEADS, axis=0) * vbd
    o = (jnp.dot(q_in, state.astype(BF16), preferred_element_type=F32)
         + jnp.dot(a.astype(BF16), v_bd, preferred_element_type=F32))
    k_out = key_factor(n_lev - 1).astype(BF16)
    e_tot = jnp.exp(fk[:, n_lev * c:(n_lev + 1) * c])
    upd = jnp.dot(k_out, v, preferred_element_type=F32)
    s_ref[...] = (state * jnp.concatenate([e_tot] * (GLA_WIDTH // c), axis=1)
                  + upd * bd.astype(F32))
    return o


def _gla_kernel(gq_f, la_f, gkt_f, lat_f, gv_f, gq_b, la_b, gkt_b, lat_b, gv_b,
                s0f_ref, s0b_ref, lq2f, lkt2f, mkf, lq2b, lkt2b, mkb, bd_ref, vbd_ref,
                of_ref, ob_ref, sf_ref, sb_ref, st_f, st_b, *, levels_of):
    n = pl.program_id(1)

    @pl.when(n == 0)
    def _():
        st_f[...] = jnp.zeros_like(st_f)
        st_b[...] = jnp.zeros_like(st_b)
        for h in range(GLA_HEADS):
            rows = slice(h * GLA_DK, (h + 1) * GLA_DK)
            cols = slice(h * GLA_DV, (h + 1) * GLA_DV)
            st_f[rows, cols] = s0f_ref[h]
            st_b[rows, cols] = s0b_ref[h]

    bd = bd_ref[...]
    vbd = vbd_ref[...]
    of_ref[...] = _gla_direction(gq_f[...], la_f[...], gkt_f[...], lat_f[...], gv_f[...],
                                 lq2f[...], lkt2f[...], mkf, bd, vbd, st_f, levels_of)
    ob_ref[...] = _gla_direction(gq_b[...], la_b[...], gkt_b[...], lat_b[...], gv_b[...],
                                 lq2b[...], lkt2b[...], mkb, bd, vbd, st_b, levels_of)

    @pl.when(n == pl.num_programs(1) - 1)
    def _():
        for h in range(GLA_HEADS):
            rows = slice(h * GLA_DK, (h + 1) * GLA_DK)
            cols = slice(h * GLA_DV, (h + 1) * GLA_DV)
            sf_ref[h] = st_f[rows, cols]
            sb_ref[h] = st_b[rows, cols]


def _gla(gq, la, gkt, lat, gv, s0f, s0b, consts, row0, n_batch, seq):
    (cf, cb), levels_of, bd, vbd = consts
    c = GLA_CHUNK
    nc = seq // c
    blk0 = row0 // c
    n_la_blocks_b = 1
    fwd = lambda b, n: blk0 + b * nc + n
    bwd = lambda b, n: blk0 + b * nc + (nc - 1 - n)

    def tok(w, which, col=0):
        return pl.BlockSpec((c, w), lambda b, n: (which(b, n), col))

    def tr(r, which, row=0):
        return pl.BlockSpec((r, c), lambda b, n: (row, which(b, n)))

    full = lambda a: pl.BlockSpec(a.shape, lambda b, n: (0,) * a.ndim)
    st_spec = pl.BlockSpec((None, GLA_HEADS, GLA_DK, GLA_DV), lambda b, n: (b, 0, 0, 0))
    in_specs = [tok(GLA_KW, fwd), tok(GLA_KW, fwd, 0), tr(GLA_KW, fwd), tr(GLA_KW, fwd, 0),
                tok(GLA_WIDTH, fwd),
                tok(GLA_KW, bwd), tok(GLA_KW, bwd, n_la_blocks_b), tr(GLA_KW, bwd),
                tr(GLA_KW, bwd, 1), tok(GLA_WIDTH, bwd),
                st_spec, st_spec,
                full(cf[0]), full(cf[1]), full(cf[2]), full(cb[0]), full(cb[1]), full(cb[2]),
                full(bd), full(vbd)]
    out_specs = (pl.BlockSpec((c, GLA_WIDTH), lambda b, n: (b * nc + n, 0)),
                 pl.BlockSpec((c, GLA_WIDTH), lambda b, n: (b * nc + (nc - 1 - n), 0)),
                 st_spec, st_spec)
    out_shape = (jax.ShapeDtypeStruct((n_batch * seq, GLA_WIDTH), F32),
                 jax.ShapeDtypeStruct((n_batch * seq, GLA_WIDTH), F32),
                 jax.ShapeDtypeStruct((n_batch, GLA_HEADS, GLA_DK, GLA_DV), F32),
                 jax.ShapeDtypeStruct((n_batch, GLA_HEADS, GLA_DK, GLA_DV), F32))
    return pl.pallas_call(
        functools.partial(_gla_kernel, levels_of=levels_of),
        grid=(n_batch, nc), in_specs=in_specs, out_specs=out_specs, out_shape=out_shape,
        scratch_shapes=[pltpu.VMEM((GLA_KW, GLA_WIDTH), F32), pltpu.VMEM((GLA_KW, GLA_WIDTH), F32)],
        compiler_params=_cparams(2), name="gla",
    )(gq, la, gkt, lat, gv, gq, la, gkt, lat, gv, s0f, s0b,
      cf[0], cf[1], cf[2], cb[0], cb[1], cb[2], bd, vbd)


def _outproj_kernel(attc_ref, attl_ref, ofc_ref, ofl_ref, obc_ref, obl_ref, gg_ref, xc_ref, xl_ref,
                    mod_ref, wo_ref, gn_ref, l1g_ref, l1b_ref, wrt_ref, sw13_ref, sw2_ref,
                    base_ref, u2_ref, lg_ref, *, n_ctx_tiles):
    is_ctx = pl.program_id(0) < n_ctx_tiles
    pick = lambda a_ref, b_ref: jnp.where(is_ctx, a_ref[...], b_ref[...])
    m = mod_ref[...]
    gate1 = m[:, 2 * D_MODEL:3 * D_MODEL]
    shift2 = m[:, 3 * D_MODEL:4 * D_MODEL]
    scale2 = m[:, 4 * D_MODEL:5 * D_MODEL]
    gate2 = m[:, 5 * D_MODEL:6 * D_MODEL]
    og = pick(ofc_ref, ofl_ref) + pick(obc_ref, obl_ref)
    gg = gg_ref[...].astype(F32)
    parts = []
    for h in range(GLA_HEADS):
        blk = og[:, h * GLA_DV:(h + 1) * GLA_DV]
        ms = jnp.mean(blk * blk, axis=-1, keepdims=True)
        nb = blk * lax.rsqrt(ms + EPS) * gn_ref[...]
        parts.append((nb * _silu(gg[:, h * GLA_DV:(h + 1) * GLA_DV])).astype(BF16))
    att_t = pick(attc_ref, attl_ref)
    hmix = (lax.dot_general(att_t, wo_ref[0:ATT_WIDTH, :], (((0,), (0,)), ((), ())),
                            preferred_element_type=F32)
            + jnp.dot(jnp.concatenate(parts, axis=1), wo_ref[ATT_WIDTH:, :],
                      preferred_element_type=F32))
    x1 = _layer_norm(ALPHA * pick(xc_ref, xl_ref) + gate1 * hmix, l1g_ref[...], l1b_ref[...])
    u2 = x1 * (1.0 + scale2) + shift2
    u2b = u2.astype(BF16)
    lg_ref[...] = lax.dot_general(wrt_ref[...], u2b, (((1,), (1,)), ((), ())),
                                  preferred_element_type=F32)
    ab = jnp.dot(u2b, sw13_ref[...], preferred_element_type=F32)
    hid = (_silu(ab[:, 0:SHARED_FF]) * ab[:, SHARED_FF:2 * SHARED_FF]).astype(BF16)
    shared = jnp.dot(hid, sw2_ref[...], preferred_element_type=F32)
    base_ref[...] = ALPHA * x1 + gate2 * shared
    _store_row_tiles(u2_ref, u2)


def _out_projection(att_c, att_l, of_c, of_l, ob_c, ob_l, gg, x_c, x_l, mod3, wo, gn, l1g, l1b, wrt,
                    sw13, sw2, n_seq_tiles):
    n_ctx = x_c.shape[0]
    n = n_ctx + x_l.shape[0]
    tb = TOK_TILE
    n_ctx_tiles = n_ctx // tb

    def mod_idx(i):
        return (jnp.where(i < n_ctx_tiles, 0, 1 + (i - n_ctx_tiles) // n_seq_tiles), 0, 0)

    ctx_blk = lambda i: jnp.minimum(i, n_ctx_tiles - 1)
    lat_blk = lambda i: jnp.maximum(i - n_ctx_tiles, 0)
    tok = lambda w: pl.BlockSpec((tb, w), lambda i: (i, 0))
    tok_c = lambda w: pl.BlockSpec((tb, w), lambda i: (ctx_blk(i), 0))
    tok_l = lambda w: pl.BlockSpec((tb, w), lambda i: (lat_blk(i), 0))
    full = lambda a: pl.BlockSpec(a.shape, lambda i: (0,) * a.ndim)
    return pl.pallas_call(
        functools.partial(_outproj_kernel, n_ctx_tiles=n_ctx_tiles),
        grid=(n // tb,),
        in_specs=[pl.BlockSpec((ATT_WIDTH, tb), lambda i: (0, ctx_blk(i))),
                  pl.BlockSpec((ATT_WIDTH, tb), lambda i: (0, lat_blk(i))),
                  tok_c(GLA_WIDTH), tok_l(GLA_WIDTH), tok_c(GLA_WIDTH), tok_l(GLA_WIDTH),
                  tok(GLA_WIDTH), tok_c(D_MODEL), tok_l(D_MODEL),
                  pl.BlockSpec((None, 1, mod3.shape[2]), mod_idx),
                  full(wo), full(gn), full(l1g), full(l1b), full(wrt), full(sw13), full(sw2)],
        out_specs=(tok(D_MODEL),
                   pl.BlockSpec((tb * ROW_CHUNKS, LANES), lambda i: (i, 0)),
                   pl.BlockSpec((N_EXPERTS, tb), lambda i: (0, i))),
        out_shape=(jax.ShapeDtypeStruct((n, D_MODEL), F32),
                   jax.ShapeDtypeStruct((n * ROW_CHUNKS, LANES), F32),
                   jax.ShapeDtypeStruct((N_EXPERTS, n), F32)),
        compiler_params=_cparams(1), name="out_projection",
    )(att_c, att_l, of_c, of_l, ob_c, ob_l, gg, x_c, x_l, mod3, wo, gn, l1g, l1b, wrt, sw13, sw2)


def _route_kernel(lg_ref, bias_ref, upper_ref, idx_ref, w_ref, pos_ref, cnt_ref, run_ref):
    i = pl.program_id(0)

    @pl.when(i == 0)
    def _():
        run_ref[...] = jnp.zeros_like(run_ref)

    s = jax.nn.sigmoid(lg_ref[...])
    work = s + bias_ref[...]
    rows = lax.broadcasted_iota(I32, s.shape, 0).astype(F32)
    sel = jnp.zeros(s.shape, F32)
    idxs, vals = [], []
    for _ in range(TOP_K):
        mx = jnp.max(work, axis=0, keepdims=True)
        idx = jnp.min(jnp.where(work == mx, rows, float(N_EXPERTS)), axis=0, keepdims=True)
        hit = rows == idx
        vals.append(jnp.sum(jnp.where(hit, s, 0.0), axis=0, keepdims=True))
        idxs.append(idx)
        sel = jnp.where(hit, 1.0, sel)
        work = jnp.where(hit, -jnp.inf, work)
    den = functools.reduce(jnp.add, vals)
    rank = jnp.dot(sel.astype(BF16), upper_ref[...], preferred_element_type=F32) + run_ref[:, 0:1]
    for k in range(TOP_K):
        idx_ref[k:k + 1, :] = idxs[k].astype(I32)
        w_ref[k:k + 1, :] = vals[k] / den * ROUTED_SCALE
        pos_ref[k:k + 1, :] = jnp.sum(jnp.where(rows == idxs[k], rank, 0.0), axis=0,
                                      keepdims=True).astype(I32)
    run_ref[...] = run_ref[...] + jnp.sum(sel, axis=1, keepdims=True)
    cnt_ref[...] = run_ref[...]


def _route(logits_t, bias_col, upper):
    n = logits_t.shape[1]
    tt = TOK_TILE
    row = lambda dt: jax.ShapeDtypeStruct((TOP_K, n), dt)
    blk = pl.BlockSpec((TOP_K, tt), lambda i: (0, i))
    return pl.pallas_call(
        _route_kernel,
        grid=(n // tt,),
        in_specs=[pl.BlockSpec((N_EXPERTS, tt), lambda i: (0, i)),
                  pl.BlockSpec((N_EXPERTS, 1), lambda i: (0, 0)),
                  pl.BlockSpec((tt, tt), lambda i: (0, 0))],
        out_specs=(blk, blk, blk, pl.BlockSpec((N_EXPERTS, LANES), lambda i: (0, 0))),
        out_shape=(row(I32), row(F32), row(I32), jax.ShapeDtypeStruct((N_EXPERTS, LANES), F32)),
        scratch_shapes=[pltpu.VMEM((N_EXPERTS, LANES), F32)],
        compiler_params=_cparams(1), name="route",
    )(logits_t, bias_col, upper)


def _dest_kernel(cnt_ref, lower_ref, idx_ref, pos_ref, dest_ref, bexp_ref, bval_ref, nused_ref):
    cnt = cnt_ref[...]
    nblk = jnp.floor((cnt + (MOE_ROWS - 1)) * (1.0 / MOE_ROWS))
    bstart = jnp.dot(lower_ref[...], nblk, precision=HIGHEST, preferred_element_type=F32)
    bend = bstart + nblk
    pstart = bstart[:, 0:1] * MOE_ROWS
    rows = lax.broadcasted_iota(I32, (N_EXPERTS, idx_ref.shape[1]), 0)
    for k in range(TOP_K):
        hit = rows == idx_ref[k:k + 1, :]
        dest_ref[k:k + 1, :] = (jnp.sum(jnp.where(hit, pstart, 0.0), axis=0, keepdims=True)
                                .astype(I32) + pos_ref[k:k + 1, :])

    @pl.when(pl.program_id(0) == 0)
    def _():
        nb = bexp_ref.shape[1]
        bid = lax.broadcasted_iota(I32, (N_EXPERTS, nb), 1).astype(F32)
        inside = jnp.logical_and(bid >= bstart[:, 0:1], bid < bend[:, 0:1])
        erow = lax.broadcasted_iota(I32, (N_EXPERTS, nb), 0).astype(F32)
        bexp_ref[...] = jnp.sum(jnp.where(inside, erow, 0.0), axis=0, keepdims=True).astype(I32)
        valid = jnp.clip(cnt[:, 0:1] - (bid - bstart[:, 0:1]) * MOE_ROWS, 0.0, float(MOE_ROWS))
        bval_ref[...] = jnp.sum(jnp.where(inside, valid, 0.0), axis=0, keepdims=True).astype(I32)
        nused_ref[...] = jnp.max(bend, axis=0, keepdims=True).astype(I32)


def _destinations(counts, lower, idx_t, pos_t, n_blocks_pad):
    n = idx_t.shape[1]
    tt = TOK_TILE
    blk = pl.BlockSpec((TOP_K, tt), lambda i: (0, i))
    one = lambda w: pl.BlockSpec((1, w), lambda i: (0, 0))
    return pl.pallas_call(
        _dest_kernel,
        grid=(n // tt,),
        in_specs=[pl.BlockSpec((N_EXPERTS, LANES), lambda i: (0, 0)),
                  pl.BlockSpec((N_EXPERTS, N_EXPERTS), lambda i: (0, 0)), blk, blk],
        out_specs=(blk, one(n_blocks_pad), one(n_blocks_pad), one(LANES)),
        out_shape=(jax.ShapeDtypeStruct((TOP_K, n), I32),
                   jax.ShapeDtypeStruct((1, n_blocks_pad), I32),
                   jax.ShapeDtypeStruct((1, n_blocks_pad), I32),
                   jax.ShapeDtypeStruct((1, LANES), I32)),
        compiler_params=_cparams(1), name="destinations",
    )(counts, lower, idx_t, pos_t)


def _dispatch_kernel(dest_hbm, x_ref, xs_hbm, dest_smem, sem_idx, sem_rows):
    i = pl.program_id(0)
    n_idx = dest_smem.shape[0]
    cp = pltpu.make_async_copy(dest_hbm.at[pl.ds(i * n_idx, n_idx)], dest_smem, sem_idx)
    cp.start()
    cp.wait()

    n_tok = x_ref.shape[0] // ROW_CHUNKS

    def issue(t, carry):
        src = x_ref.at[pl.ds(pl.multiple_of(t * ROW_CHUNKS, ROW_CHUNKS), ROW_CHUNKS)]
        for k in range(TOP_K):
            row = pl.multiple_of(dest_smem[t * TOP_K + k] * ROW_CHUNKS, ROW_CHUNKS)
            pltpu.make_async_copy(src, xs_hbm.at[pl.ds(row, ROW_CHUNKS)], sem_rows).start()
        return carry

    lax.fori_loop(0, n_tok, issue, 0)
    for k in range(TOP_K):
        pltpu.make_async_copy(x_ref, xs_hbm.at[pl.ds(0, x_ref.shape[0])], sem_rows).wait()


def _dispatch(dest_flat, u2_rows, n_rows):
    n = u2_rows.shape[0] // ROW_CHUNKS
    td = DISP_TILE
    return pl.pallas_call(
        _dispatch_kernel,
        grid=(n // td,),
        in_specs=[pl.BlockSpec(memory_space=pl.ANY),
                  pl.BlockSpec((td * ROW_CHUNKS, LANES), lambda i: (i, 0))],
        out_specs=pl.BlockSpec(memory_space=pl.ANY),
        out_shape=jax.ShapeDtypeStruct((n_rows * ROW_CHUNKS, LANES), F32),
        scratch_shapes=[pltpu.SMEM((td * TOP_K,), I32), pltpu.SemaphoreType.DMA,
                        pltpu.SemaphoreType.DMA],
        compiler_params=_cparams(1), name="dispatch",
    )(dest_flat, u2_rows)


def _moe_kernel(bexp_ref, bval_ref, nused_ref, xs_ref, w1_ref, w3_ref, w2_ref, ys_ref,
                w13_s, w2_s):
    b = pl.program_id(0)

    @pl.when(b < nused_ref[0])
    def _():
        e = bexp_ref[b]
        prev = bexp_ref[jnp.maximum(b - 1, 0)]

        @pl.when(jnp.logical_or(b == 0, e != prev))
        def _():
            w13_s[:, 0:EXPERT_FF] = w1_ref[...].astype(BF16)
            w13_s[:, EXPERT_FF:2 * EXPERT_FF] = w3_ref[...].astype(BF16)
            w2_s[...] = w2_ref[...].astype(BF16)

        x = _load_row_tiles(xs_ref, MOE_ROWS)
        rows = lax.broadcasted_iota(I32, x.shape, 0)
        x = jnp.where(rows < bval_ref[b], x, 0.0).astype(BF16)
        ab = jnp.dot(x, w13_s[...], preferred_element_type=F32)
        hid = (_silu(ab[:, 0:EXPERT_FF]) * ab[:, EXPERT_FF:2 * EXPERT_FF]).astype(BF16)
        _store_row_tiles(ys_ref, jnp.dot(hid, w2_s[...], preferred_element_type=F32))


def _moe_experts(bexp, bval, nused, xs, w1, w3, w2, n_blocks):
    br = MOE_ROWS

    def row_idx(b, bexp, bval, nused):
        return (jnp.minimum(b, nused[0] - 1), 0)

    def w_idx(b, bexp, bval, nused):
        return (bexp[jnp.minimum(b, nused[0] - 1)], 0, 0)

    grid_spec = pltpu.PrefetchScalarGridSpec(
        num_scalar_prefetch=3, grid=(n_blocks,),
        in_specs=[pl.BlockSpec((br * ROW_CHUNKS, LANES), row_idx),
                  pl.BlockSpec((None, D_MODEL, EXPERT_FF), w_idx),
                  pl.BlockSpec((None, D_MODEL, EXPERT_FF), w_idx),
                  pl.BlockSpec((None, EXPERT_FF, D_MODEL), w_idx)],
        out_specs=pl.BlockSpec((br * ROW_CHUNKS, LANES), row_idx),
        scratch_shapes=[pltpu.VMEM((D_MODEL, 2 * EXPERT_FF), BF16),
                        pltpu.VMEM((EXPERT_FF, D_MODEL), BF16)])
    return pl.pallas_call(
        _moe_kernel, grid_spec=grid_spec,
        out_shape=jax.ShapeDtypeStruct(xs.shape, F32),
        compiler_params=_cparams(1), name="moe_experts",
    )(bexp, bval, nused, xs, w1, w3, w2)


COMB_IDX_SLOTS = 3
COMB_ROW_SLOTS = 2


def _combine_kernel(dest_hbm, w_hbm, ys_hbm, base_ref, mod_ref, g_ref, b_ref, yc_ref, yl_ref,
                    dest_smem, w_smem, rows_buf, acc_buf, sem_idx, sem_w, sem_rows, *, n_ctx_tiles):
    i = pl.program_id(0)
    n_steps = pl.num_programs(0)
    n_tok = acc_buf.shape[0] // ROW_CHUNKS
    n_idx = n_tok * TOP_K
    n_buf_rows = n_idx * ROW_CHUNKS

    def idx_copies(tile):
        slot = lax.rem(tile, COMB_IDX_SLOTS) * n_idx
        return (pltpu.make_async_copy(dest_hbm.at[pl.ds(tile * n_idx, n_idx)],
                                      dest_smem.at[pl.ds(slot, n_idx)], sem_idx),
                pltpu.make_async_copy(w_hbm.at[pl.ds(tile * n_idx, n_idx)],
                                      w_smem.at[pl.ds(slot, n_idx)], sem_w))

    def issue_token(tile, t):
        islot = lax.rem(tile, COMB_IDX_SLOTS) * n_idx
        rslot = lax.rem(tile, COMB_ROW_SLOTS)
        for k in range(TOP_K):
            j = t * TOP_K + k
            src = pl.multiple_of(dest_smem[islot + j] * ROW_CHUNKS, ROW_CHUNKS)
            dst = pl.multiple_of(rslot * n_buf_rows + j * ROW_CHUNKS, ROW_CHUNKS)
            pltpu.make_async_copy(ys_hbm.at[pl.ds(src, ROW_CHUNKS)], rows_buf.at[pl.ds(dst, ROW_CHUNKS)],
                                  sem_rows.at[rslot]).start()

    def reduce_token(tile, t):
        islot = lax.rem(tile, COMB_IDX_SLOTS) * n_idx
        rbase = lax.rem(tile, COMB_ROW_SLOTS) * n_buf_rows
        acc = None
        for k in range(TOP_K):
            j = t * TOP_K + k
            row = pl.multiple_of(rbase + j * ROW_CHUNKS, ROW_CHUNKS)
            term = w_smem[islot + j] * rows_buf[pl.ds(row, ROW_CHUNKS), :]
            acc = term if acc is None else acc + term
        acc_buf[pl.ds(pl.multiple_of(t * ROW_CHUNKS, ROW_CHUNKS), ROW_CHUNKS), :] = acc

    @pl.when(i == 0)
    def _():
        for cp in idx_copies(i):
            cp.start()
        for cp in idx_copies(i):
            cp.wait()

        @pl.when(n_steps > 1)
        def _():
            for cp in idx_copies(i + 1):
                cp.start()

        lax.fori_loop(0, n_tok, lambda t, c: (issue_token(i, t), c)[1], 0)

    @pl.when(i + 1 < n_steps)
    def _():
        for cp in idx_copies(i + 1):
            cp.wait()

    @pl.when(i + 2 < n_steps)
    def _():
        for cp in idx_copies(i + 2):
            cp.start()

    rslot = lax.rem(i, COMB_ROW_SLOTS)
    pltpu.make_async_copy(
        ys_hbm.at[pl.ds(0, n_buf_rows)],
        rows_buf.at[pl.ds(pl.multiple_of(rslot * n_buf_rows, ROW_CHUNKS), n_buf_rows)],
        sem_rows.at[rslot]).wait()

    @pl.when(i + 1 < n_steps)
    def _():
        def both(t, c):
            issue_token(i + 1, t)
            reduce_token(i, t)
            return c
        lax.fori_loop(0, n_tok, both, 0)

    @pl.when(i + 1 == n_steps)
    def _():
        lax.fori_loop(0, n_tok, lambda t, c: (reduce_token(i, t), c)[1], 0)

    moe = _load_row_tiles(acc_buf, n_tok)
    gate2 = mod_ref[:, 5 * D_MODEL:6 * D_MODEL]
    y = _layer_norm(base_ref[...] + gate2 * moe, g_ref[...], b_ref[...])

    @pl.when(i < n_ctx_tiles)
    def _():
        yc_ref[...] = y

    @pl.when(i >= n_ctx_tiles)
    def _():
        yl_ref[...] = y


def _combine(dest_flat, w_flat, ys, base, mod3, l2g, l2b, n_ctx, seq_tokens):
    n = base.shape[0]
    tc = COMB_TILE
    n_ctx_tiles = n_ctx // tc
    n_seq_tiles = seq_tokens // tc

    def mod_idx(i):
        return (jnp.where(i < n_ctx_tiles, 0, 1 + (i - n_ctx_tiles) // n_seq_tiles), 0, 0)

    full = lambda a: pl.BlockSpec(a.shape, lambda i: (0,) * a.ndim)
    return pl.pallas_call(
        functools.partial(_combine_kernel, n_ctx_tiles=n_ctx_tiles),
        grid=(n // tc,),
        in_specs=[pl.BlockSpec(memory_space=pl.ANY), pl.BlockSpec(memory_space=pl.ANY),
                  pl.BlockSpec(memory_space=pl.ANY),
                  pl.BlockSpec((tc, D_MODEL), lambda i: (i, 0)),
                  pl.BlockSpec((None, 1, mod3.shape[2]), mod_idx), full(l2g), full(l2b)],
        out_specs=(pl.BlockSpec((tc, D_MODEL), lambda i: (jnp.minimum(i, n_ctx_tiles - 1), 0)),
                   pl.BlockSpec((tc, D_MODEL), lambda i: (jnp.maximum(i - n_ctx_tiles, 0), 0))),
        out_shape=(jax.ShapeDtypeStruct((n_ctx, D_MODEL), F32),
                   jax.ShapeDtypeStruct((n - n_ctx, D_MODEL), F32)),
        scratch_shapes=[pltpu.SMEM((COMB_IDX_SLOTS * tc * TOP_K,), I32),
                        pltpu.SMEM((COMB_IDX_SLOTS * tc * TOP_K,), F32),
                        pltpu.VMEM((COMB_ROW_SLOTS * tc * TOP_K * ROW_CHUNKS, LANES), F32),
                        pltpu.VMEM((tc * ROW_CHUNKS, LANES), F32),
                        pltpu.SemaphoreType.DMA, pltpu.SemaphoreType.DMA,
                        pltpu.SemaphoreType.DMA((COMB_ROW_SLOTS,))],
        compiler_params=_cparams(1), name="combine",
    )(dest_flat, w_flat, ys, base, mod3, l2g, l2b)


def _rope_tables(n_tok, tile):
    rows = n_tok // GRID_W
    row_idx = jnp.repeat(jnp.arange(rows, dtype=F32), GRID_W)
    col_idx = jnp.tile(jnp.arange(GRID_W, dtype=F32), rows)
    inv_freq = 1.0 / (ROPE_THETA ** (jnp.arange(0, ROPE_AXIS_DIM, 2, dtype=F32) / ROPE_AXIS_DIM))
    ang_r = row_idx[:, None] * inv_freq[None, :]
    ang_c = col_idx[:, None] * inv_freq[None, :]
    ang = jnp.concatenate([ang_r, ang_r, ang_c, ang_c], axis=-1)
    cos, sin = jnp.cos(ang), jnp.sin(ang)
    quarter = (jnp.arange(HEAD_DIM) // (ROPE_AXIS_DIM // 2)) % 2
    sin_a = jnp.where(quarter == 0, -sin, 0.0)
    sin_b = jnp.where(quarter == 1, sin, 0.0)
    rep = LANES // HEAD_DIM
    ident = lambda v: jnp.full((tile, LANES), v, F32)
    cos_t = jnp.concatenate([jnp.tile(cos, (1, rep)), ident(1.0)], axis=0)
    sa_t = jnp.concatenate([jnp.tile(sin_a, (1, rep)), ident(0.0)], axis=0)
    sb_t = jnp.concatenate([jnp.tile(sin_b, (1, rep)), ident(0.0)], axis=0)
    ident_tr = lambda v: jnp.full((HEAD_DIM, tile), v, F32)
    cos_tr = jnp.concatenate([cos.T, ident_tr(1.0)], axis=1)
    sin_tr = jnp.concatenate([sin.T, ident_tr(0.0)], axis=1)
    return cos_t, sa_t, sb_t, cos_tr, sin_tr


def _dup_heads(a):
    parts = []
    for h in range(KV_HEADS):
        blk = a[..., h * HEAD_DIM:(h + 1) * HEAD_DIM]
        parts += [blk] * (LANES // HEAD_DIM)
    return jnp.concatenate(parts, axis=-1)


def kernel(x_prompt, x_sample, cache_k, cache_v, state_gla_fwd, state_gla_bwd, c, c_ctx, w_ada, b_ada, w_in, q_norm, k_norm, gla_wa_fwd, gla_ba_fwd, gla_wa_bwd, gla_ba_bwd, gla_norm, w_out, ln1_g, ln1_b, ln2_g, ln2_b, w_router, router_bias, exp_w1, exp_w3, exp_w2, sh_w1, sh_w3, sh_w2):
    n_ctx_b, ctx_seq, _ = x_prompt.shape
    n_lat_b, lat_seq, _ = x_sample.shape
    n_ctx = n_ctx_b * ctx_seq
    n_lat = n_lat_b * lat_seq
    n = n_ctx + n_lat
    l = 0

    x_c = x_prompt.reshape(n_ctx, D_MODEL)
    x_l = x_sample.reshape(n_lat, D_MODEL)

    c_rows = jnp.zeros((SUBLANES, D_MODEL), F32).at[0].set(c_ctx).at[1:1 + n_lat_b].set(c)
    mod = _modulation(c_rows, w_ada[l], b_ada[l][None, :])
    mod3 = mod.reshape(SUBLANES, 1, 6 * D_MODEL)

    wi = w_in[l]
    o_q, o_k, o_v, o_gq, o_gk, o_gv, o_gg, o_rf, o_rb, o_end = np.cumsum(
        [0, ATT_WIDTH, KV_HEADS * HEAD_DIM, KV_HEADS * HEAD_DIM, GLA_KW, GLA_KW, GLA_WIDTH, GLA_WIDTH,
         GLA_GATE_RANK, GLA_GATE_RANK])
    w_tok = jnp.concatenate([
        _dup_heads(wi[:, o_k:o_v]), wi[:, o_v:o_gq], wi[:, o_gq:o_gk],
        wi[:, o_gv:o_gg], wi[:, o_gg:o_rf], wi[:, o_rf:o_end],
        jnp.zeros((D_MODEL, LANES - 2 * GLA_GATE_RANK), F32)], axis=1).astype(BF16)
    w_tr = jnp.concatenate([wi[:, o_q:o_k], wi[:, o_v:o_gq], wi[:, o_gk:o_gv], wi[:, o_rf:o_end]],
                           axis=1).T.astype(BF16)
    rep = LANES // HEAD_DIM
    qn = q_norm[l][:, None]
    kn = jnp.tile(k_norm[l], rep)[None, :]
    seg = jnp.asarray(np.kron(np.eye(rep), np.ones((HEAD_DIM, HEAD_DIM))), BF16)
    wa = jnp.zeros((LANES, 2 * GLA_KW), F32)
    wa = wa.at[0:GLA_GATE_RANK, 0:GLA_KW].set(gla_wa_fwd[l])
    wa = wa.at[GLA_GATE_RANK:2 * GLA_GATE_RANK, GLA_KW:].set(gla_wa_bwd[l])
    ba = jnp.concatenate([gla_ba_fwd[l], gla_ba_bwd[l]])[None, :]
    wat = wa[0:2 * GLA_GATE_RANK, :].T
    bat = ba.T
    cos_t, sa_t, sb_t, cos_tr, sin_tr = _rope_tables(lat_seq, TOK_TILE)

    (qt, k_dup, vt, k32, v32, gq, gv, gg, la, gkt, lat) = _in_projection(
        x_c, x_l, mod3, w_tok, w_tr, qn, kn, cos_t, sa_t, sb_t, cos_tr, sin_tr, seg, wa, ba, wat, bat,
        lat_seq // TOK_TILE)

    ck = _dup_heads(cache_k[:, l].reshape(n_lat_b, -1, KV_HEADS * HEAD_DIM)).astype(BF16)
    cvt = cache_v[:, l].reshape(n_lat_b, -1, KV_HEADS * HEAD_DIM).transpose(0, 2, 1).astype(BF16)
    att_c = _attention(qt, k_dup, vt, None, 0, n_ctx_b, ctx_seq)
    att_l = _attention(qt, k_dup, vt, (ck, cvt), n_ctx, n_lat_b, lat_seq)

    gconst, levels_of = _gla_constants()
    to_dev = lambda t: (jnp.asarray(t[0], BF16), jnp.asarray(t[1], BF16), jnp.asarray(t[2], F32))
    bd = jnp.asarray(np.kron(np.eye(GLA_HEADS), np.ones((GLA_DK, GLA_DV))), BF16)
    vbd = jnp.asarray(np.kron(np.eye(GLA_HEADS), np.ones((GLA_CHUNK, GLA_DV))), BF16)
    consts = ((to_dev(gconst["f"]), to_dev(gconst["b"])), levels_of, bd, vbd)
    s_zero = jnp.zeros((n_ctx_b, GLA_HEADS, GLA_DK, GLA_DV), F32)
    of_c, ob_c, sf_new, sb_new = _gla(gq, la, gkt, lat, gv, s_zero, s_zero, consts, 0, n_ctx_b, ctx_seq)
    of_l, ob_l, _, _ = _gla(gq, la, gkt, lat, gv, state_gla_fwd[:, l], state_gla_bwd[:, l], consts,
                            n_ctx, n_lat_b, lat_seq)

    sw13 = jnp.concatenate([sh_w1[l], sh_w3[l]], axis=1).astype(BF16)
    base, u2_rows, logits_t = _out_projection(
        att_c, att_l, of_c, of_l, ob_c, ob_l, gg, x_c, x_l, mod3, w_out[l].astype(BF16),
        gla_norm[l][None, :], ln1_g[l][None, :], ln1_b[l][None, :], w_router[l].T.astype(BF16), sw13,
        sh_w2[l].astype(BF16), lat_seq // TOK_TILE)

    upper = jnp.asarray(np.triu(np.ones((TOK_TILE, TOK_TILE)), 1), BF16)
    idx_t, w_t, pos_t, counts = _route(logits_t, router_bias[l][:, None], upper)
    n_blocks = n * TOP_K // MOE_ROWS + N_EXPERTS
    n_blocks_pad = -(-n_blocks // LANES) * LANES
    lower = jnp.asarray(np.tril(np.ones((N_EXPERTS, N_EXPERTS)), -1), F32)
    dest_t, bexp, bval, nused = _destinations(counts, lower, idx_t, pos_t, n_blocks_pad)
    dest_flat = dest_t.T.reshape(-1)
    w_flat = w_t.T.reshape(-1)

    xs = _dispatch(dest_flat, u2_rows, n_blocks * MOE_ROWS)
    ys = _moe_experts(bexp.reshape(-1), bval.reshape(-1), nused.reshape(-1)[0:1], xs,
                      exp_w1[l], exp_w3[l], exp_w2[l], n_blocks)
    y_c, y_l = _combine(dest_flat, w_flat, ys, base, mod3, ln2_g[l][None, :], ln2_b[l][None, :],
                        n_ctx, lat_seq)

    y_prompt = y_c.reshape(n_ctx_b, ctx_seq, D_MODEL)
    y_sample = y_l.reshape(n_lat_b, lat_seq, D_MODEL)
    new_cache_k = k32.reshape(n_ctx_b, 1, ctx_seq, KV_HEADS, HEAD_DIM)
    new_cache_v = v32.reshape(n_ctx_b, 1, ctx_seq, KV_HEADS, HEAD_DIM)
    return (y_prompt, y_sample, new_cache_k, new_cache_v, sf_new[:, None], sb_new[:, None])
```

```python
import functools

import numpy as np
import jax
import jax.numpy as jnp
from jax import lax
from jax.experimental import pallas as pl
from jax.experimental.pallas import tpu as pltpu

F32 = jnp.float32
BF16 = jnp.bfloat16
I32 = jnp.int32

D_MODEL = 1024
GRID_W = 64
HEAD_DIM = 64
N_HEADS = 8
KV_HEADS = 2
ATT_WIDTH = N_HEADS * HEAD_DIM
ATT_SCALE = HEAD_DIM ** -0.5
LOG2_E = 1.4426950408889634
ROPE_AXIS_DIM = HEAD_DIM // 2
ROPE_THETA = 10000.0
GLA_HEADS = 4
GLA_DK = 64
GLA_DV = 128
GLA_WIDTH = GLA_HEADS * GLA_DV
GLA_KW = GLA_HEADS * GLA_DK
GLA_GATE_RANK = 16
GLA_TAU = 16.0
N_EXPERTS = 256
TOP_K = 8
EXPERT_FF = 256
SHARED_FF = 256
ROUTED_SCALE = 2.5
DEPTH = 1
ALPHA = (2.0 * DEPTH) ** 0.25
EPS = 1e-6

LANES = 128
SUBLANES = 8
ROW_CHUNKS = D_MODEL // LANES
VMEM_LIMIT = 56 * 1024 * 1024

TOK_TILE = 512
ATT_TQ = 128
GLA_CHUNK = 128
GLA_LEVELS = ((32, 128), (8, 32), (2, 8), (1, 2))
MOE_ROWS = 256
MOE_VMEM_LIMIT = 62 * 1024 * 1024
COMB_TILE = 128
HIGHEST = lax.Precision.HIGHEST


def _cparams(n_axes):
    return pltpu.CompilerParams(dimension_semantics=("arbitrary",) * n_axes,
                                vmem_limit_bytes=VMEM_LIMIT)


def _silu(x):
    return x * jax.nn.sigmoid(x)


def _log_sigmoid(x):
    return jnp.minimum(x, 0.0) - jnp.log(1.0 + jnp.exp(-jnp.abs(x)))


def _load_row_tiles(ref, n_rows, row0=0):
    return jnp.concatenate(
        [ref[pl.ds(row0 * ROW_CHUNKS + cidx, n_rows, stride=ROW_CHUNKS), :] for cidx in range(ROW_CHUNKS)],
        axis=1)


def _store_row_tiles(ref, x):
    for cidx in range(ROW_CHUNKS):
        ref[pl.ds(cidx, x.shape[0], stride=ROW_CHUNKS), :] = x[:, cidx * LANES:(cidx + 1) * LANES]


def _layer_norm(z, g, b):
    mu = jnp.mean(z, axis=-1, keepdims=True)
    zc = z - mu
    var = jnp.mean(zc * zc, axis=-1, keepdims=True)
    return zc * lax.rsqrt(var + EPS) * g + b


def _mod_kernel(c_ref, w_ref, b_ref, o_ref):
    s = _silu(c_ref[...]).astype(BF16)
    o_ref[...] = jnp.dot(s, w_ref[...].astype(BF16), preferred_element_type=F32) + b_ref[...]


def _modulation(c_rows, w_ada, b_ada):
    n_cols = w_ada.shape[1]
    tn = 512
    return pl.pallas_call(
        _mod_kernel,
        grid=(n_cols // tn,),
        in_specs=[pl.BlockSpec((SUBLANES, D_MODEL), lambda j: (0, 0)),
                  pl.BlockSpec((D_MODEL, tn), lambda j: (0, j)),
                  pl.BlockSpec((1, tn), lambda j: (0, j))],
        out_specs=pl.BlockSpec((SUBLANES, tn), lambda j: (0, j)),
        out_shape=jax.ShapeDtypeStruct((SUBLANES, n_cols), F32),
        compiler_params=_cparams(1),
        name="modulation",
    )(c_rows, w_ada, b_ada)


_C_K = 0
_C_V = _C_K + 2 * LANES
_C_GQ = _C_V + KV_HEADS * HEAD_DIM
_C_GV = _C_GQ + GLA_KW
_C_GG = _C_GV + GLA_WIDTH
_C_RA = _C_GG + GLA_WIDTH
_C_END = _C_RA + LANES
_R_Q = 0
_R_V = _R_Q + ATT_WIDTH
_R_GK = _R_V + KV_HEADS * HEAD_DIM
_R_RA = _R_GK + GLA_KW
_R_END = _R_RA + 2 * GLA_GATE_RANK


def _inproj_kernel(xc_ref, xl_ref, mod_ref, w_ref, wt_ref, qn_ref, kn_ref, cos_ref, sa_ref, sb_ref,
                   cost_ref, sint_ref, seg_ref, wa_ref, ba_ref, wat_ref, bat_ref,
                   qt_ref, k_ref, vt_ref, k32_ref, v32_ref, gq_ref, gv_ref, gg_ref,
                   la_ref, gkt_ref, lat_ref, *, n_ctx_tiles):
    i = pl.program_id(0)
    m = mod_ref[...]
    shift1 = m[:, 0:D_MODEL]
    scale1 = m[:, D_MODEL:2 * D_MODEL]
    x = jnp.where(i < n_ctx_tiles, xc_ref[...], xl_ref[...])
    u = (x * (1.0 + scale1) + shift1).astype(BF16)

    cos = cos_ref[...]
    sin_a = sa_ref[...]
    sin_b = sb_ref[...]
    seg = seg_ref[...]
    lane = lax.broadcasted_iota(I32, (u.shape[0], LANES), 1)
    low = lane < HEAD_DIM

    def proj(c0, c1):
        return jnp.dot(u, w_ref[:, c0:c1], preferred_element_type=F32)

    def head_norm(blk, gain):
        ss = jnp.dot((blk * blk).astype(BF16), seg, preferred_element_type=F32) * (1.0 / HEAD_DIM)
        return blk * lax.rsqrt(ss + EPS) * gain

    def rope(blk):
        return (blk * cos + pltpu.roll(blk, LANES - ROPE_AXIS_DIM // 2, 1) * sin_a
                + pltpu.roll(blk, ROPE_AXIS_DIM // 2, 1) * sin_b)

    pk = proj(_C_K, _C_V)
    kn = [head_norm(pk[:, j * LANES:(j + 1) * LANES], kn_ref[...]) for j in range(KV_HEADS)]
    for j in range(KV_HEADS):
        k_ref[:, j * LANES:(j + 1) * LANES] = rope(kn[j]).astype(BF16)

    @pl.when(i < n_ctx_tiles)
    def _():
        k32_ref[...] = jnp.where(low, kn[0], kn[1])
        v32_ref[...] = proj(_C_V, _C_GQ)

    gq_ref[...] = proj(_C_GQ, _C_GV) * (GLA_DK ** -0.5)
    gv_ref[...] = proj(_C_GV, _C_GG).astype(BF16)
    gg_ref[...] = proj(_C_GG, _C_RA).astype(BF16)

    ra = proj(_C_RA, _C_END)
    pre = jnp.dot(ra, wa_ref[...], precision=HIGHEST, preferred_element_type=F32) + ba_ref[...]
    la_ref[...] = _log_sigmoid(pre) * (1.0 / GLA_TAU)

    pt = lax.dot_general(wt_ref[...], u, (((1,), (1,)), ((), ())), preferred_element_type=F32)
    cos_t = cost_ref[...]
    sin_t = sint_ref[...]
    quarter = ROPE_AXIS_DIM // 2
    for h in range(N_HEADS):
        blk = pt[_R_Q + h * HEAD_DIM:_R_Q + (h + 1) * HEAD_DIM, :]
        ms = jnp.mean(blk * blk, axis=0, keepdims=True)
        qn = blk * lax.rsqrt(ms + EPS) * qn_ref[...]
        rot = jnp.concatenate([-qn[quarter:2 * quarter], qn[0:quarter],
                               -qn[3 * quarter:4 * quarter], qn[2 * quarter:3 * quarter]], axis=0)
        qt_ref[h * HEAD_DIM:(h + 1) * HEAD_DIM, :] = (
            (qn * cos_t + rot * sin_t) * (ATT_SCALE * LOG2_E)).astype(BF16)
    vt_ref[...] = pt[_R_V:_R_GK, :].astype(BF16)
    gkt_ref[...] = pt[_R_GK:_R_RA, :]
    rat = pt[_R_RA:_R_END, :]
    pre_t = jnp.dot(wat_ref[...], rat, precision=HIGHEST, preferred_element_type=F32) + bat_ref[...]
    lat_ref[...] = _log_sigmoid(pre_t) * (1.0 / GLA_TAU)


def _in_projection(x_c, x_l, mod3, w_tok, w_tr, qn, kn, cos_t, sa_t, sb_t, cos_tr, sin_tr, seg, wa, ba,
                   wat, bat, n_seq_tiles):
    n_ctx = x_c.shape[0]
    n = n_ctx + x_l.shape[0]
    tb = TOK_TILE
    n_ctx_tiles = n_ctx // tb
    n_tiles = n // tb
    n_rope_blocks = cos_t.shape[0] // tb - 1

    def mod_idx(i):
        return (jnp.where(i < n_ctx_tiles, 0, 1 + (i - n_ctx_tiles) // n_seq_tiles), 0, 0)

    def rope_blk(i):
        return jnp.where(i < n_ctx_tiles, n_rope_blocks, (i - n_ctx_tiles) % n_seq_tiles)

    def rope_idx(i):
        return (rope_blk(i), 0)

    def ctx_idx(i):
        return (jnp.minimum(i, n_ctx_tiles - 1), 0)

    tok = lambda w: pl.BlockSpec((tb, w), lambda i: (i, 0))
    full = lambda a: pl.BlockSpec(a.shape, lambda i: (0,) * a.ndim)
    tr = lambda r: pl.BlockSpec((r, tb), lambda i: (0, i))
    rope_tr = pl.BlockSpec((HEAD_DIM, tb), lambda i: (0, rope_blk(i)))
    out_shapes = (
        jax.ShapeDtypeStruct((ATT_WIDTH, n), BF16),
        jax.ShapeDtypeStruct((n, 2 * LANES), BF16),
        jax.ShapeDtypeStruct((KV_HEADS * HEAD_DIM, n), BF16),
        jax.ShapeDtypeStruct((n_ctx, LANES), F32),
        jax.ShapeDtypeStruct((n_ctx, LANES), F32),
        jax.ShapeDtypeStruct((n, GLA_KW), F32),
        jax.ShapeDtypeStruct((n, GLA_WIDTH), BF16),
        jax.ShapeDtypeStruct((n, GLA_WIDTH), BF16),
        jax.ShapeDtypeStruct((n, 2 * GLA_KW), F32),
        jax.ShapeDtypeStruct((GLA_KW, n), F32),
        jax.ShapeDtypeStruct((2 * GLA_KW, n), F32),
    )
    out_specs = (tr(ATT_WIDTH), tok(2 * LANES), tr(KV_HEADS * HEAD_DIM),
                 pl.BlockSpec((tb, LANES), ctx_idx), pl.BlockSpec((tb, LANES), ctx_idx),
                 tok(GLA_KW), tok(GLA_WIDTH), tok(GLA_WIDTH), tok(2 * GLA_KW),
                 tr(GLA_KW), tr(2 * GLA_KW))
    in_specs = [pl.BlockSpec((tb, D_MODEL), ctx_idx),
                pl.BlockSpec((tb, D_MODEL), lambda i: (jnp.maximum(i - n_ctx_tiles, 0), 0)),
                pl.BlockSpec((None, 1, mod3.shape[2]), mod_idx),
                full(w_tok), full(w_tr), full(qn), full(kn),
                pl.BlockSpec((tb, LANES), rope_idx), pl.BlockSpec((tb, LANES), rope_idx),
                pl.BlockSpec((tb, LANES), rope_idx), rope_tr, rope_tr,
                full(seg), full(wa), full(ba), full(wat), full(bat)]
    return pl.pallas_call(
        functools.partial(_inproj_kernel, n_ctx_tiles=n_ctx_tiles),
        grid=(n_tiles,), in_specs=in_specs, out_specs=out_specs, out_shape=out_shapes,
        compiler_params=_cparams(1), name="in_projection",
    )(x_c, x_l, mod3, w_tok, w_tr, qn, kn, cos_t, sa_t, sb_t, cos_tr, sin_tr, seg, wa, ba, wat, bat)


def _attention_kernel(*refs, n_kv_parts):
    qt_ref = refs[0]
    k_refs = refs[1:1 + n_kv_parts]
    vt_refs = refs[1 + n_kv_parts:1 + 2 * n_kv_parts]
    o_ref = refs[1 + 2 * n_kv_parts]
    tq = qt_ref.shape[1]
    group = N_HEADS // KV_HEADS
    for kv in range(KV_HEADS):
        heads = range(kv * group, (kv + 1) * group)
        q_grp = jnp.concatenate([qt_ref[h * HEAD_DIM:(h + 1) * HEAD_DIM, :] for h in heads], axis=1)
        rhs = jnp.concatenate([q_grp, jnp.zeros_like(q_grp)], axis=0)
        s = [jnp.dot(k[:, kv * LANES:(kv + 1) * LANES], rhs, preferred_element_type=F32)
             for k in k_refs]
        mx = functools.reduce(jnp.maximum, [jnp.max(x, axis=0, keepdims=True) for x in s])
        pr = [jnp.exp2(x - mx) for x in s]
        den = functools.reduce(jnp.add, [jnp.sum(x, axis=0, keepdims=True) for x in pr])
        acc = functools.reduce(jnp.add, [
            jnp.dot(vt[kv * HEAD_DIM:(kv + 1) * HEAD_DIM, :], x.astype(BF16),
                    preferred_element_type=F32) for x, vt in zip(pr, vt_refs)])
        out = (acc / den).astype(BF16)
        for j, h in enumerate(heads):
            o_ref[h * HEAD_DIM:(h + 1) * HEAD_DIM, :] = out[:, j * tq:(j + 1) * tq]


def _attention(qt, k, vt, extra_kv, row0, n_batch, seq):
    tq = ATT_TQ
    n_q = seq // tq
    q_blk0 = row0 // tq
    kv_blk0 = row0 // seq
    in_specs = [pl.BlockSpec((ATT_WIDTH, tq), lambda b, i: (0, q_blk0 + b * n_q + i))]
    k_spec = pl.BlockSpec((seq, 2 * LANES), lambda b, i: (kv_blk0 + b, 0))
    vt_spec = pl.BlockSpec((KV_HEADS * HEAD_DIM, seq), lambda b, i: (0, kv_blk0 + b))
    args_k, args_v, specs_k, specs_v = [k], [vt], [k_spec], [vt_spec]
    if extra_kv is not None:
        ck, cvt = extra_kv
        args_k.append(ck)
        args_v.append(cvt)
        specs_k.append(pl.BlockSpec((None, ck.shape[1], 2 * LANES), lambda b, i: (b, 0, 0)))
        specs_v.append(pl.BlockSpec((None, KV_HEADS * HEAD_DIM, cvt.shape[2]), lambda b, i: (b, 0, 0)))
    return pl.pallas_call(
        functools.partial(_attention_kernel, n_kv_parts=len(args_k)),
        grid=(n_batch, n_q),
        in_specs=in_specs + specs_k + specs_v,
        out_specs=pl.BlockSpec((ATT_WIDTH, tq), lambda b, i: (0, b * n_q + i)),
        out_shape=jax.ShapeDtypeStruct((ATT_WIDTH, n_batch * seq), BF16),
        compiler_params=_cparams(2), name="attention",
    )(qt, *args_k, *args_v)


def _gla_constants():
    c = GLA_CHUNK
    idx = np.arange(c)
    q_mats, k_mats, masks, levels_of = [], [], [], []
    for li, (s, p) in enumerate(GLA_LEVELS):
        start = (idx // s) * s
        end = start + s - 1
        k_mats.append(((idx[None, :] > idx[:, None]) & (idx[None, :] <= end[:, None])))
        for d in range(p // s - 1):
            lo = np.maximum(start - d * s, 0)
            q_mats.append((idx[None, :] >= lo[:, None]) & (idx[None, :] <= idx[:, None]))
            masks.append((idx[:, None] // p == idx[None, :] // p)
                         & (idx[:, None] // s - idx[None, :] // s - 1 == d))
            levels_of.append(li)
    masks.append(np.eye(c, dtype=bool))
    levels_of.append(len(GLA_LEVELS) - 1)
    q_mats.append(idx[None, :] <= idx[:, None])
    k_mats = k_mats[:-1]
    k_mats.append(idx[None, :] > idx[:, None])
    k_mats.append(np.ones((c, c), bool))
    out = {}
    for name, flip in (("f", False), ("b", True)):
        f = (lambda a: a[::-1, ::-1]) if flip else (lambda a: a)
        lq = np.concatenate([f(a) for a in q_mats], axis=0).astype(np.float32)
        lkt = np.concatenate([f(a).T for a in k_mats], axis=1).astype(np.float32)
        mk = np.stack([np.tile(f(a), (1, GLA_HEADS)) for a in masks]).astype(np.float32)
        out[name] = (np.concatenate([lq, lq], axis=1), np.concatenate([lkt, lkt], axis=0), mk)
    return out, tuple(levels_of)


def _gla_direction(q, g, gkt, gt, v, lq2, lkt2, masks_ref, bd, vbd, s_ref, levels_of):
    c = GLA_CHUNK
    n_var = len(levels_of)
    n_lev = len(GLA_LEVELS)
    g_hi = g.astype(BF16)
    g_lo = (g - g_hi.astype(F32)).astype(BF16)
    fq = jnp.dot(lq2, jnp.concatenate([g_hi, g_lo], axis=0), preferred_element_type=F32)
    gt_hi = gt.astype(BF16)
    gt_lo = (gt - gt_hi.astype(F32)).astype(BF16)
    fk = jnp.dot(jnp.concatenate([gt_hi, gt_lo], axis=1), lkt2, preferred_element_type=F32)

    def key_factor(f):
        return gkt * jnp.exp(fk[:, f * c:(f + 1) * c])

    q_var = [(q * jnp.exp(fq[vi * c:(vi + 1) * c, :])).astype(BF16) for vi in range(n_var - 1)]
    q_var.append(q.astype(BF16))
    a = jnp.zeros((c, GLA_HEADS * c), F32)
    for li in range(n_lev):
        kt = (key_factor(li) if li < n_lev - 1 else gkt).astype(BF16)
        xt = jnp.concatenate([kt] * GLA_HEADS, axis=1) * bd
        vis = [vi for vi in range(n_var) if levels_of[vi] == li]
        res = jnp.dot(jnp.concatenate([q_var[vi] for vi in vis], axis=0), xt,
                      preferred_element_type=F32)
        for r, vi in enumerate(vis):
            a = a + masks_ref[vi] * res[r * c:(r + 1) * c, :]
    q_in = (q * jnp.exp(fq[(n_var - 1) * c:n_var * c, :])).astype(BF16)
    state = s_ref[...]
    v_bd = jnp.concatenate([v] * GLA_HEADS, axis=0) * vbd
    o = (jnp.dot(q_in, state.astype(BF16), preferred_element_type=F32)
         + jnp.dot(a.astype(BF16), v_bd, preferred_element_type=F32))
    k_out = key_factor(n_lev - 1).astype(BF16)
    e_tot = jnp.exp(fk[:, n_lev * c:(n_lev + 1) * c])
    upd = jnp.dot(k_out, v, preferred_element_type=F32)
    s_ref[...] = (state * jnp.concatenate([e_tot] * (GLA_WIDTH // c), axis=1)
                  + upd * bd.astype(F32))
    return o


def _gla_kernel(gq_f, la_f, gkt_f, lat_f, gv_f, gq_b, la_b, gkt_b, lat_b, gv_b,
                s0f_ref, s0b_ref, lq2f, lkt2f, mkf, lq2b, lkt2b, mkb, bd_ref, vbd_ref,
                of_ref, ob_ref, sf_ref, sb_ref, st_f, st_b, *, levels_of):
    n = pl.program_id(1)

    @pl.when(n == 0)
    def _():
        st_f[...] = jnp.zeros_like(st_f)
        st_b[...] = jnp.zeros_like(st_b)
        for h in range(GLA_HEADS):
            rows = slice(h * GLA_DK, (h + 1) * GLA_DK)
            cols = slice(h * GLA_DV, (h + 1) * GLA_DV)
            st_f[rows, cols] = s0f_ref[h]
            st_b[rows, cols] = s0b_ref[h]

    bd = bd_ref[...]
    vbd = vbd_ref[...]
    of_ref[...] = _gla_direction(gq_f[...], la_f[...], gkt_f[...], lat_f[...], gv_f[...],
                                 lq2f[...], lkt2f[...], mkf, bd, vbd, st_f, levels_of)
    ob_ref[...] = _gla_direction(gq_b[...], la_b[...], gkt_b[...], lat_b[...], gv_b[...],
                                 lq2b[...], lkt2b[...], mkb, bd, vbd, st_b, levels_of)

    @pl.when(n == pl.num_programs(1) - 1)
    def _():
        for h in range(GLA_HEADS):
            rows = slice(h * GLA_DK, (h + 1) * GLA_DK)
            cols = slice(h * GLA_DV, (h + 1) * GLA_DV)
            sf_ref[h] = st_f[rows, cols]
            sb_ref[h] = st_b[rows, cols]


def _gla(gq, la, gkt, lat, gv, s0f, s0b, consts, row0, n_batch, seq):
    (cf, cb), levels_of, bd, vbd = consts
    c = GLA_CHUNK
    nc = seq // c
    blk0 = row0 // c
    n_la_blocks_b = 1
    fwd = lambda b, n: blk0 + b * nc + n
    bwd = lambda b, n: blk0 + b * nc + (nc - 1 - n)

    def tok(w, which, col=0):
        return pl.BlockSpec((c, w), lambda b, n: (which(b, n), col))

    def tr(r, which, row=0):
        return pl.BlockSpec((r, c), lambda b, n: (row, which(b, n)))

    full = lambda a: pl.BlockSpec(a.shape, lambda b, n: (0,) * a.ndim)
    st_spec = pl.BlockSpec((None, GLA_HEADS, GLA_DK, GLA_DV), lambda b, n: (b, 0, 0, 0))
    in_specs = [tok(GLA_KW, fwd), tok(GLA_KW, fwd, 0), tr(GLA_KW, fwd), tr(GLA_KW, fwd, 0),
                tok(GLA_WIDTH, fwd),
                tok(GLA_KW, bwd), tok(GLA_KW, bwd, n_la_blocks_b), tr(GLA_KW, bwd),
                tr(GLA_KW, bwd, 1), tok(GLA_WIDTH, bwd),
                st_spec, st_spec,
                full(cf[0]), full(cf[1]), full(cf[2]), full(cb[0]), full(cb[1]), full(cb[2]),
                full(bd), full(vbd)]
    out_specs = (pl.BlockSpec((c, GLA_WIDTH), lambda b, n: (b * nc + n, 0)),
                 pl.BlockSpec((c, GLA_WIDTH), lambda b, n: (b * nc + (nc - 1 - n), 0)),
                 st_spec, st_spec)
    out_shape = (jax.ShapeDtypeStruct((n_batch * seq, GLA_WIDTH), F32),
                 jax.ShapeDtypeStruct((n_batch * seq, GLA_WIDTH), F32),
                 jax.ShapeDtypeStruct((n_batch, GLA_HEADS, GLA_DK, GLA_DV), F32),
                 jax.ShapeDtypeStruct((n_batch, GLA_HEADS, GLA_DK, GLA_DV), F32))
    return pl.pallas_call(
        functools.partial(_gla_kernel, levels_of=levels_of),
        grid=(n_batch, nc), in_specs=in_specs, out_specs=out_specs, out_shape=out_shape,
        scratch_shapes=[pltpu.VMEM((GLA_KW, GLA_WIDTH), F32), pltpu.VMEM((GLA_KW, GLA_WIDTH), F32)],
        compiler_params=_cparams(2), name="gla",
    )(gq, la, gkt, lat, gv, gq, la, gkt, lat, gv, s0f, s0b,
      cf[0], cf[1], cf[2], cb[0], cb[1], cb[2], bd, vbd)


def _outproj_kernel(attc_ref, attl_ref, ofc_ref, ofl_ref, obc_ref, obl_ref, gg_ref, xc_ref, xl_ref,
                    mod_ref, wo_ref, gn_ref, l1g_ref, l1b_ref, wrt_ref, sw13_ref, sw2_ref,
                    base_ref, u2_ref, lg_ref, *, n_ctx_tiles):
    is_ctx = pl.program_id(0) < n_ctx_tiles
    pick = lambda a_ref, b_ref: jnp.where(is_ctx, a_ref[...], b_ref[...])
    m = mod_ref[...]
    gate1 = m[:, 2 * D_MODEL:3 * D_MODEL]
    shift2 = m[:, 3 * D_MODEL:4 * D_MODEL]
    scale2 = m[:, 4 * D_MODEL:5 * D_MODEL]
    gate2 = m[:, 5 * D_MODEL:6 * D_MODEL]
    og = pick(ofc_ref, ofl_ref) + pick(obc_ref, obl_ref)
    gg = gg_ref[...].astype(F32)
    parts = []
    for h in range(GLA_HEADS):
        blk = og[:, h * GLA_DV:(h + 1) * GLA_DV]
        ms = jnp.mean(blk * blk, axis=-1, keepdims=True)
        nb = blk * lax.rsqrt(ms + EPS) * gn_ref[...]
        parts.append((nb * _silu(gg[:, h * GLA_DV:(h + 1) * GLA_DV])).astype(BF16))
    att_t = pick(attc_ref, attl_ref)
    hmix = (lax.dot_general(att_t, wo_ref[0:ATT_WIDTH, :], (((0,), (0,)), ((), ())),
                            preferred_element_type=F32)
            + jnp.dot(jnp.concatenate(parts, axis=1), wo_ref[ATT_WIDTH:, :],
                      preferred_element_type=F32))
    x1 = _layer_norm(ALPHA * pick(xc_ref, xl_ref) + gate1 * hmix, l1g_ref[...], l1b_ref[...])
    u2 = x1 * (1.0 + scale2) + shift2
    u2b = u2.astype(BF16)
    lg_ref[...] = lax.dot_general(wrt_ref[...], u2b, (((1,), (1,)), ((), ())),
                                  preferred_element_type=F32)
    ab = jnp.dot(u2b, sw13_ref[...], preferred_element_type=F32)
    hid = (_silu(ab[:, 0:SHARED_FF]) * ab[:, SHARED_FF:2 * SHARED_FF]).astype(BF16)
    shared = jnp.dot(hid, sw2_ref[...], preferred_element_type=F32)
    base_ref[...] = ALPHA * x1 + gate2 * shared
    _pack_rows(u2_ref, u2)


def _out_projection(att_c, att_l, of_c, of_l, ob_c, ob_l, gg, x_c, x_l, mod3, wo, gn, l1g, l1b, wrt,
                    sw13, sw2, n_seq_tiles):
    n_ctx = x_c.shape[0]
    n = n_ctx + x_l.shape[0]
    tb = TOK_TILE
    n_ctx_tiles = n_ctx // tb

    def mod_idx(i):
        return (jnp.where(i < n_ctx_tiles, 0, 1 + (i - n_ctx_tiles) // n_seq_tiles), 0, 0)

    ctx_blk = lambda i: jnp.minimum(i, n_ctx_tiles - 1)
    lat_blk = lambda i: jnp.maximum(i - n_ctx_tiles, 0)
    tok = lambda w: pl.BlockSpec((tb, w), lambda i: (i, 0))
    tok_c = lambda w: pl.BlockSpec((tb, w), lambda i: (ctx_blk(i), 0))
    tok_l = lambda w: pl.BlockSpec((tb, w), lambda i: (lat_blk(i), 0))
    full = lambda a: pl.BlockSpec(a.shape, lambda i: (0,) * a.ndim)
    return pl.pallas_call(
        functools.partial(_outproj_kernel, n_ctx_tiles=n_ctx_tiles),
        grid=(n // tb,),
        in_specs=[pl.BlockSpec((ATT_WIDTH, tb), lambda i: (0, ctx_blk(i))),
                  pl.BlockSpec((ATT_WIDTH, tb), lambda i: (0, lat_blk(i))),
                  tok_c(GLA_WIDTH), tok_l(GLA_WIDTH), tok_c(GLA_WIDTH), tok_l(GLA_WIDTH),
                  tok(GLA_WIDTH), tok_c(D_MODEL), tok_l(D_MODEL),
                  pl.BlockSpec((None, 1, mod3.shape[2]), mod_idx),
                  full(wo), full(gn), full(l1g), full(l1b), full(wrt), full(sw13), full(sw2)],
        out_specs=(tok(D_MODEL),
                   pl.BlockSpec((tb * PACK_CHUNKS, LANES), lambda i: (i, 0)),
                   pl.BlockSpec((N_EXPERTS, tb), lambda i: (0, i))),
        out_shape=(jax.ShapeDtypeStruct((n, D_MODEL), F32),
                   jax.ShapeDtypeStruct((n * PACK_CHUNKS, LANES), U32),
                   jax.ShapeDtypeStruct((N_EXPERTS, n), F32)),
        compiler_params=_cparams(1), name="out_projection",
    )(att_c, att_l, of_c, of_l, ob_c, ob_l, gg, x_c, x_l, mod3, wo, gn, l1g, l1b, wrt, sw13, sw2)


def _route_kernel(lg_ref, bias_ref, upper_ref, idx_ref, w_ref, pos_ref, cnt_ref, run_ref):
    i = pl.program_id(0)

    @pl.when(i == 0)
    def _():
        run_ref[...] = jnp.zeros_like(run_ref)

    s = jax.nn.sigmoid(lg_ref[...])
    work = s + bias_ref[...]
    rows = lax.broadcasted_iota(I32, s.shape, 0).astype(F32)
    sel = jnp.zeros(s.shape, F32)
    idxs, vals = [], []
    for _ in range(TOP_K):
        mx = jnp.max(work, axis=0, keepdims=True)
        idx = jnp.min(jnp.where(work == mx, rows, float(N_EXPERTS)), axis=0, keepdims=True)
        hit = rows == idx
        vals.append(jnp.sum(jnp.where(hit, s, 0.0), axis=0, keepdims=True))
        idxs.append(idx)
        sel = jnp.where(hit, 1.0, sel)
        work = jnp.where(hit, -jnp.inf, work)
    den = functools.reduce(jnp.add, vals)
    rank = jnp.dot(sel.astype(BF16), upper_ref[...], preferred_element_type=F32) + run_ref[:, 0:1]
    for k in range(TOP_K):
        idx_ref[k:k + 1, :] = idxs[k].astype(I32)
        w_ref[k:k + 1, :] = vals[k] / den * ROUTED_SCALE
        pos_ref[k:k + 1, :] = jnp.sum(jnp.where(rows == idxs[k], rank, 0.0), axis=0,
                                      keepdims=True).astype(I32)
    run_ref[...] = run_ref[...] + jnp.sum(sel, axis=1, keepdims=True)
    cnt_ref[...] = run_ref[...]


def _route(logits_t, bias_col, upper):
    n = logits_t.shape[1]
    tt = TOK_TILE
    row = lambda dt: jax.ShapeDtypeStruct((TOP_K, n), dt)
    blk = pl.BlockSpec((TOP_K, tt), lambda i: (0, i))
    return pl.pallas_call(
        _route_kernel,
        grid=(n // tt,),
        in_specs=[pl.BlockSpec((N_EXPERTS, tt), lambda i: (0, i)),
                  pl.BlockSpec((N_EXPERTS, 1), lambda i: (0, 0)),
                  pl.BlockSpec((tt, tt), lambda i: (0, 0))],
        out_specs=(blk, blk, blk, pl.BlockSpec((N_EXPERTS, LANES), lambda i: (0, 0))),
        out_shape=(row(I32), row(F32), row(I32), jax.ShapeDtypeStruct((N_EXPERTS, LANES), F32)),
        scratch_shapes=[pltpu.VMEM((N_EXPERTS, LANES), F32)],
        compiler_params=_cparams(1), name="route",
    )(logits_t, bias_col, upper)


def _dest_kernel(cnt_ref, lower_ref, idx_ref, pos_ref, dest_ref, bexp_ref, bval_ref, nused_ref):
    cnt = cnt_ref[...]
    nblk = jnp.floor((cnt + (MOE_ROWS - 1)) * (1.0 / MOE_ROWS))
    bstart = jnp.dot(lower_ref[...], nblk, precision=HIGHEST, preferred_element_type=F32)
    bend = bstart + nblk
    pstart = bstart[:, 0:1] * MOE_ROWS
    rows = lax.broadcasted_iota(I32, (N_EXPERTS, idx_ref.shape[1]), 0)
    for k in range(TOP_K):
        hit = rows == idx_ref[k:k + 1, :]
        dest_ref[k:k + 1, :] = (jnp.sum(jnp.where(hit, pstart, 0.0), axis=0, keepdims=True)
                                .astype(I32) + pos_ref[k:k + 1, :])

    @pl.when(pl.program_id(0) == 0)
    def _():
        nb = bexp_ref.shape[1]
        bid = lax.broadcasted_iota(I32, (N_EXPERTS, nb), 1).astype(F32)
        inside = jnp.logical_and(bid >= bstart[:, 0:1], bid < bend[:, 0:1])
        erow = lax.broadcasted_iota(I32, (N_EXPERTS, nb), 0).astype(F32)
        bexp_ref[...] = jnp.sum(jnp.where(inside, erow, 0.0), axis=0, keepdims=True).astype(I32)
        valid = jnp.clip(cnt[:, 0:1] - (bid - bstart[:, 0:1]) * MOE_ROWS, 0.0, float(MOE_ROWS))
        bval_ref[...] = jnp.sum(jnp.where(inside, valid, 0.0), axis=0, keepdims=True).astype(I32)
        nused_ref[...] = jnp.max(bend, axis=0, keepdims=True).astype(I32)


def _destinations(counts, lower, idx_t, pos_t, n_blocks_pad):
    n = idx_t.shape[1]
    tt = TOK_TILE
    blk = pl.BlockSpec((TOP_K, tt), lambda i: (0, i))
    one = lambda w: pl.BlockSpec((1, w), lambda i: (0, 0))
    return pl.pallas_call(
        _dest_kernel,
        grid=(n // tt,),
        in_specs=[pl.BlockSpec((N_EXPERTS, LANES), lambda i: (0, 0)),
                  pl.BlockSpec((N_EXPERTS, N_EXPERTS), lambda i: (0, 0)), blk, blk],
        out_specs=(blk, one(n_blocks_pad), one(n_blocks_pad), one(LANES)),
        out_shape=(jax.ShapeDtypeStruct((TOP_K, n), I32),
                   jax.ShapeDtypeStruct((1, n_blocks_pad), I32),
                   jax.ShapeDtypeStruct((1, n_blocks_pad), I32),
                   jax.ShapeDtypeStruct((1, LANES), I32)),
        compiler_params=_cparams(1), name="destinations",
    )(counts, lower, idx_t, pos_t)


PACK_CHUNKS = D_MODEL // (2 * LANES)
SRC_GROUP = 4
U32 = jnp.uint32


def _pack_rows(ref, x):
    bits = pltpu.bitcast(x.astype(BF16).astype(F32), U32)
    for s in range(PACK_CHUNKS):
        lo = bits[:, (2 * s) * LANES:(2 * s + 1) * LANES] >> 16
        hi = bits[:, (2 * s + 1) * LANES:(2 * s + 2) * LANES] & jnp.uint32(0xFFFF0000)
        ref[pl.ds(s, x.shape[0], stride=PACK_CHUNKS), :] = lo | hi


def _unpack_rows(ref, n_rows):
    parts = []
    for s in range(PACK_CHUNKS):
        w = ref[s * n_rows:(s + 1) * n_rows, :]
        parts.append(pltpu.bitcast(w << 16, F32))
        parts.append(pltpu.bitcast(w & jnp.uint32(0xFFFF0000), F32))
    return jnp.concatenate(parts, axis=1).astype(BF16)


def _moe_kernel(bexp_ref, bval_ref, nused_ref, u2p_hbm, src_hbm, w1_ref, w3_ref, w2_ref, yt_hbm,
                u2p_vmem, w13_s, w2_s, xbuf, ybuf, src_smem, sem_in, sem_src, sem_out,
                *, n_tokens):
    b = pl.program_id(0)
    n_used = nused_ref[0]
    br = MOE_ROWS
    grp = SRC_GROUP * br
    trash0 = n_tokens * TOP_K

    def src_copy(g):
        return pltpu.make_async_copy(src_hbm.at[pl.ds(g * grp, grp)],
                                     src_smem.at[pl.ds(lax.rem(g, 2) * grp, grp)], sem_src)

    def out_wait(slot):
        pltpu.make_async_copy(ybuf.at[pl.ds(slot * br * ROW_CHUNKS, br * ROW_CHUNKS)],
                              yt_hbm.at[pl.ds(0, br * ROW_CHUNKS)], sem_out.at[slot]).wait()

    @pl.when(b == 0)
    def _():
        cp = pltpu.make_async_copy(u2p_hbm, u2p_vmem, sem_in)
        cp.start()
        src_copy(0).start()
        cp.wait()

    @pl.when(b < n_used)
    def _():
        g = b // SRC_GROUP

        @pl.when(lax.rem(b, SRC_GROUP) == 0)
        def _():
            src_copy(g).wait()

            @pl.when((g + 1) * SRC_GROUP < n_used)
            def _():
                src_copy(g + 1).start()

        e = bexp_ref[b]
        prev = bexp_ref[jnp.maximum(b - 1, 0)]

        @pl.when(jnp.logical_or(b == 0, e != prev))
        def _():
            w13_s[:, 0:EXPERT_FF] = w1_ref[...].astype(BF16)
            w13_s[:, EXPERT_FF:2 * EXPERT_FF] = w3_ref[...].astype(BF16)
            w2_s[...] = w2_ref[...].astype(BF16)

        valid = bval_ref[b]
        sbase = lax.rem(g, 2) * grp + lax.rem(b, SRC_GROUP) * br
        slot = lax.rem(b, 2)

        def gather(r, carry):
            tok = jnp.minimum(jnp.maximum(src_smem[sbase + r], 0) // TOP_K, n_tokens - 1)
            row = pl.multiple_of(tok * PACK_CHUNKS, PACK_CHUNKS)
            xbuf[pl.ds(r, PACK_CHUNKS, stride=br), :] = u2p_vmem[pl.ds(row, PACK_CHUNKS), :]
            return carry

        lax.fori_loop(0, br, gather, 0)
        x = _unpack_rows(xbuf, br)
        rows = lax.broadcasted_iota(I32, x.shape, 0)
        x = jnp.where(rows < valid, x, jnp.zeros_like(x))
        ab = jnp.dot(x, w13_s[...], preferred_element_type=F32)
        hid = (_silu(ab[:, 0:EXPERT_FF]) * ab[:, EXPERT_FF:2 * EXPERT_FF]).astype(BF16)
        y = jnp.dot(hid, w2_s[...], preferred_element_type=F32)

        @pl.when(b >= 2)
        def _():
            out_wait(slot)

        ybase = slot * br * ROW_CHUNKS
        for cidx in range(ROW_CHUNKS):
            ybuf[pl.ds(ybase + cidx, br, stride=ROW_CHUNKS), :] = y[:, cidx * LANES:(cidx + 1) * LANES]

        def scatter(r, carry):
            dst = jnp.where(r < valid, src_smem[sbase + r], trash0 + slot * br + r)
            pltpu.make_async_copy(
                ybuf.at[pl.ds(pl.multiple_of(ybase + r * ROW_CHUNKS, ROW_CHUNKS), ROW_CHUNKS)],
                yt_hbm.at[pl.ds(pl.multiple_of(dst * ROW_CHUNKS, ROW_CHUNKS), ROW_CHUNKS)],
                sem_out.at[slot]).start()
            return carry

        lax.fori_loop(0, br, scatter, 0)

        @pl.when(b == n_used - 1)
        def _():
            @pl.when(b >= 1)
            def _():
                out_wait(1 - slot)
            out_wait(slot)


def _moe_experts(bexp, bval, nused, u2p, row_src, w1, w3, w2, n_blocks, n_tokens):
    br = MOE_ROWS

    def w_idx(b, bexp, bval, nused):
        return (bexp[jnp.minimum(b, nused[0] - 1)], 0, 0)

    grid_spec = pltpu.PrefetchScalarGridSpec(
        num_scalar_prefetch=3, grid=(n_blocks,),
        in_specs=[pl.BlockSpec(memory_space=pl.ANY), pl.BlockSpec(memory_space=pl.ANY),
                  pl.BlockSpec((None, D_MODEL, EXPERT_FF), w_idx),
                  pl.BlockSpec((None, D_MODEL, EXPERT_FF), w_idx),
                  pl.BlockSpec((None, EXPERT_FF, D_MODEL), w_idx)],
        out_specs=pl.BlockSpec(memory_space=pl.ANY),
        scratch_shapes=[pltpu.VMEM(u2p.shape, U32),
                        pltpu.VMEM((D_MODEL, 2 * EXPERT_FF), BF16),
                        pltpu.VMEM((EXPERT_FF, D_MODEL), BF16),
                        pltpu.VMEM((PACK_CHUNKS * br, LANES), U32),
                        pltpu.VMEM((2 * br * ROW_CHUNKS, LANES), F32),
                        pltpu.SMEM((2 * SRC_GROUP * br,), I32),
                        pltpu.SemaphoreType.DMA, pltpu.SemaphoreType.DMA,
                        pltpu.SemaphoreType.DMA((2,))])
    n_out_tiles = n_tokens * TOP_K + 2 * br
    return pl.pallas_call(
        functools.partial(_moe_kernel, n_tokens=n_tokens), grid_spec=grid_spec,
        out_shape=jax.ShapeDtypeStruct((n_out_tiles * ROW_CHUNKS, LANES), F32),
        compiler_params=pltpu.CompilerParams(dimension_semantics=("arbitrary",),
                                             vmem_limit_bytes=MOE_VMEM_LIMIT),
        name="moe_experts",
    )(bexp, bval, nused, u2p, row_src, w1, w3, w2)


def _combine_kernel(w_hbm, yt_ref, base_ref, mod_ref, g_ref, b_ref, yc_ref, yl_ref,
                    w_smem, acc_buf, sem_w, *, n_ctx_tiles):
    i = pl.program_id(0)
    n_steps = pl.num_programs(0)
    n_tok = acc_buf.shape[0] // ROW_CHUNKS
    n_idx = n_tok * TOP_K

    def w_copy(tile):
        return pltpu.make_async_copy(w_hbm.at[pl.ds(tile * n_idx, n_idx)],
                                     w_smem.at[pl.ds(lax.rem(tile, 2) * n_idx, n_idx)], sem_w)

    @pl.when(i == 0)
    def _():
        w_copy(i).start()

    w_copy(i).wait()

    @pl.when(i + 1 < n_steps)
    def _():
        w_copy(i + 1).start()

    wbase = lax.rem(i, 2) * n_idx

    def reduce_token(t, carry):
        acc = None
        for k in range(TOP_K):
            j = t * TOP_K + k
            row = pl.multiple_of(j * ROW_CHUNKS, ROW_CHUNKS)
            term = w_smem[wbase + j] * yt_ref[pl.ds(row, ROW_CHUNKS), :]
            acc = term if acc is None else acc + term
        acc_buf[pl.ds(pl.multiple_of(t * ROW_CHUNKS, ROW_CHUNKS), ROW_CHUNKS), :] = acc
        return carry

    lax.fori_loop(0, n_tok, reduce_token, 0)
    moe = _load_row_tiles(acc_buf, n_tok)
    gate2 = mod_ref[:, 5 * D_MODEL:6 * D_MODEL]
    y = _layer_norm(base_ref[...] + gate2 * moe, g_ref[...], b_ref[...])

    @pl.when(i < n_ctx_tiles)
    def _():
        yc_ref[...] = y

    @pl.when(i >= n_ctx_tiles)
    def _():
        yl_ref[...] = y


def _combine(w_flat, yt, base, mod3, l2g, l2b, n_ctx, seq_tokens):
    n = base.shape[0]
    tc = COMB_TILE
    n_ctx_tiles = n_ctx // tc
    n_seq_tiles = seq_tokens // tc

    def mod_idx(i):
        return (jnp.where(i < n_ctx_tiles, 0, 1 + (i - n_ctx_tiles) // n_seq_tiles), 0, 0)

    full = lambda a: pl.BlockSpec(a.shape, lambda i: (0,) * a.ndim)
    return pl.pallas_call(
        functools.partial(_combine_kernel, n_ctx_tiles=n_ctx_tiles),
        grid=(n // tc,),
        in_specs=[pl.BlockSpec(memory_space=pl.ANY),
                  pl.BlockSpec((tc * TOP_K * ROW_CHUNKS, LANES), lambda i: (i, 0)),
                  pl.BlockSpec((tc, D_MODEL), lambda i: (i, 0)),
                  pl.BlockSpec((None, 1, mod3.shape[2]), mod_idx), full(l2g), full(l2b)],
        out_specs=(pl.BlockSpec((tc, D_MODEL), lambda i: (jnp.minimum(i, n_ctx_tiles - 1), 0)),
                   pl.BlockSpec((tc, D_MODEL), lambda i: (jnp.maximum(i - n_ctx_tiles, 0), 0))),
        out_shape=(jax.ShapeDtypeStruct((n_ctx, D_MODEL), F32),
                   jax.ShapeDtypeStruct((n - n_ctx, D_MODEL), F32)),
        scratch_shapes=[pltpu.SMEM((2 * tc * TOP_K,), F32),
                        pltpu.VMEM((tc * ROW_CHUNKS, LANES), F32),
                        pltpu.SemaphoreType.DMA],
        compiler_params=_cparams(1), name="combine",
    )(w_flat, yt, base, mod3, l2g, l2b)


def _rope_tables(n_tok, tile):
    rows = n_tok // GRID_W
    row_idx = jnp.repeat(jnp.arange(rows, dtype=F32), GRID_W)
    col_idx = jnp.tile(jnp.arange(GRID_W, dtype=F32), rows)
    inv_freq = 1.0 / (ROPE_THETA ** (jnp.arange(0, ROPE_AXIS_DIM, 2, dtype=F32) / ROPE_AXIS_DIM))
    ang_r = row_idx[:, None] * inv_freq[None, :]
    ang_c = col_idx[:, None] * inv_freq[None, :]
    ang = jnp.concatenate([ang_r, ang_r, ang_c, ang_c], axis=-1)
    cos, sin = jnp.cos(ang), jnp.sin(ang)
    quarter = (jnp.arange(HEAD_DIM) // (ROPE_AXIS_DIM // 2)) % 2
    sin_a = jnp.where(quarter == 0, -sin, 0.0)
    sin_b = jnp.where(quarter == 1, sin, 0.0)
    rep = LANES // HEAD_DIM
    ident = lambda v: jnp.full((tile, LANES), v, F32)
    cos_t = jnp.concatenate([jnp.tile(cos, (1, rep)), ident(1.0)], axis=0)
    sa_t = jnp.concatenate([jnp.tile(sin_a, (1, rep)), ident(0.0)], axis=0)
    sb_t = jnp.concatenate([jnp.tile(sin_b, (1, rep)), ident(0.0)], axis=0)
    ident_tr = lambda v: jnp.full((HEAD_DIM, tile), v, F32)
    cos_tr = jnp.concatenate([cos.T, ident_tr(1.0)], axis=1)
    sin_tr = jnp.concatenate([sin.T, ident_tr(0.0)], axis=1)
    return cos_t, sa_t, sb_t, cos_tr, sin_tr


def _dup_heads(a):
    parts = []
    for h in range(KV_HEADS):
        blk = a[..., h * HEAD_DIM:(h + 1) * HEAD_DIM]
        parts += [blk] * (LANES // HEAD_DIM)
    return jnp.concatenate(parts, axis=-1)


def kernel(x_prompt, x_sample, cache_k, cache_v, state_gla_fwd, state_gla_bwd, c, c_ctx, w_ada, b_ada, w_in, q_norm, k_norm, gla_wa_fwd, gla_ba_fwd, gla_wa_bwd, gla_ba_bwd, gla_norm, w_out, ln1_g, ln1_b, ln2_g, ln2_b, w_router, router_bias, exp_w1, exp_w3, exp_w2, sh_w1, sh_w3, sh_w2):
    n_ctx_b, ctx_seq, _ = x_prompt.shape
    n_lat_b, lat_seq, _ = x_sample.shape
    n_ctx = n_ctx_b * ctx_seq
    n_lat = n_lat_b * lat_seq
    n = n_ctx + n_lat
    l = 0

    x_c = x_prompt.reshape(n_ctx, D_MODEL)
    x_l = x_sample.reshape(n_lat, D_MODEL)

    c_rows = jnp.zeros((SUBLANES, D_MODEL), F32).at[0].set(c_ctx).at[1:1 + n_lat_b].set(c)
    mod = _modulation(c_rows, w_ada[l], b_ada[l][None, :])
    mod3 = mod.reshape(SUBLANES, 1, 6 * D_MODEL)

    wi = w_in[l]
    o_q, o_k, o_v, o_gq, o_gk, o_gv, o_gg, o_rf, o_rb, o_end = np.cumsum(
        [0, ATT_WIDTH, KV_HEADS * HEAD_DIM, KV_HEADS * HEAD_DIM, GLA_KW, GLA_KW, GLA_WIDTH, GLA_WIDTH,
         GLA_GATE_RANK, GLA_GATE_RANK])
    w_tok = jnp.concatenate([
        _dup_heads(wi[:, o_k:o_v]), wi[:, o_v:o_gq], wi[:, o_gq:o_gk],
        wi[:, o_gv:o_gg], wi[:, o_gg:o_rf], wi[:, o_rf:o_end],
        jnp.zeros((D_MODEL, LANES - 2 * GLA_GATE_RANK), F32)], axis=1).astype(BF16)
    w_tr = jnp.concatenate([wi[:, o_q:o_k], wi[:, o_v:o_gq], wi[:, o_gk:o_gv], wi[:, o_rf:o_end]],
                           axis=1).T.astype(BF16)
    rep = LANES // HEAD_DIM
    qn = q_norm[l][:, None]
    kn = jnp.tile(k_norm[l], rep)[None, :]
    seg = jnp.asarray(np.kron(np.eye(rep), np.ones((HEAD_DIM, HEAD_DIM))), BF16)
    wa = jnp.zeros((LANES, 2 * GLA_KW), F32)
    wa = wa.at[0:GLA_GATE_RANK, 0:GLA_KW].set(gla_wa_fwd[l])
    wa = wa.at[GLA_GATE_RANK:2 * GLA_GATE_RANK, GLA_KW:].set(gla_wa_bwd[l])
    ba = jnp.concatenate([gla_ba_fwd[l], gla_ba_bwd[l]])[None, :]
    wat = wa[0:2 * GLA_GATE_RANK, :].T
    bat = ba.T
    cos_t, sa_t, sb_t, cos_tr, sin_tr = _rope_tables(lat_seq, TOK_TILE)

    (qt, k_dup, vt, k32, v32, gq, gv, gg, la, gkt, lat) = _in_projection(
        x_c, x_l, mod3, w_tok, w_tr, qn, kn, cos_t, sa_t, sb_t, cos_tr, sin_tr, seg, wa, ba, wat, bat,
        lat_seq // TOK_TILE)

    ck = _dup_heads(cache_k[:, l].reshape(n_lat_b, -1, KV_HEADS * HEAD_DIM)).astype(BF16)
    cvt = cache_v[:, l].reshape(n_lat_b, -1, KV_HEADS * HEAD_DIM).transpose(0, 2, 1).astype(BF16)
    att_c = _attention(qt, k_dup, vt, None, 0, n_ctx_b, ctx_seq)
    att_l = _attention(qt, k_dup, vt, (ck, cvt), n_ctx, n_lat_b, lat_seq)

    gconst, levels_of = _gla_constants()
    to_dev = lambda t: (jnp.asarray(t[0], BF16), jnp.asarray(t[1], BF16), jnp.asarray(t[2], F32))
    bd = jnp.asarray(np.kron(np.eye(GLA_HEADS), np.ones((GLA_DK, GLA_DV))), BF16)
    vbd = jnp.asarray(np.kron(np.eye(GLA_HEADS), np.ones((GLA_CHUNK, GLA_DV))), BF16)
    consts = ((to_dev(gconst["f"]), to_dev(gconst["b"])), levels_of, bd, vbd)
    s_zero = jnp.zeros((n_ctx_b, GLA_HEADS, GLA_DK, GLA_DV), F32)
    of_c, ob_c, sf_new, sb_new = _gla(gq, la, gkt, lat, gv, s_zero, s_zero, consts, 0, n_ctx_b, ctx_seq)
    of_l, ob_l, _, _ = _gla(gq, la, gkt, lat, gv, state_gla_fwd[:, l], state_gla_bwd[:, l], consts,
                            n_ctx, n_lat_b, lat_seq)

    sw13 = jnp.concatenate([sh_w1[l], sh_w3[l]], axis=1).astype(BF16)
    base, u2_rows, logits_t = _out_projection(
        att_c, att_l, of_c, of_l, ob_c, ob_l, gg, x_c, x_l, mod3, w_out[l].astype(BF16),
        gla_norm[l][None, :], ln1_g[l][None, :], ln1_b[l][None, :], w_router[l].T.astype(BF16), sw13,
        sh_w2[l].astype(BF16), lat_seq // TOK_TILE)

    upper = jnp.asarray(np.triu(np.ones((TOK_TILE, TOK_TILE)), 1), BF16)
    idx_t, w_t, pos_t, counts = _route(logits_t, router_bias[l][:, None], upper)
    n_blocks = n * TOP_K // MOE_ROWS + N_EXPERTS
    n_blocks_pad = -(-n_blocks // LANES) * LANES
    lower = jnp.asarray(np.tril(np.ones((N_EXPERTS, N_EXPERTS)), -1), F32)
    dest_t, bexp, bval, nused = _destinations(counts, lower, idx_t, pos_t, n_blocks_pad)
    dest_flat = dest_t.T.reshape(-1)
    w_flat = w_t.T.reshape(-1)

    row_src = jnp.zeros((n_blocks * MOE_ROWS,), I32).at[dest_flat].set(
        jnp.arange(n * TOP_K, dtype=I32))
    yt = _moe_experts(bexp.reshape(-1), bval.reshape(-1), nused.reshape(-1)[0:1], u2_rows, row_src,
                      exp_w1[l], exp_w3[l], exp_w2[l], n_blocks, n)
    y_c, y_l = _combine(w_flat, yt, base, mod3, ln2_g[l][None, :], ln2_b[l][None, :], n_ctx, lat_seq)

    y_prompt = y_c.reshape(n_ctx_b, ctx_seq, D_MODEL)
    y_sample = y_l.reshape(n_lat_b, lat_seq, D_MODEL)
    new_cache_k = k32.reshape(n_ctx_b, 1, ctx_seq, KV_HEADS, HEAD_DIM)
    new_cache_v = v32.reshape(n_ctx_b, 1, ctx_seq, KV_HEADS, HEAD_DIM)
    return (y_prompt, y_sample, new_cache_k, new_cache_v, sf_new[:, None], sb_new[:, None])
```

```python
import functools

import numpy as np
import jax
import jax.numpy as jnp
from jax import lax
from jax.experimental import pallas as pl
from jax.experimental.pallas import tpu as pltpu
from jax.experimental.pallas import tpu_sc as plsc

F32 = jnp.float32
BF16 = jnp.bfloat16
I32 = jnp.int32

D_MODEL = 1024
GRID_W = 64
HEAD_DIM = 64
N_HEADS = 8
KV_HEADS = 2
ATT_WIDTH = N_HEADS * HEAD_DIM
ATT_SCALE = HEAD_DIM ** -0.5
LOG2_E = 1.4426950408889634
ROPE_AXIS_DIM = HEAD_DIM // 2
ROPE_THETA = 10000.0
GLA_HEADS = 4
GLA_DK = 64
GLA_DV = 128
GLA_WIDTH = GLA_HEADS * GLA_DV
GLA_KW = GLA_HEADS * GLA_DK
GLA_GATE_RANK = 16
GLA_TAU = 16.0
N_EXPERTS = 256
TOP_K = 8
EXPERT_FF = 256
SHARED_FF = 256
ROUTED_SCALE = 2.5
DEPTH = 1
ALPHA = (2.0 * DEPTH) ** 0.25
EPS = 1e-6

LANES = 128
SUBLANES = 8
ROW_CHUNKS = D_MODEL // LANES
VMEM_LIMIT = 56 * 1024 * 1024

TOK_TILE = 512
ATT_TQ = 128
GLA_CHUNK = 128
GLA_LEVELS = ((32, 128), (8, 32), (2, 8), (1, 2))
MOE_ROWS = 256
MOE_VMEM_LIMIT = 62 * 1024 * 1024
COMB_TILE = 128
HIGHEST = lax.Precision.HIGHEST


def _cparams(n_axes):
    return pltpu.CompilerParams(dimension_semantics=("arbitrary",) * n_axes,
                                vmem_limit_bytes=VMEM_LIMIT)


def _silu(x):
    return x * jax.nn.sigmoid(x)


def _log_sigmoid(x):
    return jnp.minimum(x, 0.0) - jnp.log(1.0 + jnp.exp(-jnp.abs(x)))


def _load_row_tiles(ref, n_rows, row0=0):
    return jnp.concatenate(
        [ref[pl.ds(row0 * ROW_CHUNKS + cidx, n_rows, stride=ROW_CHUNKS), :] for cidx in range(ROW_CHUNKS)],
        axis=1)


def _store_row_tiles(ref, x):
    for cidx in range(ROW_CHUNKS):
        ref[pl.ds(cidx, x.shape[0], stride=ROW_CHUNKS), :] = x[:, cidx * LANES:(cidx + 1) * LANES]


def _layer_norm(z, g, b):
    mu = jnp.mean(z, axis=-1, keepdims=True)
    zc = z - mu
    var = jnp.mean(zc * zc, axis=-1, keepdims=True)
    return zc * lax.rsqrt(var + EPS) * g + b


def _mod_kernel(c_ref, w_ref, b_ref, o_ref):
    s = _silu(c_ref[...]).astype(BF16)
    o_ref[...] = jnp.dot(s, w_ref[...].astype(BF16), preferred_element_type=F32) + b_ref[...]


def _modulation(c_rows, w_ada, b_ada):
    n_cols = w_ada.shape[1]
    tn = 512
    return pl.pallas_call(
        _mod_kernel,
        grid=(n_cols // tn,),
        in_specs=[pl.BlockSpec((SUBLANES, D_MODEL), lambda j: (0, 0)),
                  pl.BlockSpec((D_MODEL, tn), lambda j: (0, j)),
                  pl.BlockSpec((1, tn), lambda j: (0, j))],
        out_specs=pl.BlockSpec((SUBLANES, tn), lambda j: (0, j)),
        out_shape=jax.ShapeDtypeStruct((SUBLANES, n_cols), F32),
        compiler_params=_cparams(1),
        name="modulation",
    )(c_rows, w_ada, b_ada)


_C_K = 0
_C_V = _C_K + 2 * LANES
_C_GQ = _C_V + KV_HEADS * HEAD_DIM
_C_GV = _C_GQ + GLA_KW
_C_GG = _C_GV + GLA_WIDTH
_C_RA = _C_GG + GLA_WIDTH
_C_END = _C_RA + LANES
_R_Q = 0
_R_V = _R_Q + ATT_WIDTH
_R_GK = _R_V + KV_HEADS * HEAD_DIM
_R_RA = _R_GK + GLA_KW
_R_END = _R_RA + 2 * GLA_GATE_RANK


def _inproj_kernel(xc_ref, xl_ref, mod_ref, w_ref, wt_ref, qn_ref, kn_ref, cos_ref, sa_ref, sb_ref,
                   cost_ref, sint_ref, seg_ref, wa_ref, ba_ref, wat_ref, bat_ref,
                   qt_ref, k_ref, vt_ref, k32_ref, v32_ref, gq_ref, gv_ref, gg_ref,
                   la_ref, gkt_ref, lat_ref, *, n_ctx_tiles):
    i = pl.program_id(0)
    m = mod_ref[...]
    shift1 = m[:, 0:D_MODEL]
    scale1 = m[:, D_MODEL:2 * D_MODEL]
    x = jnp.where(i < n_ctx_tiles, xc_ref[...], xl_ref[...])
    u = (x * (1.0 + scale1) + shift1).astype(BF16)

    cos = cos_ref[...]
    sin_a = sa_ref[...]
    sin_b = sb_ref[...]
    seg = seg_ref[...]
    lane = lax.broadcasted_iota(I32, (u.shape[0], LANES), 1)
    low = lane < HEAD_DIM

    def proj(c0, c1):
        return jnp.dot(u, w_ref[:, c0:c1], preferred_element_type=F32)

    def head_norm(blk, gain):
        ss = jnp.dot((blk * blk).astype(BF16), seg, preferred_element_type=F32) * (1.0 / HEAD_DIM)
        return blk * lax.rsqrt(ss + EPS) * gain

    def rope(blk):
        return (blk * cos + pltpu.roll(blk, LANES - ROPE_AXIS_DIM // 2, 1) * sin_a
                + pltpu.roll(blk, ROPE_AXIS_DIM // 2, 1) * sin_b)

    pk = proj(_C_K, _C_V)
    kn = [head_norm(pk[:, j * LANES:(j + 1) * LANES], kn_ref[...]) for j in range(KV_HEADS)]
    for j in range(KV_HEADS):
        k_ref[:, j * LANES:(j + 1) * LANES] = rope(kn[j]).astype(BF16)

    @pl.when(i < n_ctx_tiles)
    def _():
        k32_ref[...] = jnp.where(low, kn[0], kn[1])
        v32_ref[...] = proj(_C_V, _C_GQ)

    gq_ref[...] = proj(_C_GQ, _C_GV) * (GLA_DK ** -0.5)
    gv_ref[...] = proj(_C_GV, _C_GG).astype(BF16)
    gg_ref[...] = proj(_C_GG, _C_RA).astype(BF16)

    ra = proj(_C_RA, _C_END)
    pre = jnp.dot(ra, wa_ref[...], precision=HIGHEST, preferred_element_type=F32) + ba_ref[...]
    la_ref[...] = _log_sigmoid(pre) * (1.0 / GLA_TAU)

    pt = lax.dot_general(wt_ref[...], u, (((1,), (1,)), ((), ())), preferred_element_type=F32)
    cos_t = cost_ref[...]
    sin_t = sint_ref[...]
    quarter = ROPE_AXIS_DIM // 2
    for h in range(N_HEADS):
        blk = pt[_R_Q + h * HEAD_DIM:_R_Q + (h + 1) * HEAD_DIM, :]
        ms = jnp.mean(blk * blk, axis=0, keepdims=True)
        qn = blk * lax.rsqrt(ms + EPS) * qn_ref[...]
        rot = jnp.concatenate([-qn[quarter:2 * quarter], qn[0:quarter],
                               -qn[3 * quarter:4 * quarter], qn[2 * quarter:3 * quarter]], axis=0)
        qt_ref[h * HEAD_DIM:(h + 1) * HEAD_DIM, :] = (
            (qn * cos_t + rot * sin_t) * (ATT_SCALE * LOG2_E)).astype(BF16)
    vt_ref[...] = pt[_R_V:_R_GK, :].astype(BF16)
    gkt_ref[...] = pt[_R_GK:_R_RA, :]
    rat = pt[_R_RA:_R_END, :]
    pre_t = jnp.dot(wat_ref[...], rat, precision=HIGHEST, preferred_element_type=F32) + bat_ref[...]
    lat_ref[...] = _log_sigmoid(pre_t) * (1.0 / GLA_TAU)


def _in_projection(x_c, x_l, mod3, w_tok, w_tr, qn, kn, cos_t, sa_t, sb_t, cos_tr, sin_tr, seg, wa, ba,
                   wat, bat, n_seq_tiles):
    n_ctx = x_c.shape[0]
    n = n_ctx + x_l.shape[0]
    tb = TOK_TILE
    n_ctx_tiles = n_ctx // tb
    n_tiles = n // tb
    n_rope_blocks = cos_t.shape[0] // tb - 1

    def mod_idx(i):
        return (jnp.where(i < n_ctx_tiles, 0, 1 + (i - n_ctx_tiles) // n_seq_tiles), 0, 0)

    def rope_blk(i):
        return jnp.where(i < n_ctx_tiles, n_rope_blocks, (i - n_ctx_tiles) % n_seq_tiles)

    def rope_idx(i):
        return (rope_blk(i), 0)

    def ctx_idx(i):
        return (jnp.minimum(i, n_ctx_tiles - 1), 0)

    tok = lambda w: pl.BlockSpec((tb, w), lambda i: (i, 0))
    full = lambda a: pl.BlockSpec(a.shape, lambda i: (0,) * a.ndim)
    tr = lambda r: pl.BlockSpec((r, tb), lambda i: (0, i))
    rope_tr = pl.BlockSpec((HEAD_DIM, tb), lambda i: (0, rope_blk(i)))
    out_shapes = (
        jax.ShapeDtypeStruct((ATT_WIDTH, n), BF16),
        jax.ShapeDtypeStruct((n, 2 * LANES), BF16),
        jax.ShapeDtypeStruct((KV_HEADS * HEAD_DIM, n), BF16),
        jax.ShapeDtypeStruct((n_ctx, LANES), F32),
        jax.ShapeDtypeStruct((n_ctx, LANES), F32),
        jax.ShapeDtypeStruct((n, GLA_KW), F32),
        jax.ShapeDtypeStruct((n, GLA_WIDTH), BF16),
        jax.ShapeDtypeStruct((n, GLA_WIDTH), BF16),
        jax.ShapeDtypeStruct((n, 2 * GLA_KW), F32),
        jax.ShapeDtypeStruct((GLA_KW, n), F32),
        jax.ShapeDtypeStruct((2 * GLA_KW, n), F32),
    )
    out_specs = (tr(ATT_WIDTH), tok(2 * LANES), tr(KV_HEADS * HEAD_DIM),
                 pl.BlockSpec((tb, LANES), ctx_idx), pl.BlockSpec((tb, LANES), ctx_idx),
                 tok(GLA_KW), tok(GLA_WIDTH), tok(GLA_WIDTH), tok(2 * GLA_KW),
                 tr(GLA_KW), tr(2 * GLA_KW))
    in_specs = [pl.BlockSpec((tb, D_MODEL), ctx_idx),
                pl.BlockSpec((tb, D_MODEL), lambda i: (jnp.maximum(i - n_ctx_tiles, 0), 0)),
                pl.BlockSpec((None, 1, mod3.shape[2]), mod_idx),
                full(w_tok), full(w_tr), full(qn), full(kn),
                pl.BlockSpec((tb, LANES), rope_idx), pl.BlockSpec((tb, LANES), rope_idx),
                pl.BlockSpec((tb, LANES), rope_idx), rope_tr, rope_tr,
                full(seg), full(wa), full(ba), full(wat), full(bat)]
    return pl.pallas_call(
        functools.partial(_inproj_kernel, n_ctx_tiles=n_ctx_tiles),
        grid=(n_tiles,), in_specs=in_specs, out_specs=out_specs, out_shape=out_shapes,
        compiler_params=_cparams(1), name="in_projection",
    )(x_c, x_l, mod3, w_tok, w_tr, qn, kn, cos_t, sa_t, sb_t, cos_tr, sin_tr, seg, wa, ba, wat, bat)


def _attention_kernel(*refs, n_kv_parts):
    qt_ref = refs[0]
    k_refs = refs[1:1 + n_kv_parts]
    vt_refs = refs[1 + n_kv_parts:1 + 2 * n_kv_parts]
    o_ref = refs[1 + 2 * n_kv_parts]
    tq = qt_ref.shape[1]
    group = N_HEADS // KV_HEADS
    for kv in range(KV_HEADS):
        heads = range(kv * group, (kv + 1) * group)
        q_grp = jnp.concatenate([qt_ref[h * HEAD_DIM:(h + 1) * HEAD_DIM, :] for h in heads], axis=1)
        rhs = jnp.concatenate([q_grp, jnp.zeros_like(q_grp)], axis=0)
        s = [jnp.dot(k[:, kv * LANES:(kv + 1) * LANES], rhs, preferred_element_type=F32)
             for k in k_refs]
        mx = functools.reduce(jnp.maximum, [jnp.max(x, axis=0, keepdims=True) for x in s])
        pr = [jnp.exp2(x - mx) for x in s]
        den = functools.reduce(jnp.add, [jnp.sum(x, axis=0, keepdims=True) for x in pr])
        acc = functools.reduce(jnp.add, [
            jnp.dot(vt[kv * HEAD_DIM:(kv + 1) * HEAD_DIM, :], x.astype(BF16),
                    preferred_element_type=F32) for x, vt in zip(pr, vt_refs)])
        out = (acc / den).astype(BF16)
        for j, h in enumerate(heads):
            o_ref[h * HEAD_DIM:(h + 1) * HEAD_DIM, :] = out[:, j * tq:(j + 1) * tq]


def _attention(qt, k, vt, extra_kv, row0, n_batch, seq):
    tq = ATT_TQ
    n_q = seq // tq
    q_blk0 = row0 // tq
    kv_blk0 = row0 // seq
    in_specs = [pl.BlockSpec((ATT_WIDTH, tq), lambda b, i: (0, q_blk0 + b * n_q + i))]
    k_spec = pl.BlockSpec((seq, 2 * LANES), lambda b, i: (kv_blk0 + b, 0))
    vt_spec = pl.BlockSpec((KV_HEADS * HEAD_DIM, seq), lambda b, i: (0, kv_blk0 + b))
    args_k, args_v, specs_k, specs_v = [k], [vt], [k_spec], [vt_spec]
    if extra_kv is not None:
        ck, cvt = extra_kv
        args_k.append(ck)
        args_v.append(cvt)
        specs_k.append(pl.BlockSpec((None, ck.shape[1], 2 * LANES), lambda b, i: (b, 0, 0)))
        specs_v.append(pl.BlockSpec((None, KV_HEADS * HEAD_DIM, cvt.shape[2]), lambda b, i: (b, 0, 0)))
    return pl.pallas_call(
        functools.partial(_attention_kernel, n_kv_parts=len(args_k)),
        grid=(n_batch, n_q),
        in_specs=in_specs + specs_k + specs_v,
        out_specs=pl.BlockSpec((ATT_WIDTH, tq), lambda b, i: (0, b * n_q + i)),
        out_shape=jax.ShapeDtypeStruct((ATT_WIDTH, n_batch * seq), BF16),
        compiler_params=_cparams(2), name="attention",
    )(qt, *args_k, *args_v)


def _gla_constants():
    c = GLA_CHUNK
    idx = np.arange(c)
    q_mats, k_mats, masks, levels_of = [], [], [], []
    for li, (s, p) in enumerate(GLA_LEVELS):
        start = (idx // s) * s
        end = start + s - 1
        k_mats.append(((idx[None, :] > idx[:, None]) & (idx[None, :] <= end[:, None])))
        for d in range(p // s - 1):
            lo = np.maximum(start - d * s, 0)
            q_mats.append((idx[None, :] >= lo[:, None]) & (idx[None, :] <= idx[:, None]))
            masks.append((idx[:, None] // p == idx[None, :] // p)
                         & (idx[:, None] // s - idx[None, :] // s - 1 == d))
            levels_of.append(li)
    masks.append(np.eye(c, dtype=bool))
    levels_of.append(len(GLA_LEVELS) - 1)
    q_mats.append(idx[None, :] <= idx[:, None])
    k_mats = k_mats[:-1]
    k_mats.append(idx[None, :] > idx[:, None])
    k_mats.append(np.ones((c, c), bool))
    out = {}
    for name, flip in (("f", False), ("b", True)):
        f = (lambda a: a[::-1, ::-1]) if flip else (lambda a: a)
        lq = np.concatenate([f(a) for a in q_mats], axis=0).astype(np.float32)
        lkt = np.concatenate([f(a).T for a in k_mats], axis=1).astype(np.float32)
        mk = np.stack([np.tile(f(a), (1, GLA_HEADS)) for a in masks]).astype(np.float32)
        out[name] = (np.concatenate([lq, lq], axis=1), np.concatenate([lkt, lkt], axis=0), mk)
    return out, tuple(levels_of)


def _gla_direction(q, g, gkt, gt, v, lq2, lkt2, masks_ref, bd, vbd, s_ref, levels_of):
    c = GLA_CHUNK
    n_var = len(levels_of)
    n_lev = len(GLA_LEVELS)
    g_hi = g.astype(BF16)
    g_lo = (g - g_hi.astype(F32)).astype(BF16)
    fq = jnp.dot(lq2, jnp.concatenate([g_hi, g_lo], axis=0), preferred_element_type=F32)
    gt_hi = gt.astype(BF16)
    gt_lo = (gt - gt_hi.astype(F32)).astype(BF16)
    fk = jnp.dot(jnp.concatenate([gt_hi, gt_lo], axis=1), lkt2, preferred_element_type=F32)

    def key_factor(f):
        return gkt * jnp.exp(fk[:, f * c:(f + 1) * c])

    q_var = [(q * jnp.exp(fq[vi * c:(vi + 1) * c, :])).astype(BF16) for vi in range(n_var - 1)]
    q_var.append(q.astype(BF16))
    a = jnp.zeros((c, GLA_HEADS * c), F32)
    for li in range(n_lev):
        kt = (key_factor(li) if li < n_lev - 1 else gkt).astype(BF16)
        xt = jnp.concatenate([kt] * GLA_HEADS, axis=1) * bd
        vis = [vi for vi in range(n_var) if levels_of[vi] == li]
        res = jnp.dot(jnp.concatenate([q_var[vi] for vi in vis], axis=0), xt,
                      preferred_element_type=F32)
        for r, vi in enumerate(vis):
            a = a + masks_ref[vi] * res[r * c:(r + 1) * c, :]
    q_in = (q * jnp.exp(fq[(n_var - 1) * c:n_var * c, :])).astype(BF16)
    state = s_ref[...]
    v_bd = jnp.concatenate([v] * GLA_HEADS, axis=0) * vbd
    o = (jnp.dot(q_in, state.astype(BF16), preferred_element_type=F32)
         + jnp.dot(a.astype(BF16), v_bd, preferred_element_type=F32))
    k_out = key_factor(n_lev - 1).astype(BF16)
    e_tot = jnp.exp(fk[:, n_lev * c:(n_lev + 1) * c])
    upd = jnp.dot(k_out, v, preferred_element_type=F32)
    s_ref[...] = (state * jnp.concatenate([e_tot] * (GLA_WIDTH // c), axis=1)
                  + upd * bd.astype(F32))
    return o


def _gla_kernel(gq_f, la_f, gkt_f, lat_f, gv_f, gq_b, la_b, gkt_b, lat_b, gv_b,
                s0f_ref, s0b_ref, lq2f, lkt2f, mkf, lq2b, lkt2b, mkb, bd_ref, vbd_ref,
                of_ref, ob_ref, sf_ref, sb_ref, st_f, st_b, *, levels_of):
    n = pl.program_id(1)

    @pl.when(n == 0)
    def _():
        st_f[...] = jnp.zeros_like(st_f)
        st_b[...] = jnp.zeros_like(st_b)
        for h in range(GLA_HEADS):
            rows = slice(h * GLA_DK, (h + 1) * GLA_DK)
            cols = slice(h * GLA_DV, (h + 1) * GLA_DV)
            st_f[rows, cols] = s0f_ref[h]
            st_b[rows, cols] = s0b_ref[h]

    bd = bd_ref[...]
    vbd = vbd_ref[...]
    of_ref[...] = _gla_direction(gq_f[...], la_f[...], gkt_f[...], lat_f[...], gv_f[...],
                                 lq2f[...], lkt2f[...], mkf, bd, vbd, st_f, levels_of)
    ob_ref[...] = _gla_direction(gq_b[...], la_b[...], gkt_b[...], lat_b[...], gv_b[...],
                                 lq2b[...], lkt2b[...], mkb, bd, vbd, st_b, levels_of)

    @pl.when(n == pl.num_programs(1) - 1)
    def _():
        for h in range(GLA_HEADS):
            rows = slice(h * GLA_DK, (h + 1) * GLA_DK)
            cols = slice(h * GLA_DV, (h + 1) * GLA_DV)
            sf_ref[h] = st_f[rows, cols]
            sb_ref[h] = st_b[rows, cols]


def _gla(gq, la, gkt, lat, gv, s0f, s0b, consts, row0, n_batch, seq):
    (cf, cb), levels_of, bd, vbd = consts
    c = GLA_CHUNK
    nc = seq // c
    blk0 = row0 // c
    n_la_blocks_b = 1
    fwd = lambda b, n: blk0 + b * nc + n
    bwd = lambda b, n: blk0 + b * nc + (nc - 1 - n)

    def tok(w, which, col=0):
        return pl.BlockSpec((c, w), lambda b, n: (which(b, n), col))

    def tr(r, which, row=0):
        return pl.BlockSpec((r, c), lambda b, n: (row, which(b, n)))

    full = lambda a: pl.BlockSpec(a.shape, lambda b, n: (0,) * a.ndim)
    st_spec = pl.BlockSpec((None, GLA_HEADS, GLA_DK, GLA_DV), lambda b, n: (b, 0, 0, 0))
    in_specs = [tok(GLA_KW, fwd), tok(GLA_KW, fwd, 0), tr(GLA_KW, fwd), tr(GLA_KW, fwd, 0),
                tok(GLA_WIDTH, fwd),
                tok(GLA_KW, bwd), tok(GLA_KW, bwd, n_la_blocks_b), tr(GLA_KW, bwd),
                tr(GLA_KW, bwd, 1), tok(GLA_WIDTH, bwd),
                st_spec, st_spec,
                full(cf[0]), full(cf[1]), full(cf[2]), full(cb[0]), full(cb[1]), full(cb[2]),
                full(bd), full(vbd)]
    out_specs = (pl.BlockSpec((c, GLA_WIDTH), lambda b, n: (b * nc + n, 0)),
                 pl.BlockSpec((c, GLA_WIDTH), lambda b, n: (b * nc + (nc - 1 - n), 0)),
                 st_spec, st_spec)
    out_shape = (jax.ShapeDtypeStruct((n_batch * seq, GLA_WIDTH), F32),
                 jax.ShapeDtypeStruct((n_batch * seq, GLA_WIDTH), F32),
                 jax.ShapeDtypeStruct((n_batch, GLA_HEADS, GLA_DK, GLA_DV), F32),
                 jax.ShapeDtypeStruct((n_batch, GLA_HEADS, GLA_DK, GLA_DV), F32))
    return pl.pallas_call(
        functools.partial(_gla_kernel, levels_of=levels_of),
        grid=(n_batch, nc), in_specs=in_specs, out_specs=out_specs, out_shape=out_shape,
        scratch_shapes=[pltpu.VMEM((GLA_KW, GLA_WIDTH), F32), pltpu.VMEM((GLA_KW, GLA_WIDTH), F32)],
        compiler_params=_cparams(2), name="gla",
    )(gq, la, gkt, lat, gv, gq, la, gkt, lat, gv, s0f, s0b,
      cf[0], cf[1], cf[2], cb[0], cb[1], cb[2], bd, vbd)


def _outproj_kernel(attc_ref, attl_ref, ofc_ref, ofl_ref, obc_ref, obl_ref, gg_ref, xc_ref, xl_ref,
                    mod_ref, wo_ref, gn_ref, l1g_ref, l1b_ref, wrt_ref, sw13_ref, sw2_ref,
                    base_ref, u2_ref, lg_ref, *, n_ctx_tiles):
    is_ctx = pl.program_id(0) < n_ctx_tiles
    pick = lambda a_ref, b_ref: jnp.where(is_ctx, a_ref[...], b_ref[...])
    m = mod_ref[...]
    gate1 = m[:, 2 * D_MODEL:3 * D_MODEL]
    shift2 = m[:, 3 * D_MODEL:4 * D_MODEL]
    scale2 = m[:, 4 * D_MODEL:5 * D_MODEL]
    gate2 = m[:, 5 * D_MODEL:6 * D_MODEL]
    og = pick(ofc_ref, ofl_ref) + pick(obc_ref, obl_ref)
    gg = gg_ref[...].astype(F32)
    parts = []
    for h in range(GLA_HEADS):
        blk = og[:, h * GLA_DV:(h + 1) * GLA_DV]
        ms = jnp.mean(blk * blk, axis=-1, keepdims=True)
        nb = blk * lax.rsqrt(ms + EPS) * gn_ref[...]
        parts.append((nb * _silu(gg[:, h * GLA_DV:(h + 1) * GLA_DV])).astype(BF16))
    att_t = pick(attc_ref, attl_ref)
    hmix = (lax.dot_general(att_t, wo_ref[0:ATT_WIDTH, :], (((0,), (0,)), ((), ())),
                            preferred_element_type=F32)
            + jnp.dot(jnp.concatenate(parts, axis=1), wo_ref[ATT_WIDTH:, :],
                      preferred_element_type=F32))
    x1 = _layer_norm(ALPHA * pick(xc_ref, xl_ref) + gate1 * hmix, l1g_ref[...], l1b_ref[...])
    u2 = x1 * (1.0 + scale2) + shift2
    u2b = u2.astype(BF16)
    lg_ref[...] = lax.dot_general(wrt_ref[...], u2b, (((1,), (1,)), ((), ())),
                                  preferred_element_type=F32)
    ab = jnp.dot(u2b, sw13_ref[...], preferred_element_type=F32)
    hid = (_silu(ab[:, 0:SHARED_FF]) * ab[:, SHARED_FF:2 * SHARED_FF]).astype(BF16)
    shared = jnp.dot(hid, sw2_ref[...], preferred_element_type=F32)
    base_ref[...] = ALPHA * x1 + gate2 * shared
    _pack_rows(u2_ref, u2)


def _out_projection(att_c, att_l, of_c, of_l, ob_c, ob_l, gg, x_c, x_l, mod3, wo, gn, l1g, l1b, wrt,
                    sw13, sw2, n_seq_tiles):
    n_ctx = x_c.shape[0]
    n = n_ctx + x_l.shape[0]
    tb = TOK_TILE
    n_ctx_tiles = n_ctx // tb

    def mod_idx(i):
        return (jnp.where(i < n_ctx_tiles, 0, 1 + (i - n_ctx_tiles) // n_seq_tiles), 0, 0)

    ctx_blk = lambda i: jnp.minimum(i, n_ctx_tiles - 1)
    lat_blk = lambda i: jnp.maximum(i - n_ctx_tiles, 0)
    tok = lambda w: pl.BlockSpec((tb, w), lambda i: (i, 0))
    tok_c = lambda w: pl.BlockSpec((tb, w), lambda i: (ctx_blk(i), 0))
    tok_l = lambda w: pl.BlockSpec((tb, w), lambda i: (lat_blk(i), 0))
    full = lambda a: pl.BlockSpec(a.shape, lambda i: (0,) * a.ndim)
    return pl.pallas_call(
        functools.partial(_outproj_kernel, n_ctx_tiles=n_ctx_tiles),
        grid=(n // tb,),
        in_specs=[pl.BlockSpec((ATT_WIDTH, tb), lambda i: (0, ctx_blk(i))),
                  pl.BlockSpec((ATT_WIDTH, tb), lambda i: (0, lat_blk(i))),
                  tok_c(GLA_WIDTH), tok_l(GLA_WIDTH), tok_c(GLA_WIDTH), tok_l(GLA_WIDTH),
                  tok(GLA_WIDTH), tok_c(D_MODEL), tok_l(D_MODEL),
                  pl.BlockSpec((None, 1, mod3.shape[2]), mod_idx),
                  full(wo), full(gn), full(l1g), full(l1b), full(wrt), full(sw13), full(sw2)],
        out_specs=(tok(D_MODEL),
                   pl.BlockSpec((tb * PACK_CHUNKS, LANES), lambda i: (i, 0)),
                   pl.BlockSpec((N_EXPERTS, tb), lambda i: (0, i))),
        out_shape=(jax.ShapeDtypeStruct((n, D_MODEL), F32),
                   jax.ShapeDtypeStruct((n * PACK_CHUNKS, LANES), U32),
                   jax.ShapeDtypeStruct((N_EXPERTS, n), F32)),
        compiler_params=_cparams(1), name="out_projection",
    )(att_c, att_l, of_c, of_l, ob_c, ob_l, gg, x_c, x_l, mod3, wo, gn, l1g, l1b, wrt, sw13, sw2)


def _route_kernel(lg_ref, bias_ref, upper_ref, idx_ref, w_ref, pos_ref, cnt_ref, run_ref):
    i = pl.program_id(0)

    @pl.when(i == 0)
    def _():
        run_ref[...] = jnp.zeros_like(run_ref)

    s = jax.nn.sigmoid(lg_ref[...])
    work = s + bias_ref[...]
    rows = lax.broadcasted_iota(I32, s.shape, 0).astype(F32)
    sel = jnp.zeros(s.shape, F32)
    idxs, vals = [], []
    for _ in range(TOP_K):
        mx = jnp.max(work, axis=0, keepdims=True)
        idx = jnp.min(jnp.where(work == mx, rows, float(N_EXPERTS)), axis=0, keepdims=True)
        hit = rows == idx
        vals.append(jnp.sum(jnp.where(hit, s, 0.0), axis=0, keepdims=True))
        idxs.append(idx)
        sel = jnp.where(hit, 1.0, sel)
        work = jnp.where(hit, -jnp.inf, work)
    den = functools.reduce(jnp.add, vals)
    rank = jnp.dot(sel.astype(BF16), upper_ref[...], preferred_element_type=F32) + run_ref[:, 0:1]
    for k in range(TOP_K):
        idx_ref[k:k + 1, :] = idxs[k].astype(I32)
        w_ref[k:k + 1, :] = vals[k] / den * ROUTED_SCALE
        pos_ref[k:k + 1, :] = jnp.sum(jnp.where(rows == idxs[k], rank, 0.0), axis=0,
                                      keepdims=True).astype(I32)
    run_ref[...] = run_ref[...] + jnp.sum(sel, axis=1, keepdims=True)
    cnt_ref[...] = run_ref[...]


def _route(logits_t, bias_col, upper):
    n = logits_t.shape[1]
    tt = TOK_TILE
    row = lambda dt: jax.ShapeDtypeStruct((TOP_K, n), dt)
    blk = pl.BlockSpec((TOP_K, tt), lambda i: (0, i))
    return pl.pallas_call(
        _route_kernel,
        grid=(n // tt,),
        in_specs=[pl.BlockSpec((N_EXPERTS, tt), lambda i: (0, i)),
                  pl.BlockSpec((N_EXPERTS, 1), lambda i: (0, 0)),
                  pl.BlockSpec((tt, tt), lambda i: (0, 0))],
        out_specs=(blk, blk, blk, pl.BlockSpec((N_EXPERTS, LANES), lambda i: (0, 0))),
        out_shape=(row(I32), row(F32), row(I32), jax.ShapeDtypeStruct((N_EXPERTS, LANES), F32)),
        scratch_shapes=[pltpu.VMEM((N_EXPERTS, LANES), F32)],
        compiler_params=_cparams(1), name="route",
    )(logits_t, bias_col, upper)


def _dest_kernel(cnt_ref, lower_ref, idx_ref, pos_ref, dest_ref, bexp_ref, bval_ref, nused_ref):
    cnt = cnt_ref[...]
    nblk = jnp.floor((cnt + (MOE_ROWS - 1)) * (1.0 / MOE_ROWS))
    bstart = jnp.dot(lower_ref[...], nblk, precision=HIGHEST, preferred_element_type=F32)
    bend = bstart + nblk
    pstart = bstart[:, 0:1] * MOE_ROWS
    rows = lax.broadcasted_iota(I32, (N_EXPERTS, idx_ref.shape[1]), 0)
    for k in range(TOP_K):
        hit = rows == idx_ref[k:k + 1, :]
        dest_ref[k:k + 1, :] = (jnp.sum(jnp.where(hit, pstart, 0.0), axis=0, keepdims=True)
                                .astype(I32) + pos_ref[k:k + 1, :])

    @pl.when(pl.program_id(0) == 0)
    def _():
        nb = bexp_ref.shape[1]
        bid = lax.broadcasted_iota(I32, (N_EXPERTS, nb), 1).astype(F32)
        inside = jnp.logical_and(bid >= bstart[:, 0:1], bid < bend[:, 0:1])
        erow = lax.broadcasted_iota(I32, (N_EXPERTS, nb), 0).astype(F32)
        bexp_ref[...] = jnp.sum(jnp.where(inside, erow, 0.0), axis=0, keepdims=True).astype(I32)
        valid = jnp.clip(cnt[:, 0:1] - (bid - bstart[:, 0:1]) * MOE_ROWS, 0.0, float(MOE_ROWS))
        bval_ref[...] = jnp.sum(jnp.where(inside, valid, 0.0), axis=0, keepdims=True).astype(I32)
        nused_ref[...] = jnp.max(bend, axis=0, keepdims=True).astype(I32)


def _destinations(counts, lower, idx_t, pos_t, n_blocks_pad):
    n = idx_t.shape[1]
    tt = TOK_TILE
    blk = pl.BlockSpec((TOP_K, tt), lambda i: (0, i))
    one = lambda w: pl.BlockSpec((1, w), lambda i: (0, 0))
    return pl.pallas_call(
        _dest_kernel,
        grid=(n // tt,),
        in_specs=[pl.BlockSpec((N_EXPERTS, LANES), lambda i: (0, 0)),
                  pl.BlockSpec((N_EXPERTS, N_EXPERTS), lambda i: (0, 0)), blk, blk],
        out_specs=(blk, one(n_blocks_pad), one(n_blocks_pad), one(LANES)),
        out_shape=(jax.ShapeDtypeStruct((TOP_K, n), I32),
                   jax.ShapeDtypeStruct((1, n_blocks_pad), I32),
                   jax.ShapeDtypeStruct((1, n_blocks_pad), I32),
                   jax.ShapeDtypeStruct((1, LANES), I32)),
        compiler_params=_cparams(1), name="destinations",
    )(counts, lower, idx_t, pos_t)


SC_WINDOW = 128


def _invert_rows(dest_flat, n_rows):
    m = dest_flat.shape[0]
    mesh = plsc.VectorSubcoreMesh(core_axis_name="core", subcore_axis_name="subcore")

    @functools.partial(pl.kernel, out_type=jax.ShapeDtypeStruct((n_rows,), I32), mesh=mesh,
                       scratch_types=[])
    def invert(val_hbm, idx_hbm, out_hbm):
        def body(val_vmem, idx_vmem):
            pltpu.sync_copy(val_vmem.at[0], out_hbm.at[idx_vmem.at[0]])

        pltpu.emit_pipeline(
            body, grid=(m // SC_WINDOW,),
            in_specs=[pl.BlockSpec((1, SC_WINDOW), lambda i: (0, i)),
                      pl.BlockSpec((1, SC_WINDOW), lambda i: (0, i))],
            out_specs=[], core_axis_name=("core", "subcore"),
            dimension_semantics=(pltpu.PARALLEL,),
        )(val_hbm, idx_hbm)

    return invert(jnp.arange(m, dtype=I32).reshape(1, m), dest_flat.reshape(1, m))


PACK_CHUNKS = D_MODEL // (2 * LANES)
SRC_GROUP = 4
U32 = jnp.uint32


def _pack_rows(ref, x):
    bits = pltpu.bitcast(x.astype(BF16).astype(F32), U32)
    for s in range(PACK_CHUNKS):
        lo = bits[:, (2 * s) * LANES:(2 * s + 1) * LANES] >> 16
        hi = bits[:, (2 * s + 1) * LANES:(2 * s + 2) * LANES] & jnp.uint32(0xFFFF0000)
        ref[pl.ds(s, x.shape[0], stride=PACK_CHUNKS), :] = lo | hi


def _unpack_rows(ref, n_rows):
    parts = []
    for s in range(PACK_CHUNKS):
        w = ref[s * n_rows:(s + 1) * n_rows, :]
        parts.append(pltpu.bitcast(w << 16, F32))
        parts.append(pltpu.bitcast(w & jnp.uint32(0xFFFF0000), F32))
    return jnp.concatenate(parts, axis=1).astype(BF16)


def _moe_kernel(bexp_ref, bval_ref, nused_ref, u2p_hbm, src_hbm, w1_ref, w3_ref, w2_ref, yt_hbm,
                u2p_vmem, w13_s, w2_s, xbuf, ybuf, src_smem, sem_in, sem_src, sem_out,
                *, n_tokens):
    b = pl.program_id(0)
    n_used = nused_ref[0]
    br = MOE_ROWS
    grp = SRC_GROUP * br
    trash0 = n_tokens * TOP_K

    def src_copy(g):
        return pltpu.make_async_copy(src_hbm.at[pl.ds(g * grp, grp)],
                                     src_smem.at[pl.ds(lax.rem(g, 2) * grp, grp)], sem_src)

    def out_wait(slot):
        pltpu.make_async_copy(ybuf.at[pl.ds(slot * br * ROW_CHUNKS, br * ROW_CHUNKS)],
                              yt_hbm.at[pl.ds(0, br * ROW_CHUNKS)], sem_out.at[slot]).wait()

    @pl.when(b == 0)
    def _():
        cp = pltpu.make_async_copy(u2p_hbm, u2p_vmem, sem_in)
        cp.start()
        src_copy(0).start()
        cp.wait()

    @pl.when(b < n_used)
    def _():
        g = b // SRC_GROUP

        @pl.when(lax.rem(b, SRC_GROUP) == 0)
        def _():
            src_copy(g).wait()

            @pl.when((g + 1) * SRC_GROUP < n_used)
            def _():
                src_copy(g + 1).start()

        e = bexp_ref[b]
        prev = bexp_ref[jnp.maximum(b - 1, 0)]

        @pl.when(jnp.logical_or(b == 0, e != prev))
        def _():
            w13_s[:, 0:EXPERT_FF] = w1_ref[...].astype(BF16)
            w13_s[:, EXPERT_FF:2 * EXPERT_FF] = w3_ref[...].astype(BF16)
            w2_s[...] = w2_ref[...].astype(BF16)

        valid = bval_ref[b]
        sbase = lax.rem(g, 2) * grp + lax.rem(b, SRC_GROUP) * br
        slot = lax.rem(b, 2)

        def gather(r, carry):
            tok = jnp.minimum(jnp.maximum(src_smem[sbase + r], 0) // TOP_K, n_tokens - 1)
            row = pl.multiple_of(tok * PACK_CHUNKS, PACK_CHUNKS)
            xbuf[pl.ds(r, PACK_CHUNKS, stride=br), :] = u2p_vmem[pl.ds(row, PACK_CHUNKS), :]
            return carry

        lax.fori_loop(0, br, gather, 0)
        x = _unpack_rows(xbuf, br)
        rows = lax.broadcasted_iota(I32, x.shape, 0)
        x = jnp.where(rows < valid, x, jnp.zeros_like(x))
        ab = jnp.dot(x, w13_s[...], preferred_element_type=F32)
        hid = (_silu(ab[:, 0:EXPERT_FF]) * ab[:, EXPERT_FF:2 * EXPERT_FF]).astype(BF16)
        y = jnp.dot(hid, w2_s[...], preferred_element_type=F32)

        @pl.when(b >= 2)
        def _():
            out_wait(slot)

        ybase = slot * br * ROW_CHUNKS
        for cidx in range(ROW_CHUNKS):
            ybuf[pl.ds(ybase + cidx, br, stride=ROW_CHUNKS), :] = y[:, cidx * LANES:(cidx + 1) * LANES]

        def scatter(r, carry):
            dst = jnp.where(r < valid, src_smem[sbase + r], trash0 + slot * br + r)
            pltpu.make_async_copy(
                ybuf.at[pl.ds(pl.multiple_of(ybase + r * ROW_CHUNKS, ROW_CHUNKS), ROW_CHUNKS)],
                yt_hbm.at[pl.ds(pl.multiple_of(dst * ROW_CHUNKS, ROW_CHUNKS), ROW_CHUNKS)],
                sem_out.at[slot]).start()
            return carry

        lax.fori_loop(0, br, scatter, 0)

        @pl.when(b == n_used - 1)
        def _():
            @pl.when(b >= 1)
            def _():
                out_wait(1 - slot)
            out_wait(slot)


def _moe_experts(bexp, bval, nused, u2p, row_src, w1, w3, w2, n_blocks, n_tokens):
    br = MOE_ROWS

    def w_idx(b, bexp, bval, nused):
        return (bexp[jnp.minimum(b, nused[0] - 1)], 0, 0)

    grid_spec = pltpu.PrefetchScalarGridSpec(
        num_scalar_prefetch=3, grid=(n_blocks,),
        in_specs=[pl.BlockSpec(memory_space=pl.ANY), pl.BlockSpec(memory_space=pl.ANY),
                  pl.BlockSpec((None, D_MODEL, EXPERT_FF), w_idx),
                  pl.BlockSpec((None, D_MODEL, EXPERT_FF), w_idx),
                  pl.BlockSpec((None, EXPERT_FF, D_MODEL), w_idx)],
        out_specs=pl.BlockSpec(memory_space=pl.ANY),
        scratch_shapes=[pltpu.VMEM(u2p.shape, U32),
                        pltpu.VMEM((D_MODEL, 2 * EXPERT_FF), BF16),
                        pltpu.VMEM((EXPERT_FF, D_MODEL), BF16),
                        pltpu.VMEM((PACK_CHUNKS * br, LANES), U32),
                        pltpu.VMEM((2 * br * ROW_CHUNKS, LANES), F32),
                        pltpu.SMEM((2 * SRC_GROUP * br,), I32),
                        pltpu.SemaphoreType.DMA, pltpu.SemaphoreType.DMA,
                        pltpu.SemaphoreType.DMA((2,))])
    n_out_tiles = n_tokens * TOP_K + 2 * br
    return pl.pallas_call(
        functools.partial(_moe_kernel, n_tokens=n_tokens), grid_spec=grid_spec,
        out_shape=jax.ShapeDtypeStruct((n_out_tiles * ROW_CHUNKS, LANES), F32),
        compiler_params=pltpu.CompilerParams(dimension_semantics=("arbitrary",),
                                             vmem_limit_bytes=MOE_VMEM_LIMIT),
        name="moe_experts",
    )(bexp, bval, nused, u2p, row_src, w1, w3, w2)


def _combine_kernel(w_hbm, yt_ref, base_ref, mod_ref, g_ref, b_ref, yc_ref, yl_ref,
                    w_smem, acc_buf, sem_w, *, n_ctx_tiles):
    i = pl.program_id(0)
    n_steps = pl.num_programs(0)
    n_tok = acc_buf.shape[0] // ROW_CHUNKS
    n_idx = n_tok * TOP_K

    def w_copy(tile):
        return pltpu.make_async_copy(w_hbm.at[pl.ds(tile * n_idx, n_idx)],
                                     w_smem.at[pl.ds(lax.rem(tile, 2) * n_idx, n_idx)], sem_w)

    @pl.when(i == 0)
    def _():
        w_copy(i).start()

    w_copy(i).wait()

    @pl.when(i + 1 < n_steps)
    def _():
        w_copy(i + 1).start()

    wbase = lax.rem(i, 2) * n_idx

    def reduce_token(t, carry):
        acc = None
        for k in range(TOP_K):
            j = t * TOP_K + k
            row = pl.multiple_of(j * ROW_CHUNKS, ROW_CHUNKS)
            term = w_smem[wbase + j] * yt_ref[pl.ds(row, ROW_CHUNKS), :]
            acc = term if acc is None else acc + term
        acc_buf[pl.ds(pl.multiple_of(t * ROW_CHUNKS, ROW_CHUNKS), ROW_CHUNKS), :] = acc
        return carry

    lax.fori_loop(0, n_tok, reduce_token, 0)
    moe = _load_row_tiles(acc_buf, n_tok)
    gate2 = mod_ref[:, 5 * D_MODEL:6 * D_MODEL]
    y = _layer_norm(base_ref[...] + gate2 * moe, g_ref[...], b_ref[...])

    @pl.when(i < n_ctx_tiles)
    def _():
        yc_ref[...] = y

    @pl.when(i >= n_ctx_tiles)
    def _():
        yl_ref[...] = y


def _combine(w_flat, yt, base, mod3, l2g, l2b, n_ctx, seq_tokens):
    n = base.shape[0]
    tc = COMB_TILE
    n_ctx_tiles = n_ctx // tc
    n_seq_tiles = seq_tokens // tc

    def mod_idx(i):
        return (jnp.where(i < n_ctx_tiles, 0, 1 + (i - n_ctx_tiles) // n_seq_tiles), 0, 0)

    full = lambda a: pl.BlockSpec(a.shape, lambda i: (0,) * a.ndim)
    return pl.pallas_call(
        functools.partial(_combine_kernel, n_ctx_tiles=n_ctx_tiles),
        grid=(n // tc,),
        in_specs=[pl.BlockSpec(memory_space=pl.ANY),
                  pl.BlockSpec((tc * TOP_K * ROW_CHUNKS, LANES), lambda i: (i, 0)),
                  pl.BlockSpec((tc, D_MODEL), lambda i: (i, 0)),
                  pl.BlockSpec((None, 1, mod3.shape[2]), mod_idx), full(l2g), full(l2b)],
        out_specs=(pl.BlockSpec((tc, D_MODEL), lambda i: (jnp.minimum(i, n_ctx_tiles - 1), 0)),
                   pl.BlockSpec((tc, D_MODEL), lambda i: (jnp.maximum(i - n_ctx_tiles, 0), 0))),
        out_shape=(jax.ShapeDtypeStruct((n_ctx, D_MODEL), F32),
                   jax.ShapeDtypeStruct((n - n_ctx, D_MODEL), F32)),
        scratch_shapes=[pltpu.SMEM((2 * tc * TOP_K,), F32),
                        pltpu.VMEM((tc * ROW_CHUNKS, LANES), F32),
                        pltpu.SemaphoreType.DMA],
        compiler_params=_cparams(1), name="combine",
    )(w_flat, yt, base, mod3, l2g, l2b)


def _rope_tables(n_tok, tile):
    rows = n_tok // GRID_W
    row_idx = jnp.repeat(jnp.arange(rows, dtype=F32), GRID_W)
    col_idx = jnp.tile(jnp.arange(GRID_W, dtype=F32), rows)
    inv_freq = 1.0 / (ROPE_THETA ** (jnp.arange(0, ROPE_AXIS_DIM, 2, dtype=F32) / ROPE_AXIS_DIM))
    ang_r = row_idx[:, None] * inv_freq[None, :]
    ang_c = col_idx[:, None] * inv_freq[None, :]
    ang = jnp.concatenate([ang_r, ang_r, ang_c, ang_c], axis=-1)
    cos, sin = jnp.cos(ang), jnp.sin(ang)
    quarter = (jnp.arange(HEAD_DIM) // (ROPE_AXIS_DIM // 2)) % 2
    sin_a = jnp.where(quarter == 0, -sin, 0.0)
    sin_b = jnp.where(quarter == 1, sin, 0.0)
    rep = LANES // HEAD_DIM
    ident = lambda v: jnp.full((tile, LANES), v, F32)
    cos_t = jnp.concatenate([jnp.tile(cos, (1, rep)), ident(1.0)], axis=0)
    sa_t = jnp.concatenate([jnp.tile(sin_a, (1, rep)), ident(0.0)], axis=0)
    sb_t = jnp.concatenate([jnp.tile(sin_b, (1, rep)), ident(0.0)], axis=0)
    ident_tr = lambda v: jnp.full((HEAD_DIM, tile), v, F32)
    cos_tr = jnp.concatenate([cos.T, ident_tr(1.0)], axis=1)
    sin_tr = jnp.concatenate([sin.T, ident_tr(0.0)], axis=1)
    return cos_t, sa_t, sb_t, cos_tr, sin_tr


def _dup_heads(a):
    parts = []
    for h in range(KV_HEADS):
        blk = a[..., h * HEAD_DIM:(h + 1) * HEAD_DIM]
        parts += [blk] * (LANES // HEAD_DIM)
    return jnp.concatenate(parts, axis=-1)


def kernel(x_prompt, x_sample, cache_k, cache_v, state_gla_fwd, state_gla_bwd, c, c_ctx, w_ada, b_ada, w_in, q_norm, k_norm, gla_wa_fwd, gla_ba_fwd, gla_wa_bwd, gla_ba_bwd, gla_norm, w_out, ln1_g, ln1_b, ln2_g, ln2_b, w_router, router_bias, exp_w1, exp_w3, exp_w2, sh_w1, sh_w3, sh_w2):
    n_ctx_b, ctx_seq, _ = x_prompt.shape
    n_lat_b, lat_seq, _ = x_sample.shape
    n_ctx = n_ctx_b * ctx_seq
    n_lat = n_lat_b * lat_seq
    n = n_ctx + n_lat
    l = 0

    x_c = x_prompt.reshape(n_ctx, D_MODEL)
    x_l = x_sample.reshape(n_lat, D_MODEL)

    c_rows = jnp.zeros((SUBLANES, D_MODEL), F32).at[0].set(c_ctx).at[1:1 + n_lat_b].set(c)
    mod = _modulation(c_rows, w_ada[l], b_ada[l][None, :])
    mod3 = mod.reshape(SUBLANES, 1, 6 * D_MODEL)

    wi = w_in[l]
    o_q, o_k, o_v, o_gq, o_gk, o_gv, o_gg, o_rf, o_rb, o_end = np.cumsum(
        [0, ATT_WIDTH, KV_HEADS * HEAD_DIM, KV_HEADS * HEAD_DIM, GLA_KW, GLA_KW, GLA_WIDTH, GLA_WIDTH,
         GLA_GATE_RANK, GLA_GATE_RANK])
    w_tok = jnp.concatenate([
        _dup_heads(wi[:, o_k:o_v]), wi[:, o_v:o_gq], wi[:, o_gq:o_gk],
        wi[:, o_gv:o_gg], wi[:, o_gg:o_rf], wi[:, o_rf:o_end],
        jnp.zeros((D_MODEL, LANES - 2 * GLA_GATE_RANK), F32)], axis=1).astype(BF16)
    w_tr = jnp.concatenate([wi[:, o_q:o_k], wi[:, o_v:o_gq], wi[:, o_gk:o_gv], wi[:, o_rf:o_end]],
                           axis=1).T.astype(BF16)
    rep = LANES // HEAD_DIM
    qn = q_norm[l][:, None]
    kn = jnp.tile(k_norm[l], rep)[None, :]
    seg = jnp.asarray(np.kron(np.eye(rep), np.ones((HEAD_DIM, HEAD_DIM))), BF16)
    wa = jnp.zeros((LANES, 2 * GLA_KW), F32)
    wa = wa.at[0:GLA_GATE_RANK, 0:GLA_KW].set(gla_wa_fwd[l])
    wa = wa.at[GLA_GATE_RANK:2 * GLA_GATE_RANK, GLA_KW:].set(gla_wa_bwd[l])
    ba = jnp.concatenate([gla_ba_fwd[l], gla_ba_bwd[l]])[None, :]
    wat = wa[0:2 * GLA_GATE_RANK, :].T
    bat = ba.T
    cos_t, sa_t, sb_t, cos_tr, sin_tr = _rope_tables(lat_seq, TOK_TILE)

    (qt, k_dup, vt, k32, v32, gq, gv, gg, la, gkt, lat) = _in_projection(
        x_c, x_l, mod3, w_tok, w_tr, qn, kn, cos_t, sa_t, sb_t, cos_tr, sin_tr, seg, wa, ba, wat, bat,
        lat_seq // TOK_TILE)

    ck = _dup_heads(cache_k[:, l].reshape(n_lat_b, -1, KV_HEADS * HEAD_DIM)).astype(BF16)
    cvt = cache_v[:, l].reshape(n_lat_b, -1, KV_HEADS * HEAD_DIM).transpose(0, 2, 1).astype(BF16)
    att_c = _attention(qt, k_dup, vt, None, 0, n_ctx_b, ctx_seq)
    att_l = _attention(qt, k_dup, vt, (ck, cvt), n_ctx, n_lat_b, lat_seq)

    gconst, levels_of = _gla_constants()
    to_dev = lambda t: (jnp.asarray(t[0], BF16), jnp.asarray(t[1], BF16), jnp.asarray(t[2], F32))
    bd = jnp.asarray(np.kron(np.eye(GLA_HEADS), np.ones((GLA_DK, GLA_DV))), BF16)
    vbd = jnp.asarray(np.kron(np.eye(GLA_HEADS), np.ones((GLA_CHUNK, GLA_DV))), BF16)
    consts = ((to_dev(gconst["f"]), to_dev(gconst["b"])), levels_of, bd, vbd)
    s_zero = jnp.zeros((n_ctx_b, GLA_HEADS, GLA_DK, GLA_DV), F32)
    of_c, ob_c, sf_new, sb_new = _gla(gq, la, gkt, lat, gv, s_zero, s_zero, consts, 0, n_ctx_b, ctx_seq)
    of_l, ob_l, _, _ = _gla(gq, la, gkt, lat, gv, state_gla_fwd[:, l], state_gla_bwd[:, l], consts,
                            n_ctx, n_lat_b, lat_seq)

    sw13 = jnp.concatenate([sh_w1[l], sh_w3[l]], axis=1).astype(BF16)
    base, u2_rows, logits_t = _out_projection(
        att_c, att_l, of_c, of_l, ob_c, ob_l, gg, x_c, x_l, mod3, w_out[l].astype(BF16),
        gla_norm[l][None, :], ln1_g[l][None, :], ln1_b[l][None, :], w_router[l].T.astype(BF16), sw13,
        sh_w2[l].astype(BF16), lat_seq // TOK_TILE)

    upper = jnp.asarray(np.triu(np.ones((TOK_TILE, TOK_TILE)), 1), BF16)
    idx_t, w_t, pos_t, counts = _route(logits_t, router_bias[l][:, None], upper)
    n_blocks = n * TOP_K // MOE_ROWS + N_EXPERTS
    n_blocks_pad = -(-n_blocks // LANES) * LANES
    lower = jnp.asarray(np.tril(np.ones((N_EXPERTS, N_EXPERTS)), -1), F32)
    dest_t, bexp, bval, nused = _destinations(counts, lower, idx_t, pos_t, n_blocks_pad)
    dest_flat = dest_t.T.reshape(-1)
    w_flat = w_t.T.reshape(-1)

    row_src = _invert_rows(dest_flat, n_blocks * MOE_ROWS)
    yt = _moe_experts(bexp.reshape(-1), bval.reshape(-1), nused.reshape(-1)[0:1], u2_rows, row_src,
                      exp_w1[l], exp_w3[l], exp_w2[l], n_blocks, n)
    y_c, y_l = _combine(w_flat, yt, base, mod3, ln2_g[l][None, :], ln2_b[l][None, :], n_ctx, lat_seq)

    y_prompt = y_c.reshape(n_ctx_b, ctx_seq, D_MODEL)
    y_sample = y_l.reshape(n_lat_b, lat_seq, D_MODEL)
    new_cache_k = k32.reshape(n_ctx_b, 1, ctx_seq, KV_HEADS, HEAD_DIM)
    new_cache_v = v32.reshape(n_ctx_b, 1, ctx_seq, KV_HEADS, HEAD_DIM)
    return (y_prompt, y_sample, new_cache_k, new_cache_v, sf_new[:, None], sb_new[:, None])
```

```python
import functools

import numpy as np
import jax
import jax.numpy as jnp
from jax import lax
from jax.experimental import pallas as pl
from jax.experimental.pallas import tpu as pltpu
from jax.experimental.pallas import tpu_sc as plsc

F32 = jnp.float32
BF16 = jnp.bfloat16
I32 = jnp.int32

D_MODEL = 1024
GRID_W = 64
HEAD_DIM = 64
N_HEADS = 8
KV_HEADS = 2
ATT_WIDTH = N_HEADS * HEAD_DIM
ATT_SCALE = HEAD_DIM ** -0.5
LOG2_E = 1.4426950408889634
ROPE_AXIS_DIM = HEAD_DIM // 2
ROPE_THETA = 10000.0
GLA_HEADS = 4
GLA_DK = 64
GLA_DV = 128
GLA_WIDTH = GLA_HEADS * GLA_DV
GLA_KW = GLA_HEADS * GLA_DK
GLA_GATE_RANK = 16
GLA_TAU = 16.0
N_EXPERTS = 256
TOP_K = 8
EXPERT_FF = 256
SHARED_FF = 256
ROUTED_SCALE = 2.5
DEPTH = 1
ALPHA = (2.0 * DEPTH) ** 0.25
EPS = 1e-6

LANES = 128
SUBLANES = 8
ROW_CHUNKS = D_MODEL // LANES
VMEM_LIMIT = 56 * 1024 * 1024

TOK_TILE = 512
ATT_TQ = 128
GLA_CHUNK = 128
GLA_LEVELS = ((32, 128), (8, 32), (2, 8), (1, 2))
MOE_ROWS = 256
MOE_VMEM_LIMIT = 62 * 1024 * 1024
COMB_TILE = 128
HIGHEST = lax.Precision.HIGHEST


def _cparams(n_axes):
    return pltpu.CompilerParams(dimension_semantics=("arbitrary",) * n_axes,
                                vmem_limit_bytes=VMEM_LIMIT)


def _silu(x):
    return x * jax.nn.sigmoid(x)


def _log_sigmoid(x):
    return jnp.minimum(x, 0.0) - jnp.log(1.0 + jnp.exp(-jnp.abs(x)))


def _load_row_tiles(ref, n_rows, row0=0):
    return jnp.concatenate(
        [ref[pl.ds(row0 * ROW_CHUNKS + cidx, n_rows, stride=ROW_CHUNKS), :] for cidx in range(ROW_CHUNKS)],
        axis=1)


def _store_row_tiles(ref, x):
    for cidx in range(ROW_CHUNKS):
        ref[pl.ds(cidx, x.shape[0], stride=ROW_CHUNKS), :] = x[:, cidx * LANES:(cidx + 1) * LANES]


def _layer_norm(z, g, b):
    mu = jnp.mean(z, axis=-1, keepdims=True)
    zc = z - mu
    var = jnp.mean(zc * zc, axis=-1, keepdims=True)
    return zc * lax.rsqrt(var + EPS) * g + b


def _mod_kernel(c_ref, w_ref, b_ref, o_ref):
    s = _silu(c_ref[...]).astype(BF16)
    o_ref[...] = jnp.dot(s, w_ref[...].astype(BF16), preferred_element_type=F32) + b_ref[...]


def _modulation(c_rows, w_ada, b_ada):
    n_cols = w_ada.shape[1]
    tn = 512
    return pl.pallas_call(
        _mod_kernel,
        grid=(n_cols // tn,),
        in_specs=[pl.BlockSpec((SUBLANES, D_MODEL), lambda j: (0, 0)),
                  pl.BlockSpec((D_MODEL, tn), lambda j: (0, j)),
                  pl.BlockSpec((1, tn), lambda j: (0, j))],
        out_specs=pl.BlockSpec((SUBLANES, tn), lambda j: (0, j)),
        out_shape=jax.ShapeDtypeStruct((SUBLANES, n_cols), F32),
        compiler_params=_cparams(1),
        name="modulation",
    )(c_rows, w_ada, b_ada)


_C_K = 0
_C_V = _C_K + 2 * LANES
_C_GQ = _C_V + KV_HEADS * HEAD_DIM
_C_GV = _C_GQ + GLA_KW
_C_GG = _C_GV + GLA_WIDTH
_C_RA = _C_GG + GLA_WIDTH
_C_END = _C_RA + LANES
_R_Q = 0
_R_V = _R_Q + ATT_WIDTH
_R_GK = _R_V + KV_HEADS * HEAD_DIM
_R_RA = _R_GK + GLA_KW
_R_END = _R_RA + 2 * GLA_GATE_RANK


def _inproj_kernel(xc_ref, xl_ref, mod_ref, w_ref, wt_ref, qn_ref, kn_ref, cos_ref, sa_ref, sb_ref,
                   cost_ref, sint_ref, seg_ref, wa_ref, ba_ref, wat_ref, bat_ref,
                   qt_ref, k_ref, vt_ref, k32_ref, v32_ref, gq_ref, gv_ref, gg_ref,
                   la_ref, gkt_ref, lat_ref, *, n_ctx_tiles):
    i = pl.program_id(0)
    m = mod_ref[...]
    shift1 = m[:, 0:D_MODEL]
    scale1 = m[:, D_MODEL:2 * D_MODEL]
    x = jnp.where(i < n_ctx_tiles, xc_ref[...], xl_ref[...])
    u = (x * (1.0 + scale1) + shift1).astype(BF16)

    cos = cos_ref[...]
    sin_a = sa_ref[...]
    sin_b = sb_ref[...]
    seg = seg_ref[...]
    lane = lax.broadcasted_iota(I32, (u.shape[0], LANES), 1)
    low = lane < HEAD_DIM

    def proj(c0, c1):
        return jnp.dot(u, w_ref[:, c0:c1], preferred_element_type=F32)

    def head_norm(blk, gain):
        ss = jnp.dot((blk * blk).astype(BF16), seg, preferred_element_type=F32) * (1.0 / HEAD_DIM)
        return blk * lax.rsqrt(ss + EPS) * gain

    def rope(blk):
        return (blk * cos + pltpu.roll(blk, LANES - ROPE_AXIS_DIM // 2, 1) * sin_a
                + pltpu.roll(blk, ROPE_AXIS_DIM // 2, 1) * sin_b)

    pk = proj(_C_K, _C_V)
    kn = [head_norm(pk[:, j * LANES:(j + 1) * LANES], kn_ref[...]) for j in range(KV_HEADS)]
    for j in range(KV_HEADS):
        k_ref[:, j * LANES:(j + 1) * LANES] = rope(kn[j]).astype(BF16)

    @pl.when(i < n_ctx_tiles)
    def _():
        k32_ref[...] = jnp.where(low, kn[0], kn[1])
        v32_ref[...] = proj(_C_V, _C_GQ)

    gq_ref[...] = proj(_C_GQ, _C_GV) * (GLA_DK ** -0.5)
    gv_ref[...] = proj(_C_GV, _C_GG).astype(BF16)
    gg_ref[...] = proj(_C_GG, _C_RA).astype(BF16)

    ra = proj(_C_RA, _C_END)
    pre = jnp.dot(ra, wa_ref[...], precision=HIGHEST, preferred_element_type=F32) + ba_ref[...]
    la_ref[...] = _log_sigmoid(pre) * (1.0 / GLA_TAU)

    pt = lax.dot_general(wt_ref[...], u, (((1,), (1,)), ((), ())), preferred_element_type=F32)
    cos_t = cost_ref[...]
    sin_t = sint_ref[...]
    quarter = ROPE_AXIS_DIM // 2
    for h in range(N_HEADS):
        blk = pt[_R_Q + h * HEAD_DIM:_R_Q + (h + 1) * HEAD_DIM, :]
        ms = jnp.mean(blk * blk, axis=0, keepdims=True)
        qn = blk * lax.rsqrt(ms + EPS) * qn_ref[...]
        rot = jnp.concatenate([-qn[quarter:2 * quarter], qn[0:quarter],
                               -qn[3 * quarter:4 * quarter], qn[2 * quarter:3 * quarter]], axis=0)
        qt_ref[h * HEAD_DIM:(h + 1) * HEAD_DIM, :] = (
            (qn * cos_t + rot * sin_t) * (ATT_SCALE * LOG2_E)).astype(BF16)
    vt_ref[...] = pt[_R_V:_R_GK, :].astype(BF16)
    gkt_ref[...] = pt[_R_GK:_R_RA, :]
    rat = pt[_R_RA:_R_END, :]
    pre_t = jnp.dot(wat_ref[...], rat, precision=HIGHEST, preferred_element_type=F32) + bat_ref[...]
    lat_ref[...] = _log_sigmoid(pre_t) * (1.0 / GLA_TAU)


def _in_projection(x_c, x_l, mod3, w_tok, w_tr, qn, kn, cos_t, sa_t, sb_t, cos_tr, sin_tr, seg, wa, ba,
                   wat, bat, n_seq_tiles):
    n_ctx = x_c.shape[0]
    n = n_ctx + x_l.shape[0]
    tb = TOK_TILE
    n_ctx_tiles = n_ctx // tb
    n_tiles = n // tb
    n_rope_blocks = cos_t.shape[0] // tb - 1

    def mod_idx(i):
        return (jnp.where(i < n_ctx_tiles, 0, 1 + (i - n_ctx_tiles) // n_seq_tiles), 0, 0)

    def rope_blk(i):
        return jnp.where(i < n_ctx_tiles, n_rope_blocks, (i - n_ctx_tiles) % n_seq_tiles)

    def rope_idx(i):
        return (rope_blk(i), 0)

    def ctx_idx(i):
        return (jnp.minimum(i, n_ctx_tiles - 1), 0)

    tok = lambda w: pl.BlockSpec((tb, w), lambda i: (i, 0))
    full = lambda a: pl.BlockSpec(a.shape, lambda i: (0,) * a.ndim)
    tr = lambda r: pl.BlockSpec((r, tb), lambda i: (0, i))
    rope_tr = pl.BlockSpec((HEAD_DIM, tb), lambda i: (0, rope_blk(i)))
    out_shapes = (
        jax.ShapeDtypeStruct((ATT_WIDTH, n), BF16),
        jax.ShapeDtypeStruct((n, 2 * LANES), BF16),
        jax.ShapeDtypeStruct((KV_HEADS * HEAD_DIM, n), BF16),
        jax.ShapeDtypeStruct((n_ctx, LANES), F32),
        jax.ShapeDtypeStruct((n_ctx, LANES), F32),
        jax.ShapeDtypeStruct((n, GLA_KW), F32),
        jax.ShapeDtypeStruct((n, GLA_WIDTH), BF16),
        jax.ShapeDtypeStruct((n, GLA_WIDTH), BF16),
        jax.ShapeDtypeStruct((n, 2 * GLA_KW), F32),
        jax.ShapeDtypeStruct((GLA_KW, n), F32),
        jax.ShapeDtypeStruct((2 * GLA_KW, n), F32),
    )
    out_specs = (tr(ATT_WIDTH), tok(2 * LANES), tr(KV_HEADS * HEAD_DIM),
                 pl.BlockSpec((tb, LANES), ctx_idx), pl.BlockSpec((tb, LANES), ctx_idx),
                 tok(GLA_KW), tok(GLA_WIDTH), tok(GLA_WIDTH), tok(2 * GLA_KW),
                 tr(GLA_KW), tr(2 * GLA_KW))
    in_specs = [pl.BlockSpec((tb, D_MODEL), ctx_idx),
                pl.BlockSpec((tb, D_MODEL), lambda i: (jnp.maximum(i - n_ctx_tiles, 0), 0)),
                pl.BlockSpec((None, 1, mod3.shape[2]), mod_idx),
                full(w_tok), full(w_tr), full(qn), full(kn),
                pl.BlockSpec((tb, LANES), rope_idx), pl.BlockSpec((tb, LANES), rope_idx),
                pl.BlockSpec((tb, LANES), rope_idx), rope_tr, rope_tr,
                full(seg), full(wa), full(ba), full(wat), full(bat)]
    return pl.pallas_call(
        functools.partial(_inproj_kernel, n_ctx_tiles=n_ctx_tiles),
        grid=(n_tiles,), in_specs=in_specs, out_specs=out_specs, out_shape=out_shapes,
        compiler_params=_cparams(1), name="in_projection",
    )(x_c, x_l, mod3, w_tok, w_tr, qn, kn, cos_t, sa_t, sb_t, cos_tr, sin_tr, seg, wa, ba, wat, bat)


def _attention_kernel(*refs, n_kv_parts):
    qt_ref = refs[0]
    k_refs = refs[1:1 + n_kv_parts]
    vt_refs = refs[1 + n_kv_parts:1 + 2 * n_kv_parts]
    o_ref = refs[1 + 2 * n_kv_parts]
    tq = qt_ref.shape[1]
    group = N_HEADS // KV_HEADS
    for kv in range(KV_HEADS):
        heads = range(kv * group, (kv + 1) * group)
        q_grp = jnp.concatenate([qt_ref[h * HEAD_DIM:(h + 1) * HEAD_DIM, :] for h in heads], axis=1)
        rhs = jnp.concatenate([q_grp, jnp.zeros_like(q_grp)], axis=0)
        s = [jnp.dot(k[:, kv * LANES:(kv + 1) * LANES], rhs, preferred_element_type=F32)
             for k in k_refs]
        mx = functools.reduce(jnp.maximum, [jnp.max(x, axis=0, keepdims=True) for x in s])
        pr = [jnp.exp2(x - mx) for x in s]
        den = functools.reduce(jnp.add, [jnp.sum(x, axis=0, keepdims=True) for x in pr])
        acc = functools.reduce(jnp.add, [
            jnp.dot(vt[kv * HEAD_DIM:(kv + 1) * HEAD_DIM, :], x.astype(BF16),
                    preferred_element_type=F32) for x, vt in zip(pr, vt_refs)])
        out = (acc / den).astype(BF16)
        for j, h in enumerate(heads):
            o_ref[h * HEAD_DIM:(h + 1) * HEAD_DIM, :] = out[:, j * tq:(j + 1) * tq]


def _attention(qt, k, vt, extra_kv, row0, n_batch, seq):
    tq = ATT_TQ
    n_q = seq // tq
    q_blk0 = row0 // tq
    kv_blk0 = row0 // seq
    in_specs = [pl.BlockSpec((ATT_WIDTH, tq), lambda b, i: (0, q_blk0 + b * n_q + i))]
    k_spec = pl.BlockSpec((seq, 2 * LANES), lambda b, i: (kv_blk0 + b, 0))
    vt_spec = pl.BlockSpec((KV_HEADS * HEAD_DIM, seq), lambda b, i: (0, kv_blk0 + b))
    args_k, args_v, specs_k, specs_v = [k], [vt], [k_spec], [vt_spec]
    if extra_kv is not None:
        ck, cvt = extra_kv
        args_k.append(ck)
        args_v.append(cvt)
        specs_k.append(pl.BlockSpec((None, ck.shape[1], 2 * LANES), lambda b, i: (b, 0, 0)))
        specs_v.append(pl.BlockSpec((None, KV_HEADS * HEAD_DIM, cvt.shape[2]), lambda b, i: (b, 0, 0)))
    return pl.pallas_call(
        functools.partial(_attention_kernel, n_kv_parts=len(args_k)),
        grid=(n_batch, n_q),
        in_specs=in_specs + specs_k + specs_v,
        out_specs=pl.BlockSpec((ATT_WIDTH, tq), lambda b, i: (0, b * n_q + i)),
        out_shape=jax.ShapeDtypeStruct((ATT_WIDTH, n_batch * seq), BF16),
        compiler_params=_cparams(2), name="attention",
    )(qt, *args_k, *args_v)


def _gla_constants():
    c = GLA_CHUNK
    idx = np.arange(c)
    q_mats, k_mats, masks, levels_of = [], [], [], []
    for li, (s, p) in enumerate(GLA_LEVELS):
        start = (idx // s) * s
        end = start + s - 1
        k_mats.append(((idx[None, :] > idx[:, None]) & (idx[None, :] <= end[:, None])))
        for d in range(p // s - 1):
            lo = np.maximum(start - d * s, 0)
            q_mats.append((idx[None, :] >= lo[:, None]) & (idx[None, :] <= idx[:, None]))
            masks.append((idx[:, None] // p == idx[None, :] // p)
                         & (idx[:, None] // s - idx[None, :] // s - 1 == d))
            levels_of.append(li)
    masks.append(np.eye(c, dtype=bool))
    levels_of.append(len(GLA_LEVELS) - 1)
    q_mats.append(idx[None, :] <= idx[:, None])
    k_mats = k_mats[:-1]
    k_mats.append(idx[None, :] > idx[:, None])
    k_mats.append(np.ones((c, c), bool))
    out = {}
    for name, flip in (("f", False), ("b", True)):
        f = (lambda a: a[::-1, ::-1]) if flip else (lambda a: a)
        lq = np.concatenate([f(a) for a in q_mats], axis=0).astype(np.float32)
        lkt = np.concatenate([f(a).T for a in k_mats], axis=1).astype(np.float32)
        mk = np.stack([np.tile(f(a), (1, GLA_HEADS)) for a in masks]).astype(np.float32)
        out[name] = (np.concatenate([lq, lq], axis=1), np.concatenate([lkt, lkt], axis=0), mk)
    return out, tuple(levels_of)


def _gla_direction(q, g, gkt, gt, v, lq2, lkt2, masks_ref, bd, vbd, s_ref, levels_of):
    c = GLA_CHUNK
    n_var = len(levels_of)
    n_lev = len(GLA_LEVELS)
    g_hi = g.astype(BF16)
    g_lo = (g - g_hi.astype(F32)).astype(BF16)
    fq = jnp.dot(lq2, jnp.concatenate([g_hi, g_lo], axis=0), preferred_element_type=F32)
    gt_hi = gt.astype(BF16)
    gt_lo = (gt - gt_hi.astype(F32)).astype(BF16)
    fk = jnp.dot(jnp.concatenate([gt_hi, gt_lo], axis=1), lkt2, preferred_element_type=F32)

    def key_factor(f):
        return gkt * jnp.exp(fk[:, f * c:(f + 1) * c])

    q_var = [(q * jnp.exp(fq[vi * c:(vi + 1) * c, :])).astype(BF16) for vi in range(n_var - 1)]
    q_var.append(q.astype(BF16))
    a = jnp.zeros((c, GLA_HEADS * c), F32)
    for li in range(n_lev):
        kt = (key_factor(li) if li < n_lev - 1 else gkt).astype(BF16)
        xt = jnp.concatenate([kt] * GLA_HEADS, axis=1) * bd
        vis = [vi for vi in range(n_var) if levels_of[vi] == li]
        res = jnp.dot(jnp.concatenate([q_var[vi] for vi in vis], axis=0), xt,
                      preferred_element_type=F32)
        for r, vi in enumerate(vis):
            a = a + masks_ref[vi] * res[r * c:(r + 1) * c, :]
    q_in = (q * jnp.exp(fq[(n_var - 1) * c:n_var * c, :])).astype(BF16)
    state = s_ref[...]
    v_bd = jnp.concatenate([v] * GLA_HEADS, axis=0) * vbd
    o = (jnp.dot(q_in, state.astype(BF16), preferred_element_type=F32)
         + jnp.dot(a.astype(BF16), v_bd, preferred_element_type=F32))
    k_out = key_factor(n_lev - 1).astype(BF16)
    e_tot = jnp.exp(fk[:, n_lev * c:(n_lev + 1) * c])
    upd = jnp.dot(k_out, v, preferred_element_type=F32)
    s_ref[...] = (state * jnp.concatenate([e_tot] * (GLA_WIDTH // c), axis=1)
                  + upd * bd.astype(F32))
    return o


def _gla_kernel(gq_f, la_f, gkt_f, lat_f, gv_f, gq_b, la_b, gkt_b, lat_b, gv_b,
                s0f_ref, s0b_ref, lq2f, lkt2f, mkf, lq2b, lkt2b, mkb, bd_ref, vbd_ref,
                of_ref, ob_ref, sf_ref, sb_ref, st_f, st_b, *, levels_of):
    n = pl.program_id(1)

    @pl.when(n == 0)
    def _():
        st_f[...] = jnp.zeros_like(st_f)
        st_b[...] = jnp.zeros_like(st_b)
        for h in range(GLA_HEADS):
            rows = slice(h * GLA_DK, (h + 1) * GLA_DK)
            cols = slice(h * GLA_DV, (h + 1) * GLA_DV)
            st_f[rows, cols] = s0f_ref[h]
            st_b[rows, cols] = s0b_ref[h]

    bd = bd_ref[...]
    vbd = vbd_ref[...]
    of_ref[...] = _gla_direction(gq_f[...], la_f[...], gkt_f[...], lat_f[...], gv_f[...],
                                 lq2f[...], lkt2f[...], mkf, bd, vbd, st_f, levels_of)
    ob_ref[...] = _gla_direction(gq_b[...], la_b[...], gkt_b[...], lat_b[...], gv_b[...],
                                 lq2b[...], lkt2b[...], mkb, bd, vbd, st_b, levels_of)

    @pl.when(n == pl.num_programs(1) - 1)
    def _():
        for h in range(GLA_HEADS):
            rows = slice(h * GLA_DK, (h + 1) * GLA_DK)
            cols = slice(h * GLA_DV, (h + 1) * GLA_DV)
            sf_ref[h] = st_f[rows, cols]
            sb_ref[h] = st_b[rows, cols]


def _gla(gq, la, gkt, lat, gv, s0f, s0b, consts, row0, n_batch, seq):
    (cf, cb), levels_of, bd, vbd = consts
    c = GLA_CHUNK
    nc = seq // c
    blk0 = row0 // c
    n_la_blocks_b = 1
    fwd = lambda b, n: blk0 + b * nc + n
    bwd = lambda b, n: blk0 + b * nc + (nc - 1 - n)

    def tok(w, which, col=0):
        return pl.BlockSpec((c, w), lambda b, n: (which(b, n), col))

    def tr(r, which, row=0):
        return pl.BlockSpec((r, c), lambda b, n: (row, which(b, n)))

    full = lambda a: pl.BlockSpec(a.shape, lambda b, n: (0,) * a.ndim)
    st_spec = pl.BlockSpec((None, GLA_HEADS, GLA_DK, GLA_DV), lambda b, n: (b, 0, 0, 0))
    in_specs = [tok(GLA_KW, fwd), tok(GLA_KW, fwd, 0), tr(GLA_KW, fwd), tr(GLA_KW, fwd, 0),
                tok(GLA_WIDTH, fwd),
                tok(GLA_KW, bwd), tok(GLA_KW, bwd, n_la_blocks_b), tr(GLA_KW, bwd),
                tr(GLA_KW, bwd, 1), tok(GLA_WIDTH, bwd),
                st_spec, st_spec,
                full(cf[0]), full(cf[1]), full(cf[2]), full(cb[0]), full(cb[1]), full(cb[2]),
                full(bd), full(vbd)]
    out_specs = (pl.BlockSpec((c, GLA_WIDTH), lambda b, n: (b * nc + n, 0)),
                 pl.BlockSpec((c, GLA_WIDTH), lambda b, n: (b * nc + (nc - 1 - n), 0)),
                 st_spec, st_spec)
    out_shape = (jax.ShapeDtypeStruct((n_batch * seq, GLA_WIDTH), F32),
                 jax.ShapeDtypeStruct((n_batch * seq, GLA_WIDTH), F32),
                 jax.ShapeDtypeStruct((n_batch, GLA_HEADS, GLA_DK, GLA_DV), F32),
                 jax.ShapeDtypeStruct((n_batch, GLA_HEADS, GLA_DK, GLA_DV), F32))
    return pl.pallas_call(
        functools.partial(_gla_kernel, levels_of=levels_of),
        grid=(n_batch, nc), in_specs=in_specs, out_specs=out_specs, out_shape=out_shape,
        scratch_shapes=[pltpu.VMEM((GLA_KW, GLA_WIDTH), F32), pltpu.VMEM((GLA_KW, GLA_WIDTH), F32)],
        compiler_params=_cparams(2), name="gla",
    )(gq, la, gkt, lat, gv, gq, la, gkt, lat, gv, s0f, s0b,
      cf[0], cf[1], cf[2], cb[0], cb[1], cb[2], bd, vbd)


def _outproj_kernel(attc_ref, attl_ref, ofc_ref, ofl_ref, obc_ref, obl_ref, gg_ref, xc_ref, xl_ref,
                    mod_ref, wo_ref, gn_ref, l1g_ref, l1b_ref, wrt_ref, sw13_ref, sw2_ref,
                    base_ref, u2_ref, lg_ref, *, n_ctx_tiles):
    is_ctx = pl.program_id(0) < n_ctx_tiles
    pick = lambda a_ref, b_ref: jnp.where(is_ctx, a_ref[...], b_ref[...])
    m = mod_ref[...]
    gate1 = m[:, 2 * D_MODEL:3 * D_MODEL]
    shift2 = m[:, 3 * D_MODEL:4 * D_MODEL]
    scale2 = m[:, 4 * D_MODEL:5 * D_MODEL]
    gate2 = m[:, 5 * D_MODEL:6 * D_MODEL]
    og = pick(ofc_ref, ofl_ref) + pick(obc_ref, obl_ref)
    gg = gg_ref[...].astype(F32)
    parts = []
    for h in range(GLA_HEADS):
        blk = og[:, h * GLA_DV:(h + 1) * GLA_DV]
        ms = jnp.mean(blk * blk, axis=-1, keepdims=True)
        nb = blk * lax.rsqrt(ms + EPS) * gn_ref[...]
        parts.append((nb * _silu(gg[:, h * GLA_DV:(h + 1) * GLA_DV])).astype(BF16))
    att_t = pick(attc_ref, attl_ref)
    hmix = (lax.dot_general(att_t, wo_ref[0:ATT_WIDTH, :], (((0,), (0,)), ((), ())),
                            preferred_element_type=F32)
            + jnp.dot(jnp.concatenate(parts, axis=1), wo_ref[ATT_WIDTH:, :],
                      preferred_element_type=F32))
    x1 = _layer_norm(ALPHA * pick(xc_ref, xl_ref) + gate1 * hmix, l1g_ref[...], l1b_ref[...])
    u2 = x1 * (1.0 + scale2) + shift2
    u2b = u2.astype(BF16)
    lg_ref[...] = lax.dot_general(wrt_ref[...], u2b, (((1,), (1,)), ((), ())),
                                  preferred_element_type=F32)
    ab = jnp.dot(u2b, sw13_ref[...], preferred_element_type=F32)
    hid = (_silu(ab[:, 0:SHARED_FF]) * ab[:, SHARED_FF:2 * SHARED_FF]).astype(BF16)
    shared = jnp.dot(hid, sw2_ref[...], preferred_element_type=F32)
    base_ref[...] = ALPHA * x1 + gate2 * shared
    _pack_rows(u2_ref, u2)


def _out_projection(att_c, att_l, of_c, of_l, ob_c, ob_l, gg, x_c, x_l, mod3, wo, gn, l1g, l1b, wrt,
                    sw13, sw2, n_seq_tiles):
    n_ctx = x_c.shape[0]
    n = n_ctx + x_l.shape[0]
    tb = TOK_TILE
    n_ctx_tiles = n_ctx // tb

    def mod_idx(i):
        return (jnp.where(i < n_ctx_tiles, 0, 1 + (i - n_ctx_tiles) // n_seq_tiles), 0, 0)

    ctx_blk = lambda i: jnp.minimum(i, n_ctx_tiles - 1)
    lat_blk = lambda i: jnp.maximum(i - n_ctx_tiles, 0)
    tok = lambda w: pl.BlockSpec((tb, w), lambda i: (i, 0))
    tok_c = lambda w: pl.BlockSpec((tb, w), lambda i: (ctx_blk(i), 0))
    tok_l = lambda w: pl.BlockSpec((tb, w), lambda i: (lat_blk(i), 0))
    full = lambda a: pl.BlockSpec(a.shape, lambda i: (0,) * a.ndim)
    return pl.pallas_call(
        functools.partial(_outproj_kernel, n_ctx_tiles=n_ctx_tiles),
        grid=(n // tb,),
        in_specs=[pl.BlockSpec((ATT_WIDTH, tb), lambda i: (0, ctx_blk(i))),
                  pl.BlockSpec((ATT_WIDTH, tb), lambda i: (0, lat_blk(i))),
                  tok_c(GLA_WIDTH), tok_l(GLA_WIDTH), tok_c(GLA_WIDTH), tok_l(GLA_WIDTH),
                  tok(GLA_WIDTH), tok_c(D_MODEL), tok_l(D_MODEL),
                  pl.BlockSpec((None, 1, mod3.shape[2]), mod_idx),
                  full(wo), full(gn), full(l1g), full(l1b), full(wrt), full(sw13), full(sw2)],
        out_specs=(tok(D_MODEL),
                   pl.BlockSpec((tb * PACK_CHUNKS, LANES), lambda i: (i, 0)),
                   pl.BlockSpec((N_EXPERTS, tb), lambda i: (0, i))),
        out_shape=(jax.ShapeDtypeStruct((n, D_MODEL), F32),
                   jax.ShapeDtypeStruct((n * PACK_CHUNKS, LANES), U32),
                   jax.ShapeDtypeStruct((N_EXPERTS, n), F32)),
        compiler_params=_cparams(1), name="out_projection",
    )(att_c, att_l, of_c, of_l, ob_c, ob_l, gg, x_c, x_l, mod3, wo, gn, l1g, l1b, wrt, sw13, sw2)


def _route_kernel(lg_ref, bias_ref, upper_ref, idx_ref, w_ref, pos_ref, cnt_ref, run_ref):
    i = pl.program_id(0)

    @pl.when(i == 0)
    def _():
        run_ref[...] = jnp.zeros_like(run_ref)

    s = jax.nn.sigmoid(lg_ref[...])
    work = s + bias_ref[...]
    rows = lax.broadcasted_iota(I32, s.shape, 0).astype(F32)
    sel = jnp.zeros(s.shape, F32)
    idxs, vals = [], []
    for _ in range(TOP_K):
        mx = jnp.max(work, axis=0, keepdims=True)
        idx = jnp.min(jnp.where(work == mx, rows, float(N_EXPERTS)), axis=0, keepdims=True)
        hit = rows == idx
        vals.append(jnp.sum(jnp.where(hit, s, 0.0), axis=0, keepdims=True))
        idxs.append(idx)
        sel = jnp.where(hit, 1.0, sel)
        work = jnp.where(hit, -jnp.inf, work)
    den = functools.reduce(jnp.add, vals)
    rank = jnp.dot(sel.astype(BF16), upper_ref[...], preferred_element_type=F32) + run_ref[:, 0:1]
    for k in range(TOP_K):
        idx_ref[k:k + 1, :] = idxs[k].astype(I32)
        w_ref[k:k + 1, :] = vals[k] / den * ROUTED_SCALE
        pos_ref[k:k + 1, :] = jnp.sum(jnp.where(rows == idxs[k], rank, 0.0), axis=0,
                                      keepdims=True).astype(I32)
    run_ref[...] = run_ref[...] + jnp.sum(sel, axis=1, keepdims=True)
    cnt_ref[...] = run_ref[...]


def _route(logits_t, bias_col, upper):
    n = logits_t.shape[1]
    tt = TOK_TILE
    row = lambda dt: jax.ShapeDtypeStruct((TOP_K, n), dt)
    blk = pl.BlockSpec((TOP_K, tt), lambda i: (0, i))
    return pl.pallas_call(
        _route_kernel,
        grid=(n // tt,),
        in_specs=[pl.BlockSpec((N_EXPERTS, tt), lambda i: (0, i)),
                  pl.BlockSpec((N_EXPERTS, 1), lambda i: (0, 0)),
                  pl.BlockSpec((tt, tt), lambda i: (0, 0))],
        out_specs=(blk, blk, blk, pl.BlockSpec((N_EXPERTS, LANES), lambda i: (0, 0))),
        out_shape=(row(I32), row(F32), row(I32), jax.ShapeDtypeStruct((N_EXPERTS, LANES), F32)),
        scratch_shapes=[pltpu.VMEM((N_EXPERTS, LANES), F32)],
        compiler_params=_cparams(1), name="route",
    )(logits_t, bias_col, upper)


def _dest_kernel(cnt_ref, lower_ref, idx_ref, pos_ref, dest_ref, bexp_ref, bval_ref, nused_ref):
    cnt = cnt_ref[...]
    nblk = jnp.floor((cnt + (MOE_ROWS - 1)) * (1.0 / MOE_ROWS))
    bstart = jnp.dot(lower_ref[...], nblk, precision=HIGHEST, preferred_element_type=F32)
    bend = bstart + nblk
    pstart = bstart[:, 0:1] * MOE_ROWS
    rows = lax.broadcasted_iota(I32, (N_EXPERTS, idx_ref.shape[1]), 0)
    for k in range(TOP_K):
        hit = rows == idx_ref[k:k + 1, :]
        dest_ref[k:k + 1, :] = (jnp.sum(jnp.where(hit, pstart, 0.0), axis=0, keepdims=True)
                                .astype(I32) + pos_ref[k:k + 1, :])

    @pl.when(pl.program_id(0) == 0)
    def _():
        nb = bexp_ref.shape[1]
        bid = lax.broadcasted_iota(I32, (N_EXPERTS, nb), 1).astype(F32)
        inside = jnp.logical_and(bid >= bstart[:, 0:1], bid < bend[:, 0:1])
        erow = lax.broadcasted_iota(I32, (N_EXPERTS, nb), 0).astype(F32)
        bexp_ref[...] = jnp.sum(jnp.where(inside, erow, 0.0), axis=0, keepdims=True).astype(I32)
        valid = jnp.clip(cnt[:, 0:1] - (bid - bstart[:, 0:1]) * MOE_ROWS, 0.0, float(MOE_ROWS))
        bval_ref[...] = jnp.sum(jnp.where(inside, valid, 0.0), axis=0, keepdims=True).astype(I32)
        nused_ref[...] = jnp.max(bend, axis=0, keepdims=True).astype(I32)


def _destinations(counts, lower, idx_t, pos_t, n_blocks_pad):
    n = idx_t.shape[1]
    tt = TOK_TILE
    blk = pl.BlockSpec((TOP_K, tt), lambda i: (0, i))
    one = lambda w: pl.BlockSpec((1, w), lambda i: (0, 0))
    return pl.pallas_call(
        _dest_kernel,
        grid=(n // tt,),
        in_specs=[pl.BlockSpec((N_EXPERTS, LANES), lambda i: (0, 0)),
                  pl.BlockSpec((N_EXPERTS, N_EXPERTS), lambda i: (0, 0)), blk, blk],
        out_specs=(blk, one(n_blocks_pad), one(n_blocks_pad), one(LANES)),
        out_shape=(jax.ShapeDtypeStruct((TOP_K, n), I32),
                   jax.ShapeDtypeStruct((1, n_blocks_pad), I32),
                   jax.ShapeDtypeStruct((1, n_blocks_pad), I32),
                   jax.ShapeDtypeStruct((1, LANES), I32)),
        compiler_params=_cparams(1), name="destinations",
    )(counts, lower, idx_t, pos_t)


SC_WINDOW = 128


def _invert_rows(dest_flat, n_rows):
    m = dest_flat.shape[0]
    mesh = plsc.VectorSubcoreMesh(core_axis_name="core", subcore_axis_name="subcore")

    @functools.partial(pl.kernel, out_type=jax.ShapeDtypeStruct((n_rows,), I32), mesh=mesh,
                       scratch_types=[])
    def invert(val_hbm, idx_hbm, out_hbm):
        def body(val_vmem, idx_vmem):
            pltpu.sync_copy(val_vmem.at[0], out_hbm.at[idx_vmem.at[0]])

        pltpu.emit_pipeline(
            body, grid=(m // SC_WINDOW,),
            in_specs=[pl.BlockSpec((1, SC_WINDOW), lambda i: (0, i)),
                      pl.BlockSpec((1, SC_WINDOW), lambda i: (0, i))],
            out_specs=[], core_axis_name=("core", "subcore"),
            dimension_semantics=(pltpu.PARALLEL,),
        )(val_hbm, idx_hbm)

    return invert(jnp.arange(m, dtype=I32).reshape(1, m), dest_flat.reshape(1, m))


PACK_CHUNKS = D_MODEL // (2 * LANES)
SRC_GROUP = 4
TOP_K_LOG2 = TOP_K.bit_length() - 1
PACK_CHUNKS_LOG2 = PACK_CHUNKS.bit_length() - 1
U32 = jnp.uint32


def _pack_rows(ref, x):
    bits = pltpu.bitcast(x.astype(BF16).astype(F32), U32)
    for s in range(PACK_CHUNKS):
        lo = bits[:, (2 * s) * LANES:(2 * s + 1) * LANES] >> 16
        hi = bits[:, (2 * s + 1) * LANES:(2 * s + 2) * LANES] & jnp.uint32(0xFFFF0000)
        ref[pl.ds(s, x.shape[0], stride=PACK_CHUNKS), :] = lo | hi


def _unpack_rows(ref, n_rows):
    parts = []
    for s in range(PACK_CHUNKS):
        w = ref[pl.ds(s, n_rows, stride=PACK_CHUNKS), :]
        parts.append(pltpu.bitcast(w << 16, F32))
        parts.append(pltpu.bitcast(w & jnp.uint32(0xFFFF0000), F32))
    return jnp.concatenate(parts, axis=1).astype(BF16)


def _moe_kernel(bexp_ref, bval_ref, nused_ref, u2p_hbm, src_hbm, w1_ref, w3_ref, w2_ref, yt_hbm,
                u2p_vmem, w13_s, w2_s, xbuf, ybuf, src_smem, sem_in, sem_src, sem_out,
                *, n_tokens):
    b = pl.program_id(0)
    n_used = nused_ref[0]
    br = MOE_ROWS
    grp = SRC_GROUP * br
    trash0 = n_tokens * TOP_K

    def src_copy(g):
        return pltpu.make_async_copy(src_hbm.at[pl.ds(g * grp, grp)],
                                     src_smem.at[pl.ds(lax.rem(g, 2) * grp, grp)], sem_src)

    def out_wait(slot):
        pltpu.make_async_copy(ybuf.at[pl.ds(slot * br * ROW_CHUNKS, br * ROW_CHUNKS)],
                              yt_hbm.at[pl.ds(0, br * ROW_CHUNKS)], sem_out.at[slot]).wait()

    def src_base(blk):
        return lax.rem(blk // SRC_GROUP, 2) * grp + lax.rem(blk, SRC_GROUP) * br

    def scatter_row(blk_slot, sbase, valid, r):
        dst = jnp.where(r < valid, src_smem[sbase + r], trash0 + blk_slot * br + r)
        pltpu.make_async_copy(
            ybuf.at[pl.ds(pl.multiple_of((blk_slot * br + r) * ROW_CHUNKS, ROW_CHUNKS), ROW_CHUNKS)],
            yt_hbm.at[pl.ds(pl.multiple_of(dst * ROW_CHUNKS, ROW_CHUNKS), ROW_CHUNKS)],
            sem_out.at[blk_slot]).start()

    @pl.when(b == 0)
    def _():
        cp = pltpu.make_async_copy(u2p_hbm, u2p_vmem, sem_in)
        cp.start()
        src_copy(0).start()
        ybuf[...] = jnp.zeros_like(ybuf)
        cp.wait()
        src_copy(0).wait()

    @pl.when(b < n_used)
    def _():
        g = b // SRC_GROUP
        phase = lax.rem(b, SRC_GROUP)

        @pl.when(jnp.logical_and(phase == 1, (g + 1) * SRC_GROUP < n_used))
        def _():
            src_copy(g + 1).start()

        @pl.when(jnp.logical_and(phase == 0, b > 0))
        def _():
            src_copy(g).wait()

        e = bexp_ref[b]
        prev = bexp_ref[jnp.maximum(b - 1, 0)]

        @pl.when(jnp.logical_or(b == 0, e != prev))
        def _():
            w13_s[:, 0:EXPERT_FF] = w1_ref[...].astype(BF16)
            w13_s[:, EXPERT_FF:2 * EXPERT_FF] = w3_ref[...].astype(BF16)
            w2_s[...] = w2_ref[...].astype(BF16)

        valid = bval_ref[b]
        sbase = src_base(b)
        slot = lax.rem(b, 2)

        def gather(rg, carry):
            sb = sbase + rg * SUBLANES
            xb = rg * (SUBLANES * PACK_CHUNKS)
            for u in range(SUBLANES):
                row = lax.shift_right_logical(src_smem[sb + u], TOP_K_LOG2 - PACK_CHUNKS_LOG2)
                row = jnp.minimum(row & (-PACK_CHUNKS & 0x7FFFFFFF), (n_tokens - 1) * PACK_CHUNKS)
                xbuf[pl.ds(pl.multiple_of(xb + u * PACK_CHUNKS, PACK_CHUNKS), PACK_CHUNKS), :] = (
                    u2p_vmem[pl.ds(pl.multiple_of(row, PACK_CHUNKS), PACK_CHUNKS), :])
            return carry

        lax.fori_loop(0, br // SUBLANES, gather, 0)

        pb = jnp.maximum(b - 1, 0)
        p_valid = jnp.where(b > 0, bval_ref[pb], 0)
        p_base = src_base(pb)
        for r in range(br):
            scatter_row(1 - slot, p_base, p_valid, r)

        x = _unpack_rows(xbuf, br)
        rows = lax.broadcasted_iota(I32, x.shape, 0)
        x = jnp.where(rows < valid, x, jnp.zeros_like(x))
        ab = jnp.dot(x, w13_s[...], preferred_element_type=F32)
        hid = (_silu(ab[:, 0:EXPERT_FF]) * ab[:, EXPERT_FF:2 * EXPERT_FF]).astype(BF16)
        y = jnp.dot(hid, w2_s[...], preferred_element_type=F32)

        @pl.when(b >= 1)
        def _():
            out_wait(slot)

        ybase = slot * br * ROW_CHUNKS
        for cidx in range(ROW_CHUNKS):
            ybuf[pl.ds(ybase + cidx, br, stride=ROW_CHUNKS), :] = y[:, cidx * LANES:(cidx + 1) * LANES]

        @pl.when(b == n_used - 1)
        def _():
            lax.fori_loop(0, br, lambda r, c: (scatter_row(slot, sbase, valid, r), c)[1], 0)
            out_wait(1 - slot)
            out_wait(slot)


def _moe_experts(bexp, bval, nused, u2p, row_src, w1, w3, w2, n_blocks, n_tokens):
    br = MOE_ROWS

    def w_idx(b, bexp, bval, nused):
        return (bexp[jnp.minimum(b, nused[0] - 1)], 0, 0)

    grid_spec = pltpu.PrefetchScalarGridSpec(
        num_scalar_prefetch=3, grid=(n_blocks,),
        in_specs=[pl.BlockSpec(memory_space=pl.ANY), pl.BlockSpec(memory_space=pl.ANY),
                  pl.BlockSpec((None, D_MODEL, EXPERT_FF), w_idx),
                  pl.BlockSpec((None, D_MODEL, EXPERT_FF), w_idx),
                  pl.BlockSpec((None, EXPERT_FF, D_MODEL), w_idx)],
        out_specs=pl.BlockSpec(memory_space=pl.ANY),
        scratch_shapes=[pltpu.VMEM(u2p.shape, U32),
                        pltpu.VMEM((D_MODEL, 2 * EXPERT_FF), BF16),
                        pltpu.VMEM((EXPERT_FF, D_MODEL), BF16),
                        pltpu.VMEM((PACK_CHUNKS * br, LANES), U32),
                        pltpu.VMEM((2 * br * ROW_CHUNKS, LANES), F32),
                        pltpu.SMEM((2 * SRC_GROUP * br,), I32),
                        pltpu.SemaphoreType.DMA, pltpu.SemaphoreType.DMA,
                        pltpu.SemaphoreType.DMA((2,))])
    n_out_tiles = n_tokens * TOP_K + 2 * br
    return pl.pallas_call(
        functools.partial(_moe_kernel, n_tokens=n_tokens), grid_spec=grid_spec,
        out_shape=jax.ShapeDtypeStruct((n_out_tiles * ROW_CHUNKS, LANES), F32),
        compiler_params=pltpu.CompilerParams(dimension_semantics=("arbitrary",),
                                             vmem_limit_bytes=MOE_VMEM_LIMIT),
        name="moe_experts",
    )(bexp, bval, nused, u2p, row_src, w1, w3, w2)


def _combine_kernel(w_hbm, yt_ref, base_ref, mod_ref, g_ref, b_ref, yc_ref, yl_ref,
                    w_smem, acc_buf, sem_w, *, n_ctx_tiles):
    i = pl.program_id(0)
    n_steps = pl.num_programs(0)
    n_tok = acc_buf.shape[0] // ROW_CHUNKS
    n_idx = n_tok * TOP_K

    def w_copy(tile):
        return pltpu.make_async_copy(w_hbm.at[pl.ds(tile * n_idx, n_idx)],
                                     w_smem.at[pl.ds(lax.rem(tile, 2) * n_idx, n_idx)], sem_w)

    @pl.when(i == 0)
    def _():
        w_copy(i).start()

    w_copy(i).wait()

    @pl.when(i + 1 < n_steps)
    def _():
        w_copy(i + 1).start()

    wbase = lax.rem(i, 2) * n_idx

    def reduce_token(t, carry):
        acc = None
        for k in range(TOP_K):
            j = t * TOP_K + k
            row = pl.multiple_of(j * ROW_CHUNKS, ROW_CHUNKS)
            term = w_smem[wbase + j] * yt_ref[pl.ds(row, ROW_CHUNKS), :]
            acc = term if acc is None else acc + term
        acc_buf[pl.ds(pl.multiple_of(t * ROW_CHUNKS, ROW_CHUNKS), ROW_CHUNKS), :] = acc
        return carry

    lax.fori_loop(0, n_tok, reduce_token, 0)
    moe = _load_row_tiles(acc_buf, n_tok)
    gate2 = mod_ref[:, 5 * D_MODEL:6 * D_MODEL]
    y = _layer_norm(base_ref[...] + gate2 * moe, g_ref[...], b_ref[...])

    @pl.when(i < n_ctx_tiles)
    def _():
        yc_ref[...] = y

    @pl.when(i >= n_ctx_tiles)
    def _():
        yl_ref[...] = y


def _combine(w_flat, yt, base, mod3, l2g, l2b, n_ctx, seq_tokens):
    n = base.shape[0]
    tc = COMB_TILE
    n_ctx_tiles = n_ctx // tc
    n_seq_tiles = seq_tokens // tc

    def mod_idx(i):
        return (jnp.where(i < n_ctx_tiles, 0, 1 + (i - n_ctx_tiles) // n_seq_tiles), 0, 0)

    full = lambda a: pl.BlockSpec(a.shape, lambda i: (0,) * a.ndim)
    return pl.pallas_call(
        functools.partial(_combine_kernel, n_ctx_tiles=n_ctx_tiles),
        grid=(n // tc,),
        in_specs=[pl.BlockSpec(memory_space=pl.ANY),
                  pl.BlockSpec((tc * TOP_K * ROW_CHUNKS, LANES), lambda i: (i, 0)),
                  pl.BlockSpec((tc, D_MODEL), lambda i: (i, 0)),
                  pl.BlockSpec((None, 1, mod3.shape[2]), mod_idx), full(l2g), full(l2b)],
        out_specs=(pl.BlockSpec((tc, D_MODEL), lambda i: (jnp.minimum(i, n_ctx_tiles - 1), 0)),
                   pl.BlockSpec((tc, D_MODEL), lambda i: (jnp.maximum(i - n_ctx_tiles, 0), 0))),
        out_shape=(jax.ShapeDtypeStruct((n_ctx, D_MODEL), F32),
                   jax.ShapeDtypeStruct((n - n_ctx, D_MODEL), F32)),
        scratch_shapes=[pltpu.SMEM((2 * tc * TOP_K,), F32),
                        pltpu.VMEM((tc * ROW_CHUNKS, LANES), F32),
                        pltpu.SemaphoreType.DMA],
        compiler_params=_cparams(1), name="combine",
    )(w_flat, yt, base, mod3, l2g, l2b)


def _rope_tables(n_tok, tile):
    rows = n_tok // GRID_W
    row_idx = jnp.repeat(jnp.arange(rows, dtype=F32), GRID_W)
    col_idx = jnp.tile(jnp.arange(GRID_W, dtype=F32), rows)
    inv_freq = 1.0 / (ROPE_THETA ** (jnp.arange(0, ROPE_AXIS_DIM, 2, dtype=F32) / ROPE_AXIS_DIM))
    ang_r = row_idx[:, None] * inv_freq[None, :]
    ang_c = col_idx[:, None] * inv_freq[None, :]
    ang = jnp.concatenate([ang_r, ang_r, ang_c, ang_c], axis=-1)
    cos, sin = jnp.cos(ang), jnp.sin(ang)
    quarter = (jnp.arange(HEAD_DIM) // (ROPE_AXIS_DIM // 2)) % 2
    sin_a = jnp.where(quarter == 0, -sin, 0.0)
    sin_b = jnp.where(quarter == 1, sin, 0.0)
    rep = LANES // HEAD_DIM
    ident = lambda v: jnp.full((tile, LANES), v, F32)
    cos_t = jnp.concatenate([jnp.tile(cos, (1, rep)), ident(1.0)], axis=0)
    sa_t = jnp.concatenate([jnp.tile(sin_a, (1, rep)), ident(0.0)], axis=0)
    sb_t = jnp.concatenate([jnp.tile(sin_b, (1, rep)), ident(0.0)], axis=0)
    ident_tr = lambda v: jnp.full((HEAD_DIM, tile), v, F32)
    cos_tr = jnp.concatenate([cos.T, ident_tr(1.0)], axis=1)
    sin_tr = jnp.concatenate([sin.T, ident_tr(0.0)], axis=1)
    return cos_t, sa_t, sb_t, cos_tr, sin_tr


def _dup_heads(a):
    parts = []
    for h in range(KV_HEADS):
        blk = a[..., h * HEAD_DIM:(h + 1) * HEAD_DIM]
        parts += [blk] * (LANES // HEAD_DIM)
    return jnp.concatenate(parts, axis=-1)


def kernel(x_prompt, x_sample, cache_k, cache_v, state_gla_fwd, state_gla_bwd, c, c_ctx, w_ada, b_ada, w_in, q_norm, k_norm, gla_wa_fwd, gla_ba_fwd, gla_wa_bwd, gla_ba_bwd, gla_norm, w_out, ln1_g, ln1_b, ln2_g, ln2_b, w_router, router_bias, exp_w1, exp_w3, exp_w2, sh_w1, sh_w3, sh_w2):
    n_ctx_b, ctx_seq, _ = x_prompt.shape
    n_lat_b, lat_seq, _ = x_sample.shape
    n_ctx = n_ctx_b * ctx_seq
    n_lat = n_lat_b * lat_seq
    n = n_ctx + n_lat
    l = 0

    x_c = x_prompt.reshape(n_ctx, D_MODEL)
    x_l = x_sample.reshape(n_lat, D_MODEL)

    c_rows = jnp.zeros((SUBLANES, D_MODEL), F32).at[0].set(c_ctx).at[1:1 + n_lat_b].set(c)
    mod = _modulation(c_rows, w_ada[l], b_ada[l][None, :])
    mod3 = mod.reshape(SUBLANES, 1, 6 * D_MODEL)

    wi = w_in[l]
    o_q, o_k, o_v, o_gq, o_gk, o_gv, o_gg, o_rf, o_rb, o_end = np.cumsum(
        [0, ATT_WIDTH, KV_HEADS * HEAD_DIM, KV_HEADS * HEAD_DIM, GLA_KW, GLA_KW, GLA_WIDTH, GLA_WIDTH,
         GLA_GATE_RANK, GLA_GATE_RANK])
    w_tok = jnp.concatenate([
        _dup_heads(wi[:, o_k:o_v]), wi[:, o_v:o_gq], wi[:, o_gq:o_gk],
        wi[:, o_gv:o_gg], wi[:, o_gg:o_rf], wi[:, o_rf:o_end],
        jnp.zeros((D_MODEL, LANES - 2 * GLA_GATE_RANK), F32)], axis=1).astype(BF16)
    w_tr = jnp.concatenate([wi[:, o_q:o_k], wi[:, o_v:o_gq], wi[:, o_gk:o_gv], wi[:, o_rf:o_end]],
                           axis=1).T.astype(BF16)
    rep = LANES // HEAD_DIM
    qn = q_norm[l][:, None]
    kn = jnp.tile(k_norm[l], rep)[None, :]
    seg = jnp.asarray(np.kron(np.eye(rep), np.ones((HEAD_DIM, HEAD_DIM))), BF16)
    wa = jnp.zeros((LANES, 2 * GLA_KW), F32)
    wa = wa.at[0:GLA_GATE_RANK, 0:GLA_KW].set(gla_wa_fwd[l])
    wa = wa.at[GLA_GATE_RANK:2 * GLA_GATE_RANK, GLA_KW:].set(gla_wa_bwd[l])
    ba = jnp.concatenate([gla_ba_fwd[l], gla_ba_bwd[l]])[None, :]
    wat = wa[0:2 * GLA_GATE_RANK, :].T
    bat = ba.T
    cos_t, sa_t, sb_t, cos_tr, sin_tr = _rope_tables(lat_seq, TOK_TILE)

    (qt, k_dup, vt, k32, v32, gq, gv, gg, la, gkt, lat) = _in_projection(
        x_c, x_l, mod3, w_tok, w_tr, qn, kn, cos_t, sa_t, sb_t, cos_tr, sin_tr, seg, wa, ba, wat, bat,
        lat_seq // TOK_TILE)

    ck = _dup_heads(cache_k[:, l].reshape(n_lat_b, -1, KV_HEADS * HEAD_DIM)).astype(BF16)
    cvt = cache_v[:, l].reshape(n_lat_b, -1, KV_HEADS * HEAD_DIM).transpose(0, 2, 1).astype(BF16)
    att_c = _attention(qt, k_dup, vt, None, 0, n_ctx_b, ctx_seq)
    att_l = _attention(qt, k_dup, vt, (ck, cvt), n_ctx, n_lat_b, lat_seq)

    gconst, levels_of = _gla_constants()
    to_dev = lambda t: (jnp.asarray(t[0], BF16), jnp.asarray(t[1], BF16), jnp.asarray(t[2], F32))
    bd = jnp.asarray(np.kron(np.eye(GLA_HEADS), np.ones((GLA_DK, GLA_DV))), BF16)
    vbd = jnp.asarray(np.kron(np.eye(GLA_HEADS), np.ones((GLA_CHUNK, GLA_DV))), BF16)
    consts = ((to_dev(gconst["f"]), to_dev(gconst["b"])), levels_of, bd, vbd)
    s_zero = jnp.zeros((n_ctx_b, GLA_HEADS, GLA_DK, GLA_DV), F32)
    of_c, ob_c, sf_new, sb_new = _gla(gq, la, gkt, lat, gv, s_zero, s_zero, consts, 0, n_ctx_b, ctx_seq)
    of_l, ob_l, _, _ = _gla(gq, la, gkt, lat, gv, state_gla_fwd[:, l], state_gla_bwd[:, l], consts,
                            n_ctx, n_lat_b, lat_seq)

    sw13 = jnp.concatenate([sh_w1[l], sh_w3[l]], axis=1).astype(BF16)
    base, u2_rows, logits_t = _out_projection(
        att_c, att_l, of_c, of_l, ob_c, ob_l, gg, x_c, x_l, mod3, w_out[l].astype(BF16),
        gla_norm[l][None, :], ln1_g[l][None, :], ln1_b[l][None, :], w_router[l].T.astype(BF16), sw13,
        sh_w2[l].astype(BF16), lat_seq // TOK_TILE)

    upper = jnp.asarray(np.triu(np.ones((TOK_TILE, TOK_TILE)), 1), BF16)
    idx_t, w_t, pos_t, counts = _route(logits_t, router_bias[l][:, None], upper)
    n_blocks = n * TOP_K // MOE_ROWS + N_EXPERTS
    n_blocks_pad = -(-n_blocks // LANES) * LANES
    lower = jnp.asarray(np.tril(np.ones((N_EXPERTS, N_EXPERTS)), -1), F32)
    dest_t, bexp, bval, nused = _destinations(counts, lower, idx_t, pos_t, n_blocks_pad)
    dest_flat = dest_t.T.reshape(-1)
    w_flat = w_t.T.reshape(-1)

    row_src = _invert_rows(dest_flat, n_blocks * MOE_ROWS)
    yt = _moe_experts(bexp.reshape(-1), bval.reshape(-1), nused.reshape(-1)[0:1], u2_rows, row_src,
                      exp_w1[l], exp_w3[l], exp_w2[l], n_blocks, n)
    y_c, y_l = _combine(w_flat, yt, base, mod3, ln2_g[l][None, :], ln2_b[l][None, :], n_ctx, lat_seq)

    y_prompt = y_c.reshape(n_ctx_b, ctx_seq, D_MODEL)
    y_sample = y_l.reshape(n_lat_b, lat_seq, D_MODEL)
    new_cache_k = k32.reshape(n_ctx_b, 1, ctx_seq, KV_HEADS, HEAD_DIM)
    new_cache_v = v32.reshape(n_ctx_b, 1, ctx_seq, KV_HEADS, HEAD_DIM)
    return (y_prompt, y_sample, new_cache_k, new_cache_v, sf_new[:, None], sb_new[:, None])
```

```python
import functools

import numpy as np
import jax
import jax.numpy as jnp
from jax import lax
from jax.experimental import pallas as pl
from jax.experimental.pallas import tpu as pltpu
from jax.experimental.pallas import tpu_sc as plsc

F32 = jnp.float32
BF16 = jnp.bfloat16
I32 = jnp.int32

D_MODEL = 1024
GRID_W = 64
HEAD_DIM = 64
N_HEADS = 8
KV_HEADS = 2
ATT_WIDTH = N_HEADS * HEAD_DIM
ATT_SCALE = HEAD_DIM ** -0.5
LOG2_E = 1.4426950408889634
ROPE_AXIS_DIM = HEAD_DIM // 2
ROPE_THETA = 10000.0
GLA_HEADS = 4
GLA_DK = 64
GLA_DV = 128
GLA_WIDTH = GLA_HEADS * GLA_DV
GLA_KW = GLA_HEADS * GLA_DK
GLA_GATE_RANK = 16
GLA_TAU = 16.0
N_EXPERTS = 256
TOP_K = 8
EXPERT_FF = 256
SHARED_FF = 256
ROUTED_SCALE = 2.5
DEPTH = 1
ALPHA = (2.0 * DEPTH) ** 0.25
EPS = 1e-6

LANES = 128
SUBLANES = 8
ROW_CHUNKS = D_MODEL // LANES
VMEM_LIMIT = 56 * 1024 * 1024

TOK_TILE = 512
ATT_TQ = 128
GLA_CHUNK = 128
GLA_LEVELS = ((32, 128), (8, 32), (2, 8), (1, 2))
MOE_ROWS = 256
MOE_VMEM_LIMIT = 62 * 1024 * 1024
COMB_TILE = 128
HIGHEST = lax.Precision.HIGHEST


def _cparams(n_axes):
    return pltpu.CompilerParams(dimension_semantics=("arbitrary",) * n_axes,
                                vmem_limit_bytes=VMEM_LIMIT)


def _silu(x):
    return x * jax.nn.sigmoid(x)


def _log_sigmoid(x):
    return jnp.minimum(x, 0.0) - jnp.log(1.0 + jnp.exp(-jnp.abs(x)))


def _load_row_tiles(ref, n_rows, row0=0):
    return jnp.concatenate(
        [ref[pl.ds(row0 * ROW_CHUNKS + cidx, n_rows, stride=ROW_CHUNKS), :] for cidx in range(ROW_CHUNKS)],
        axis=1)


def _store_row_tiles(ref, x):
    for cidx in range(ROW_CHUNKS):
        ref[pl.ds(cidx, x.shape[0], stride=ROW_CHUNKS), :] = x[:, cidx * LANES:(cidx + 1) * LANES]


def _layer_norm(z, g, b):
    mu = jnp.mean(z, axis=-1, keepdims=True)
    zc = z - mu
    var = jnp.mean(zc * zc, axis=-1, keepdims=True)
    return zc * lax.rsqrt(var + EPS) * g + b


def _mod_kernel(c_ref, w_ref, b_ref, o_ref):
    s = _silu(c_ref[...]).astype(BF16)
    o_ref[...] = jnp.dot(s, w_ref[...].astype(BF16), preferred_element_type=F32) + b_ref[...]


def _modulation(c_rows, w_ada, b_ada):
    n_cols = w_ada.shape[1]
    tn = 512
    return pl.pallas_call(
        _mod_kernel,
        grid=(n_cols // tn,),
        in_specs=[pl.BlockSpec((SUBLANES, D_MODEL), lambda j: (0, 0)),
                  pl.BlockSpec((D_MODEL, tn), lambda j: (0, j)),
                  pl.BlockSpec((1, tn), lambda j: (0, j))],
        out_specs=pl.BlockSpec((SUBLANES, tn), lambda j: (0, j)),
        out_shape=jax.ShapeDtypeStruct((SUBLANES, n_cols), F32),
        compiler_params=_cparams(1),
        name="modulation",
    )(c_rows, w_ada, b_ada)


_C_K = 0
_C_V = _C_K + 2 * LANES
_C_GQ = _C_V + KV_HEADS * HEAD_DIM
_C_GV = _C_GQ + GLA_KW
_C_GG = _C_GV + GLA_WIDTH
_C_RA = _C_GG + GLA_WIDTH
_C_END = _C_RA + LANES
_R_Q = 0
_R_V = _R_Q + ATT_WIDTH
_R_GK = _R_V + KV_HEADS * HEAD_DIM
_R_RA = _R_GK + GLA_KW
_R_END = _R_RA + 2 * GLA_GATE_RANK


def _inproj_kernel(xc_ref, xl_ref, mod_ref, w_ref, wt_ref, qn_ref, kn_ref, cos_ref, sa_ref, sb_ref,
                   cost_ref, sint_ref, seg_ref, wa_ref, ba_ref, wat_ref, bat_ref,
                   qt_ref, k_ref, vt_ref, k32_ref, v32_ref, gq_ref, gv_ref, gg_ref,
                   la_ref, gkt_ref, lat_ref, *, n_ctx_tiles):
    i = pl.program_id(0)
    m = mod_ref[...]
    shift1 = m[:, 0:D_MODEL]
    scale1 = m[:, D_MODEL:2 * D_MODEL]
    x = jnp.where(i < n_ctx_tiles, xc_ref[...], xl_ref[...])
    u = (x * (1.0 + scale1) + shift1).astype(BF16)

    cos = cos_ref[...]
    sin_a = sa_ref[...]
    sin_b = sb_ref[...]
    seg = seg_ref[...]
    lane = lax.broadcasted_iota(I32, (u.shape[0], LANES), 1)
    low = lane < HEAD_DIM

    def proj(c0, c1):
        return jnp.dot(u, w_ref[:, c0:c1], preferred_element_type=F32)

    def head_norm(blk, gain):
        ss = jnp.dot((blk * blk).astype(BF16), seg, preferred_element_type=F32) * (1.0 / HEAD_DIM)
        return blk * lax.rsqrt(ss + EPS) * gain

    def rope(blk):
        return (blk * cos + pltpu.roll(blk, LANES - ROPE_AXIS_DIM // 2, 1) * sin_a
                + pltpu.roll(blk, ROPE_AXIS_DIM // 2, 1) * sin_b)

    pk = proj(_C_K, _C_V)
    kn = [head_norm(pk[:, j * LANES:(j + 1) * LANES], kn_ref[...]) for j in range(KV_HEADS)]
    for j in range(KV_HEADS):
        k_ref[:, j * LANES:(j + 1) * LANES] = rope(kn[j]).astype(BF16)

    @pl.when(i < n_ctx_tiles)
    def _():
        k32_ref[...] = jnp.where(low, kn[0], kn[1])
        v32_ref[...] = proj(_C_V, _C_GQ)

    gq_ref[...] = proj(_C_GQ, _C_GV) * (GLA_DK ** -0.5)
    gv_ref[...] = proj(_C_GV, _C_GG).astype(BF16)
    gg_ref[...] = proj(_C_GG, _C_RA).astype(BF16)

    ra = proj(_C_RA, _C_END)
    pre = jnp.dot(ra, wa_ref[...], precision=HIGHEST, preferred_element_type=F32) + ba_ref[...]
    la_ref[...] = _log_sigmoid(pre) * (1.0 / GLA_TAU)

    pt = lax.dot_general(wt_ref[...], u, (((1,), (1,)), ((), ())), preferred_element_type=F32)
    cos_t = cost_ref[...]
    sin_t = sint_ref[...]
    quarter = ROPE_AXIS_DIM // 2
    for h in range(N_HEADS):
        blk = pt[_R_Q + h * HEAD_DIM:_R_Q + (h + 1) * HEAD_DIM, :]
        ms = jnp.mean(blk * blk, axis=0, keepdims=True)
        qn = blk * lax.rsqrt(ms + EPS) * qn_ref[...]
        rot = jnp.concatenate([-qn[quarter:2 * quarter], qn[0:quarter],
                               -qn[3 * quarter:4 * quarter], qn[2 * quarter:3 * quarter]], axis=0)
        qt_ref[h * HEAD_DIM:(h + 1) * HEAD_DIM, :] = (
            (qn * cos_t + rot * sin_t) * (ATT_SCALE * LOG2_E)).astype(BF16)
    vt_ref[...] = pt[_R_V:_R_GK, :].astype(BF16)
    gkt_ref[...] = pt[_R_GK:_R_RA, :]
    rat = pt[_R_RA:_R_END, :]
    pre_t = jnp.dot(wat_ref[...], rat, precision=HIGHEST, preferred_element_type=F32) + bat_ref[...]
    lat_ref[...] = _log_sigmoid(pre_t) * (1.0 / GLA_TAU)


def _in_projection(x_c, x_l, mod3, w_tok, w_tr, qn, kn, cos_t, sa_t, sb_t, cos_tr, sin_tr, seg, wa, ba,
                   wat, bat, n_seq_tiles):
    n_ctx = x_c.shape[0]
    n = n_ctx + x_l.shape[0]
    tb = TOK_TILE
    n_ctx_tiles = n_ctx // tb
    n_tiles = n // tb
    n_rope_blocks = cos_t.shape[0] // tb - 1

    def mod_idx(i):
        return (jnp.where(i < n_ctx_tiles, 0, 1 + (i - n_ctx_tiles) // n_seq_tiles), 0, 0)

    def rope_blk(i):
        return jnp.where(i < n_ctx_tiles, n_rope_blocks, (i - n_ctx_tiles) % n_seq_tiles)

    def rope_idx(i):
        return (rope_blk(i), 0)

    def ctx_idx(i):
        return (jnp.minimum(i, n_ctx_tiles - 1), 0)

    tok = lambda w: pl.BlockSpec((tb, w), lambda i: (i, 0))
    full = lambda a: pl.BlockSpec(a.shape, lambda i: (0,) * a.ndim)
    tr = lambda r: pl.BlockSpec((r, tb), lambda i: (0, i))
    rope_tr = pl.BlockSpec((HEAD_DIM, tb), lambda i: (0, rope_blk(i)))
    out_shapes = (
        jax.ShapeDtypeStruct((ATT_WIDTH, n), BF16),
        jax.ShapeDtypeStruct((n, 2 * LANES), BF16),
        jax.ShapeDtypeStruct((KV_HEADS * HEAD_DIM, n), BF16),
        jax.ShapeDtypeStruct((n_ctx, LANES), F32),
        jax.ShapeDtypeStruct((n_ctx, LANES), F32),
        jax.ShapeDtypeStruct((n, GLA_KW), F32),
        jax.ShapeDtypeStruct((n, GLA_WIDTH), BF16),
        jax.ShapeDtypeStruct((n, GLA_WIDTH), BF16),
        jax.ShapeDtypeStruct((n, 2 * GLA_KW), F32),
        jax.ShapeDtypeStruct((GLA_KW, n), F32),
        jax.ShapeDtypeStruct((2 * GLA_KW, n), F32),
    )
    out_specs = (tr(ATT_WIDTH), tok(2 * LANES), tr(KV_HEADS * HEAD_DIM),
                 pl.BlockSpec((tb, LANES), ctx_idx), pl.BlockSpec((tb, LANES), ctx_idx),
                 tok(GLA_KW), tok(GLA_WIDTH), tok(GLA_WIDTH), tok(2 * GLA_KW),
                 tr(GLA_KW), tr(2 * GLA_KW))
    in_specs = [pl.BlockSpec((tb, D_MODEL), ctx_idx),
                pl.BlockSpec((tb, D_MODEL), lambda i: (jnp.maximum(i - n_ctx_tiles, 0), 0)),
                pl.BlockSpec((None, 1, mod3.shape[2]), mod_idx),
                full(w_tok), full(w_tr), full(qn), full(kn),
                pl.BlockSpec((tb, LANES), rope_idx), pl.BlockSpec((tb, LANES), rope_idx),
                pl.BlockSpec((tb, LANES), rope_idx), rope_tr, rope_tr,
                full(seg), full(wa), full(ba), full(wat), full(bat)]
    return pl.pallas_call(
        functools.partial(_inproj_kernel, n_ctx_tiles=n_ctx_tiles),
        grid=(n_tiles,), in_specs=in_specs, out_specs=out_specs, out_shape=out_shapes,
        compiler_params=_cparams(1), name="in_projection",
    )(x_c, x_l, mod3, w_tok, w_tr, qn, kn, cos_t, sa_t, sb_t, cos_tr, sin_tr, seg, wa, ba, wat, bat)


def _attention_kernel(*refs, n_kv_parts):
    qt_ref = refs[0]
    k_refs = refs[1:1 + n_kv_parts]
    vt_refs = refs[1 + n_kv_parts:1 + 2 * n_kv_parts]
    o_ref = refs[1 + 2 * n_kv_parts]
    tq = qt_ref.shape[1]
    group = N_HEADS // KV_HEADS
    for kv in range(KV_HEADS):
        heads = range(kv * group, (kv + 1) * group)
        q_grp = jnp.concatenate([qt_ref[h * HEAD_DIM:(h + 1) * HEAD_DIM, :] for h in heads], axis=1)
        rhs = jnp.concatenate([q_grp, jnp.zeros_like(q_grp)], axis=0)
        s = [jnp.dot(k[:, kv * LANES:(kv + 1) * LANES], rhs, preferred_element_type=F32)
             for k in k_refs]
        mx = functools.reduce(jnp.maximum, [jnp.max(x, axis=0, keepdims=True) for x in s])
        pr = [jnp.exp2(x - mx) for x in s]
        den = functools.reduce(jnp.add, [jnp.sum(x, axis=0, keepdims=True) for x in pr])
        acc = functools.reduce(jnp.add, [
            jnp.dot(vt[kv * HEAD_DIM:(kv + 1) * HEAD_DIM, :], x.astype(BF16),
                    preferred_element_type=F32) for x, vt in zip(pr, vt_refs)])
        out = (acc / den).astype(BF16)
        for j, h in enumerate(heads):
            o_ref[h * HEAD_DIM:(h + 1) * HEAD_DIM, :] = out[:, j * tq:(j + 1) * tq]


def _attention(qt, k, vt, extra_kv, row0, n_batch, seq):
    tq = ATT_TQ
    n_q = seq // tq
    q_blk0 = row0 // tq
    kv_blk0 = row0 // seq
    in_specs = [pl.BlockSpec((ATT_WIDTH, tq), lambda b, i: (0, q_blk0 + b * n_q + i))]
    k_spec = pl.BlockSpec((seq, 2 * LANES), lambda b, i: (kv_blk0 + b, 0))
    vt_spec = pl.BlockSpec((KV_HEADS * HEAD_DIM, seq), lambda b, i: (0, kv_blk0 + b))
    args_k, args_v, specs_k, specs_v = [k], [vt], [k_spec], [vt_spec]
    if extra_kv is not None:
        ck, cvt = extra_kv
        args_k.append(ck)
        args_v.append(cvt)
        specs_k.append(pl.BlockSpec((None, ck.shape[1], 2 * LANES), lambda b, i: (b, 0, 0)))
        specs_v.append(pl.BlockSpec((None, KV_HEADS * HEAD_DIM, cvt.shape[2]), lambda b, i: (b, 0, 0)))
    return pl.pallas_call(
        functools.partial(_attention_kernel, n_kv_parts=len(args_k)),
        grid=(n_batch, n_q),
        in_specs=in_specs + specs_k + specs_v,
        out_specs=pl.BlockSpec((ATT_WIDTH, tq), lambda b, i: (0, b * n_q + i)),
        out_shape=jax.ShapeDtypeStruct((ATT_WIDTH, n_batch * seq), BF16),
        compiler_params=_cparams(2), name="attention",
    )(qt, *args_k, *args_v)


def _gla_constants():
    c = GLA_CHUNK
    idx = np.arange(c)
    q_mats, k_mats, masks, levels_of = [], [], [], []
    for li, (s, p) in enumerate(GLA_LEVELS):
        start = (idx // s) * s
        end = start + s - 1
        k_mats.append(((idx[None, :] > idx[:, None]) & (idx[None, :] <= end[:, None])))
        for d in range(p // s - 1):
            lo = np.maximum(start - d * s, 0)
            q_mats.append((idx[None, :] >= lo[:, None]) & (idx[None, :] <= idx[:, None]))
            masks.append((idx[:, None] // p == idx[None, :] // p)
                         & (idx[:, None] // s - idx[None, :] // s - 1 == d))
            levels_of.append(li)
    masks.append(np.eye(c, dtype=bool))
    levels_of.append(len(GLA_LEVELS) - 1)
    q_mats.append(idx[None, :] <= idx[:, None])
    k_mats = k_mats[:-1]
    k_mats.append(idx[None, :] > idx[:, None])
    k_mats.append(np.ones((c, c), bool))
    out = {}
    for name, flip in (("f", False), ("b", True)):
        f = (lambda a: a[::-1, ::-1]) if flip else (lambda a: a)
        lq = np.concatenate([f(a) for a in q_mats], axis=0).astype(np.float32)
        lkt = np.concatenate([f(a).T for a in k_mats], axis=1).astype(np.float32)
        mk = np.stack([np.tile(f(a), (1, GLA_HEADS)) for a in masks]).astype(np.float32)
        out[name] = (np.concatenate([lq, lq], axis=1), np.concatenate([lkt, lkt], axis=0), mk)
    return out, tuple(levels_of)


def _gla_direction(q, g, gkt, gt, v, lq2, lkt2, masks_ref, bd, vbd, s_ref, levels_of):
    c = GLA_CHUNK
    n_var = len(levels_of)
    n_lev = len(GLA_LEVELS)
    g_hi = g.astype(BF16)
    g_lo = (g - g_hi.astype(F32)).astype(BF16)
    fq = jnp.dot(lq2, jnp.concatenate([g_hi, g_lo], axis=0), preferred_element_type=F32)
    gt_hi = gt.astype(BF16)
    gt_lo = (gt - gt_hi.astype(F32)).astype(BF16)
    fk = jnp.dot(jnp.concatenate([gt_hi, gt_lo], axis=1), lkt2, preferred_element_type=F32)

    def key_factor(f):
        return gkt * jnp.exp(fk[:, f * c:(f + 1) * c])

    q_var = [(q * jnp.exp(fq[vi * c:(vi + 1) * c, :])).astype(BF16) for vi in range(n_var - 1)]
    q_var.append(q.astype(BF16))
    a = jnp.zeros((c, GLA_HEADS * c), F32)
    for li in range(n_lev):
        kt = (key_factor(li) if li < n_lev - 1 else gkt).astype(BF16)
        xt = jnp.concatenate([kt] * GLA_HEADS, axis=1) * bd
        vis = [vi for vi in range(n_var) if levels_of[vi] == li]
        res = jnp.dot(jnp.concatenate([q_var[vi] for vi in vis], axis=0), xt,
                      preferred_element_type=F32)
        for r, vi in enumerate(vis):
            a = a + masks_ref[vi] * res[r * c:(r + 1) * c, :]
    q_in = (q * jnp.exp(fq[(n_var - 1) * c:n_var * c, :])).astype(BF16)
    state = s_ref[...]
    v_bd = jnp.concatenate([v] * GLA_HEADS, axis=0) * vbd
    o = (jnp.dot(q_in, state.astype(BF16), preferred_element_type=F32)
         + jnp.dot(a.astype(BF16), v_bd, preferred_element_type=F32))
    k_out = key_factor(n_lev - 1).astype(BF16)
    e_tot = jnp.exp(fk[:, n_lev * c:(n_lev + 1) * c])
    upd = jnp.dot(k_out, v, preferred_element_type=F32)
    s_ref[...] = (state * jnp.concatenate([e_tot] * (GLA_WIDTH // c), axis=1)
                  + upd * bd.astype(F32))
    return o


def _gla_kernel(gq_f, la_f, gkt_f, lat_f, gv_f, gq_b, la_b, gkt_b, lat_b, gv_b,
                s0f_ref, s0b_ref, lq2f, lkt2f, mkf, lq2b, lkt2b, mkb, bd_ref, vbd_ref,
                of_ref, ob_ref, sf_ref, sb_ref, st_f, st_b, *, levels_of):
    n = pl.program_id(1)

    @pl.when(n == 0)
    def _():
        st_f[...] = jnp.zeros_like(st_f)
        st_b[...] = jnp.zeros_like(st_b)
        for h in range(GLA_HEADS):
            rows = slice(h * GLA_DK, (h + 1) * GLA_DK)
            cols = slice(h * GLA_DV, (h + 1) * GLA_DV)
            st_f[rows, cols] = s0f_ref[h]
            st_b[rows, cols] = s0b_ref[h]

    bd = bd_ref[...]
    vbd = vbd_ref[...]
    of_ref[...] = _gla_direction(gq_f[...], la_f[...], gkt_f[...], lat_f[...], gv_f[...],
                                 lq2f[...], lkt2f[...], mkf, bd, vbd, st_f, levels_of)
    ob_ref[...] = _gla_direction(gq_b[...], la_b[...], gkt_b[...], lat_b[...], gv_b[...],
                                 lq2b[...], lkt2b[...], mkb, bd, vbd, st_b, levels_of)

    @pl.when(n == pl.num_programs(1) - 1)
    def _():
        for h in range(GLA_HEADS):
            rows = slice(h * GLA_DK, (h + 1) * GLA_DK)
            cols = slice(h * GLA_DV, (h + 1) * GLA_DV)
            sf_ref[h] = st_f[rows, cols]
            sb_ref[h] = st_b[rows, cols]


def _gla(gq, la, gkt, lat, gv, s0f, s0b, consts, row0, n_batch, seq):
    (cf, cb), levels_of, bd, vbd = consts
    c = GLA_CHUNK
    nc = seq // c
    blk0 = row0 // c
    n_la_blocks_b = 1
    fwd = lambda b, n: blk0 + b * nc + n
    bwd = lambda b, n: blk0 + b * nc + (nc - 1 - n)

    def tok(w, which, col=0):
        return pl.BlockSpec((c, w), lambda b, n: (which(b, n), col))

    def tr(r, which, row=0):
        return pl.BlockSpec((r, c), lambda b, n: (row, which(b, n)))

    full = lambda a: pl.BlockSpec(a.shape, lambda b, n: (0,) * a.ndim)
    st_spec = pl.BlockSpec((None, GLA_HEADS, GLA_DK, GLA_DV), lambda b, n: (b, 0, 0, 0))
    in_specs = [tok(GLA_KW, fwd), tok(GLA_KW, fwd, 0), tr(GLA_KW, fwd), tr(GLA_KW, fwd, 0),
                tok(GLA_WIDTH, fwd),
                tok(GLA_KW, bwd), tok(GLA_KW, bwd, n_la_blocks_b), tr(GLA_KW, bwd),
                tr(GLA_KW, bwd, 1), tok(GLA_WIDTH, bwd),
                st_spec, st_spec,
                full(cf[0]), full(cf[1]), full(cf[2]), full(cb[0]), full(cb[1]), full(cb[2]),
                full(bd), full(vbd)]
    out_specs = (pl.BlockSpec((c, GLA_WIDTH), lambda b, n: (b * nc + n, 0)),
                 pl.BlockSpec((c, GLA_WIDTH), lambda b, n: (b * nc + (nc - 1 - n), 0)),
                 st_spec, st_spec)
    out_shape = (jax.ShapeDtypeStruct((n_batch * seq, GLA_WIDTH), F32),
                 jax.ShapeDtypeStruct((n_batch * seq, GLA_WIDTH), F32),
                 jax.ShapeDtypeStruct((n_batch, GLA_HEADS, GLA_DK, GLA_DV), F32),
                 jax.ShapeDtypeStruct((n_batch, GLA_HEADS, GLA_DK, GLA_DV), F32))
    return pl.pallas_call(
        functools.partial(_gla_kernel, levels_of=levels_of),
        grid=(n_batch, nc), in_specs=in_specs, out_specs=out_specs, out_shape=out_shape,
        scratch_shapes=[pltpu.VMEM((GLA_KW, GLA_WIDTH), F32), pltpu.VMEM((GLA_KW, GLA_WIDTH), F32)],
        compiler_params=_cparams(2), name="gla",
    )(gq, la, gkt, lat, gv, gq, la, gkt, lat, gv, s0f, s0b,
      cf[0], cf[1], cf[2], cb[0], cb[1], cb[2], bd, vbd)


def _outproj_kernel(attc_ref, attl_ref, ofc_ref, ofl_ref, obc_ref, obl_ref, gg_ref, xc_ref, xl_ref,
                    mod_ref, wo_ref, gn_ref, l1g_ref, l1b_ref, wrt_ref, sw13_ref, sw2_ref,
                    base_ref, u2_ref, lg_ref, *, n_ctx_tiles):
    is_ctx = pl.program_id(0) < n_ctx_tiles
    pick = lambda a_ref, b_ref: jnp.where(is_ctx, a_ref[...], b_ref[...])
    m = mod_ref[...]
    gate1 = m[:, 2 * D_MODEL:3 * D_MODEL]
    shift2 = m[:, 3 * D_MODEL:4 * D_MODEL]
    scale2 = m[:, 4 * D_MODEL:5 * D_MODEL]
    gate2 = m[:, 5 * D_MODEL:6 * D_MODEL]
    og = pick(ofc_ref, ofl_ref) + pick(obc_ref, obl_ref)
    gg = gg_ref[...].astype(F32)
    parts = []
    for h in range(GLA_HEADS):
        blk = og[:, h * GLA_DV:(h + 1) * GLA_DV]
        ms = jnp.mean(blk * blk, axis=-1, keepdims=True)
        nb = blk * lax.rsqrt(ms + EPS) * gn_ref[...]
        parts.append((nb * _silu(gg[:, h * GLA_DV:(h + 1) * GLA_DV])).astype(BF16))
    att_t = pick(attc_ref, attl_ref)
    hmix = (lax.dot_general(att_t, wo_ref[0:ATT_WIDTH, :], (((0,), (0,)), ((), ())),
                            preferred_element_type=F32)
            + jnp.dot(jnp.concatenate(parts, axis=1), wo_ref[ATT_WIDTH:, :],
                      preferred_element_type=F32))
    x1 = _layer_norm(ALPHA * pick(xc_ref, xl_ref) + gate1 * hmix, l1g_ref[...], l1b_ref[...])
    u2 = x1 * (1.0 + scale2) + shift2
    u2b = u2.astype(BF16)
    lg_ref[...] = lax.dot_general(wrt_ref[...], u2b, (((1,), (1,)), ((), ())),
                                  preferred_element_type=F32)
    ab = jnp.dot(u2b, sw13_ref[...], preferred_element_type=F32)
    hid = (_silu(ab[:, 0:SHARED_FF]) * ab[:, SHARED_FF:2 * SHARED_FF]).astype(BF16)
    shared = jnp.dot(hid, sw2_ref[...], preferred_element_type=F32)
    base_ref[...] = ALPHA * x1 + gate2 * shared
    _pack_rows(u2_ref, u2)


def _out_projection(att_c, att_l, of_c, of_l, ob_c, ob_l, gg, x_c, x_l, mod3, wo, gn, l1g, l1b, wrt,
                    sw13, sw2, n_seq_tiles):
    n_ctx = x_c.shape[0]
    n = n_ctx + x_l.shape[0]
    tb = TOK_TILE
    n_ctx_tiles = n_ctx // tb

    def mod_idx(i):
        return (jnp.where(i < n_ctx_tiles, 0, 1 + (i - n_ctx_tiles) // n_seq_tiles), 0, 0)

    ctx_blk = lambda i: jnp.minimum(i, n_ctx_tiles - 1)
    lat_blk = lambda i: jnp.maximum(i - n_ctx_tiles, 0)
    tok = lambda w: pl.BlockSpec((tb, w), lambda i: (i, 0))
    tok_c = lambda w: pl.BlockSpec((tb, w), lambda i: (ctx_blk(i), 0))
    tok_l = lambda w: pl.BlockSpec((tb, w), lambda i: (lat_blk(i), 0))
    full = lambda a: pl.BlockSpec(a.shape, lambda i: (0,) * a.ndim)
    return pl.pallas_call(
        functools.partial(_outproj_kernel, n_ctx_tiles=n_ctx_tiles),
        grid=(n // tb,),
        in_specs=[pl.BlockSpec((ATT_WIDTH, tb), lambda i: (0, ctx_blk(i))),
                  pl.BlockSpec((ATT_WIDTH, tb), lambda i: (0, lat_blk(i))),
                  tok_c(GLA_WIDTH), tok_l(GLA_WIDTH), tok_c(GLA_WIDTH), tok_l(GLA_WIDTH),
                  tok(GLA_WIDTH), tok_c(D_MODEL), tok_l(D_MODEL),
                  pl.BlockSpec((None, 1, mod3.shape[2]), mod_idx),
                  full(wo), full(gn), full(l1g), full(l1b), full(wrt), full(sw13), full(sw2)],
        out_specs=(tok(D_MODEL),
                   pl.BlockSpec((tb * PACK_CHUNKS, LANES), lambda i: (i, 0)),
                   pl.BlockSpec((N_EXPERTS, tb), lambda i: (0, i))),
        out_shape=(jax.ShapeDtypeStruct((n, D_MODEL), F32),
                   jax.ShapeDtypeStruct((n * PACK_CHUNKS, LANES), U32),
                   jax.ShapeDtypeStruct((N_EXPERTS, n), F32)),
        compiler_params=_cparams(1), name="out_projection",
    )(att_c, att_l, of_c, of_l, ob_c, ob_l, gg, x_c, x_l, mod3, wo, gn, l1g, l1b, wrt, sw13, sw2)


def _route_kernel(lg_ref, bias_ref, upper_ref, idx_ref, w_ref, pos_ref, cnt_ref, run_ref):
    i = pl.program_id(0)

    @pl.when(i == 0)
    def _():
        run_ref[...] = jnp.zeros_like(run_ref)

    s = jax.nn.sigmoid(lg_ref[...])
    work = s + bias_ref[...]
    rows = lax.broadcasted_iota(I32, s.shape, 0).astype(F32)
    sel = jnp.zeros(s.shape, F32)
    idxs, vals = [], []
    for _ in range(TOP_K):
        mx = jnp.max(work, axis=0, keepdims=True)
        idx = jnp.min(jnp.where(work == mx, rows, float(N_EXPERTS)), axis=0, keepdims=True)
        hit = rows == idx
        vals.append(jnp.sum(jnp.where(hit, s, 0.0), axis=0, keepdims=True))
        idxs.append(idx)
        sel = jnp.where(hit, 1.0, sel)
        work = jnp.where(hit, -jnp.inf, work)
    den = functools.reduce(jnp.add, vals)
    rank = jnp.dot(sel.astype(BF16), upper_ref[...], preferred_element_type=F32) + run_ref[:, 0:1]
    for k in range(TOP_K):
        idx_ref[k:k + 1, :] = idxs[k].astype(I32)
        w_ref[k:k + 1, :] = vals[k] / den * ROUTED_SCALE
        pos_ref[k:k + 1, :] = jnp.sum(jnp.where(rows == idxs[k], rank, 0.0), axis=0,
                                      keepdims=True).astype(I32)
    run_ref[...] = run_ref[...] + jnp.sum(sel, axis=1, keepdims=True)
    cnt_ref[...] = run_ref[...]


def _route(logits_t, bias_col, upper):
    n = logits_t.shape[1]
    tt = TOK_TILE
    row = lambda dt: jax.ShapeDtypeStruct((TOP_K, n), dt)
    blk = pl.BlockSpec((TOP_K, tt), lambda i: (0, i))
    return pl.pallas_call(
        _route_kernel,
        grid=(n // tt,),
        in_specs=[pl.BlockSpec((N_EXPERTS, tt), lambda i: (0, i)),
                  pl.BlockSpec((N_EXPERTS, 1), lambda i: (0, 0)),
                  pl.BlockSpec((tt, tt), lambda i: (0, 0))],
        out_specs=(blk, blk, blk, pl.BlockSpec((N_EXPERTS, LANES), lambda i: (0, 0))),
        out_shape=(row(I32), row(F32), row(I32), jax.ShapeDtypeStruct((N_EXPERTS, LANES), F32)),
        scratch_shapes=[pltpu.VMEM((N_EXPERTS, LANES), F32)],
        compiler_params=_cparams(1), name="route",
    )(logits_t, bias_col, upper)


def _dest_kernel(cnt_ref, lower_ref, idx_ref, pos_ref, dest_ref, bexp_ref, bval_ref, nused_ref):
    cnt = cnt_ref[...]
    nblk = jnp.floor((cnt + (MOE_ROWS - 1)) * (1.0 / MOE_ROWS))
    bstart = jnp.dot(lower_ref[...], nblk, precision=HIGHEST, preferred_element_type=F32)
    bend = bstart + nblk
    pstart = bstart[:, 0:1] * MOE_ROWS
    rows = lax.broadcasted_iota(I32, (N_EXPERTS, idx_ref.shape[1]), 0)
    for k in range(TOP_K):
        hit = rows == idx_ref[k:k + 1, :]
        dest_ref[k:k + 1, :] = (jnp.sum(jnp.where(hit, pstart, 0.0), axis=0, keepdims=True)
                                .astype(I32) + pos_ref[k:k + 1, :])

    @pl.when(pl.program_id(0) == 0)
    def _():
        nb = bexp_ref.shape[1]
        bid = lax.broadcasted_iota(I32, (N_EXPERTS, nb), 1).astype(F32)
        inside = jnp.logical_and(bid >= bstart[:, 0:1], bid < bend[:, 0:1])
        erow = lax.broadcasted_iota(I32, (N_EXPERTS, nb), 0).astype(F32)
        bexp_ref[...] = jnp.sum(jnp.where(inside, erow, 0.0), axis=0, keepdims=True).astype(I32)
        valid = jnp.clip(cnt[:, 0:1] - (bid - bstart[:, 0:1]) * MOE_ROWS, 0.0, float(MOE_ROWS))
        bval_ref[...] = jnp.sum(jnp.where(inside, valid, 0.0), axis=0, keepdims=True).astype(I32)
        nused_ref[...] = jnp.max(bend, axis=0, keepdims=True).astype(I32)


def _destinations(counts, lower, idx_t, pos_t, n_blocks_pad):
    n = idx_t.shape[1]
    tt = TOK_TILE
    blk = pl.BlockSpec((TOP_K, tt), lambda i: (0, i))
    one = lambda w: pl.BlockSpec((1, w), lambda i: (0, 0))
    return pl.pallas_call(
        _dest_kernel,
        grid=(n // tt,),
        in_specs=[pl.BlockSpec((N_EXPERTS, LANES), lambda i: (0, 0)),
                  pl.BlockSpec((N_EXPERTS, N_EXPERTS), lambda i: (0, 0)), blk, blk],
        out_specs=(blk, one(n_blocks_pad), one(n_blocks_pad), one(LANES)),
        out_shape=(jax.ShapeDtypeStruct((TOP_K, n), I32),
                   jax.ShapeDtypeStruct((1, n_blocks_pad), I32),
                   jax.ShapeDtypeStruct((1, n_blocks_pad), I32),
                   jax.ShapeDtypeStruct((1, LANES), I32)),
        compiler_params=_cparams(1), name="destinations",
    )(counts, lower, idx_t, pos_t)


SC_WINDOW = 128


def _invert_rows(dest_flat, n_rows):
    m = dest_flat.shape[0]
    mesh = plsc.VectorSubcoreMesh(core_axis_name="core", subcore_axis_name="subcore")

    @functools.partial(pl.kernel, out_type=jax.ShapeDtypeStruct((n_rows,), I32), mesh=mesh,
                       scratch_types=[])
    def invert(val_hbm, idx_hbm, out_hbm):
        def body(val_vmem, idx_vmem):
            pltpu.sync_copy(val_vmem.at[0], out_hbm.at[idx_vmem.at[0]])

        pltpu.emit_pipeline(
            body, grid=(m // SC_WINDOW,),
            in_specs=[pl.BlockSpec((1, SC_WINDOW), lambda i: (0, i)),
                      pl.BlockSpec((1, SC_WINDOW), lambda i: (0, i))],
            out_specs=[], core_axis_name=("core", "subcore"),
            dimension_semantics=(pltpu.PARALLEL,),
        )(val_hbm, idx_hbm)

    return invert(jnp.arange(m, dtype=I32).reshape(1, m), dest_flat.reshape(1, m))


PACK_CHUNKS = D_MODEL // (2 * LANES)
SRC_GROUP = 4
TOP_K_LOG2 = TOP_K.bit_length() - 1
PACK_CHUNKS_LOG2 = PACK_CHUNKS.bit_length() - 1
U32 = jnp.uint32


def _pack_rows(ref, x):
    bits = pltpu.bitcast(x.astype(BF16).astype(F32), U32)
    for s in range(PACK_CHUNKS):
        lo = bits[:, (2 * s) * LANES:(2 * s + 1) * LANES] >> 16
        hi = bits[:, (2 * s + 1) * LANES:(2 * s + 2) * LANES] & jnp.uint32(0xFFFF0000)
        ref[pl.ds(s, x.shape[0], stride=PACK_CHUNKS), :] = lo | hi


def _unpack_rows(ref, n_rows, row0=0):
    parts = []
    for s in range(PACK_CHUNKS):
        w = ref[pl.ds(row0 + s, n_rows, stride=PACK_CHUNKS), :]
        parts.append(pltpu.bitcast(w << 16, F32))
        parts.append(pltpu.bitcast(w & jnp.uint32(0xFFFF0000), F32))
    return jnp.concatenate(parts, axis=1).astype(BF16)


def _moe_kernel(bexp_ref, bval_ref, nused_ref, u2p_hbm, src_hbm, w1_ref, w3_ref, w2_ref, yt_hbm,
                u2p_vmem, w13_s, w2_s, xbuf, ybuf, src_smem, sem_in, sem_src, sem_out,
                *, n_tokens):
    b = pl.program_id(0)
    n_used = nused_ref[0]
    br = MOE_ROWS
    grp = SRC_GROUP * br
    trash0 = n_tokens * TOP_K

    def src_copy(g):
        return pltpu.make_async_copy(src_hbm.at[pl.ds(g * grp, grp)],
                                     src_smem.at[pl.ds(lax.rem(g, 2) * grp, grp)], sem_src)

    def out_wait(slot):
        pltpu.make_async_copy(ybuf.at[pl.ds(slot * br * ROW_CHUNKS, br * ROW_CHUNKS)],
                              yt_hbm.at[pl.ds(0, br * ROW_CHUNKS)], sem_out.at[slot]).wait()

    def src_base(blk):
        return lax.rem(blk // SRC_GROUP, 2) * grp + lax.rem(blk, SRC_GROUP) * br

    def scatter_row(blk_slot, sbase, valid, r):
        dst = jnp.where(r < valid, src_smem[sbase + r], trash0 + blk_slot * br + r)
        pltpu.make_async_copy(
            ybuf.at[pl.ds(pl.multiple_of((blk_slot * br + r) * ROW_CHUNKS, ROW_CHUNKS), ROW_CHUNKS)],
            yt_hbm.at[pl.ds(pl.multiple_of(dst * ROW_CHUNKS, ROW_CHUNKS), ROW_CHUNKS)],
            sem_out.at[blk_slot]).start()

    def gather_row(xslot, sbase, r):
        row = lax.shift_right_logical(src_smem[sbase + r], TOP_K_LOG2 - PACK_CHUNKS_LOG2)
        row = jnp.minimum(row & (-PACK_CHUNKS & 0x7FFFFFFF), (n_tokens - 1) * PACK_CHUNKS)
        dst = pl.multiple_of((xslot * br + r) * PACK_CHUNKS, PACK_CHUNKS)
        xbuf[pl.ds(dst, PACK_CHUNKS), :] = u2p_vmem[pl.ds(pl.multiple_of(row, PACK_CHUNKS), PACK_CHUNKS), :]

    @pl.when(b == 0)
    def _():
        cp = pltpu.make_async_copy(u2p_hbm, u2p_vmem, sem_in)
        cp.start()
        src_copy(0).start()
        ybuf[...] = jnp.zeros_like(ybuf)
        cp.wait()
        src_copy(0).wait()
        lax.fori_loop(0, br, lambda r, c: (gather_row(0, 0, r), c)[1], 0)

    @pl.when(b < n_used)
    def _():
        g = b // SRC_GROUP
        phase = lax.rem(b, SRC_GROUP)

        more = (g + 1) * SRC_GROUP < n_used

        @pl.when(jnp.logical_and(phase == 1, more))
        def _():
            src_copy(g + 1).start()

        @pl.when(jnp.logical_and(phase == SRC_GROUP - 1, more))
        def _():
            src_copy(g + 1).wait()

        e = bexp_ref[b]
        prev = bexp_ref[jnp.maximum(b - 1, 0)]

        @pl.when(jnp.logical_or(b == 0, e != prev))
        def _():
            w13_s[:, 0:EXPERT_FF] = w1_ref[...].astype(BF16)
            w13_s[:, EXPERT_FF:2 * EXPERT_FF] = w3_ref[...].astype(BF16)
            w2_s[...] = w2_ref[...].astype(BF16)

        valid = bval_ref[b]
        sbase = src_base(b)
        slot = lax.rem(b, 2)

        pb = jnp.maximum(b - 1, 0)
        p_valid = jnp.where(b > 0, bval_ref[pb], 0)
        p_base = src_base(pb)
        n_base = src_base(b + 1)
        for r in range(br):
            gather_row(1 - slot, n_base, r)
            scatter_row(1 - slot, p_base, p_valid, r)

        x = _unpack_rows(xbuf, br, slot * (br * PACK_CHUNKS))
        rows = lax.broadcasted_iota(I32, x.shape, 0)
        x = jnp.where(rows < valid, x, jnp.zeros_like(x))
        ab = jnp.dot(x, w13_s[...], preferred_element_type=F32)
        hid = (_silu(ab[:, 0:EXPERT_FF]) * ab[:, EXPERT_FF:2 * EXPERT_FF]).astype(BF16)
        y = jnp.dot(hid, w2_s[...], preferred_element_type=F32)

        @pl.when(b >= 1)
        def _():
            out_wait(slot)

        ybase = slot * br * ROW_CHUNKS
        for cidx in range(ROW_CHUNKS):
            ybuf[pl.ds(ybase + cidx, br, stride=ROW_CHUNKS), :] = y[:, cidx * LANES:(cidx + 1) * LANES]

        @pl.when(b == n_used - 1)
        def _():
            lax.fori_loop(0, br, lambda r, c: (scatter_row(slot, sbase, valid, r), c)[1], 0)
            out_wait(1 - slot)
            out_wait(slot)


def _moe_experts(bexp, bval, nused, u2p, row_src, w1, w3, w2, n_blocks, n_tokens):
    br = MOE_ROWS

    def w_idx(b, bexp, bval, nused):
        return (bexp[jnp.minimum(b, nused[0] - 1)], 0, 0)

    grid_spec = pltpu.PrefetchScalarGridSpec(
        num_scalar_prefetch=3, grid=(n_blocks,),
        in_specs=[pl.BlockSpec(memory_space=pl.ANY), pl.BlockSpec(memory_space=pl.ANY),
                  pl.BlockSpec((None, D_MODEL, EXPERT_FF), w_idx),
                  pl.BlockSpec((None, D_MODEL, EXPERT_FF), w_idx),
                  pl.BlockSpec((None, EXPERT_FF, D_MODEL), w_idx)],
        out_specs=pl.BlockSpec(memory_space=pl.ANY),
        scratch_shapes=[pltpu.VMEM(u2p.shape, U32),
                        pltpu.VMEM((D_MODEL, 2 * EXPERT_FF), BF16),
                        pltpu.VMEM((EXPERT_FF, D_MODEL), BF16),
                        pltpu.VMEM((2 * PACK_CHUNKS * br, LANES), U32),
                        pltpu.VMEM((2 * br * ROW_CHUNKS, LANES), F32),
                        pltpu.SMEM((2 * SRC_GROUP * br,), I32),
                        pltpu.SemaphoreType.DMA, pltpu.SemaphoreType.DMA,
                        pltpu.SemaphoreType.DMA((2,))])
    n_out_tiles = n_tokens * TOP_K + 2 * br
    return pl.pallas_call(
        functools.partial(_moe_kernel, n_tokens=n_tokens), grid_spec=grid_spec,
        out_shape=jax.ShapeDtypeStruct((n_out_tiles * ROW_CHUNKS, LANES), F32),
        compiler_params=pltpu.CompilerParams(dimension_semantics=("arbitrary",),
                                             vmem_limit_bytes=MOE_VMEM_LIMIT),
        name="moe_experts",
    )(bexp, bval, nused, u2p, row_src, w1, w3, w2)


def _combine_kernel(w_hbm, yt_ref, base_ref, mod_ref, g_ref, b_ref, yc_ref, yl_ref,
                    w_smem, acc_buf, sem_w, *, n_ctx_tiles):
    i = pl.program_id(0)
    n_steps = pl.num_programs(0)
    n_tok = acc_buf.shape[0] // ROW_CHUNKS
    n_idx = n_tok * TOP_K

    def w_copy(tile):
        return pltpu.make_async_copy(w_hbm.at[pl.ds(tile * n_idx, n_idx)],
                                     w_smem.at[pl.ds(lax.rem(tile, 2) * n_idx, n_idx)], sem_w)

    @pl.when(i == 0)
    def _():
        w_copy(i).start()

    w_copy(i).wait()

    @pl.when(i + 1 < n_steps)
    def _():
        w_copy(i + 1).start()

    wbase = lax.rem(i, 2) * n_idx

    def reduce_token(t, carry):
        acc = None
        for k in range(TOP_K):
            j = t * TOP_K + k
            row = pl.multiple_of(j * ROW_CHUNKS, ROW_CHUNKS)
            term = w_smem[wbase + j] * yt_ref[pl.ds(row, ROW_CHUNKS), :]
            acc = term if acc is None else acc + term
        acc_buf[pl.ds(pl.multiple_of(t * ROW_CHUNKS, ROW_CHUNKS), ROW_CHUNKS), :] = acc
        return carry

    lax.fori_loop(0, n_tok, reduce_token, 0)
    moe = _load_row_tiles(acc_buf, n_tok)
    gate2 = mod_ref[:, 5 * D_MODEL:6 * D_MODEL]
    y = _layer_norm(base_ref[...] + gate2 * moe, g_ref[...], b_ref[...])

    @pl.when(i < n_ctx_tiles)
    def _():
        yc_ref[...] = y

    @pl.when(i >= n_ctx_tiles)
    def _():
        yl_ref[...] = y


def _combine(w_flat, yt, base, mod3, l2g, l2b, n_ctx, seq_tokens):
    n = base.shape[0]
    tc = COMB_TILE
    n_ctx_tiles = n_ctx // tc
    n_seq_tiles = seq_tokens // tc

    def mod_idx(i):
        return (jnp.where(i < n_ctx_tiles, 0, 1 + (i - n_ctx_tiles) // n_seq_tiles), 0, 0)

    full = lambda a: pl.BlockSpec(a.shape, lambda i: (0,) * a.ndim)
    return pl.pallas_call(
        functools.partial(_combine_kernel, n_ctx_tiles=n_ctx_tiles),
        grid=(n // tc,),
        in_specs=[pl.BlockSpec(memory_space=pl.ANY),
                  pl.BlockSpec((tc * TOP_K * ROW_CHUNKS, LANES), lambda i: (i, 0)),
                  pl.BlockSpec((tc, D_MODEL), lambda i: (i, 0)),
                  pl.BlockSpec((None, 1, mod3.shape[2]), mod_idx), full(l2g), full(l2b)],
        out_specs=(pl.BlockSpec((tc, D_MODEL), lambda i: (jnp.minimum(i, n_ctx_tiles - 1), 0)),
                   pl.BlockSpec((tc, D_MODEL), lambda i: (jnp.maximum(i - n_ctx_tiles, 0), 0))),
        out_shape=(jax.ShapeDtypeStruct((n_ctx, D_MODEL), F32),
                   jax.ShapeDtypeStruct((n - n_ctx, D_MODEL), F32)),
        scratch_shapes=[pltpu.SMEM((2 * tc * TOP_K,), F32),
                        pltpu.VMEM((tc * ROW_CHUNKS, LANES), F32),
                        pltpu.SemaphoreType.DMA],
        compiler_params=_cparams(1), name="combine",
    )(w_flat, yt, base, mod3, l2g, l2b)


def _rope_tables(n_tok, tile):
    rows = n_tok // GRID_W
    row_idx = jnp.repeat(jnp.arange(rows, dtype=F32), GRID_W)
    col_idx = jnp.tile(jnp.arange(GRID_W, dtype=F32), rows)
    inv_freq = 1.0 / (ROPE_THETA ** (jnp.arange(0, ROPE_AXIS_DIM, 2, dtype=F32) / ROPE_AXIS_DIM))
    ang_r = row_idx[:, None] * inv_freq[None, :]
    ang_c = col_idx[:, None] * inv_freq[None, :]
    ang = jnp.concatenate([ang_r, ang_r, ang_c, ang_c], axis=-1)
    cos, sin = jnp.cos(ang), jnp.sin(ang)
    quarter = (jnp.arange(HEAD_DIM) // (ROPE_AXIS_DIM // 2)) % 2
    sin_a = jnp.where(quarter == 0, -sin, 0.0)
    sin_b = jnp.where(quarter == 1, sin, 0.0)
    rep = LANES // HEAD_DIM
    ident = lambda v: jnp.full((tile, LANES), v, F32)
    cos_t = jnp.concatenate([jnp.tile(cos, (1, rep)), ident(1.0)], axis=0)
    sa_t = jnp.concatenate([jnp.tile(sin_a, (1, rep)), ident(0.0)], axis=0)
    sb_t = jnp.concatenate([jnp.tile(sin_b, (1, rep)), ident(0.0)], axis=0)
    ident_tr = lambda v: jnp.full((HEAD_DIM, tile), v, F32)
    cos_tr = jnp.concatenate([cos.T, ident_tr(1.0)], axis=1)
    sin_tr = jnp.concatenate([sin.T, ident_tr(0.0)], axis=1)
    return cos_t, sa_t, sb_t, cos_tr, sin_tr


def _dup_heads(a):
    parts = []
    for h in range(KV_HEADS):
        blk = a[..., h * HEAD_DIM:(h + 1) * HEAD_DIM]
        parts += [blk] * (LANES // HEAD_DIM)
    return jnp.concatenate(parts, axis=-1)


def kernel(x_prompt, x_sample, cache_k, cache_v, state_gla_fwd, state_gla_bwd, c, c_ctx, w_ada, b_ada, w_in, q_norm, k_norm, gla_wa_fwd, gla_ba_fwd, gla_wa_bwd, gla_ba_bwd, gla_norm, w_out, ln1_g, ln1_b, ln2_g, ln2_b, w_router, router_bias, exp_w1, exp_w3, exp_w2, sh_w1, sh_w3, sh_w2):
    n_ctx_b, ctx_seq, _ = x_prompt.shape
    n_lat_b, lat_seq, _ = x_sample.shape
    n_ctx = n_ctx_b * ctx_seq
    n_lat = n_lat_b * lat_seq
    n = n_ctx + n_lat
    l = 0

    x_c = x_prompt.reshape(n_ctx, D_MODEL)
    x_l = x_sample.reshape(n_lat, D_MODEL)

    c_rows = jnp.zeros((SUBLANES, D_MODEL), F32).at[0].set(c_ctx).at[1:1 + n_lat_b].set(c)
    mod = _modulation(c_rows, w_ada[l], b_ada[l][None, :])
    mod3 = mod.reshape(SUBLANES, 1, 6 * D_MODEL)

    wi = w_in[l]
    o_q, o_k, o_v, o_gq, o_gk, o_gv, o_gg, o_rf, o_rb, o_end = np.cumsum(
        [0, ATT_WIDTH, KV_HEADS * HEAD_DIM, KV_HEADS * HEAD_DIM, GLA_KW, GLA_KW, GLA_WIDTH, GLA_WIDTH,
         GLA_GATE_RANK, GLA_GATE_RANK])
    w_tok = jnp.concatenate([
        _dup_heads(wi[:, o_k:o_v]), wi[:, o_v:o_gq], wi[:, o_gq:o_gk],
        wi[:, o_gv:o_gg], wi[:, o_gg:o_rf], wi[:, o_rf:o_end],
        jnp.zeros((D_MODEL, LANES - 2 * GLA_GATE_RANK), F32)], axis=1).astype(BF16)
    w_tr = jnp.concatenate([wi[:, o_q:o_k], wi[:, o_v:o_gq], wi[:, o_gk:o_gv], wi[:, o_rf:o_end]],
                           axis=1).T.astype(BF16)
    rep = LANES // HEAD_DIM
    qn = q_norm[l][:, None]
    kn = jnp.tile(k_norm[l], rep)[None, :]
    seg = jnp.asarray(np.kron(np.eye(rep), np.ones((HEAD_DIM, HEAD_DIM))), BF16)
    wa = jnp.zeros((LANES, 2 * GLA_KW), F32)
    wa = wa.at[0:GLA_GATE_RANK, 0:GLA_KW].set(gla_wa_fwd[l])
    wa = wa.at[GLA_GATE_RANK:2 * GLA_GATE_RANK, GLA_KW:].set(gla_wa_bwd[l])
    ba = jnp.concatenate([gla_ba_fwd[l], gla_ba_bwd[l]])[None, :]
    wat = wa[0:2 * GLA_GATE_RANK, :].T
    bat = ba.T
    cos_t, sa_t, sb_t, cos_tr, sin_tr = _rope_tables(lat_seq, TOK_TILE)

    (qt, k_dup, vt, k32, v32, gq, gv, gg, la, gkt, lat) = _in_projection(
        x_c, x_l, mod3, w_tok, w_tr, qn, kn, cos_t, sa_t, sb_t, cos_tr, sin_tr, seg, wa, ba, wat, bat,
        lat_seq // TOK_TILE)

    ck = _dup_heads(cache_k[:, l].reshape(n_lat_b, -1, KV_HEADS * HEAD_DIM)).astype(BF16)
    cvt = cache_v[:, l].reshape(n_lat_b, -1, KV_HEADS * HEAD_DIM).transpose(0, 2, 1).astype(BF16)
    att_c = _attention(qt, k_dup, vt, None, 0, n_ctx_b, ctx_seq)
    att_l = _attention(qt, k_dup, vt, (ck, cvt), n_ctx, n_lat_b, lat_seq)

    gconst, levels_of = _gla_constants()
    to_dev = lambda t: (jnp.asarray(t[0], BF16), jnp.asarray(t[1], BF16), jnp.asarray(t[2], F32))
    bd = jnp.asarray(np.kron(np.eye(GLA_HEADS), np.ones((GLA_DK, GLA_DV))), BF16)
    vbd = jnp.asarray(np.kron(np.eye(GLA_HEADS), np.ones((GLA_CHUNK, GLA_DV))), BF16)
    consts = ((to_dev(gconst["f"]), to_dev(gconst["b"])), levels_of, bd, vbd)
    s_zero = jnp.zeros((n_ctx_b, GLA_HEADS, GLA_DK, GLA_DV), F32)
    of_c, ob_c, sf_new, sb_new = _gla(gq, la, gkt, lat, gv, s_zero, s_zero, consts, 0, n_ctx_b, ctx_seq)
    of_l, ob_l, _, _ = _gla(gq, la, gkt, lat, gv, state_gla_fwd[:, l], state_gla_bwd[:, l], consts,
                            n_ctx, n_lat_b, lat_seq)

    sw13 = jnp.concatenate([sh_w1[l], sh_w3[l]], axis=1).astype(BF16)
    base, u2_rows, logits_t = _out_projection(
        att_c, att_l, of_c, of_l, ob_c, ob_l, gg, x_c, x_l, mod3, w_out[l].astype(BF16),
        gla_norm[l][None, :], ln1_g[l][None, :], ln1_b[l][None, :], w_router[l].T.astype(BF16), sw13,
        sh_w2[l].astype(BF16), lat_seq // TOK_TILE)

    upper = jnp.asarray(np.triu(np.ones((TOK_TILE, TOK_TILE)), 1), BF16)
    idx_t, w_t, pos_t, counts = _route(logits_t, router_bias[l][:, None], upper)
    n_blocks = n * TOP_K // MOE_ROWS + N_EXPERTS
    n_blocks_pad = -(-n_blocks // LANES) * LANES
    lower = jnp.asarray(np.tril(np.ones((N_EXPERTS, N_EXPERTS)), -1), F32)
    dest_t, bexp, bval, nused = _destinations(counts, lower, idx_t, pos_t, n_blocks_pad)
    dest_flat = dest_t.T.reshape(-1)
    w_flat = w_t.T.reshape(-1)

    row_src = _invert_rows(dest_flat, n_blocks * MOE_ROWS)
    yt = _moe_experts(bexp.reshape(-1), bval.reshape(-1), nused.reshape(-1)[0:1], u2_rows, row_src,
                      exp_w1[l], exp_w3[l], exp_w2[l], n_blocks, n)
    y_c, y_l = _combine(w_flat, yt, base, mod3, ln2_g[l][None, :], ln2_b[l][None, :], n_ctx, lat_seq)

    y_prompt = y_c.reshape(n_ctx_b, ctx_seq, D_MODEL)
    y_sample = y_l.reshape(n_lat_b, lat_seq, D_MODEL)
    new_cache_k = k32.reshape(n_ctx_b, 1, ctx_seq, KV_HEADS, HEAD_DIM)
    new_cache_v = v32.reshape(n_ctx_b, 1, ctx_seq, KV_HEADS, HEAD_DIM)
    return (y_prompt, y_sample, new_cache_k, new_cache_v, sf_new[:, None], sb_new[:, None])
```

```python
import functools

import numpy as np
import jax
import jax.numpy as jnp
from jax import lax
from jax.experimental import pallas as pl
from jax.experimental.pallas import tpu as pltpu
from jax.experimental.pallas import tpu_sc as plsc

F32 = jnp.float32
BF16 = jnp.bfloat16
I32 = jnp.int32

D_MODEL = 1024
GRID_W = 64
HEAD_DIM = 64
N_HEADS = 8
KV_HEADS = 2
ATT_WIDTH = N_HEADS * HEAD_DIM
ATT_SCALE = HEAD_DIM ** -0.5
LOG2_E = 1.4426950408889634
ROPE_AXIS_DIM = HEAD_DIM // 2
ROPE_THETA = 10000.0
GLA_HEADS = 4
GLA_DK = 64
GLA_DV = 128
GLA_WIDTH = GLA_HEADS * GLA_DV
GLA_KW = GLA_HEADS * GLA_DK
GLA_GATE_RANK = 16
GLA_TAU = 16.0
N_EXPERTS = 256
TOP_K = 8
EXPERT_FF = 256
SHARED_FF = 256
ROUTED_SCALE = 2.5
DEPTH = 1
ALPHA = (2.0 * DEPTH) ** 0.25
EPS = 1e-6

LANES = 128
SUBLANES = 8
ROW_CHUNKS = D_MODEL // LANES
VMEM_LIMIT = 56 * 1024 * 1024

TOK_TILE = 512
ATT_TQ = 128
GLA_CHUNK = 128
GLA_LEVELS = ((32, 128), (8, 32), (2, 8), (1, 2))
MOE_ROWS = 256
MOE_VMEM_LIMIT = 62 * 1024 * 1024
COMB_TILE = 128
HIGHEST = lax.Precision.HIGHEST


def _cparams(n_axes):
    return pltpu.CompilerParams(dimension_semantics=("arbitrary",) * n_axes,
                                vmem_limit_bytes=VMEM_LIMIT)


def _silu(x):
    return x * jax.nn.sigmoid(x)


def _log_sigmoid(x):
    return jnp.minimum(x, 0.0) - jnp.log(1.0 + jnp.exp(-jnp.abs(x)))


def _load_row_tiles(ref, n_rows, row0=0):
    return jnp.concatenate(
        [ref[pl.ds(row0 * ROW_CHUNKS + cidx, n_rows, stride=ROW_CHUNKS), :] for cidx in range(ROW_CHUNKS)],
        axis=1)


def _store_row_tiles(ref, x):
    for cidx in range(ROW_CHUNKS):
        ref[pl.ds(cidx, x.shape[0], stride=ROW_CHUNKS), :] = x[:, cidx * LANES:(cidx + 1) * LANES]


def _layer_norm(z, g, b):
    mu = jnp.mean(z, axis=-1, keepdims=True)
    zc = z - mu
    var = jnp.mean(zc * zc, axis=-1, keepdims=True)
    return zc * lax.rsqrt(var + EPS) * g + b


def _mod_kernel(c_ref, w_ref, b_ref, o_ref):
    s = _silu(c_ref[...]).astype(BF16)
    o_ref[...] = jnp.dot(s, w_ref[...].astype(BF16), preferred_element_type=F32) + b_ref[...]


def _modulation(c_rows, w_ada, b_ada):
    n_cols = w_ada.shape[1]
    tn = 512
    return pl.pallas_call(
        _mod_kernel,
        grid=(n_cols // tn,),
        in_specs=[pl.BlockSpec((SUBLANES, D_MODEL), lambda j: (0, 0)),
                  pl.BlockSpec((D_MODEL, tn), lambda j: (0, j)),
                  pl.BlockSpec((1, tn), lambda j: (0, j))],
        out_specs=pl.BlockSpec((SUBLANES, tn), lambda j: (0, j)),
        out_shape=jax.ShapeDtypeStruct((SUBLANES, n_cols), F32),
        compiler_params=_cparams(1),
        name="modulation",
    )(c_rows, w_ada, b_ada)


_C_K = 0
_C_V = _C_K + 2 * LANES
_C_GQ = _C_V + KV_HEADS * HEAD_DIM
_C_GV = _C_GQ + GLA_KW
_C_GG = _C_GV + GLA_WIDTH
_C_RA = _C_GG + GLA_WIDTH
_C_END = _C_RA + LANES
_R_Q = 0
_R_V = _R_Q + ATT_WIDTH
_R_GK = _R_V + KV_HEADS * HEAD_DIM
_R_RA = _R_GK + GLA_KW
_R_END = _R_RA + 2 * GLA_GATE_RANK


def _inproj_kernel(xc_ref, xl_ref, mod_ref, w_ref, wt_ref, qn_ref, kn_ref, cos_ref, sa_ref, sb_ref,
                   cost_ref, sint_ref, seg_ref, wa_ref, ba_ref, wat_ref, bat_ref,
                   qt_ref, k_ref, vt_ref, k32_ref, v32_ref, gq_ref, gv_ref, gg_ref,
                   la_ref, gkt_ref, lat_ref, *, n_ctx_tiles):
    i = pl.program_id(0)
    m = mod_ref[...]
    shift1 = m[:, 0:D_MODEL]
    scale1 = m[:, D_MODEL:2 * D_MODEL]
    x = jnp.where(i < n_ctx_tiles, xc_ref[...], xl_ref[...])
    u = (x * (1.0 + scale1) + shift1).astype(BF16)

    cos = cos_ref[...]
    sin_a = sa_ref[...]
    sin_b = sb_ref[...]
    seg = seg_ref[...]
    lane = lax.broadcasted_iota(I32, (u.shape[0], LANES), 1)
    low = lane < HEAD_DIM

    def proj(c0, c1):
        return jnp.dot(u, w_ref[:, c0:c1], preferred_element_type=F32)

    def head_norm(blk, gain):
        ss = jnp.dot((blk * blk).astype(BF16), seg, preferred_element_type=F32) * (1.0 / HEAD_DIM)
        return blk * lax.rsqrt(ss + EPS) * gain

    def rope(blk):
        return (blk * cos + pltpu.roll(blk, LANES - ROPE_AXIS_DIM // 2, 1) * sin_a
                + pltpu.roll(blk, ROPE_AXIS_DIM // 2, 1) * sin_b)

    pk = proj(_C_K, _C_V)
    kn = [head_norm(pk[:, j * LANES:(j + 1) * LANES], kn_ref[...]) for j in range(KV_HEADS)]
    for j in range(KV_HEADS):
        k_ref[:, j * LANES:(j + 1) * LANES] = rope(kn[j]).astype(BF16)

    @pl.when(i < n_ctx_tiles)
    def _():
        k32_ref[...] = jnp.where(low, kn[0], kn[1])
        v32_ref[...] = proj(_C_V, _C_GQ)

    gq_ref[...] = proj(_C_GQ, _C_GV) * (GLA_DK ** -0.5)
    gv_ref[...] = proj(_C_GV, _C_GG).astype(BF16)
    gg_ref[...] = proj(_C_GG, _C_RA).astype(BF16)

    ra = proj(_C_RA, _C_END)
    pre = jnp.dot(ra, wa_ref[...], precision=HIGHEST, preferred_element_type=F32) + ba_ref[...]
    la_ref[...] = _log_sigmoid(pre) * (1.0 / GLA_TAU)

    pt = lax.dot_general(wt_ref[...], u, (((1,), (1,)), ((), ())), preferred_element_type=F32)
    cos_t = cost_ref[...]
    sin_t = sint_ref[...]
    quarter = ROPE_AXIS_DIM // 2
    for h in range(N_HEADS):
        blk = pt[_R_Q + h * HEAD_DIM:_R_Q + (h + 1) * HEAD_DIM, :]
        ms = jnp.mean(blk * blk, axis=0, keepdims=True)
        qn = blk * lax.rsqrt(ms + EPS) * qn_ref[...]
        rot = jnp.concatenate([-qn[quarter:2 * quarter], qn[0:quarter],
                               -qn[3 * quarter:4 * quarter], qn[2 * quarter:3 * quarter]], axis=0)
        qt_ref[h * HEAD_DIM:(h + 1) * HEAD_DIM, :] = (
            (qn * cos_t + rot * sin_t) * (ATT_SCALE * LOG2_E)).astype(BF16)
    vt_ref[...] = pt[_R_V:_R_GK, :].astype(BF16)
    gkt_ref[...] = pt[_R_GK:_R_RA, :]
    rat = pt[_R_RA:_R_END, :]
    pre_t = jnp.dot(wat_ref[...], rat, precision=HIGHEST, preferred_element_type=F32) + bat_ref[...]
    lat_ref[...] = _log_sigmoid(pre_t) * (1.0 / GLA_TAU)


def _in_projection(x_c, x_l, mod3, w_tok, w_tr, qn, kn, cos_t, sa_t, sb_t, cos_tr, sin_tr, seg, wa, ba,
                   wat, bat, n_seq_tiles):
    n_ctx = x_c.shape[0]
    n = n_ctx + x_l.shape[0]
    tb = TOK_TILE
    n_ctx_tiles = n_ctx // tb
    n_tiles = n // tb
    n_rope_blocks = cos_t.shape[0] // tb - 1

    def mod_idx(i):
        return (jnp.where(i < n_ctx_tiles, 0, 1 + (i - n_ctx_tiles) // n_seq_tiles), 0, 0)

    def rope_blk(i):
        return jnp.where(i < n_ctx_tiles, n_rope_blocks, (i - n_ctx_tiles) % n_seq_tiles)

    def rope_idx(i):
        return (rope_blk(i), 0)

    def ctx_idx(i):
        return (jnp.minimum(i, n_ctx_tiles - 1), 0)

    tok = lambda w: pl.BlockSpec((tb, w), lambda i: (i, 0))
    full = lambda a: pl.BlockSpec(a.shape, lambda i: (0,) * a.ndim)
    tr = lambda r: pl.BlockSpec((r, tb), lambda i: (0, i))
    rope_tr = pl.BlockSpec((HEAD_DIM, tb), lambda i: (0, rope_blk(i)))
    out_shapes = (
        jax.ShapeDtypeStruct((ATT_WIDTH, n), BF16),
        jax.ShapeDtypeStruct((n, 2 * LANES), BF16),
        jax.ShapeDtypeStruct((KV_HEADS * HEAD_DIM, n), BF16),
        jax.ShapeDtypeStruct((n_ctx, LANES), F32),
        jax.ShapeDtypeStruct((n_ctx, LANES), F32),
        jax.ShapeDtypeStruct((n, GLA_KW), F32),
        jax.ShapeDtypeStruct((n, GLA_WIDTH), BF16),
        jax.ShapeDtypeStruct((n, GLA_WIDTH), BF16),
        jax.ShapeDtypeStruct((n, 2 * GLA_KW), F32),
        jax.ShapeDtypeStruct((GLA_KW, n), F32),
        jax.ShapeDtypeStruct((2 * GLA_KW, n), F32),
    )
    out_specs = (tr(ATT_WIDTH), tok(2 * LANES), tr(KV_HEADS * HEAD_DIM),
                 pl.BlockSpec((tb, LANES), ctx_idx), pl.BlockSpec((tb, LANES), ctx_idx),
                 tok(GLA_KW), tok(GLA_WIDTH), tok(GLA_WIDTH), tok(2 * GLA_KW),
                 tr(GLA_KW), tr(2 * GLA_KW))
    in_specs = [pl.BlockSpec((tb, D_MODEL), ctx_idx),
                pl.BlockSpec((tb, D_MODEL), lambda i: (jnp.maximum(i - n_ctx_tiles, 0), 0)),
                pl.BlockSpec((None, 1, mod3.shape[2]), mod_idx),
                full(w_tok), full(w_tr), full(qn), full(kn),
                pl.BlockSpec((tb, LANES), rope_idx), pl.BlockSpec((tb, LANES), rope_idx),
                pl.BlockSpec((tb, LANES), rope_idx), rope_tr, rope_tr,
                full(seg), full(wa), full(ba), full(wat), full(bat)]
    return pl.pallas_call(
        functools.partial(_inproj_kernel, n_ctx_tiles=n_ctx_tiles),
        grid=(n_tiles,), in_specs=in_specs, out_specs=out_specs, out_shape=out_shapes,
        compiler_params=_cparams(1), name="in_projection",
    )(x_c, x_l, mod3, w_tok, w_tr, qn, kn, cos_t, sa_t, sb_t, cos_tr, sin_tr, seg, wa, ba, wat, bat)


def _attention_kernel(*refs, n_kv_parts):
    qt_ref = refs[0]
    k_refs = refs[1:1 + n_kv_parts]
    vt_refs = refs[1 + n_kv_parts:1 + 2 * n_kv_parts]
    o_ref = refs[1 + 2 * n_kv_parts]
    tq = qt_ref.shape[1]
    group = N_HEADS // KV_HEADS
    for kv in range(KV_HEADS):
        heads = range(kv * group, (kv + 1) * group)
        q_grp = jnp.concatenate([qt_ref[h * HEAD_DIM:(h + 1) * HEAD_DIM, :] for h in heads], axis=1)
        rhs = jnp.concatenate([q_grp, jnp.zeros_like(q_grp)], axis=0)
        s = [jnp.dot(k[:, kv * LANES:(kv + 1) * LANES], rhs, preferred_element_type=F32)
             for k in k_refs]
        mx = functools.reduce(jnp.maximum, [jnp.max(x, axis=0, keepdims=True) for x in s])
        pr = [jnp.exp2(x - mx) for x in s]
        den = functools.reduce(jnp.add, [jnp.sum(x, axis=0, keepdims=True) for x in pr])
        acc = functools.reduce(jnp.add, [
            jnp.dot(vt[kv * HEAD_DIM:(kv + 1) * HEAD_DIM, :], x.astype(BF16),
                    preferred_element_type=F32) for x, vt in zip(pr, vt_refs)])
        out = (acc / den).astype(BF16)
        for j, h in enumerate(heads):
            o_ref[h * HEAD_DIM:(h + 1) * HEAD_DIM, :] = out[:, j * tq:(j + 1) * tq]


def _attention(qt, k, vt, extra_kv, row0, n_batch, seq):
    tq = ATT_TQ
    n_q = seq // tq
    q_blk0 = row0 // tq
    kv_blk0 = row0 // seq
    in_specs = [pl.BlockSpec((ATT_WIDTH, tq), lambda b, i: (0, q_blk0 + b * n_q + i))]
    k_spec = pl.BlockSpec((seq, 2 * LANES), lambda b, i: (kv_blk0 + b, 0))
    vt_spec = pl.BlockSpec((KV_HEADS * HEAD_DIM, seq), lambda b, i: (0, kv_blk0 + b))
    args_k, args_v, specs_k, specs_v = [k], [vt], [k_spec], [vt_spec]
    if extra_kv is not None:
        ck, cvt = extra_kv
        args_k.append(ck)
        args_v.append(cvt)
        specs_k.append(pl.BlockSpec((None, ck.shape[1], 2 * LANES), lambda b, i: (b, 0, 0)))
        specs_v.append(pl.BlockSpec((None, KV_HEADS * HEAD_DIM, cvt.shape[2]), lambda b, i: (b, 0, 0)))
    return pl.pallas_call(
        functools.partial(_attention_kernel, n_kv_parts=len(args_k)),
        grid=(n_batch, n_q),
        in_specs=in_specs + specs_k + specs_v,
        out_specs=pl.BlockSpec((ATT_WIDTH, tq), lambda b, i: (0, b * n_q + i)),
        out_shape=jax.ShapeDtypeStruct((ATT_WIDTH, n_batch * seq), BF16),
        compiler_params=_cparams(2), name="attention",
    )(qt, *args_k, *args_v)


def _gla_constants():
    c = GLA_CHUNK
    idx = np.arange(c)
    q_mats, k_mats, masks, levels_of = [], [], [], []
    for li, (s, p) in enumerate(GLA_LEVELS):
        start = (idx // s) * s
        end = start + s - 1
        k_mats.append(((idx[None, :] > idx[:, None]) & (idx[None, :] <= end[:, None])))
        for d in range(p // s - 1):
            lo = np.maximum(start - d * s, 0)
            q_mats.append((idx[None, :] >= lo[:, None]) & (idx[None, :] <= idx[:, None]))
            masks.append((idx[:, None] // p == idx[None, :] // p)
                         & (idx[:, None] // s - idx[None, :] // s - 1 == d))
            levels_of.append(li)
    masks.append(np.eye(c, dtype=bool))
    levels_of.append(len(GLA_LEVELS) - 1)
    q_mats.append(idx[None, :] <= idx[:, None])
    k_mats = k_mats[:-1]
    k_mats.append(idx[None, :] > idx[:, None])
    k_mats.append(np.ones((c, c), bool))
    out = {}
    for name, flip in (("f", False), ("b", True)):
        f = (lambda a: a[::-1, ::-1]) if flip else (lambda a: a)
        lq = np.concatenate([f(a) for a in q_mats], axis=0).astype(np.float32)
        lkt = np.concatenate([f(a).T for a in k_mats], axis=1).astype(np.float32)
        mk = np.stack([np.tile(f(a), (1, GLA_HEADS)) for a in masks]).astype(np.float32)
        out[name] = (np.concatenate([lq, lq], axis=1), np.concatenate([lkt, lkt], axis=0), mk)
    return out, tuple(levels_of)


def _gla_direction(q, g, gkt, gt, v, lq2, lkt2, masks_ref, bd, vbd, s_ref, levels_of):
    c = GLA_CHUNK
    n_var = len(levels_of)
    n_lev = len(GLA_LEVELS)
    g_hi = g.astype(BF16)
    g_lo = (g - g_hi.astype(F32)).astype(BF16)
    fq = jnp.dot(lq2, jnp.concatenate([g_hi, g_lo], axis=0), preferred_element_type=F32)
    gt_hi = gt.astype(BF16)
    gt_lo = (gt - gt_hi.astype(F32)).astype(BF16)
    fk = jnp.dot(jnp.concatenate([gt_hi, gt_lo], axis=1), lkt2, preferred_element_type=F32)

    def key_factor(f):
        return gkt * jnp.exp(fk[:, f * c:(f + 1) * c])

    q_var = [(q * jnp.exp(fq[vi * c:(vi + 1) * c, :])).astype(BF16) for vi in range(n_var - 1)]
    q_var.append(q.astype(BF16))
    a = jnp.zeros((c, GLA_HEADS * c), F32)
    for li in range(n_lev):
        kt = (key_factor(li) if li < n_lev - 1 else gkt).astype(BF16)
        xt = jnp.concatenate([kt] * GLA_HEADS, axis=1) * bd
        vis = [vi for vi in range(n_var) if levels_of[vi] == li]
        res = jnp.dot(jnp.concatenate([q_var[vi] for vi in vis], axis=0), xt,
                      preferred_element_type=F32)
        for r, vi in enumerate(vis):
            a = a + masks_ref[vi] * res[r * c:(r + 1) * c, :]
    q_in = (q * jnp.exp(fq[(n_var - 1) * c:n_var * c, :])).astype(BF16)
    state = s_ref[...]
    v_bd = jnp.concatenate([v] * GLA_HEADS, axis=0) * vbd
    o = (jnp.dot(q_in, state.astype(BF16), preferred_element_type=F32)
         + jnp.dot(a.astype(BF16), v_bd, preferred_element_type=F32))
    k_out = key_factor(n_lev - 1).astype(BF16)
    e_tot = jnp.exp(fk[:, n_lev * c:(n_lev + 1) * c])
    upd = jnp.dot(k_out, v, preferred_element_type=F32)
    s_ref[...] = (state * jnp.concatenate([e_tot] * (GLA_WIDTH // c), axis=1)
                  + upd * bd.astype(F32))
    return o


def _gla_kernel(gq_f, la_f, gkt_f, lat_f, gv_f, gq_b, la_b, gkt_b, lat_b, gv_b,
                s0f_ref, s0b_ref, lq2f, lkt2f, mkf, lq2b, lkt2b, mkb, bd_ref, vbd_ref,
                of_ref, ob_ref, sf_ref, sb_ref, st_f, st_b, *, levels_of):
    n = pl.program_id(1)

    @pl.when(n == 0)
    def _():
        st_f[...] = jnp.zeros_like(st_f)
        st_b[...] = jnp.zeros_like(st_b)
        for h in range(GLA_HEADS):
            rows = slice(h * GLA_DK, (h + 1) * GLA_DK)
            cols = slice(h * GLA_DV, (h + 1) * GLA_DV)
            st_f[rows, cols] = s0f_ref[h]
            st_b[rows, cols] = s0b_ref[h]

    bd = bd_ref[...]
    vbd = vbd_ref[...]
    of_ref[...] = _gla_direction(gq_f[...], la_f[...], gkt_f[...], lat_f[...], gv_f[...],
                                 lq2f[...], lkt2f[...], mkf, bd, vbd, st_f, levels_of)
    ob_ref[...] = _gla_direction(gq_b[...], la_b[...], gkt_b[...], lat_b[...], gv_b[...],
                                 lq2b[...], lkt2b[...], mkb, bd, vbd, st_b, levels_of)

    @pl.when(n == pl.num_programs(1) - 1)
    def _():
        for h in range(GLA_HEADS):
            rows = slice(h * GLA_DK, (h + 1) * GLA_DK)
            cols = slice(h * GLA_DV, (h + 1) * GLA_DV)
            sf_ref[h] = st_f[rows, cols]
            sb_ref[h] = st_b[rows, cols]


def _gla(gq, la, gkt, lat, gv, s0f, s0b, consts, row0, n_batch, seq):
    (cf, cb), levels_of, bd, vbd = consts
    c = GLA_CHUNK
    nc = seq // c
    blk0 = row0 // c
    n_la_blocks_b = 1
    fwd = lambda b, n: blk0 + b * nc + n
    bwd = lambda b, n: blk0 + b * nc + (nc - 1 - n)

    def tok(w, which, col=0):
        return pl.BlockSpec((c, w), lambda b, n: (which(b, n), col))

    def tr(r, which, row=0):
        return pl.BlockSpec((r, c), lambda b, n: (row, which(b, n)))

    full = lambda a: pl.BlockSpec(a.shape, lambda b, n: (0,) * a.ndim)
    st_spec = pl.BlockSpec((None, GLA_HEADS, GLA_DK, GLA_DV), lambda b, n: (b, 0, 0, 0))
    in_specs = [tok(GLA_KW, fwd), tok(GLA_KW, fwd, 0), tr(GLA_KW, fwd), tr(GLA_KW, fwd, 0),
                tok(GLA_WIDTH, fwd),
                tok(GLA_KW, bwd), tok(GLA_KW, bwd, n_la_blocks_b), tr(GLA_KW, bwd),
                tr(GLA_KW, bwd, 1), tok(GLA_WIDTH, bwd),
                st_spec, st_spec,
                full(cf[0]), full(cf[1]), full(cf[2]), full(cb[0]), full(cb[1]), full(cb[2]),
                full(bd), full(vbd)]
    out_specs = (pl.BlockSpec((c, GLA_WIDTH), lambda b, n: (b * nc + n, 0)),
                 pl.BlockSpec((c, GLA_WIDTH), lambda b, n: (b * nc + (nc - 1 - n), 0)),
                 st_spec, st_spec)
    out_shape = (jax.ShapeDtypeStruct((n_batch * seq, GLA_WIDTH), F32),
                 jax.ShapeDtypeStruct((n_batch * seq, GLA_WIDTH), F32),
                 jax.ShapeDtypeStruct((n_batch, GLA_HEADS, GLA_DK, GLA_DV), F32),
                 jax.ShapeDtypeStruct((n_batch, GLA_HEADS, GLA_DK, GLA_DV), F32))
    return pl.pallas_call(
        functools.partial(_gla_kernel, levels_of=levels_of),
        grid=(n_batch, nc), in_specs=in_specs, out_specs=out_specs, out_shape=out_shape,
        scratch_shapes=[pltpu.VMEM((GLA_KW, GLA_WIDTH), F32), pltpu.VMEM((GLA_KW, GLA_WIDTH), F32)],
        compiler_params=_cparams(2), name="gla",
    )(gq, la, gkt, lat, gv, gq, la, gkt, lat, gv, s0f, s0b,
      cf[0], cf[1], cf[2], cb[0], cb[1], cb[2], bd, vbd)


def _outproj_kernel(attc_ref, attl_ref, ofc_ref, ofl_ref, obc_ref, obl_ref, gg_ref, xc_ref, xl_ref,
                    mod_ref, wo_ref, gn_ref, l1g_ref, l1b_ref, wrt_ref, sw13_ref, sw2_ref,
                    base_ref, u2_ref, lg_ref, *, n_ctx_tiles):
    is_ctx = pl.program_id(0) < n_ctx_tiles
    pick = lambda a_ref, b_ref: jnp.where(is_ctx, a_ref[...], b_ref[...])
    m = mod_ref[...]
    gate1 = m[:, 2 * D_MODEL:3 * D_MODEL]
    shift2 = m[:, 3 * D_MODEL:4 * D_MODEL]
    scale2 = m[:, 4 * D_MODEL:5 * D_MODEL]
    gate2 = m[:, 5 * D_MODEL:6 * D_MODEL]
    og = pick(ofc_ref, ofl_ref) + pick(obc_ref, obl_ref)
    gg = gg_ref[...].astype(F32)
    parts = []
    for h in range(GLA_HEADS):
        blk = og[:, h * GLA_DV:(h + 1) * GLA_DV]
        ms = jnp.mean(blk * blk, axis=-1, keepdims=True)
        nb = blk * lax.rsqrt(ms + EPS) * gn_ref[...]
        parts.append((nb * _silu(gg[:, h * GLA_DV:(h + 1) * GLA_DV])).astype(BF16))
    att_t = pick(attc_ref, attl_ref)
    hmix = (lax.dot_general(att_t, wo_ref[0:ATT_WIDTH, :], (((0,), (0,)), ((), ())),
                            preferred_element_type=F32)
            + jnp.dot(jnp.concatenate(parts, axis=1), wo_ref[ATT_WIDTH:, :],
                      preferred_element_type=F32))
    x1 = _layer_norm(ALPHA * pick(xc_ref, xl_ref) + gate1 * hmix, l1g_ref[...], l1b_ref[...])
    u2 = x1 * (1.0 + scale2) + shift2
    u2b = u2.astype(BF16)
    lg_ref[...] = lax.dot_general(wrt_ref[...], u2b, (((1,), (1,)), ((), ())),
                                  preferred_element_type=F32)
    ab = jnp.dot(u2b, sw13_ref[...], preferred_element_type=F32)
    hid = (_silu(ab[:, 0:SHARED_FF]) * ab[:, SHARED_FF:2 * SHARED_FF]).astype(BF16)
    shared = jnp.dot(hid, sw2_ref[...], preferred_element_type=F32)
    base_ref[...] = ALPHA * x1 + gate2 * shared
    _pack_rows(u2_ref, u2)


def _out_projection(att_c, att_l, of_c, of_l, ob_c, ob_l, gg, x_c, x_l, mod3, wo, gn, l1g, l1b, wrt,
                    sw13, sw2, n_seq_tiles):
    n_ctx = x_c.shape[0]
    n = n_ctx + x_l.shape[0]
    tb = TOK_TILE
    n_ctx_tiles = n_ctx // tb

    def mod_idx(i):
        return (jnp.where(i < n_ctx_tiles, 0, 1 + (i - n_ctx_tiles) // n_seq_tiles), 0, 0)

    ctx_blk = lambda i: jnp.minimum(i, n_ctx_tiles - 1)
    lat_blk = lambda i: jnp.maximum(i - n_ctx_tiles, 0)
    tok = lambda w: pl.BlockSpec((tb, w), lambda i: (i, 0))
    tok_c = lambda w: pl.BlockSpec((tb, w), lambda i: (ctx_blk(i), 0))
    tok_l = lambda w: pl.BlockSpec((tb, w), lambda i: (lat_blk(i), 0))
    full = lambda a: pl.BlockSpec(a.shape, lambda i: (0,) * a.ndim)
    return pl.pallas_call(
        functools.partial(_outproj_kernel, n_ctx_tiles=n_ctx_tiles),
        grid=(n // tb,),
        in_specs=[pl.BlockSpec((ATT_WIDTH, tb), lambda i: (0, ctx_blk(i))),
                  pl.BlockSpec((ATT_WIDTH, tb), lambda i: (0, lat_blk(i))),
                  tok_c(GLA_WIDTH), tok_l(GLA_WIDTH), tok_c(GLA_WIDTH), tok_l(GLA_WIDTH),
                  tok(GLA_WIDTH), tok_c(D_MODEL), tok_l(D_MODEL),
                  pl.BlockSpec((None, 1, mod3.shape[2]), mod_idx),
                  full(wo), full(gn), full(l1g), full(l1b), full(wrt), full(sw13), full(sw2)],
        out_specs=(tok(D_MODEL),
                   pl.BlockSpec((tb * PACK_CHUNKS, LANES), lambda i: (i, 0)),
                   pl.BlockSpec((N_EXPERTS, tb), lambda i: (0, i))),
        out_shape=(jax.ShapeDtypeStruct((n, D_MODEL), F32),
                   jax.ShapeDtypeStruct((n * PACK_CHUNKS, LANES), U32),
                   jax.ShapeDtypeStruct((N_EXPERTS, n), F32)),
        compiler_params=_cparams(1), name="out_projection",
    )(att_c, att_l, of_c, of_l, ob_c, ob_l, gg, x_c, x_l, mod3, wo, gn, l1g, l1b, wrt, sw13, sw2)


def _route_kernel(lg_ref, bias_ref, upper_ref, idx_ref, w_ref, pos_ref, cnt_ref, run_ref):
    i = pl.program_id(0)

    @pl.when(i == 0)
    def _():
        run_ref[...] = jnp.zeros_like(run_ref)

    s = jax.nn.sigmoid(lg_ref[...])
    work = s + bias_ref[...]
    rows = lax.broadcasted_iota(I32, s.shape, 0).astype(F32)
    sel = jnp.zeros(s.shape, F32)
    idxs, vals = [], []
    for _ in range(TOP_K):
        mx = jnp.max(work, axis=0, keepdims=True)
        idx = jnp.min(jnp.where(work == mx, rows, float(N_EXPERTS)), axis=0, keepdims=True)
        hit = rows == idx
        vals.append(jnp.sum(jnp.where(hit, s, 0.0), axis=0, keepdims=True))
        idxs.append(idx)
        sel = jnp.where(hit, 1.0, sel)
        work = jnp.where(hit, -jnp.inf, work)
    den = functools.reduce(jnp.add, vals)
    rank = jnp.dot(sel.astype(BF16), upper_ref[...], preferred_element_type=F32) + run_ref[:, 0:1]
    for k in range(TOP_K):
        idx_ref[k:k + 1, :] = idxs[k].astype(I32)
        w_ref[k:k + 1, :] = vals[k] / den * ROUTED_SCALE
        pos_ref[k:k + 1, :] = jnp.sum(jnp.where(rows == idxs[k], rank, 0.0), axis=0,
                                      keepdims=True).astype(I32)
    run_ref[...] = run_ref[...] + jnp.sum(sel, axis=1, keepdims=True)
    cnt_ref[...] = run_ref[...]


def _route(logits_t, bias_col, upper):
    n = logits_t.shape[1]
    tt = TOK_TILE
    row = lambda dt: jax.ShapeDtypeStruct((TOP_K, n), dt)
    blk = pl.BlockSpec((TOP_K, tt), lambda i: (0, i))
    return pl.pallas_call(
        _route_kernel,
        grid=(n // tt,),
        in_specs=[pl.BlockSpec((N_EXPERTS, tt), lambda i: (0, i)),
                  pl.BlockSpec((N_EXPERTS, 1), lambda i: (0, 0)),
                  pl.BlockSpec((tt, tt), lambda i: (0, 0))],
        out_specs=(blk, blk, blk, pl.BlockSpec((N_EXPERTS, LANES), lambda i: (0, 0))),
        out_shape=(row(I32), row(F32), row(I32), jax.ShapeDtypeStruct((N_EXPERTS, LANES), F32)),
        scratch_shapes=[pltpu.VMEM((N_EXPERTS, LANES), F32)],
        compiler_params=_cparams(1), name="route",
    )(logits_t, bias_col, upper)


def _dest_kernel(cnt_ref, lower_ref, idx_ref, pos_ref, dest_ref, bexp_ref, bval_ref, nused_ref):
    cnt = cnt_ref[...]
    nblk = jnp.floor((cnt + (MOE_ROWS - 1)) * (1.0 / MOE_ROWS))
    bstart = jnp.dot(lower_ref[...], nblk, precision=HIGHEST, preferred_element_type=F32)
    bend = bstart + nblk
    pstart = bstart[:, 0:1] * MOE_ROWS
    rows = lax.broadcasted_iota(I32, (N_EXPERTS, idx_ref.shape[1]), 0)
    for k in range(TOP_K):
        hit = rows == idx_ref[k:k + 1, :]
        dest_ref[k:k + 1, :] = (jnp.sum(jnp.where(hit, pstart, 0.0), axis=0, keepdims=True)
                                .astype(I32) + pos_ref[k:k + 1, :])

    @pl.when(pl.program_id(0) == 0)
    def _():
        nb = bexp_ref.shape[1]
        bid = lax.broadcasted_iota(I32, (N_EXPERTS, nb), 1).astype(F32)
        inside = jnp.logical_and(bid >= bstart[:, 0:1], bid < bend[:, 0:1])
        erow = lax.broadcasted_iota(I32, (N_EXPERTS, nb), 0).astype(F32)
        bexp_ref[...] = jnp.sum(jnp.where(inside, erow, 0.0), axis=0, keepdims=True).astype(I32)
        valid = jnp.clip(cnt[:, 0:1] - (bid - bstart[:, 0:1]) * MOE_ROWS, 0.0, float(MOE_ROWS))
        bval_ref[...] = jnp.sum(jnp.where(inside, valid, 0.0), axis=0, keepdims=True).astype(I32)
        nused_ref[...] = jnp.max(bend, axis=0, keepdims=True).astype(I32)


def _destinations(counts, lower, idx_t, pos_t, n_blocks_pad):
    n = idx_t.shape[1]
    tt = TOK_TILE
    blk = pl.BlockSpec((TOP_K, tt), lambda i: (0, i))
    one = lambda w: pl.BlockSpec((1, w), lambda i: (0, 0))
    return pl.pallas_call(
        _dest_kernel,
        grid=(n // tt,),
        in_specs=[pl.BlockSpec((N_EXPERTS, LANES), lambda i: (0, 0)),
                  pl.BlockSpec((N_EXPERTS, N_EXPERTS), lambda i: (0, 0)), blk, blk],
        out_specs=(blk, one(n_blocks_pad), one(n_blocks_pad), one(LANES)),
        out_shape=(jax.ShapeDtypeStruct((TOP_K, n), I32),
                   jax.ShapeDtypeStruct((1, n_blocks_pad), I32),
                   jax.ShapeDtypeStruct((1, n_blocks_pad), I32),
                   jax.ShapeDtypeStruct((1, LANES), I32)),
        compiler_params=_cparams(1), name="destinations",
    )(counts, lower, idx_t, pos_t)


SC_WINDOW = 128


def _invert_rows(dest_flat, n_rows):
    m = dest_flat.shape[0]
    mesh = plsc.VectorSubcoreMesh(core_axis_name="core", subcore_axis_name="subcore")

    @functools.partial(pl.kernel, out_type=jax.ShapeDtypeStruct((n_rows,), I32), mesh=mesh,
                       scratch_types=[])
    def invert(val_hbm, idx_hbm, out_hbm):
        def body(val_vmem, idx_vmem):
            pltpu.sync_copy(val_vmem.at[0], out_hbm.at[idx_vmem.at[0]])

        pltpu.emit_pipeline(
            body, grid=(m // SC_WINDOW,),
            in_specs=[pl.BlockSpec((1, SC_WINDOW), lambda i: (0, i)),
                      pl.BlockSpec((1, SC_WINDOW), lambda i: (0, i))],
            out_specs=[], core_axis_name=("core", "subcore"),
            dimension_semantics=(pltpu.PARALLEL,),
        )(val_hbm, idx_hbm)

    return invert(jnp.arange(m, dtype=I32).reshape(1, m), dest_flat.reshape(1, m))


PACK_CHUNKS = D_MODEL // (2 * LANES)
SRC_GROUP = 4
TOP_K_LOG2 = TOP_K.bit_length() - 1
PACK_CHUNKS_LOG2 = PACK_CHUNKS.bit_length() - 1
U32 = jnp.uint32


def _pack_rows(ref, x):
    bits = pltpu.bitcast(x.astype(BF16).astype(F32), U32)
    for s in range(PACK_CHUNKS):
        lo = bits[:, (2 * s) * LANES:(2 * s + 1) * LANES] >> 16
        hi = bits[:, (2 * s + 1) * LANES:(2 * s + 2) * LANES] & jnp.uint32(0xFFFF0000)
        ref[pl.ds(s, x.shape[0], stride=PACK_CHUNKS), :] = lo | hi


def _unpack_rows(ref, n_rows, row0=0):
    parts = []
    for s in range(PACK_CHUNKS):
        w = ref[pl.ds(row0 + s, n_rows, stride=PACK_CHUNKS), :]
        parts.append(pltpu.bitcast(w << 16, F32))
        parts.append(pltpu.bitcast(w & jnp.uint32(0xFFFF0000), F32))
    return jnp.concatenate(parts, axis=1).astype(BF16)


def _moe_kernel(bexp_ref, bval_ref, nused_ref, u2p_hbm, src_hbm, w1_ref, w3_ref, w2_ref, yt_hbm,
                u2p_vmem, w13_s, w2_s, xbuf, ybuf, src_smem, sem_in, sem_src, sem_out,
                *, n_tokens):
    b = pl.program_id(0)
    n_used = nused_ref[0]
    br = MOE_ROWS
    grp = SRC_GROUP * br
    trash0 = n_tokens * TOP_K

    def src_copy(g):
        return pltpu.make_async_copy(src_hbm.at[pl.ds(g * grp, grp)],
                                     src_smem.at[pl.ds(lax.rem(g, 2) * grp, grp)], sem_src)

    def out_wait(slot):
        pltpu.make_async_copy(ybuf.at[pl.ds(slot * br * ROW_CHUNKS, br * ROW_CHUNKS)],
                              yt_hbm.at[pl.ds(0, br * ROW_CHUNKS)], sem_out.at[slot]).wait()

    def src_base(blk):
        return lax.rem(blk // SRC_GROUP, 2) * grp + lax.rem(blk, SRC_GROUP) * br

    def scatter_row(blk_slot, sbase, valid, r):
        dst = jnp.where(r < valid, src_smem[sbase + r], trash0 + blk_slot * br + r)
        pltpu.make_async_copy(
            ybuf.at[pl.ds(pl.multiple_of((blk_slot * br + r) * ROW_CHUNKS, ROW_CHUNKS), ROW_CHUNKS)],
            yt_hbm.at[pl.ds(pl.multiple_of(dst * ROW_CHUNKS, ROW_CHUNKS), ROW_CHUNKS)],
            sem_out.at[blk_slot]).start(priority=r % 2 if isinstance(r, int) else 0)

    def gather_row(xslot, sbase, r):
        row = lax.shift_right_logical(src_smem[sbase + r], TOP_K_LOG2 - PACK_CHUNKS_LOG2)
        row = jnp.minimum(row & (-PACK_CHUNKS & 0x7FFFFFFF), (n_tokens - 1) * PACK_CHUNKS)
        dst = pl.multiple_of((xslot * br + r) * PACK_CHUNKS, PACK_CHUNKS)
        xbuf[pl.ds(dst, PACK_CHUNKS), :] = u2p_vmem[pl.ds(pl.multiple_of(row, PACK_CHUNKS), PACK_CHUNKS), :]

    @pl.when(b == 0)
    def _():
        cp = pltpu.make_async_copy(u2p_hbm, u2p_vmem, sem_in)
        cp.start()
        src_copy(0).start()
        ybuf[...] = jnp.zeros_like(ybuf)
        cp.wait()
        src_copy(0).wait()
        lax.fori_loop(0, br, lambda r, c: (gather_row(0, 0, r), c)[1], 0)

    @pl.when(b < n_used)
    def _():
        g = b // SRC_GROUP
        phase = lax.rem(b, SRC_GROUP)

        more = (g + 1) * SRC_GROUP < n_used

        @pl.when(jnp.logical_and(phase == 1, more))
        def _():
            src_copy(g + 1).start()

        @pl.when(jnp.logical_and(phase == SRC_GROUP - 1, more))
        def _():
            src_copy(g + 1).wait()

        e = bexp_ref[b]
        prev = bexp_ref[jnp.maximum(b - 1, 0)]

        @pl.when(jnp.logical_or(b == 0, e != prev))
        def _():
            w13_s[:, 0:EXPERT_FF] = w1_ref[...].astype(BF16)
            w13_s[:, EXPERT_FF:2 * EXPERT_FF] = w3_ref[...].astype(BF16)
            w2_s[...] = w2_ref[...].astype(BF16)

        valid = bval_ref[b]
        sbase = src_base(b)
        slot = lax.rem(b, 2)

        pb = jnp.maximum(b - 1, 0)
        p_valid = jnp.where(b > 0, bval_ref[pb], 0)
        p_base = src_base(pb)
        n_base = src_base(b + 1)
        for r in range(br):
            gather_row(1 - slot, n_base, r)
            scatter_row(1 - slot, p_base, p_valid, r)

        x = _unpack_rows(xbuf, br, slot * (br * PACK_CHUNKS))
        rows = lax.broadcasted_iota(I32, x.shape, 0)
        x = jnp.where(rows < valid, x, jnp.zeros_like(x))
        ab = jnp.dot(x, w13_s[...], preferred_element_type=F32)
        hid = (_silu(ab[:, 0:EXPERT_FF]) * ab[:, EXPERT_FF:2 * EXPERT_FF]).astype(BF16)
        y = jnp.dot(hid, w2_s[...], preferred_element_type=F32)

        @pl.when(b >= 1)
        def _():
            out_wait(slot)

        ybase = slot * br * ROW_CHUNKS
        for cidx in range(ROW_CHUNKS):
            ybuf[pl.ds(ybase + cidx, br, stride=ROW_CHUNKS), :] = y[:, cidx * LANES:(cidx + 1) * LANES]

        @pl.when(b == n_used - 1)
        def _():
            lax.fori_loop(0, br, lambda r, c: (scatter_row(slot, sbase, valid, r), c)[1], 0)
            out_wait(1 - slot)
            out_wait(slot)


def _moe_experts(bexp, bval, nused, u2p, row_src, w1, w3, w2, n_blocks, n_tokens):
    br = MOE_ROWS

    def w_idx(b, bexp, bval, nused):
        return (bexp[jnp.minimum(b, nused[0] - 1)], 0, 0)

    grid_spec = pltpu.PrefetchScalarGridSpec(
        num_scalar_prefetch=3, grid=(n_blocks,),
        in_specs=[pl.BlockSpec(memory_space=pl.ANY), pl.BlockSpec(memory_space=pl.ANY),
                  pl.BlockSpec((None, D_MODEL, EXPERT_FF), w_idx),
                  pl.BlockSpec((None, D_MODEL, EXPERT_FF), w_idx),
                  pl.BlockSpec((None, EXPERT_FF, D_MODEL), w_idx)],
        out_specs=pl.BlockSpec(memory_space=pl.ANY),
        scratch_shapes=[pltpu.VMEM(u2p.shape, U32),
                        pltpu.VMEM((D_MODEL, 2 * EXPERT_FF), BF16),
                        pltpu.VMEM((EXPERT_FF, D_MODEL), BF16),
                        pltpu.VMEM((2 * PACK_CHUNKS * br, LANES), U32),
                        pltpu.VMEM((2 * br * ROW_CHUNKS, LANES), F32),
                        pltpu.SMEM((2 * SRC_GROUP * br,), I32),
                        pltpu.SemaphoreType.DMA, pltpu.SemaphoreType.DMA,
                        pltpu.SemaphoreType.DMA((2,))])
    n_out_tiles = n_tokens * TOP_K + 2 * br
    return pl.pallas_call(
        functools.partial(_moe_kernel, n_tokens=n_tokens), grid_spec=grid_spec,
        out_shape=jax.ShapeDtypeStruct((n_out_tiles * ROW_CHUNKS, LANES), F32),
        compiler_params=pltpu.CompilerParams(dimension_semantics=("arbitrary",),
                                             vmem_limit_bytes=MOE_VMEM_LIMIT),
        name="moe_experts",
    )(bexp, bval, nused, u2p, row_src, w1, w3, w2)


def _combine_kernel(w_hbm, yt_ref, base_ref, mod_ref, g_ref, b_ref, yc_ref, yl_ref,
                    w_smem, acc_buf, sem_w, *, n_ctx_tiles):
    i = pl.program_id(0)
    n_steps = pl.num_programs(0)
    n_tok = acc_buf.shape[0] // ROW_CHUNKS
    n_idx = n_tok * TOP_K

    def w_copy(tile):
        return pltpu.make_async_copy(w_hbm.at[pl.ds(tile * n_idx, n_idx)],
                                     w_smem.at[pl.ds(lax.rem(tile, 2) * n_idx, n_idx)], sem_w)

    @pl.when(i == 0)
    def _():
        w_copy(i).start()

    w_copy(i).wait()

    @pl.when(i + 1 < n_steps)
    def _():
        w_copy(i + 1).start()

    wbase = lax.rem(i, 2) * n_idx

    def reduce_token(t, carry):
        acc = None
        for k in range(TOP_K):
            j = t * TOP_K + k
            row = pl.multiple_of(j * ROW_CHUNKS, ROW_CHUNKS)
            term = w_smem[wbase + j] * yt_ref[pl.ds(row, ROW_CHUNKS), :]
            acc = term if acc is None else acc + term
        acc_buf[pl.ds(pl.multiple_of(t * ROW_CHUNKS, ROW_CHUNKS), ROW_CHUNKS), :] = acc
        return carry

    lax.fori_loop(0, n_tok, reduce_token, 0)
    moe = _load_row_tiles(acc_buf, n_tok)
    gate2 = mod_ref[:, 5 * D_MODEL:6 * D_MODEL]
    y = _layer_norm(base_ref[...] + gate2 * moe, g_ref[...], b_ref[...])

    @pl.when(i < n_ctx_tiles)
    def _():
        yc_ref[...] = y

    @pl.when(i >= n_ctx_tiles)
    def _():
        yl_ref[...] = y


def _combine(w_flat, yt, base, mod3, l2g, l2b, n_ctx, seq_tokens):
    n = base.shape[0]
    tc = COMB_TILE
    n_ctx_tiles = n_ctx // tc
    n_seq_tiles = seq_tokens // tc

    def mod_idx(i):
        return (jnp.where(i < n_ctx_tiles, 0, 1 + (i - n_ctx_tiles) // n_seq_tiles), 0, 0)

    full = lambda a: pl.BlockSpec(a.shape, lambda i: (0,) * a.ndim)
    return pl.pallas_call(
        functools.partial(_combine_kernel, n_ctx_tiles=n_ctx_tiles),
        grid=(n // tc,),
        in_specs=[pl.BlockSpec(memory_space=pl.ANY),
                  pl.BlockSpec((tc * TOP_K * ROW_CHUNKS, LANES), lambda i: (i, 0)),
                  pl.BlockSpec((tc, D_MODEL), lambda i: (i, 0)),
                  pl.BlockSpec((None, 1, mod3.shape[2]), mod_idx), full(l2g), full(l2b)],
        out_specs=(pl.BlockSpec((tc, D_MODEL), lambda i: (jnp.minimum(i, n_ctx_tiles - 1), 0)),
                   pl.BlockSpec((tc, D_MODEL), lambda i: (jnp.maximum(i - n_ctx_tiles, 0), 0))),
        out_shape=(jax.ShapeDtypeStruct((n_ctx, D_MODEL), F32),
                   jax.ShapeDtypeStruct((n - n_ctx, D_MODEL), F32)),
        scratch_shapes=[pltpu.SMEM((2 * tc * TOP_K,), F32),
                        pltpu.VMEM((tc * ROW_CHUNKS, LANES), F32),
                        pltpu.SemaphoreType.DMA],
        compiler_params=_cparams(1), name="combine",
    )(w_flat, yt, base, mod3, l2g, l2b)


def _rope_tables(n_tok, tile):
    rows = n_tok // GRID_W
    row_idx = jnp.repeat(jnp.arange(rows, dtype=F32), GRID_W)
    col_idx = jnp.tile(jnp.arange(GRID_W, dtype=F32), rows)
    inv_freq = 1.0 / (ROPE_THETA ** (jnp.arange(0, ROPE_AXIS_DIM, 2, dtype=F32) / ROPE_AXIS_DIM))
    ang_r = row_idx[:, None] * inv_freq[None, :]
    ang_c = col_idx[:, None] * inv_freq[None, :]
    ang = jnp.concatenate([ang_r, ang_r, ang_c, ang_c], axis=-1)
    cos, sin = jnp.cos(ang), jnp.sin(ang)
    quarter = (jnp.arange(HEAD_DIM) // (ROPE_AXIS_DIM // 2)) % 2
    sin_a = jnp.where(quarter == 0, -sin, 0.0)
    sin_b = jnp.where(quarter == 1, sin, 0.0)
    rep = LANES // HEAD_DIM
    ident = lambda v: jnp.full((tile, LANES), v, F32)
    cos_t = jnp.concatenate([jnp.tile(cos, (1, rep)), ident(1.0)], axis=0)
    sa_t = jnp.concatenate([jnp.tile(sin_a, (1, rep)), ident(0.0)], axis=0)
    sb_t = jnp.concatenate([jnp.tile(sin_b, (1, rep)), ident(0.0)], axis=0)
    ident_tr = lambda v: jnp.full((HEAD_DIM, tile), v, F32)
    cos_tr = jnp.concatenate([cos.T, ident_tr(1.0)], axis=1)
    sin_tr = jnp.concatenate([sin.T, ident_tr(0.0)], axis=1)
    return cos_t, sa_t, sb_t, cos_tr, sin_tr


def _dup_heads(a):
    parts = []
    for h in range(KV_HEADS):
        blk = a[..., h * HEAD_DIM:(h + 1) * HEAD_DIM]
        parts += [blk] * (LANES // HEAD_DIM)
    return jnp.concatenate(parts, axis=-1)


def kernel(x_prompt, x_sample, cache_k, cache_v, state_gla_fwd, state_gla_bwd, c, c_ctx, w_ada, b_ada, w_in, q_norm, k_norm, gla_wa_fwd, gla_ba_fwd, gla_wa_bwd, gla_ba_bwd, gla_norm, w_out, ln1_g, ln1_b, ln2_g, ln2_b, w_router, router_bias, exp_w1, exp_w3, exp_w2, sh_w1, sh_w3, sh_w2):
    n_ctx_b, ctx_seq, _ = x_prompt.shape
    n_lat_b, lat_seq, _ = x_sample.shape
    n_ctx = n_ctx_b * ctx_seq
    n_lat = n_lat_b * lat_seq
    n = n_ctx + n_lat
    l = 0

    x_c = x_prompt.reshape(n_ctx, D_MODEL)
    x_l = x_sample.reshape(n_lat, D_MODEL)

    c_rows = jnp.zeros((SUBLANES, D_MODEL), F32).at[0].set(c_ctx).at[1:1 + n_lat_b].set(c)
    mod = _modulation(c_rows, w_ada[l], b_ada[l][None, :])
    mod3 = mod.reshape(SUBLANES, 1, 6 * D_MODEL)

    wi = w_in[l]
    o_q, o_k, o_v, o_gq, o_gk, o_gv, o_gg, o_rf, o_rb, o_end = np.cumsum(
        [0, ATT_WIDTH, KV_HEADS * HEAD_DIM, KV_HEADS * HEAD_DIM, GLA_KW, GLA_KW, GLA_WIDTH, GLA_WIDTH,
         GLA_GATE_RANK, GLA_GATE_RANK])
    w_tok = jnp.concatenate([
        _dup_heads(wi[:, o_k:o_v]), wi[:, o_v:o_gq], wi[:, o_gq:o_gk],
        wi[:, o_gv:o_gg], wi[:, o_gg:o_rf], wi[:, o_rf:o_end],
        jnp.zeros((D_MODEL, LANES - 2 * GLA_GATE_RANK), F32)], axis=1).astype(BF16)
    w_tr = jnp.concatenate([wi[:, o_q:o_k], wi[:, o_v:o_gq], wi[:, o_gk:o_gv], wi[:, o_rf:o_end]],
                           axis=1).T.astype(BF16)
    rep = LANES // HEAD_DIM
    qn = q_norm[l][:, None]
    kn = jnp.tile(k_norm[l], rep)[None, :]
    seg = jnp.asarray(np.kron(np.eye(rep), np.ones((HEAD_DIM, HEAD_DIM))), BF16)
    wa = jnp.zeros((LANES, 2 * GLA_KW), F32)
    wa = wa.at[0:GLA_GATE_RANK, 0:GLA_KW].set(gla_wa_fwd[l])
    wa = wa.at[GLA_GATE_RANK:2 * GLA_GATE_RANK, GLA_KW:].set(gla_wa_bwd[l])
    ba = jnp.concatenate([gla_ba_fwd[l], gla_ba_bwd[l]])[None, :]
    wat = wa[0:2 * GLA_GATE_RANK, :].T
    bat = ba.T
    cos_t, sa_t, sb_t, cos_tr, sin_tr = _rope_tables(lat_seq, TOK_TILE)

    (qt, k_dup, vt, k32, v32, gq, gv, gg, la, gkt, lat) = _in_projection(
        x_c, x_l, mod3, w_tok, w_tr, qn, kn, cos_t, sa_t, sb_t, cos_tr, sin_tr, seg, wa, ba, wat, bat,
        lat_seq // TOK_TILE)

    ck = _dup_heads(cache_k[:, l].reshape(n_lat_b, -1, KV_HEADS * HEAD_DIM)).astype(BF16)
    cvt = cache_v[:, l].reshape(n_lat_b, -1, KV_HEADS * HEAD_DIM).transpose(0, 2, 1).astype(BF16)
    att_c = _attention(qt, k_dup, vt, None, 0, n_ctx_b, ctx_seq)
    att_l = _attention(qt, k_dup, vt, (ck, cvt), n_ctx, n_lat_b, lat_seq)

    gconst, levels_of = _gla_constants()
    to_dev = lambda t: (jnp.asarray(t[0], BF16), jnp.asarray(t[1], BF16), jnp.asarray(t[2], F32))
    bd = jnp.asarray(np.kron(np.eye(GLA_HEADS), np.ones((GLA_DK, GLA_DV))), BF16)
    vbd = jnp.asarray(np.kron(np.eye(GLA_HEADS), np.ones((GLA_CHUNK, GLA_DV))), BF16)
    consts = ((to_dev(gconst["f"]), to_dev(gconst["b"])), levels_of, bd, vbd)
    s_zero = jnp.zeros((n_ctx_b, GLA_HEADS, GLA_DK, GLA_DV), F32)
    of_c, ob_c, sf_new, sb_new = _gla(gq, la, gkt, lat, gv, s_zero, s_zero, consts, 0, n_ctx_b, ctx_seq)
    of_l, ob_l, _, _ = _gla(gq, la, gkt, lat, gv, state_gla_fwd[:, l], state_gla_bwd[:, l], consts,
                            n_ctx, n_lat_b, lat_seq)

    sw13 = jnp.concatenate([sh_w1[l], sh_w3[l]], axis=1).astype(BF16)
    base, u2_rows, logits_t = _out_projection(
        att_c, att_l, of_c, of_l, ob_c, ob_l, gg, x_c, x_l, mod3, w_out[l].astype(BF16),
        gla_norm[l][None, :], ln1_g[l][None, :], ln1_b[l][None, :], w_router[l].T.astype(BF16), sw13,
        sh_w2[l].astype(BF16), lat_seq // TOK_TILE)

    upper = jnp.asarray(np.triu(np.ones((TOK_TILE, TOK_TILE)), 1), BF16)
    idx_t, w_t, pos_t, counts = _route(logits_t, router_bias[l][:, None], upper)
    n_blocks = n * TOP_K // MOE_ROWS + N_EXPERTS
    n_blocks_pad = -(-n_blocks // LANES) * LANES
    lower = jnp.asarray(np.tril(np.ones((N_EXPERTS, N_EXPERTS)), -1), F32)
    dest_t, bexp, bval, nused = _destinations(counts, lower, idx_t, pos_t, n_blocks_pad)
    dest_flat = dest_t.T.reshape(-1)
    w_flat = w_t.T.reshape(-1)

    row_src = _invert_rows(dest_flat, n_blocks * MOE_ROWS)
    yt = _moe_experts(bexp.reshape(-1), bval.reshape(-1), nused.reshape(-1)[0:1], u2_rows, row_src,
                      exp_w1[l], exp_w3[l], exp_w2[l], n_blocks, n)
    y_c, y_l = _combine(w_flat, yt, base, mod3, ln2_g[l][None, :], ln2_b[l][None, :], n_ctx, lat_seq)

    y_prompt = y_c.reshape(n_ctx_b, ctx_seq, D_MODEL)
    y_sample = y_l.reshape(n_lat_b, lat_seq, D_MODEL)
    new_cache_k = k32.reshape(n_ctx_b, 1, ctx_seq, KV_HEADS, HEAD_DIM)
    new_cache_v = v32.reshape(n_ctx_b, 1, ctx_seq, KV_HEADS, HEAD_DIM)
    return (y_prompt, y_sample, new_cache_k, new_cache_v, sf_new[:, None], sb_new[:, None])
```

```python
import functools

import numpy as np
import jax
import jax.numpy as jnp
from jax import lax
from jax.experimental import pallas as pl
from jax.experimental.pallas import tpu as pltpu
from jax.experimental.pallas import tpu_sc as plsc

F32 = jnp.float32
BF16 = jnp.bfloat16
I32 = jnp.int32

D_MODEL = 1024
GRID_W = 64
HEAD_DIM = 64
N_HEADS = 8
KV_HEADS = 2
ATT_WIDTH = N_HEADS * HEAD_DIM
ATT_SCALE = HEAD_DIM ** -0.5
LOG2_E = 1.4426950408889634
ROPE_AXIS_DIM = HEAD_DIM // 2
ROPE_THETA = 10000.0
GLA_HEADS = 4
GLA_DK = 64
GLA_DV = 128
GLA_WIDTH = GLA_HEADS * GLA_DV
GLA_KW = GLA_HEADS * GLA_DK
GLA_GATE_RANK = 16
GLA_TAU = 16.0
N_EXPERTS = 256
TOP_K = 8
EXPERT_FF = 256
SHARED_FF = 256
ROUTED_SCALE = 2.5
DEPTH = 1
ALPHA = (2.0 * DEPTH) ** 0.25
EPS = 1e-6

LANES = 128
SUBLANES = 8
ROW_CHUNKS = D_MODEL // LANES
VMEM_LIMIT = 56 * 1024 * 1024

TOK_TILE = 512
ATT_TQ = 128
GLA_CHUNK = 128
GLA_LEVELS = ((32, 128), (8, 32), (2, 8), (1, 2))
MOE_ROWS = 256
MOE_VMEM_LIMIT = 62 * 1024 * 1024
COMB_TILE = 128
HIGHEST = lax.Precision.HIGHEST


def _cparams(n_axes):
    return pltpu.CompilerParams(dimension_semantics=("arbitrary",) * n_axes,
                                vmem_limit_bytes=VMEM_LIMIT)


def _silu(x):
    return x * jax.nn.sigmoid(x)


def _log_sigmoid(x):
    return jnp.minimum(x, 0.0) - jnp.log(1.0 + jnp.exp(-jnp.abs(x)))


def _load_row_tiles(ref, n_rows, row0=0):
    return jnp.concatenate(
        [ref[pl.ds(row0 * ROW_CHUNKS + cidx, n_rows, stride=ROW_CHUNKS), :] for cidx in range(ROW_CHUNKS)],
        axis=1)


def _store_row_tiles(ref, x):
    for cidx in range(ROW_CHUNKS):
        ref[pl.ds(cidx, x.shape[0], stride=ROW_CHUNKS), :] = x[:, cidx * LANES:(cidx + 1) * LANES]


def _layer_norm(z, g, b):
    mu = jnp.mean(z, axis=-1, keepdims=True)
    zc = z - mu
    var = jnp.mean(zc * zc, axis=-1, keepdims=True)
    return zc * lax.rsqrt(var + EPS) * g + b


def _mod_kernel(c_ref, w_ref, b_ref, o_ref):
    s = _silu(c_ref[...]).astype(BF16)
    o_ref[...] = jnp.dot(s, w_ref[...].astype(BF16), preferred_element_type=F32) + b_ref[...]


def _modulation(c_rows, w_ada, b_ada):
    n_cols = w_ada.shape[1]
    tn = 512
    return pl.pallas_call(
        _mod_kernel,
        grid=(n_cols // tn,),
        in_specs=[pl.BlockSpec((SUBLANES, D_MODEL), lambda j: (0, 0)),
                  pl.BlockSpec((D_MODEL, tn), lambda j: (0, j)),
                  pl.BlockSpec((1, tn), lambda j: (0, j))],
        out_specs=pl.BlockSpec((SUBLANES, tn), lambda j: (0, j)),
        out_shape=jax.ShapeDtypeStruct((SUBLANES, n_cols), F32),
        compiler_params=_cparams(1),
        name="modulation",
    )(c_rows, w_ada, b_ada)


_C_K = 0
_C_V = _C_K + 2 * LANES
_C_GQ = _C_V + KV_HEADS * HEAD_DIM
_C_GV = _C_GQ + GLA_KW
_C_GG = _C_GV + GLA_WIDTH
_C_RA = _C_GG + GLA_WIDTH
_C_END = _C_RA + LANES
_R_Q = 0
_R_V = _R_Q + ATT_WIDTH
_R_GK = _R_V + KV_HEADS * HEAD_DIM
_R_RA = _R_GK + GLA_KW
_R_END = _R_RA + 2 * GLA_GATE_RANK


def _inproj_kernel(xc_ref, xl_ref, mod_ref, w_ref, wt_ref, qn_ref, kn_ref, cos_ref, sa_ref, sb_ref,
                   cost_ref, sint_ref, seg_ref, wa_ref, ba_ref, wat_ref, bat_ref,
                   qt_ref, k_ref, vt_ref, k32_ref, v32_ref, gq_ref, gv_ref, gg_ref,
                   la_ref, gkt_ref, lat_ref, *, n_ctx_tiles):
    i = pl.program_id(0)
    m = mod_ref[...]
    shift1 = m[:, 0:D_MODEL]
    scale1 = m[:, D_MODEL:2 * D_MODEL]
    x = jnp.where(i < n_ctx_tiles, xc_ref[...], xl_ref[...])
    u = (x * (1.0 + scale1) + shift1).astype(BF16)

    cos = cos_ref[...]
    sin_a = sa_ref[...]
    sin_b = sb_ref[...]
    seg = seg_ref[...]
    lane = lax.broadcasted_iota(I32, (u.shape[0], LANES), 1)
    low = lane < HEAD_DIM

    def proj(c0, c1):
        return jnp.dot(u, w_ref[:, c0:c1], preferred_element_type=F32)

    def head_norm(blk, gain):
        ss = jnp.dot((blk * blk).astype(BF16), seg, preferred_element_type=F32) * (1.0 / HEAD_DIM)
        return blk * lax.rsqrt(ss + EPS) * gain

    def rope(blk):
        return (blk * cos + pltpu.roll(blk, LANES - ROPE_AXIS_DIM // 2, 1) * sin_a
                + pltpu.roll(blk, ROPE_AXIS_DIM // 2, 1) * sin_b)

    pk = proj(_C_K, _C_V)
    kn = [head_norm(pk[:, j * LANES:(j + 1) * LANES], kn_ref[...]) for j in range(KV_HEADS)]
    for j in range(KV_HEADS):
        k_ref[:, j * LANES:(j + 1) * LANES] = rope(kn[j]).astype(BF16)

    @pl.when(i < n_ctx_tiles)
    def _():
        k32_ref[...] = jnp.where(low, kn[0], kn[1])
        v32_ref[...] = proj(_C_V, _C_GQ)

    gq_ref[...] = proj(_C_GQ, _C_GV) * (GLA_DK ** -0.5)
    gv_ref[...] = proj(_C_GV, _C_GG).astype(BF16)
    gg_ref[...] = proj(_C_GG, _C_RA).astype(BF16)

    ra = proj(_C_RA, _C_END)
    pre = jnp.dot(ra, wa_ref[...], precision=HIGHEST, preferred_element_type=F32) + ba_ref[...]
    la_ref[...] = _log_sigmoid(pre) * (1.0 / GLA_TAU)

    pt = lax.dot_general(wt_ref[...], u, (((1,), (1,)), ((), ())), preferred_element_type=F32)
    cos_t = cost_ref[...]
    sin_t = sint_ref[...]
    quarter = ROPE_AXIS_DIM // 2
    for h in range(N_HEADS):
        blk = pt[_R_Q + h * HEAD_DIM:_R_Q + (h + 1) * HEAD_DIM, :]
        ms = jnp.mean(blk * blk, axis=0, keepdims=True)
        qn = blk * lax.rsqrt(ms + EPS) * qn_ref[...]
        rot = jnp.concatenate([-qn[quarter:2 * quarter], qn[0:quarter],
                               -qn[3 * quarter:4 * quarter], qn[2 * quarter:3 * quarter]], axis=0)
        qt_ref[h * HEAD_DIM:(h + 1) * HEAD_DIM, :] = (
            (qn * cos_t + rot * sin_t) * (ATT_SCALE * LOG2_E)).astype(BF16)
    vt_ref[...] = pt[_R_V:_R_GK, :].astype(BF16)
    gkt_ref[...] = pt[_R_GK:_R_RA, :]
    rat = pt[_R_RA:_R_END, :]
    pre_t = jnp.dot(wat_ref[...], rat, precision=HIGHEST, preferred_element_type=F32) + bat_ref[...]
    lat_ref[...] = _log_sigmoid(pre_t) * (1.0 / GLA_TAU)


def _in_projection(x_c, x_l, mod3, w_tok, w_tr, qn, kn, cos_t, sa_t, sb_t, cos_tr, sin_tr, seg, wa, ba,
                   wat, bat, n_seq_tiles):
    n_ctx = x_c.shape[0]
    n = n_ctx + x_l.shape[0]
    tb = TOK_TILE
    n_ctx_tiles = n_ctx // tb
    n_tiles = n // tb
    n_rope_blocks = cos_t.shape[0] // tb - 1

    def mod_idx(i):
        return (jnp.where(i < n_ctx_tiles, 0, 1 + (i - n_ctx_tiles) // n_seq_tiles), 0, 0)

    def rope_blk(i):
        return jnp.where(i < n_ctx_tiles, n_rope_blocks, (i - n_ctx_tiles) % n_seq_tiles)

    def rope_idx(i):
        return (rope_blk(i), 0)

    def ctx_idx(i):
        return (jnp.minimum(i, n_ctx_tiles - 1), 0)

    tok = lambda w: pl.BlockSpec((tb, w), lambda i: (i, 0))
    full = lambda a: pl.BlockSpec(a.shape, lambda i: (0,) * a.ndim)
    tr = lambda r: pl.BlockSpec((r, tb), lambda i: (0, i))
    rope_tr = pl.BlockSpec((HEAD_DIM, tb), lambda i: (0, rope_blk(i)))
    out_shapes = (
        jax.ShapeDtypeStruct((ATT_WIDTH, n), BF16),
        jax.ShapeDtypeStruct((n, 2 * LANES), BF16),
        jax.ShapeDtypeStruct((KV_HEADS * HEAD_DIM, n), BF16),
        jax.ShapeDtypeStruct((n_ctx, LANES), F32),
        jax.ShapeDtypeStruct((n_ctx, LANES), F32),
        jax.ShapeDtypeStruct((n, GLA_KW), F32),
        jax.ShapeDtypeStruct((n, GLA_WIDTH), BF16),
        jax.ShapeDtypeStruct((n, GLA_WIDTH), BF16),
        jax.ShapeDtypeStruct((n, 2 * GLA_KW), F32),
        jax.ShapeDtypeStruct((GLA_KW, n), F32),
        jax.ShapeDtypeStruct((2 * GLA_KW, n), F32),
    )
    out_specs = (tr(ATT_WIDTH), tok(2 * LANES), tr(KV_HEADS * HEAD_DIM),
                 pl.BlockSpec((tb, LANES), ctx_idx), pl.BlockSpec((tb, LANES), ctx_idx),
                 tok(GLA_KW), tok(GLA_WIDTH), tok(GLA_WIDTH), tok(2 * GLA_KW),
                 tr(GLA_KW), tr(2 * GLA_KW))
    in_specs = [pl.BlockSpec((tb, D_MODEL), ctx_idx),
                pl.BlockSpec((tb, D_MODEL), lambda i: (jnp.maximum(i - n_ctx_tiles, 0), 0)),
                pl.BlockSpec((None, 1, mod3.shape[2]), mod_idx),
                full(w_tok), full(w_tr), full(qn), full(kn),
                pl.BlockSpec((tb, LANES), rope_idx), pl.BlockSpec((tb, LANES), rope_idx),
                pl.BlockSpec((tb, LANES), rope_idx), rope_tr, rope_tr,
                full(seg), full(wa), full(ba), full(wat), full(bat)]
    return pl.pallas_call(
        functools.partial(_inproj_kernel, n_ctx_tiles=n_ctx_tiles),
        grid=(n_tiles,), in_specs=in_specs, out_specs=out_specs, out_shape=out_shapes,
        compiler_params=_cparams(1), name="in_projection",
    )(x_c, x_l, mod3, w_tok, w_tr, qn, kn, cos_t, sa_t, sb_t, cos_tr, sin_tr, seg, wa, ba, wat, bat)


def _attention_kernel(*refs, n_kv_parts):
    qt_ref = refs[0]
    k_refs = refs[1:1 + n_kv_parts]
    vt_refs = refs[1 + n_kv_parts:1 + 2 * n_kv_parts]
    o_ref = refs[1 + 2 * n_kv_parts]
    tq = qt_ref.shape[1]
    group = N_HEADS // KV_HEADS
    for kv in range(KV_HEADS):
        heads = range(kv * group, (kv + 1) * group)
        q_grp = jnp.concatenate([qt_ref[h * HEAD_DIM:(h + 1) * HEAD_DIM, :] for h in heads], axis=1)
        rhs = jnp.concatenate([q_grp, jnp.zeros_like(q_grp)], axis=0)
        s = [jnp.dot(k[:, kv * LANES:(kv + 1) * LANES], rhs, preferred_element_type=F32)
             for k in k_refs]
        mx = functools.reduce(jnp.maximum, [jnp.max(x, axis=0, keepdims=True) for x in s])
        pr = [jnp.exp2(x - mx) for x in s]
        den = functools.reduce(jnp.add, [jnp.sum(x, axis=0, keepdims=True) for x in pr])
        acc = functools.reduce(jnp.add, [
            jnp.dot(vt[kv * HEAD_DIM:(kv + 1) * HEAD_DIM, :], x.astype(BF16),
                    preferred_element_type=F32) for x, vt in zip(pr, vt_refs)])
        out = (acc / den).astype(BF16)
        for j, h in enumerate(heads):
            o_ref[h * HEAD_DIM:(h + 1) * HEAD_DIM, :] = out[:, j * tq:(j + 1) * tq]


def _attention(qt, k, vt, extra_kv, row0, n_batch, seq):
    tq = ATT_TQ
    n_q = seq // tq
    q_blk0 = row0 // tq
    kv_blk0 = row0 // seq
    in_specs = [pl.BlockSpec((ATT_WIDTH, tq), lambda b, i: (0, q_blk0 + b * n_q + i))]
    k_spec = pl.BlockSpec((seq, 2 * LANES), lambda b, i: (kv_blk0 + b, 0))
    vt_spec = pl.BlockSpec((KV_HEADS * HEAD_DIM, seq), lambda b, i: (0, kv_blk0 + b))
    args_k, args_v, specs_k, specs_v = [k], [vt], [k_spec], [vt_spec]
    if extra_kv is not None:
        ck, cvt = extra_kv
        args_k.append(ck)
        args_v.append(cvt)
        specs_k.append(pl.BlockSpec((None, ck.shape[1], 2 * LANES), lambda b, i: (b, 0, 0)))
        specs_v.append(pl.BlockSpec((None, KV_HEADS * HEAD_DIM, cvt.shape[2]), lambda b, i: (b, 0, 0)))
    return pl.pallas_call(
        functools.partial(_attention_kernel, n_kv_parts=len(args_k)),
        grid=(n_batch, n_q),
        in_specs=in_specs + specs_k + specs_v,
        out_specs=pl.BlockSpec((ATT_WIDTH, tq), lambda b, i: (0, b * n_q + i)),
        out_shape=jax.ShapeDtypeStruct((ATT_WIDTH, n_batch * seq), BF16),
        compiler_params=_cparams(2), name="attention",
    )(qt, *args_k, *args_v)


def _gla_constants():
    c = GLA_CHUNK
    idx = np.arange(c)
    q_mats, k_mats, masks, levels_of = [], [], [], []
    for li, (s, p) in enumerate(GLA_LEVELS):
        start = (idx // s) * s
        end = start + s - 1
        k_mats.append(((idx[None, :] > idx[:, None]) & (idx[None, :] <= end[:, None])))
        for d in range(p // s - 1):
            lo = np.maximum(start - d * s, 0)
            q_mats.append((idx[None, :] >= lo[:, None]) & (idx[None, :] <= idx[:, None]))
            masks.append((idx[:, None] // p == idx[None, :] // p)
                         & (idx[:, None] // s - idx[None, :] // s - 1 == d))
            levels_of.append(li)
    masks.append(np.eye(c, dtype=bool))
    levels_of.append(len(GLA_LEVELS) - 1)
    q_mats.append(idx[None, :] <= idx[:, None])
    k_mats = k_mats[:-1]
    k_mats.append(idx[None, :] > idx[:, None])
    k_mats.append(np.ones((c, c), bool))
    out = {}
    for name, flip in (("f", False), ("b", True)):
        f = (lambda a: a[::-1, ::-1]) if flip else (lambda a: a)
        lq = np.concatenate([f(a) for a in q_mats], axis=0).astype(np.float32)
        lkt = np.concatenate([f(a).T for a in k_mats], axis=1).astype(np.float32)
        mk = np.stack([np.tile(f(a), (1, GLA_HEADS)) for a in masks]).astype(np.float32)
        out[name] = (np.concatenate([lq, lq], axis=1), np.concatenate([lkt, lkt], axis=0), mk)
    return out, tuple(levels_of)


def _gla_direction(q, g, gkt, gt, v, lq2, lkt2, masks_ref, bd, vbd, s_ref, levels_of):
    c = GLA_CHUNK
    n_var = len(levels_of)
    n_lev = len(GLA_LEVELS)
    g_hi = g.astype(BF16)
    g_lo = (g - g_hi.astype(F32)).astype(BF16)
    fq = jnp.dot(lq2, jnp.concatenate([g_hi, g_lo], axis=0), preferred_element_type=F32)
    gt_hi = gt.astype(BF16)
    gt_lo = (gt - gt_hi.astype(F32)).astype(BF16)
    fk = jnp.dot(jnp.concatenate([gt_hi, gt_lo], axis=1), lkt2, preferred_element_type=F32)

    def key_factor(f):
        return gkt * jnp.exp(fk[:, f * c:(f + 1) * c])

    q_var = [(q * jnp.exp(fq[vi * c:(vi + 1) * c, :])).astype(BF16) for vi in range(n_var - 1)]
    q_var.append(q.astype(BF16))
    a = jnp.zeros((c, GLA_HEADS * c), F32)
    for li in range(n_lev):
        kt = (key_factor(li) if li < n_lev - 1 else gkt).astype(BF16)
        xt = jnp.concatenate([kt] * GLA_HEADS, axis=1) * bd
        vis = [vi for vi in range(n_var) if levels_of[vi] == li]
        res = jnp.dot(jnp.concatenate([q_var[vi] for vi in vis], axis=0), xt,
                      preferred_element_type=F32)
        for r, vi in enumerate(vis):
            a = a + masks_ref[vi] * res[r * c:(r + 1) * c, :]
    q_in = (q * jnp.exp(fq[(n_var - 1) * c:n_var * c, :])).astype(BF16)
    state = s_ref[...]
    v_bd = jnp.concatenate([v] * GLA_HEADS, axis=0) * vbd
    o = (jnp.dot(q_in, state.astype(BF16), preferred_element_type=F32)
         + jnp.dot(a.astype(BF16), v_bd, preferred_element_type=F32))
    k_out = key_factor(n_lev - 1).astype(BF16)
    e_tot = jnp.exp(fk[:, n_lev * c:(n_lev + 1) * c])
    upd = jnp.dot(k_out, v, preferred_element_type=F32)
    s_ref[...] = (state * jnp.concatenate([e_tot] * (GLA_WIDTH // c), axis=1)
                  + upd * bd.astype(F32))
    return o


def _gla_kernel(gq_f, la_f, gkt_f, lat_f, gv_f, gq_b, la_b, gkt_b, lat_b, gv_b,
                s0f_ref, s0b_ref, lq2f, lkt2f, mkf, lq2b, lkt2b, mkb, bd_ref, vbd_ref,
                of_ref, ob_ref, sf_ref, sb_ref, st_f, st_b, *, levels_of):
    n = pl.program_id(1)

    @pl.when(n == 0)
    def _():
        st_f[...] = jnp.zeros_like(st_f)
        st_b[...] = jnp.zeros_like(st_b)
        for h in range(GLA_HEADS):
            rows = slice(h * GLA_DK, (h + 1) * GLA_DK)
            cols = slice(h * GLA_DV, (h + 1) * GLA_DV)
            st_f[rows, cols] = s0f_ref[h]
            st_b[rows, cols] = s0b_ref[h]

    bd = bd_ref[...]
    vbd = vbd_ref[...]
    of_ref[...] = _gla_direction(gq_f[...], la_f[...], gkt_f[...], lat_f[...], gv_f[...],
                                 lq2f[...], lkt2f[...], mkf, bd, vbd, st_f, levels_of)
    ob_ref[...] = _gla_direction(gq_b[...], la_b[...], gkt_b[...], lat_b[...], gv_b[...],
                                 lq2b[...], lkt2b[...], mkb, bd, vbd, st_b, levels_of)

    @pl.when(n == pl.num_programs(1) - 1)
    def _():
        for h in range(GLA_HEADS):
            rows = slice(h * GLA_DK, (h + 1) * GLA_DK)
            cols = slice(h * GLA_DV, (h + 1) * GLA_DV)
            sf_ref[h] = st_f[rows, cols]
            sb_ref[h] = st_b[rows, cols]


def _gla(gq, la, gkt, lat, gv, s0f, s0b, consts, row0, n_batch, seq):
    (cf, cb), levels_of, bd, vbd = consts
    c = GLA_CHUNK
    nc = seq // c
    blk0 = row0 // c
    n_la_blocks_b = 1
    fwd = lambda b, n: blk0 + b * nc + n
    bwd = lambda b, n: blk0 + b * nc + (nc - 1 - n)

    def tok(w, which, col=0):
        return pl.BlockSpec((c, w), lambda b, n: (which(b, n), col))

    def tr(r, which, row=0):
        return pl.BlockSpec((r, c), lambda b, n: (row, which(b, n)))

    full = lambda a: pl.BlockSpec(a.shape, lambda b, n: (0,) * a.ndim)
    st_spec = pl.BlockSpec((None, GLA_HEADS, GLA_DK, GLA_DV), lambda b, n: (b, 0, 0, 0))
    in_specs = [tok(GLA_KW, fwd), tok(GLA_KW, fwd, 0), tr(GLA_KW, fwd), tr(GLA_KW, fwd, 0),
                tok(GLA_WIDTH, fwd),
                tok(GLA_KW, bwd), tok(GLA_KW, bwd, n_la_blocks_b), tr(GLA_KW, bwd),
                tr(GLA_KW, bwd, 1), tok(GLA_WIDTH, bwd),
                st_spec, st_spec,
                full(cf[0]), full(cf[1]), full(cf[2]), full(cb[0]), full(cb[1]), full(cb[2]),
                full(bd), full(vbd)]
    out_specs = (pl.BlockSpec((c, GLA_WIDTH), lambda b, n: (b * nc + n, 0)),
                 pl.BlockSpec((c, GLA_WIDTH), lambda b, n: (b * nc + (nc - 1 - n), 0)),
                 st_spec, st_spec)
    out_shape = (jax.ShapeDtypeStruct((n_batch * seq, GLA_WIDTH), F32),
                 jax.ShapeDtypeStruct((n_batch * seq, GLA_WIDTH), F32),
                 jax.ShapeDtypeStruct((n_batch, GLA_HEADS, GLA_DK, GLA_DV), F32),
                 jax.ShapeDtypeStruct((n_batch, GLA_HEADS, GLA_DK, GLA_DV), F32))
    return pl.pallas_call(
        functools.partial(_gla_kernel, levels_of=levels_of),
        grid=(n_batch, nc), in_specs=in_specs, out_specs=out_specs, out_shape=out_shape,
        scratch_shapes=[pltpu.VMEM((GLA_KW, GLA_WIDTH), F32), pltpu.VMEM((GLA_KW, GLA_WIDTH), F32)],
        compiler_params=_cparams(2), name="gla",
    )(gq, la, gkt, lat, gv, gq, la, gkt, lat, gv, s0f, s0b,
      cf[0], cf[1], cf[2], cb[0], cb[1], cb[2], bd, vbd)


def _outproj_kernel(attc_ref, attl_ref, ofc_ref, ofl_ref, obc_ref, obl_ref, gg_ref, xc_ref, xl_ref,
                    mod_ref, wo_ref, gn_ref, l1g_ref, l1b_ref, wrt_ref, sw13_ref, sw2_ref,
                    base_ref, u2_ref, lg_ref, *, n_ctx_tiles):
    is_ctx = pl.program_id(0) < n_ctx_tiles
    pick = lambda a_ref, b_ref: jnp.where(is_ctx, a_ref[...], b_ref[...])
    m = mod_ref[...]
    gate1 = m[:, 2 * D_MODEL:3 * D_MODEL]
    shift2 = m[:, 3 * D_MODEL:4 * D_MODEL]
    scale2 = m[:, 4 * D_MODEL:5 * D_MODEL]
    gate2 = m[:, 5 * D_MODEL:6 * D_MODEL]
    og = pick(ofc_ref, ofl_ref) + pick(obc_ref, obl_ref)
    gg = gg_ref[...].astype(F32)
    parts = []
    for h in range(GLA_HEADS):
        blk = og[:, h * GLA_DV:(h + 1) * GLA_DV]
        ms = jnp.mean(blk * blk, axis=-1, keepdims=True)
        nb = blk * lax.rsqrt(ms + EPS) * gn_ref[...]
        parts.append((nb * _silu(gg[:, h * GLA_DV:(h + 1) * GLA_DV])).astype(BF16))
    att_t = pick(attc_ref, attl_ref)
    hmix = (lax.dot_general(att_t, wo_ref[0:ATT_WIDTH, :], (((0,), (0,)), ((), ())),
                            preferred_element_type=F32)
            + jnp.dot(jnp.concatenate(parts, axis=1), wo_ref[ATT_WIDTH:, :],
                      preferred_element_type=F32))
    x1 = _layer_norm(ALPHA * pick(xc_ref, xl_ref) + gate1 * hmix, l1g_ref[...], l1b_ref[...])
    u2 = x1 * (1.0 + scale2) + shift2
    u2b = u2.astype(BF16)
    lg_ref[...] = lax.dot_general(wrt_ref[...], u2b, (((1,), (1,)), ((), ())),
                                  preferred_element_type=F32)
    ab = jnp.dot(u2b, sw13_ref[...], preferred_element_type=F32)
    hid = (_silu(ab[:, 0:SHARED_FF]) * ab[:, SHARED_FF:2 * SHARED_FF]).astype(BF16)
    shared = jnp.dot(hid, sw2_ref[...], preferred_element_type=F32)
    base_ref[...] = ALPHA * x1 + gate2 * shared
    _pack_rows(u2_ref, u2)


def _out_projection(att_c, att_l, of_c, of_l, ob_c, ob_l, gg, x_c, x_l, mod3, wo, gn, l1g, l1b, wrt,
                    sw13, sw2, n_seq_tiles):
    n_ctx = x_c.shape[0]
    n = n_ctx + x_l.shape[0]
    tb = TOK_TILE
    n_ctx_tiles = n_ctx // tb

    def mod_idx(i):
        return (jnp.where(i < n_ctx_tiles, 0, 1 + (i - n_ctx_tiles) // n_seq_tiles), 0, 0)

    ctx_blk = lambda i: jnp.minimum(i, n_ctx_tiles - 1)
    lat_blk = lambda i: jnp.maximum(i - n_ctx_tiles, 0)
    tok = lambda w: pl.BlockSpec((tb, w), lambda i: (i, 0))
    tok_c = lambda w: pl.BlockSpec((tb, w), lambda i: (ctx_blk(i), 0))
    tok_l = lambda w: pl.BlockSpec((tb, w), lambda i: (lat_blk(i), 0))
    full = lambda a: pl.BlockSpec(a.shape, lambda i: (0,) * a.ndim)
    return pl.pallas_call(
        functools.partial(_outproj_kernel, n_ctx_tiles=n_ctx_tiles),
        grid=(n // tb,),
        in_specs=[pl.BlockSpec((ATT_WIDTH, tb), lambda i: (0, ctx_blk(i))),
                  pl.BlockSpec((ATT_WIDTH, tb), lambda i: (0, lat_blk(i))),
                  tok_c(GLA_WIDTH), tok_l(GLA_WIDTH), tok_c(GLA_WIDTH), tok_l(GLA_WIDTH),
                  tok(GLA_WIDTH), tok_c(D_MODEL), tok_l(D_MODEL),
                  pl.BlockSpec((None, 1, mod3.shape[2]), mod_idx),
                  full(wo), full(gn), full(l1g), full(l1b), full(wrt), full(sw13), full(sw2)],
        out_specs=(tok(D_MODEL),
                   pl.BlockSpec((tb * PACK_CHUNKS, LANES), lambda i: (i, 0)),
                   pl.BlockSpec((N_EXPERTS, tb), lambda i: (0, i))),
        out_shape=(jax.ShapeDtypeStruct((n, D_MODEL), F32),
                   jax.ShapeDtypeStruct((n * PACK_CHUNKS, LANES), U32),
                   jax.ShapeDtypeStruct((N_EXPERTS, n), F32)),
        compiler_params=_cparams(1), name="out_projection",
    )(att_c, att_l, of_c, of_l, ob_c, ob_l, gg, x_c, x_l, mod3, wo, gn, l1g, l1b, wrt, sw13, sw2)


def _route_kernel(lg_ref, bias_ref, upper_ref, idx_ref, w_ref, pos_ref, cnt_ref, run_ref):
    i = pl.program_id(0)

    @pl.when(i == 0)
    def _():
        run_ref[...] = jnp.zeros_like(run_ref)

    s = jax.nn.sigmoid(lg_ref[...])
    work = s + bias_ref[...]
    rows = lax.broadcasted_iota(I32, s.shape, 0).astype(F32)
    sel = jnp.zeros(s.shape, F32)
    idxs, vals = [], []
    for _ in range(TOP_K):
        mx = jnp.max(work, axis=0, keepdims=True)
        idx = jnp.min(jnp.where(work == mx, rows, float(N_EXPERTS)), axis=0, keepdims=True)
        hit = rows == idx
        vals.append(jnp.sum(jnp.where(hit, s, 0.0), axis=0, keepdims=True))
        idxs.append(idx)
        sel = jnp.where(hit, 1.0, sel)
        work = jnp.where(hit, -jnp.inf, work)
    den = functools.reduce(jnp.add, vals)
    rank = jnp.dot(sel.astype(BF16), upper_ref[...], preferred_element_type=F32) + run_ref[:, 0:1]
    for k in range(TOP_K):
        idx_ref[k:k + 1, :] = idxs[k].astype(I32)
        w_ref[k:k + 1, :] = vals[k] / den * ROUTED_SCALE
        pos_ref[k:k + 1, :] = jnp.sum(jnp.where(rows == idxs[k], rank, 0.0), axis=0,
                                      keepdims=True).astype(I32)
    run_ref[...] = run_ref[...] + jnp.sum(sel, axis=1, keepdims=True)
    cnt_ref[...] = run_ref[...]


def _route(logits_t, bias_col, upper):
    n = logits_t.shape[1]
    tt = TOK_TILE
    row = lambda dt: jax.ShapeDtypeStruct((TOP_K, n), dt)
    blk = pl.BlockSpec((TOP_K, tt), lambda i: (0, i))
    return pl.pallas_call(
        _route_kernel,
        grid=(n // tt,),
        in_specs=[pl.BlockSpec((N_EXPERTS, tt), lambda i: (0, i)),
                  pl.BlockSpec((N_EXPERTS, 1), lambda i: (0, 0)),
                  pl.BlockSpec((tt, tt), lambda i: (0, 0))],
        out_specs=(blk, blk, blk, pl.BlockSpec((N_EXPERTS, LANES), lambda i: (0, 0))),
        out_shape=(row(I32), row(F32), row(I32), jax.ShapeDtypeStruct((N_EXPERTS, LANES), F32)),
        scratch_shapes=[pltpu.VMEM((N_EXPERTS, LANES), F32)],
        compiler_params=_cparams(1), name="route",
    )(logits_t, bias_col, upper)


def _dest_kernel(cnt_ref, lower_ref, idx_ref, pos_ref, dest_ref, bexp_ref, bval_ref, nused_ref):
    cnt = cnt_ref[...]
    nblk = jnp.floor((cnt + (MOE_ROWS - 1)) * (1.0 / MOE_ROWS))
    bstart = jnp.dot(lower_ref[...], nblk, precision=HIGHEST, preferred_element_type=F32)
    bend = bstart + nblk
    pstart = bstart[:, 0:1] * MOE_ROWS
    rows = lax.broadcasted_iota(I32, (N_EXPERTS, idx_ref.shape[1]), 0)
    for k in range(TOP_K):
        hit = rows == idx_ref[k:k + 1, :]
        dest_ref[k:k + 1, :] = (jnp.sum(jnp.where(hit, pstart, 0.0), axis=0, keepdims=True)
                                .astype(I32) + pos_ref[k:k + 1, :])

    @pl.when(pl.program_id(0) == 0)
    def _():
        nb = bexp_ref.shape[1]
        bid = lax.broadcasted_iota(I32, (N_EXPERTS, nb), 1).astype(F32)
        inside = jnp.logical_and(bid >= bstart[:, 0:1], bid < bend[:, 0:1])
        erow = lax.broadcasted_iota(I32, (N_EXPERTS, nb), 0).astype(F32)
        bexp_ref[...] = jnp.sum(jnp.where(inside, erow, 0.0), axis=0, keepdims=True).astype(I32)
        valid = jnp.clip(cnt[:, 0:1] - (bid - bstart[:, 0:1]) * MOE_ROWS, 0.0, float(MOE_ROWS))
        bval_ref[...] = jnp.sum(jnp.where(inside, valid, 0.0), axis=0, keepdims=True).astype(I32)
        nused_ref[...] = jnp.max(bend, axis=0, keepdims=True).astype(I32)


def _destinations(counts, lower, idx_t, pos_t, n_blocks_pad):
    n = idx_t.shape[1]
    tt = TOK_TILE
    blk = pl.BlockSpec((TOP_K, tt), lambda i: (0, i))
    one = lambda w: pl.BlockSpec((1, w), lambda i: (0, 0))
    return pl.pallas_call(
        _dest_kernel,
        grid=(n // tt,),
        in_specs=[pl.BlockSpec((N_EXPERTS, LANES), lambda i: (0, 0)),
                  pl.BlockSpec((N_EXPERTS, N_EXPERTS), lambda i: (0, 0)), blk, blk],
        out_specs=(blk, one(n_blocks_pad), one(n_blocks_pad), one(LANES)),
        out_shape=(jax.ShapeDtypeStruct((TOP_K, n), I32),
                   jax.ShapeDtypeStruct((1, n_blocks_pad), I32),
                   jax.ShapeDtypeStruct((1, n_blocks_pad), I32),
                   jax.ShapeDtypeStruct((1, LANES), I32)),
        compiler_params=_cparams(1), name="destinations",
    )(counts, lower, idx_t, pos_t)


SC_WINDOW = 128


def _invert_rows(dest_flat, n_rows):
    m = dest_flat.shape[0]
    mesh = plsc.VectorSubcoreMesh(core_axis_name="core", subcore_axis_name="subcore")

    @functools.partial(pl.kernel, out_type=jax.ShapeDtypeStruct((n_rows,), I32), mesh=mesh,
                       scratch_types=[])
    def invert(val_hbm, idx_hbm, out_hbm):
        def body(val_vmem, idx_vmem):
            pltpu.sync_copy(val_vmem.at[0], out_hbm.at[idx_vmem.at[0]])

        pltpu.emit_pipeline(
            body, grid=(m // SC_WINDOW,),
            in_specs=[pl.BlockSpec((1, SC_WINDOW), lambda i: (0, i)),
                      pl.BlockSpec((1, SC_WINDOW), lambda i: (0, i))],
            out_specs=[], core_axis_name=("core", "subcore"),
            dimension_semantics=(pltpu.PARALLEL,),
        )(val_hbm, idx_hbm)

    return invert(jnp.arange(m, dtype=I32).reshape(1, m), dest_flat.reshape(1, m))


PACK_CHUNKS = D_MODEL // (2 * LANES)
SRC_GROUP = 4
TOP_K_LOG2 = TOP_K.bit_length() - 1
PACK_CHUNKS_LOG2 = PACK_CHUNKS.bit_length() - 1
U32 = jnp.uint32


def _pack_rows(ref, x, row0=0):
    bits = pltpu.bitcast(x.astype(BF16).astype(F32), U32)
    for s in range(PACK_CHUNKS):
        lo = bits[:, (2 * s) * LANES:(2 * s + 1) * LANES] >> 16
        hi = bits[:, (2 * s + 1) * LANES:(2 * s + 2) * LANES] & jnp.uint32(0xFFFF0000)
        ref[pl.ds(row0 + s, x.shape[0], stride=PACK_CHUNKS), :] = lo | hi


def _unpack_rows(ref, n_rows, row0=0):
    parts = []
    for s in range(PACK_CHUNKS):
        w = ref[pl.ds(row0 + s, n_rows, stride=PACK_CHUNKS), :]
        parts.append(pltpu.bitcast(w << 16, F32))
        parts.append(pltpu.bitcast(w & jnp.uint32(0xFFFF0000), F32))
    return jnp.concatenate(parts, axis=1).astype(BF16)


def _moe_kernel(bexp_ref, bval_ref, nused_ref, u2p_hbm, src_hbm, w1_ref, w3_ref, w2_ref, yt_hbm,
                u2p_vmem, w13_s, w2_s, xbuf, ybuf, src_smem, sem_in, sem_src, sem_out,
                *, n_tokens):
    b = pl.program_id(0)
    n_used = nused_ref[0]
    br = MOE_ROWS
    grp = SRC_GROUP * br
    trash0 = n_tokens * TOP_K

    def src_copy(g):
        return pltpu.make_async_copy(src_hbm.at[pl.ds(g * grp, grp)],
                                     src_smem.at[pl.ds(lax.rem(g, 2) * grp, grp)], sem_src)

    def out_wait(slot):
        pltpu.make_async_copy(ybuf.at[pl.ds(slot * br * PACK_CHUNKS, br * PACK_CHUNKS)],
                              yt_hbm.at[pl.ds(0, br * PACK_CHUNKS)], sem_out.at[slot]).wait()

    def src_base(blk):
        return lax.rem(blk // SRC_GROUP, 2) * grp + lax.rem(blk, SRC_GROUP) * br

    def scatter_row(blk_slot, sbase, valid, r):
        dst = jnp.where(r < valid, src_smem[sbase + r], trash0 + blk_slot * br + r)
        pltpu.make_async_copy(
            ybuf.at[pl.ds(pl.multiple_of((blk_slot * br + r) * PACK_CHUNKS, PACK_CHUNKS), PACK_CHUNKS)],
            yt_hbm.at[pl.ds(pl.multiple_of(dst * PACK_CHUNKS, PACK_CHUNKS), PACK_CHUNKS)],
            sem_out.at[blk_slot]).start(priority=r % 2 if isinstance(r, int) else 0)

    def gather_row(xslot, sbase, r):
        row = lax.shift_right_logical(src_smem[sbase + r], TOP_K_LOG2 - PACK_CHUNKS_LOG2)
        row = jnp.minimum(row & (-PACK_CHUNKS & 0x7FFFFFFF), (n_tokens - 1) * PACK_CHUNKS)
        dst = pl.multiple_of((xslot * br + r) * PACK_CHUNKS, PACK_CHUNKS)
        xbuf[pl.ds(dst, PACK_CHUNKS), :] = u2p_vmem[pl.ds(pl.multiple_of(row, PACK_CHUNKS), PACK_CHUNKS), :]

    @pl.when(b == 0)
    def _():
        cp = pltpu.make_async_copy(u2p_hbm, u2p_vmem, sem_in)
        cp.start()
        src_copy(0).start()
        ybuf[...] = jnp.zeros_like(ybuf)
        cp.wait()
        src_copy(0).wait()
        lax.fori_loop(0, br, lambda r, c: (gather_row(0, 0, r), c)[1], 0)

    @pl.when(b < n_used)
    def _():
        g = b // SRC_GROUP
        phase = lax.rem(b, SRC_GROUP)

        more = (g + 1) * SRC_GROUP < n_used

        @pl.when(jnp.logical_and(phase == 1, more))
        def _():
            src_copy(g + 1).start()

        @pl.when(jnp.logical_and(phase == SRC_GROUP - 1, more))
        def _():
            src_copy(g + 1).wait()

        e = bexp_ref[b]
        prev = bexp_ref[jnp.maximum(b - 1, 0)]

        @pl.when(jnp.logical_or(b == 0, e != prev))
        def _():
            w13_s[:, 0:EXPERT_FF] = w1_ref[...].astype(BF16)
            w13_s[:, EXPERT_FF:2 * EXPERT_FF] = w3_ref[...].astype(BF16)
            w2_s[...] = w2_ref[...].astype(BF16)

        valid = bval_ref[b]
        sbase = src_base(b)
        slot = lax.rem(b, 2)

        pb = jnp.maximum(b - 1, 0)
        p_valid = jnp.where(b > 0, bval_ref[pb], 0)
        p_base = src_base(pb)
        n_base = src_base(b + 1)
        for r in range(br):
            gather_row(1 - slot, n_base, r)
            scatter_row(1 - slot, p_base, p_valid, r)

        x = _unpack_rows(xbuf, br, slot * (br * PACK_CHUNKS))
        rows = lax.broadcasted_iota(I32, x.shape, 0)
        x = jnp.where(rows < valid, x, jnp.zeros_like(x))
        ab = jnp.dot(x, w13_s[...], preferred_element_type=F32)
        hid = (_silu(ab[:, 0:EXPERT_FF]) * ab[:, EXPERT_FF:2 * EXPERT_FF]).astype(BF16)
        y = jnp.dot(hid, w2_s[...], preferred_element_type=F32)

        @pl.when(b >= 1)
        def _():
            out_wait(slot)

        _pack_rows(ybuf, y, slot * (br * PACK_CHUNKS))

        @pl.when(b == n_used - 1)
        def _():
            lax.fori_loop(0, br, lambda r, c: (scatter_row(slot, sbase, valid, r), c)[1], 0)
            out_wait(1 - slot)
            out_wait(slot)


def _moe_experts(bexp, bval, nused, u2p, row_src, w1, w3, w2, n_blocks, n_tokens):
    br = MOE_ROWS

    def w_idx(b, bexp, bval, nused):
        return (bexp[jnp.minimum(b, nused[0] - 1)], 0, 0)

    grid_spec = pltpu.PrefetchScalarGridSpec(
        num_scalar_prefetch=3, grid=(n_blocks,),
        in_specs=[pl.BlockSpec(memory_space=pl.ANY), pl.BlockSpec(memory_space=pl.ANY),
                  pl.BlockSpec((None, D_MODEL, EXPERT_FF), w_idx),
                  pl.BlockSpec((None, D_MODEL, EXPERT_FF), w_idx),
                  pl.BlockSpec((None, EXPERT_FF, D_MODEL), w_idx)],
        out_specs=pl.BlockSpec(memory_space=pl.ANY),
        scratch_shapes=[pltpu.VMEM(u2p.shape, U32),
                        pltpu.VMEM((D_MODEL, 2 * EXPERT_FF), BF16),
                        pltpu.VMEM((EXPERT_FF, D_MODEL), BF16),
                        pltpu.VMEM((2 * PACK_CHUNKS * br, LANES), U32),
                        pltpu.VMEM((2 * br * PACK_CHUNKS, LANES), U32),
                        pltpu.SMEM((2 * SRC_GROUP * br,), I32),
                        pltpu.SemaphoreType.DMA, pltpu.SemaphoreType.DMA,
                        pltpu.SemaphoreType.DMA((2,))])
    n_out_tiles = n_tokens * TOP_K + 2 * br
    return pl.pallas_call(
        functools.partial(_moe_kernel, n_tokens=n_tokens), grid_spec=grid_spec,
        out_shape=jax.ShapeDtypeStruct((n_out_tiles * PACK_CHUNKS, LANES), U32),
        compiler_params=pltpu.CompilerParams(dimension_semantics=("arbitrary",),
                                             vmem_limit_bytes=MOE_VMEM_LIMIT),
        name="moe_experts",
    )(bexp, bval, nused, u2p, row_src, w1, w3, w2)


def _combine_kernel(w_hbm, yt_ref, base_ref, mod_ref, g_ref, b_ref, yc_ref, yl_ref,
                    w_smem, acc_lo, acc_hi, sem_w, *, n_ctx_tiles):
    i = pl.program_id(0)
    n_steps = pl.num_programs(0)
    n_tok = acc_lo.shape[0] // PACK_CHUNKS
    n_idx = n_tok * TOP_K

    def w_copy(tile):
        return pltpu.make_async_copy(w_hbm.at[pl.ds(tile * n_idx, n_idx)],
                                     w_smem.at[pl.ds(lax.rem(tile, 2) * n_idx, n_idx)], sem_w)

    @pl.when(i == 0)
    def _():
        w_copy(i).start()

    w_copy(i).wait()

    @pl.when(i + 1 < n_steps)
    def _():
        w_copy(i + 1).start()

    wbase = lax.rem(i, 2) * n_idx

    per_tile = SUBLANES // PACK_CHUNKS
    first = lax.broadcasted_iota(I32, (SUBLANES, LANES), 0) < PACK_CHUNKS

    def reduce_token(t, carry):
        lo = hi = None
        for m in range(TOP_K // per_tile):
            j = t * TOP_K + m * per_tile
            words = yt_ref[pl.ds(pl.multiple_of(j * PACK_CHUNKS, SUBLANES), SUBLANES), :]
            wgt = jnp.where(first, w_smem[wbase + j], w_smem[wbase + j + 1])
            t_lo = wgt * pltpu.bitcast(words << 16, F32)
            t_hi = wgt * pltpu.bitcast(words & jnp.uint32(0xFFFF0000), F32)
            lo = t_lo if lo is None else lo + t_lo
            hi = t_hi if hi is None else hi + t_hi
        row = pl.multiple_of(t * PACK_CHUNKS, PACK_CHUNKS)
        acc_lo[pl.ds(row, PACK_CHUNKS), :] = lo[0:PACK_CHUNKS] + lo[PACK_CHUNKS:SUBLANES]
        acc_hi[pl.ds(row, PACK_CHUNKS), :] = hi[0:PACK_CHUNKS] + hi[PACK_CHUNKS:SUBLANES]
        return carry

    lax.fori_loop(0, n_tok, reduce_token, 0)
    parts = []
    for s in range(PACK_CHUNKS):
        parts.append(acc_lo[pl.ds(s, n_tok, stride=PACK_CHUNKS), :])
        parts.append(acc_hi[pl.ds(s, n_tok, stride=PACK_CHUNKS), :])
    moe = jnp.concatenate(parts, axis=1)
    gate2 = mod_ref[:, 5 * D_MODEL:6 * D_MODEL]
    y = _layer_norm(base_ref[...] + gate2 * moe, g_ref[...], b_ref[...])

    @pl.when(i < n_ctx_tiles)
    def _():
        yc_ref[...] = y

    @pl.when(i >= n_ctx_tiles)
    def _():
        yl_ref[...] = y


def _combine(w_flat, yt, base, mod3, l2g, l2b, n_ctx, seq_tokens):
    n = base.shape[0]
    tc = COMB_TILE
    n_ctx_tiles = n_ctx // tc
    n_seq_tiles = seq_tokens // tc

    def mod_idx(i):
        return (jnp.where(i < n_ctx_tiles, 0, 1 + (i - n_ctx_tiles) // n_seq_tiles), 0, 0)

    full = lambda a: pl.BlockSpec(a.shape, lambda i: (0,) * a.ndim)
    return pl.pallas_call(
        functools.partial(_combine_kernel, n_ctx_tiles=n_ctx_tiles),
        grid=(n // tc,),
        in_specs=[pl.BlockSpec(memory_space=pl.ANY),
                  pl.BlockSpec((tc * TOP_K * PACK_CHUNKS, LANES), lambda i: (i, 0)),
                  pl.BlockSpec((tc, D_MODEL), lambda i: (i, 0)),
                  pl.BlockSpec((None, 1, mod3.shape[2]), mod_idx), full(l2g), full(l2b)],
        out_specs=(pl.BlockSpec((tc, D_MODEL), lambda i: (jnp.minimum(i, n_ctx_tiles - 1), 0)),
                   pl.BlockSpec((tc, D_MODEL), lambda i: (jnp.maximum(i - n_ctx_tiles, 0), 0))),
        out_shape=(jax.ShapeDtypeStruct((n_ctx, D_MODEL), F32),
                   jax.ShapeDtypeStruct((n - n_ctx, D_MODEL), F32)),
        scratch_shapes=[pltpu.SMEM((2 * tc * TOP_K,), F32),
                        pltpu.VMEM((tc * PACK_CHUNKS, LANES), F32),
                        pltpu.VMEM((tc * PACK_CHUNKS, LANES), F32),
                        pltpu.SemaphoreType.DMA],
        compiler_params=_cparams(1), name="combine",
    )(w_flat, yt, base, mod3, l2g, l2b)


def _rope_tables(n_tok, tile):
    rows = n_tok // GRID_W
    row_idx = jnp.repeat(jnp.arange(rows, dtype=F32), GRID_W)
    col_idx = jnp.tile(jnp.arange(GRID_W, dtype=F32), rows)
    inv_freq = 1.0 / (ROPE_THETA ** (jnp.arange(0, ROPE_AXIS_DIM, 2, dtype=F32) / ROPE_AXIS_DIM))
    ang_r = row_idx[:, None] * inv_freq[None, :]
    ang_c = col_idx[:, None] * inv_freq[None, :]
    ang = jnp.concatenate([ang_r, ang_r, ang_c, ang_c], axis=-1)
    cos, sin = jnp.cos(ang), jnp.sin(ang)
    quarter = (jnp.arange(HEAD_DIM) // (ROPE_AXIS_DIM // 2)) % 2
    sin_a = jnp.where(quarter == 0, -sin, 0.0)
    sin_b = jnp.where(quarter == 1, sin, 0.0)
    rep = LANES // HEAD_DIM
    ident = lambda v: jnp.full((tile, LANES), v, F32)
    cos_t = jnp.concatenate([jnp.tile(cos, (1, rep)), ident(1.0)], axis=0)
    sa_t = jnp.concatenate([jnp.tile(sin_a, (1, rep)), ident(0.0)], axis=0)
    sb_t = jnp.concatenate([jnp.tile(sin_b, (1, rep)), ident(0.0)], axis=0)
    ident_tr = lambda v: jnp.full((HEAD_DIM, tile), v, F32)
    cos_tr = jnp.concatenate([cos.T, ident_tr(1.0)], axis=1)
    sin_tr = jnp.concatenate([sin.T, ident_tr(0.0)], axis=1)
    return cos_t, sa_t, sb_t, cos_tr, sin_tr


def _dup_heads(a):
    parts = []
    for h in range(KV_HEADS):
        blk = a[..., h * HEAD_DIM:(h + 1) * HEAD_DIM]
        parts += [blk] * (LANES // HEAD_DIM)
    return jnp.concatenate(parts, axis=-1)


def kernel(x_prompt, x_sample, cache_k, cache_v, state_gla_fwd, state_gla_bwd, c, c_ctx, w_ada, b_ada, w_in, q_norm, k_norm, gla_wa_fwd, gla_ba_fwd, gla_wa_bwd, gla_ba_bwd, gla_norm, w_out, ln1_g, ln1_b, ln2_g, ln2_b, w_router, router_bias, exp_w1, exp_w3, exp_w2, sh_w1, sh_w3, sh_w2):
    n_ctx_b, ctx_seq, _ = x_prompt.shape
    n_lat_b, lat_seq, _ = x_sample.shape
    n_ctx = n_ctx_b * ctx_seq
    n_lat = n_lat_b * lat_seq
    n = n_ctx + n_lat
    l = 0

    x_c = x_prompt.reshape(n_ctx, D_MODEL)
    x_l = x_sample.reshape(n_lat, D_MODEL)

    c_rows = jnp.zeros((SUBLANES, D_MODEL), F32).at[0].set(c_ctx).at[1:1 + n_lat_b].set(c)
    mod = _modulation(c_rows, w_ada[l], b_ada[l][None, :])
    mod3 = mod.reshape(SUBLANES, 1, 6 * D_MODEL)

    wi = w_in[l]
    o_q, o_k, o_v, o_gq, o_gk, o_gv, o_gg, o_rf, o_rb, o_end = np.cumsum(
        [0, ATT_WIDTH, KV_HEADS * HEAD_DIM, KV_HEADS * HEAD_DIM, GLA_KW, GLA_KW, GLA_WIDTH, GLA_WIDTH,
         GLA_GATE_RANK, GLA_GATE_RANK])
    w_tok = jnp.concatenate([
        _dup_heads(wi[:, o_k:o_v]), wi[:, o_v:o_gq], wi[:, o_gq:o_gk],
        wi[:, o_gv:o_gg], wi[:, o_gg:o_rf], wi[:, o_rf:o_end],
        jnp.zeros((D_MODEL, LANES - 2 * GLA_GATE_RANK), F32)], axis=1).astype(BF16)
    w_tr = jnp.concatenate([wi[:, o_q:o_k], wi[:, o_v:o_gq], wi[:, o_gk:o_gv], wi[:, o_rf:o_end]],
                           axis=1).T.astype(BF16)
    rep = LANES // HEAD_DIM
    qn = q_norm[l][:, None]
    kn = jnp.tile(k_norm[l], rep)[None, :]
    seg = jnp.asarray(np.kron(np.eye(rep), np.ones((HEAD_DIM, HEAD_DIM))), BF16)
    wa = jnp.zeros((LANES, 2 * GLA_KW), F32)
    wa = wa.at[0:GLA_GATE_RANK, 0:GLA_KW].set(gla_wa_fwd[l])
    wa = wa.at[GLA_GATE_RANK:2 * GLA_GATE_RANK, GLA_KW:].set(gla_wa_bwd[l])
    ba = jnp.concatenate([gla_ba_fwd[l], gla_ba_bwd[l]])[None, :]
    wat = wa[0:2 * GLA_GATE_RANK, :].T
    bat = ba.T
    cos_t, sa_t, sb_t, cos_tr, sin_tr = _rope_tables(lat_seq, TOK_TILE)

    (qt, k_dup, vt, k32, v32, gq, gv, gg, la, gkt, lat) = _in_projection(
        x_c, x_l, mod3, w_tok, w_tr, qn, kn, cos_t, sa_t, sb_t, cos_tr, sin_tr, seg, wa, ba, wat, bat,
        lat_seq // TOK_TILE)

    ck = _dup_heads(cache_k[:, l].reshape(n_lat_b, -1, KV_HEADS * HEAD_DIM)).astype(BF16)
    cvt = cache_v[:, l].reshape(n_lat_b, -1, KV_HEADS * HEAD_DIM).transpose(0, 2, 1).astype(BF16)
    att_c = _attention(qt, k_dup, vt, None, 0, n_ctx_b, ctx_seq)
    att_l = _attention(qt, k_dup, vt, (ck, cvt), n_ctx, n_lat_b, lat_seq)

    gconst, levels_of = _gla_constants()
    to_dev = lambda t: (jnp.asarray(t[0], BF16), jnp.asarray(t[1], BF16), jnp.asarray(t[2], F32))
    bd = jnp.asarray(np.kron(np.eye(GLA_HEADS), np.ones((GLA_DK, GLA_DV))), BF16)
    vbd = jnp.asarray(np.kron(np.eye(GLA_HEADS), np.ones((GLA_CHUNK, GLA_DV))), BF16)
    consts = ((to_dev(gconst["f"]), to_dev(gconst["b"])), levels_of, bd, vbd)
    s_zero = jnp.zeros((n_ctx_b, GLA_HEADS, GLA_DK, GLA_DV), F32)
    of_c, ob_c, sf_new, sb_new = _gla(gq, la, gkt, lat, gv, s_zero, s_zero, consts, 0, n_ctx_b, ctx_seq)
    of_l, ob_l, _, _ = _gla(gq, la, gkt, lat, gv, state_gla_fwd[:, l], state_gla_bwd[:, l], consts,
                            n_ctx, n_lat_b, lat_seq)

    sw13 = jnp.concatenate([sh_w1[l], sh_w3[l]], axis=1).astype(BF16)
    base, u2_rows, logits_t = _out_projection(
        att_c, att_l, of_c, of_l, ob_c, ob_l, gg, x_c, x_l, mod3, w_out[l].astype(BF16),
        gla_norm[l][None, :], ln1_g[l][None, :], ln1_b[l][None, :], w_router[l].T.astype(BF16), sw13,
        sh_w2[l].astype(BF16), lat_seq // TOK_TILE)

    upper = jnp.asarray(np.triu(np.ones((TOK_TILE, TOK_TILE)), 1), BF16)
    idx_t, w_t, pos_t, counts = _route(logits_t, router_bias[l][:, None], upper)
    n_blocks = n * TOP_K // MOE_ROWS + N_EXPERTS
    n_blocks_pad = -(-n_blocks // LANES) * LANES
    lower = jnp.asarray(np.tril(np.ones((N_EXPERTS, N_EXPERTS)), -1), F32)
    dest_t, bexp, bval, nused = _destinations(counts, lower, idx_t, pos_t, n_blocks_pad)
    dest_flat = dest_t.T.reshape(-1)
    w_flat = w_t.T.reshape(-1)

    row_src = _invert_rows(dest_flat, n_blocks * MOE_ROWS)
    yt = _moe_experts(bexp.reshape(-1), bval.reshape(-1), nused.reshape(-1)[0:1], u2_rows, row_src,
                      exp_w1[l], exp_w3[l], exp_w2[l], n_blocks, n)
    y_c, y_l = _combine(w_flat, yt, base, mod3, ln2_g[l][None, :], ln2_b[l][None, :], n_ctx, lat_seq)

    y_prompt = y_c.reshape(n_ctx_b, ctx_seq, D_MODEL)
    y_sample = y_l.reshape(n_lat_b, lat_seq, D_MODEL)
    new_cache_k = k32.reshape(n_ctx_b, 1, ctx_seq, KV_HEADS, HEAD_DIM)
    new_cache_v = v32.reshape(n_ctx_b, 1, ctx_seq, KV_HEADS, HEAD_DIM)
    return (y_prompt, y_sample, new_cache_k, new_cache_v, sf_new[:, None], sb_new[:, None])
```

```python
import functools

import numpy as np
import jax
import jax.numpy as jnp
from jax import lax
from jax.experimental import pallas as pl
from jax.experimental.pallas import tpu as pltpu
from jax.experimental.pallas import tpu_sc as plsc

F32 = jnp.float32
BF16 = jnp.bfloat16
I32 = jnp.int32

D_MODEL = 1024
GRID_W = 64
HEAD_DIM = 64
N_HEADS = 8
KV_HEADS = 2
ATT_WIDTH = N_HEADS * HEAD_DIM
ATT_SCALE = HEAD_DIM ** -0.5
LOG2_E = 1.4426950408889634
ROPE_AXIS_DIM = HEAD_DIM // 2
ROPE_THETA = 10000.0
GLA_HEADS = 4
GLA_DK = 64
GLA_DV = 128
GLA_WIDTH = GLA_HEADS * GLA_DV
GLA_KW = GLA_HEADS * GLA_DK
GLA_GATE_RANK = 16
GLA_TAU = 16.0
N_EXPERTS = 256
TOP_K = 8
EXPERT_FF = 256
SHARED_FF = 256
ROUTED_SCALE = 2.5
DEPTH = 1
ALPHA = (2.0 * DEPTH) ** 0.25
EPS = 1e-6

LANES = 128
SUBLANES = 8
ROW_CHUNKS = D_MODEL // LANES
VMEM_LIMIT = 56 * 1024 * 1024

TOK_TILE = 512
ATT_TQ = 128
GLA_CHUNK = 128
GLA_LEVELS = ((32, 128), (8, 32), (2, 8), (1, 2))
MOE_ROWS = 256
MOE_VMEM_LIMIT = 62 * 1024 * 1024
COMB_TILE = 128
HIGHEST = lax.Precision.HIGHEST


def _cparams(n_axes):
    return pltpu.CompilerParams(dimension_semantics=("arbitrary",) * n_axes,
                                vmem_limit_bytes=VMEM_LIMIT)


def _silu(x):
    return x * jax.nn.sigmoid(x)


def _log_sigmoid(x):
    return jnp.minimum(x, 0.0) - jnp.log(1.0 + jnp.exp(-jnp.abs(x)))


def _load_row_tiles(ref, n_rows, row0=0):
    return jnp.concatenate(
        [ref[pl.ds(row0 * ROW_CHUNKS + cidx, n_rows, stride=ROW_CHUNKS), :] for cidx in range(ROW_CHUNKS)],
        axis=1)


def _store_row_tiles(ref, x):
    for cidx in range(ROW_CHUNKS):
        ref[pl.ds(cidx, x.shape[0], stride=ROW_CHUNKS), :] = x[:, cidx * LANES:(cidx + 1) * LANES]


def _layer_norm(z, g, b):
    mu = jnp.mean(z, axis=-1, keepdims=True)
    zc = z - mu
    var = jnp.mean(zc * zc, axis=-1, keepdims=True)
    return zc * lax.rsqrt(var + EPS) * g + b


def _mod_kernel(c_ref, w_ref, b_ref, o_ref):
    s = _silu(c_ref[...]).astype(BF16)
    o_ref[...] = jnp.dot(s, w_ref[...].astype(BF16), preferred_element_type=F32) + b_ref[...]


def _modulation(c_rows, w_ada, b_ada):
    n_cols = w_ada.shape[1]
    tn = 512
    return pl.pallas_call(
        _mod_kernel,
        grid=(n_cols // tn,),
        in_specs=[pl.BlockSpec((SUBLANES, D_MODEL), lambda j: (0, 0)),
                  pl.BlockSpec((D_MODEL, tn), lambda j: (0, j)),
                  pl.BlockSpec((1, tn), lambda j: (0, j))],
        out_specs=pl.BlockSpec((SUBLANES, tn), lambda j: (0, j)),
        out_shape=jax.ShapeDtypeStruct((SUBLANES, n_cols), F32),
        compiler_params=_cparams(1),
        name="modulation",
    )(c_rows, w_ada, b_ada)


_C_K = 0
_C_V = _C_K + 2 * LANES
_C_GQ = _C_V + KV_HEADS * HEAD_DIM
_C_GV = _C_GQ + GLA_KW
_C_GG = _C_GV + GLA_WIDTH
_C_RA = _C_GG + GLA_WIDTH
_C_END = _C_RA + LANES
_R_Q = 0
_R_V = _R_Q + ATT_WIDTH
_R_GK = _R_V + KV_HEADS * HEAD_DIM
_R_RA = _R_GK + GLA_KW
_R_END = _R_RA + 2 * GLA_GATE_RANK


def _inproj_kernel(xc_ref, xl_ref, mod_ref, w_ref, wt_ref, qn_ref, kn_ref, cos_ref, sa_ref, sb_ref,
                   cost_ref, sint_ref, seg_ref, wa_ref, ba_ref, wat_ref, bat_ref,
                   qt_ref, k_ref, vt_ref, k32_ref, v32_ref, gq_ref, gv_ref, gg_ref,
                   la_ref, gkt_ref, lat_ref, *, n_ctx_tiles):
    i = pl.program_id(0)
    m = mod_ref[...]
    shift1 = m[:, 0:D_MODEL]
    scale1 = m[:, D_MODEL:2 * D_MODEL]
    x = jnp.where(i < n_ctx_tiles, xc_ref[...], xl_ref[...])
    u = (x * (1.0 + scale1) + shift1).astype(BF16)

    cos = cos_ref[...]
    sin_a = sa_ref[...]
    sin_b = sb_ref[...]
    seg = seg_ref[...]
    lane = lax.broadcasted_iota(I32, (u.shape[0], LANES), 1)
    low = lane < HEAD_DIM

    def proj(c0, c1):
        return jnp.dot(u, w_ref[:, c0:c1], preferred_element_type=F32)

    def head_norm(blk, gain):
        ss = jnp.dot((blk * blk).astype(BF16), seg, preferred_element_type=F32) * (1.0 / HEAD_DIM)
        return blk * lax.rsqrt(ss + EPS) * gain

    def rope(blk):
        return (blk * cos + pltpu.roll(blk, LANES - ROPE_AXIS_DIM // 2, 1) * sin_a
                + pltpu.roll(blk, ROPE_AXIS_DIM // 2, 1) * sin_b)

    pk = proj(_C_K, _C_V)
    kn = [head_norm(pk[:, j * LANES:(j + 1) * LANES], kn_ref[...]) for j in range(KV_HEADS)]
    for j in range(KV_HEADS):
        k_ref[:, j * LANES:(j + 1) * LANES] = rope(kn[j]).astype(BF16)

    @pl.when(i < n_ctx_tiles)
    def _():
        k32_ref[...] = jnp.where(low, kn[0], kn[1])
        v32_ref[...] = proj(_C_V, _C_GQ)

    gq_ref[...] = proj(_C_GQ, _C_GV) * (GLA_DK ** -0.5)
    gv_ref[...] = proj(_C_GV, _C_GG).astype(BF16)
    gg_ref[...] = proj(_C_GG, _C_RA).astype(BF16)

    ra = proj(_C_RA, _C_END)
    pre = jnp.dot(ra, wa_ref[...], precision=HIGHEST, preferred_element_type=F32) + ba_ref[...]
    la_ref[...] = _log_sigmoid(pre) * (1.0 / GLA_TAU)

    pt = lax.dot_general(wt_ref[...], u, (((1,), (1,)), ((), ())), preferred_element_type=F32)
    cos_t = cost_ref[...]
    sin_t = sint_ref[...]
    quarter = ROPE_AXIS_DIM // 2
    for h in range(N_HEADS):
        blk = pt[_R_Q + h * HEAD_DIM:_R_Q + (h + 1) * HEAD_DIM, :]
        ms = jnp.mean(blk * blk, axis=0, keepdims=True)
        qn = blk * lax.rsqrt(ms + EPS) * qn_ref[...]
        rot = jnp.concatenate([-qn[quarter:2 * quarter], qn[0:quarter],
                               -qn[3 * quarter:4 * quarter], qn[2 * quarter:3 * quarter]], axis=0)
        qt_ref[h * HEAD_DIM:(h + 1) * HEAD_DIM, :] = (
            (qn * cos_t + rot * sin_t) * (ATT_SCALE * LOG2_E)).astype(BF16)
    vt_ref[...] = pt[_R_V:_R_GK, :].astype(BF16)
    gkt_ref[...] = pt[_R_GK:_R_RA, :]
    rat = pt[_R_RA:_R_END, :]
    pre_t = jnp.dot(wat_ref[...], rat, precision=HIGHEST, preferred_element_type=F32) + bat_ref[...]
    lat_ref[...] = _log_sigmoid(pre_t) * (1.0 / GLA_TAU)


def _in_projection(x_c, x_l, mod3, w_tok, w_tr, qn, kn, cos_t, sa_t, sb_t, cos_tr, sin_tr, seg, wa, ba,
                   wat, bat, n_seq_tiles):
    n_ctx = x_c.shape[0]
    n = n_ctx + x_l.shape[0]
    tb = TOK_TILE
    n_ctx_tiles = n_ctx // tb
    n_tiles = n // tb
    n_rope_blocks = cos_t.shape[0] // tb - 1

    def mod_idx(i):
        return (jnp.where(i < n_ctx_tiles, 0, 1 + (i - n_ctx_tiles) // n_seq_tiles), 0, 0)

    def rope_blk(i):
        return jnp.where(i < n_ctx_tiles, n_rope_blocks, (i - n_ctx_tiles) % n_seq_tiles)

    def rope_idx(i):
        return (rope_blk(i), 0)

    def ctx_idx(i):
        return (jnp.minimum(i, n_ctx_tiles - 1), 0)

    tok = lambda w: pl.BlockSpec((tb, w), lambda i: (i, 0))
    full = lambda a: pl.BlockSpec(a.shape, lambda i: (0,) * a.ndim)
    tr = lambda r: pl.BlockSpec((r, tb), lambda i: (0, i))
    rope_tr = pl.BlockSpec((HEAD_DIM, tb), lambda i: (0, rope_blk(i)))
    out_shapes = (
        jax.ShapeDtypeStruct((ATT_WIDTH, n), BF16),
        jax.ShapeDtypeStruct((n, 2 * LANES), BF16),
        jax.ShapeDtypeStruct((KV_HEADS * HEAD_DIM, n), BF16),
        jax.ShapeDtypeStruct((n_ctx, LANES), F32),
        jax.ShapeDtypeStruct((n_ctx, LANES), F32),
        jax.ShapeDtypeStruct((n, GLA_KW), F32),
        jax.ShapeDtypeStruct((n, GLA_WIDTH), BF16),
        jax.ShapeDtypeStruct((n, GLA_WIDTH), BF16),
        jax.ShapeDtypeStruct((n, 2 * GLA_KW), F32),
        jax.ShapeDtypeStruct((GLA_KW, n), F32),
        jax.ShapeDtypeStruct((2 * GLA_KW, n), F32),
    )
    out_specs = (tr(ATT_WIDTH), tok(2 * LANES), tr(KV_HEADS * HEAD_DIM),
                 pl.BlockSpec((tb, LANES), ctx_idx), pl.BlockSpec((tb, LANES), ctx_idx),
                 tok(GLA_KW), tok(GLA_WIDTH), tok(GLA_WIDTH), tok(2 * GLA_KW),
                 tr(GLA_KW), tr(2 * GLA_KW))
    in_specs = [pl.BlockSpec((tb, D_MODEL), ctx_idx),
                pl.BlockSpec((tb, D_MODEL), lambda i: (jnp.maximum(i - n_ctx_tiles, 0), 0)),
                pl.BlockSpec((None, 1, mod3.shape[2]), mod_idx),
                full(w_tok), full(w_tr), full(qn), full(kn),
                pl.BlockSpec((tb, LANES), rope_idx), pl.BlockSpec((tb, LANES), rope_idx),
                pl.BlockSpec((tb, LANES), rope_idx), rope_tr, rope_tr,
                full(seg), full(wa), full(ba), full(wat), full(bat)]
    return pl.pallas_call(
        functools.partial(_inproj_kernel, n_ctx_tiles=n_ctx_tiles),
        grid=(n_tiles,), in_specs=in_specs, out_specs=out_specs, out_shape=out_shapes,
        compiler_params=_cparams(1), name="in_projection",
    )(x_c, x_l, mod3, w_tok, w_tr, qn, kn, cos_t, sa_t, sb_t, cos_tr, sin_tr, seg, wa, ba, wat, bat)


def _attention_kernel(*refs, n_kv_parts):
    qt_ref = refs[0]
    k_refs = refs[1:1 + n_kv_parts]
    vt_refs = refs[1 + n_kv_parts:1 + 2 * n_kv_parts]
    o_ref = refs[1 + 2 * n_kv_parts]
    tq = qt_ref.shape[1]
    group = N_HEADS // KV_HEADS
    scores = []
    for kv in range(KV_HEADS):
        heads = range(kv * group, (kv + 1) * group)
        q_grp = jnp.concatenate([qt_ref[h * HEAD_DIM:(h + 1) * HEAD_DIM, :] for h in heads], axis=1)
        rhs = jnp.concatenate([q_grp, jnp.zeros_like(q_grp)], axis=0)
        scores.append([jnp.dot(k[:, kv * LANES:(kv + 1) * LANES], rhs, preferred_element_type=F32)
                       for k in k_refs])
    for kv in range(KV_HEADS):
        heads = range(kv * group, (kv + 1) * group)
        s = scores[kv]
        mx = functools.reduce(jnp.maximum, [jnp.max(x, axis=0, keepdims=True) for x in s])
        pr = [jnp.exp2(x - mx) for x in s]
        den = functools.reduce(jnp.add, [jnp.sum(x, axis=0, keepdims=True) for x in pr])
        acc = functools.reduce(jnp.add, [
            jnp.dot(vt[kv * HEAD_DIM:(kv + 1) * HEAD_DIM, :], x.astype(BF16),
                    preferred_element_type=F32) for x, vt in zip(pr, vt_refs)])
        out = (acc / den).astype(BF16)
        for j, h in enumerate(heads):
            o_ref[h * HEAD_DIM:(h + 1) * HEAD_DIM, :] = out[:, j * tq:(j + 1) * tq]


def _attention(qt, k, vt, extra_kv, row0, n_batch, seq):
    tq = ATT_TQ
    n_q = seq // tq
    q_blk0 = row0 // tq
    kv_blk0 = row0 // seq
    in_specs = [pl.BlockSpec((ATT_WIDTH, tq), lambda b, i: (0, q_blk0 + b * n_q + i))]
    k_spec = pl.BlockSpec((seq, 2 * LANES), lambda b, i: (kv_blk0 + b, 0))
    vt_spec = pl.BlockSpec((KV_HEADS * HEAD_DIM, seq), lambda b, i: (0, kv_blk0 + b))
    args_k, args_v, specs_k, specs_v = [k], [vt], [k_spec], [vt_spec]
    if extra_kv is not None:
        ck, cvt = extra_kv
        args_k.append(ck)
        args_v.append(cvt)
        specs_k.append(pl.BlockSpec((None, ck.shape[1], 2 * LANES), lambda b, i: (b, 0, 0)))
        specs_v.append(pl.BlockSpec((None, KV_HEADS * HEAD_DIM, cvt.shape[2]), lambda b, i: (b, 0, 0)))
    return pl.pallas_call(
        functools.partial(_attention_kernel, n_kv_parts=len(args_k)),
        grid=(n_batch, n_q),
        in_specs=in_specs + specs_k + specs_v,
        out_specs=pl.BlockSpec((ATT_WIDTH, tq), lambda b, i: (0, b * n_q + i)),
        out_shape=jax.ShapeDtypeStruct((ATT_WIDTH, n_batch * seq), BF16),
        compiler_params=_cparams(2), name="attention",
    )(qt, *args_k, *args_v)


def _gla_constants():
    c = GLA_CHUNK
    idx = np.arange(c)
    q_mats, k_mats, masks, levels_of = [], [], [], []
    for li, (s, p) in enumerate(GLA_LEVELS):
        start = (idx // s) * s
        end = start + s - 1
        k_mats.append(((idx[None, :] > idx[:, None]) & (idx[None, :] <= end[:, None])))
        for d in range(p // s - 1):
            lo = np.maximum(start - d * s, 0)
            q_mats.append((idx[None, :] >= lo[:, None]) & (idx[None, :] <= idx[:, None]))
            masks.append((idx[:, None] // p == idx[None, :] // p)
                         & (idx[:, None] // s - idx[None, :] // s - 1 == d))
            levels_of.append(li)
    masks.append(np.eye(c, dtype=bool))
    levels_of.append(len(GLA_LEVELS) - 1)
    q_mats.append(idx[None, :] <= idx[:, None])
    k_mats = k_mats[:-1]
    k_mats.append(idx[None, :] > idx[:, None])
    k_mats.append(np.ones((c, c), bool))
    out = {}
    for name, flip in (("f", False), ("b", True)):
        f = (lambda a: a[::-1, ::-1]) if flip else (lambda a: a)
        lq = np.concatenate([f(a) for a in q_mats], axis=0).astype(np.float32)
        lkt = np.concatenate([f(a).T for a in k_mats], axis=1).astype(np.float32)
        mk = np.stack([np.tile(f(a), (1, GLA_HEADS)) for a in masks]).astype(np.float32)
        out[name] = (np.concatenate([lq, lq], axis=1), np.concatenate([lkt, lkt], axis=0), mk)
    return out, tuple(levels_of)


def _gla_direction(q, g, gkt, gt, v, lq2, lkt2, masks_ref, bd, vbd, s_ref, levels_of):
    c = GLA_CHUNK
    n_var = len(levels_of)
    n_lev = len(GLA_LEVELS)
    g_hi = g.astype(BF16)
    g_lo = (g - g_hi.astype(F32)).astype(BF16)
    fq = jnp.dot(lq2, jnp.concatenate([g_hi, g_lo], axis=0), preferred_element_type=F32)
    gt_hi = gt.astype(BF16)
    gt_lo = (gt - gt_hi.astype(F32)).astype(BF16)
    fk = jnp.dot(jnp.concatenate([gt_hi, gt_lo], axis=1), lkt2, preferred_element_type=F32)

    def key_factor(f):
        return gkt * jnp.exp(fk[:, f * c:(f + 1) * c])

    q_var = [(q * jnp.exp(fq[vi * c:(vi + 1) * c, :])).astype(BF16) for vi in range(n_var - 1)]
    q_var.append(q.astype(BF16))
    a = jnp.zeros((c, GLA_HEADS * c), F32)
    for li in range(n_lev):
        kt = (key_factor(li) if li < n_lev - 1 else gkt).astype(BF16)
        xt = jnp.concatenate([kt] * GLA_HEADS, axis=1) * bd
        vis = [vi for vi in range(n_var) if levels_of[vi] == li]
        res = jnp.dot(jnp.concatenate([q_var[vi] for vi in vis], axis=0), xt,
                      preferred_element_type=F32)
        for r, vi in enumerate(vis):
            a = a + masks_ref[vi] * res[r * c:(r + 1) * c, :]
    q_in = (q * jnp.exp(fq[(n_var - 1) * c:n_var * c, :])).astype(BF16)
    state = s_ref[...]
    v_bd = jnp.concatenate([v] * GLA_HEADS, axis=0) * vbd
    o = (jnp.dot(q_in, state.astype(BF16), preferred_element_type=F32)
         + jnp.dot(a.astype(BF16), v_bd, preferred_element_type=F32))
    k_out = key_factor(n_lev - 1).astype(BF16)
    e_tot = jnp.exp(fk[:, n_lev * c:(n_lev + 1) * c])
    upd = jnp.dot(k_out, v, preferred_element_type=F32)
    s_ref[...] = (state * jnp.concatenate([e_tot] * (GLA_WIDTH // c), axis=1)
                  + upd * bd.astype(F32))
    return o


def _gla_kernel(gq_f, la_f, gkt_f, lat_f, gv_f, gq_b, la_b, gkt_b, lat_b, gv_b,
                s0f_ref, s0b_ref, lq2f, lkt2f, mkf, lq2b, lkt2b, mkb, bd_ref, vbd_ref,
                of_ref, ob_ref, sf_ref, sb_ref, st_f, st_b, *, levels_of):
    n = pl.program_id(1)

    @pl.when(n == 0)
    def _():
        st_f[...] = jnp.zeros_like(st_f)
        st_b[...] = jnp.zeros_like(st_b)
        for h in range(GLA_HEADS):
            rows = slice(h * GLA_DK, (h + 1) * GLA_DK)
            cols = slice(h * GLA_DV, (h + 1) * GLA_DV)
            st_f[rows, cols] = s0f_ref[h]
            st_b[rows, cols] = s0b_ref[h]

    bd = bd_ref[...]
    vbd = vbd_ref[...]
    of_ref[...] = _gla_direction(gq_f[...], la_f[...], gkt_f[...], lat_f[...], gv_f[...],
                                 lq2f[...], lkt2f[...], mkf, bd, vbd, st_f, levels_of)
    ob_ref[...] = _gla_direction(gq_b[...], la_b[...], gkt_b[...], lat_b[...], gv_b[...],
                                 lq2b[...], lkt2b[...], mkb, bd, vbd, st_b, levels_of)

    @pl.when(n == pl.num_programs(1) - 1)
    def _():
        for h in range(GLA_HEADS):
            rows = slice(h * GLA_DK, (h + 1) * GLA_DK)
            cols = slice(h * GLA_DV, (h + 1) * GLA_DV)
            sf_ref[h] = st_f[rows, cols]
            sb_ref[h] = st_b[rows, cols]


def _gla(gq, la, gkt, lat, gv, s0f, s0b, consts, row0, n_batch, seq):
    (cf, cb), levels_of, bd, vbd = consts
    c = GLA_CHUNK
    nc = seq // c
    blk0 = row0 // c
    n_la_blocks_b = 1
    fwd = lambda b, n: blk0 + b * nc + n
    bwd = lambda b, n: blk0 + b * nc + (nc - 1 - n)

    def tok(w, which, col=0):
        return pl.BlockSpec((c, w), lambda b, n: (which(b, n), col))

    def tr(r, which, row=0):
        return pl.BlockSpec((r, c), lambda b, n: (row, which(b, n)))

    full = lambda a: pl.BlockSpec(a.shape, lambda b, n: (0,) * a.ndim)
    st_spec = pl.BlockSpec((None, GLA_HEADS, GLA_DK, GLA_DV), lambda b, n: (b, 0, 0, 0))
    in_specs = [tok(GLA_KW, fwd), tok(GLA_KW, fwd, 0), tr(GLA_KW, fwd), tr(GLA_KW, fwd, 0),
                tok(GLA_WIDTH, fwd),
                tok(GLA_KW, bwd), tok(GLA_KW, bwd, n_la_blocks_b), tr(GLA_KW, bwd),
                tr(GLA_KW, bwd, 1), tok(GLA_WIDTH, bwd),
                st_spec, st_spec,
                full(cf[0]), full(cf[1]), full(cf[2]), full(cb[0]), full(cb[1]), full(cb[2]),
                full(bd), full(vbd)]
    out_specs = (pl.BlockSpec((c, GLA_WIDTH), lambda b, n: (b * nc + n, 0)),
                 pl.BlockSpec((c, GLA_WIDTH), lambda b, n: (b * nc + (nc - 1 - n), 0)),
                 st_spec, st_spec)
    out_shape = (jax.ShapeDtypeStruct((n_batch * seq, GLA_WIDTH), F32),
                 jax.ShapeDtypeStruct((n_batch * seq, GLA_WIDTH), F32),
                 jax.ShapeDtypeStruct((n_batch, GLA_HEADS, GLA_DK, GLA_DV), F32),
                 jax.ShapeDtypeStruct((n_batch, GLA_HEADS, GLA_DK, GLA_DV), F32))
    return pl.pallas_call(
        functools.partial(_gla_kernel, levels_of=levels_of),
        grid=(n_batch, nc), in_specs=in_specs, out_specs=out_specs, out_shape=out_shape,
        scratch_shapes=[pltpu.VMEM((GLA_KW, GLA_WIDTH), F32), pltpu.VMEM((GLA_KW, GLA_WIDTH), F32)],
        compiler_params=_cparams(2), name="gla",
    )(gq, la, gkt, lat, gv, gq, la, gkt, lat, gv, s0f, s0b,
      cf[0], cf[1], cf[2], cb[0], cb[1], cb[2], bd, vbd)


def _outproj_kernel(attc_ref, attl_ref, ofc_ref, ofl_ref, obc_ref, obl_ref, gg_ref, xc_ref, xl_ref,
                    mod_ref, wo_ref, gn_ref, l1g_ref, l1b_ref, wrt_ref, sw13_ref, sw2_ref,
                    base_ref, u2_ref, lg_ref, *, n_ctx_tiles):
    is_ctx = pl.program_id(0) < n_ctx_tiles
    pick = lambda a_ref, b_ref: jnp.where(is_ctx, a_ref[...], b_ref[...])
    m = mod_ref[...]
    gate1 = m[:, 2 * D_MODEL:3 * D_MODEL]
    shift2 = m[:, 3 * D_MODEL:4 * D_MODEL]
    scale2 = m[:, 4 * D_MODEL:5 * D_MODEL]
    gate2 = m[:, 5 * D_MODEL:6 * D_MODEL]
    og = pick(ofc_ref, ofl_ref) + pick(obc_ref, obl_ref)
    gg = gg_ref[...].astype(F32)
    parts = []
    for h in range(GLA_HEADS):
        blk = og[:, h * GLA_DV:(h + 1) * GLA_DV]
        ms = jnp.mean(blk * blk, axis=-1, keepdims=True)
        nb = blk * lax.rsqrt(ms + EPS) * gn_ref[...]
        parts.append((nb * _silu(gg[:, h * GLA_DV:(h + 1) * GLA_DV])).astype(BF16))
    att_t = pick(attc_ref, attl_ref)
    hmix = (lax.dot_general(att_t, wo_ref[0:ATT_WIDTH, :], (((0,), (0,)), ((), ())),
                            preferred_element_type=F32)
            + jnp.dot(jnp.concatenate(parts, axis=1), wo_ref[ATT_WIDTH:, :],
                      preferred_element_type=F32))
    x1 = _layer_norm(ALPHA * pick(xc_ref, xl_ref) + gate1 * hmix, l1g_ref[...], l1b_ref[...])
    u2 = x1 * (1.0 + scale2) + shift2
    u2b = u2.astype(BF16)
    lg_ref[...] = lax.dot_general(wrt_ref[...], u2b, (((1,), (1,)), ((), ())),
                                  preferred_element_type=F32)
    ab = jnp.dot(u2b, sw13_ref[...], preferred_element_type=F32)
    hid = (_silu(ab[:, 0:SHARED_FF]) * ab[:, SHARED_FF:2 * SHARED_FF]).astype(BF16)
    shared = jnp.dot(hid, sw2_ref[...], preferred_element_type=F32)
    base_ref[...] = ALPHA * x1 + gate2 * shared
    _pack_rows(u2_ref, u2)


def _out_projection(att_c, att_l, of_c, of_l, ob_c, ob_l, gg, x_c, x_l, mod3, wo, gn, l1g, l1b, wrt,
                    sw13, sw2, n_seq_tiles):
    n_ctx = x_c.shape[0]
    n = n_ctx + x_l.shape[0]
    tb = TOK_TILE
    n_ctx_tiles = n_ctx // tb

    def mod_idx(i):
        return (jnp.where(i < n_ctx_tiles, 0, 1 + (i - n_ctx_tiles) // n_seq_tiles), 0, 0)

    ctx_blk = lambda i: jnp.minimum(i, n_ctx_tiles - 1)
    lat_blk = lambda i: jnp.maximum(i - n_ctx_tiles, 0)
    tok = lambda w: pl.BlockSpec((tb, w), lambda i: (i, 0))
    tok_c = lambda w: pl.BlockSpec((tb, w), lambda i: (ctx_blk(i), 0))
    tok_l = lambda w: pl.BlockSpec((tb, w), lambda i: (lat_blk(i), 0))
    full = lambda a: pl.BlockSpec(a.shape, lambda i: (0,) * a.ndim)
    return pl.pallas_call(
        functools.partial(_outproj_kernel, n_ctx_tiles=n_ctx_tiles),
        grid=(n // tb,),
        in_specs=[pl.BlockSpec((ATT_WIDTH, tb), lambda i: (0, ctx_blk(i))),
                  pl.BlockSpec((ATT_WIDTH, tb), lambda i: (0, lat_blk(i))),
                  tok_c(GLA_WIDTH), tok_l(GLA_WIDTH), tok_c(GLA_WIDTH), tok_l(GLA_WIDTH),
                  tok(GLA_WIDTH), tok_c(D_MODEL), tok_l(D_MODEL),
                  pl.BlockSpec((None, 1, mod3.shape[2]), mod_idx),
                  full(wo), full(gn), full(l1g), full(l1b), full(wrt), full(sw13), full(sw2)],
        out_specs=(tok(D_MODEL),
                   pl.BlockSpec((tb * PACK_CHUNKS, LANES), lambda i: (i, 0)),
                   pl.BlockSpec((N_EXPERTS, tb), lambda i: (0, i))),
        out_shape=(jax.ShapeDtypeStruct((n, D_MODEL), F32),
                   jax.ShapeDtypeStruct((n * PACK_CHUNKS, LANES), U32),
                   jax.ShapeDtypeStruct((N_EXPERTS, n), F32)),
        compiler_params=_cparams(1), name="out_projection",
    )(att_c, att_l, of_c, of_l, ob_c, ob_l, gg, x_c, x_l, mod3, wo, gn, l1g, l1b, wrt, sw13, sw2)


def _route_kernel(lg_ref, bias_ref, upper_ref, idx_ref, w_ref, pos_ref, cnt_ref, run_ref):
    i = pl.program_id(0)

    @pl.when(i == 0)
    def _():
        run_ref[...] = jnp.zeros_like(run_ref)

    s = jax.nn.sigmoid(lg_ref[...])
    work = s + bias_ref[...]
    rows = lax.broadcasted_iota(I32, s.shape, 0).astype(F32)
    sel = jnp.zeros(s.shape, F32)
    idxs, vals = [], []
    for _ in range(TOP_K):
        mx = jnp.max(work, axis=0, keepdims=True)
        idx = jnp.min(jnp.where(work == mx, rows, float(N_EXPERTS)), axis=0, keepdims=True)
        hit = rows == idx
        vals.append(jnp.sum(jnp.where(hit, s, 0.0), axis=0, keepdims=True))
        idxs.append(idx)
        sel = jnp.where(hit, 1.0, sel)
        work = jnp.where(hit, -jnp.inf, work)
    den = functools.reduce(jnp.add, vals)
    rank = jnp.dot(sel.astype(BF16), upper_ref[...], preferred_element_type=F32) + run_ref[:, 0:1]
    for k in range(TOP_K):
        idx_ref[k:k + 1, :] = idxs[k].astype(I32)
        w_ref[k:k + 1, :] = vals[k] / den * ROUTED_SCALE
        pos_ref[k:k + 1, :] = jnp.sum(jnp.where(rows == idxs[k], rank, 0.0), axis=0,
                                      keepdims=True).astype(I32)
    run_ref[...] = run_ref[...] + jnp.sum(sel, axis=1, keepdims=True)
    cnt_ref[...] = run_ref[...]


def _route(logits_t, bias_col, upper):
    n = logits_t.shape[1]
    tt = TOK_TILE
    row = lambda dt: jax.ShapeDtypeStruct((TOP_K, n), dt)
    blk = pl.BlockSpec((TOP_K, tt), lambda i: (0, i))
    return pl.pallas_call(
        _route_kernel,
        grid=(n // tt,),
        in_specs=[pl.BlockSpec((N_EXPERTS, tt), lambda i: (0, i)),
                  pl.BlockSpec((N_EXPERTS, 1), lambda i: (0, 0)),
                  pl.BlockSpec((tt, tt), lambda i: (0, 0))],
        out_specs=(blk, blk, blk, pl.BlockSpec((N_EXPERTS, LANES), lambda i: (0, 0))),
        out_shape=(row(I32), row(F32), row(I32), jax.ShapeDtypeStruct((N_EXPERTS, LANES), F32)),
        scratch_shapes=[pltpu.VMEM((N_EXPERTS, LANES), F32)],
        compiler_params=_cparams(1), name="route",
    )(logits_t, bias_col, upper)


def _dest_kernel(cnt_ref, lower_ref, idx_ref, pos_ref, dest_ref, bexp_ref, bval_ref, nused_ref):
    cnt = cnt_ref[...]
    nblk = jnp.floor((cnt + (MOE_ROWS - 1)) * (1.0 / MOE_ROWS))
    bstart = jnp.dot(lower_ref[...], nblk, precision=HIGHEST, preferred_element_type=F32)
    bend = bstart + nblk
    pstart = bstart[:, 0:1] * MOE_ROWS
    rows = lax.broadcasted_iota(I32, (N_EXPERTS, idx_ref.shape[1]), 0)
    for k in range(TOP_K):
        hit = rows == idx_ref[k:k + 1, :]
        dest_ref[k:k + 1, :] = (jnp.sum(jnp.where(hit, pstart, 0.0), axis=0, keepdims=True)
                                .astype(I32) + pos_ref[k:k + 1, :])

    @pl.when(pl.program_id(0) == 0)
    def _():
        nb = bexp_ref.shape[1]
        bid = lax.broadcasted_iota(I32, (N_EXPERTS, nb), 1).astype(F32)
        inside = jnp.logical_and(bid >= bstart[:, 0:1], bid < bend[:, 0:1])
        erow = lax.broadcasted_iota(I32, (N_EXPERTS, nb), 0).astype(F32)
        bexp_ref[...] = jnp.sum(jnp.where(inside, erow, 0.0), axis=0, keepdims=True).astype(I32)
        valid = jnp.clip(cnt[:, 0:1] - (bid - bstart[:, 0:1]) * MOE_ROWS, 0.0, float(MOE_ROWS))
        bval_ref[...] = jnp.sum(jnp.where(inside, valid, 0.0), axis=0, keepdims=True).astype(I32)
        nused_ref[...] = jnp.max(bend, axis=0, keepdims=True).astype(I32)


def _destinations(counts, lower, idx_t, pos_t, n_blocks_pad):
    n = idx_t.shape[1]
    tt = TOK_TILE
    blk = pl.BlockSpec((TOP_K, tt), lambda i: (0, i))
    one = lambda w: pl.BlockSpec((1, w), lambda i: (0, 0))
    return pl.pallas_call(
        _dest_kernel,
        grid=(n // tt,),
        in_specs=[pl.BlockSpec((N_EXPERTS, LANES), lambda i: (0, 0)),
                  pl.BlockSpec((N_EXPERTS, N_EXPERTS), lambda i: (0, 0)), blk, blk],
        out_specs=(blk, one(n_blocks_pad), one(n_blocks_pad), one(LANES)),
        out_shape=(jax.ShapeDtypeStruct((TOP_K, n), I32),
                   jax.ShapeDtypeStruct((1, n_blocks_pad), I32),
                   jax.ShapeDtypeStruct((1, n_blocks_pad), I32),
                   jax.ShapeDtypeStruct((1, LANES), I32)),
        compiler_params=_cparams(1), name="destinations",
    )(counts, lower, idx_t, pos_t)


SC_WINDOW = 128
SC_WINDOWS_PER_STEP = 8


def _invert_rows(dest_flat, n_rows):
    m = dest_flat.shape[0]
    mesh = plsc.VectorSubcoreMesh(core_axis_name="core", subcore_axis_name="subcore")

    @functools.partial(pl.kernel, out_type=jax.ShapeDtypeStruct((n_rows,), I32), mesh=mesh,
                       scratch_types=[])
    def invert(val_hbm, idx_hbm, out_hbm):
        def body(val_vmem, idx_vmem):
            for j in range(SC_WINDOWS_PER_STEP):
                pltpu.sync_copy(val_vmem.at[j], out_hbm.at[idx_vmem.at[j]])

        blk = pl.BlockSpec((SC_WINDOWS_PER_STEP, SC_WINDOW), lambda i: (i, 0))
        pltpu.emit_pipeline(
            body, grid=(m // (SC_WINDOW * SC_WINDOWS_PER_STEP),),
            in_specs=[blk, blk], out_specs=[], core_axis_name=("core", "subcore"),
            dimension_semantics=(pltpu.PARALLEL,),
        )(val_hbm, idx_hbm)

    shape = (m // SC_WINDOW, SC_WINDOW)
    return invert(jnp.arange(m, dtype=I32).reshape(shape), dest_flat.reshape(shape))


PACK_CHUNKS = D_MODEL // (2 * LANES)
SRC_GROUP = 4
TOP_K_LOG2 = TOP_K.bit_length() - 1
PACK_CHUNKS_LOG2 = PACK_CHUNKS.bit_length() - 1
U32 = jnp.uint32


def _pack_rows(ref, x, row0=0):
    bits = pltpu.bitcast(x.astype(BF16).astype(F32), U32)
    for s in range(PACK_CHUNKS):
        lo = bits[:, (2 * s) * LANES:(2 * s + 1) * LANES] >> 16
        hi = bits[:, (2 * s + 1) * LANES:(2 * s + 2) * LANES] & jnp.uint32(0xFFFF0000)
        ref[pl.ds(row0 + s, x.shape[0], stride=PACK_CHUNKS), :] = lo | hi


def _unpack_rows(ref, n_rows, row0=0):
    parts = []
    for s in range(PACK_CHUNKS):
        w = ref[pl.ds(row0 + s, n_rows, stride=PACK_CHUNKS), :]
        parts.append(pltpu.bitcast(w << 16, F32))
        parts.append(pltpu.bitcast(w & jnp.uint32(0xFFFF0000), F32))
    return jnp.concatenate(parts, axis=1).astype(BF16)


def _moe_kernel(bexp_ref, bval_ref, nused_ref, u2p_hbm, src_hbm, w1_ref, w3_ref, w2_ref, yt_hbm,
                u2p_vmem, w13_s, w2_s, xbuf, ybuf, src_smem, sem_in, sem_src, sem_out,
                *, n_tokens):
    b = pl.program_id(0)
    n_used = nused_ref[0]
    br = MOE_ROWS
    grp = SRC_GROUP * br
    trash0 = n_tokens * TOP_K

    def src_copy(g):
        return pltpu.make_async_copy(src_hbm.at[pl.ds(g * grp, grp)],
                                     src_smem.at[pl.ds(lax.rem(g, 2) * grp, grp)], sem_src)

    def out_wait(slot):
        pltpu.make_async_copy(ybuf.at[pl.ds(slot * br * PACK_CHUNKS, br * PACK_CHUNKS)],
                              yt_hbm.at[pl.ds(0, br * PACK_CHUNKS)], sem_out.at[slot]).wait()

    def src_base(blk):
        return lax.rem(blk // SRC_GROUP, 2) * grp + lax.rem(blk, SRC_GROUP) * br

    def scatter_row(blk_slot, sbase, valid, r):
        dst = jnp.where(r < valid, src_smem[sbase + r], trash0 + blk_slot * br + r)
        pltpu.make_async_copy(
            ybuf.at[pl.ds(pl.multiple_of((blk_slot * br + r) * PACK_CHUNKS, PACK_CHUNKS), PACK_CHUNKS)],
            yt_hbm.at[pl.ds(pl.multiple_of(dst * PACK_CHUNKS, PACK_CHUNKS), PACK_CHUNKS)],
            sem_out.at[blk_slot]).start(priority=r % 2 if isinstance(r, int) else 0)

    def gather_row(xslot, sbase, r):
        row = lax.shift_right_logical(src_smem[sbase + r], TOP_K_LOG2 - PACK_CHUNKS_LOG2)
        row = jnp.minimum(row & (-PACK_CHUNKS & 0x7FFFFFFF), (n_tokens - 1) * PACK_CHUNKS)
        dst = pl.multiple_of((xslot * br + r) * PACK_CHUNKS, PACK_CHUNKS)
        xbuf[pl.ds(dst, PACK_CHUNKS), :] = u2p_vmem[pl.ds(pl.multiple_of(row, PACK_CHUNKS), PACK_CHUNKS), :]

    @pl.when(b == 0)
    def _():
        cp = pltpu.make_async_copy(u2p_hbm, u2p_vmem, sem_in)
        cp.start()
        src_copy(0).start()
        ybuf[...] = jnp.zeros_like(ybuf)
        cp.wait()
        src_copy(0).wait()
        lax.fori_loop(0, br, lambda r, c: (gather_row(0, 0, r), c)[1], 0)

    @pl.when(b < n_used)
    def _():
        g = b // SRC_GROUP
        phase = lax.rem(b, SRC_GROUP)

        more = (g + 1) * SRC_GROUP < n_used

        @pl.when(jnp.logical_and(phase == 1, more))
        def _():
            src_copy(g + 1).start()

        @pl.when(jnp.logical_and(phase == SRC_GROUP - 1, more))
        def _():
            src_copy(g + 1).wait()

        e = bexp_ref[b]
        prev = bexp_ref[jnp.maximum(b - 1, 0)]

        @pl.when(jnp.logical_or(b == 0, e != prev))
        def _():
            w13_s[:, 0:EXPERT_FF] = w1_ref[...].astype(BF16)
            w13_s[:, EXPERT_FF:2 * EXPERT_FF] = w3_ref[...].astype(BF16)
            w2_s[...] = w2_ref[...].astype(BF16)

        valid = bval_ref[b]
        sbase = src_base(b)
        slot = lax.rem(b, 2)

        pb = jnp.maximum(b - 1, 0)
        p_valid = jnp.where(b > 0, bval_ref[pb], 0)
        p_base = src_base(pb)
        n_base = src_base(b + 1)
        for r in range(br):
            gather_row(1 - slot, n_base, r)
            scatter_row(1 - slot, p_base, p_valid, r)

        x = _unpack_rows(xbuf, br, slot * (br * PACK_CHUNKS))
        rows = lax.broadcasted_iota(I32, x.shape, 0)
        x = jnp.where(rows < valid, x, jnp.zeros_like(x))
        ab = jnp.dot(x, w13_s[...], preferred_element_type=F32)
        hid = (_silu(ab[:, 0:EXPERT_FF]) * ab[:, EXPERT_FF:2 * EXPERT_FF]).astype(BF16)
        y = jnp.dot(hid, w2_s[...], preferred_element_type=F32)

        @pl.when(b >= 1)
        def _():
            out_wait(slot)

        _pack_rows(ybuf, y, slot * (br * PACK_CHUNKS))

        @pl.when(b == n_used - 1)
        def _():
            lax.fori_loop(0, br, lambda r, c: (scatter_row(slot, sbase, valid, r), c)[1], 0)
            out_wait(1 - slot)
            out_wait(slot)


def _moe_experts(bexp, bval, nused, u2p, row_src, w1, w3, w2, n_blocks, n_tokens):
    br = MOE_ROWS

    def w_idx(b, bexp, bval, nused):
        return (bexp[jnp.minimum(b, nused[0] - 1)], 0, 0)

    grid_spec = pltpu.PrefetchScalarGridSpec(
        num_scalar_prefetch=3, grid=(n_blocks,),
        in_specs=[pl.BlockSpec(memory_space=pl.ANY), pl.BlockSpec(memory_space=pl.ANY),
                  pl.BlockSpec((None, D_MODEL, EXPERT_FF), w_idx),
                  pl.BlockSpec((None, D_MODEL, EXPERT_FF), w_idx),
                  pl.BlockSpec((None, EXPERT_FF, D_MODEL), w_idx)],
        out_specs=pl.BlockSpec(memory_space=pl.ANY),
        scratch_shapes=[pltpu.VMEM(u2p.shape, U32),
                        pltpu.VMEM((D_MODEL, 2 * EXPERT_FF), BF16),
                        pltpu.VMEM((EXPERT_FF, D_MODEL), BF16),
                        pltpu.VMEM((2 * PACK_CHUNKS * br, LANES), U32),
                        pltpu.VMEM((2 * br * PACK_CHUNKS, LANES), U32),
                        pltpu.SMEM((2 * SRC_GROUP * br,), I32),
                        pltpu.SemaphoreType.DMA, pltpu.SemaphoreType.DMA,
                        pltpu.SemaphoreType.DMA((2,))])
    n_out_tiles = n_tokens * TOP_K + 2 * br
    return pl.pallas_call(
        functools.partial(_moe_kernel, n_tokens=n_tokens), grid_spec=grid_spec,
        out_shape=jax.ShapeDtypeStruct((n_out_tiles * PACK_CHUNKS, LANES), U32),
        compiler_params=pltpu.CompilerParams(dimension_semantics=("arbitrary",),
                                             vmem_limit_bytes=MOE_VMEM_LIMIT),
        name="moe_experts",
    )(bexp, bval, nused, u2p, row_src, w1, w3, w2)


def _combine_kernel(w_hbm, yt_ref, base_ref, mod_ref, g_ref, b_ref, yc_ref, yl_ref,
                    w_smem, acc_lo, acc_hi, sem_w, *, n_ctx_tiles):
    i = pl.program_id(0)
    n_steps = pl.num_programs(0)
    n_tok = acc_lo.shape[0] // PACK_CHUNKS
    n_idx = n_tok * TOP_K

    def w_copy(tile):
        return pltpu.make_async_copy(w_hbm.at[pl.ds(tile * n_idx, n_idx)],
                                     w_smem.at[pl.ds(lax.rem(tile, 2) * n_idx, n_idx)], sem_w)

    @pl.when(i == 0)
    def _():
        w_copy(i).start()

    w_copy(i).wait()

    @pl.when(i + 1 < n_steps)
    def _():
        w_copy(i + 1).start()

    wbase = lax.rem(i, 2) * n_idx

    per_tile = SUBLANES // PACK_CHUNKS
    first = lax.broadcasted_iota(I32, (SUBLANES, LANES), 0) < PACK_CHUNKS

    def reduce_token(t, carry):
        lo = hi = None
        for m in range(TOP_K // per_tile):
            j = t * TOP_K + m * per_tile
            words = yt_ref[pl.ds(pl.multiple_of(j * PACK_CHUNKS, SUBLANES), SUBLANES), :]
            wgt = jnp.where(first, w_smem[wbase + j], w_smem[wbase + j + 1])
            t_lo = wgt * pltpu.bitcast(words << 16, F32)
            t_hi = wgt * pltpu.bitcast(words & jnp.uint32(0xFFFF0000), F32)
            lo = t_lo if lo is None else lo + t_lo
            hi = t_hi if hi is None else hi + t_hi
        row = pl.multiple_of(t * PACK_CHUNKS, PACK_CHUNKS)
        acc_lo[pl.ds(row, PACK_CHUNKS), :] = lo[0:PACK_CHUNKS] + lo[PACK_CHUNKS:SUBLANES]
        acc_hi[pl.ds(row, PACK_CHUNKS), :] = hi[0:PACK_CHUNKS] + hi[PACK_CHUNKS:SUBLANES]
        return carry

    lax.fori_loop(0, n_tok, reduce_token, 0)
    parts = []
    for s in range(PACK_CHUNKS):
        parts.append(acc_lo[pl.ds(s, n_tok, stride=PACK_CHUNKS), :])
        parts.append(acc_hi[pl.ds(s, n_tok, stride=PACK_CHUNKS), :])
    moe = jnp.concatenate(parts, axis=1)
    gate2 = mod_ref[:, 5 * D_MODEL:6 * D_MODEL]
    y = _layer_norm(base_ref[...] + gate2 * moe, g_ref[...], b_ref[...])

    @pl.when(i < n_ctx_tiles)
    def _():
        yc_ref[...] = y

    @pl.when(i >= n_ctx_tiles)
    def _():
        yl_ref[...] = y


def _combine(w_flat, yt, base, mod3, l2g, l2b, n_ctx, seq_tokens):
    n = base.shape[0]
    tc = COMB_TILE
    n_ctx_tiles = n_ctx // tc
    n_seq_tiles = seq_tokens // tc

    def mod_idx(i):
        return (jnp.where(i < n_ctx_tiles, 0, 1 + (i - n_ctx_tiles) // n_seq_tiles), 0, 0)

    full = lambda a: pl.BlockSpec(a.shape, lambda i: (0,) * a.ndim)
    return pl.pallas_call(
        functools.partial(_combine_kernel, n_ctx_tiles=n_ctx_tiles),
        grid=(n // tc,),
        in_specs=[pl.BlockSpec(memory_space=pl.ANY),
                  pl.BlockSpec((tc * TOP_K * PACK_CHUNKS, LANES), lambda i: (i, 0)),
                  pl.BlockSpec((tc, D_MODEL), lambda i: (i, 0)),
                  pl.BlockSpec((None, 1, mod3.shape[2]), mod_idx), full(l2g), full(l2b)],
        out_specs=(pl.BlockSpec((tc, D_MODEL), lambda i: (jnp.minimum(i, n_ctx_tiles - 1), 0)),
                   pl.BlockSpec((tc, D_MODEL), lambda i: (jnp.maximum(i - n_ctx_tiles, 0), 0))),
        out_shape=(jax.ShapeDtypeStruct((n_ctx, D_MODEL), F32),
                   jax.ShapeDtypeStruct((n - n_ctx, D_MODEL), F32)),
        scratch_shapes=[pltpu.SMEM((2 * tc * TOP_K,), F32),
                        pltpu.VMEM((tc * PACK_CHUNKS, LANES), F32),
                        pltpu.VMEM((tc * PACK_CHUNKS, LANES), F32),
                        pltpu.SemaphoreType.DMA],
        compiler_params=_cparams(1), name="combine",
    )(w_flat, yt, base, mod3, l2g, l2b)


def _rope_tables(n_tok, tile):
    rows = n_tok // GRID_W
    row_idx = jnp.repeat(jnp.arange(rows, dtype=F32), GRID_W)
    col_idx = jnp.tile(jnp.arange(GRID_W, dtype=F32), rows)
    inv_freq = 1.0 / (ROPE_THETA ** (jnp.arange(0, ROPE_AXIS_DIM, 2, dtype=F32) / ROPE_AXIS_DIM))
    ang_r = row_idx[:, None] * inv_freq[None, :]
    ang_c = col_idx[:, None] * inv_freq[None, :]
    ang = jnp.concatenate([ang_r, ang_r, ang_c, ang_c], axis=-1)
    cos, sin = jnp.cos(ang), jnp.sin(ang)
    quarter = (jnp.arange(HEAD_DIM) // (ROPE_AXIS_DIM // 2)) % 2
    sin_a = jnp.where(quarter == 0, -sin, 0.0)
    sin_b = jnp.where(quarter == 1, sin, 0.0)
    rep = LANES // HEAD_DIM
    ident = lambda v: jnp.full((tile, LANES), v, F32)
    cos_t = jnp.concatenate([jnp.tile(cos, (1, rep)), ident(1.0)], axis=0)
    sa_t = jnp.concatenate([jnp.tile(sin_a, (1, rep)), ident(0.0)], axis=0)
    sb_t = jnp.concatenate([jnp.tile(sin_b, (1, rep)), ident(0.0)], axis=0)
    ident_tr = lambda v: jnp.full((HEAD_DIM, tile), v, F32)
    cos_tr = jnp.concatenate([cos.T, ident_tr(1.0)], axis=1)
    sin_tr = jnp.concatenate([sin.T, ident_tr(0.0)], axis=1)
    return cos_t, sa_t, sb_t, cos_tr, sin_tr


def _dup_heads(a):
    parts = []
    for h in range(KV_HEADS):
        blk = a[..., h * HEAD_DIM:(h + 1) * HEAD_DIM]
        parts += [blk] * (LANES // HEAD_DIM)
    return jnp.concatenate(parts, axis=-1)


def kernel(x_prompt, x_sample, cache_k, cache_v, state_gla_fwd, state_gla_bwd, c, c_ctx, w_ada, b_ada, w_in, q_norm, k_norm, gla_wa_fwd, gla_ba_fwd, gla_wa_bwd, gla_ba_bwd, gla_norm, w_out, ln1_g, ln1_b, ln2_g, ln2_b, w_router, router_bias, exp_w1, exp_w3, exp_w2, sh_w1, sh_w3, sh_w2):
    n_ctx_b, ctx_seq, _ = x_prompt.shape
    n_lat_b, lat_seq, _ = x_sample.shape
    n_ctx = n_ctx_b * ctx_seq
    n_lat = n_lat_b * lat_seq
    n = n_ctx + n_lat
    l = 0

    x_c = x_prompt.reshape(n_ctx, D_MODEL)
    x_l = x_sample.reshape(n_lat, D_MODEL)

    c_rows = jnp.zeros((SUBLANES, D_MODEL), F32).at[0].set(c_ctx).at[1:1 + n_lat_b].set(c)
    mod = _modulation(c_rows, w_ada[l], b_ada[l][None, :])
    mod3 = mod.reshape(SUBLANES, 1, 6 * D_MODEL)

    wi = w_in[l]
    o_q, o_k, o_v, o_gq, o_gk, o_gv, o_gg, o_rf, o_rb, o_end = np.cumsum(
        [0, ATT_WIDTH, KV_HEADS * HEAD_DIM, KV_HEADS * HEAD_DIM, GLA_KW, GLA_KW, GLA_WIDTH, GLA_WIDTH,
         GLA_GATE_RANK, GLA_GATE_RANK])
    w_tok = jnp.concatenate([
        _dup_heads(wi[:, o_k:o_v]), wi[:, o_v:o_gq], wi[:, o_gq:o_gk],
        wi[:, o_gv:o_gg], wi[:, o_gg:o_rf], wi[:, o_rf:o_end],
        jnp.zeros((D_MODEL, LANES - 2 * GLA_GATE_RANK), F32)], axis=1).astype(BF16)
    w_tr = jnp.concatenate([wi[:, o_q:o_k], wi[:, o_v:o_gq], wi[:, o_gk:o_gv], wi[:, o_rf:o_end]],
                           axis=1).T.astype(BF16)
    rep = LANES // HEAD_DIM
    qn = q_norm[l][:, None]
    kn = jnp.tile(k_norm[l], rep)[None, :]
    seg = jnp.asarray(np.kron(np.eye(rep), np.ones((HEAD_DIM, HEAD_DIM))), BF16)
    wa = jnp.zeros((LANES, 2 * GLA_KW), F32)
    wa = wa.at[0:GLA_GATE_RANK, 0:GLA_KW].set(gla_wa_fwd[l])
    wa = wa.at[GLA_GATE_RANK:2 * GLA_GATE_RANK, GLA_KW:].set(gla_wa_bwd[l])
    ba = jnp.concatenate([gla_ba_fwd[l], gla_ba_bwd[l]])[None, :]
    wat = wa[0:2 * GLA_GATE_RANK, :].T
    bat = ba.T
    cos_t, sa_t, sb_t, cos_tr, sin_tr = _rope_tables(lat_seq, TOK_TILE)

    (qt, k_dup, vt, k32, v32, gq, gv, gg, la, gkt, lat) = _in_projection(
        x_c, x_l, mod3, w_tok, w_tr, qn, kn, cos_t, sa_t, sb_t, cos_tr, sin_tr, seg, wa, ba, wat, bat,
        lat_seq // TOK_TILE)

    ck = _dup_heads(cache_k[:, l].reshape(n_lat_b, -1, KV_HEADS * HEAD_DIM)).astype(BF16)
    cvt = cache_v[:, l].reshape(n_lat_b, -1, KV_HEADS * HEAD_DIM).transpose(0, 2, 1).astype(BF16)
    att_c = _attention(qt, k_dup, vt, None, 0, n_ctx_b, ctx_seq)
    att_l = _attention(qt, k_dup, vt, (ck, cvt), n_ctx, n_lat_b, lat_seq)

    gconst, levels_of = _gla_constants()
    to_dev = lambda t: (jnp.asarray(t[0], BF16), jnp.asarray(t[1], BF16), jnp.asarray(t[2], F32))
    bd = jnp.asarray(np.kron(np.eye(GLA_HEADS), np.ones((GLA_DK, GLA_DV))), BF16)
    vbd = jnp.asarray(np.kron(np.eye(GLA_HEADS), np.ones((GLA_CHUNK, GLA_DV))), BF16)
    consts = ((to_dev(gconst["f"]), to_dev(gconst["b"])), levels_of, bd, vbd)
    s_zero = jnp.zeros((n_ctx_b, GLA_HEADS, GLA_DK, GLA_DV), F32)
    of_c, ob_c, sf_new, sb_new = _gla(gq, la, gkt, lat, gv, s_zero, s_zero, consts, 0, n_ctx_b, ctx_seq)
    of_l, ob_l, _, _ = _gla(gq, la, gkt, lat, gv, state_gla_fwd[:, l], state_gla_bwd[:, l], consts,
                            n_ctx, n_lat_b, lat_seq)

    sw13 = jnp.concatenate([sh_w1[l], sh_w3[l]], axis=1).astype(BF16)
    base, u2_rows, logits_t = _out_projection(
        att_c, att_l, of_c, of_l, ob_c, ob_l, gg, x_c, x_l, mod3, w_out[l].astype(BF16),
        gla_norm[l][None, :], ln1_g[l][None, :], ln1_b[l][None, :], w_router[l].T.astype(BF16), sw13,
        sh_w2[l].astype(BF16), lat_seq // TOK_TILE)

    upper = jnp.asarray(np.triu(np.ones((TOK_TILE, TOK_TILE)), 1), BF16)
    idx_t, w_t, pos_t, counts = _route(logits_t, router_bias[l][:, None], upper)
    n_blocks = n * TOP_K // MOE_ROWS + N_EXPERTS
    n_blocks_pad = -(-n_blocks // LANES) * LANES
    lower = jnp.asarray(np.tril(np.ones((N_EXPERTS, N_EXPERTS)), -1), F32)
    dest_t, bexp, bval, nused = _destinations(counts, lower, idx_t, pos_t, n_blocks_pad)
    dest_flat = dest_t.T.reshape(-1)
    w_flat = w_t.T.reshape(-1)

    row_src = _invert_rows(dest_flat, n_blocks * MOE_ROWS)
    yt = _moe_experts(bexp.reshape(-1), bval.reshape(-1), nused.reshape(-1)[0:1], u2_rows, row_src,
                      exp_w1[l], exp_w3[l], exp_w2[l], n_blocks, n)
    y_c, y_l = _combine(w_flat, yt, base, mod3, ln2_g[l][None, :], ln2_b[l][None, :], n_ctx, lat_seq)

    y_prompt = y_c.reshape(n_ctx_b, ctx_seq, D_MODEL)
    y_sample = y_l.reshape(n_lat_b, lat_seq, D_MODEL)
    new_cache_k = k32.reshape(n_ctx_b, 1, ctx_seq, KV_HEADS, HEAD_DIM)
    new_cache_v = v32.reshape(n_ctx_b, 1, ctx_seq, KV_HEADS, HEAD_DIM)
    return (y_prompt, y_sample, new_cache_k, new_cache_v, sf_new[:, None], sb_new[:, None])
```

```python
import functools

import numpy as np
import jax
import jax.numpy as jnp
from jax import lax
from jax.experimental import pallas as pl
from jax.experimental.pallas import tpu as pltpu
from jax.experimental.pallas import tpu_sc as plsc

F32 = jnp.float32
BF16 = jnp.bfloat16
I32 = jnp.int32

D_MODEL = 1024
GRID_W = 64
HEAD_DIM = 64
N_HEADS = 8
KV_HEADS = 2
ATT_WIDTH = N_HEADS * HEAD_DIM
ATT_SCALE = HEAD_DIM ** -0.5
LOG2_E = 1.4426950408889634
ROPE_AXIS_DIM = HEAD_DIM // 2
ROPE_THETA = 10000.0
GLA_HEADS = 4
GLA_DK = 64
GLA_DV = 128
GLA_WIDTH = GLA_HEADS * GLA_DV
GLA_KW = GLA_HEADS * GLA_DK
GLA_GATE_RANK = 16
GLA_TAU = 16.0
N_EXPERTS = 256
TOP_K = 8
EXPERT_FF = 256
SHARED_FF = 256
ROUTED_SCALE = 2.5
DEPTH = 1
ALPHA = (2.0 * DEPTH) ** 0.25
EPS = 1e-6

LANES = 128
SUBLANES = 8
ROW_CHUNKS = D_MODEL // LANES
VMEM_LIMIT = 56 * 1024 * 1024

TOK_TILE = 512
ATT_TQ = 128
GLA_CHUNK = 128
GLA_LEVELS = ((32, 128), (8, 32), (2, 8), (1, 2))
MOE_ROWS = 256
MOE_VMEM_LIMIT = 62 * 1024 * 1024
COMB_TILE = 128
HIGHEST = lax.Precision.HIGHEST


def _cparams(n_axes):
    return pltpu.CompilerParams(dimension_semantics=("arbitrary",) * n_axes,
                                vmem_limit_bytes=VMEM_LIMIT)


def _silu(x):
    return x * jax.nn.sigmoid(x)


def _log_sigmoid(x):
    return jnp.minimum(x, 0.0) - jnp.log(1.0 + jnp.exp(-jnp.abs(x)))


def _load_row_tiles(ref, n_rows, row0=0):
    return jnp.concatenate(
        [ref[pl.ds(row0 * ROW_CHUNKS + cidx, n_rows, stride=ROW_CHUNKS), :] for cidx in range(ROW_CHUNKS)],
        axis=1)


def _store_row_tiles(ref, x):
    for cidx in range(ROW_CHUNKS):
        ref[pl.ds(cidx, x.shape[0], stride=ROW_CHUNKS), :] = x[:, cidx * LANES:(cidx + 1) * LANES]


def _layer_norm(z, g, b):
    mu = jnp.mean(z, axis=-1, keepdims=True)
    zc = z - mu
    var = jnp.mean(zc * zc, axis=-1, keepdims=True)
    return zc * lax.rsqrt(var + EPS) * g + b


def _mod_kernel(c_ref, w_ref, b_ref, o_ref):
    s = _silu(c_ref[...]).astype(BF16)
    o_ref[...] = jnp.dot(s, w_ref[...].astype(BF16), preferred_element_type=F32) + b_ref[...]


def _modulation(c_rows, w_ada, b_ada):
    n_cols = w_ada.shape[1]
    tn = 512
    return pl.pallas_call(
        _mod_kernel,
        grid=(n_cols // tn,),
        in_specs=[pl.BlockSpec((SUBLANES, D_MODEL), lambda j: (0, 0)),
                  pl.BlockSpec((D_MODEL, tn), lambda j: (0, j)),
                  pl.BlockSpec((1, tn), lambda j: (0, j))],
        out_specs=pl.BlockSpec((SUBLANES, tn), lambda j: (0, j)),
        out_shape=jax.ShapeDtypeStruct((SUBLANES, n_cols), F32),
        compiler_params=_cparams(1),
        name="modulation",
    )(c_rows, w_ada, b_ada)


_C_K = 0
_C_V = _C_K + 2 * LANES
_C_GQ = _C_V + KV_HEADS * HEAD_DIM
_C_GV = _C_GQ + GLA_KW
_C_GG = _C_GV + GLA_WIDTH
_C_RA = _C_GG + GLA_WIDTH
_C_END = _C_RA + LANES
_R_Q = 0
_R_V = _R_Q + ATT_WIDTH
_R_GK = _R_V + KV_HEADS * HEAD_DIM
_R_RA = _R_GK + GLA_KW
_R_END = _R_RA + 2 * GLA_GATE_RANK


def _inproj_kernel(xc_ref, xl_ref, mod_ref, w_ref, wt_ref, qn_ref, kn_ref, cos_ref, sa_ref, sb_ref,
                   cost_ref, sint_ref, seg_ref, wa_ref, ba_ref, wat_ref, bat_ref,
                   qt_ref, k_ref, vt_ref, k32_ref, v32_ref, gq_ref, gv_ref, gg_ref,
                   la_ref, gkt_ref, lat_ref, *, n_ctx_tiles):
    i = pl.program_id(0)
    m = mod_ref[...]
    shift1 = m[:, 0:D_MODEL]
    scale1 = m[:, D_MODEL:2 * D_MODEL]
    x = jnp.where(i < n_ctx_tiles, xc_ref[...], xl_ref[...])
    u = (x * (1.0 + scale1) + shift1).astype(BF16)

    cos = cos_ref[...]
    sin_a = sa_ref[...]
    sin_b = sb_ref[...]
    seg = seg_ref[...]
    lane = lax.broadcasted_iota(I32, (u.shape[0], LANES), 1)
    low = lane < HEAD_DIM

    def proj(c0, c1):
        return jnp.dot(u, w_ref[:, c0:c1], preferred_element_type=F32)

    def head_norm(blk, gain):
        ss = jnp.dot((blk * blk).astype(BF16), seg, preferred_element_type=F32) * (1.0 / HEAD_DIM)
        return blk * lax.rsqrt(ss + EPS) * gain

    def rope(blk):
        return (blk * cos + pltpu.roll(blk, LANES - ROPE_AXIS_DIM // 2, 1) * sin_a
                + pltpu.roll(blk, ROPE_AXIS_DIM // 2, 1) * sin_b)

    pk = proj(_C_K, _C_V)
    kn = [head_norm(pk[:, j * LANES:(j + 1) * LANES], kn_ref[...]) for j in range(KV_HEADS)]
    for j in range(KV_HEADS):
        k_ref[:, j * LANES:(j + 1) * LANES] = rope(kn[j]).astype(BF16)

    @pl.when(i < n_ctx_tiles)
    def _():
        k32_ref[...] = jnp.where(low, kn[0], kn[1])
        v32_ref[...] = proj(_C_V, _C_GQ)

    gq_ref[...] = proj(_C_GQ, _C_GV) * (GLA_DK ** -0.5)
    gv_ref[...] = proj(_C_GV, _C_GG).astype(BF16)
    gg_ref[...] = proj(_C_GG, _C_RA).astype(BF16)

    ra = proj(_C_RA, _C_END)
    pre = jnp.dot(ra, wa_ref[...], precision=HIGHEST, preferred_element_type=F32) + ba_ref[...]
    la_ref[...] = _log_sigmoid(pre) * (1.0 / GLA_TAU)

    pt = lax.dot_general(wt_ref[...], u, (((1,), (1,)), ((), ())), preferred_element_type=F32)
    cos_t = cost_ref[...]
    sin_t = sint_ref[...]
    quarter = ROPE_AXIS_DIM // 2
    for h in range(N_HEADS):
        blk = pt[_R_Q + h * HEAD_DIM:_R_Q + (h + 1) * HEAD_DIM, :]
        ms = jnp.mean(blk * blk, axis=0, keepdims=True)
        qn = blk * lax.rsqrt(ms + EPS) * qn_ref[...]
        rot = jnp.concatenate([-qn[quarter:2 * quarter], qn[0:quarter],
                               -qn[3 * quarter:4 * quarter], qn[2 * quarter:3 * quarter]], axis=0)
        qt_ref[h * HEAD_DIM:(h + 1) * HEAD_DIM, :] = (
            (qn * cos_t + rot * sin_t) * (ATT_SCALE * LOG2_E)).astype(BF16)
    vt_ref[...] = pt[_R_V:_R_GK, :].astype(BF16)
    gkt_ref[...] = pt[_R_GK:_R_RA, :]
    rat = pt[_R_RA:_R_END, :]
    pre_t = jnp.dot(wat_ref[...], rat, precision=HIGHEST, preferred_element_type=F32) + bat_ref[...]
    lat_ref[...] = _log_sigmoid(pre_t) * (1.0 / GLA_TAU)


def _in_projection(x_c, x_l, mod3, w_tok, w_tr, qn, kn, cos_t, sa_t, sb_t, cos_tr, sin_tr, seg, wa, ba,
                   wat, bat, n_seq_tiles):
    n_ctx = x_c.shape[0]
    n = n_ctx + x_l.shape[0]
    tb = TOK_TILE
    n_ctx_tiles = n_ctx // tb
    n_tiles = n // tb
    n_rope_blocks = cos_t.shape[0] // tb - 1

    def mod_idx(i):
        return (jnp.where(i < n_ctx_tiles, 0, 1 + (i - n_ctx_tiles) // n_seq_tiles), 0, 0)

    def rope_blk(i):
        return jnp.where(i < n_ctx_tiles, n_rope_blocks, (i - n_ctx_tiles) % n_seq_tiles)

    def rope_idx(i):
        return (rope_blk(i), 0)

    def ctx_idx(i):
        return (jnp.minimum(i, n_ctx_tiles - 1), 0)

    tok = lambda w: pl.BlockSpec((tb, w), lambda i: (i, 0))
    full = lambda a: pl.BlockSpec(a.shape, lambda i: (0,) * a.ndim)
    tr = lambda r: pl.BlockSpec((r, tb), lambda i: (0, i))
    rope_tr = pl.BlockSpec((HEAD_DIM, tb), lambda i: (0, rope_blk(i)))
    out_shapes = (
        jax.ShapeDtypeStruct((ATT_WIDTH, n), BF16),
        jax.ShapeDtypeStruct((n, 2 * LANES), BF16),
        jax.ShapeDtypeStruct((KV_HEADS * HEAD_DIM, n), BF16),
        jax.ShapeDtypeStruct((n_ctx, LANES), F32),
        jax.ShapeDtypeStruct((n_ctx, LANES), F32),
        jax.ShapeDtypeStruct((n, GLA_KW), F32),
        jax.ShapeDtypeStruct((n, GLA_WIDTH), BF16),
        jax.ShapeDtypeStruct((n, GLA_WIDTH), BF16),
        jax.ShapeDtypeStruct((n, 2 * GLA_KW), F32),
        jax.ShapeDtypeStruct((GLA_KW, n), F32),
        jax.ShapeDtypeStruct((2 * GLA_KW, n), F32),
    )
    out_specs = (tr(ATT_WIDTH), tok(2 * LANES), tr(KV_HEADS * HEAD_DIM),
                 pl.BlockSpec((tb, LANES), ctx_idx), pl.BlockSpec((tb, LANES), ctx_idx),
                 tok(GLA_KW), tok(GLA_WIDTH), tok(GLA_WIDTH), tok(2 * GLA_KW),
                 tr(GLA_KW), tr(2 * GLA_KW))
    in_specs = [pl.BlockSpec((tb, D_MODEL), ctx_idx),
                pl.BlockSpec((tb, D_MODEL), lambda i: (jnp.maximum(i - n_ctx_tiles, 0), 0)),
                pl.BlockSpec((None, 1, mod3.shape[2]), mod_idx),
                full(w_tok), full(w_tr), full(qn), full(kn),
                pl.BlockSpec((tb, LANES), rope_idx), pl.BlockSpec((tb, LANES), rope_idx),
                pl.BlockSpec((tb, LANES), rope_idx), rope_tr, rope_tr,
                full(seg), full(wa), full(ba), full(wat), full(bat)]
    return pl.pallas_call(
        functools.partial(_inproj_kernel, n_ctx_tiles=n_ctx_tiles),
        grid=(n_tiles,), in_specs=in_specs, out_specs=out_specs, out_shape=out_shapes,
        compiler_params=_cparams(1), name="in_projection",
    )(x_c, x_l, mod3, w_tok, w_tr, qn, kn, cos_t, sa_t, sb_t, cos_tr, sin_tr, seg, wa, ba, wat, bat)


def _attention_kernel(*refs, n_kv_parts):
    qt_ref = refs[0]
    k_refs = refs[1:1 + n_kv_parts]
    vt_refs = refs[1 + n_kv_parts:1 + 2 * n_kv_parts]
    o_ref = refs[1 + 2 * n_kv_parts]
    tq = qt_ref.shape[1]
    group = N_HEADS // KV_HEADS
    scores = []
    for kv in range(KV_HEADS):
        heads = range(kv * group, (kv + 1) * group)
        q_grp = jnp.concatenate([qt_ref[h * HEAD_DIM:(h + 1) * HEAD_DIM, :] for h in heads], axis=1)
        rhs = jnp.concatenate([q_grp, jnp.zeros_like(q_grp)], axis=0)
        scores.append([jnp.dot(k[:, kv * LANES:(kv + 1) * LANES], rhs, preferred_element_type=F32)
                       for k in k_refs])
    for kv in range(KV_HEADS):
        heads = range(kv * group, (kv + 1) * group)
        s = scores[kv]
        mx = functools.reduce(jnp.maximum, [jnp.max(x, axis=0, keepdims=True) for x in s])
        pr = [jnp.exp2(x - mx) for x in s]
        den = functools.reduce(jnp.add, [jnp.sum(x, axis=0, keepdims=True) for x in pr])
        acc = functools.reduce(jnp.add, [
            jnp.dot(vt[kv * HEAD_DIM:(kv + 1) * HEAD_DIM, :], x.astype(BF16),
                    preferred_element_type=F32) for x, vt in zip(pr, vt_refs)])
        out = (acc / den).astype(BF16)
        for j, h in enumerate(heads):
            o_ref[h * HEAD_DIM:(h + 1) * HEAD_DIM, :] = out[:, j * tq:(j + 1) * tq]


def _attention(qt, k, vt, extra_kv, row0, n_batch, seq):
    tq = ATT_TQ
    n_q = seq // tq
    q_blk0 = row0 // tq
    kv_blk0 = row0 // seq
    in_specs = [pl.BlockSpec((ATT_WIDTH, tq), lambda b, i: (0, q_blk0 + b * n_q + i))]
    k_spec = pl.BlockSpec((seq, 2 * LANES), lambda b, i: (kv_blk0 + b, 0))
    vt_spec = pl.BlockSpec((KV_HEADS * HEAD_DIM, seq), lambda b, i: (0, kv_blk0 + b))
    args_k, args_v, specs_k, specs_v = [k], [vt], [k_spec], [vt_spec]
    if extra_kv is not None:
        ck, cvt = extra_kv
        args_k.append(ck)
        args_v.append(cvt)
        specs_k.append(pl.BlockSpec((None, ck.shape[1], 2 * LANES), lambda b, i: (b, 0, 0)))
        specs_v.append(pl.BlockSpec((None, KV_HEADS * HEAD_DIM, cvt.shape[2]), lambda b, i: (b, 0, 0)))
    return pl.pallas_call(
        functools.partial(_attention_kernel, n_kv_parts=len(args_k)),
        grid=(n_batch, n_q),
        in_specs=in_specs + specs_k + specs_v,
        out_specs=pl.BlockSpec((ATT_WIDTH, tq), lambda b, i: (0, b * n_q + i)),
        out_shape=jax.ShapeDtypeStruct((ATT_WIDTH, n_batch * seq), BF16),
        compiler_params=_cparams(2), name="attention",
    )(qt, *args_k, *args_v)


def _gla_constants():
    c = GLA_CHUNK
    idx = np.arange(c)
    q_mats, k_mats, masks, levels_of = [], [], [], []
    for li, (s, p) in enumerate(GLA_LEVELS):
        start = (idx // s) * s
        end = start + s - 1
        k_mats.append(((idx[None, :] > idx[:, None]) & (idx[None, :] <= end[:, None])))
        for d in range(p // s - 1):
            lo = np.maximum(start - d * s, 0)
            q_mats.append((idx[None, :] >= lo[:, None]) & (idx[None, :] <= idx[:, None]))
            masks.append((idx[:, None] // p == idx[None, :] // p)
                         & (idx[:, None] // s - idx[None, :] // s - 1 == d))
            levels_of.append(li)
    masks.append(np.eye(c, dtype=bool))
    levels_of.append(len(GLA_LEVELS) - 1)
    q_mats.append(idx[None, :] <= idx[:, None])
    k_mats = k_mats[:-1]
    k_mats.append(idx[None, :] > idx[:, None])
    k_mats.append(np.ones((c, c), bool))
    out = {}
    for name, flip in (("f", False), ("b", True)):
        f = (lambda a: a[::-1, ::-1]) if flip else (lambda a: a)
        lq = np.concatenate([f(a) for a in q_mats], axis=0).astype(np.float32)
        lkt = np.concatenate([f(a).T for a in k_mats], axis=1).astype(np.float32)
        mk = np.stack([np.tile(f(a), (1, GLA_HEADS)) for a in masks]).astype(np.float32)
        out[name] = (np.concatenate([lq, lq], axis=1), np.concatenate([lkt, lkt], axis=0), mk)
    return out, tuple(levels_of)


def _gla_direction(q, g, gkt, gt, v, lq2, lkt2, masks_ref, bd, vbd, s_ref, levels_of):
    c = GLA_CHUNK
    n_var = len(levels_of)
    n_lev = len(GLA_LEVELS)
    g_hi = g.astype(BF16)
    g_lo = (g - g_hi.astype(F32)).astype(BF16)
    fq = jnp.dot(lq2, jnp.concatenate([g_hi, g_lo], axis=0), preferred_element_type=F32)
    gt_hi = gt.astype(BF16)
    gt_lo = (gt - gt_hi.astype(F32)).astype(BF16)
    fk = jnp.dot(jnp.concatenate([gt_hi, gt_lo], axis=1), lkt2, preferred_element_type=F32)

    def key_factor(f):
        return gkt * jnp.exp(fk[:, f * c:(f + 1) * c])

    q_var = [(q * jnp.exp(fq[vi * c:(vi + 1) * c, :])).astype(BF16) for vi in range(n_var - 1)]
    q_var.append(q.astype(BF16))
    a = jnp.zeros((c, GLA_HEADS * c), F32)
    for li in range(n_lev):
        kt = (key_factor(li) if li < n_lev - 1 else gkt).astype(BF16)
        xt = jnp.concatenate([kt] * GLA_HEADS, axis=1) * bd
        vis = [vi for vi in range(n_var) if levels_of[vi] == li]
        res = jnp.dot(jnp.concatenate([q_var[vi] for vi in vis], axis=0), xt,
                      preferred_element_type=F32)
        for r, vi in enumerate(vis):
            a = a + masks_ref[vi] * res[r * c:(r + 1) * c, :]
    q_in = (q * jnp.exp(fq[(n_var - 1) * c:n_var * c, :])).astype(BF16)
    state = s_ref[...]
    v_bd = jnp.concatenate([v] * GLA_HEADS, axis=0) * vbd
    o = (jnp.dot(q_in, state.astype(BF16), preferred_element_type=F32)
         + jnp.dot(a.astype(BF16), v_bd, preferred_element_type=F32))
    k_out = key_factor(n_lev - 1).astype(BF16)
    e_tot = jnp.exp(fk[:, n_lev * c:(n_lev + 1) * c])
    upd = jnp.dot(k_out, v, preferred_element_type=F32)
    s_ref[...] = (state * jnp.concatenate([e_tot] * (GLA_WIDTH // c), axis=1)
                  + upd * bd.astype(F32))
    return o


def _gla_kernel(gq_f, la_f, gkt_f, lat_f, gv_f, gq_b, la_b, gkt_b, lat_b, gv_b,
                s0f_ref, s0b_ref, lq2f, lkt2f, mkf, lq2b, lkt2b, mkb, bd_ref, vbd_ref,
                of_ref, ob_ref, sf_ref, sb_ref, st_f, st_b, *, levels_of):
    n = pl.program_id(1)

    @pl.when(n == 0)
    def _():
        st_f[...] = jnp.zeros_like(st_f)
        st_b[...] = jnp.zeros_like(st_b)
        for h in range(GLA_HEADS):
            rows = slice(h * GLA_DK, (h + 1) * GLA_DK)
            cols = slice(h * GLA_DV, (h + 1) * GLA_DV)
            st_f[rows, cols] = s0f_ref[h]
            st_b[rows, cols] = s0b_ref[h]

    bd = bd_ref[...]
    vbd = vbd_ref[...]
    of_ref[...] = _gla_direction(gq_f[...], la_f[...], gkt_f[...], lat_f[...], gv_f[...],
                                 lq2f[...], lkt2f[...], mkf, bd, vbd, st_f, levels_of)
    ob_ref[...] = _gla_direction(gq_b[...], la_b[...], gkt_b[...], lat_b[...], gv_b[...],
                                 lq2b[...], lkt2b[...], mkb, bd, vbd, st_b, levels_of)

    @pl.when(n == pl.num_programs(1) - 1)
    def _():
        for h in range(GLA_HEADS):
            rows = slice(h * GLA_DK, (h + 1) * GLA_DK)
            cols = slice(h * GLA_DV, (h + 1) * GLA_DV)
            sf_ref[h] = st_f[rows, cols]
            sb_ref[h] = st_b[rows, cols]


def _gla(gq, la, gkt, lat, gv, s0f, s0b, consts, row0, n_batch, seq):
    (cf, cb), levels_of, bd, vbd = consts
    c = GLA_CHUNK
    nc = seq // c
    blk0 = row0 // c
    n_la_blocks_b = 1
    fwd = lambda b, n: blk0 + b * nc + n
    bwd = lambda b, n: blk0 + b * nc + (nc - 1 - n)

    def tok(w, which, col=0):
        return pl.BlockSpec((c, w), lambda b, n: (which(b, n), col))

    def tr(r, which, row=0):
        return pl.BlockSpec((r, c), lambda b, n: (row, which(b, n)))

    full = lambda a: pl.BlockSpec(a.shape, lambda b, n: (0,) * a.ndim)
    st_spec = pl.BlockSpec((None, GLA_HEADS, GLA_DK, GLA_DV), lambda b, n: (b, 0, 0, 0))
    in_specs = [tok(GLA_KW, fwd), tok(GLA_KW, fwd, 0), tr(GLA_KW, fwd), tr(GLA_KW, fwd, 0),
                tok(GLA_WIDTH, fwd),
                tok(GLA_KW, bwd), tok(GLA_KW, bwd, n_la_blocks_b), tr(GLA_KW, bwd),
                tr(GLA_KW, bwd, 1), tok(GLA_WIDTH, bwd),
                st_spec, st_spec,
                full(cf[0]), full(cf[1]), full(cf[2]), full(cb[0]), full(cb[1]), full(cb[2]),
                full(bd), full(vbd)]
    out_specs = (pl.BlockSpec((c, GLA_WIDTH), lambda b, n: (b * nc + n, 0)),
                 pl.BlockSpec((c, GLA_WIDTH), lambda b, n: (b * nc + (nc - 1 - n), 0)),
                 st_spec, st_spec)
    out_shape = (jax.ShapeDtypeStruct((n_batch * seq, GLA_WIDTH), F32),
                 jax.ShapeDtypeStruct((n_batch * seq, GLA_WIDTH), F32),
                 jax.ShapeDtypeStruct((n_batch, GLA_HEADS, GLA_DK, GLA_DV), F32),
                 jax.ShapeDtypeStruct((n_batch, GLA_HEADS, GLA_DK, GLA_DV), F32))
    return pl.pallas_call(
        functools.partial(_gla_kernel, levels_of=levels_of),
        grid=(n_batch, nc), in_specs=in_specs, out_specs=out_specs, out_shape=out_shape,
        scratch_shapes=[pltpu.VMEM((GLA_KW, GLA_WIDTH), F32), pltpu.VMEM((GLA_KW, GLA_WIDTH), F32)],
        compiler_params=_cparams(2), name="gla",
    )(gq, la, gkt, lat, gv, gq, la, gkt, lat, gv, s0f, s0b,
      cf[0], cf[1], cf[2], cb[0], cb[1], cb[2], bd, vbd)


def _outproj_kernel(attc_ref, attl_ref, ofc_ref, ofl_ref, obc_ref, obl_ref, gg_ref, xc_ref, xl_ref,
                    mod_ref, wo_ref, gn_ref, l1g_ref, l1b_ref, wrt_ref, sw13_ref, sw2_ref,
                    base_ref, u2_ref, lg_ref, *, n_ctx_tiles):
    is_ctx = pl.program_id(0) < n_ctx_tiles
    pick = lambda a_ref, b_ref: jnp.where(is_ctx, a_ref[...], b_ref[...])
    m = mod_ref[...]
    gate1 = m[:, 2 * D_MODEL:3 * D_MODEL]
    shift2 = m[:, 3 * D_MODEL:4 * D_MODEL]
    scale2 = m[:, 4 * D_MODEL:5 * D_MODEL]
    gate2 = m[:, 5 * D_MODEL:6 * D_MODEL]
    og = pick(ofc_ref, ofl_ref) + pick(obc_ref, obl_ref)
    gg = gg_ref[...].astype(F32)
    parts = []
    for h in range(GLA_HEADS):
        blk = og[:, h * GLA_DV:(h + 1) * GLA_DV]
        ms = jnp.mean(blk * blk, axis=-1, keepdims=True)
        nb = blk * lax.rsqrt(ms + EPS) * gn_ref[...]
        parts.append((nb * _silu(gg[:, h * GLA_DV:(h + 1) * GLA_DV])).astype(BF16))
    att_t = pick(attc_ref, attl_ref)
    hmix = (lax.dot_general(att_t, wo_ref[0:ATT_WIDTH, :], (((0,), (0,)), ((), ())),
                            preferred_element_type=F32)
            + jnp.dot(jnp.concatenate(parts, axis=1), wo_ref[ATT_WIDTH:, :],
                      preferred_element_type=F32))
    x1 = _layer_norm(ALPHA * pick(xc_ref, xl_ref) + gate1 * hmix, l1g_ref[...], l1b_ref[...])
    u2 = x1 * (1.0 + scale2) + shift2
    u2b = u2.astype(BF16)
    lg_ref[...] = lax.dot_general(wrt_ref[...], u2b, (((1,), (1,)), ((), ())),
                                  preferred_element_type=F32)
    ab = jnp.dot(u2b, sw13_ref[...], preferred_element_type=F32)
    hid = (_silu(ab[:, 0:SHARED_FF]) * ab[:, SHARED_FF:2 * SHARED_FF]).astype(BF16)
    shared = jnp.dot(hid, sw2_ref[...], preferred_element_type=F32)
    base_ref[...] = ALPHA * x1 + gate2 * shared
    _pack_rows(u2_ref, u2)


def _out_projection(att_c, att_l, of_c, of_l, ob_c, ob_l, gg, x_c, x_l, mod3, wo, gn, l1g, l1b, wrt,
                    sw13, sw2, n_seq_tiles):
    n_ctx = x_c.shape[0]
    n = n_ctx + x_l.shape[0]
    tb = TOK_TILE
    n_ctx_tiles = n_ctx // tb

    def mod_idx(i):
        return (jnp.where(i < n_ctx_tiles, 0, 1 + (i - n_ctx_tiles) // n_seq_tiles), 0, 0)

    ctx_blk = lambda i: jnp.minimum(i, n_ctx_tiles - 1)
    lat_blk = lambda i: jnp.maximum(i - n_ctx_tiles, 0)
    tok = lambda w: pl.BlockSpec((tb, w), lambda i: (i, 0))
    tok_c = lambda w: pl.BlockSpec((tb, w), lambda i: (ctx_blk(i), 0))
    tok_l = lambda w: pl.BlockSpec((tb, w), lambda i: (lat_blk(i), 0))
    full = lambda a: pl.BlockSpec(a.shape, lambda i: (0,) * a.ndim)
    return pl.pallas_call(
        functools.partial(_outproj_kernel, n_ctx_tiles=n_ctx_tiles),
        grid=(n // tb,),
        in_specs=[pl.BlockSpec((ATT_WIDTH, tb), lambda i: (0, ctx_blk(i))),
                  pl.BlockSpec((ATT_WIDTH, tb), lambda i: (0, lat_blk(i))),
                  tok_c(GLA_WIDTH), tok_l(GLA_WIDTH), tok_c(GLA_WIDTH), tok_l(GLA_WIDTH),
                  tok(GLA_WIDTH), tok_c(D_MODEL), tok_l(D_MODEL),
                  pl.BlockSpec((None, 1, mod3.shape[2]), mod_idx),
                  full(wo), full(gn), full(l1g), full(l1b), full(wrt), full(sw13), full(sw2)],
        out_specs=(tok(D_MODEL),
                   pl.BlockSpec((tb * PACK_CHUNKS, LANES), lambda i: (i, 0)),
                   pl.BlockSpec((N_EXPERTS, tb), lambda i: (0, i))),
        out_shape=(jax.ShapeDtypeStruct((n, D_MODEL), F32),
                   jax.ShapeDtypeStruct((n * PACK_CHUNKS, LANES), U32),
                   jax.ShapeDtypeStruct((N_EXPERTS, n), F32)),
        compiler_params=_cparams(1), name="out_projection",
    )(att_c, att_l, of_c, of_l, ob_c, ob_l, gg, x_c, x_l, mod3, wo, gn, l1g, l1b, wrt, sw13, sw2)


def _route_kernel(lg_ref, bias_ref, upper_ref, idx_ref, w_ref, pos_ref, cnt_ref, run_ref):
    i = pl.program_id(0)

    @pl.when(i == 0)
    def _():
        run_ref[...] = jnp.zeros_like(run_ref)

    s = jax.nn.sigmoid(lg_ref[...])
    work = s + bias_ref[...]
    rows = lax.broadcasted_iota(I32, s.shape, 0).astype(F32)
    sel = jnp.zeros(s.shape, F32)
    idxs, vals = [], []
    for _ in range(TOP_K):
        mx = jnp.max(work, axis=0, keepdims=True)
        idx = jnp.min(jnp.where(work == mx, rows, float(N_EXPERTS)), axis=0, keepdims=True)
        hit = rows == idx
        vals.append(jnp.sum(jnp.where(hit, s, 0.0), axis=0, keepdims=True))
        idxs.append(idx)
        sel = jnp.where(hit, 1.0, sel)
        work = jnp.where(hit, -jnp.inf, work)
    den = functools.reduce(jnp.add, vals)
    rank = jnp.dot(sel.astype(BF16), upper_ref[...], preferred_element_type=F32) + run_ref[:, 0:1]
    for k in range(TOP_K):
        idx_ref[k:k + 1, :] = idxs[k].astype(I32)
        w_ref[k:k + 1, :] = vals[k] / den * ROUTED_SCALE
        pos_ref[k:k + 1, :] = jnp.sum(jnp.where(rows == idxs[k], rank, 0.0), axis=0,
                                      keepdims=True).astype(I32)
    run_ref[...] = run_ref[...] + jnp.sum(sel, axis=1, keepdims=True)
    cnt_ref[...] = run_ref[...]


def _route(logits_t, bias_col, upper):
    n = logits_t.shape[1]
    tt = TOK_TILE
    row = lambda dt: jax.ShapeDtypeStruct((TOP_K, n), dt)
    blk = pl.BlockSpec((TOP_K, tt), lambda i: (0, i))
    return pl.pallas_call(
        _route_kernel,
        grid=(n // tt,),
        in_specs=[pl.BlockSpec((N_EXPERTS, tt), lambda i: (0, i)),
                  pl.BlockSpec((N_EXPERTS, 1), lambda i: (0, 0)),
                  pl.BlockSpec((tt, tt), lambda i: (0, 0))],
        out_specs=(blk, blk, blk, pl.BlockSpec((N_EXPERTS, LANES), lambda i: (0, 0))),
        out_shape=(row(I32), row(F32), row(I32), jax.ShapeDtypeStruct((N_EXPERTS, LANES), F32)),
        scratch_shapes=[pltpu.VMEM((N_EXPERTS, LANES), F32)],
        compiler_params=_cparams(1), name="route",
    )(logits_t, bias_col, upper)


def _dest_kernel(cnt_ref, lower_ref, idx_ref, pos_ref, dest_ref, bexp_ref, bval_ref, nused_ref):
    cnt = cnt_ref[...]
    nblk = jnp.floor((cnt + (MOE_ROWS - 1)) * (1.0 / MOE_ROWS))
    bstart = jnp.dot(lower_ref[...], nblk, precision=HIGHEST, preferred_element_type=F32)
    bend = bstart + nblk
    pstart = bstart[:, 0:1] * MOE_ROWS
    rows = lax.broadcasted_iota(I32, (N_EXPERTS, idx_ref.shape[1]), 0)
    for k in range(TOP_K):
        hit = rows == idx_ref[k:k + 1, :]
        dest_ref[k:k + 1, :] = (jnp.sum(jnp.where(hit, pstart, 0.0), axis=0, keepdims=True)
                                .astype(I32) + pos_ref[k:k + 1, :])

    @pl.when(pl.program_id(0) == 0)
    def _():
        nb = bexp_ref.shape[1]
        bid = lax.broadcasted_iota(I32, (N_EXPERTS, nb), 1).astype(F32)
        inside = jnp.logical_and(bid >= bstart[:, 0:1], bid < bend[:, 0:1])
        erow = lax.broadcasted_iota(I32, (N_EXPERTS, nb), 0).astype(F32)
        bexp_ref[...] = jnp.sum(jnp.where(inside, erow, 0.0), axis=0, keepdims=True).astype(I32)
        valid = jnp.clip(cnt[:, 0:1] - (bid - bstart[:, 0:1]) * MOE_ROWS, 0.0, float(MOE_ROWS))
        bval_ref[...] = jnp.sum(jnp.where(inside, valid, 0.0), axis=0, keepdims=True).astype(I32)
        nused_ref[...] = jnp.max(bend, axis=0, keepdims=True).astype(I32)


def _destinations(counts, lower, idx_t, pos_t, n_blocks_pad):
    n = idx_t.shape[1]
    tt = TOK_TILE
    blk = pl.BlockSpec((TOP_K, tt), lambda i: (0, i))
    one = lambda w: pl.BlockSpec((1, w), lambda i: (0, 0))
    return pl.pallas_call(
        _dest_kernel,
        grid=(n // tt,),
        in_specs=[pl.BlockSpec((N_EXPERTS, LANES), lambda i: (0, 0)),
                  pl.BlockSpec((N_EXPERTS, N_EXPERTS), lambda i: (0, 0)), blk, blk],
        out_specs=(blk, one(n_blocks_pad), one(n_blocks_pad), one(LANES)),
        out_shape=(jax.ShapeDtypeStruct((TOP_K, n), I32),
                   jax.ShapeDtypeStruct((1, n_blocks_pad), I32),
                   jax.ShapeDtypeStruct((1, n_blocks_pad), I32),
                   jax.ShapeDtypeStruct((1, LANES), I32)),
        compiler_params=_cparams(1), name="destinations",
    )(counts, lower, idx_t, pos_t)


SC_WINDOW = 128
SC_WINDOWS_PER_STEP = 8


def _invert_rows(dest_flat, n_rows):
    m = dest_flat.shape[0]
    mesh = plsc.VectorSubcoreMesh(core_axis_name="core", subcore_axis_name="subcore")

    @functools.partial(pl.kernel, out_type=jax.ShapeDtypeStruct((n_rows,), I32), mesh=mesh,
                       scratch_types=[])
    def invert(val_hbm, idx_hbm, out_hbm):
        def body(val_vmem, idx_vmem):
            for j in range(SC_WINDOWS_PER_STEP):
                pltpu.sync_copy(val_vmem.at[j], out_hbm.at[idx_vmem.at[j]])

        blk = pl.BlockSpec((SC_WINDOWS_PER_STEP, SC_WINDOW), lambda i: (i, 0))
        pltpu.emit_pipeline(
            body, grid=(m // (SC_WINDOW * SC_WINDOWS_PER_STEP),),
            in_specs=[blk, blk], out_specs=[], core_axis_name=("core", "subcore"),
            dimension_semantics=(pltpu.PARALLEL,),
        )(val_hbm, idx_hbm)

    shape = (m // SC_WINDOW, SC_WINDOW)
    return invert(jnp.arange(m, dtype=I32).reshape(shape), dest_flat.reshape(shape))


PACK_CHUNKS = D_MODEL // (2 * LANES)
SRC_GROUP = 4
TOP_K_LOG2 = TOP_K.bit_length() - 1
PACK_CHUNKS_LOG2 = PACK_CHUNKS.bit_length() - 1
U32 = jnp.uint32


def _pack_rows(ref, x, row0=0):
    bits = pltpu.bitcast(x.astype(BF16).astype(F32), U32)
    for s in range(PACK_CHUNKS):
        lo = bits[:, (2 * s) * LANES:(2 * s + 1) * LANES] >> 16
        hi = bits[:, (2 * s + 1) * LANES:(2 * s + 2) * LANES] & jnp.uint32(0xFFFF0000)
        ref[pl.ds(row0 + s, x.shape[0], stride=PACK_CHUNKS), :] = lo | hi


def _unpack_rows(ref, n_rows, row0=0):
    parts = []
    for s in range(PACK_CHUNKS):
        w = ref[pl.ds(row0 + s, n_rows, stride=PACK_CHUNKS), :]
        parts.append(pltpu.bitcast(w << 16, F32))
        parts.append(pltpu.bitcast(w & jnp.uint32(0xFFFF0000), F32))
    return jnp.concatenate(parts, axis=1).astype(BF16)


def _moe_kernel(bexp_ref, bval_ref, nused_ref, u2p_hbm, src_hbm, w1_hbm, w3_hbm, w2_hbm, yt_hbm,
                u2p_vmem, w1_f, w3_f, w2_f, w13_s, w2_s, xbuf, ybuf, src_smem,
                sem_in, sem_src, sem_w, sem_out, *, n_tokens):
    n_used = nused_ref[0]
    br = MOE_ROWS
    grp = SRC_GROUP * br
    trash0 = n_tokens * TOP_K

    def src_copy(g):
        return pltpu.make_async_copy(src_hbm.at[pl.ds(g * grp, grp)],
                                     src_smem.at[pl.ds(lax.rem(g, 2) * grp, grp)], sem_src)

    def out_wait(slot):
        pltpu.make_async_copy(ybuf.at[pl.ds(slot * br * PACK_CHUNKS, br * PACK_CHUNKS)],
                              yt_hbm.at[pl.ds(0, br * PACK_CHUNKS)], sem_out.at[slot]).wait()

    def src_base(blk):
        return lax.rem(blk // SRC_GROUP, 2) * grp + lax.rem(blk, SRC_GROUP) * br

    def scatter_row(blk_slot, sbase, valid, r):
        dst = jnp.where(r < valid, src_smem[sbase + r], trash0 + blk_slot * br + r)
        pltpu.make_async_copy(
            ybuf.at[pl.ds(pl.multiple_of((blk_slot * br + r) * PACK_CHUNKS, PACK_CHUNKS), PACK_CHUNKS)],
            yt_hbm.at[pl.ds(pl.multiple_of(dst * PACK_CHUNKS, PACK_CHUNKS), PACK_CHUNKS)],
            sem_out.at[blk_slot]).start(priority=r % 2 if isinstance(r, int) else 0)

    def gather_row(xslot, sbase, r):
        row = lax.shift_right_logical(src_smem[sbase + r], TOP_K_LOG2 - PACK_CHUNKS_LOG2)
        row = jnp.minimum(row & (-PACK_CHUNKS & 0x7FFFFFFF), (n_tokens - 1) * PACK_CHUNKS)
        dst = pl.multiple_of((xslot * br + r) * PACK_CHUNKS, PACK_CHUNKS)
        xbuf[pl.ds(dst, PACK_CHUNKS), :] = u2p_vmem[pl.ds(pl.multiple_of(row, PACK_CHUNKS), PACK_CHUNKS), :]

    def weight_copies(e, wslot):
        return [pltpu.make_async_copy(src.at[e], dst.at[wslot], sem_w.at[wslot])
                for src, dst in ((w1_hbm, w1_f), (w3_hbm, w3_f), (w2_hbm, w2_f))]

    cp = pltpu.make_async_copy(u2p_hbm, u2p_vmem, sem_in)
    cp.start()
    src_copy(0).start()
    for wcp in weight_copies(bexp_ref[0], 0):
        wcp.start()
    ybuf[...] = jnp.zeros_like(ybuf)
    cp.wait()
    src_copy(0).wait()
    lax.fori_loop(0, br, lambda r, c: (gather_row(0, 0, r), c)[1], 0)

    def block(b, wslot):
        g = b // SRC_GROUP
        phase = lax.rem(b, SRC_GROUP)

        more = (g + 1) * SRC_GROUP < n_used

        @pl.when(jnp.logical_and(phase == 1, more))
        def _():
            src_copy(g + 1).start()

        @pl.when(jnp.logical_and(phase == SRC_GROUP - 1, more))
        def _():
            src_copy(g + 1).wait()

        e = bexp_ref[b]
        prev = bexp_ref[jnp.maximum(b - 1, 0)]

        @pl.when(jnp.logical_or(b == 0, e != prev))
        def _():
            for wcp in weight_copies(e, wslot):
                wcp.wait()
            w13_s[:, 0:EXPERT_FF] = w1_f[wslot].astype(BF16)
            w13_s[:, EXPERT_FF:2 * EXPERT_FF] = w3_f[wslot].astype(BF16)
            w2_s[...] = w2_f[wslot].astype(BF16)

        e_next = bexp_ref[b + 1]
        switch = jnp.logical_and(b + 1 < n_used, e_next != e)

        @pl.when(switch)
        def _():
            for wcp in weight_copies(e_next, 1 - wslot):
                wcp.start()

        valid = bval_ref[b]
        sbase = src_base(b)
        slot = lax.rem(b, 2)

        pb = jnp.maximum(b - 1, 0)
        p_valid = jnp.where(b > 0, bval_ref[pb], 0)
        p_base = src_base(pb)
        n_base = src_base(b + 1)
        for r in range(br):
            gather_row(1 - slot, n_base, r)
            scatter_row(1 - slot, p_base, p_valid, r)

        x = _unpack_rows(xbuf, br, slot * (br * PACK_CHUNKS))
        rows = lax.broadcasted_iota(I32, x.shape, 0)
        x = jnp.where(rows < valid, x, jnp.zeros_like(x))
        ab = jnp.dot(x, w13_s[...], preferred_element_type=F32)
        hid = (_silu(ab[:, 0:EXPERT_FF]) * ab[:, EXPERT_FF:2 * EXPERT_FF]).astype(BF16)
        y = jnp.dot(hid, w2_s[...], preferred_element_type=F32)

        @pl.when(b >= 1)
        def _():
            out_wait(slot)

        _pack_rows(ybuf, y, slot * (br * PACK_CHUNKS))
        return jnp.where(switch, 1 - wslot, wslot)

    lax.fori_loop(0, n_used, block, 0)

    last = n_used - 1
    l_slot = lax.rem(last, 2)
    l_base = src_base(last)
    l_valid = bval_ref[last]
    lax.fori_loop(0, br, lambda r, c: (scatter_row(l_slot, l_base, l_valid, r), c)[1], 0)
    out_wait(1 - l_slot)
    out_wait(l_slot)


def _moe_experts(bexp, bval, nused, u2p, row_src, w1, w3, w2, n_blocks, n_tokens):
    br = MOE_ROWS
    any_spec = pl.BlockSpec(memory_space=pl.ANY)
    grid_spec = pltpu.PrefetchScalarGridSpec(
        num_scalar_prefetch=3, grid=(1,),
        in_specs=[any_spec] * 5,
        out_specs=any_spec,
        scratch_shapes=[pltpu.VMEM(u2p.shape, U32),
                        pltpu.VMEM((2, D_MODEL, EXPERT_FF), F32),
                        pltpu.VMEM((2, D_MODEL, EXPERT_FF), F32),
                        pltpu.VMEM((2, EXPERT_FF, D_MODEL), F32),
                        pltpu.VMEM((D_MODEL, 2 * EXPERT_FF), BF16),
                        pltpu.VMEM((EXPERT_FF, D_MODEL), BF16),
                        pltpu.VMEM((2 * PACK_CHUNKS * br, LANES), U32),
                        pltpu.VMEM((2 * br * PACK_CHUNKS, LANES), U32),
                        pltpu.SMEM((2 * SRC_GROUP * br,), I32),
                        pltpu.SemaphoreType.DMA, pltpu.SemaphoreType.DMA,
                        pltpu.SemaphoreType.DMA((2,)), pltpu.SemaphoreType.DMA((2,))])
    n_out_tiles = n_tokens * TOP_K + 2 * br
    return pl.pallas_call(
        functools.partial(_moe_kernel, n_tokens=n_tokens), grid_spec=grid_spec,
        out_shape=jax.ShapeDtypeStruct((n_out_tiles * PACK_CHUNKS, LANES), U32),
        compiler_params=pltpu.CompilerParams(dimension_semantics=("arbitrary",),
                                             vmem_limit_bytes=MOE_VMEM_LIMIT),
        name="moe_experts",
    )(bexp, bval, nused, u2p, row_src, w1, w3, w2)


def _combine_kernel(w_hbm, yt_ref, base_ref, mod_ref, g_ref, b_ref, yc_ref, yl_ref,
                    w_smem, acc_lo, acc_hi, sem_w, *, n_ctx_tiles):
    i = pl.program_id(0)
    n_steps = pl.num_programs(0)
    n_tok = acc_lo.shape[0] // PACK_CHUNKS
    n_idx = n_tok * TOP_K

    def w_copy(tile):
        return pltpu.make_async_copy(w_hbm.at[pl.ds(tile * n_idx, n_idx)],
                                     w_smem.at[pl.ds(lax.rem(tile, 2) * n_idx, n_idx)], sem_w)

    @pl.when(i == 0)
    def _():
        w_copy(i).start()

    w_copy(i).wait()

    @pl.when(i + 1 < n_steps)
    def _():
        w_copy(i + 1).start()

    wbase = lax.rem(i, 2) * n_idx

    per_tile = SUBLANES // PACK_CHUNKS
    first = lax.broadcasted_iota(I32, (SUBLANES, LANES), 0) < PACK_CHUNKS

    def reduce_token(t, carry):
        lo = hi = None
        for m in range(TOP_K // per_tile):
            j = t * TOP_K + m * per_tile
            words = yt_ref[pl.ds(pl.multiple_of(j * PACK_CHUNKS, SUBLANES), SUBLANES), :]
            wgt = jnp.where(first, w_smem[wbase + j], w_smem[wbase + j + 1])
            t_lo = wgt * pltpu.bitcast(words << 16, F32)
            t_hi = wgt * pltpu.bitcast(words & jnp.uint32(0xFFFF0000), F32)
            lo = t_lo if lo is None else lo + t_lo
            hi = t_hi if hi is None else hi + t_hi
        row = pl.multiple_of(t * PACK_CHUNKS, PACK_CHUNKS)
        acc_lo[pl.ds(row, PACK_CHUNKS), :] = lo[0:PACK_CHUNKS] + lo[PACK_CHUNKS:SUBLANES]
        acc_hi[pl.ds(row, PACK_CHUNKS), :] = hi[0:PACK_CHUNKS] + hi[PACK_CHUNKS:SUBLANES]
        return carry

    lax.fori_loop(0, n_tok, reduce_token, 0)
    parts = []
    for s in range(PACK_CHUNKS):
        parts.append(acc_lo[pl.ds(s, n_tok, stride=PACK_CHUNKS), :])
        parts.append(acc_hi[pl.ds(s, n_tok, stride=PACK_CHUNKS), :])
    moe = jnp.concatenate(parts, axis=1)
    gate2 = mod_ref[:, 5 * D_MODEL:6 * D_MODEL]
    y = _layer_norm(base_ref[...] + gate2 * moe, g_ref[...], b_ref[...])

    @pl.when(i < n_ctx_tiles)
    def _():
        yc_ref[...] = y

    @pl.when(i >= n_ctx_tiles)
    def _():
        yl_ref[...] = y


def _combine(w_flat, yt, base, mod3, l2g, l2b, n_ctx, seq_tokens):
    n = base.shape[0]
    tc = COMB_TILE
    n_ctx_tiles = n_ctx // tc
    n_seq_tiles = seq_tokens // tc

    def mod_idx(i):
        return (jnp.where(i < n_ctx_tiles, 0, 1 + (i - n_ctx_tiles) // n_seq_tiles), 0, 0)

    full = lambda a: pl.BlockSpec(a.shape, lambda i: (0,) * a.ndim)
    return pl.pallas_call(
        functools.partial(_combine_kernel, n_ctx_tiles=n_ctx_tiles),
        grid=(n // tc,),
        in_specs=[pl.BlockSpec(memory_space=pl.ANY),
                  pl.BlockSpec((tc * TOP_K * PACK_CHUNKS, LANES), lambda i: (i, 0)),
                  pl.BlockSpec((tc, D_MODEL), lambda i: (i, 0)),
                  pl.BlockSpec((None, 1, mod3.shape[2]), mod_idx), full(l2g), full(l2b)],
        out_specs=(pl.BlockSpec((tc, D_MODEL), lambda i: (jnp.minimum(i, n_ctx_tiles - 1), 0)),
                   pl.BlockSpec((tc, D_MODEL), lambda i: (jnp.maximum(i - n_ctx_tiles, 0), 0))),
        out_shape=(jax.ShapeDtypeStruct((n_ctx, D_MODEL), F32),
                   jax.ShapeDtypeStruct((n - n_ctx, D_MODEL), F32)),
        scratch_shapes=[pltpu.SMEM((2 * tc * TOP_K,), F32),
                        pltpu.VMEM((tc * PACK_CHUNKS, LANES), F32),
                        pltpu.VMEM((tc * PACK_CHUNKS, LANES), F32),
                        pltpu.SemaphoreType.DMA],
        compiler_params=_cparams(1), name="combine",
    )(w_flat, yt, base, mod3, l2g, l2b)


def _rope_tables(n_tok, tile):
    rows = n_tok // GRID_W
    row_idx = jnp.repeat(jnp.arange(rows, dtype=F32), GRID_W)
    col_idx = jnp.tile(jnp.arange(GRID_W, dtype=F32), rows)
    inv_freq = 1.0 / (ROPE_THETA ** (jnp.arange(0, ROPE_AXIS_DIM, 2, dtype=F32) / ROPE_AXIS_DIM))
    ang_r = row_idx[:, None] * inv_freq[None, :]
    ang_c = col_idx[:, None] * inv_freq[None, :]
    ang = jnp.concatenate([ang_r, ang_r, ang_c, ang_c], axis=-1)
    cos, sin = jnp.cos(ang), jnp.sin(ang)
    quarter = (jnp.arange(HEAD_DIM) // (ROPE_AXIS_DIM // 2)) % 2
    sin_a = jnp.where(quarter == 0, -sin, 0.0)
    sin_b = jnp.where(quarter == 1, sin, 0.0)
    rep = LANES // HEAD_DIM
    ident = lambda v: jnp.full((tile, LANES), v, F32)
    cos_t = jnp.concatenate([jnp.tile(cos, (1, rep)), ident(1.0)], axis=0)
    sa_t = jnp.concatenate([jnp.tile(sin_a, (1, rep)), ident(0.0)], axis=0)
    sb_t = jnp.concatenate([jnp.tile(sin_b, (1, rep)), ident(0.0)], axis=0)
    ident_tr = lambda v: jnp.full((HEAD_DIM, tile), v, F32)
    cos_tr = jnp.concatenate([cos.T, ident_tr(1.0)], axis=1)
    sin_tr = jnp.concatenate([sin.T, ident_tr(0.0)], axis=1)
    return cos_t, sa_t, sb_t, cos_tr, sin_tr


def _dup_heads(a):
    parts = []
    for h in range(KV_HEADS):
        blk = a[..., h * HEAD_DIM:(h + 1) * HEAD_DIM]
        parts += [blk] * (LANES // HEAD_DIM)
    return jnp.concatenate(parts, axis=-1)


def kernel(x_prompt, x_sample, cache_k, cache_v, state_gla_fwd, state_gla_bwd, c, c_ctx, w_ada, b_ada, w_in, q_norm, k_norm, gla_wa_fwd, gla_ba_fwd, gla_wa_bwd, gla_ba_bwd, gla_norm, w_out, ln1_g, ln1_b, ln2_g, ln2_b, w_router, router_bias, exp_w1, exp_w3, exp_w2, sh_w1, sh_w3, sh_w2):
    n_ctx_b, ctx_seq, _ = x_prompt.shape
    n_lat_b, lat_seq, _ = x_sample.shape
    n_ctx = n_ctx_b * ctx_seq
    n_lat = n_lat_b * lat_seq
    n = n_ctx + n_lat
    l = 0

    x_c = x_prompt.reshape(n_ctx, D_MODEL)
    x_l = x_sample.reshape(n_lat, D_MODEL)

    c_rows = jnp.zeros((SUBLANES, D_MODEL), F32).at[0].set(c_ctx).at[1:1 + n_lat_b].set(c)
    mod = _modulation(c_rows, w_ada[l], b_ada[l][None, :])
    mod3 = mod.reshape(SUBLANES, 1, 6 * D_MODEL)

    wi = w_in[l]
    o_q, o_k, o_v, o_gq, o_gk, o_gv, o_gg, o_rf, o_rb, o_end = np.cumsum(
        [0, ATT_WIDTH, KV_HEADS * HEAD_DIM, KV_HEADS * HEAD_DIM, GLA_KW, GLA_KW, GLA_WIDTH, GLA_WIDTH,
         GLA_GATE_RANK, GLA_GATE_RANK])
    w_tok = jnp.concatenate([
        _dup_heads(wi[:, o_k:o_v]), wi[:, o_v:o_gq], wi[:, o_gq:o_gk],
        wi[:, o_gv:o_gg], wi[:, o_gg:o_rf], wi[:, o_rf:o_end],
        jnp.zeros((D_MODEL, LANES - 2 * GLA_GATE_RANK), F32)], axis=1).astype(BF16)
    w_tr = jnp.concatenate([wi[:, o_q:o_k], wi[:, o_v:o_gq], wi[:, o_gk:o_gv], wi[:, o_rf:o_end]],
                           axis=1).T.astype(BF16)
    rep = LANES // HEAD_DIM
    qn = q_norm[l][:, None]
    kn = jnp.tile(k_norm[l], rep)[None, :]
    seg = jnp.asarray(np.kron(np.eye(rep), np.ones((HEAD_DIM, HEAD_DIM))), BF16)
    wa = jnp.zeros((LANES, 2 * GLA_KW), F32)
    wa = wa.at[0:GLA_GATE_RANK, 0:GLA_KW].set(gla_wa_fwd[l])
    wa = wa.at[GLA_GATE_RANK:2 * GLA_GATE_RANK, GLA_KW:].set(gla_wa_bwd[l])
    ba = jnp.concatenate([gla_ba_fwd[l], gla_ba_bwd[l]])[None, :]
    wat = wa[0:2 * GLA_GATE_RANK, :].T
    bat = ba.T
    cos_t, sa_t, sb_t, cos_tr, sin_tr = _rope_tables(lat_seq, TOK_TILE)

    (qt, k_dup, vt, k32, v32, gq, gv, gg, la, gkt, lat) = _in_projection(
        x_c, x_l, mod3, w_tok, w_tr, qn, kn, cos_t, sa_t, sb_t, cos_tr, sin_tr, seg, wa, ba, wat, bat,
        lat_seq // TOK_TILE)

    ck = _dup_heads(cache_k[:, l].reshape(n_lat_b, -1, KV_HEADS * HEAD_DIM)).astype(BF16)
    cvt = cache_v[:, l].reshape(n_lat_b, -1, KV_HEADS * HEAD_DIM).transpose(0, 2, 1).astype(BF16)
    att_c = _attention(qt, k_dup, vt, None, 0, n_ctx_b, ctx_seq)
    att_l = _attention(qt, k_dup, vt, (ck, cvt), n_ctx, n_lat_b, lat_seq)

    gconst, levels_of = _gla_constants()
    to_dev = lambda t: (jnp.asarray(t[0], BF16), jnp.asarray(t[1], BF16), jnp.asarray(t[2], F32))
    bd = jnp.asarray(np.kron(np.eye(GLA_HEADS), np.ones((GLA_DK, GLA_DV))), BF16)
    vbd = jnp.asarray(np.kron(np.eye(GLA_HEADS), np.ones((GLA_CHUNK, GLA_DV))), BF16)
    consts = ((to_dev(gconst["f"]), to_dev(gconst["b"])), levels_of, bd, vbd)
    s_zero = jnp.zeros((n_ctx_b, GLA_HEADS, GLA_DK, GLA_DV), F32)
    of_c, ob_c, sf_new, sb_new = _gla(gq, la, gkt, lat, gv, s_zero, s_zero, consts, 0, n_ctx_b, ctx_seq)
    of_l, ob_l, _, _ = _gla(gq, la, gkt, lat, gv, state_gla_fwd[:, l], state_gla_bwd[:, l], consts,
                            n_ctx, n_lat_b, lat_seq)

    sw13 = jnp.concatenate([sh_w1[l], sh_w3[l]], axis=1).astype(BF16)
    base, u2_rows, logits_t = _out_projection(
        att_c, att_l, of_c, of_l, ob_c, ob_l, gg, x_c, x_l, mod3, w_out[l].astype(BF16),
        gla_norm[l][None, :], ln1_g[l][None, :], ln1_b[l][None, :], w_router[l].T.astype(BF16), sw13,
        sh_w2[l].astype(BF16), lat_seq // TOK_TILE)

    upper = jnp.asarray(np.triu(np.ones((TOK_TILE, TOK_TILE)), 1), BF16)
    idx_t, w_t, pos_t, counts = _route(logits_t, router_bias[l][:, None], upper)
    n_blocks = n * TOP_K // MOE_ROWS + N_EXPERTS
    n_blocks_pad = -(-n_blocks // LANES) * LANES
    lower = jnp.asarray(np.tril(np.ones((N_EXPERTS, N_EXPERTS)), -1), F32)
    dest_t, bexp, bval, nused = _destinations(counts, lower, idx_t, pos_t, n_blocks_pad)
    dest_flat = dest_t.T.reshape(-1)
    w_flat = w_t.T.reshape(-1)

    row_src = _invert_rows(dest_flat, n_blocks * MOE_ROWS)
    yt = _moe_experts(bexp.reshape(-1), bval.reshape(-1), nused.reshape(-1)[0:1], u2_rows, row_src,
                      exp_w1[l], exp_w3[l], exp_w2[l], n_blocks, n)
    y_c, y_l = _combine(w_flat, yt, base, mod3, ln2_g[l][None, :], ln2_b[l][None, :], n_ctx, lat_seq)

    y_prompt = y_c.reshape(n_ctx_b, ctx_seq, D_MODEL)
    y_sample = y_l.reshape(n_lat_b, lat_seq, D_MODEL)
    new_cache_k = k32.reshape(n_ctx_b, 1, ctx_seq, KV_HEADS, HEAD_DIM)
    new_cache_v = v32.reshape(n_ctx_b, 1, ctx_seq, KV_HEADS, HEAD_DIM)
    return (y_prompt, y_sample, new_cache_k, new_cache_v, sf_new[:, None], sb_new[:, None])
```

```python
import functools

import numpy as np
import jax
import jax.numpy as jnp
from jax import lax
from jax.experimental import pallas as pl
from jax.experimental.pallas import tpu as pltpu
from jax.experimental.pallas import tpu_sc as plsc

F32 = jnp.float32
BF16 = jnp.bfloat16
I32 = jnp.int32

D_MODEL = 1024
GRID_W = 64
HEAD_DIM = 64
N_HEADS = 8
KV_HEADS = 2
ATT_WIDTH = N_HEADS * HEAD_DIM
ATT_SCALE = HEAD_DIM ** -0.5
LOG2_E = 1.4426950408889634
ROPE_AXIS_DIM = HEAD_DIM // 2
ROPE_THETA = 10000.0
GLA_HEADS = 4
GLA_DK = 64
GLA_DV = 128
GLA_WIDTH = GLA_HEADS * GLA_DV
GLA_KW = GLA_HEADS * GLA_DK
GLA_GATE_RANK = 16
GLA_TAU = 16.0
N_EXPERTS = 256
TOP_K = 8
EXPERT_FF = 256
SHARED_FF = 256
ROUTED_SCALE = 2.5
DEPTH = 1
ALPHA = (2.0 * DEPTH) ** 0.25
EPS = 1e-6

LANES = 128
SUBLANES = 8
ROW_CHUNKS = D_MODEL // LANES
VMEM_LIMIT = 56 * 1024 * 1024

TOK_TILE = 512
ATT_TQ = 128
GLA_CHUNK = 128
GLA_LEVELS = ((32, 128), (8, 32), (2, 8), (1, 2))
MOE_ROWS = 256
MOE_VMEM_LIMIT = 62 * 1024 * 1024
COMB_TILE = 128
HIGHEST = lax.Precision.HIGHEST


def _cparams(n_axes):
    return pltpu.CompilerParams(dimension_semantics=("arbitrary",) * n_axes,
                                vmem_limit_bytes=VMEM_LIMIT)


def _silu(x):
    return x * jax.nn.sigmoid(x)


def _log_sigmoid(x):
    return jnp.minimum(x, 0.0) - jnp.log(1.0 + jnp.exp(-jnp.abs(x)))


def _load_row_tiles(ref, n_rows, row0=0):
    return jnp.concatenate(
        [ref[pl.ds(row0 * ROW_CHUNKS + cidx, n_rows, stride=ROW_CHUNKS), :] for cidx in range(ROW_CHUNKS)],
        axis=1)


def _store_row_tiles(ref, x):
    for cidx in range(ROW_CHUNKS):
        ref[pl.ds(cidx, x.shape[0], stride=ROW_CHUNKS), :] = x[:, cidx * LANES:(cidx + 1) * LANES]


def _layer_norm(z, g, b):
    mu = jnp.mean(z, axis=-1, keepdims=True)
    zc = z - mu
    var = jnp.mean(zc * zc, axis=-1, keepdims=True)
    return zc * lax.rsqrt(var + EPS) * g + b


def _mod_kernel(c_ref, w_ref, b_ref, o_ref):
    s = _silu(c_ref[...]).astype(BF16)
    o_ref[...] = jnp.dot(s, w_ref[...].astype(BF16), preferred_element_type=F32) + b_ref[...]


def _modulation(c_rows, w_ada, b_ada):
    n_cols = w_ada.shape[1]
    tn = 512
    return pl.pallas_call(
        _mod_kernel,
        grid=(n_cols // tn,),
        in_specs=[pl.BlockSpec((SUBLANES, D_MODEL), lambda j: (0, 0)),
                  pl.BlockSpec((D_MODEL, tn), lambda j: (0, j)),
                  pl.BlockSpec((1, tn), lambda j: (0, j))],
        out_specs=pl.BlockSpec((SUBLANES, tn), lambda j: (0, j)),
        out_shape=jax.ShapeDtypeStruct((SUBLANES, n_cols), F32),
        compiler_params=_cparams(1),
        name="modulation",
    )(c_rows, w_ada, b_ada)


_C_K = 0
_C_V = _C_K + 2 * LANES
_C_GQ = _C_V + KV_HEADS * HEAD_DIM
_C_GV = _C_GQ + GLA_KW
_C_GG = _C_GV + GLA_WIDTH
_C_RA = _C_GG + GLA_WIDTH
_C_END = _C_RA + LANES
_R_Q = 0
_R_V = _R_Q + ATT_WIDTH
_R_GK = _R_V + KV_HEADS * HEAD_DIM
_R_RA = _R_GK + GLA_KW
_R_END = _R_RA + 2 * GLA_GATE_RANK


def _inproj_kernel(xc_ref, xl_ref, mod_ref, w_ref, wt_ref, qn_ref, kn_ref, cos_ref, sa_ref, sb_ref,
                   cost_ref, sint_ref, seg_ref, wa_ref, ba_ref, wat_ref, bat_ref,
                   qt_ref, k_ref, vt_ref, k32_ref, v32_ref, gq_ref, gv_ref, gg_ref,
                   la_ref, gkt_ref, lat_ref, *, n_ctx_tiles):
    i = pl.program_id(0)
    m = mod_ref[...]
    shift1 = m[:, 0:D_MODEL]
    scale1 = m[:, D_MODEL:2 * D_MODEL]
    x = jnp.where(i < n_ctx_tiles, xc_ref[...], xl_ref[...])
    u = (x * (1.0 + scale1) + shift1).astype(BF16)

    cos = cos_ref[...]
    sin_a = sa_ref[...]
    sin_b = sb_ref[...]
    seg = seg_ref[...]
    lane = lax.broadcasted_iota(I32, (u.shape[0], LANES), 1)
    low = lane < HEAD_DIM

    def proj(c0, c1):
        return jnp.dot(u, w_ref[:, c0:c1], preferred_element_type=F32)

    def head_norm(blk, gain):
        ss = jnp.dot((blk * blk).astype(BF16), seg, preferred_element_type=F32) * (1.0 / HEAD_DIM)
        return blk * lax.rsqrt(ss + EPS) * gain

    def rope(blk):
        return (blk * cos + pltpu.roll(blk, LANES - ROPE_AXIS_DIM // 2, 1) * sin_a
                + pltpu.roll(blk, ROPE_AXIS_DIM // 2, 1) * sin_b)

    pk = proj(_C_K, _C_V)
    kn = [head_norm(pk[:, j * LANES:(j + 1) * LANES], kn_ref[...]) for j in range(KV_HEADS)]
    for j in range(KV_HEADS):
        k_ref[:, j * LANES:(j + 1) * LANES] = rope(kn[j]).astype(BF16)

    @pl.when(i < n_ctx_tiles)
    def _():
        k32_ref[...] = jnp.where(low, kn[0], kn[1])
        v32_ref[...] = proj(_C_V, _C_GQ)

    gq_ref[...] = proj(_C_GQ, _C_GV) * (GLA_DK ** -0.5)
    gv_ref[...] = proj(_C_GV, _C_GG).astype(BF16)
    gg_ref[...] = proj(_C_GG, _C_RA).astype(BF16)

    ra = proj(_C_RA, _C_END)
    pre = jnp.dot(ra, wa_ref[...], precision=HIGHEST, preferred_element_type=F32) + ba_ref[...]
    la_ref[...] = _log_sigmoid(pre) * (1.0 / GLA_TAU)

    pt = lax.dot_general(wt_ref[...], u, (((1,), (1,)), ((), ())), preferred_element_type=F32)
    cos_t = cost_ref[...]
    sin_t = sint_ref[...]
    quarter = ROPE_AXIS_DIM // 2
    for h in range(N_HEADS):
        blk = pt[_R_Q + h * HEAD_DIM:_R_Q + (h + 1) * HEAD_DIM, :]
        ms = jnp.mean(blk * blk, axis=0, keepdims=True)
        qn = blk * lax.rsqrt(ms + EPS) * qn_ref[...]
        rot = jnp.concatenate([-qn[quarter:2 * quarter], qn[0:quarter],
                               -qn[3 * quarter:4 * quarter], qn[2 * quarter:3 * quarter]], axis=0)
        qt_ref[h * HEAD_DIM:(h + 1) * HEAD_DIM, :] = (
            (qn * cos_t + rot * sin_t) * (ATT_SCALE * LOG2_E)).astype(BF16)
    vt_ref[...] = pt[_R_V:_R_GK, :].astype(BF16)
    gkt_ref[...] = pt[_R_GK:_R_RA, :]
    rat = pt[_R_RA:_R_END, :]
    pre_t = jnp.dot(wat_ref[...], rat, precision=HIGHEST, preferred_element_type=F32) + bat_ref[...]
    lat_ref[...] = _log_sigmoid(pre_t) * (1.0 / GLA_TAU)


def _in_projection(x_c, x_l, mod3, w_tok, w_tr, qn, kn, cos_t, sa_t, sb_t, cos_tr, sin_tr, seg, wa, ba,
                   wat, bat, n_seq_tiles):
    n_ctx = x_c.shape[0]
    n = n_ctx + x_l.shape[0]
    tb = TOK_TILE
    n_ctx_tiles = n_ctx // tb
    n_tiles = n // tb
    n_rope_blocks = cos_t.shape[0] // tb - 1

    def mod_idx(i):
        return (jnp.where(i < n_ctx_tiles, 0, 1 + (i - n_ctx_tiles) // n_seq_tiles), 0, 0)

    def rope_blk(i):
        return jnp.where(i < n_ctx_tiles, n_rope_blocks, (i - n_ctx_tiles) % n_seq_tiles)

    def rope_idx(i):
        return (rope_blk(i), 0)

    def ctx_idx(i):
        return (jnp.minimum(i, n_ctx_tiles - 1), 0)

    tok = lambda w: pl.BlockSpec((tb, w), lambda i: (i, 0))
    full = lambda a: pl.BlockSpec(a.shape, lambda i: (0,) * a.ndim)
    tr = lambda r: pl.BlockSpec((r, tb), lambda i: (0, i))
    rope_tr = pl.BlockSpec((HEAD_DIM, tb), lambda i: (0, rope_blk(i)))
    out_shapes = (
        jax.ShapeDtypeStruct((ATT_WIDTH, n), BF16),
        jax.ShapeDtypeStruct((n, 2 * LANES), BF16),
        jax.ShapeDtypeStruct((KV_HEADS * HEAD_DIM, n), BF16),
        jax.ShapeDtypeStruct((n_ctx, LANES), F32),
        jax.ShapeDtypeStruct((n_ctx, LANES), F32),
        jax.ShapeDtypeStruct((n, GLA_KW), F32),
        jax.ShapeDtypeStruct((n, GLA_WIDTH), BF16),
        jax.ShapeDtypeStruct((n, GLA_WIDTH), BF16),
        jax.ShapeDtypeStruct((n, 2 * GLA_KW), F32),
        jax.ShapeDtypeStruct((GLA_KW, n), F32),
        jax.ShapeDtypeStruct((2 * GLA_KW, n), F32),
    )
    out_specs = (tr(ATT_WIDTH), tok(2 * LANES), tr(KV_HEADS * HEAD_DIM),
                 pl.BlockSpec((tb, LANES), ctx_idx), pl.BlockSpec((tb, LANES), ctx_idx),
                 tok(GLA_KW), tok(GLA_WIDTH), tok(GLA_WIDTH), tok(2 * GLA_KW),
                 tr(GLA_KW), tr(2 * GLA_KW))
    in_specs = [pl.BlockSpec((tb, D_MODEL), ctx_idx),
                pl.BlockSpec((tb, D_MODEL), lambda i: (jnp.maximum(i - n_ctx_tiles, 0), 0)),
                pl.BlockSpec((None, 1, mod3.shape[2]), mod_idx),
                full(w_tok), full(w_tr), full(qn), full(kn),
                pl.BlockSpec((tb, LANES), rope_idx), pl.BlockSpec((tb, LANES), rope_idx),
                pl.BlockSpec((tb, LANES), rope_idx), rope_tr, rope_tr,
                full(seg), full(wa), full(ba), full(wat), full(bat)]
    return pl.pallas_call(
        functools.partial(_inproj_kernel, n_ctx_tiles=n_ctx_tiles),
        grid=(n_tiles,), in_specs=in_specs, out_specs=out_specs, out_shape=out_shapes,
        compiler_params=_cparams(1), name="in_projection",
    )(x_c, x_l, mod3, w_tok, w_tr, qn, kn, cos_t, sa_t, sb_t, cos_tr, sin_tr, seg, wa, ba, wat, bat)


def _attention_kernel(*refs, n_kv_parts):
    qt_ref = refs[0]
    k_refs = refs[1:1 + n_kv_parts]
    vt_refs = refs[1 + n_kv_parts:1 + 2 * n_kv_parts]
    o_ref = refs[1 + 2 * n_kv_parts]
    tq = qt_ref.shape[1]
    group = N_HEADS // KV_HEADS
    scores = []
    for kv in range(KV_HEADS):
        heads = range(kv * group, (kv + 1) * group)
        q_grp = jnp.concatenate([qt_ref[h * HEAD_DIM:(h + 1) * HEAD_DIM, :] for h in heads], axis=1)
        rhs = jnp.concatenate([q_grp, jnp.zeros_like(q_grp)], axis=0)
        scores.append([jnp.dot(k[:, kv * LANES:(kv + 1) * LANES], rhs, preferred_element_type=F32)
                       for k in k_refs])
    for kv in range(KV_HEADS):
        heads = range(kv * group, (kv + 1) * group)
        s = scores[kv]
        mx = functools.reduce(jnp.maximum, [jnp.max(x, axis=0, keepdims=True) for x in s])
        pr = [jnp.exp2(x - mx) for x in s]
        den = functools.reduce(jnp.add, [jnp.sum(x, axis=0, keepdims=True) for x in pr])
        acc = functools.reduce(jnp.add, [
            jnp.dot(vt[kv * HEAD_DIM:(kv + 1) * HEAD_DIM, :], x.astype(BF16),
                    preferred_element_type=F32) for x, vt in zip(pr, vt_refs)])
        out = (acc / den).astype(BF16)
        for j, h in enumerate(heads):
            o_ref[h * HEAD_DIM:(h + 1) * HEAD_DIM, :] = out[:, j * tq:(j + 1) * tq]


def _attention(qt, k, vt, extra_kv, row0, n_batch, seq):
    tq = ATT_TQ
    n_q = seq // tq
    q_blk0 = row0 // tq
    kv_blk0 = row0 // seq
    in_specs = [pl.BlockSpec((ATT_WIDTH, tq), lambda b, i: (0, q_blk0 + b * n_q + i))]
    k_spec = pl.BlockSpec((seq, 2 * LANES), lambda b, i: (kv_blk0 + b, 0))
    vt_spec = pl.BlockSpec((KV_HEADS * HEAD_DIM, seq), lambda b, i: (0, kv_blk0 + b))
    args_k, args_v, specs_k, specs_v = [k], [vt], [k_spec], [vt_spec]
    if extra_kv is not None:
        ck, cvt = extra_kv
        args_k.append(ck)
        args_v.append(cvt)
        specs_k.append(pl.BlockSpec((None, ck.shape[1], 2 * LANES), lambda b, i: (b, 0, 0)))
        specs_v.append(pl.BlockSpec((None, KV_HEADS * HEAD_DIM, cvt.shape[2]), lambda b, i: (b, 0, 0)))
    return pl.pallas_call(
        functools.partial(_attention_kernel, n_kv_parts=len(args_k)),
        grid=(n_batch, n_q),
        in_specs=in_specs + specs_k + specs_v,
        out_specs=pl.BlockSpec((ATT_WIDTH, tq), lambda b, i: (0, b * n_q + i)),
        out_shape=jax.ShapeDtypeStruct((ATT_WIDTH, n_batch * seq), BF16),
        compiler_params=_cparams(2), name="attention",
    )(qt, *args_k, *args_v)


def _gla_constants():
    c = GLA_CHUNK
    idx = np.arange(c)
    q_mats, k_mats, masks, levels_of = [], [], [], []
    for li, (s, p) in enumerate(GLA_LEVELS):
        start = (idx // s) * s
        end = start + s - 1
        k_mats.append(((idx[None, :] > idx[:, None]) & (idx[None, :] <= end[:, None])))
        for d in range(p // s - 1):
            lo = np.maximum(start - d * s, 0)
            q_mats.append((idx[None, :] >= lo[:, None]) & (idx[None, :] <= idx[:, None]))
            masks.append((idx[:, None] // p == idx[None, :] // p)
                         & (idx[:, None] // s - idx[None, :] // s - 1 == d))
            levels_of.append(li)
    masks.append(np.eye(c, dtype=bool))
    levels_of.append(len(GLA_LEVELS) - 1)
    q_mats.append(idx[None, :] <= idx[:, None])
    k_mats = k_mats[:-1]
    k_mats.append(idx[None, :] > idx[:, None])
    k_mats.append(np.ones((c, c), bool))
    out = {}
    for name, flip in (("f", False), ("b", True)):
        f = (lambda a: a[::-1, ::-1]) if flip else (lambda a: a)
        lq = np.concatenate([f(a) for a in q_mats], axis=0).astype(np.float32)
        lkt = np.concatenate([f(a).T for a in k_mats], axis=1).astype(np.float32)
        mk = np.stack([np.tile(f(a), (1, GLA_HEADS)) for a in masks]).astype(np.float32)
        out[name] = (np.concatenate([lq, lq], axis=1), np.concatenate([lkt, lkt], axis=0), mk)
    return out, tuple(levels_of)


def _gla_direction(q, g, gkt, gt, v, lq2, lkt2, masks_ref, bd, vbd, s_ref, levels_of):
    c = GLA_CHUNK
    n_var = len(levels_of)
    n_lev = len(GLA_LEVELS)
    g_hi = g.astype(BF16)
    g_lo = (g - g_hi.astype(F32)).astype(BF16)
    fq = jnp.dot(lq2, jnp.concatenate([g_hi, g_lo], axis=0), preferred_element_type=F32)
    gt_hi = gt.astype(BF16)
    gt_lo = (gt - gt_hi.astype(F32)).astype(BF16)
    fk = jnp.dot(jnp.concatenate([gt_hi, gt_lo], axis=1), lkt2, preferred_element_type=F32)

    def key_factor(f):
        return gkt * jnp.exp(fk[:, f * c:(f + 1) * c])

    q_var = [(q * jnp.exp(fq[vi * c:(vi + 1) * c, :])).astype(BF16) for vi in range(n_var - 1)]
    q_var.append(q.astype(BF16))
    a = jnp.zeros((c, GLA_HEADS * c), F32)
    for li in range(n_lev):
        kt = (key_factor(li) if li < n_lev - 1 else gkt).astype(BF16)
        xt = jnp.concatenate([kt] * GLA_HEADS, axis=1) * bd
        vis = [vi for vi in range(n_var) if levels_of[vi] == li]
        res = jnp.dot(jnp.concatenate([q_var[vi] for vi in vis], axis=0), xt,
                      preferred_element_type=F32)
        for r, vi in enumerate(vis):
            a = a + masks_ref[vi] * res[r * c:(r + 1) * c, :]
    q_in = (q * jnp.exp(fq[(n_var - 1) * c:n_var * c, :])).astype(BF16)
    state = s_ref[...]
    v_bd = jnp.concatenate([v] * GLA_HEADS, axis=0) * vbd
    o = (jnp.dot(q_in, state.astype(BF16), preferred_element_type=F32)
         + jnp.dot(a.astype(BF16), v_bd, preferred_element_type=F32))
    k_out = key_factor(n_lev - 1).astype(BF16)
    e_tot = jnp.exp(fk[:, n_lev * c:(n_lev + 1) * c])
    upd = jnp.dot(k_out, v, preferred_element_type=F32)
    s_ref[...] = (state * jnp.concatenate([e_tot] * (GLA_WIDTH // c), axis=1)
                  + upd * bd.astype(F32))
    return o


def _gla_kernel(gq_f, la_f, gkt_f, lat_f, gv_f, gq_b, la_b, gkt_b, lat_b, gv_b,
                s0f_ref, s0b_ref, lq2f, lkt2f, mkf, lq2b, lkt2b, mkb, bd_ref, vbd_ref,
                of_ref, ob_ref, sf_ref, sb_ref, st_f, st_b, *, levels_of):
    n = pl.program_id(1)

    @pl.when(n == 0)
    def _():
        st_f[...] = jnp.zeros_like(st_f)
        st_b[...] = jnp.zeros_like(st_b)
        for h in range(GLA_HEADS):
            rows = slice(h * GLA_DK, (h + 1) * GLA_DK)
            cols = slice(h * GLA_DV, (h + 1) * GLA_DV)
            st_f[rows, cols] = s0f_ref[h]
            st_b[rows, cols] = s0b_ref[h]

    bd = bd_ref[...]
    vbd = vbd_ref[...]
    of_ref[...] = _gla_direction(gq_f[...], la_f[...], gkt_f[...], lat_f[...], gv_f[...],
                                 lq2f[...], lkt2f[...], mkf, bd, vbd, st_f, levels_of)
    ob_ref[...] = _gla_direction(gq_b[...], la_b[...], gkt_b[...], lat_b[...], gv_b[...],
                                 lq2b[...], lkt2b[...], mkb, bd, vbd, st_b, levels_of)

    @pl.when(n == pl.num_programs(1) - 1)
    def _():
        for h in range(GLA_HEADS):
            rows = slice(h * GLA_DK, (h + 1) * GLA_DK)
            cols = slice(h * GLA_DV, (h + 1) * GLA_DV)
            sf_ref[h] = st_f[rows, cols]
            sb_ref[h] = st_b[rows, cols]


def _gla(gq, la, gkt, lat, gv, s0f, s0b, consts, row0, n_batch, seq):
    (cf, cb), levels_of, bd, vbd = consts
    c = GLA_CHUNK
    nc = seq // c
    blk0 = row0 // c
    n_la_blocks_b = 1
    fwd = lambda b, n: blk0 + b * nc + n
    bwd = lambda b, n: blk0 + b * nc + (nc - 1 - n)

    def tok(w, which, col=0):
        return pl.BlockSpec((c, w), lambda b, n: (which(b, n), col))

    def tr(r, which, row=0):
        return pl.BlockSpec((r, c), lambda b, n: (row, which(b, n)))

    full = lambda a: pl.BlockSpec(a.shape, lambda b, n: (0,) * a.ndim)
    st_spec = pl.BlockSpec((None, GLA_HEADS, GLA_DK, GLA_DV), lambda b, n: (b, 0, 0, 0))
    in_specs = [tok(GLA_KW, fwd), tok(GLA_KW, fwd, 0), tr(GLA_KW, fwd), tr(GLA_KW, fwd, 0),
                tok(GLA_WIDTH, fwd),
                tok(GLA_KW, bwd), tok(GLA_KW, bwd, n_la_blocks_b), tr(GLA_KW, bwd),
                tr(GLA_KW, bwd, 1), tok(GLA_WIDTH, bwd),
                st_spec, st_spec,
                full(cf[0]), full(cf[1]), full(cf[2]), full(cb[0]), full(cb[1]), full(cb[2]),
                full(bd), full(vbd)]
    out_specs = (pl.BlockSpec((c, GLA_WIDTH), lambda b, n: (b * nc + n, 0)),
                 pl.BlockSpec((c, GLA_WIDTH), lambda b, n: (b * nc + (nc - 1 - n), 0)),
                 st_spec, st_spec)
    out_shape = (jax.ShapeDtypeStruct((n_batch * seq, GLA_WIDTH), F32),
                 jax.ShapeDtypeStruct((n_batch * seq, GLA_WIDTH), F32),
                 jax.ShapeDtypeStruct((n_batch, GLA_HEADS, GLA_DK, GLA_DV), F32),
                 jax.ShapeDtypeStruct((n_batch, GLA_HEADS, GLA_DK, GLA_DV), F32))
    return pl.pallas_call(
        functools.partial(_gla_kernel, levels_of=levels_of),
        grid=(n_batch, nc), in_specs=in_specs, out_specs=out_specs, out_shape=out_shape,
        scratch_shapes=[pltpu.VMEM((GLA_KW, GLA_WIDTH), F32), pltpu.VMEM((GLA_KW, GLA_WIDTH), F32)],
        compiler_params=_cparams(2), name="gla",
    )(gq, la, gkt, lat, gv, gq, la, gkt, lat, gv, s0f, s0b,
      cf[0], cf[1], cf[2], cb[0], cb[1], cb[2], bd, vbd)


def _outproj_kernel(attc_ref, attl_ref, ofc_ref, ofl_ref, obc_ref, obl_ref, gg_ref, xc_ref, xl_ref,
                    mod_ref, wo_ref, gn_ref, l1g_ref, l1b_ref, wrt_ref, sw13_ref, sw2_ref,
                    base_ref, u2_ref, lg_ref, *, n_ctx_tiles):
    is_ctx = pl.program_id(0) < n_ctx_tiles
    pick = lambda a_ref, b_ref: jnp.where(is_ctx, a_ref[...], b_ref[...])
    m = mod_ref[...]
    gate1 = m[:, 2 * D_MODEL:3 * D_MODEL]
    shift2 = m[:, 3 * D_MODEL:4 * D_MODEL]
    scale2 = m[:, 4 * D_MODEL:5 * D_MODEL]
    gate2 = m[:, 5 * D_MODEL:6 * D_MODEL]
    og = pick(ofc_ref, ofl_ref) + pick(obc_ref, obl_ref)
    gg = gg_ref[...].astype(F32)
    parts = []
    for h in range(GLA_HEADS):
        blk = og[:, h * GLA_DV:(h + 1) * GLA_DV]
        ms = jnp.mean(blk * blk, axis=-1, keepdims=True)
        nb = blk * lax.rsqrt(ms + EPS) * gn_ref[...]
        parts.append((nb * _silu(gg[:, h * GLA_DV:(h + 1) * GLA_DV])).astype(BF16))
    att_t = pick(attc_ref, attl_ref)
    hmix = (lax.dot_general(att_t, wo_ref[0:ATT_WIDTH, :], (((0,), (0,)), ((), ())),
                            preferred_element_type=F32)
            + jnp.dot(jnp.concatenate(parts, axis=1), wo_ref[ATT_WIDTH:, :],
                      preferred_element_type=F32))
    x1 = _layer_norm(ALPHA * pick(xc_ref, xl_ref) + gate1 * hmix, l1g_ref[...], l1b_ref[...])
    u2 = x1 * (1.0 + scale2) + shift2
    u2b = u2.astype(BF16)
    lg_ref[...] = lax.dot_general(wrt_ref[...], u2b, (((1,), (1,)), ((), ())),
                                  preferred_element_type=F32)
    ab = jnp.dot(u2b, sw13_ref[...], preferred_element_type=F32)
    hid = (_silu(ab[:, 0:SHARED_FF]) * ab[:, SHARED_FF:2 * SHARED_FF]).astype(BF16)
    shared = jnp.dot(hid, sw2_ref[...], preferred_element_type=F32)
    base_ref[...] = ALPHA * x1 + gate2 * shared
    _pack_rows(u2_ref, u2)


def _out_projection(att_c, att_l, of_c, of_l, ob_c, ob_l, gg, x_c, x_l, mod3, wo, gn, l1g, l1b, wrt,
                    sw13, sw2, n_seq_tiles):
    n_ctx = x_c.shape[0]
    n = n_ctx + x_l.shape[0]
    tb = TOK_TILE
    n_ctx_tiles = n_ctx // tb

    def mod_idx(i):
        return (jnp.where(i < n_ctx_tiles, 0, 1 + (i - n_ctx_tiles) // n_seq_tiles), 0, 0)

    ctx_blk = lambda i: jnp.minimum(i, n_ctx_tiles - 1)
    lat_blk = lambda i: jnp.maximum(i - n_ctx_tiles, 0)
    tok = lambda w: pl.BlockSpec((tb, w), lambda i: (i, 0))
    tok_c = lambda w: pl.BlockSpec((tb, w), lambda i: (ctx_blk(i), 0))
    tok_l = lambda w: pl.BlockSpec((tb, w), lambda i: (lat_blk(i), 0))
    full = lambda a: pl.BlockSpec(a.shape, lambda i: (0,) * a.ndim)
    return pl.pallas_call(
        functools.partial(_outproj_kernel, n_ctx_tiles=n_ctx_tiles),
        grid=(n // tb,),
        in_specs=[pl.BlockSpec((ATT_WIDTH, tb), lambda i: (0, ctx_blk(i))),
                  pl.BlockSpec((ATT_WIDTH, tb), lambda i: (0, lat_blk(i))),
                  tok_c(GLA_WIDTH), tok_l(GLA_WIDTH), tok_c(GLA_WIDTH), tok_l(GLA_WIDTH),
                  tok(GLA_WIDTH), tok_c(D_MODEL), tok_l(D_MODEL),
                  pl.BlockSpec((None, 1, mod3.shape[2]), mod_idx),
                  full(wo), full(gn), full(l1g), full(l1b), full(wrt), full(sw13), full(sw2)],
        out_specs=(tok(D_MODEL),
                   pl.BlockSpec((tb * PACK_CHUNKS, LANES), lambda i: (i, 0)),
                   pl.BlockSpec((N_EXPERTS, tb), lambda i: (0, i))),
        out_shape=(jax.ShapeDtypeStruct((n, D_MODEL), F32),
                   jax.ShapeDtypeStruct((n * PACK_CHUNKS, LANES), U32),
                   jax.ShapeDtypeStruct((N_EXPERTS, n), F32)),
        compiler_params=_cparams(1), name="out_projection",
    )(att_c, att_l, of_c, of_l, ob_c, ob_l, gg, x_c, x_l, mod3, wo, gn, l1g, l1b, wrt, sw13, sw2)


def _route_kernel(lg_ref, bias_ref, upper_ref, idx_ref, w_ref, pos_ref, cnt_ref, run_ref):
    i = pl.program_id(0)

    @pl.when(i == 0)
    def _():
        run_ref[...] = jnp.zeros_like(run_ref)

    s = jax.nn.sigmoid(lg_ref[...])
    work = s + bias_ref[...]
    rows = lax.broadcasted_iota(I32, s.shape, 0).astype(F32)
    sel = jnp.zeros(s.shape, F32)
    idxs, vals = [], []
    for _ in range(TOP_K):
        mx = jnp.max(work, axis=0, keepdims=True)
        idx = jnp.min(jnp.where(work == mx, rows, float(N_EXPERTS)), axis=0, keepdims=True)
        hit = rows == idx
        vals.append(jnp.sum(jnp.where(hit, s, 0.0), axis=0, keepdims=True))
        idxs.append(idx)
        sel = jnp.where(hit, 1.0, sel)
        work = jnp.where(hit, -jnp.inf, work)
    den = functools.reduce(jnp.add, vals)
    rank = jnp.dot(sel.astype(BF16), upper_ref[...], preferred_element_type=F32) + run_ref[:, 0:1]
    for k in range(TOP_K):
        idx_ref[k:k + 1, :] = idxs[k].astype(I32)
        w_ref[k:k + 1, :] = vals[k] / den * ROUTED_SCALE
        pos_ref[k:k + 1, :] = jnp.sum(jnp.where(rows == idxs[k], rank, 0.0), axis=0,
                                      keepdims=True).astype(I32)
    run_ref[...] = run_ref[...] + jnp.sum(sel, axis=1, keepdims=True)
    cnt_ref[...] = run_ref[...]


def _route(logits_t, bias_col, upper):
    n = logits_t.shape[1]
    tt = TOK_TILE
    row = lambda dt: jax.ShapeDtypeStruct((TOP_K, n), dt)
    blk = pl.BlockSpec((TOP_K, tt), lambda i: (0, i))
    return pl.pallas_call(
        _route_kernel,
        grid=(n // tt,),
        in_specs=[pl.BlockSpec((N_EXPERTS, tt), lambda i: (0, i)),
                  pl.BlockSpec((N_EXPERTS, 1), lambda i: (0, 0)),
                  pl.BlockSpec((tt, tt), lambda i: (0, 0))],
        out_specs=(blk, blk, blk, pl.BlockSpec((N_EXPERTS, LANES), lambda i: (0, 0))),
        out_shape=(row(I32), row(F32), row(I32), jax.ShapeDtypeStruct((N_EXPERTS, LANES), F32)),
        scratch_shapes=[pltpu.VMEM((N_EXPERTS, LANES), F32)],
        compiler_params=_cparams(1), name="route",
    )(logits_t, bias_col, upper)


def _dest_kernel(cnt_ref, lower_ref, idx_ref, pos_ref, dest_ref, bexp_ref, bval_ref, nused_ref):
    cnt = cnt_ref[...]
    nblk = jnp.floor((cnt + (MOE_ROWS - 1)) * (1.0 / MOE_ROWS))
    bstart = jnp.dot(lower_ref[...], nblk, precision=HIGHEST, preferred_element_type=F32)
    bend = bstart + nblk
    pstart = bstart[:, 0:1] * MOE_ROWS
    rows = lax.broadcasted_iota(I32, (N_EXPERTS, idx_ref.shape[1]), 0)
    for k in range(TOP_K):
        hit = rows == idx_ref[k:k + 1, :]
        dest_ref[k:k + 1, :] = (jnp.sum(jnp.where(hit, pstart, 0.0), axis=0, keepdims=True)
                                .astype(I32) + pos_ref[k:k + 1, :])

    @pl.when(pl.program_id(0) == 0)
    def _():
        nb = bexp_ref.shape[1]
        bid = lax.broadcasted_iota(I32, (N_EXPERTS, nb), 1).astype(F32)
        inside = jnp.logical_and(bid >= bstart[:, 0:1], bid < bend[:, 0:1])
        erow = lax.broadcasted_iota(I32, (N_EXPERTS, nb), 0).astype(F32)
        bexp_ref[...] = jnp.sum(jnp.where(inside, erow, 0.0), axis=0, keepdims=True).astype(I32)
        valid = jnp.clip(cnt[:, 0:1] - (bid - bstart[:, 0:1]) * MOE_ROWS, 0.0, float(MOE_ROWS))
        bval_ref[...] = jnp.sum(jnp.where(inside, valid, 0.0), axis=0, keepdims=True).astype(I32)
        nused_ref[...] = jnp.max(bend, axis=0, keepdims=True).astype(I32)


def _destinations(counts, lower, idx_t, pos_t, n_blocks_pad):
    n = idx_t.shape[1]
    tt = TOK_TILE
    blk = pl.BlockSpec((TOP_K, tt), lambda i: (0, i))
    one = lambda w: pl.BlockSpec((1, w), lambda i: (0, 0))
    return pl.pallas_call(
        _dest_kernel,
        grid=(n // tt,),
        in_specs=[pl.BlockSpec((N_EXPERTS, LANES), lambda i: (0, 0)),
                  pl.BlockSpec((N_EXPERTS, N_EXPERTS), lambda i: (0, 0)), blk, blk],
        out_specs=(blk, one(n_blocks_pad), one(n_blocks_pad), one(LANES)),
        out_shape=(jax.ShapeDtypeStruct((TOP_K, n), I32),
                   jax.ShapeDtypeStruct((1, n_blocks_pad), I32),
                   jax.ShapeDtypeStruct((1, n_blocks_pad), I32),
                   jax.ShapeDtypeStruct((1, LANES), I32)),
        compiler_params=_cparams(1), name="destinations",
    )(counts, lower, idx_t, pos_t)


SC_WINDOW = 128
SC_WINDOWS_PER_STEP = 8


def _invert_rows(dest_flat, n_rows):
    m = dest_flat.shape[0]
    mesh = plsc.VectorSubcoreMesh(core_axis_name="core", subcore_axis_name="subcore")

    @functools.partial(pl.kernel, out_type=jax.ShapeDtypeStruct((n_rows,), I32), mesh=mesh,
                       scratch_types=[])
    def invert(val_hbm, idx_hbm, out_hbm):
        def body(val_vmem, idx_vmem):
            for j in range(SC_WINDOWS_PER_STEP):
                pltpu.sync_copy(val_vmem.at[j], out_hbm.at[idx_vmem.at[j]])

        blk = pl.BlockSpec((SC_WINDOWS_PER_STEP, SC_WINDOW), lambda i: (i, 0))
        pltpu.emit_pipeline(
            body, grid=(m // (SC_WINDOW * SC_WINDOWS_PER_STEP),),
            in_specs=[blk, blk], out_specs=[], core_axis_name=("core", "subcore"),
            dimension_semantics=(pltpu.PARALLEL,),
        )(val_hbm, idx_hbm)

    shape = (m // SC_WINDOW, SC_WINDOW)
    return invert(jnp.arange(m, dtype=I32).reshape(shape), dest_flat.reshape(shape))


PACK_CHUNKS = D_MODEL // (2 * LANES)
SRC_GROUP = 4
TOP_K_LOG2 = TOP_K.bit_length() - 1
PACK_CHUNKS_LOG2 = PACK_CHUNKS.bit_length() - 1
U32 = jnp.uint32


def _pack_rows(ref, x, row0=0):
    bits = pltpu.bitcast(x.astype(BF16).astype(F32), U32)
    for s in range(PACK_CHUNKS):
        lo = bits[:, (2 * s) * LANES:(2 * s + 1) * LANES] >> 16
        hi = bits[:, (2 * s + 1) * LANES:(2 * s + 2) * LANES] & jnp.uint32(0xFFFF0000)
        ref[pl.ds(row0 + s, x.shape[0], stride=PACK_CHUNKS), :] = lo | hi


def _unpack_rows(ref, n_rows, row0=0):
    parts = []
    for s in range(PACK_CHUNKS):
        w = ref[pl.ds(row0 + s, n_rows, stride=PACK_CHUNKS), :]
        parts.append(pltpu.bitcast(w << 16, F32))
        parts.append(pltpu.bitcast(w & jnp.uint32(0xFFFF0000), F32))
    return jnp.concatenate(parts, axis=1).astype(BF16)


def _moe_kernel(bexp_ref, bval_ref, nused_ref, u2p_hbm, src_hbm, w1_hbm, w3_hbm, w2_hbm, yt_hbm,
                u2p_vmem, w1_f, w3_f, w2_f, w13_s, w2_s, xbuf, ybuf, src_smem,
                sem_in, sem_src, sem_w, sem_out, *, n_tokens):
    n_used = nused_ref[0]
    br = MOE_ROWS
    grp = SRC_GROUP * br
    trash0 = n_tokens * TOP_K

    def src_copy(g):
        return pltpu.make_async_copy(src_hbm.at[pl.ds(g * grp, grp)],
                                     src_smem.at[pl.ds(lax.rem(g, 2) * grp, grp)], sem_src)

    def out_wait(slot):
        pltpu.make_async_copy(ybuf.at[pl.ds(slot * br * PACK_CHUNKS, br * PACK_CHUNKS)],
                              yt_hbm.at[pl.ds(0, br * PACK_CHUNKS)], sem_out.at[slot]).wait()

    def src_base(blk):
        return lax.rem(blk // SRC_GROUP, 2) * grp + lax.rem(blk, SRC_GROUP) * br

    def scatter_row(blk_slot, sbase, valid, r):
        dst = jnp.where(r < valid, src_smem[sbase + r], trash0 + blk_slot * br + r)
        pltpu.make_async_copy(
            ybuf.at[pl.ds(pl.multiple_of((blk_slot * br + r) * PACK_CHUNKS, PACK_CHUNKS), PACK_CHUNKS)],
            yt_hbm.at[pl.ds(pl.multiple_of(dst * PACK_CHUNKS, PACK_CHUNKS), PACK_CHUNKS)],
            sem_out.at[blk_slot]).start(priority=r % 2 if isinstance(r, int) else 0)

    def gather_row(xslot, sbase, r):
        row = lax.shift_right_logical(src_smem[sbase + r], TOP_K_LOG2 - PACK_CHUNKS_LOG2)
        row = jnp.minimum(row & (-PACK_CHUNKS & 0x7FFFFFFF), (n_tokens - 1) * PACK_CHUNKS)
        dst = pl.multiple_of((xslot * br + r) * PACK_CHUNKS, PACK_CHUNKS)
        xbuf[pl.ds(dst, PACK_CHUNKS), :] = u2p_vmem[pl.ds(pl.multiple_of(row, PACK_CHUNKS), PACK_CHUNKS), :]

    def weight_copies(e, wslot):
        return [pltpu.make_async_copy(src.at[e], dst.at[wslot], sem_w.at[wslot])
                for src, dst in ((w1_hbm, w1_f), (w3_hbm, w3_f), (w2_hbm, w2_f))]

    cp = pltpu.make_async_copy(u2p_hbm, u2p_vmem, sem_in)
    cp.start()
    src_copy(0).start()
    for wcp in weight_copies(bexp_ref[0], 0):
        wcp.start()
    ybuf[...] = jnp.zeros_like(ybuf)
    cp.wait()
    src_copy(0).wait()
    lax.fori_loop(0, br, lambda r, c: (gather_row(0, 0, r), c)[1], 0)

    def block(b, wslot):
        g = b // SRC_GROUP
        phase = lax.rem(b, SRC_GROUP)

        more = (g + 1) * SRC_GROUP < n_used

        @pl.when(jnp.logical_and(phase == 1, more))
        def _():
            src_copy(g + 1).start()

        @pl.when(jnp.logical_and(phase == SRC_GROUP - 1, more))
        def _():
            src_copy(g + 1).wait()

        e = bexp_ref[b]
        prev = bexp_ref[jnp.maximum(b - 1, 0)]

        @pl.when(jnp.logical_or(b == 0, e != prev))
        def _():
            for wcp in weight_copies(e, wslot):
                wcp.wait()
            w13_s[:, 0:EXPERT_FF] = w1_f[wslot].astype(BF16)
            w13_s[:, EXPERT_FF:2 * EXPERT_FF] = w3_f[wslot].astype(BF16)
            w2_s[...] = w2_f[wslot].astype(BF16)
            nxt = lax.while_loop(
                lambda j: jnp.logical_and(j < n_used, bexp_ref[jnp.minimum(j, n_used - 1)] == e),
                lambda j: j + 1, b + 1)

            @pl.when(nxt < n_used)
            def _():
                for wcp in weight_copies(bexp_ref[nxt], 1 - wslot):
                    wcp.start()

        switch = jnp.logical_and(b + 1 < n_used, bexp_ref[b + 1] != e)

        valid = bval_ref[b]
        sbase = src_base(b)
        slot = lax.rem(b, 2)

        pb = jnp.maximum(b - 1, 0)
        p_valid = jnp.where(b > 0, bval_ref[pb], 0)
        p_base = src_base(pb)
        n_base = src_base(b + 1)
        for r in range(br):
            gather_row(1 - slot, n_base, r)
            scatter_row(1 - slot, p_base, p_valid, r)

        x = _unpack_rows(xbuf, br, slot * (br * PACK_CHUNKS))
        rows = lax.broadcasted_iota(I32, x.shape, 0)
        x = jnp.where(rows < valid, x, jnp.zeros_like(x))
        ab = jnp.dot(x, w13_s[...], preferred_element_type=F32)
        hid = (_silu(ab[:, 0:EXPERT_FF]) * ab[:, EXPERT_FF:2 * EXPERT_FF]).astype(BF16)
        y = jnp.dot(hid, w2_s[...], preferred_element_type=F32)

        @pl.when(b >= 1)
        def _():
            out_wait(slot)

        _pack_rows(ybuf, y, slot * (br * PACK_CHUNKS))
        return jnp.where(switch, 1 - wslot, wslot)

    lax.fori_loop(0, n_used, block, 0)

    last = n_used - 1
    l_slot = lax.rem(last, 2)
    l_base = src_base(last)
    l_valid = bval_ref[last]
    lax.fori_loop(0, br, lambda r, c: (scatter_row(l_slot, l_base, l_valid, r), c)[1], 0)
    out_wait(1 - l_slot)
    out_wait(l_slot)


def _moe_experts(bexp, bval, nused, u2p, row_src, w1, w3, w2, n_blocks, n_tokens):
    br = MOE_ROWS
    any_spec = pl.BlockSpec(memory_space=pl.ANY)
    grid_spec = pltpu.PrefetchScalarGridSpec(
        num_scalar_prefetch=3, grid=(1,),
        in_specs=[any_spec] * 5,
        out_specs=any_spec,
        scratch_shapes=[pltpu.VMEM(u2p.shape, U32),
                        pltpu.VMEM((2, D_MODEL, EXPERT_FF), F32),
                        pltpu.VMEM((2, D_MODEL, EXPERT_FF), F32),
                        pltpu.VMEM((2, EXPERT_FF, D_MODEL), F32),
                        pltpu.VMEM((D_MODEL, 2 * EXPERT_FF), BF16),
                        pltpu.VMEM((EXPERT_FF, D_MODEL), BF16),
                        pltpu.VMEM((2 * PACK_CHUNKS * br, LANES), U32),
                        pltpu.VMEM((2 * br * PACK_CHUNKS, LANES), U32),
                        pltpu.SMEM((2 * SRC_GROUP * br,), I32),
                        pltpu.SemaphoreType.DMA, pltpu.SemaphoreType.DMA,
                        pltpu.SemaphoreType.DMA((2,)), pltpu.SemaphoreType.DMA((2,))])
    n_out_tiles = n_tokens * TOP_K + 2 * br
    return pl.pallas_call(
        functools.partial(_moe_kernel, n_tokens=n_tokens), grid_spec=grid_spec,
        out_shape=jax.ShapeDtypeStruct((n_out_tiles * PACK_CHUNKS, LANES), U32),
        compiler_params=pltpu.CompilerParams(dimension_semantics=("arbitrary",),
                                             vmem_limit_bytes=MOE_VMEM_LIMIT),
        name="moe_experts",
    )(bexp, bval, nused, u2p, row_src, w1, w3, w2)


def _combine_kernel(w_hbm, yt_ref, base_ref, mod_ref, g_ref, b_ref, yc_ref, yl_ref,
                    w_smem, acc_lo, acc_hi, sem_w, *, n_ctx_tiles):
    i = pl.program_id(0)
    n_steps = pl.num_programs(0)
    n_tok = acc_lo.shape[0] // PACK_CHUNKS
    n_idx = n_tok * TOP_K

    def w_copy(tile):
        return pltpu.make_async_copy(w_hbm.at[pl.ds(tile * n_idx, n_idx)],
                                     w_smem.at[pl.ds(lax.rem(tile, 2) * n_idx, n_idx)], sem_w)

    @pl.when(i == 0)
    def _():
        w_copy(i).start()

    w_copy(i).wait()

    @pl.when(i + 1 < n_steps)
    def _():
        w_copy(i + 1).start()

    wbase = lax.rem(i, 2) * n_idx

    per_tile = SUBLANES // PACK_CHUNKS
    first = lax.broadcasted_iota(I32, (SUBLANES, LANES), 0) < PACK_CHUNKS

    def reduce_token(t, carry):
        lo = hi = None
        for m in range(TOP_K // per_tile):
            j = t * TOP_K + m * per_tile
            words = yt_ref[pl.ds(pl.multiple_of(j * PACK_CHUNKS, SUBLANES), SUBLANES), :]
            wgt = jnp.where(first, w_smem[wbase + j], w_smem[wbase + j + 1])
            t_lo = wgt * pltpu.bitcast(words << 16, F32)
            t_hi = wgt * pltpu.bitcast(words & jnp.uint32(0xFFFF0000), F32)
            lo = t_lo if lo is None else lo + t_lo
            hi = t_hi if hi is None else hi + t_hi
        row = pl.multiple_of(t * PACK_CHUNKS, PACK_CHUNKS)
        acc_lo[pl.ds(row, PACK_CHUNKS), :] = lo[0:PACK_CHUNKS] + lo[PACK_CHUNKS:SUBLANES]
        acc_hi[pl.ds(row, PACK_CHUNKS), :] = hi[0:PACK_CHUNKS] + hi[PACK_CHUNKS:SUBLANES]
        return carry

    lax.fori_loop(0, n_tok, reduce_token, 0)
    parts = []
    for s in range(PACK_CHUNKS):
        parts.append(acc_lo[pl.ds(s, n_tok, stride=PACK_CHUNKS), :])
        parts.append(acc_hi[pl.ds(s, n_tok, stride=PACK_CHUNKS), :])
    moe = jnp.concatenate(parts, axis=1)
    gate2 = mod_ref[:, 5 * D_MODEL:6 * D_MODEL]
    y = _layer_norm(base_ref[...] + gate2 * moe, g_ref[...], b_ref[...])

    @pl.when(i < n_ctx_tiles)
    def _():
        yc_ref[...] = y

    @pl.when(i >= n_ctx_tiles)
    def _():
        yl_ref[...] = y


def _combine(w_flat, yt, base, mod3, l2g, l2b, n_ctx, seq_tokens):
    n = base.shape[0]
    tc = COMB_TILE
    n_ctx_tiles = n_ctx // tc
    n_seq_tiles = seq_tokens // tc

    def mod_idx(i):
        return (jnp.where(i < n_ctx_tiles, 0, 1 + (i - n_ctx_tiles) // n_seq_tiles), 0, 0)

    full = lambda a: pl.BlockSpec(a.shape, lambda i: (0,) * a.ndim)
    return pl.pallas_call(
        functools.partial(_combine_kernel, n_ctx_tiles=n_ctx_tiles),
        grid=(n // tc,),
        in_specs=[pl.BlockSpec(memory_space=pl.ANY),
                  pl.BlockSpec((tc * TOP_K * PACK_CHUNKS, LANES), lambda i: (i, 0)),
                  pl.BlockSpec((tc, D_MODEL), lambda i: (i, 0)),
                  pl.BlockSpec((None, 1, mod3.shape[2]), mod_idx), full(l2g), full(l2b)],
        out_specs=(pl.BlockSpec((tc, D_MODEL), lambda i: (jnp.minimum(i, n_ctx_tiles - 1), 0)),
                   pl.BlockSpec((tc, D_MODEL), lambda i: (jnp.maximum(i - n_ctx_tiles, 0), 0))),
        out_shape=(jax.ShapeDtypeStruct((n_ctx, D_MODEL), F32),
                   jax.ShapeDtypeStruct((n - n_ctx, D_MODEL), F32)),
        scratch_shapes=[pltpu.SMEM((2 * tc * TOP_K,), F32),
                        pltpu.VMEM((tc * PACK_CHUNKS, LANES), F32),
                        pltpu.VMEM((tc * PACK_CHUNKS, LANES), F32),
                        pltpu.SemaphoreType.DMA],
        compiler_params=_cparams(1), name="combine",
    )(w_flat, yt, base, mod3, l2g, l2b)


def _rope_tables(n_tok, tile):
    rows = n_tok // GRID_W
    row_idx = jnp.repeat(jnp.arange(rows, dtype=F32), GRID_W)
    col_idx = jnp.tile(jnp.arange(GRID_W, dtype=F32), rows)
    inv_freq = 1.0 / (ROPE_THETA ** (jnp.arange(0, ROPE_AXIS_DIM, 2, dtype=F32) / ROPE_AXIS_DIM))
    ang_r = row_idx[:, None] * inv_freq[None, :]
    ang_c = col_idx[:, None] * inv_freq[None, :]
    ang = jnp.concatenate([ang_r, ang_r, ang_c, ang_c], axis=-1)
    cos, sin = jnp.cos(ang), jnp.sin(ang)
    quarter = (jnp.arange(HEAD_DIM) // (ROPE_AXIS_DIM // 2)) % 2
    sin_a = jnp.where(quarter == 0, -sin, 0.0)
    sin_b = jnp.where(quarter == 1, sin, 0.0)
    rep = LANES // HEAD_DIM
    ident = lambda v: jnp.full((tile, LANES), v, F32)
    cos_t = jnp.concatenate([jnp.tile(cos, (1, rep)), ident(1.0)], axis=0)
    sa_t = jnp.concatenate([jnp.tile(sin_a, (1, rep)), ident(0.0)], axis=0)
    sb_t = jnp.concatenate([jnp.tile(sin_b, (1, rep)), ident(0.0)], axis=0)
    ident_tr = lambda v: jnp.full((HEAD_DIM, tile), v, F32)
    cos_tr = jnp.concatenate([cos.T, ident_tr(1.0)], axis=1)
    sin_tr = jnp.concatenate([sin.T, ident_tr(0.0)], axis=1)
    return cos_t, sa_t, sb_t, cos_tr, sin_tr


def _dup_heads(a):
    parts = []
    for h in range(KV_HEADS):
        blk = a[..., h * HEAD_DIM:(h + 1) * HEAD_DIM]
        parts += [blk] * (LANES // HEAD_DIM)
    return jnp.concatenate(parts, axis=-1)


def kernel(x_prompt, x_sample, cache_k, cache_v, state_gla_fwd, state_gla_bwd, c, c_ctx, w_ada, b_ada, w_in, q_norm, k_norm, gla_wa_fwd, gla_ba_fwd, gla_wa_bwd, gla_ba_bwd, gla_norm, w_out, ln1_g, ln1_b, ln2_g, ln2_b, w_router, router_bias, exp_w1, exp_w3, exp_w2, sh_w1, sh_w3, sh_w2):
    n_ctx_b, ctx_seq, _ = x_prompt.shape
    n_lat_b, lat_seq, _ = x_sample.shape
    n_ctx = n_ctx_b * ctx_seq
    n_lat = n_lat_b * lat_seq
    n = n_ctx + n_lat
    l = 0

    x_c = x_prompt.reshape(n_ctx, D_MODEL)
    x_l = x_sample.reshape(n_lat, D_MODEL)

    c_rows = jnp.zeros((SUBLANES, D_MODEL), F32).at[0].set(c_ctx).at[1:1 + n_lat_b].set(c)
    mod = _modulation(c_rows, w_ada[l], b_ada[l][None, :])
    mod3 = mod.reshape(SUBLANES, 1, 6 * D_MODEL)

    wi = w_in[l]
    o_q, o_k, o_v, o_gq, o_gk, o_gv, o_gg, o_rf, o_rb, o_end = np.cumsum(
        [0, ATT_WIDTH, KV_HEADS * HEAD_DIM, KV_HEADS * HEAD_DIM, GLA_KW, GLA_KW, GLA_WIDTH, GLA_WIDTH,
         GLA_GATE_RANK, GLA_GATE_RANK])
    w_tok = jnp.concatenate([
        _dup_heads(wi[:, o_k:o_v]), wi[:, o_v:o_gq], wi[:, o_gq:o_gk],
        wi[:, o_gv:o_gg], wi[:, o_gg:o_rf], wi[:, o_rf:o_end],
        jnp.zeros((D_MODEL, LANES - 2 * GLA_GATE_RANK), F32)], axis=1).astype(BF16)
    w_tr = jnp.concatenate([wi[:, o_q:o_k], wi[:, o_v:o_gq], wi[:, o_gk:o_gv], wi[:, o_rf:o_end]],
                           axis=1).T.astype(BF16)
    rep = LANES // HEAD_DIM
    qn = q_norm[l][:, None]
    kn = jnp.tile(k_norm[l], rep)[None, :]
    seg = jnp.asarray(np.kron(np.eye(rep), np.ones((HEAD_DIM, HEAD_DIM))), BF16)
    wa = jnp.zeros((LANES, 2 * GLA_KW), F32)
    wa = wa.at[0:GLA_GATE_RANK, 0:GLA_KW].set(gla_wa_fwd[l])
    wa = wa.at[GLA_GATE_RANK:2 * GLA_GATE_RANK, GLA_KW:].set(gla_wa_bwd[l])
    ba = jnp.concatenate([gla_ba_fwd[l], gla_ba_bwd[l]])[None, :]
    wat = wa[0:2 * GLA_GATE_RANK, :].T
    bat = ba.T
    cos_t, sa_t, sb_t, cos_tr, sin_tr = _rope_tables(lat_seq, TOK_TILE)

    (qt, k_dup, vt, k32, v32, gq, gv, gg, la, gkt, lat) = _in_projection(
        x_c, x_l, mod3, w_tok, w_tr, qn, kn, cos_t, sa_t, sb_t, cos_tr, sin_tr, seg, wa, ba, wat, bat,
        lat_seq // TOK_TILE)

    ck = _dup_heads(cache_k[:, l].reshape(n_lat_b, -1, KV_HEADS * HEAD_DIM)).astype(BF16)
    cvt = cache_v[:, l].reshape(n_lat_b, -1, KV_HEADS * HEAD_DIM).transpose(0, 2, 1).astype(BF16)
    att_c = _attention(qt, k_dup, vt, None, 0, n_ctx_b, ctx_seq)
    att_l = _attention(qt, k_dup, vt, (ck, cvt), n_ctx, n_lat_b, lat_seq)

    gconst, levels_of = _gla_constants()
    to_dev = lambda t: (jnp.asarray(t[0], BF16), jnp.asarray(t[1], BF16), jnp.asarray(t[2], F32))
    bd = jnp.asarray(np.kron(np.eye(GLA_HEADS), np.ones((GLA_DK, GLA_DV))), BF16)
    vbd = jnp.asarray(np.kron(np.eye(GLA_HEADS), np.ones((GLA_CHUNK, GLA_DV))), BF16)
    consts = ((to_dev(gconst["f"]), to_dev(gconst["b"])), levels_of, bd, vbd)
    s_zero = jnp.zeros((n_ctx_b, GLA_HEADS, GLA_DK, GLA_DV), F32)
    of_c, ob_c, sf_new, sb_new = _gla(gq, la, gkt, lat, gv, s_zero, s_zero, consts, 0, n_ctx_b, ctx_seq)
    of_l, ob_l, _, _ = _gla(gq, la, gkt, lat, gv, state_gla_fwd[:, l], state_gla_bwd[:, l], consts,
                            n_ctx, n_lat_b, lat_seq)

    sw13 = jnp.concatenate([sh_w1[l], sh_w3[l]], axis=1).astype(BF16)
    base, u2_rows, logits_t = _out_projection(
        att_c, att_l, of_c, of_l, ob_c, ob_l, gg, x_c, x_l, mod3, w_out[l].astype(BF16),
        gla_norm[l][None, :], ln1_g[l][None, :], ln1_b[l][None, :], w_router[l].T.astype(BF16), sw13,
        sh_w2[l].astype(BF16), lat_seq // TOK_TILE)

    upper = jnp.asarray(np.triu(np.ones((TOK_TILE, TOK_TILE)), 1), BF16)
    idx_t, w_t, pos_t, counts = _route(logits_t, router_bias[l][:, None], upper)
    n_blocks = n * TOP_K // MOE_ROWS + N_EXPERTS
    n_blocks_pad = -(-n_blocks // LANES) * LANES
    lower = jnp.asarray(np.tril(np.ones((N_EXPERTS, N_EXPERTS)), -1), F32)
    dest_t, bexp, bval, nused = _destinations(counts, lower, idx_t, pos_t, n_blocks_pad)
    dest_flat = dest_t.T.reshape(-1)
    w_flat = w_t.T.reshape(-1)

    row_src = _invert_rows(dest_flat, n_blocks * MOE_ROWS)
    yt = _moe_experts(bexp.reshape(-1), bval.reshape(-1), nused.reshape(-1)[0:1], u2_rows, row_src,
                      exp_w1[l], exp_w3[l], exp_w2[l], n_blocks, n)
    y_c, y_l = _combine(w_flat, yt, base, mod3, ln2_g[l][None, :], ln2_b[l][None, :], n_ctx, lat_seq)

    y_prompt = y_c.reshape(n_ctx_b, ctx_seq, D_MODEL)
    y_sample = y_l.reshape(n_lat_b, lat_seq, D_MODEL)
    new_cache_k = k32.reshape(n_ctx_b, 1, ctx_seq, KV_HEADS, HEAD_DIM)
    new_cache_v = v32.reshape(n_ctx_b, 1, ctx_seq, KV_HEADS, HEAD_DIM)
    return (y_prompt, y_sample, new_cache_k, new_cache_v, sf_new[:, None], sb_new[:, None])
```

```python
import functools

import numpy as np
import jax
import jax.numpy as jnp
from jax import lax
from jax.experimental import pallas as pl
from jax.experimental.pallas import tpu as pltpu
from jax.experimental.pallas import tpu_sc as plsc

F32 = jnp.float32
BF16 = jnp.bfloat16
I32 = jnp.int32

D_MODEL = 1024
GRID_W = 64
HEAD_DIM = 64
N_HEADS = 8
KV_HEADS = 2
ATT_WIDTH = N_HEADS * HEAD_DIM
ATT_SCALE = HEAD_DIM ** -0.5
LOG2_E = 1.4426950408889634
ROPE_AXIS_DIM = HEAD_DIM // 2
ROPE_THETA = 10000.0
GLA_HEADS = 4
GLA_DK = 64
GLA_DV = 128
GLA_WIDTH = GLA_HEADS * GLA_DV
GLA_KW = GLA_HEADS * GLA_DK
GLA_GATE_RANK = 16
GLA_TAU = 16.0
N_EXPERTS = 256
TOP_K = 8
EXPERT_FF = 256
SHARED_FF = 256
ROUTED_SCALE = 2.5
DEPTH = 1
ALPHA = (2.0 * DEPTH) ** 0.25
EPS = 1e-6

LANES = 128
SUBLANES = 8
ROW_CHUNKS = D_MODEL // LANES
VMEM_LIMIT = 56 * 1024 * 1024

TOK_TILE = 512
ATT_TQ = 128
GLA_CHUNK = 128
GLA_LEVELS = ((32, 128), (8, 32), (2, 8), (1, 2))
MOE_ROWS = 256
MOE_VMEM_LIMIT = 62 * 1024 * 1024
COMB_TILE = 128
COMB_UNROLL = 4
HIGHEST = lax.Precision.HIGHEST


def _cparams(n_axes):
    return pltpu.CompilerParams(dimension_semantics=("arbitrary",) * n_axes,
                                vmem_limit_bytes=VMEM_LIMIT)


def _silu(x):
    return x * jax.nn.sigmoid(x)


def _log_sigmoid(x):
    return jnp.minimum(x, 0.0) - jnp.log(1.0 + jnp.exp(-jnp.abs(x)))


def _dot_split(a, b):
    a_hi = a.astype(BF16)
    b_hi = b.astype(BF16)
    a_lo = (a - a_hi.astype(F32)).astype(BF16)
    b_lo = (b - b_hi.astype(F32)).astype(BF16)
    dot = functools.partial(jnp.dot, preferred_element_type=F32)
    return dot(a_hi, b_hi) + dot(a_lo, b_hi) + dot(a_hi, b_lo)


def _load_row_tiles(ref, n_rows, row0=0):
    return jnp.concatenate(
        [ref[pl.ds(row0 * ROW_CHUNKS + cidx, n_rows, stride=ROW_CHUNKS), :] for cidx in range(ROW_CHUNKS)],
        axis=1)


def _store_row_tiles(ref, x):
    for cidx in range(ROW_CHUNKS):
        ref[pl.ds(cidx, x.shape[0], stride=ROW_CHUNKS), :] = x[:, cidx * LANES:(cidx + 1) * LANES]


def _layer_norm(z, g, b):
    mu = jnp.mean(z, axis=-1, keepdims=True)
    zc = z - mu
    var = jnp.mean(zc * zc, axis=-1, keepdims=True)
    return zc * lax.rsqrt(var + EPS) * g + b


def _mod_kernel(c_ref, w_ref, b_ref, o_ref):
    s = _silu(c_ref[...]).astype(BF16)
    o_ref[...] = jnp.dot(s, w_ref[...].astype(BF16), preferred_element_type=F32) + b_ref[...]


def _modulation(c_rows, w_ada, b_ada):
    n_cols = w_ada.shape[1]
    tn = 512
    return pl.pallas_call(
        _mod_kernel,
        grid=(n_cols // tn,),
        in_specs=[pl.BlockSpec((SUBLANES, D_MODEL), lambda j: (0, 0)),
                  pl.BlockSpec((D_MODEL, tn), lambda j: (0, j)),
                  pl.BlockSpec((1, tn), lambda j: (0, j))],
        out_specs=pl.BlockSpec((SUBLANES, tn), lambda j: (0, j)),
        out_shape=jax.ShapeDtypeStruct((SUBLANES, n_cols), F32),
        compiler_params=_cparams(1),
        name="modulation",
    )(c_rows, w_ada, b_ada)


_C_K = 0
_C_V = _C_K + 2 * LANES
_C_GQ = _C_V + KV_HEADS * HEAD_DIM
_C_GV = _C_GQ + GLA_KW
_C_GG = _C_GV + GLA_WIDTH
_C_RA = _C_GG + GLA_WIDTH
_C_END = _C_RA + LANES
_R_Q = 0
_R_V = _R_Q + ATT_WIDTH
_R_GK = _R_V + KV_HEADS * HEAD_DIM
_R_RA = _R_GK + GLA_KW
_R_END = _R_RA + 2 * GLA_GATE_RANK


def _inproj_kernel(xc_ref, xl_ref, mod_ref, w_ref, wt_ref, qn_ref, kn_ref, cos_ref, sa_ref, sb_ref,
                   cost_ref, sint_ref, seg_ref, wa_ref, ba_ref, wat_ref, bat_ref,
                   qt_ref, k_ref, vt_ref, k32_ref, v32_ref, gq_ref, gv_ref, gg_ref,
                   la_ref, gkt_ref, lat_ref, *, n_ctx_tiles):
    i = pl.program_id(0)
    m = mod_ref[...]
    shift1 = m[:, 0:D_MODEL]
    scale1 = m[:, D_MODEL:2 * D_MODEL]
    x = jnp.where(i < n_ctx_tiles, xc_ref[...], xl_ref[...])
    u = (x * (1.0 + scale1) + shift1).astype(BF16)

    cos = cos_ref[...]
    sin_a = sa_ref[...]
    sin_b = sb_ref[...]
    seg = seg_ref[...]
    lane = lax.broadcasted_iota(I32, (u.shape[0], LANES), 1)
    low = lane < HEAD_DIM

    def proj(c0, c1):
        return jnp.dot(u, w_ref[:, c0:c1], preferred_element_type=F32)

    def head_norm(blk, gain):
        ss = jnp.dot((blk * blk).astype(BF16), seg, preferred_element_type=F32) * (1.0 / HEAD_DIM)
        return blk * lax.rsqrt(ss + EPS) * gain

    def rope(blk):
        return (blk * cos + pltpu.roll(blk, LANES - ROPE_AXIS_DIM // 2, 1) * sin_a
                + pltpu.roll(blk, ROPE_AXIS_DIM // 2, 1) * sin_b)

    pk = proj(_C_K, _C_V)
    kn = [head_norm(pk[:, j * LANES:(j + 1) * LANES], kn_ref[...]) for j in range(KV_HEADS)]
    for j in range(KV_HEADS):
        k_ref[:, j * LANES:(j + 1) * LANES] = rope(kn[j]).astype(BF16)

    @pl.when(i < n_ctx_tiles)
    def _():
        k32_ref[...] = jnp.where(low, kn[0], kn[1])
        v32_ref[...] = proj(_C_V, _C_GQ)

    gq_ref[...] = proj(_C_GQ, _C_GV) * (GLA_DK ** -0.5)
    gv_ref[...] = proj(_C_GV, _C_GG).astype(BF16)
    gg_ref[...] = proj(_C_GG, _C_RA).astype(BF16)

    ra = proj(_C_RA, _C_END)
    pre = _dot_split(ra, wa_ref[...]) + ba_ref[...]
    la_ref[...] = _log_sigmoid(pre) * (1.0 / GLA_TAU)

    pt = lax.dot_general(wt_ref[...], u, (((1,), (1,)), ((), ())), preferred_element_type=F32)
    cos_t = cost_ref[...]
    sin_t = sint_ref[...]
    quarter = ROPE_AXIS_DIM // 2
    for h in range(N_HEADS):
        blk = pt[_R_Q + h * HEAD_DIM:_R_Q + (h + 1) * HEAD_DIM, :]
        ms = jnp.mean(blk * blk, axis=0, keepdims=True)
        qn = blk * lax.rsqrt(ms + EPS) * qn_ref[...]
        rot = jnp.concatenate([-qn[quarter:2 * quarter], qn[0:quarter],
                               -qn[3 * quarter:4 * quarter], qn[2 * quarter:3 * quarter]], axis=0)
        qt_ref[h * HEAD_DIM:(h + 1) * HEAD_DIM, :] = (
            (qn * cos_t + rot * sin_t) * (ATT_SCALE * LOG2_E)).astype(BF16)
    vt_ref[...] = pt[_R_V:_R_GK, :].astype(BF16)
    gkt_ref[...] = pt[_R_GK:_R_RA, :]
    rat = pt[_R_RA:_R_END, :]
    pre_t = _dot_split(wat_ref[...], rat) + bat_ref[...]
    lat_ref[...] = _log_sigmoid(pre_t) * (1.0 / GLA_TAU)


def _in_projection(x_c, x_l, mod3, w_tok, w_tr, qn, kn, cos_t, sa_t, sb_t, cos_tr, sin_tr, seg, wa, ba,
                   wat, bat, n_seq_tiles):
    n_ctx = x_c.shape[0]
    n = n_ctx + x_l.shape[0]
    tb = TOK_TILE
    n_ctx_tiles = n_ctx // tb
    n_tiles = n // tb
    n_rope_blocks = cos_t.shape[0] // tb - 1

    def mod_idx(i):
        return (jnp.where(i < n_ctx_tiles, 0, 1 + (i - n_ctx_tiles) // n_seq_tiles), 0, 0)

    def rope_blk(i):
        return jnp.where(i < n_ctx_tiles, n_rope_blocks, (i - n_ctx_tiles) % n_seq_tiles)

    def rope_idx(i):
        return (rope_blk(i), 0)

    def ctx_idx(i):
        return (jnp.minimum(i, n_ctx_tiles - 1), 0)

    tok = lambda w: pl.BlockSpec((tb, w), lambda i: (i, 0))
    full = lambda a: pl.BlockSpec(a.shape, lambda i: (0,) * a.ndim)
    tr = lambda r: pl.BlockSpec((r, tb), lambda i: (0, i))
    rope_tr = pl.BlockSpec((HEAD_DIM, tb), lambda i: (0, rope_blk(i)))
    out_shapes = (
        jax.ShapeDtypeStruct((ATT_WIDTH, n), BF16),
        jax.ShapeDtypeStruct((n, 2 * LANES), BF16),
        jax.ShapeDtypeStruct((KV_HEADS * HEAD_DIM, n), BF16),
        jax.ShapeDtypeStruct((n_ctx, LANES), F32),
        jax.ShapeDtypeStruct((n_ctx, LANES), F32),
        jax.ShapeDtypeStruct((n, GLA_KW), F32),
        jax.ShapeDtypeStruct((n, GLA_WIDTH), BF16),
        jax.ShapeDtypeStruct((n, GLA_WIDTH), BF16),
        jax.ShapeDtypeStruct((n, 2 * GLA_KW), F32),
        jax.ShapeDtypeStruct((GLA_KW, n), F32),
        jax.ShapeDtypeStruct((2 * GLA_KW, n), F32),
    )
    out_specs = (tr(ATT_WIDTH), tok(2 * LANES), tr(KV_HEADS * HEAD_DIM),
                 pl.BlockSpec((tb, LANES), ctx_idx), pl.BlockSpec((tb, LANES), ctx_idx),
                 tok(GLA_KW), tok(GLA_WIDTH), tok(GLA_WIDTH), tok(2 * GLA_KW),
                 tr(GLA_KW), tr(2 * GLA_KW))
    in_specs = [pl.BlockSpec((tb, D_MODEL), ctx_idx),
                pl.BlockSpec((tb, D_MODEL), lambda i: (jnp.maximum(i - n_ctx_tiles, 0), 0)),
                pl.BlockSpec((None, 1, mod3.shape[2]), mod_idx),
                full(w_tok), full(w_tr), full(qn), full(kn),
                pl.BlockSpec((tb, LANES), rope_idx), pl.BlockSpec((tb, LANES), rope_idx),
                pl.BlockSpec((tb, LANES), rope_idx), rope_tr, rope_tr,
                full(seg), full(wa), full(ba), full(wat), full(bat)]
    return pl.pallas_call(
        functools.partial(_inproj_kernel, n_ctx_tiles=n_ctx_tiles),
        grid=(n_tiles,), in_specs=in_specs, out_specs=out_specs, out_shape=out_shapes,
        compiler_params=_cparams(1), name="in_projection",
    )(x_c, x_l, mod3, w_tok, w_tr, qn, kn, cos_t, sa_t, sb_t, cos_tr, sin_tr, seg, wa, ba, wat, bat)


def _attention_kernel(*refs, n_kv_parts):
    qt_ref = refs[0]
    k_refs = refs[1:1 + n_kv_parts]
    vt_refs = refs[1 + n_kv_parts:1 + 2 * n_kv_parts]
    o_ref = refs[1 + 2 * n_kv_parts]
    tq = qt_ref.shape[1]
    group = N_HEADS // KV_HEADS
    scores = []
    for kv in range(KV_HEADS):
        heads = range(kv * group, (kv + 1) * group)
        q_grp = jnp.concatenate([qt_ref[h * HEAD_DIM:(h + 1) * HEAD_DIM, :] for h in heads], axis=1)
        rhs = jnp.concatenate([q_grp, jnp.zeros_like(q_grp)], axis=0)
        scores.append([jnp.dot(k[:, kv * LANES:(kv + 1) * LANES], rhs, preferred_element_type=F32)
                       for k in k_refs])
    for kv in range(KV_HEADS):
        heads = range(kv * group, (kv + 1) * group)
        s = scores[kv]
        mx = functools.reduce(jnp.maximum, [jnp.max(x, axis=0, keepdims=True) for x in s])
        pr = [jnp.exp2(x - mx) for x in s]
        den = functools.reduce(jnp.add, [jnp.sum(x, axis=0, keepdims=True) for x in pr])
        acc = functools.reduce(jnp.add, [
            jnp.dot(vt[kv * HEAD_DIM:(kv + 1) * HEAD_DIM, :], x.astype(BF16),
                    preferred_element_type=F32) for x, vt in zip(pr, vt_refs)])
        out = (acc / den).astype(BF16)
        for j, h in enumerate(heads):
            o_ref[h * HEAD_DIM:(h + 1) * HEAD_DIM, :] = out[:, j * tq:(j + 1) * tq]


def _attention(qt, k, vt, extra_kv, row0, n_batch, seq):
    tq = ATT_TQ
    n_q = seq // tq
    q_blk0 = row0 // tq
    kv_blk0 = row0 // seq
    in_specs = [pl.BlockSpec((ATT_WIDTH, tq), lambda b, i: (0, q_blk0 + b * n_q + i))]
    k_spec = pl.BlockSpec((seq, 2 * LANES), lambda b, i: (kv_blk0 + b, 0))
    vt_spec = pl.BlockSpec((KV_HEADS * HEAD_DIM, seq), lambda b, i: (0, kv_blk0 + b))
    args_k, args_v, specs_k, specs_v = [k], [vt], [k_spec], [vt_spec]
    if extra_kv is not None:
        ck, cvt = extra_kv
        args_k.append(ck)
        args_v.append(cvt)
        specs_k.append(pl.BlockSpec((None, ck.shape[1], 2 * LANES), lambda b, i: (b, 0, 0)))
        specs_v.append(pl.BlockSpec((None, KV_HEADS * HEAD_DIM, cvt.shape[2]), lambda b, i: (b, 0, 0)))
    return pl.pallas_call(
        functools.partial(_attention_kernel, n_kv_parts=len(args_k)),
        grid=(n_batch, n_q),
        in_specs=in_specs + specs_k + specs_v,
        out_specs=pl.BlockSpec((ATT_WIDTH, tq), lambda b, i: (0, b * n_q + i)),
        out_shape=jax.ShapeDtypeStruct((ATT_WIDTH, n_batch * seq), BF16),
        compiler_params=_cparams(2), name="attention",
    )(qt, *args_k, *args_v)


def _gla_constants():
    c = GLA_CHUNK
    idx = np.arange(c)
    q_mats, k_mats, masks, levels_of = [], [], [], []
    for li, (s, p) in enumerate(GLA_LEVELS):
        start = (idx // s) * s
        end = start + s - 1
        k_mats.append(((idx[None, :] > idx[:, None]) & (idx[None, :] <= end[:, None])))
        for d in range(p // s - 1):
            lo = np.maximum(start - d * s, 0)
            q_mats.append((idx[None, :] >= lo[:, None]) & (idx[None, :] <= idx[:, None]))
            masks.append((idx[:, None] // p == idx[None, :] // p)
                         & (idx[:, None] // s - idx[None, :] // s - 1 == d))
            levels_of.append(li)
    masks.append(np.eye(c, dtype=bool))
    levels_of.append(len(GLA_LEVELS) - 1)
    q_mats.append(idx[None, :] <= idx[:, None])
    k_mats = k_mats[:-1]
    k_mats.append(idx[None, :] > idx[:, None])
    k_mats.append(np.ones((c, c), bool))
    out = {}
    for name, flip in (("f", False), ("b", True)):
        f = (lambda a: a[::-1, ::-1]) if flip else (lambda a: a)
        lq = np.concatenate([f(a) for a in q_mats], axis=0).astype(np.float32)
        lkt = np.concatenate([f(a).T for a in k_mats], axis=1).astype(np.float32)
        mk = np.stack([np.tile(f(a), (1, GLA_HEADS)) for a in masks]).astype(np.float32)
        out[name] = (np.concatenate([lq, lq], axis=1), np.concatenate([lkt, lkt], axis=0), mk)
    return out, tuple(levels_of)


def _gla_direction(q, g, gkt, gt, v, lq2, lkt2, masks_ref, bd, vbd, s_ref, levels_of):
    c = GLA_CHUNK
    n_var = len(levels_of)
    n_lev = len(GLA_LEVELS)
    g_hi = g.astype(BF16)
    g_lo = (g - g_hi.astype(F32)).astype(BF16)
    fq = jnp.dot(lq2, jnp.concatenate([g_hi, g_lo], axis=0), preferred_element_type=F32)
    gt_hi = gt.astype(BF16)
    gt_lo = (gt - gt_hi.astype(F32)).astype(BF16)
    fk = jnp.dot(jnp.concatenate([gt_hi, gt_lo], axis=1), lkt2, preferred_element_type=F32)

    def key_factor(f):
        return gkt * jnp.exp(fk[:, f * c:(f + 1) * c])

    q_var = [(q * jnp.exp(fq[vi * c:(vi + 1) * c, :])).astype(BF16) for vi in range(n_var - 1)]
    q_var.append(q.astype(BF16))
    a = jnp.zeros((c, GLA_HEADS * c), F32)
    for li in range(n_lev):
        kt = (key_factor(li) if li < n_lev - 1 else gkt).astype(BF16)
        xt = jnp.concatenate([kt] * GLA_HEADS, axis=1) * bd
        vis = [vi for vi in range(n_var) if levels_of[vi] == li]
        res = jnp.dot(jnp.concatenate([q_var[vi] for vi in vis], axis=0), xt,
                      preferred_element_type=F32)
        for r, vi in enumerate(vis):
            a = a + masks_ref[vi] * res[r * c:(r + 1) * c, :]
    q_in = (q * jnp.exp(fq[(n_var - 1) * c:n_var * c, :])).astype(BF16)
    state = s_ref[...]
    v_bd = jnp.concatenate([v] * GLA_HEADS, axis=0) * vbd
    o = (jnp.dot(q_in, state.astype(BF16), preferred_element_type=F32)
         + jnp.dot(a.astype(BF16), v_bd, preferred_element_type=F32))
    k_out = key_factor(n_lev - 1).astype(BF16)
    e_tot = jnp.exp(fk[:, n_lev * c:(n_lev + 1) * c])
    upd = jnp.dot(k_out, v, preferred_element_type=F32)
    s_ref[...] = (state * jnp.concatenate([e_tot] * (GLA_WIDTH // c), axis=1)
                  + upd * bd.astype(F32))
    return o


def _gla_kernel(gq_f, la_f, gkt_f, lat_f, gv_f, gq_b, la_b, gkt_b, lat_b, gv_b,
                s0f_ref, s0b_ref, lq2f, lkt2f, mkf, lq2b, lkt2b, mkb, bd_ref, vbd_ref,
                of_ref, ob_ref, sf_ref, sb_ref, st_f, st_b, *, levels_of):
    n = pl.program_id(1)

    @pl.when(n == 0)
    def _():
        st_f[...] = jnp.zeros_like(st_f)
        st_b[...] = jnp.zeros_like(st_b)
        for h in range(GLA_HEADS):
            rows = slice(h * GLA_DK, (h + 1) * GLA_DK)
            cols = slice(h * GLA_DV, (h + 1) * GLA_DV)
            st_f[rows, cols] = s0f_ref[h]
            st_b[rows, cols] = s0b_ref[h]

    bd = bd_ref[...]
    vbd = vbd_ref[...]
    of_ref[...] = _gla_direction(gq_f[...], la_f[...], gkt_f[...], lat_f[...], gv_f[...],
                                 lq2f[...], lkt2f[...], mkf, bd, vbd, st_f, levels_of)
    ob_ref[...] = _gla_direction(gq_b[...], la_b[...], gkt_b[...], lat_b[...], gv_b[...],
                                 lq2b[...], lkt2b[...], mkb, bd, vbd, st_b, levels_of)

    @pl.when(n == pl.num_programs(1) - 1)
    def _():
        for h in range(GLA_HEADS):
            rows = slice(h * GLA_DK, (h + 1) * GLA_DK)
            cols = slice(h * GLA_DV, (h + 1) * GLA_DV)
            sf_ref[h] = st_f[rows, cols]
            sb_ref[h] = st_b[rows, cols]


def _gla(gq, la, gkt, lat, gv, s0f, s0b, consts, row0, n_batch, seq):
    (cf, cb), levels_of, bd, vbd = consts
    c = GLA_CHUNK
    nc = seq // c
    blk0 = row0 // c
    n_la_blocks_b = 1
    fwd = lambda b, n: blk0 + b * nc + n
    bwd = lambda b, n: blk0 + b * nc + (nc - 1 - n)

    def tok(w, which, col=0):
        return pl.BlockSpec((c, w), lambda b, n: (which(b, n), col))

    def tr(r, which, row=0):
        return pl.BlockSpec((r, c), lambda b, n: (row, which(b, n)))

    full = lambda a: pl.BlockSpec(a.shape, lambda b, n: (0,) * a.ndim)
    st_spec = pl.BlockSpec((None, GLA_HEADS, GLA_DK, GLA_DV), lambda b, n: (b, 0, 0, 0))
    in_specs = [tok(GLA_KW, fwd), tok(GLA_KW, fwd, 0), tr(GLA_KW, fwd), tr(GLA_KW, fwd, 0),
                tok(GLA_WIDTH, fwd),
                tok(GLA_KW, bwd), tok(GLA_KW, bwd, n_la_blocks_b), tr(GLA_KW, bwd),
                tr(GLA_KW, bwd, 1), tok(GLA_WIDTH, bwd),
                st_spec, st_spec,
                full(cf[0]), full(cf[1]), full(cf[2]), full(cb[0]), full(cb[1]), full(cb[2]),
                full(bd), full(vbd)]
    out_specs = (pl.BlockSpec((c, GLA_WIDTH), lambda b, n: (b * nc + n, 0)),
                 pl.BlockSpec((c, GLA_WIDTH), lambda b, n: (b * nc + (nc - 1 - n), 0)),
                 st_spec, st_spec)
    out_shape = (jax.ShapeDtypeStruct((n_batch * seq, GLA_WIDTH), F32),
                 jax.ShapeDtypeStruct((n_batch * seq, GLA_WIDTH), F32),
                 jax.ShapeDtypeStruct((n_batch, GLA_HEADS, GLA_DK, GLA_DV), F32),
                 jax.ShapeDtypeStruct((n_batch, GLA_HEADS, GLA_DK, GLA_DV), F32))
    return pl.pallas_call(
        functools.partial(_gla_kernel, levels_of=levels_of),
        grid=(n_batch, nc), in_specs=in_specs, out_specs=out_specs, out_shape=out_shape,
        scratch_shapes=[pltpu.VMEM((GLA_KW, GLA_WIDTH), F32), pltpu.VMEM((GLA_KW, GLA_WIDTH), F32)],
        compiler_params=_cparams(2), name="gla",
    )(gq, la, gkt, lat, gv, gq, la, gkt, lat, gv, s0f, s0b,
      cf[0], cf[1], cf[2], cb[0], cb[1], cb[2], bd, vbd)


def _outproj_kernel(attc_ref, attl_ref, ofc_ref, ofl_ref, obc_ref, obl_ref, gg_ref, xc_ref, xl_ref,
                    mod_ref, wo_ref, gn_ref, l1g_ref, l1b_ref, wrt_ref, sw13_ref, sw2_ref,
                    base_ref, u2_ref, lg_ref, *, n_ctx_tiles):
    is_ctx = pl.program_id(0) < n_ctx_tiles
    pick = lambda a_ref, b_ref: jnp.where(is_ctx, a_ref[...], b_ref[...])
    m = mod_ref[...]
    gate1 = m[:, 2 * D_MODEL:3 * D_MODEL]
    shift2 = m[:, 3 * D_MODEL:4 * D_MODEL]
    scale2 = m[:, 4 * D_MODEL:5 * D_MODEL]
    gate2 = m[:, 5 * D_MODEL:6 * D_MODEL]
    og = pick(ofc_ref, ofl_ref) + pick(obc_ref, obl_ref)
    gg = gg_ref[...].astype(F32)
    parts = []
    for h in range(GLA_HEADS):
        blk = og[:, h * GLA_DV:(h + 1) * GLA_DV]
        ms = jnp.mean(blk * blk, axis=-1, keepdims=True)
        nb = blk * lax.rsqrt(ms + EPS) * gn_ref[...]
        parts.append((nb * _silu(gg[:, h * GLA_DV:(h + 1) * GLA_DV])).astype(BF16))
    att_t = pick(attc_ref, attl_ref)
    hmix = (lax.dot_general(att_t, wo_ref[0:ATT_WIDTH, :], (((0,), (0,)), ((), ())),
                            preferred_element_type=F32)
            + jnp.dot(jnp.concatenate(parts, axis=1), wo_ref[ATT_WIDTH:, :],
                      preferred_element_type=F32))
    x1 = _layer_norm(ALPHA * pick(xc_ref, xl_ref) + gate1 * hmix, l1g_ref[...], l1b_ref[...])
    u2 = x1 * (1.0 + scale2) + shift2
    u2b = u2.astype(BF16)
    lg_ref[...] = lax.dot_general(wrt_ref[...], u2b, (((1,), (1,)), ((), ())),
                                  preferred_element_type=F32)
    ab = jnp.dot(u2b, sw13_ref[...], preferred_element_type=F32)
    hid = (_silu(ab[:, 0:SHARED_FF]) * ab[:, SHARED_FF:2 * SHARED_FF]).astype(BF16)
    shared = jnp.dot(hid, sw2_ref[...], preferred_element_type=F32)
    base_ref[...] = ALPHA * x1 + gate2 * shared
    _pack_rows(u2_ref, u2)


def _out_projection(att_c, att_l, of_c, of_l, ob_c, ob_l, gg, x_c, x_l, mod3, wo, gn, l1g, l1b, wrt,
                    sw13, sw2, n_seq_tiles):
    n_ctx = x_c.shape[0]
    n = n_ctx + x_l.shape[0]
    tb = TOK_TILE
    n_ctx_tiles = n_ctx // tb

    def mod_idx(i):
        return (jnp.where(i < n_ctx_tiles, 0, 1 + (i - n_ctx_tiles) // n_seq_tiles), 0, 0)

    ctx_blk = lambda i: jnp.minimum(i, n_ctx_tiles - 1)
    lat_blk = lambda i: jnp.maximum(i - n_ctx_tiles, 0)
    tok = lambda w: pl.BlockSpec((tb, w), lambda i: (i, 0))
    tok_c = lambda w: pl.BlockSpec((tb, w), lambda i: (ctx_blk(i), 0))
    tok_l = lambda w: pl.BlockSpec((tb, w), lambda i: (lat_blk(i), 0))
    full = lambda a: pl.BlockSpec(a.shape, lambda i: (0,) * a.ndim)
    return pl.pallas_call(
        functools.partial(_outproj_kernel, n_ctx_tiles=n_ctx_tiles),
        grid=(n // tb,),
        in_specs=[pl.BlockSpec((ATT_WIDTH, tb), lambda i: (0, ctx_blk(i))),
                  pl.BlockSpec((ATT_WIDTH, tb), lambda i: (0, lat_blk(i))),
                  tok_c(GLA_WIDTH), tok_l(GLA_WIDTH), tok_c(GLA_WIDTH), tok_l(GLA_WIDTH),
                  tok(GLA_WIDTH), tok_c(D_MODEL), tok_l(D_MODEL),
                  pl.BlockSpec((None, 1, mod3.shape[2]), mod_idx),
                  full(wo), full(gn), full(l1g), full(l1b), full(wrt), full(sw13), full(sw2)],
        out_specs=(tok(D_MODEL),
                   pl.BlockSpec((tb * PACK_CHUNKS, LANES), lambda i: (i, 0)),
                   pl.BlockSpec((N_EXPERTS, tb), lambda i: (0, i))),
        out_shape=(jax.ShapeDtypeStruct((n, D_MODEL), F32),
                   jax.ShapeDtypeStruct((n * PACK_CHUNKS, LANES), U32),
                   jax.ShapeDtypeStruct((N_EXPERTS, n), F32)),
        compiler_params=_cparams(1), name="out_projection",
    )(att_c, att_l, of_c, of_l, ob_c, ob_l, gg, x_c, x_l, mod3, wo, gn, l1g, l1b, wrt, sw13, sw2)


def _route_kernel(lg_ref, bias_ref, upper_ref, idx_ref, w_ref, pos_ref, cnt_ref, run_ref):
    i = pl.program_id(0)

    @pl.when(i == 0)
    def _():
        run_ref[...] = jnp.zeros_like(run_ref)

    s = jax.nn.sigmoid(lg_ref[...])
    work = s + bias_ref[...]
    rows = lax.broadcasted_iota(I32, s.shape, 0).astype(F32)
    sel = jnp.zeros(s.shape, F32)
    idxs, vals = [], []
    for _ in range(TOP_K):
        mx = jnp.max(work, axis=0, keepdims=True)
        idx = jnp.min(jnp.where(work == mx, rows, float(N_EXPERTS)), axis=0, keepdims=True)
        hit = rows == idx
        vals.append(jnp.sum(jnp.where(hit, s, 0.0), axis=0, keepdims=True))
        idxs.append(idx)
        sel = jnp.where(hit, 1.0, sel)
        work = jnp.where(hit, -jnp.inf, work)
    den = functools.reduce(jnp.add, vals)
    rank = jnp.dot(sel.astype(BF16), upper_ref[...], preferred_element_type=F32) + run_ref[:, 0:1]
    for k in range(TOP_K):
        idx_ref[k:k + 1, :] = idxs[k].astype(I32)
        w_ref[k:k + 1, :] = vals[k] / den * ROUTED_SCALE
        pos_ref[k:k + 1, :] = jnp.sum(jnp.where(rows == idxs[k], rank, 0.0), axis=0,
                                      keepdims=True).astype(I32)
    run_ref[...] = run_ref[...] + jnp.sum(sel, axis=1, keepdims=True)
    cnt_ref[...] = run_ref[...]


def _route(logits_t, bias_col, upper):
    n = logits_t.shape[1]
    tt = TOK_TILE
    row = lambda dt: jax.ShapeDtypeStruct((TOP_K, n), dt)
    blk = pl.BlockSpec((TOP_K, tt), lambda i: (0, i))
    return pl.pallas_call(
        _route_kernel,
        grid=(n // tt,),
        in_specs=[pl.BlockSpec((N_EXPERTS, tt), lambda i: (0, i)),
                  pl.BlockSpec((N_EXPERTS, 1), lambda i: (0, 0)),
                  pl.BlockSpec((tt, tt), lambda i: (0, 0))],
        out_specs=(blk, blk, blk, pl.BlockSpec((N_EXPERTS, LANES), lambda i: (0, 0))),
        out_shape=(row(I32), row(F32), row(I32), jax.ShapeDtypeStruct((N_EXPERTS, LANES), F32)),
        scratch_shapes=[pltpu.VMEM((N_EXPERTS, LANES), F32)],
        compiler_params=_cparams(1), name="route",
    )(logits_t, bias_col, upper)


def _dest_kernel(cnt_ref, lower_ref, idx_ref, pos_ref, dest_ref, bexp_ref, bval_ref, nused_ref):
    cnt = cnt_ref[...]
    nblk = jnp.floor((cnt + (MOE_ROWS - 1)) * (1.0 / MOE_ROWS))
    bstart = jnp.dot(lower_ref[...], nblk, precision=HIGHEST, preferred_element_type=F32)
    bend = bstart + nblk
    pstart = bstart[:, 0:1] * MOE_ROWS
    rows = lax.broadcasted_iota(I32, (N_EXPERTS, idx_ref.shape[1]), 0)
    for k in range(TOP_K):
        hit = rows == idx_ref[k:k + 1, :]
        dest_ref[k:k + 1, :] = (jnp.sum(jnp.where(hit, pstart, 0.0), axis=0, keepdims=True)
                                .astype(I32) + pos_ref[k:k + 1, :])

    @pl.when(pl.program_id(0) == 0)
    def _():
        nb = bexp_ref.shape[1]
        bid = lax.broadcasted_iota(I32, (N_EXPERTS, nb), 1).astype(F32)
        inside = jnp.logical_and(bid >= bstart[:, 0:1], bid < bend[:, 0:1])
        erow = lax.broadcasted_iota(I32, (N_EXPERTS, nb), 0).astype(F32)
        bexp_ref[...] = jnp.sum(jnp.where(inside, erow, 0.0), axis=0, keepdims=True).astype(I32)
        valid = jnp.clip(cnt[:, 0:1] - (bid - bstart[:, 0:1]) * MOE_ROWS, 0.0, float(MOE_ROWS))
        bval_ref[...] = jnp.sum(jnp.where(inside, valid, 0.0), axis=0, keepdims=True).astype(I32)
        nused_ref[...] = jnp.max(bend, axis=0, keepdims=True).astype(I32)


def _destinations(counts, lower, idx_t, pos_t, n_blocks_pad):
    n = idx_t.shape[1]
    tt = TOK_TILE
    blk = pl.BlockSpec((TOP_K, tt), lambda i: (0, i))
    one = lambda w: pl.BlockSpec((1, w), lambda i: (0, 0))
    return pl.pallas_call(
        _dest_kernel,
        grid=(n // tt,),
        in_specs=[pl.BlockSpec((N_EXPERTS, LANES), lambda i: (0, 0)),
                  pl.BlockSpec((N_EXPERTS, N_EXPERTS), lambda i: (0, 0)), blk, blk],
        out_specs=(blk, one(n_blocks_pad), one(n_blocks_pad), one(LANES)),
        out_shape=(jax.ShapeDtypeStruct((TOP_K, n), I32),
                   jax.ShapeDtypeStruct((1, n_blocks_pad), I32),
                   jax.ShapeDtypeStruct((1, n_blocks_pad), I32),
                   jax.ShapeDtypeStruct((1, LANES), I32)),
        compiler_params=_cparams(1), name="destinations",
    )(counts, lower, idx_t, pos_t)


SC_WINDOW = 128
SC_WINDOWS_PER_STEP = 8


def _invert_rows(dest_flat, n_rows):
    m = dest_flat.shape[0]
    mesh = plsc.VectorSubcoreMesh(core_axis_name="core", subcore_axis_name="subcore")

    @functools.partial(pl.kernel, out_type=jax.ShapeDtypeStruct((n_rows,), I32), mesh=mesh,
                       scratch_types=[])
    def invert(val_hbm, idx_hbm, out_hbm):
        def body(val_vmem, idx_vmem):
            for j in range(SC_WINDOWS_PER_STEP):
                pltpu.sync_copy(val_vmem.at[j], out_hbm.at[idx_vmem.at[j]])

        blk = pl.BlockSpec((SC_WINDOWS_PER_STEP, SC_WINDOW), lambda i: (i, 0))
        pltpu.emit_pipeline(
            body, grid=(m // (SC_WINDOW * SC_WINDOWS_PER_STEP),),
            in_specs=[blk, blk], out_specs=[], core_axis_name=("core", "subcore"),
            dimension_semantics=(pltpu.PARALLEL,),
        )(val_hbm, idx_hbm)

    shape = (m // SC_WINDOW, SC_WINDOW)
    return invert(jnp.arange(m, dtype=I32).reshape(shape), dest_flat.reshape(shape))


PACK_CHUNKS = D_MODEL // (2 * LANES)
SRC_GROUP = 4
TOP_K_LOG2 = TOP_K.bit_length() - 1
PACK_CHUNKS_LOG2 = PACK_CHUNKS.bit_length() - 1
U32 = jnp.uint32


def _pack_rows(ref, x, row0=0):
    bits = pltpu.bitcast(x.astype(BF16).astype(F32), U32)
    for s in range(PACK_CHUNKS):
        lo = bits[:, (2 * s) * LANES:(2 * s + 1) * LANES] >> 16
        hi = bits[:, (2 * s + 1) * LANES:(2 * s + 2) * LANES] & jnp.uint32(0xFFFF0000)
        ref[pl.ds(row0 + s, x.shape[0], stride=PACK_CHUNKS), :] = lo | hi


def _unpack_rows(ref, n_rows, row0=0):
    parts = []
    for s in range(PACK_CHUNKS):
        w = ref[pl.ds(row0 + s, n_rows, stride=PACK_CHUNKS), :]
        parts.append(pltpu.bitcast(w << 16, F32))
        parts.append(pltpu.bitcast(w & jnp.uint32(0xFFFF0000), F32))
    return jnp.concatenate(parts, axis=1).astype(BF16)


def _moe_kernel(bexp_ref, bval_ref, nused_ref, u2p_hbm, src_hbm, w1_hbm, w3_hbm, w2_hbm, yt_hbm,
                u2p_vmem, w1_f, w3_f, w2_f, w13_s, w2_s, xbuf, ybuf, src_smem,
                sem_in, sem_src, sem_w, sem_out, *, n_tokens):
    n_used = nused_ref[0]
    br = MOE_ROWS
    grp = SRC_GROUP * br
    trash0 = n_tokens * TOP_K

    def src_copy(g):
        return pltpu.make_async_copy(src_hbm.at[pl.ds(g * grp, grp)],
                                     src_smem.at[pl.ds(lax.rem(g, 2) * grp, grp)], sem_src)

    def out_wait(slot):
        pltpu.make_async_copy(ybuf.at[pl.ds(slot * br * PACK_CHUNKS, br * PACK_CHUNKS)],
                              yt_hbm.at[pl.ds(0, br * PACK_CHUNKS)], sem_out.at[slot]).wait()

    def src_base(blk):
        return lax.rem(blk // SRC_GROUP, 2) * grp + lax.rem(blk, SRC_GROUP) * br

    def scatter_row(blk_slot, sbase, valid, r):
        dst = jnp.where(r < valid, src_smem[sbase + r], trash0 + blk_slot * br + r)
        pltpu.make_async_copy(
            ybuf.at[pl.ds(pl.multiple_of((blk_slot * br + r) * PACK_CHUNKS, PACK_CHUNKS), PACK_CHUNKS)],
            yt_hbm.at[pl.ds(pl.multiple_of(dst * PACK_CHUNKS, PACK_CHUNKS), PACK_CHUNKS)],
            sem_out.at[blk_slot]).start(priority=r % 2 if isinstance(r, int) else 0)

    def gather_row(xslot, sbase, r):
        row = lax.shift_right_logical(src_smem[sbase + r], TOP_K_LOG2 - PACK_CHUNKS_LOG2)
        row = jnp.minimum(row & (-PACK_CHUNKS & 0x7FFFFFFF), (n_tokens - 1) * PACK_CHUNKS)
        dst = pl.multiple_of((xslot * br + r) * PACK_CHUNKS, PACK_CHUNKS)
        xbuf[pl.ds(dst, PACK_CHUNKS), :] = u2p_vmem[pl.ds(pl.multiple_of(row, PACK_CHUNKS), PACK_CHUNKS), :]

    def weight_copies(e, wslot):
        return [pltpu.make_async_copy(src.at[e], dst.at[wslot], sem_w.at[wslot])
                for src, dst in ((w1_hbm, w1_f), (w3_hbm, w3_f), (w2_hbm, w2_f))]

    cp = pltpu.make_async_copy(u2p_hbm, u2p_vmem, sem_in)
    cp.start()
    src_copy(0).start()
    for wcp in weight_copies(bexp_ref[0], 0):
        wcp.start()
    ybuf[...] = jnp.zeros_like(ybuf)
    cp.wait()
    src_copy(0).wait()
    lax.fori_loop(0, br, lambda r, c: (gather_row(0, 0, r), c)[1], 0)

    def block(b, wslot):
        g = b // SRC_GROUP
        phase = lax.rem(b, SRC_GROUP)

        more = (g + 1) * SRC_GROUP < n_used

        @pl.when(jnp.logical_and(phase == 1, more))
        def _():
            src_copy(g + 1).start()

        @pl.when(jnp.logical_and(phase == SRC_GROUP - 1, more))
        def _():
            src_copy(g + 1).wait()

        e = bexp_ref[b]
        prev = bexp_ref[jnp.maximum(b - 1, 0)]

        @pl.when(jnp.logical_or(b == 0, e != prev))
        def _():
            for wcp in weight_copies(e, wslot):
                wcp.wait()
            w13_s[:, 0:EXPERT_FF] = w1_f[wslot].astype(BF16)
            w13_s[:, EXPERT_FF:2 * EXPERT_FF] = w3_f[wslot].astype(BF16)
            w2_s[...] = w2_f[wslot].astype(BF16)
            nxt = lax.while_loop(
                lambda j: jnp.logical_and(j < n_used, bexp_ref[jnp.minimum(j, n_used - 1)] == e),
                lambda j: j + 1, b + 1)

            @pl.when(nxt < n_used)
            def _():
                for wcp in weight_copies(bexp_ref[nxt], 1 - wslot):
                    wcp.start()

        switch = jnp.logical_and(b + 1 < n_used, bexp_ref[b + 1] != e)

        valid = bval_ref[b]
        sbase = src_base(b)
        slot = lax.rem(b, 2)

        pb = jnp.maximum(b - 1, 0)
        p_valid = jnp.where(b > 0, bval_ref[pb], 0)
        p_base = src_base(pb)
        n_base = src_base(b + 1)
        for r in range(br):
            gather_row(1 - slot, n_base, r)
            scatter_row(1 - slot, p_base, p_valid, r)

        x = _unpack_rows(xbuf, br, slot * (br * PACK_CHUNKS))
        rows = lax.broadcasted_iota(I32, x.shape, 0)
        x = jnp.where(rows < valid, x, jnp.zeros_like(x))
        ab = jnp.dot(x, w13_s[...], preferred_element_type=F32)
        hid = (_silu(ab[:, 0:EXPERT_FF]) * ab[:, EXPERT_FF:2 * EXPERT_FF]).astype(BF16)
        y = jnp.dot(hid, w2_s[...], preferred_element_type=F32)

        @pl.when(b >= 1)
        def _():
            out_wait(slot)

        _pack_rows(ybuf, y, slot * (br * PACK_CHUNKS))
        return jnp.where(switch, 1 - wslot, wslot)

    lax.fori_loop(0, n_used, block, 0)

    last = n_used - 1
    l_slot = lax.rem(last, 2)
    l_base = src_base(last)
    l_valid = bval_ref[last]
    lax.fori_loop(0, br, lambda r, c: (scatter_row(l_slot, l_base, l_valid, r), c)[1], 0)
    out_wait(1 - l_slot)
    out_wait(l_slot)


def _moe_experts(bexp, bval, nused, u2p, row_src, w1, w3, w2, n_blocks, n_tokens):
    br = MOE_ROWS
    any_spec = pl.BlockSpec(memory_space=pl.ANY)
    grid_spec = pltpu.PrefetchScalarGridSpec(
        num_scalar_prefetch=3, grid=(1,),
        in_specs=[any_spec] * 5,
        out_specs=any_spec,
        scratch_shapes=[pltpu.VMEM(u2p.shape, U32),
                        pltpu.VMEM((2, D_MODEL, EXPERT_FF), F32),
                        pltpu.VMEM((2, D_MODEL, EXPERT_FF), F32),
                        pltpu.VMEM((2, EXPERT_FF, D_MODEL), F32),
                        pltpu.VMEM((D_MODEL, 2 * EXPERT_FF), BF16),
                        pltpu.VMEM((EXPERT_FF, D_MODEL), BF16),
                        pltpu.VMEM((2 * PACK_CHUNKS * br, LANES), U32),
                        pltpu.VMEM((2 * br * PACK_CHUNKS, LANES), U32),
                        pltpu.SMEM((2 * SRC_GROUP * br,), I32),
                        pltpu.SemaphoreType.DMA, pltpu.SemaphoreType.DMA,
                        pltpu.SemaphoreType.DMA((2,)), pltpu.SemaphoreType.DMA((2,))])
    n_out_tiles = n_tokens * TOP_K + 2 * br
    return pl.pallas_call(
        functools.partial(_moe_kernel, n_tokens=n_tokens), grid_spec=grid_spec,
        out_shape=jax.ShapeDtypeStruct((n_out_tiles * PACK_CHUNKS, LANES), U32),
        compiler_params=pltpu.CompilerParams(dimension_semantics=("arbitrary",),
                                             vmem_limit_bytes=MOE_VMEM_LIMIT),
        name="moe_experts",
    )(bexp, bval, nused, u2p, row_src, w1, w3, w2)


def _combine_kernel(w_hbm, yt_ref, base_ref, mod_ref, g_ref, b_ref, yc_ref, yl_ref,
                    w_smem, acc_lo, acc_hi, sem_w, *, n_ctx_tiles):
    i = pl.program_id(0)
    n_steps = pl.num_programs(0)
    n_tok = acc_lo.shape[0] // PACK_CHUNKS
    n_idx = n_tok * TOP_K

    def w_copy(tile):
        return pltpu.make_async_copy(w_hbm.at[pl.ds(tile * n_idx, n_idx)],
                                     w_smem.at[pl.ds(lax.rem(tile, 2) * n_idx, n_idx)], sem_w)

    @pl.when(i == 0)
    def _():
        w_copy(i).start()

    w_copy(i).wait()

    @pl.when(i + 1 < n_steps)
    def _():
        w_copy(i + 1).start()

    wbase = lax.rem(i, 2) * n_idx

    per_tile = SUBLANES // PACK_CHUNKS
    first = lax.broadcasted_iota(I32, (SUBLANES, LANES), 0) < PACK_CHUNKS

    def reduce_token(t):
        lo = hi = None
        for m in range(TOP_K // per_tile):
            j = t * TOP_K + m * per_tile
            words = yt_ref[pl.ds(pl.multiple_of(j * PACK_CHUNKS, SUBLANES), SUBLANES), :]
            wgt = jnp.where(first, w_smem[wbase + j], w_smem[wbase + j + 1])
            t_lo = wgt * pltpu.bitcast(words << 16, F32)
            t_hi = wgt * pltpu.bitcast(words & jnp.uint32(0xFFFF0000), F32)
            lo = t_lo if lo is None else lo + t_lo
            hi = t_hi if hi is None else hi + t_hi
        row = pl.multiple_of(t * PACK_CHUNKS, PACK_CHUNKS)
        acc_lo[pl.ds(row, PACK_CHUNKS), :] = lo[0:PACK_CHUNKS] + lo[PACK_CHUNKS:SUBLANES]
        acc_hi[pl.ds(row, PACK_CHUNKS), :] = hi[0:PACK_CHUNKS] + hi[PACK_CHUNKS:SUBLANES]

    def reduce_group(i, carry):
        for u in range(COMB_UNROLL):
            reduce_token(i * COMB_UNROLL + u)
        return carry

    lax.fori_loop(0, n_tok // COMB_UNROLL, reduce_group, 0)
    parts = []
    for s in range(PACK_CHUNKS):
        parts.append(acc_lo[pl.ds(s, n_tok, stride=PACK_CHUNKS), :])
        parts.append(acc_hi[pl.ds(s, n_tok, stride=PACK_CHUNKS), :])
    moe = jnp.concatenate(parts, axis=1)
    gate2 = mod_ref[:, 5 * D_MODEL:6 * D_MODEL]
    y = _layer_norm(base_ref[...] + gate2 * moe, g_ref[...], b_ref[...])

    @pl.when(i < n_ctx_tiles)
    def _():
        yc_ref[...] = y

    @pl.when(i >= n_ctx_tiles)
    def _():
        yl_ref[...] = y


def _combine(w_flat, yt, base, mod3, l2g, l2b, n_ctx, seq_tokens):
    n = base.shape[0]
    tc = COMB_TILE
    n_ctx_tiles = n_ctx // tc
    n_seq_tiles = seq_tokens // tc

    def mod_idx(i):
        return (jnp.where(i < n_ctx_tiles, 0, 1 + (i - n_ctx_tiles) // n_seq_tiles), 0, 0)

    full = lambda a: pl.BlockSpec(a.shape, lambda i: (0,) * a.ndim)
    return pl.pallas_call(
        functools.partial(_combine_kernel, n_ctx_tiles=n_ctx_tiles),
        grid=(n // tc,),
        in_specs=[pl.BlockSpec(memory_space=pl.ANY),
                  pl.BlockSpec((tc * TOP_K * PACK_CHUNKS, LANES), lambda i: (i, 0)),
                  pl.BlockSpec((tc, D_MODEL), lambda i: (i, 0)),
                  pl.BlockSpec((None, 1, mod3.shape[2]), mod_idx), full(l2g), full(l2b)],
        out_specs=(pl.BlockSpec((tc, D_MODEL), lambda i: (jnp.minimum(i, n_ctx_tiles - 1), 0)),
                   pl.BlockSpec((tc, D_MODEL), lambda i: (jnp.maximum(i - n_ctx_tiles, 0), 0))),
        out_shape=(jax.ShapeDtypeStruct((n_ctx, D_MODEL), F32),
                   jax.ShapeDtypeStruct((n - n_ctx, D_MODEL), F32)),
        scratch_shapes=[pltpu.SMEM((2 * tc * TOP_K,), F32),
                        pltpu.VMEM((tc * PACK_CHUNKS, LANES), F32),
                        pltpu.VMEM((tc * PACK_CHUNKS, LANES), F32),
                        pltpu.SemaphoreType.DMA],
        compiler_params=_cparams(1), name="combine",
    )(w_flat, yt, base, mod3, l2g, l2b)


def _rope_tables(n_tok, tile):
    rows = n_tok // GRID_W
    row_idx = jnp.repeat(jnp.arange(rows, dtype=F32), GRID_W)
    col_idx = jnp.tile(jnp.arange(GRID_W, dtype=F32), rows)
    inv_freq = 1.0 / (ROPE_THETA ** (jnp.arange(0, ROPE_AXIS_DIM, 2, dtype=F32) / ROPE_AXIS_DIM))
    ang_r = row_idx[:, None] * inv_freq[None, :]
    ang_c = col_idx[:, None] * inv_freq[None, :]
    ang = jnp.concatenate([ang_r, ang_r, ang_c, ang_c], axis=-1)
    cos, sin = jnp.cos(ang), jnp.sin(ang)
    quarter = (jnp.arange(HEAD_DIM) // (ROPE_AXIS_DIM // 2)) % 2
    sin_a = jnp.where(quarter == 0, -sin, 0.0)
    sin_b = jnp.where(quarter == 1, sin, 0.0)
    rep = LANES // HEAD_DIM
    ident = lambda v: jnp.full((tile, LANES), v, F32)
    cos_t = jnp.concatenate([jnp.tile(cos, (1, rep)), ident(1.0)], axis=0)
    sa_t = jnp.concatenate([jnp.tile(sin_a, (1, rep)), ident(0.0)], axis=0)
    sb_t = jnp.concatenate([jnp.tile(sin_b, (1, rep)), ident(0.0)], axis=0)
    ident_tr = lambda v: jnp.full((HEAD_DIM, tile), v, F32)
    cos_tr = jnp.concatenate([cos.T, ident_tr(1.0)], axis=1)
    sin_tr = jnp.concatenate([sin.T, ident_tr(0.0)], axis=1)
    return cos_t, sa_t, sb_t, cos_tr, sin_tr


def _dup_heads(a):
    parts = []
    for h in range(KV_HEADS):
        blk = a[..., h * HEAD_DIM:(h + 1) * HEAD_DIM]
        parts += [blk] * (LANES // HEAD_DIM)
    return jnp.concatenate(parts, axis=-1)


def kernel(x_prompt, x_sample, cache_k, cache_v, state_gla_fwd, state_gla_bwd, c, c_ctx, w_ada, b_ada, w_in, q_norm, k_norm, gla_wa_fwd, gla_ba_fwd, gla_wa_bwd, gla_ba_bwd, gla_norm, w_out, ln1_g, ln1_b, ln2_g, ln2_b, w_router, router_bias, exp_w1, exp_w3, exp_w2, sh_w1, sh_w3, sh_w2):
    n_ctx_b, ctx_seq, _ = x_prompt.shape
    n_lat_b, lat_seq, _ = x_sample.shape
    n_ctx = n_ctx_b * ctx_seq
    n_lat = n_lat_b * lat_seq
    n = n_ctx + n_lat
    l = 0

    x_c = x_prompt.reshape(n_ctx, D_MODEL)
    x_l = x_sample.reshape(n_lat, D_MODEL)

    c_rows = jnp.zeros((SUBLANES, D_MODEL), F32).at[0].set(c_ctx).at[1:1 + n_lat_b].set(c)
    mod = _modulation(c_rows, w_ada[l], b_ada[l][None, :])
    mod3 = mod.reshape(SUBLANES, 1, 6 * D_MODEL)

    wi = w_in[l]
    o_q, o_k, o_v, o_gq, o_gk, o_gv, o_gg, o_rf, o_rb, o_end = np.cumsum(
        [0, ATT_WIDTH, KV_HEADS * HEAD_DIM, KV_HEADS * HEAD_DIM, GLA_KW, GLA_KW, GLA_WIDTH, GLA_WIDTH,
         GLA_GATE_RANK, GLA_GATE_RANK])
    w_tok = jnp.concatenate([
        _dup_heads(wi[:, o_k:o_v]), wi[:, o_v:o_gq], wi[:, o_gq:o_gk],
        wi[:, o_gv:o_gg], wi[:, o_gg:o_rf], wi[:, o_rf:o_end],
        jnp.zeros((D_MODEL, LANES - 2 * GLA_GATE_RANK), F32)], axis=1).astype(BF16)
    w_tr = jnp.concatenate([wi[:, o_q:o_k], wi[:, o_v:o_gq], wi[:, o_gk:o_gv], wi[:, o_rf:o_end]],
                           axis=1).T.astype(BF16)
    rep = LANES // HEAD_DIM
    qn = q_norm[l][:, None]
    kn = jnp.tile(k_norm[l], rep)[None, :]
    seg = jnp.asarray(np.kron(np.eye(rep), np.ones((HEAD_DIM, HEAD_DIM))), BF16)
    wa = jnp.zeros((LANES, 2 * GLA_KW), F32)
    wa = wa.at[0:GLA_GATE_RANK, 0:GLA_KW].set(gla_wa_fwd[l])
    wa = wa.at[GLA_GATE_RANK:2 * GLA_GATE_RANK, GLA_KW:].set(gla_wa_bwd[l])
    ba = jnp.concatenate([gla_ba_fwd[l], gla_ba_bwd[l]])[None, :]
    wat = wa[0:2 * GLA_GATE_RANK, :].T
    bat = ba.T
    cos_t, sa_t, sb_t, cos_tr, sin_tr = _rope_tables(lat_seq, TOK_TILE)

    (qt, k_dup, vt, k32, v32, gq, gv, gg, la, gkt, lat) = _in_projection(
        x_c, x_l, mod3, w_tok, w_tr, qn, kn, cos_t, sa_t, sb_t, cos_tr, sin_tr, seg, wa, ba, wat, bat,
        lat_seq // TOK_TILE)

    ck = _dup_heads(cache_k[:, l].reshape(n_lat_b, -1, KV_HEADS * HEAD_DIM)).astype(BF16)
    cvt = cache_v[:, l].reshape(n_lat_b, -1, KV_HEADS * HEAD_DIM).transpose(0, 2, 1).astype(BF16)
    att_c = _attention(qt, k_dup, vt, None, 0, n_ctx_b, ctx_seq)
    att_l = _attention(qt, k_dup, vt, (ck, cvt), n_ctx, n_lat_b, lat_seq)

    gconst, levels_of = _gla_constants()
    to_dev = lambda t: (jnp.asarray(t[0], BF16), jnp.asarray(t[1], BF16), jnp.asarray(t[2], F32))
    bd = jnp.asarray(np.kron(np.eye(GLA_HEADS), np.ones((GLA_DK, GLA_DV))), BF16)
    vbd = jnp.asarray(np.kron(np.eye(GLA_HEADS), np.ones((GLA_CHUNK, GLA_DV))), BF16)
    consts = ((to_dev(gconst["f"]), to_dev(gconst["b"])), levels_of, bd, vbd)
    s_zero = jnp.zeros((n_ctx_b, GLA_HEADS, GLA_DK, GLA_DV), F32)
    of_c, ob_c, sf_new, sb_new = _gla(gq, la, gkt, lat, gv, s_zero, s_zero, consts, 0, n_ctx_b, ctx_seq)
    of_l, ob_l, _, _ = _gla(gq, la, gkt, lat, gv, state_gla_fwd[:, l], state_gla_bwd[:, l], consts,
                            n_ctx, n_lat_b, lat_seq)

    sw13 = jnp.concatenate([sh_w1[l], sh_w3[l]], axis=1).astype(BF16)
    base, u2_rows, logits_t = _out_projection(
        att_c, att_l, of_c, of_l, ob_c, ob_l, gg, x_c, x_l, mod3, w_out[l].astype(BF16),
        gla_norm[l][None, :], ln1_g[l][None, :], ln1_b[l][None, :], w_router[l].T.astype(BF16), sw13,
        sh_w2[l].astype(BF16), lat_seq // TOK_TILE)

    upper = jnp.asarray(np.triu(np.ones((TOK_TILE, TOK_TILE)), 1), BF16)
    idx_t, w_t, pos_t, counts = _route(logits_t, router_bias[l][:, None], upper)
    n_blocks = n * TOP_K // MOE_ROWS + N_EXPERTS
    n_blocks_pad = -(-n_blocks // LANES) * LANES
    lower = jnp.asarray(np.tril(np.ones((N_EXPERTS, N_EXPERTS)), -1), F32)
    dest_t, bexp, bval, nused = _destinations(counts, lower, idx_t, pos_t, n_blocks_pad)
    dest_flat = dest_t.T.reshape(-1)
    w_flat = w_t.T.reshape(-1)

    row_src = _invert_rows(dest_flat, n_blocks * MOE_ROWS)
    yt = _moe_experts(bexp.reshape(-1), bval.reshape(-1), nused.reshape(-1)[0:1], u2_rows, row_src,
                      exp_w1[l], exp_w3[l], exp_w2[l], n_blocks, n)
    y_c, y_l = _combine(w_flat, yt, base, mod3, ln2_g[l][None, :], ln2_b[l][None, :], n_ctx, lat_seq)

    y_prompt = y_c.reshape(n_ctx_b, ctx_seq, D_MODEL)
    y_sample = y_l.reshape(n_lat_b, lat_seq, D_MODEL)
    new_cache_k = k32.reshape(n_ctx_b, 1, ctx_seq, KV_HEADS, HEAD_DIM)
    new_cache_v = v32.reshape(n_ctx_b, 1, ctx_seq, KV_HEADS, HEAD_DIM)
    return (y_prompt, y_sample, new_cache_k, new_cache_v, sf_new[:, None], sb_new[:, None])
```

```python
import functools

import numpy as np
import jax
import jax.numpy as jnp
from jax import lax
from jax.experimental import pallas as pl
from jax.experimental.pallas import tpu as pltpu
from jax.experimental.pallas import tpu_sc as plsc

F32 = jnp.float32
BF16 = jnp.bfloat16
I32 = jnp.int32

D_MODEL = 1024
GRID_W = 64
HEAD_DIM = 64
N_HEADS = 8
KV_HEADS = 2
ATT_WIDTH = N_HEADS * HEAD_DIM
ATT_SCALE = HEAD_DIM ** -0.5
LOG2_E = 1.4426950408889634
ROPE_AXIS_DIM = HEAD_DIM // 2
ROPE_THETA = 10000.0
GLA_HEADS = 4
GLA_DK = 64
GLA_DV = 128
GLA_WIDTH = GLA_HEADS * GLA_DV
GLA_KW = GLA_HEADS * GLA_DK
GLA_GATE_RANK = 16
GLA_TAU = 16.0
N_EXPERTS = 256
TOP_K = 8
EXPERT_FF = 256
SHARED_FF = 256
ROUTED_SCALE = 2.5
DEPTH = 1
ALPHA = (2.0 * DEPTH) ** 0.25
EPS = 1e-6

LANES = 128
SUBLANES = 8
ROW_CHUNKS = D_MODEL // LANES
VMEM_LIMIT = 56 * 1024 * 1024

TOK_TILE = 512
ATT_TQ = 128
GLA_CHUNK = 128
GLA_LEVELS = ((32, 128), (8, 32), (2, 8), (1, 2))
MOE_ROWS = 256
MOE_VMEM_LIMIT = 62 * 1024 * 1024
COMB_TILE = 256
COMB_UNROLL = 8
HIGHEST = lax.Precision.HIGHEST


def _cparams(n_axes):
    return pltpu.CompilerParams(dimension_semantics=("arbitrary",) * n_axes,
                                vmem_limit_bytes=VMEM_LIMIT)


def _silu(x):
    return x * jax.nn.sigmoid(x)


def _log_sigmoid(x):
    return jnp.minimum(x, 0.0) - jnp.log(1.0 + jnp.exp(-jnp.abs(x)))


def _dot_split(a, b):
    a_hi = a.astype(BF16)
    b_hi = b.astype(BF16)
    a_lo = (a - a_hi.astype(F32)).astype(BF16)
    b_lo = (b - b_hi.astype(F32)).astype(BF16)
    dot = functools.partial(jnp.dot, preferred_element_type=F32)
    return dot(a_hi, b_hi) + dot(a_lo, b_hi) + dot(a_hi, b_lo)


def _load_row_tiles(ref, n_rows, row0=0):
    return jnp.concatenate(
        [ref[pl.ds(row0 * ROW_CHUNKS + cidx, n_rows, stride=ROW_CHUNKS), :] for cidx in range(ROW_CHUNKS)],
        axis=1)


def _store_row_tiles(ref, x):
    for cidx in range(ROW_CHUNKS):
        ref[pl.ds(cidx, x.shape[0], stride=ROW_CHUNKS), :] = x[:, cidx * LANES:(cidx + 1) * LANES]


def _layer_norm(z, g, b):
    mu = jnp.mean(z, axis=-1, keepdims=True)
    zc = z - mu
    var = jnp.mean(zc * zc, axis=-1, keepdims=True)
    return zc * lax.rsqrt(var + EPS) * g + b


def _mod_kernel(c_ref, w_ref, b_ref, o_ref):
    s = _silu(c_ref[...]).astype(BF16)
    o_ref[...] = jnp.dot(s, w_ref[...].astype(BF16), preferred_element_type=F32) + b_ref[...]


def _modulation(c_rows, w_ada, b_ada):
    n_cols = w_ada.shape[1]
    tn = 512
    return pl.pallas_call(
        _mod_kernel,
        grid=(n_cols // tn,),
        in_specs=[pl.BlockSpec((SUBLANES, D_MODEL), lambda j: (0, 0)),
                  pl.BlockSpec((D_MODEL, tn), lambda j: (0, j)),
                  pl.BlockSpec((1, tn), lambda j: (0, j))],
        out_specs=pl.BlockSpec((SUBLANES, tn), lambda j: (0, j)),
        out_shape=jax.ShapeDtypeStruct((SUBLANES, n_cols), F32),
        compiler_params=_cparams(1),
        name="modulation",
    )(c_rows, w_ada, b_ada)


_C_K = 0
_C_V = _C_K + 2 * LANES
_C_GQ = _C_V + KV_HEADS * HEAD_DIM
_C_GV = _C_GQ + GLA_KW
_C_GG = _C_GV + GLA_WIDTH
_C_RA = _C_GG + GLA_WIDTH
_C_END = _C_RA + LANES
_R_Q = 0
_R_V = _R_Q + ATT_WIDTH
_R_GK = _R_V + KV_HEADS * HEAD_DIM
_R_RA = _R_GK + GLA_KW
_R_END = _R_RA + 2 * GLA_GATE_RANK


def _inproj_kernel(xc_ref, xl_ref, mod_ref, w_ref, wt_ref, qn_ref, kn_ref, cos_ref, sa_ref, sb_ref,
                   cost_ref, sint_ref, seg_ref, wa_ref, ba_ref, wat_ref, bat_ref,
                   qt_ref, k_ref, vt_ref, k32_ref, v32_ref, gq_ref, gv_ref, gg_ref,
                   la_ref, gkt_ref, lat_ref, *, n_ctx_tiles):
    i = pl.program_id(0)
    m = mod_ref[...]
    shift1 = m[:, 0:D_MODEL]
    scale1 = m[:, D_MODEL:2 * D_MODEL]
    x = jnp.where(i < n_ctx_tiles, xc_ref[...], xl_ref[...])
    u = (x * (1.0 + scale1) + shift1).astype(BF16)

    cos = cos_ref[...]
    sin_a = sa_ref[...]
    sin_b = sb_ref[...]
    seg = seg_ref[...]
    lane = lax.broadcasted_iota(I32, (u.shape[0], LANES), 1)
    low = lane < HEAD_DIM

    def proj(c0, c1):
        return jnp.dot(u, w_ref[:, c0:c1], preferred_element_type=F32)

    def head_norm(blk, gain):
        ss = jnp.dot((blk * blk).astype(BF16), seg, preferred_element_type=F32) * (1.0 / HEAD_DIM)
        return blk * lax.rsqrt(ss + EPS) * gain

    def rope(blk):
        return (blk * cos + pltpu.roll(blk, LANES - ROPE_AXIS_DIM // 2, 1) * sin_a
                + pltpu.roll(blk, ROPE_AXIS_DIM // 2, 1) * sin_b)

    pk = proj(_C_K, _C_V)
    kn = [head_norm(pk[:, j * LANES:(j + 1) * LANES], kn_ref[...]) for j in range(KV_HEADS)]
    for j in range(KV_HEADS):
        k_ref[:, j * LANES:(j + 1) * LANES] = rope(kn[j]).astype(BF16)

    @pl.when(i < n_ctx_tiles)
    def _():
        k32_ref[...] = jnp.where(low, kn[0], kn[1])
        v32_ref[...] = proj(_C_V, _C_GQ)

    gq_ref[...] = proj(_C_GQ, _C_GV) * (GLA_DK ** -0.5)
    gv_ref[...] = proj(_C_GV, _C_GG).astype(BF16)
    gg_ref[...] = proj(_C_GG, _C_RA).astype(BF16)

    ra = proj(_C_RA, _C_END)
    pre = _dot_split(ra, wa_ref[...]) + ba_ref[...]
    la_ref[...] = _log_sigmoid(pre) * (1.0 / GLA_TAU)

    pt = lax.dot_general(wt_ref[...], u, (((1,), (1,)), ((), ())), preferred_element_type=F32)
    cos_t = cost_ref[...]
    sin_t = sint_ref[...]
    quarter = ROPE_AXIS_DIM // 2
    for h in range(N_HEADS):
        blk = pt[_R_Q + h * HEAD_DIM:_R_Q + (h + 1) * HEAD_DIM, :]
        ms = jnp.mean(blk * blk, axis=0, keepdims=True)
        qn = blk * lax.rsqrt(ms + EPS) * qn_ref[...]
        rot = jnp.concatenate([-qn[quarter:2 * quarter], qn[0:quarter],
                               -qn[3 * quarter:4 * quarter], qn[2 * quarter:3 * quarter]], axis=0)
        qt_ref[h * HEAD_DIM:(h + 1) * HEAD_DIM, :] = (
            (qn * cos_t + rot * sin_t) * (ATT_SCALE * LOG2_E)).astype(BF16)
    vt_ref[...] = pt[_R_V:_R_GK, :].astype(BF16)
    gkt_ref[...] = pt[_R_GK:_R_RA, :]
    rat = pt[_R_RA:_R_END, :]
    pre_t = _dot_split(wat_ref[...], rat) + bat_ref[...]
    lat_ref[...] = _log_sigmoid(pre_t) * (1.0 / GLA_TAU)


def _in_projection(x_c, x_l, mod3, w_tok, w_tr, qn, kn, cos_t, sa_t, sb_t, cos_tr, sin_tr, seg, wa, ba,
                   wat, bat, n_seq_tiles):
    n_ctx = x_c.shape[0]
    n = n_ctx + x_l.shape[0]
    tb = TOK_TILE
    n_ctx_tiles = n_ctx // tb
    n_tiles = n // tb
    n_rope_blocks = cos_t.shape[0] // tb - 1

    def mod_idx(i):
        return (jnp.where(i < n_ctx_tiles, 0, 1 + (i - n_ctx_tiles) // n_seq_tiles), 0, 0)

    def rope_blk(i):
        return jnp.where(i < n_ctx_tiles, n_rope_blocks, (i - n_ctx_tiles) % n_seq_tiles)

    def rope_idx(i):
        return (rope_blk(i), 0)

    def ctx_idx(i):
        return (jnp.minimum(i, n_ctx_tiles - 1), 0)

    tok = lambda w: pl.BlockSpec((tb, w), lambda i: (i, 0))
    full = lambda a: pl.BlockSpec(a.shape, lambda i: (0,) * a.ndim)
    tr = lambda r: pl.BlockSpec((r, tb), lambda i: (0, i))
    rope_tr = pl.BlockSpec((HEAD_DIM, tb), lambda i: (0, rope_blk(i)))
    out_shapes = (
        jax.ShapeDtypeStruct((ATT_WIDTH, n), BF16),
        jax.ShapeDtypeStruct((n, 2 * LANES), BF16),
        jax.ShapeDtypeStruct((KV_HEADS * HEAD_DIM, n), BF16),
        jax.ShapeDtypeStruct((n_ctx, LANES), F32),
        jax.ShapeDtypeStruct((n_ctx, LANES), F32),
        jax.ShapeDtypeStruct((n, GLA_KW), F32),
        jax.ShapeDtypeStruct((n, GLA_WIDTH), BF16),
        jax.ShapeDtypeStruct((n, GLA_WIDTH), BF16),
        jax.ShapeDtypeStruct((n, 2 * GLA_KW), F32),
        jax.ShapeDtypeStruct((GLA_KW, n), F32),
        jax.ShapeDtypeStruct((2 * GLA_KW, n), F32),
    )
    out_specs = (tr(ATT_WIDTH), tok(2 * LANES), tr(KV_HEADS * HEAD_DIM),
                 pl.BlockSpec((tb, LANES), ctx_idx), pl.BlockSpec((tb, LANES), ctx_idx),
                 tok(GLA_KW), tok(GLA_WIDTH), tok(GLA_WIDTH), tok(2 * GLA_KW),
                 tr(GLA_KW), tr(2 * GLA_KW))
    in_specs = [pl.BlockSpec((tb, D_MODEL), ctx_idx),
                pl.BlockSpec((tb, D_MODEL), lambda i: (jnp.maximum(i - n_ctx_tiles, 0), 0)),
                pl.BlockSpec((None, 1, mod3.shape[2]), mod_idx),
                full(w_tok), full(w_tr), full(qn), full(kn),
                pl.BlockSpec((tb, LANES), rope_idx), pl.BlockSpec((tb, LANES), rope_idx),
                pl.BlockSpec((tb, LANES), rope_idx), rope_tr, rope_tr,
                full(seg), full(wa), full(ba), full(wat), full(bat)]
    return pl.pallas_call(
        functools.partial(_inproj_kernel, n_ctx_tiles=n_ctx_tiles),
        grid=(n_tiles,), in_specs=in_specs, out_specs=out_specs, out_shape=out_shapes,
        compiler_params=_cparams(1), name="in_projection",
    )(x_c, x_l, mod3, w_tok, w_tr, qn, kn, cos_t, sa_t, sb_t, cos_tr, sin_tr, seg, wa, ba, wat, bat)


def _attention_kernel(*refs, n_kv_parts):
    qt_ref = refs[0]
    k_refs = refs[1:1 + n_kv_parts]
    vt_refs = refs[1 + n_kv_parts:1 + 2 * n_kv_parts]
    o_ref = refs[1 + 2 * n_kv_parts]
    tq = qt_ref.shape[1]
    group = N_HEADS // KV_HEADS
    scores = []
    for kv in range(KV_HEADS):
        heads = range(kv * group, (kv + 1) * group)
        q_grp = jnp.concatenate([qt_ref[h * HEAD_DIM:(h + 1) * HEAD_DIM, :] for h in heads], axis=1)
        rhs = jnp.concatenate([q_grp, jnp.zeros_like(q_grp)], axis=0)
        scores.append([jnp.dot(k[:, kv * LANES:(kv + 1) * LANES], rhs, preferred_element_type=F32)
                       for k in k_refs])
    for kv in range(KV_HEADS):
        heads = range(kv * group, (kv + 1) * group)
        s = scores[kv]
        mx = functools.reduce(jnp.maximum, [jnp.max(x, axis=0, keepdims=True) for x in s])
        pr = [jnp.exp2(x - mx) for x in s]
        den = functools.reduce(jnp.add, [jnp.sum(x, axis=0, keepdims=True) for x in pr])
        acc = functools.reduce(jnp.add, [
            jnp.dot(vt[kv * HEAD_DIM:(kv + 1) * HEAD_DIM, :], x.astype(BF16),
                    preferred_element_type=F32) for x, vt in zip(pr, vt_refs)])
        out = (acc / den).astype(BF16)
        for j, h in enumerate(heads):
            o_ref[h * HEAD_DIM:(h + 1) * HEAD_DIM, :] = out[:, j * tq:(j + 1) * tq]


def _attention(qt, k, vt, extra_kv, row0, n_batch, seq):
    tq = ATT_TQ
    n_q = seq // tq
    q_blk0 = row0 // tq
    kv_blk0 = row0 // seq
    in_specs = [pl.BlockSpec((ATT_WIDTH, tq), lambda b, i: (0, q_blk0 + b * n_q + i))]
    k_spec = pl.BlockSpec((seq, 2 * LANES), lambda b, i: (kv_blk0 + b, 0))
    vt_spec = pl.BlockSpec((KV_HEADS * HEAD_DIM, seq), lambda b, i: (0, kv_blk0 + b))
    args_k, args_v, specs_k, specs_v = [k], [vt], [k_spec], [vt_spec]
    if extra_kv is not None:
        ck, cvt = extra_kv
        args_k.append(ck)
        args_v.append(cvt)
        specs_k.append(pl.BlockSpec((None, ck.shape[1], 2 * LANES), lambda b, i: (b, 0, 0)))
        specs_v.append(pl.BlockSpec((None, KV_HEADS * HEAD_DIM, cvt.shape[2]), lambda b, i: (b, 0, 0)))
    return pl.pallas_call(
        functools.partial(_attention_kernel, n_kv_parts=len(args_k)),
        grid=(n_batch, n_q),
        in_specs=in_specs + specs_k + specs_v,
        out_specs=pl.BlockSpec((ATT_WIDTH, tq), lambda b, i: (0, b * n_q + i)),
        out_shape=jax.ShapeDtypeStruct((ATT_WIDTH, n_batch * seq), BF16),
        compiler_params=_cparams(2), name="attention",
    )(qt, *args_k, *args_v)


def _gla_constants():
    c = GLA_CHUNK
    idx = np.arange(c)
    q_mats, k_mats, masks, levels_of = [], [], [], []
    for li, (s, p) in enumerate(GLA_LEVELS):
        start = (idx // s) * s
        end = start + s - 1
        k_mats.append(((idx[None, :] > idx[:, None]) & (idx[None, :] <= end[:, None])))
        for d in range(p // s - 1):
            lo = np.maximum(start - d * s, 0)
            q_mats.append((idx[None, :] >= lo[:, None]) & (idx[None, :] <= idx[:, None]))
            masks.append((idx[:, None] // p == idx[None, :] // p)
                         & (idx[:, None] // s - idx[None, :] // s - 1 == d))
            levels_of.append(li)
    masks.append(np.eye(c, dtype=bool))
    levels_of.append(len(GLA_LEVELS) - 1)
    q_mats.append(idx[None, :] <= idx[:, None])
    k_mats = k_mats[:-1]
    k_mats.append(idx[None, :] > idx[:, None])
    k_mats.append(np.ones((c, c), bool))
    out = {}
    for name, flip in (("f", False), ("b", True)):
        f = (lambda a: a[::-1, ::-1]) if flip else (lambda a: a)
        lq = np.concatenate([f(a) for a in q_mats], axis=0).astype(np.float32)
        lkt = np.concatenate([f(a).T for a in k_mats], axis=1).astype(np.float32)
        mk = np.stack([np.tile(f(a), (1, GLA_HEADS)) for a in masks]).astype(np.float32)
        out[name] = (np.concatenate([lq, lq], axis=1), np.concatenate([lkt, lkt], axis=0), mk)
    return out, tuple(levels_of)


def _gla_direction(q, g, gkt, gt, v, lq2, lkt2, masks_ref, bd, vbd, s_ref, levels_of):
    c = GLA_CHUNK
    n_var = len(levels_of)
    n_lev = len(GLA_LEVELS)
    g_hi = g.astype(BF16)
    g_lo = (g - g_hi.astype(F32)).astype(BF16)
    fq = jnp.dot(lq2, jnp.concatenate([g_hi, g_lo], axis=0), preferred_element_type=F32)
    gt_hi = gt.astype(BF16)
    gt_lo = (gt - gt_hi.astype(F32)).astype(BF16)
    fk = jnp.dot(jnp.concatenate([gt_hi, gt_lo], axis=1), lkt2, preferred_element_type=F32)

    def key_factor(f):
        return gkt * jnp.exp(fk[:, f * c:(f + 1) * c])

    q_var = [(q * jnp.exp(fq[vi * c:(vi + 1) * c, :])).astype(BF16) for vi in range(n_var - 1)]
    q_var.append(q.astype(BF16))
    a = jnp.zeros((c, GLA_HEADS * c), F32)
    for li in range(n_lev):
        kt = (key_factor(li) if li < n_lev - 1 else gkt).astype(BF16)
        xt = jnp.concatenate([kt] * GLA_HEADS, axis=1) * bd
        vis = [vi for vi in range(n_var) if levels_of[vi] == li]
        res = jnp.dot(jnp.concatenate([q_var[vi] for vi in vis], axis=0), xt,
                      preferred_element_type=F32)
        for r, vi in enumerate(vis):
            a = a + masks_ref[vi] * res[r * c:(r + 1) * c, :]
    q_in = (q * jnp.exp(fq[(n_var - 1) * c:n_var * c, :])).astype(BF16)
    state = s_ref[...]
    v_bd = jnp.concatenate([v] * GLA_HEADS, axis=0) * vbd
    o = (jnp.dot(q_in, state.astype(BF16), preferred_element_type=F32)
         + jnp.dot(a.astype(BF16), v_bd, preferred_element_type=F32))
    k_out = key_factor(n_lev - 1).astype(BF16)
    e_tot = jnp.exp(fk[:, n_lev * c:(n_lev + 1) * c])
    upd = jnp.dot(k_out, v, preferred_element_type=F32)
    s_ref[...] = (state * jnp.concatenate([e_tot] * (GLA_WIDTH // c), axis=1)
                  + upd * bd.astype(F32))
    return o


def _gla_kernel(gq_f, la_f, gkt_f, lat_f, gv_f, gq_b, la_b, gkt_b, lat_b, gv_b,
                s0f_ref, s0b_ref, lq2f, lkt2f, mkf, lq2b, lkt2b, mkb, bd_ref, vbd_ref,
                of_ref, ob_ref, sf_ref, sb_ref, st_f, st_b, *, levels_of):
    n = pl.program_id(1)

    @pl.when(n == 0)
    def _():
        st_f[...] = jnp.zeros_like(st_f)
        st_b[...] = jnp.zeros_like(st_b)
        for h in range(GLA_HEADS):
            rows = slice(h * GLA_DK, (h + 1) * GLA_DK)
            cols = slice(h * GLA_DV, (h + 1) * GLA_DV)
            st_f[rows, cols] = s0f_ref[h]
            st_b[rows, cols] = s0b_ref[h]

    bd = bd_ref[...]
    vbd = vbd_ref[...]
    of_ref[...] = _gla_direction(gq_f[...], la_f[...], gkt_f[...], lat_f[...], gv_f[...],
                                 lq2f[...], lkt2f[...], mkf, bd, vbd, st_f, levels_of)
    ob_ref[...] = _gla_direction(gq_b[...], la_b[...], gkt_b[...], lat_b[...], gv_b[...],
                                 lq2b[...], lkt2b[...], mkb, bd, vbd, st_b, levels_of)

    @pl.when(n == pl.num_programs(1) - 1)
    def _():
        for h in range(GLA_HEADS):
            rows = slice(h * GLA_DK, (h + 1) * GLA_DK)
            cols = slice(h * GLA_DV, (h + 1) * GLA_DV)
            sf_ref[h] = st_f[rows, cols]
            sb_ref[h] = st_b[rows, cols]


def _gla(gq, la, gkt, lat, gv, s0f, s0b, consts, row0, n_batch, seq):
    (cf, cb), levels_of, bd, vbd = consts
    c = GLA_CHUNK
    nc = seq // c
    blk0 = row0 // c
    n_la_blocks_b = 1
    fwd = lambda b, n: blk0 + b * nc + n
    bwd = lambda b, n: blk0 + b * nc + (nc - 1 - n)

    def tok(w, which, col=0):
        return pl.BlockSpec((c, w), lambda b, n: (which(b, n), col))

    def tr(r, which, row=0):
        return pl.BlockSpec((r, c), lambda b, n: (row, which(b, n)))

    full = lambda a: pl.BlockSpec(a.shape, lambda b, n: (0,) * a.ndim)
    st_spec = pl.BlockSpec((None, GLA_HEADS, GLA_DK, GLA_DV), lambda b, n: (b, 0, 0, 0))
    in_specs = [tok(GLA_KW, fwd), tok(GLA_KW, fwd, 0), tr(GLA_KW, fwd), tr(GLA_KW, fwd, 0),
                tok(GLA_WIDTH, fwd),
                tok(GLA_KW, bwd), tok(GLA_KW, bwd, n_la_blocks_b), tr(GLA_KW, bwd),
                tr(GLA_KW, bwd, 1), tok(GLA_WIDTH, bwd),
                st_spec, st_spec,
                full(cf[0]), full(cf[1]), full(cf[2]), full(cb[0]), full(cb[1]), full(cb[2]),
                full(bd), full(vbd)]
    out_specs = (pl.BlockSpec((c, GLA_WIDTH), lambda b, n: (b * nc + n, 0)),
                 pl.BlockSpec((c, GLA_WIDTH), lambda b, n: (b * nc + (nc - 1 - n), 0)),
                 st_spec, st_spec)
    out_shape = (jax.ShapeDtypeStruct((n_batch * seq, GLA_WIDTH), F32),
                 jax.ShapeDtypeStruct((n_batch * seq, GLA_WIDTH), F32),
                 jax.ShapeDtypeStruct((n_batch, GLA_HEADS, GLA_DK, GLA_DV), F32),
                 jax.ShapeDtypeStruct((n_batch, GLA_HEADS, GLA_DK, GLA_DV), F32))
    return pl.pallas_call(
        functools.partial(_gla_kernel, levels_of=levels_of),
        grid=(n_batch, nc), in_specs=in_specs, out_specs=out_specs, out_shape=out_shape,
        scratch_shapes=[pltpu.VMEM((GLA_KW, GLA_WIDTH), F32), pltpu.VMEM((GLA_KW, GLA_WIDTH), F32)],
        compiler_params=_cparams(2), name="gla",
    )(gq, la, gkt, lat, gv, gq, la, gkt, lat, gv, s0f, s0b,
      cf[0], cf[1], cf[2], cb[0], cb[1], cb[2], bd, vbd)


def _outproj_kernel(attc_ref, attl_ref, ofc_ref, ofl_ref, obc_ref, obl_ref, gg_ref, xc_ref, xl_ref,
                    mod_ref, wo_ref, gn_ref, l1g_ref, l1b_ref, wrt_ref, sw13_ref, sw2_ref,
                    base_ref, u2_ref, lg_ref, *, n_ctx_tiles):
    is_ctx = pl.program_id(0) < n_ctx_tiles
    pick = lambda a_ref, b_ref: jnp.where(is_ctx, a_ref[...], b_ref[...])
    m = mod_ref[...]
    gate1 = m[:, 2 * D_MODEL:3 * D_MODEL]
    shift2 = m[:, 3 * D_MODEL:4 * D_MODEL]
    scale2 = m[:, 4 * D_MODEL:5 * D_MODEL]
    gate2 = m[:, 5 * D_MODEL:6 * D_MODEL]
    og = pick(ofc_ref, ofl_ref) + pick(obc_ref, obl_ref)
    gg = gg_ref[...].astype(F32)
    parts = []
    for h in range(GLA_HEADS):
        blk = og[:, h * GLA_DV:(h + 1) * GLA_DV]
        ms = jnp.mean(blk * blk, axis=-1, keepdims=True)
        nb = blk * lax.rsqrt(ms + EPS) * gn_ref[...]
        parts.append((nb * _silu(gg[:, h * GLA_DV:(h + 1) * GLA_DV])).astype(BF16))
    att_t = pick(attc_ref, attl_ref)
    hmix = (lax.dot_general(att_t, wo_ref[0:ATT_WIDTH, :], (((0,), (0,)), ((), ())),
                            preferred_element_type=F32)
            + jnp.dot(jnp.concatenate(parts, axis=1), wo_ref[ATT_WIDTH:, :],
                      preferred_element_type=F32))
    x1 = _layer_norm(ALPHA * pick(xc_ref, xl_ref) + gate1 * hmix, l1g_ref[...], l1b_ref[...])
    u2 = x1 * (1.0 + scale2) + shift2
    u2b = u2.astype(BF16)
    lg_ref[...] = lax.dot_general(wrt_ref[...], u2b, (((1,), (1,)), ((), ())),
                                  preferred_element_type=F32)
    ab = jnp.dot(u2b, sw13_ref[...], preferred_element_type=F32)
    hid = (_silu(ab[:, 0:SHARED_FF]) * ab[:, SHARED_FF:2 * SHARED_FF]).astype(BF16)
    shared = jnp.dot(hid, sw2_ref[...], preferred_element_type=F32)
    base_ref[...] = ALPHA * x1 + gate2 * shared
    _pack_rows(u2_ref, u2)


def _out_projection(att_c, att_l, of_c, of_l, ob_c, ob_l, gg, x_c, x_l, mod3, wo, gn, l1g, l1b, wrt,
                    sw13, sw2, n_seq_tiles):
    n_ctx = x_c.shape[0]
    n = n_ctx + x_l.shape[0]
    tb = TOK_TILE
    n_ctx_tiles = n_ctx // tb

    def mod_idx(i):
        return (jnp.where(i < n_ctx_tiles, 0, 1 + (i - n_ctx_tiles) // n_seq_tiles), 0, 0)

    ctx_blk = lambda i: jnp.minimum(i, n_ctx_tiles - 1)
    lat_blk = lambda i: jnp.maximum(i - n_ctx_tiles, 0)
    tok = lambda w: pl.BlockSpec((tb, w), lambda i: (i, 0))
    tok_c = lambda w: pl.BlockSpec((tb, w), lambda i: (ctx_blk(i), 0))
    tok_l = lambda w: pl.BlockSpec((tb, w), lambda i: (lat_blk(i), 0))
    full = lambda a: pl.BlockSpec(a.shape, lambda i: (0,) * a.ndim)
    return pl.pallas_call(
        functools.partial(_outproj_kernel, n_ctx_tiles=n_ctx_tiles),
        grid=(n // tb,),
        in_specs=[pl.BlockSpec((ATT_WIDTH, tb), lambda i: (0, ctx_blk(i))),
                  pl.BlockSpec((ATT_WIDTH, tb), lambda i: (0, lat_blk(i))),
                  tok_c(GLA_WIDTH), tok_l(GLA_WIDTH), tok_c(GLA_WIDTH), tok_l(GLA_WIDTH),
                  tok(GLA_WIDTH), tok_c(D_MODEL), tok_l(D_MODEL),
                  pl.BlockSpec((None, 1, mod3.shape[2]), mod_idx),
                  full(wo), full(gn), full(l1g), full(l1b), full(wrt), full(sw13), full(sw2)],
        out_specs=(tok(D_MODEL),
                   pl.BlockSpec((tb * PACK_CHUNKS, LANES), lambda i: (i, 0)),
                   pl.BlockSpec((N_EXPERTS, tb), lambda i: (0, i))),
        out_shape=(jax.ShapeDtypeStruct((n, D_MODEL), F32),
                   jax.ShapeDtypeStruct((n * PACK_CHUNKS, LANES), U32),
                   jax.ShapeDtypeStruct((N_EXPERTS, n), F32)),
        compiler_params=_cparams(1), name="out_projection",
    )(att_c, att_l, of_c, of_l, ob_c, ob_l, gg, x_c, x_l, mod3, wo, gn, l1g, l1b, wrt, sw13, sw2)


def _route_kernel(lg_ref, bias_ref, upper_ref, idx_ref, w_ref, pos_ref, cnt_ref, run_ref):
    i = pl.program_id(0)

    @pl.when(i == 0)
    def _():
        run_ref[...] = jnp.zeros_like(run_ref)

    s = jax.nn.sigmoid(lg_ref[...])
    work = s + bias_ref[...]
    rows = lax.broadcasted_iota(I32, s.shape, 0).astype(F32)
    sel = jnp.zeros(s.shape, F32)
    idxs, vals = [], []
    for _ in range(TOP_K):
        mx = jnp.max(work, axis=0, keepdims=True)
        idx = jnp.min(jnp.where(work == mx, rows, float(N_EXPERTS)), axis=0, keepdims=True)
        hit = rows == idx
        vals.append(jnp.sum(jnp.where(hit, s, 0.0), axis=0, keepdims=True))
        idxs.append(idx)
        sel = jnp.where(hit, 1.0, sel)
        work = jnp.where(hit, -jnp.inf, work)
    den = functools.reduce(jnp.add, vals)
    rank = jnp.dot(sel.astype(BF16), upper_ref[...], preferred_element_type=F32) + run_ref[:, 0:1]
    for k in range(TOP_K):
        idx_ref[k:k + 1, :] = idxs[k].astype(I32)
        w_ref[k:k + 1, :] = vals[k] / den * ROUTED_SCALE
        pos_ref[k:k + 1, :] = jnp.sum(jnp.where(rows == idxs[k], rank, 0.0), axis=0,
                                      keepdims=True).astype(I32)
    run_ref[...] = run_ref[...] + jnp.sum(sel, axis=1, keepdims=True)
    cnt_ref[...] = run_ref[...]


def _route(logits_t, bias_col, upper):
    n = logits_t.shape[1]
    tt = TOK_TILE
    row = lambda dt: jax.ShapeDtypeStruct((TOP_K, n), dt)
    blk = pl.BlockSpec((TOP_K, tt), lambda i: (0, i))
    return pl.pallas_call(
        _route_kernel,
        grid=(n // tt,),
        in_specs=[pl.BlockSpec((N_EXPERTS, tt), lambda i: (0, i)),
                  pl.BlockSpec((N_EXPERTS, 1), lambda i: (0, 0)),
                  pl.BlockSpec((tt, tt), lambda i: (0, 0))],
        out_specs=(blk, blk, blk, pl.BlockSpec((N_EXPERTS, LANES), lambda i: (0, 0))),
        out_shape=(row(I32), row(F32), row(I32), jax.ShapeDtypeStruct((N_EXPERTS, LANES), F32)),
        scratch_shapes=[pltpu.VMEM((N_EXPERTS, LANES), F32)],
        compiler_params=_cparams(1), name="route",
    )(logits_t, bias_col, upper)


def _dest_kernel(cnt_ref, lower_ref, idx_ref, pos_ref, dest_ref, bexp_ref, bval_ref, nused_ref):
    cnt = cnt_ref[...]
    nblk = jnp.floor((cnt + (MOE_ROWS - 1)) * (1.0 / MOE_ROWS))
    bstart = jnp.dot(lower_ref[...], nblk, precision=HIGHEST, preferred_element_type=F32)
    bend = bstart + nblk
    pstart = bstart[:, 0:1] * MOE_ROWS
    rows = lax.broadcasted_iota(I32, (N_EXPERTS, idx_ref.shape[1]), 0)
    for k in range(TOP_K):
        hit = rows == idx_ref[k:k + 1, :]
        dest_ref[k:k + 1, :] = (jnp.sum(jnp.where(hit, pstart, 0.0), axis=0, keepdims=True)
                                .astype(I32) + pos_ref[k:k + 1, :])

    @pl.when(pl.program_id(0) == 0)
    def _():
        nb = bexp_ref.shape[1]
        bid = lax.broadcasted_iota(I32, (N_EXPERTS, nb), 1).astype(F32)
        inside = jnp.logical_and(bid >= bstart[:, 0:1], bid < bend[:, 0:1])
        erow = lax.broadcasted_iota(I32, (N_EXPERTS, nb), 0).astype(F32)
        bexp_ref[...] = jnp.sum(jnp.where(inside, erow, 0.0), axis=0, keepdims=True).astype(I32)
        valid = jnp.clip(cnt[:, 0:1] - (bid - bstart[:, 0:1]) * MOE_ROWS, 0.0, float(MOE_ROWS))
        bval_ref[...] = jnp.sum(jnp.where(inside, valid, 0.0), axis=0, keepdims=True).astype(I32)
        nused_ref[...] = jnp.max(bend, axis=0, keepdims=True).astype(I32)


def _destinations(counts, lower, idx_t, pos_t, n_blocks_pad):
    n = idx_t.shape[1]
    tt = TOK_TILE
    blk = pl.BlockSpec((TOP_K, tt), lambda i: (0, i))
    one = lambda w: pl.BlockSpec((1, w), lambda i: (0, 0))
    return pl.pallas_call(
        _dest_kernel,
        grid=(n // tt,),
        in_specs=[pl.BlockSpec((N_EXPERTS, LANES), lambda i: (0, 0)),
                  pl.BlockSpec((N_EXPERTS, N_EXPERTS), lambda i: (0, 0)), blk, blk],
        out_specs=(blk, one(n_blocks_pad), one(n_blocks_pad), one(LANES)),
        out_shape=(jax.ShapeDtypeStruct((TOP_K, n), I32),
                   jax.ShapeDtypeStruct((1, n_blocks_pad), I32),
                   jax.ShapeDtypeStruct((1, n_blocks_pad), I32),
                   jax.ShapeDtypeStruct((1, LANES), I32)),
        compiler_params=_cparams(1), name="destinations",
    )(counts, lower, idx_t, pos_t)


SC_WINDOW = 128
SC_WINDOWS_PER_STEP = 8


def _invert_rows(dest_flat, n_rows):
    m = dest_flat.shape[0]
    mesh = plsc.VectorSubcoreMesh(core_axis_name="core", subcore_axis_name="subcore")

    @functools.partial(pl.kernel, out_type=jax.ShapeDtypeStruct((n_rows,), I32), mesh=mesh,
                       scratch_types=[])
    def invert(val_hbm, idx_hbm, out_hbm):
        def body(val_vmem, idx_vmem):
            for j in range(SC_WINDOWS_PER_STEP):
                pltpu.sync_copy(val_vmem.at[j], out_hbm.at[idx_vmem.at[j]])

        blk = pl.BlockSpec((SC_WINDOWS_PER_STEP, SC_WINDOW), lambda i: (i, 0))
        pltpu.emit_pipeline(
            body, grid=(m // (SC_WINDOW * SC_WINDOWS_PER_STEP),),
            in_specs=[blk, blk], out_specs=[], core_axis_name=("core", "subcore"),
            dimension_semantics=(pltpu.PARALLEL,),
        )(val_hbm, idx_hbm)

    shape = (m // SC_WINDOW, SC_WINDOW)
    return invert(jnp.arange(m, dtype=I32).reshape(shape), dest_flat.reshape(shape))


PACK_CHUNKS = D_MODEL // (2 * LANES)
SRC_GROUP = 4
TOP_K_LOG2 = TOP_K.bit_length() - 1
PACK_CHUNKS_LOG2 = PACK_CHUNKS.bit_length() - 1
U32 = jnp.uint32


def _pack_rows(ref, x, row0=0):
    bits = pltpu.bitcast(x.astype(BF16).astype(F32), U32)
    for s in range(PACK_CHUNKS):
        lo = bits[:, (2 * s) * LANES:(2 * s + 1) * LANES] >> 16
        hi = bits[:, (2 * s + 1) * LANES:(2 * s + 2) * LANES] & jnp.uint32(0xFFFF0000)
        ref[pl.ds(row0 + s, x.shape[0], stride=PACK_CHUNKS), :] = lo | hi


def _unpack_rows(ref, n_rows, row0=0):
    parts = []
    for s in range(PACK_CHUNKS):
        w = ref[pl.ds(row0 + s, n_rows, stride=PACK_CHUNKS), :]
        parts.append(pltpu.bitcast(w << 16, F32))
        parts.append(pltpu.bitcast(w & jnp.uint32(0xFFFF0000), F32))
    return jnp.concatenate(parts, axis=1).astype(BF16)


def _moe_kernel(bexp_ref, bval_ref, nused_ref, u2p_hbm, src_hbm, w1_hbm, w3_hbm, w2_hbm, yt_hbm,
                u2p_vmem, w1_f, w3_f, w2_f, w13_s, w2_s, xbuf, ybuf, src_smem,
                sem_in, sem_src, sem_w, sem_out, *, n_tokens):
    n_used = nused_ref[0]
    br = MOE_ROWS
    grp = SRC_GROUP * br
    trash0 = n_tokens * TOP_K

    def src_copy(g):
        return pltpu.make_async_copy(src_hbm.at[pl.ds(g * grp, grp)],
                                     src_smem.at[pl.ds(lax.rem(g, 2) * grp, grp)], sem_src)

    def out_wait(slot):
        pltpu.make_async_copy(ybuf.at[pl.ds(slot * br * PACK_CHUNKS, br * PACK_CHUNKS)],
                              yt_hbm.at[pl.ds(0, br * PACK_CHUNKS)], sem_out.at[slot]).wait()

    def src_base(blk):
        return lax.rem(blk // SRC_GROUP, 2) * grp + lax.rem(blk, SRC_GROUP) * br

    def scatter_row(blk_slot, sbase, valid, r):
        dst = jnp.where(r < valid, src_smem[sbase + r], trash0 + blk_slot * br + r)
        pltpu.make_async_copy(
            ybuf.at[pl.ds(pl.multiple_of((blk_slot * br + r) * PACK_CHUNKS, PACK_CHUNKS), PACK_CHUNKS)],
            yt_hbm.at[pl.ds(pl.multiple_of(dst * PACK_CHUNKS, PACK_CHUNKS), PACK_CHUNKS)],
            sem_out.at[blk_slot]).start(priority=r % 2 if isinstance(r, int) else 0)

    def gather_row(xslot, sbase, r):
        row = lax.shift_right_logical(src_smem[sbase + r], TOP_K_LOG2 - PACK_CHUNKS_LOG2)
        row = jnp.minimum(row & (-PACK_CHUNKS & 0x7FFFFFFF), (n_tokens - 1) * PACK_CHUNKS)
        dst = pl.multiple_of((xslot * br + r) * PACK_CHUNKS, PACK_CHUNKS)
        xbuf[pl.ds(dst, PACK_CHUNKS), :] = u2p_vmem[pl.ds(pl.multiple_of(row, PACK_CHUNKS), PACK_CHUNKS), :]

    def weight_copies(e, wslot):
        return [pltpu.make_async_copy(src.at[e], dst.at[wslot], sem_w.at[wslot])
                for src, dst in ((w1_hbm, w1_f), (w3_hbm, w3_f), (w2_hbm, w2_f))]

    cp = pltpu.make_async_copy(u2p_hbm, u2p_vmem, sem_in)
    cp.start()
    src_copy(0).start()
    for wcp in weight_copies(bexp_ref[0], 0):
        wcp.start()
    ybuf[...] = jnp.zeros_like(ybuf)
    cp.wait()
    src_copy(0).wait()
    lax.fori_loop(0, br, lambda r, c: (gather_row(0, 0, r), c)[1], 0)

    def block(b, wslot):
        g = b // SRC_GROUP
        phase = lax.rem(b, SRC_GROUP)

        more = (g + 1) * SRC_GROUP < n_used

        @pl.when(jnp.logical_and(phase == 1, more))
        def _():
            src_copy(g + 1).start()

        @pl.when(jnp.logical_and(phase == SRC_GROUP - 1, more))
        def _():
            src_copy(g + 1).wait()

        e = bexp_ref[b]
        prev = bexp_ref[jnp.maximum(b - 1, 0)]

        @pl.when(jnp.logical_or(b == 0, e != prev))
        def _():
            for wcp in weight_copies(e, wslot):
                wcp.wait()
            w13_s[:, 0:EXPERT_FF] = w1_f[wslot].astype(BF16)
            w13_s[:, EXPERT_FF:2 * EXPERT_FF] = w3_f[wslot].astype(BF16)
            w2_s[...] = w2_f[wslot].astype(BF16)
            nxt = lax.while_loop(
                lambda j: jnp.logical_and(j < n_used, bexp_ref[jnp.minimum(j, n_used - 1)] == e),
                lambda j: j + 1, b + 1)

            @pl.when(nxt < n_used)
            def _():
                for wcp in weight_copies(bexp_ref[nxt], 1 - wslot):
                    wcp.start()

        switch = jnp.logical_and(b + 1 < n_used, bexp_ref[b + 1] != e)

        valid = bval_ref[b]
        sbase = src_base(b)
        slot = lax.rem(b, 2)

        pb = jnp.maximum(b - 1, 0)
        p_valid = jnp.where(b > 0, bval_ref[pb], 0)
        p_base = src_base(pb)
        n_base = src_base(b + 1)
        for r in range(br):
            gather_row(1 - slot, n_base, r)
            scatter_row(1 - slot, p_base, p_valid, r)

        x = _unpack_rows(xbuf, br, slot * (br * PACK_CHUNKS))
        rows = lax.broadcasted_iota(I32, x.shape, 0)
        x = jnp.where(rows < valid, x, jnp.zeros_like(x))
        ab = jnp.dot(x, w13_s[...], preferred_element_type=F32)
        hid = (_silu(ab[:, 0:EXPERT_FF]) * ab[:, EXPERT_FF:2 * EXPERT_FF]).astype(BF16)
        y = jnp.dot(hid, w2_s[...], preferred_element_type=F32)

        @pl.when(b >= 1)
        def _():
            out_wait(slot)

        _pack_rows(ybuf, y, slot * (br * PACK_CHUNKS))
        return jnp.where(switch, 1 - wslot, wslot)

    lax.fori_loop(0, n_used, block, 0)

    last = n_used - 1
    l_slot = lax.rem(last, 2)
    l_base = src_base(last)
    l_valid = bval_ref[last]
    lax.fori_loop(0, br, lambda r, c: (scatter_row(l_slot, l_base, l_valid, r), c)[1], 0)
    out_wait(1 - l_slot)
    out_wait(l_slot)


def _moe_experts(bexp, bval, nused, u2p, row_src, w1, w3, w2, n_blocks, n_tokens):
    br = MOE_ROWS
    any_spec = pl.BlockSpec(memory_space=pl.ANY)
    grid_spec = pltpu.PrefetchScalarGridSpec(
        num_scalar_prefetch=3, grid=(1,),
        in_specs=[any_spec] * 5,
        out_specs=any_spec,
        scratch_shapes=[pltpu.VMEM(u2p.shape, U32),
                        pltpu.VMEM((2, D_MODEL, EXPERT_FF), F32),
                        pltpu.VMEM((2, D_MODEL, EXPERT_FF), F32),
                        pltpu.VMEM((2, EXPERT_FF, D_MODEL), F32),
                        pltpu.VMEM((D_MODEL, 2 * EXPERT_FF), BF16),
                        pltpu.VMEM((EXPERT_FF, D_MODEL), BF16),
                        pltpu.VMEM((2 * PACK_CHUNKS * br, LANES), U32),
                        pltpu.VMEM((2 * br * PACK_CHUNKS, LANES), U32),
                        pltpu.SMEM((2 * SRC_GROUP * br,), I32),
                        pltpu.SemaphoreType.DMA, pltpu.SemaphoreType.DMA,
                        pltpu.SemaphoreType.DMA((2,)), pltpu.SemaphoreType.DMA((2,))])
    n_out_tiles = n_tokens * TOP_K + 2 * br
    return pl.pallas_call(
        functools.partial(_moe_kernel, n_tokens=n_tokens), grid_spec=grid_spec,
        out_shape=jax.ShapeDtypeStruct((n_out_tiles * PACK_CHUNKS, LANES), U32),
        compiler_params=pltpu.CompilerParams(dimension_semantics=("arbitrary",),
                                             vmem_limit_bytes=MOE_VMEM_LIMIT),
        name="moe_experts",
    )(bexp, bval, nused, u2p, row_src, w1, w3, w2)


def _combine_kernel(w_hbm, yt_ref, base_ref, mod_ref, g_ref, b_ref, yc_ref, yl_ref,
                    w_smem, acc_lo, acc_hi, sem_w, *, n_ctx_tiles):
    i = pl.program_id(0)
    n_steps = pl.num_programs(0)
    n_tok = acc_lo.shape[0] // PACK_CHUNKS
    n_idx = n_tok * TOP_K

    def w_copy(tile):
        return pltpu.make_async_copy(w_hbm.at[pl.ds(tile * n_idx, n_idx)],
                                     w_smem.at[pl.ds(lax.rem(tile, 2) * n_idx, n_idx)], sem_w)

    @pl.when(i == 0)
    def _():
        w_copy(i).start()

    w_copy(i).wait()

    @pl.when(i + 1 < n_steps)
    def _():
        w_copy(i + 1).start()

    wbase = lax.rem(i, 2) * n_idx

    per_tile = SUBLANES // PACK_CHUNKS
    first = lax.broadcasted_iota(I32, (SUBLANES, LANES), 0) < PACK_CHUNKS

    def reduce_token(t):
        lo = hi = None
        for m in range(TOP_K // per_tile):
            j = t * TOP_K + m * per_tile
            words = yt_ref[pl.ds(pl.multiple_of(j * PACK_CHUNKS, SUBLANES), SUBLANES), :]
            wgt = jnp.where(first, w_smem[wbase + j], w_smem[wbase + j + 1])
            t_lo = wgt * pltpu.bitcast(words << 16, F32)
            t_hi = wgt * pltpu.bitcast(words & jnp.uint32(0xFFFF0000), F32)
            lo = t_lo if lo is None else lo + t_lo
            hi = t_hi if hi is None else hi + t_hi
        row = pl.multiple_of(t * PACK_CHUNKS, PACK_CHUNKS)
        acc_lo[pl.ds(row, PACK_CHUNKS), :] = lo[0:PACK_CHUNKS] + lo[PACK_CHUNKS:SUBLANES]
        acc_hi[pl.ds(row, PACK_CHUNKS), :] = hi[0:PACK_CHUNKS] + hi[PACK_CHUNKS:SUBLANES]

    def reduce_group(i, carry):
        for u in range(COMB_UNROLL):
            reduce_token(i * COMB_UNROLL + u)
        return carry

    lax.fori_loop(0, n_tok // COMB_UNROLL, reduce_group, 0)
    parts = []
    for s in range(PACK_CHUNKS):
        parts.append(acc_lo[pl.ds(s, n_tok, stride=PACK_CHUNKS), :])
        parts.append(acc_hi[pl.ds(s, n_tok, stride=PACK_CHUNKS), :])
    moe = jnp.concatenate(parts, axis=1)
    gate2 = mod_ref[:, 5 * D_MODEL:6 * D_MODEL]
    y = _layer_norm(base_ref[...] + gate2 * moe, g_ref[...], b_ref[...])

    @pl.when(i < n_ctx_tiles)
    def _():
        yc_ref[...] = y

    @pl.when(i >= n_ctx_tiles)
    def _():
        yl_ref[...] = y


def _combine(w_flat, yt, base, mod3, l2g, l2b, n_ctx, seq_tokens):
    n = base.shape[0]
    tc = COMB_TILE
    n_ctx_tiles = n_ctx // tc
    n_seq_tiles = seq_tokens // tc

    def mod_idx(i):
        return (jnp.where(i < n_ctx_tiles, 0, 1 + (i - n_ctx_tiles) // n_seq_tiles), 0, 0)

    full = lambda a: pl.BlockSpec(a.shape, lambda i: (0,) * a.ndim)
    return pl.pallas_call(
        functools.partial(_combine_kernel, n_ctx_tiles=n_ctx_tiles),
        grid=(n // tc,),
        in_specs=[pl.BlockSpec(memory_space=pl.ANY),
                  pl.BlockSpec((tc * TOP_K * PACK_CHUNKS, LANES), lambda i: (i, 0)),
                  pl.BlockSpec((tc, D_MODEL), lambda i: (i, 0)),
                  pl.BlockSpec((None, 1, mod3.shape[2]), mod_idx), full(l2g), full(l2b)],
        out_specs=(pl.BlockSpec((tc, D_MODEL), lambda i: (jnp.minimum(i, n_ctx_tiles - 1), 0)),
                   pl.BlockSpec((tc, D_MODEL), lambda i: (jnp.maximum(i - n_ctx_tiles, 0), 0))),
        out_shape=(jax.ShapeDtypeStruct((n_ctx, D_MODEL), F32),
                   jax.ShapeDtypeStruct((n - n_ctx, D_MODEL), F32)),
        scratch_shapes=[pltpu.SMEM((2 * tc * TOP_K,), F32),
                        pltpu.VMEM((tc * PACK_CHUNKS, LANES), F32),
                        pltpu.VMEM((tc * PACK_CHUNKS, LANES), F32),
                        pltpu.SemaphoreType.DMA],
        compiler_params=_cparams(1), name="combine",
    )(w_flat, yt, base, mod3, l2g, l2b)


def _rope_tables(n_tok, tile):
    rows = n_tok // GRID_W
    row_idx = jnp.repeat(jnp.arange(rows, dtype=F32), GRID_W)
    col_idx = jnp.tile(jnp.arange(GRID_W, dtype=F32), rows)
    inv_freq = 1.0 / (ROPE_THETA ** (jnp.arange(0, ROPE_AXIS_DIM, 2, dtype=F32) / ROPE_AXIS_DIM))
    ang_r = row_idx[:, None] * inv_freq[None, :]
    ang_c = col_idx[:, None] * inv_freq[None, :]
    ang = jnp.concatenate([ang_r, ang_r, ang_c, ang_c], axis=-1)
    cos, sin = jnp.cos(ang), jnp.sin(ang)
    quarter = (jnp.arange(HEAD_DIM) // (ROPE_AXIS_DIM // 2)) % 2
    sin_a = jnp.where(quarter == 0, -sin, 0.0)
    sin_b = jnp.where(quarter == 1, sin, 0.0)
    rep = LANES // HEAD_DIM
    ident = lambda v: jnp.full((tile, LANES), v, F32)
    cos_t = jnp.concatenate([jnp.tile(cos, (1, rep)), ident(1.0)], axis=0)
    sa_t = jnp.concatenate([jnp.tile(sin_a, (1, rep)), ident(0.0)], axis=0)
    sb_t = jnp.concatenate([jnp.tile(sin_b, (1, rep)), ident(0.0)], axis=0)
    ident_tr = lambda v: jnp.full((HEAD_DIM, tile), v, F32)
    cos_tr = jnp.concatenate([cos.T, ident_tr(1.0)], axis=1)
    sin_tr = jnp.concatenate([sin.T, ident_tr(0.0)], axis=1)
    return cos_t, sa_t, sb_t, cos_tr, sin_tr


def _dup_heads(a):
    parts = []
    for h in range(KV_HEADS):
        blk = a[..., h * HEAD_DIM:(h + 1) * HEAD_DIM]
        parts += [blk] * (LANES // HEAD_DIM)
    return jnp.concatenate(parts, axis=-1)


def kernel(x_prompt, x_sample, cache_k, cache_v, state_gla_fwd, state_gla_bwd, c, c_ctx, w_ada, b_ada, w_in, q_norm, k_norm, gla_wa_fwd, gla_ba_fwd, gla_wa_bwd, gla_ba_bwd, gla_norm, w_out, ln1_g, ln1_b, ln2_g, ln2_b, w_router, router_bias, exp_w1, exp_w3, exp_w2, sh_w1, sh_w3, sh_w2):
    n_ctx_b, ctx_seq, _ = x_prompt.shape
    n_lat_b, lat_seq, _ = x_sample.shape
    n_ctx = n_ctx_b * ctx_seq
    n_lat = n_lat_b * lat_seq
    n = n_ctx + n_lat
    l = 0

    x_c = x_prompt.reshape(n_ctx, D_MODEL)
    x_l = x_sample.reshape(n_lat, D_MODEL)

    c_rows = jnp.zeros((SUBLANES, D_MODEL), F32).at[0].set(c_ctx).at[1:1 + n_lat_b].set(c)
    mod = _modulation(c_rows, w_ada[l], b_ada[l][None, :])
    mod3 = mod.reshape(SUBLANES, 1, 6 * D_MODEL)

    wi = w_in[l]
    o_q, o_k, o_v, o_gq, o_gk, o_gv, o_gg, o_rf, o_rb, o_end = np.cumsum(
        [0, ATT_WIDTH, KV_HEADS * HEAD_DIM, KV_HEADS * HEAD_DIM, GLA_KW, GLA_KW, GLA_WIDTH, GLA_WIDTH,
         GLA_GATE_RANK, GLA_GATE_RANK])
    w_tok = jnp.concatenate([
        _dup_heads(wi[:, o_k:o_v]), wi[:, o_v:o_gq], wi[:, o_gq:o_gk],
        wi[:, o_gv:o_gg], wi[:, o_gg:o_rf], wi[:, o_rf:o_end],
        jnp.zeros((D_MODEL, LANES - 2 * GLA_GATE_RANK), F32)], axis=1).astype(BF16)
    w_tr = jnp.concatenate([wi[:, o_q:o_k], wi[:, o_v:o_gq], wi[:, o_gk:o_gv], wi[:, o_rf:o_end]],
                           axis=1).T.astype(BF16)
    rep = LANES // HEAD_DIM
    qn = q_norm[l][:, None]
    kn = jnp.tile(k_norm[l], rep)[None, :]
    seg = jnp.asarray(np.kron(np.eye(rep), np.ones((HEAD_DIM, HEAD_DIM))), BF16)
    wa = jnp.zeros((LANES, 2 * GLA_KW), F32)
    wa = wa.at[0:GLA_GATE_RANK, 0:GLA_KW].set(gla_wa_fwd[l])
    wa = wa.at[GLA_GATE_RANK:2 * GLA_GATE_RANK, GLA_KW:].set(gla_wa_bwd[l])
    ba = jnp.concatenate([gla_ba_fwd[l], gla_ba_bwd[l]])[None, :]
    wat = wa[0:2 * GLA_GATE_RANK, :].T
    bat = ba.T
    cos_t, sa_t, sb_t, cos_tr, sin_tr = _rope_tables(lat_seq, TOK_TILE)

    (qt, k_dup, vt, k32, v32, gq, gv, gg, la, gkt, lat) = _in_projection(
        x_c, x_l, mod3, w_tok, w_tr, qn, kn, cos_t, sa_t, sb_t, cos_tr, sin_tr, seg, wa, ba, wat, bat,
        lat_seq // TOK_TILE)

    ck = _dup_heads(cache_k[:, l].reshape(n_lat_b, -1, KV_HEADS * HEAD_DIM)).astype(BF16)
    cvt = cache_v[:, l].reshape(n_lat_b, -1, KV_HEADS * HEAD_DIM).transpose(0, 2, 1).astype(BF16)
    att_c = _attention(qt, k_dup, vt, None, 0, n_ctx_b, ctx_seq)
    att_l = _attention(qt, k_dup, vt, (ck, cvt), n_ctx, n_lat_b, lat_seq)

    gconst, levels_of = _gla_constants()
    to_dev = lambda t: (jnp.asarray(t[0], BF16), jnp.asarray(t[1], BF16), jnp.asarray(t[2], F32))
    bd = jnp.asarray(np.kron(np.eye(GLA_HEADS), np.ones((GLA_DK, GLA_DV))), BF16)
    vbd = jnp.asarray(np.kron(np.eye(GLA_HEADS), np.ones((GLA_CHUNK, GLA_DV))), BF16)
    consts = ((to_dev(gconst["f"]), to_dev(gconst["b"])), levels_of, bd, vbd)
    s_zero = jnp.zeros((n_ctx_b, GLA_HEADS, GLA_DK, GLA_DV), F32)
    of_c, ob_c, sf_new, sb_new = _gla(gq, la, gkt, lat, gv, s_zero, s_zero, consts, 0, n_ctx_b, ctx_seq)
    of_l, ob_l, _, _ = _gla(gq, la, gkt, lat, gv, state_gla_fwd[:, l], state_gla_bwd[:, l], consts,
                            n_ctx, n_lat_b, lat_seq)

    sw13 = jnp.concatenate([sh_w1[l], sh_w3[l]], axis=1).astype(BF16)
    base, u2_rows, logits_t = _out_projection(
        att_c, att_l, of_c, of_l, ob_c, ob_l, gg, x_c, x_l, mod3, w_out[l].astype(BF16),
        gla_norm[l][None, :], ln1_g[l][None, :], ln1_b[l][None, :], w_router[l].T.astype(BF16), sw13,
        sh_w2[l].astype(BF16), lat_seq // TOK_TILE)

    upper = jnp.asarray(np.triu(np.ones((TOK_TILE, TOK_TILE)), 1), BF16)
    idx_t, w_t, pos_t, counts = _route(logits_t, router_bias[l][:, None], upper)
    n_blocks = n * TOP_K // MOE_ROWS + N_EXPERTS
    n_blocks_pad = -(-n_blocks // LANES) * LANES
    lower = jnp.asarray(np.tril(np.ones((N_EXPERTS, N_EXPERTS)), -1), F32)
    dest_t, bexp, bval, nused = _destinations(counts, lower, idx_t, pos_t, n_blocks_pad)
    dest_flat = dest_t.T.reshape(-1)
    w_flat = w_t.T.reshape(-1)

    row_src = _invert_rows(dest_flat, n_blocks * MOE_ROWS)
    yt = _moe_experts(bexp.reshape(-1), bval.reshape(-1), nused.reshape(-1)[0:1], u2_rows, row_src,
                      exp_w1[l], exp_w3[l], exp_w2[l], n_blocks, n)
    y_c, y_l = _combine(w_flat, yt, base, mod3, ln2_g[l][None, :], ln2_b[l][None, :], n_ctx, lat_seq)

    y_prompt = y_c.reshape(n_ctx_b, ctx_seq, D_MODEL)
    y_sample = y_l.reshape(n_lat_b, lat_seq, D_MODEL)
    new_cache_k = k32.reshape(n_ctx_b, 1, ctx_seq, KV_HEADS, HEAD_DIM)
    new_cache_v = v32.reshape(n_ctx_b, 1, ctx_seq, KV_HEADS, HEAD_DIM)
    return (y_prompt, y_sample, new_cache_k, new_cache_v, sf_new[:, None], sb_new[:, None])
```

```python
import functools

import numpy as np
import jax
import jax.numpy as jnp
from jax import lax
from jax.experimental import pallas as pl
from jax.experimental.pallas import tpu as pltpu
from jax.experimental.pallas import tpu_sc as plsc

F32 = jnp.float32
BF16 = jnp.bfloat16
I32 = jnp.int32

D_MODEL = 1024
GRID_W = 64
HEAD_DIM = 64
N_HEADS = 8
KV_HEADS = 2
ATT_WIDTH = N_HEADS * HEAD_DIM
ATT_SCALE = HEAD_DIM ** -0.5
LOG2_E = 1.4426950408889634
ROPE_AXIS_DIM = HEAD_DIM // 2
ROPE_THETA = 10000.0
GLA_HEADS = 4
GLA_DK = 64
GLA_DV = 128
GLA_WIDTH = GLA_HEADS * GLA_DV
GLA_KW = GLA_HEADS * GLA_DK
GLA_GATE_RANK = 16
GLA_TAU = 16.0
N_EXPERTS = 256
TOP_K = 8
EXPERT_FF = 256
SHARED_FF = 256
ROUTED_SCALE = 2.5
DEPTH = 1
ALPHA = (2.0 * DEPTH) ** 0.25
EPS = 1e-6

LANES = 128
SUBLANES = 8
VMEM_BYTES = 64 * 1024 * 1024
VMEM_LIMIT = VMEM_BYTES - 8 * 1024 * 1024
PACK_CHUNKS = D_MODEL // (2 * LANES)
PACK_CHUNKS_LOG2 = PACK_CHUNKS.bit_length() - 1
HIGH_HALF = 0xFFFF0000
U32 = jnp.uint32

TOK_TILE = 512
ATT_TQ = 128
GLA_CHUNK = 128
GLA_LEVELS = ((32, 128), (8, 32), (2, 8), (1, 2))
MOE_ROWS = 256
MOE_VMEM_LIMIT = VMEM_BYTES - 2 * 1024 * 1024
TOP_K_LOG2 = TOP_K.bit_length() - 1
SRC_GROUP = 4
COMB_TILE = 256
COMB_UNROLL = 8
HIGHEST = lax.Precision.HIGHEST


def _cparams(n_axes):
    return pltpu.CompilerParams(dimension_semantics=("arbitrary",) * n_axes,
                                vmem_limit_bytes=VMEM_LIMIT)


def _silu(x):
    return x * jax.nn.sigmoid(x)


def _log_sigmoid(x):
    return jnp.minimum(x, 0.0) - jnp.log(1.0 + jnp.exp(-jnp.abs(x)))


def _dot_split(a, b):
    a_hi = a.astype(BF16)
    b_hi = b.astype(BF16)
    a_lo = (a - a_hi.astype(F32)).astype(BF16)
    b_lo = (b - b_hi.astype(F32)).astype(BF16)
    dot = functools.partial(jnp.dot, preferred_element_type=F32)
    return dot(a_hi, b_hi) + dot(a_lo, b_hi) + dot(a_hi, b_lo)


def _layer_norm(z, g, b):
    mu = jnp.mean(z, axis=-1, keepdims=True)
    zc = z - mu
    var = jnp.mean(zc * zc, axis=-1, keepdims=True)
    return zc * lax.rsqrt(var + EPS) * g + b


def _mod_kernel(c_ref, w_ref, b_ref, o_ref):
    s = _silu(c_ref[...]).astype(BF16)
    o_ref[...] = jnp.dot(s, w_ref[...].astype(BF16), preferred_element_type=F32) + b_ref[...]


def _modulation(c_rows, w_ada, b_ada):
    n_cols = w_ada.shape[1]
    tn = 512
    return pl.pallas_call(
        _mod_kernel,
        grid=(n_cols // tn,),
        in_specs=[pl.BlockSpec((SUBLANES, D_MODEL), lambda j: (0, 0)),
                  pl.BlockSpec((D_MODEL, tn), lambda j: (0, j)),
                  pl.BlockSpec((1, tn), lambda j: (0, j))],
        out_specs=pl.BlockSpec((SUBLANES, tn), lambda j: (0, j)),
        out_shape=jax.ShapeDtypeStruct((SUBLANES, n_cols), F32),
        compiler_params=_cparams(1),
        name="modulation",
    )(c_rows, w_ada, b_ada)


_C_K = 0
_C_V = _C_K + 2 * LANES
_C_GQ = _C_V + KV_HEADS * HEAD_DIM
_C_GV = _C_GQ + GLA_KW
_C_GG = _C_GV + GLA_WIDTH
_C_RA = _C_GG + GLA_WIDTH
_C_END = _C_RA + LANES
_R_Q = 0
_R_V = _R_Q + ATT_WIDTH
_R_GK = _R_V + KV_HEADS * HEAD_DIM
_R_RA = _R_GK + GLA_KW
_R_END = _R_RA + 2 * GLA_GATE_RANK


def _inproj_kernel(xc_ref, xl_ref, mod_ref, w_ref, wt_ref, qn_ref, kn_ref, cos_ref, sa_ref, sb_ref,
                   cost_ref, sint_ref, seg_ref, wa_ref, ba_ref, wat_ref, bat_ref,
                   qt_ref, k_ref, vt_ref, k32_ref, v32_ref, gq_ref, gv_ref, gg_ref,
                   la_ref, gkt_ref, lat_ref, *, n_ctx_tiles):
    i = pl.program_id(0)
    m = mod_ref[...]
    shift1 = m[:, 0:D_MODEL]
    scale1 = m[:, D_MODEL:2 * D_MODEL]
    x = jnp.where(i < n_ctx_tiles, xc_ref[...], xl_ref[...])
    u = (x * (1.0 + scale1) + shift1).astype(BF16)

    cos = cos_ref[...]
    sin_a = sa_ref[...]
    sin_b = sb_ref[...]
    seg = seg_ref[...]
    lane = lax.broadcasted_iota(I32, (u.shape[0], LANES), 1)
    low = lane < HEAD_DIM

    def proj(c0, c1):
        return jnp.dot(u, w_ref[:, c0:c1], preferred_element_type=F32)

    def head_norm(blk, gain):
        ss = jnp.dot((blk * blk).astype(BF16), seg, preferred_element_type=F32) * (1.0 / HEAD_DIM)
        return blk * lax.rsqrt(ss + EPS) * gain

    def rope(blk):
        return (blk * cos + pltpu.roll(blk, LANES - ROPE_AXIS_DIM // 2, 1) * sin_a
                + pltpu.roll(blk, ROPE_AXIS_DIM // 2, 1) * sin_b)

    pk = proj(_C_K, _C_V)
    kn = [head_norm(pk[:, j * LANES:(j + 1) * LANES], kn_ref[...]) for j in range(KV_HEADS)]
    for j in range(KV_HEADS):
        k_ref[:, j * LANES:(j + 1) * LANES] = rope(kn[j]).astype(BF16)

    @pl.when(i < n_ctx_tiles)
    def _():
        k32_ref[...] = jnp.where(low, kn[0], kn[1])
        v32_ref[...] = proj(_C_V, _C_GQ)

    gq_ref[...] = proj(_C_GQ, _C_GV) * (GLA_DK ** -0.5)
    gv_ref[...] = proj(_C_GV, _C_GG).astype(BF16)
    gg_ref[...] = proj(_C_GG, _C_RA).astype(BF16)

    ra = proj(_C_RA, _C_END)
    pre = _dot_split(ra, wa_ref[...]) + ba_ref[...]
    la_ref[...] = _log_sigmoid(pre) * (1.0 / GLA_TAU)

    pt = lax.dot_general(wt_ref[...], u, (((1,), (1,)), ((), ())), preferred_element_type=F32)
    cos_t = cost_ref[...]
    sin_t = sint_ref[...]
    quarter = ROPE_AXIS_DIM // 2
    for h in range(N_HEADS):
        blk = pt[_R_Q + h * HEAD_DIM:_R_Q + (h + 1) * HEAD_DIM, :]
        ms = jnp.mean(blk * blk, axis=0, keepdims=True)
        qn = blk * lax.rsqrt(ms + EPS) * qn_ref[...]
        rot = jnp.concatenate([-qn[quarter:2 * quarter], qn[0:quarter],
                               -qn[3 * quarter:4 * quarter], qn[2 * quarter:3 * quarter]], axis=0)
        qt_ref[h * HEAD_DIM:(h + 1) * HEAD_DIM, :] = (
            (qn * cos_t + rot * sin_t) * (ATT_SCALE * LOG2_E)).astype(BF16)
    vt_ref[...] = pt[_R_V:_R_GK, :].astype(BF16)
    gkt_ref[...] = pt[_R_GK:_R_RA, :]
    rat = pt[_R_RA:_R_END, :]
    pre_t = _dot_split(wat_ref[...], rat) + bat_ref[...]
    lat_ref[...] = _log_sigmoid(pre_t) * (1.0 / GLA_TAU)


def _in_projection(x_c, x_l, mod3, w_tok, w_tr, qn, kn, cos_t, sa_t, sb_t, cos_tr, sin_tr, seg, wa, ba,
                   wat, bat, n_seq_tiles):
    n_ctx = x_c.shape[0]
    n = n_ctx + x_l.shape[0]
    tb = TOK_TILE
    n_ctx_tiles = n_ctx // tb
    n_tiles = n // tb
    n_rope_blocks = cos_t.shape[0] // tb - 1

    def mod_idx(i):
        return (jnp.where(i < n_ctx_tiles, 0, 1 + (i - n_ctx_tiles) // n_seq_tiles), 0, 0)

    def rope_blk(i):
        return jnp.where(i < n_ctx_tiles, n_rope_blocks, (i - n_ctx_tiles) % n_seq_tiles)

    def rope_idx(i):
        return (rope_blk(i), 0)

    def ctx_idx(i):
        return (jnp.minimum(i, n_ctx_tiles - 1), 0)

    tok = lambda w: pl.BlockSpec((tb, w), lambda i: (i, 0))
    full = lambda a: pl.BlockSpec(a.shape, lambda i: (0,) * a.ndim)
    tr = lambda r: pl.BlockSpec((r, tb), lambda i: (0, i))
    rope_tr = pl.BlockSpec((HEAD_DIM, tb), lambda i: (0, rope_blk(i)))
    out_shapes = (
        jax.ShapeDtypeStruct((ATT_WIDTH, n), BF16),
        jax.ShapeDtypeStruct((n, 2 * LANES), BF16),
        jax.ShapeDtypeStruct((KV_HEADS * HEAD_DIM, n), BF16),
        jax.ShapeDtypeStruct((n_ctx, LANES), F32),
        jax.ShapeDtypeStruct((n_ctx, LANES), F32),
        jax.ShapeDtypeStruct((n, GLA_KW), F32),
        jax.ShapeDtypeStruct((n, GLA_WIDTH), BF16),
        jax.ShapeDtypeStruct((n, GLA_WIDTH), BF16),
        jax.ShapeDtypeStruct((n, 2 * GLA_KW), F32),
        jax.ShapeDtypeStruct((GLA_KW, n), F32),
        jax.ShapeDtypeStruct((2 * GLA_KW, n), F32),
    )
    out_specs = (tr(ATT_WIDTH), tok(2 * LANES), tr(KV_HEADS * HEAD_DIM),
                 pl.BlockSpec((tb, LANES), ctx_idx), pl.BlockSpec((tb, LANES), ctx_idx),
                 tok(GLA_KW), tok(GLA_WIDTH), tok(GLA_WIDTH), tok(2 * GLA_KW),
                 tr(GLA_KW), tr(2 * GLA_KW))
    in_specs = [pl.BlockSpec((tb, D_MODEL), ctx_idx),
                pl.BlockSpec((tb, D_MODEL), lambda i: (jnp.maximum(i - n_ctx_tiles, 0), 0)),
                pl.BlockSpec((None, 1, mod3.shape[2]), mod_idx),
                full(w_tok), full(w_tr), full(qn), full(kn),
                pl.BlockSpec((tb, LANES), rope_idx), pl.BlockSpec((tb, LANES), rope_idx),
                pl.BlockSpec((tb, LANES), rope_idx), rope_tr, rope_tr,
                full(seg), full(wa), full(ba), full(wat), full(bat)]
    return pl.pallas_call(
        functools.partial(_inproj_kernel, n_ctx_tiles=n_ctx_tiles),
        grid=(n_tiles,), in_specs=in_specs, out_specs=out_specs, out_shape=out_shapes,
        compiler_params=_cparams(1), name="in_projection",
    )(x_c, x_l, mod3, w_tok, w_tr, qn, kn, cos_t, sa_t, sb_t, cos_tr, sin_tr, seg, wa, ba, wat, bat)


def _attention_kernel(*refs, n_kv_parts):
    qt_ref = refs[0]
    k_refs = refs[1:1 + n_kv_parts]
    vt_refs = refs[1 + n_kv_parts:1 + 2 * n_kv_parts]
    o_ref = refs[1 + 2 * n_kv_parts]
    tq = qt_ref.shape[1]
    group = N_HEADS // KV_HEADS
    scores = []
    for kv in range(KV_HEADS):
        heads = range(kv * group, (kv + 1) * group)
        q_grp = jnp.concatenate([qt_ref[h * HEAD_DIM:(h + 1) * HEAD_DIM, :] for h in heads], axis=1)
        rhs = jnp.concatenate([q_grp, jnp.zeros_like(q_grp)], axis=0)
        scores.append([jnp.dot(k[:, kv * LANES:(kv + 1) * LANES], rhs, preferred_element_type=F32)
                       for k in k_refs])
    for kv in range(KV_HEADS):
        heads = range(kv * group, (kv + 1) * group)
        s = scores[kv]
        mx = functools.reduce(jnp.maximum, [jnp.max(x, axis=0, keepdims=True) for x in s])
        pr = [jnp.exp2(x - mx) for x in s]
        den = functools.reduce(jnp.add, [jnp.sum(x, axis=0, keepdims=True) for x in pr])
        acc = functools.reduce(jnp.add, [
            jnp.dot(vt[kv * HEAD_DIM:(kv + 1) * HEAD_DIM, :], x.astype(BF16),
                    preferred_element_type=F32) for x, vt in zip(pr, vt_refs)])
        out = (acc / den).astype(BF16)
        for j, h in enumerate(heads):
            o_ref[h * HEAD_DIM:(h + 1) * HEAD_DIM, :] = out[:, j * tq:(j + 1) * tq]


def _attention(qt, k, vt, extra_kv, row0, n_batch, seq):
    tq = ATT_TQ
    n_q = seq // tq
    q_blk0 = row0 // tq
    kv_blk0 = row0 // seq
    in_specs = [pl.BlockSpec((ATT_WIDTH, tq), lambda b, i: (0, q_blk0 + b * n_q + i))]
    k_spec = pl.BlockSpec((seq, 2 * LANES), lambda b, i: (kv_blk0 + b, 0))
    vt_spec = pl.BlockSpec((KV_HEADS * HEAD_DIM, seq), lambda b, i: (0, kv_blk0 + b))
    args_k, args_v, specs_k, specs_v = [k], [vt], [k_spec], [vt_spec]
    if extra_kv is not None:
        ck, cvt = extra_kv
        args_k.append(ck)
        args_v.append(cvt)
        specs_k.append(pl.BlockSpec((None, ck.shape[1], 2 * LANES), lambda b, i: (b, 0, 0)))
        specs_v.append(pl.BlockSpec((None, KV_HEADS * HEAD_DIM, cvt.shape[2]), lambda b, i: (b, 0, 0)))
    return pl.pallas_call(
        functools.partial(_attention_kernel, n_kv_parts=len(args_k)),
        grid=(n_batch, n_q),
        in_specs=in_specs + specs_k + specs_v,
        out_specs=pl.BlockSpec((ATT_WIDTH, tq), lambda b, i: (0, b * n_q + i)),
        out_shape=jax.ShapeDtypeStruct((ATT_WIDTH, n_batch * seq), BF16),
        compiler_params=_cparams(2), name="attention",
    )(qt, *args_k, *args_v)


def _gla_constants():
    c = GLA_CHUNK
    idx = np.arange(c)
    q_mats, k_mats, masks, levels_of = [], [], [], []
    for li, (s, p) in enumerate(GLA_LEVELS):
        start = (idx // s) * s
        end = start + s - 1
        k_mats.append(((idx[None, :] > idx[:, None]) & (idx[None, :] <= end[:, None])))
        for d in range(p // s - 1):
            lo = np.maximum(start - d * s, 0)
            q_mats.append((idx[None, :] >= lo[:, None]) & (idx[None, :] <= idx[:, None]))
            masks.append((idx[:, None] // p == idx[None, :] // p)
                         & (idx[:, None] // s - idx[None, :] // s - 1 == d))
            levels_of.append(li)
    masks.append(np.eye(c, dtype=bool))
    levels_of.append(len(GLA_LEVELS) - 1)
    q_mats.append(idx[None, :] <= idx[:, None])
    k_mats = k_mats[:-1]
    k_mats.append(idx[None, :] > idx[:, None])
    k_mats.append(np.ones((c, c), bool))
    out = {}
    for name, flip in (("f", False), ("b", True)):
        f = (lambda a: a[::-1, ::-1]) if flip else (lambda a: a)
        lq = np.concatenate([f(a) for a in q_mats], axis=0).astype(np.float32)
        lkt = np.concatenate([f(a).T for a in k_mats], axis=1).astype(np.float32)
        mk = np.stack([np.tile(f(a), (1, GLA_HEADS)) for a in masks]).astype(np.float32)
        out[name] = (np.concatenate([lq, lq], axis=1), np.concatenate([lkt, lkt], axis=0), mk)
    return out, tuple(levels_of)


def _gla_direction(q, g, gkt, gt, v, lq2, lkt2, masks_ref, bd, vbd, s_ref, levels_of):
    c = GLA_CHUNK
    n_var = len(levels_of)
    n_lev = len(GLA_LEVELS)
    g_hi = g.astype(BF16)
    g_lo = (g - g_hi.astype(F32)).astype(BF16)
    fq = jnp.dot(lq2, jnp.concatenate([g_hi, g_lo], axis=0), preferred_element_type=F32)
    gt_hi = gt.astype(BF16)
    gt_lo = (gt - gt_hi.astype(F32)).astype(BF16)
    fk = jnp.dot(jnp.concatenate([gt_hi, gt_lo], axis=1), lkt2, preferred_element_type=F32)

    def key_factor(f):
        return gkt * jnp.exp(fk[:, f * c:(f + 1) * c])

    q_var = [(q * jnp.exp(fq[vi * c:(vi + 1) * c, :])).astype(BF16) for vi in range(n_var - 1)]
    q_var.append(q.astype(BF16))
    a = jnp.zeros((c, GLA_HEADS * c), F32)
    for li in range(n_lev):
        kt = (key_factor(li) if li < n_lev - 1 else gkt).astype(BF16)
        xt = jnp.concatenate([kt] * GLA_HEADS, axis=1) * bd
        vis = [vi for vi in range(n_var) if levels_of[vi] == li]
        res = jnp.dot(jnp.concatenate([q_var[vi] for vi in vis], axis=0), xt,
                      preferred_element_type=F32)
        for r, vi in enumerate(vis):
            a = a + masks_ref[vi] * res[r * c:(r + 1) * c, :]
    q_in = (q * jnp.exp(fq[(n_var - 1) * c:n_var * c, :])).astype(BF16)
    state = s_ref[...]
    v_bd = jnp.concatenate([v] * GLA_HEADS, axis=0) * vbd
    o = (jnp.dot(q_in, state.astype(BF16), preferred_element_type=F32)
         + jnp.dot(a.astype(BF16), v_bd, preferred_element_type=F32))
    k_out = key_factor(n_lev - 1).astype(BF16)
    e_tot = jnp.exp(fk[:, n_lev * c:(n_lev + 1) * c])
    upd = jnp.dot(k_out, v, preferred_element_type=F32)
    s_ref[...] = (state * jnp.concatenate([e_tot] * (GLA_WIDTH // c), axis=1)
                  + upd * bd.astype(F32))
    return o


def _gla_kernel(gq_f, la_f, gkt_f, lat_f, gv_f, gq_b, la_b, gkt_b, lat_b, gv_b,
                s0f_ref, s0b_ref, lq2f, lkt2f, mkf, lq2b, lkt2b, mkb, bd_ref, vbd_ref,
                of_ref, ob_ref, sf_ref, sb_ref, st_f, st_b, *, levels_of):
    n = pl.program_id(1)

    @pl.when(n == 0)
    def _():
        st_f[...] = jnp.zeros_like(st_f)
        st_b[...] = jnp.zeros_like(st_b)
        for h in range(GLA_HEADS):
            rows = slice(h * GLA_DK, (h + 1) * GLA_DK)
            cols = slice(h * GLA_DV, (h + 1) * GLA_DV)
            st_f[rows, cols] = s0f_ref[h]
            st_b[rows, cols] = s0b_ref[h]

    bd = bd_ref[...]
    vbd = vbd_ref[...]
    of_ref[...] = _gla_direction(gq_f[...], la_f[...], gkt_f[...], lat_f[...], gv_f[...],
                                 lq2f[...], lkt2f[...], mkf, bd, vbd, st_f, levels_of)
    ob_ref[...] = _gla_direction(gq_b[...], la_b[...], gkt_b[...], lat_b[...], gv_b[...],
                                 lq2b[...], lkt2b[...], mkb, bd, vbd, st_b, levels_of)

    @pl.when(n == pl.num_programs(1) - 1)
    def _():
        for h in range(GLA_HEADS):
            rows = slice(h * GLA_DK, (h + 1) * GLA_DK)
            cols = slice(h * GLA_DV, (h + 1) * GLA_DV)
            sf_ref[h] = st_f[rows, cols]
            sb_ref[h] = st_b[rows, cols]


def _gla(gq, la, gkt, lat, gv, s0f, s0b, consts, row0, n_batch, seq):
    (cf, cb), levels_of, bd, vbd = consts
    c = GLA_CHUNK
    nc = seq // c
    blk0 = row0 // c
    n_la_blocks_b = 1
    fwd = lambda b, n: blk0 + b * nc + n
    bwd = lambda b, n: blk0 + b * nc + (nc - 1 - n)

    def tok(w, which, col=0):
        return pl.BlockSpec((c, w), lambda b, n: (which(b, n), col))

    def tr(r, which, row=0):
        return pl.BlockSpec((r, c), lambda b, n: (row, which(b, n)))

    full = lambda a: pl.BlockSpec(a.shape, lambda b, n: (0,) * a.ndim)
    st_spec = pl.BlockSpec((None, GLA_HEADS, GLA_DK, GLA_DV), lambda b, n: (b, 0, 0, 0))
    in_specs = [tok(GLA_KW, fwd), tok(GLA_KW, fwd, 0), tr(GLA_KW, fwd), tr(GLA_KW, fwd, 0),
                tok(GLA_WIDTH, fwd),
                tok(GLA_KW, bwd), tok(GLA_KW, bwd, n_la_blocks_b), tr(GLA_KW, bwd),
                tr(GLA_KW, bwd, 1), tok(GLA_WIDTH, bwd),
                st_spec, st_spec,
                full(cf[0]), full(cf[1]), full(cf[2]), full(cb[0]), full(cb[1]), full(cb[2]),
                full(bd), full(vbd)]
    out_specs = (pl.BlockSpec((c, GLA_WIDTH), lambda b, n: (b * nc + n, 0)),
                 pl.BlockSpec((c, GLA_WIDTH), lambda b, n: (b * nc + (nc - 1 - n), 0)),
                 st_spec, st_spec)
    out_shape = (jax.ShapeDtypeStruct((n_batch * seq, GLA_WIDTH), F32),
                 jax.ShapeDtypeStruct((n_batch * seq, GLA_WIDTH), F32),
                 jax.ShapeDtypeStruct((n_batch, GLA_HEADS, GLA_DK, GLA_DV), F32),
                 jax.ShapeDtypeStruct((n_batch, GLA_HEADS, GLA_DK, GLA_DV), F32))
    return pl.pallas_call(
        functools.partial(_gla_kernel, levels_of=levels_of),
        grid=(n_batch, nc), in_specs=in_specs, out_specs=out_specs, out_shape=out_shape,
        scratch_shapes=[pltpu.VMEM((GLA_KW, GLA_WIDTH), F32), pltpu.VMEM((GLA_KW, GLA_WIDTH), F32)],
        compiler_params=_cparams(2), name="gla",
    )(gq, la, gkt, lat, gv, gq, la, gkt, lat, gv, s0f, s0b,
      cf[0], cf[1], cf[2], cb[0], cb[1], cb[2], bd, vbd)


def _outproj_kernel(attc_ref, attl_ref, ofc_ref, ofl_ref, obc_ref, obl_ref, gg_ref, xc_ref, xl_ref,
                    mod_ref, wo_ref, gn_ref, l1g_ref, l1b_ref, wrt_ref, sw13_ref, sw2_ref,
                    base_ref, u2_ref, lg_ref, *, n_ctx_tiles):
    is_ctx = pl.program_id(0) < n_ctx_tiles
    pick = lambda a_ref, b_ref: jnp.where(is_ctx, a_ref[...], b_ref[...])
    m = mod_ref[...]
    gate1 = m[:, 2 * D_MODEL:3 * D_MODEL]
    shift2 = m[:, 3 * D_MODEL:4 * D_MODEL]
    scale2 = m[:, 4 * D_MODEL:5 * D_MODEL]
    gate2 = m[:, 5 * D_MODEL:6 * D_MODEL]
    og = pick(ofc_ref, ofl_ref) + pick(obc_ref, obl_ref)
    gg = gg_ref[...].astype(F32)
    parts = []
    for h in range(GLA_HEADS):
        blk = og[:, h * GLA_DV:(h + 1) * GLA_DV]
        ms = jnp.mean(blk * blk, axis=-1, keepdims=True)
        nb = blk * lax.rsqrt(ms + EPS) * gn_ref[...]
        parts.append((nb * _silu(gg[:, h * GLA_DV:(h + 1) * GLA_DV])).astype(BF16))
    att_t = pick(attc_ref, attl_ref)
    hmix = (lax.dot_general(att_t, wo_ref[0:ATT_WIDTH, :], (((0,), (0,)), ((), ())),
                            preferred_element_type=F32)
            + jnp.dot(jnp.concatenate(parts, axis=1), wo_ref[ATT_WIDTH:, :],
                      preferred_element_type=F32))
    x1 = _layer_norm(ALPHA * pick(xc_ref, xl_ref) + gate1 * hmix, l1g_ref[...], l1b_ref[...])
    u2 = x1 * (1.0 + scale2) + shift2
    u2b = u2.astype(BF16)
    lg_ref[...] = lax.dot_general(wrt_ref[...], u2b, (((1,), (1,)), ((), ())),
                                  preferred_element_type=F32)
    ab = jnp.dot(u2b, sw13_ref[...], preferred_element_type=F32)
    hid = (_silu(ab[:, 0:SHARED_FF]) * ab[:, SHARED_FF:2 * SHARED_FF]).astype(BF16)
    shared = jnp.dot(hid, sw2_ref[...], preferred_element_type=F32)
    base_ref[...] = ALPHA * x1 + gate2 * shared
    _pack_rows(u2_ref, u2)


def _out_projection(att_c, att_l, of_c, of_l, ob_c, ob_l, gg, x_c, x_l, mod3, wo, gn, l1g, l1b, wrt,
                    sw13, sw2, n_seq_tiles):
    n_ctx = x_c.shape[0]
    n = n_ctx + x_l.shape[0]
    tb = TOK_TILE
    n_ctx_tiles = n_ctx // tb

    def mod_idx(i):
        return (jnp.where(i < n_ctx_tiles, 0, 1 + (i - n_ctx_tiles) // n_seq_tiles), 0, 0)

    ctx_blk = lambda i: jnp.minimum(i, n_ctx_tiles - 1)
    lat_blk = lambda i: jnp.maximum(i - n_ctx_tiles, 0)
    tok = lambda w: pl.BlockSpec((tb, w), lambda i: (i, 0))
    tok_c = lambda w: pl.BlockSpec((tb, w), lambda i: (ctx_blk(i), 0))
    tok_l = lambda w: pl.BlockSpec((tb, w), lambda i: (lat_blk(i), 0))
    full = lambda a: pl.BlockSpec(a.shape, lambda i: (0,) * a.ndim)
    return pl.pallas_call(
        functools.partial(_outproj_kernel, n_ctx_tiles=n_ctx_tiles),
        grid=(n // tb,),
        in_specs=[pl.BlockSpec((ATT_WIDTH, tb), lambda i: (0, ctx_blk(i))),
                  pl.BlockSpec((ATT_WIDTH, tb), lambda i: (0, lat_blk(i))),
                  tok_c(GLA_WIDTH), tok_l(GLA_WIDTH), tok_c(GLA_WIDTH), tok_l(GLA_WIDTH),
                  tok(GLA_WIDTH), tok_c(D_MODEL), tok_l(D_MODEL),
                  pl.BlockSpec((None, 1, mod3.shape[2]), mod_idx),
                  full(wo), full(gn), full(l1g), full(l1b), full(wrt), full(sw13), full(sw2)],
        out_specs=(tok(D_MODEL),
                   pl.BlockSpec((tb * PACK_CHUNKS, LANES), lambda i: (i, 0)),
                   pl.BlockSpec((N_EXPERTS, tb), lambda i: (0, i))),
        out_shape=(jax.ShapeDtypeStruct((n, D_MODEL), F32),
                   jax.ShapeDtypeStruct((n * PACK_CHUNKS, LANES), U32),
                   jax.ShapeDtypeStruct((N_EXPERTS, n), F32)),
        compiler_params=_cparams(1), name="out_projection",
    )(att_c, att_l, of_c, of_l, ob_c, ob_l, gg, x_c, x_l, mod3, wo, gn, l1g, l1b, wrt, sw13, sw2)


def _route_kernel(lg_ref, bias_ref, upper_ref, idx_ref, w_ref, pos_ref, cnt_ref, run_ref):
    i = pl.program_id(0)

    @pl.when(i == 0)
    def _():
        run_ref[...] = jnp.zeros_like(run_ref)

    s = jax.nn.sigmoid(lg_ref[...])
    work = s + bias_ref[...]
    rows = lax.broadcasted_iota(I32, s.shape, 0).astype(F32)
    sel = jnp.zeros(s.shape, F32)
    idxs, vals = [], []
    for _ in range(TOP_K):
        mx = jnp.max(work, axis=0, keepdims=True)
        idx = jnp.min(jnp.where(work == mx, rows, float(N_EXPERTS)), axis=0, keepdims=True)
        hit = rows == idx
        vals.append(jnp.sum(jnp.where(hit, s, 0.0), axis=0, keepdims=True))
        idxs.append(idx)
        sel = jnp.where(hit, 1.0, sel)
        work = jnp.where(hit, -jnp.inf, work)
    den = functools.reduce(jnp.add, vals)
    rank = jnp.dot(sel.astype(BF16), upper_ref[...], preferred_element_type=F32) + run_ref[:, 0:1]
    for k in range(TOP_K):
        idx_ref[k:k + 1, :] = idxs[k].astype(I32)
        w_ref[k:k + 1, :] = vals[k] / den * ROUTED_SCALE
        pos_ref[k:k + 1, :] = jnp.sum(jnp.where(rows == idxs[k], rank, 0.0), axis=0,
                                      keepdims=True).astype(I32)
    run_ref[...] = run_ref[...] + jnp.sum(sel, axis=1, keepdims=True)
    cnt_ref[...] = run_ref[...]


def _route(logits_t, bias_col, upper):
    n = logits_t.shape[1]
    tt = TOK_TILE
    row = lambda dt: jax.ShapeDtypeStruct((TOP_K, n), dt)
    blk = pl.BlockSpec((TOP_K, tt), lambda i: (0, i))
    return pl.pallas_call(
        _route_kernel,
        grid=(n // tt,),
        in_specs=[pl.BlockSpec((N_EXPERTS, tt), lambda i: (0, i)),
                  pl.BlockSpec((N_EXPERTS, 1), lambda i: (0, 0)),
                  pl.BlockSpec((tt, tt), lambda i: (0, 0))],
        out_specs=(blk, blk, blk, pl.BlockSpec((N_EXPERTS, LANES), lambda i: (0, 0))),
        out_shape=(row(I32), row(F32), row(I32), jax.ShapeDtypeStruct((N_EXPERTS, LANES), F32)),
        scratch_shapes=[pltpu.VMEM((N_EXPERTS, LANES), F32)],
        compiler_params=_cparams(1), name="route",
    )(logits_t, bias_col, upper)


def _dest_kernel(cnt_ref, lower_ref, idx_ref, pos_ref, dest_ref, bexp_ref, bval_ref, nused_ref):
    cnt = cnt_ref[...]
    nblk = jnp.floor((cnt + (MOE_ROWS - 1)) * (1.0 / MOE_ROWS))
    bstart = jnp.dot(lower_ref[...], nblk, precision=HIGHEST, preferred_element_type=F32)
    bend = bstart + nblk
    pstart = bstart[:, 0:1] * MOE_ROWS
    rows = lax.broadcasted_iota(I32, (N_EXPERTS, idx_ref.shape[1]), 0)
    for k in range(TOP_K):
        hit = rows == idx_ref[k:k + 1, :]
        dest_ref[k:k + 1, :] = (jnp.sum(jnp.where(hit, pstart, 0.0), axis=0, keepdims=True)
                                .astype(I32) + pos_ref[k:k + 1, :])

    @pl.when(pl.program_id(0) == 0)
    def _():
        nb = bexp_ref.shape[1]
        bid = lax.broadcasted_iota(I32, (N_EXPERTS, nb), 1).astype(F32)
        inside = jnp.logical_and(bid >= bstart[:, 0:1], bid < bend[:, 0:1])
        erow = lax.broadcasted_iota(I32, (N_EXPERTS, nb), 0).astype(F32)
        bexp_ref[...] = jnp.sum(jnp.where(inside, erow, 0.0), axis=0, keepdims=True).astype(I32)
        valid = jnp.clip(cnt[:, 0:1] - (bid - bstart[:, 0:1]) * MOE_ROWS, 0.0, float(MOE_ROWS))
        bval_ref[...] = jnp.sum(jnp.where(inside, valid, 0.0), axis=0, keepdims=True).astype(I32)
        nused_ref[...] = jnp.max(bend, axis=0, keepdims=True).astype(I32)


def _destinations(counts, lower, idx_t, pos_t, n_blocks_pad):
    n = idx_t.shape[1]
    tt = TOK_TILE
    blk = pl.BlockSpec((TOP_K, tt), lambda i: (0, i))
    one = lambda w: pl.BlockSpec((1, w), lambda i: (0, 0))
    return pl.pallas_call(
        _dest_kernel,
        grid=(n // tt,),
        in_specs=[pl.BlockSpec((N_EXPERTS, LANES), lambda i: (0, 0)),
                  pl.BlockSpec((N_EXPERTS, N_EXPERTS), lambda i: (0, 0)), blk, blk],
        out_specs=(blk, one(n_blocks_pad), one(n_blocks_pad), one(LANES)),
        out_shape=(jax.ShapeDtypeStruct((TOP_K, n), I32),
                   jax.ShapeDtypeStruct((1, n_blocks_pad), I32),
                   jax.ShapeDtypeStruct((1, n_blocks_pad), I32),
                   jax.ShapeDtypeStruct((1, LANES), I32)),
        compiler_params=_cparams(1), name="destinations",
    )(counts, lower, idx_t, pos_t)


SC_WINDOW = 128
SC_WINDOWS_PER_STEP = 8


def _invert_rows(dest_flat, n_rows):
    m = dest_flat.shape[0]
    mesh = plsc.VectorSubcoreMesh(core_axis_name="core", subcore_axis_name="subcore")

    @functools.partial(pl.kernel, out_type=jax.ShapeDtypeStruct((n_rows,), I32), mesh=mesh,
                       scratch_types=[])
    def invert(val_hbm, idx_hbm, out_hbm):
        def body(val_vmem, idx_vmem):
            for j in range(SC_WINDOWS_PER_STEP):
                pltpu.sync_copy(val_vmem.at[j], out_hbm.at[idx_vmem.at[j]])

        blk = pl.BlockSpec((SC_WINDOWS_PER_STEP, SC_WINDOW), lambda i: (i, 0))
        pltpu.emit_pipeline(
            body, grid=(m // (SC_WINDOW * SC_WINDOWS_PER_STEP),),
            in_specs=[blk, blk], out_specs=[], core_axis_name=("core", "subcore"),
            dimension_semantics=(pltpu.PARALLEL,),
        )(val_hbm, idx_hbm)

    shape = (m // SC_WINDOW, SC_WINDOW)
    return invert(jnp.arange(m, dtype=I32).reshape(shape), dest_flat.reshape(shape))


def _pack_rows(ref, x, row0=0):
    bits = pltpu.bitcast(x.astype(BF16).astype(F32), U32)
    for s in range(PACK_CHUNKS):
        lo = bits[:, (2 * s) * LANES:(2 * s + 1) * LANES] >> 16
        hi = bits[:, (2 * s + 1) * LANES:(2 * s + 2) * LANES] & jnp.uint32(HIGH_HALF)
        ref[pl.ds(row0 + s, x.shape[0], stride=PACK_CHUNKS), :] = lo | hi


def _unpack_rows(ref, n_rows, row0=0):
    parts = []
    for s in range(PACK_CHUNKS):
        w = ref[pl.ds(row0 + s, n_rows, stride=PACK_CHUNKS), :]
        parts.append(pltpu.bitcast(w << 16, F32))
        parts.append(pltpu.bitcast(w & jnp.uint32(HIGH_HALF), F32))
    return jnp.concatenate(parts, axis=1).astype(BF16)


def _moe_kernel(bexp_ref, bval_ref, nused_ref, u2p_hbm, src_hbm, w1_hbm, w3_hbm, w2_hbm, yt_hbm,
                u2p_vmem, w1_f, w3_f, w2_f, w13_s, w2_s, xbuf, ybuf, src_smem,
                sem_in, sem_src, sem_w, sem_out, *, n_tokens):
    n_used = nused_ref[0]
    br = MOE_ROWS
    grp = SRC_GROUP * br
    trash0 = n_tokens * TOP_K

    def src_copy(g):
        return pltpu.make_async_copy(src_hbm.at[pl.ds(g * grp, grp)],
                                     src_smem.at[pl.ds(lax.rem(g, 2) * grp, grp)], sem_src)

    def out_wait(slot):
        pltpu.make_async_copy(ybuf.at[pl.ds(slot * br * PACK_CHUNKS, br * PACK_CHUNKS)],
                              yt_hbm.at[pl.ds(0, br * PACK_CHUNKS)], sem_out.at[slot]).wait()

    def src_base(blk):
        return lax.rem(blk // SRC_GROUP, 2) * grp + lax.rem(blk, SRC_GROUP) * br

    def scatter_row(blk_slot, sbase, valid, r, priority=0):
        dst = jnp.where(r < valid, src_smem[sbase + r], trash0 + blk_slot * br + r)
        pltpu.make_async_copy(
            ybuf.at[pl.ds(pl.multiple_of((blk_slot * br + r) * PACK_CHUNKS, PACK_CHUNKS), PACK_CHUNKS)],
            yt_hbm.at[pl.ds(pl.multiple_of(dst * PACK_CHUNKS, PACK_CHUNKS), PACK_CHUNKS)],
            sem_out.at[blk_slot]).start(priority=priority)

    def gather_row(xslot, sbase, r):
        row = lax.shift_right_logical(src_smem[sbase + r], TOP_K_LOG2 - PACK_CHUNKS_LOG2)
        row = jnp.minimum(row & (int(jnp.iinfo(I32).max) - (PACK_CHUNKS - 1)),
                          (n_tokens - 1) * PACK_CHUNKS)
        dst = pl.multiple_of((xslot * br + r) * PACK_CHUNKS, PACK_CHUNKS)
        xbuf[pl.ds(dst, PACK_CHUNKS), :] = u2p_vmem[pl.ds(pl.multiple_of(row, PACK_CHUNKS), PACK_CHUNKS), :]

    def weight_copies(e, wslot):
        return [pltpu.make_async_copy(src.at[e], dst.at[wslot], sem_w.at[wslot])
                for src, dst in ((w1_hbm, w1_f), (w3_hbm, w3_f), (w2_hbm, w2_f))]

    cp = pltpu.make_async_copy(u2p_hbm, u2p_vmem, sem_in)
    cp.start()
    src_copy(0).start()
    for wcp in weight_copies(bexp_ref[0], 0):
        wcp.start()
    ybuf[...] = jnp.zeros_like(ybuf)
    cp.wait()
    src_copy(0).wait()
    lax.fori_loop(0, br, lambda r, c: (gather_row(0, 0, r), c)[1], 0)

    def block(b, wslot):
        g = b // SRC_GROUP
        phase = lax.rem(b, SRC_GROUP)

        more = (g + 1) * SRC_GROUP < n_used

        @pl.when(jnp.logical_and(phase == 1, more))
        def _():
            src_copy(g + 1).start()

        @pl.when(jnp.logical_and(phase == SRC_GROUP - 1, more))
        def _():
            src_copy(g + 1).wait()

        e = bexp_ref[b]
        prev = bexp_ref[jnp.maximum(b - 1, 0)]

        @pl.when(jnp.logical_or(b == 0, e != prev))
        def _():
            for wcp in weight_copies(e, wslot):
                wcp.wait()
            w13_s[:, 0:EXPERT_FF] = w1_f[wslot].astype(BF16)
            w13_s[:, EXPERT_FF:2 * EXPERT_FF] = w3_f[wslot].astype(BF16)
            w2_s[...] = w2_f[wslot].astype(BF16)
            nxt = lax.while_loop(
                lambda j: jnp.logical_and(j < n_used, bexp_ref[jnp.minimum(j, n_used - 1)] == e),
                lambda j: j + 1, b + 1)

            @pl.when(nxt < n_used)
            def _():
                for wcp in weight_copies(bexp_ref[nxt], 1 - wslot):
                    wcp.start()

        switch = jnp.logical_and(b + 1 < n_used, bexp_ref[b + 1] != e)

        valid = bval_ref[b]
        sbase = src_base(b)
        slot = lax.rem(b, 2)

        pb = jnp.maximum(b - 1, 0)
        p_valid = jnp.where(b > 0, bval_ref[pb], 0)
        p_base = src_base(pb)
        n_base = src_base(b + 1)
        for r in range(br):
            gather_row(1 - slot, n_base, r)
            scatter_row(1 - slot, p_base, p_valid, r, priority=r % 2)

        x = _unpack_rows(xbuf, br, slot * (br * PACK_CHUNKS))
        rows = lax.broadcasted_iota(I32, x.shape, 0)
        x = jnp.where(rows < valid, x, jnp.zeros_like(x))
        ab = jnp.dot(x, w13_s[...], preferred_element_type=F32)
        hid = (_silu(ab[:, 0:EXPERT_FF]) * ab[:, EXPERT_FF:2 * EXPERT_FF]).astype(BF16)
        y = jnp.dot(hid, w2_s[...], preferred_element_type=F32)

        @pl.when(b >= 1)
        def _():
            out_wait(slot)

        _pack_rows(ybuf, y, slot * (br * PACK_CHUNKS))
        return jnp.where(switch, 1 - wslot, wslot)

    lax.fori_loop(0, n_used, block, 0)

    last = n_used - 1
    l_slot = lax.rem(last, 2)
    l_base = src_base(last)
    l_valid = bval_ref[last]
    lax.fori_loop(0, br, lambda r, c: (scatter_row(l_slot, l_base, l_valid, r), c)[1], 0)
    out_wait(1 - l_slot)
    out_wait(l_slot)


def _moe_experts(bexp, bval, nused, u2p, row_src, w1, w3, w2, n_tokens):
    br = MOE_ROWS
    any_spec = pl.BlockSpec(memory_space=pl.ANY)
    grid_spec = pltpu.PrefetchScalarGridSpec(
        num_scalar_prefetch=3, grid=(1,),
        in_specs=[any_spec] * 5,
        out_specs=any_spec,
        scratch_shapes=[pltpu.VMEM(u2p.shape, U32),
                        pltpu.VMEM((2, D_MODEL, EXPERT_FF), F32),
                        pltpu.VMEM((2, D_MODEL, EXPERT_FF), F32),
                        pltpu.VMEM((2, EXPERT_FF, D_MODEL), F32),
                        pltpu.VMEM((D_MODEL, 2 * EXPERT_FF), BF16),
                        pltpu.VMEM((EXPERT_FF, D_MODEL), BF16),
                        pltpu.VMEM((2 * PACK_CHUNKS * br, LANES), U32),
                        pltpu.VMEM((2 * br * PACK_CHUNKS, LANES), U32),
                        pltpu.SMEM((2 * SRC_GROUP * br,), I32),
                        pltpu.SemaphoreType.DMA, pltpu.SemaphoreType.DMA,
                        pltpu.SemaphoreType.DMA((2,)), pltpu.SemaphoreType.DMA((2,))])
    n_out_tiles = n_tokens * TOP_K + 2 * br
    return pl.pallas_call(
        functools.partial(_moe_kernel, n_tokens=n_tokens), grid_spec=grid_spec,
        out_shape=jax.ShapeDtypeStruct((n_out_tiles * PACK_CHUNKS, LANES), U32),
        compiler_params=pltpu.CompilerParams(dimension_semantics=("arbitrary",),
                                             vmem_limit_bytes=MOE_VMEM_LIMIT),
        name="moe_experts",
    )(bexp, bval, nused, u2p, row_src, w1, w3, w2)


def _combine_kernel(w_hbm, yt_ref, base_ref, mod_ref, g_ref, b_ref, yc_ref, yl_ref,
                    w_smem, acc_lo, acc_hi, sem_w, *, n_ctx_tiles):
    i = pl.program_id(0)
    n_steps = pl.num_programs(0)
    n_tok = acc_lo.shape[0] // PACK_CHUNKS
    n_idx = n_tok * TOP_K

    def w_copy(tile):
        return pltpu.make_async_copy(w_hbm.at[pl.ds(tile * n_idx, n_idx)],
                                     w_smem.at[pl.ds(lax.rem(tile, 2) * n_idx, n_idx)], sem_w)

    @pl.when(i == 0)
    def _():
        w_copy(i).start()

    w_copy(i).wait()

    @pl.when(i + 1 < n_steps)
    def _():
        w_copy(i + 1).start()

    wbase = lax.rem(i, 2) * n_idx

    per_tile = SUBLANES // PACK_CHUNKS
    first = lax.broadcasted_iota(I32, (SUBLANES, LANES), 0) < PACK_CHUNKS

    def reduce_token(t):
        lo = hi = None
        for m in range(TOP_K // per_tile):
            j = t * TOP_K + m * per_tile
            words = yt_ref[pl.ds(pl.multiple_of(j * PACK_CHUNKS, SUBLANES), SUBLANES), :]
            wgt = jnp.where(first, w_smem[wbase + j], w_smem[wbase + j + 1])
            t_lo = wgt * pltpu.bitcast(words << 16, F32)
            t_hi = wgt * pltpu.bitcast(words & jnp.uint32(HIGH_HALF), F32)
            lo = t_lo if lo is None else lo + t_lo
            hi = t_hi if hi is None else hi + t_hi
        row = pl.multiple_of(t * PACK_CHUNKS, PACK_CHUNKS)
        acc_lo[pl.ds(row, PACK_CHUNKS), :] = lo[0:PACK_CHUNKS] + lo[PACK_CHUNKS:SUBLANES]
        acc_hi[pl.ds(row, PACK_CHUNKS), :] = hi[0:PACK_CHUNKS] + hi[PACK_CHUNKS:SUBLANES]

    def reduce_group(i, carry):
        for u in range(COMB_UNROLL):
            reduce_token(i * COMB_UNROLL + u)
        return carry

    lax.fori_loop(0, n_tok // COMB_UNROLL, reduce_group, 0)
    parts = []
    for s in range(PACK_CHUNKS):
        parts.append(acc_lo[pl.ds(s, n_tok, stride=PACK_CHUNKS), :])
        parts.append(acc_hi[pl.ds(s, n_tok, stride=PACK_CHUNKS), :])
    moe = jnp.concatenate(parts, axis=1)
    gate2 = mod_ref[:, 5 * D_MODEL:6 * D_MODEL]
    y = _layer_norm(base_ref[...] + gate2 * moe, g_ref[...], b_ref[...])

    @pl.when(i < n_ctx_tiles)
    def _():
        yc_ref[...] = y

    @pl.when(i >= n_ctx_tiles)
    def _():
        yl_ref[...] = y


def _combine(w_flat, yt, base, mod3, l2g, l2b, n_ctx, seq_tokens):
    n = base.shape[0]
    tc = COMB_TILE
    n_ctx_tiles = n_ctx // tc
    n_seq_tiles = seq_tokens // tc

    def mod_idx(i):
        return (jnp.where(i < n_ctx_tiles, 0, 1 + (i - n_ctx_tiles) // n_seq_tiles), 0, 0)

    full = lambda a: pl.BlockSpec(a.shape, lambda i: (0,) * a.ndim)
    return pl.pallas_call(
        functools.partial(_combine_kernel, n_ctx_tiles=n_ctx_tiles),
        grid=(n // tc,),
        in_specs=[pl.BlockSpec(memory_space=pl.ANY),
                  pl.BlockSpec((tc * TOP_K * PACK_CHUNKS, LANES), lambda i: (i, 0)),
                  pl.BlockSpec((tc, D_MODEL), lambda i: (i, 0)),
                  pl.BlockSpec((None, 1, mod3.shape[2]), mod_idx), full(l2g), full(l2b)],
        out_specs=(pl.BlockSpec((tc, D_MODEL), lambda i: (jnp.minimum(i, n_ctx_tiles - 1), 0)),
                   pl.BlockSpec((tc, D_MODEL), lambda i: (jnp.maximum(i - n_ctx_tiles, 0), 0))),
        out_shape=(jax.ShapeDtypeStruct((n_ctx, D_MODEL), F32),
                   jax.ShapeDtypeStruct((n - n_ctx, D_MODEL), F32)),
        scratch_shapes=[pltpu.SMEM((2 * tc * TOP_K,), F32),
                        pltpu.VMEM((tc * PACK_CHUNKS, LANES), F32),
                        pltpu.VMEM((tc * PACK_CHUNKS, LANES), F32),
                        pltpu.SemaphoreType.DMA],
        compiler_params=_cparams(1), name="combine",
    )(w_flat, yt, base, mod3, l2g, l2b)


def _rope_tables(n_tok, tile):
    rows = n_tok // GRID_W
    row_idx = jnp.repeat(jnp.arange(rows, dtype=F32), GRID_W)
    col_idx = jnp.tile(jnp.arange(GRID_W, dtype=F32), rows)
    inv_freq = 1.0 / (ROPE_THETA ** (jnp.arange(0, ROPE_AXIS_DIM, 2, dtype=F32) / ROPE_AXIS_DIM))
    ang_r = row_idx[:, None] * inv_freq[None, :]
    ang_c = col_idx[:, None] * inv_freq[None, :]
    ang = jnp.concatenate([ang_r, ang_r, ang_c, ang_c], axis=-1)
    cos, sin = jnp.cos(ang), jnp.sin(ang)
    quarter = (jnp.arange(HEAD_DIM) // (ROPE_AXIS_DIM // 2)) % 2
    sin_a = jnp.where(quarter == 0, -sin, 0.0)
    sin_b = jnp.where(quarter == 1, sin, 0.0)
    rep = LANES // HEAD_DIM
    ident = lambda v: jnp.full((tile, LANES), v, F32)
    cos_t = jnp.concatenate([jnp.tile(cos, (1, rep)), ident(1.0)], axis=0)
    sa_t = jnp.concatenate([jnp.tile(sin_a, (1, rep)), ident(0.0)], axis=0)
    sb_t = jnp.concatenate([jnp.tile(sin_b, (1, rep)), ident(0.0)], axis=0)
    ident_tr = lambda v: jnp.full((HEAD_DIM, tile), v, F32)
    cos_tr = jnp.concatenate([cos.T, ident_tr(1.0)], axis=1)
    sin_tr = jnp.concatenate([sin.T, ident_tr(0.0)], axis=1)
    return cos_t, sa_t, sb_t, cos_tr, sin_tr


def _dup_heads(a):
    parts = []
    for h in range(KV_HEADS):
        blk = a[..., h * HEAD_DIM:(h + 1) * HEAD_DIM]
        parts += [blk] * (LANES // HEAD_DIM)
    return jnp.concatenate(parts, axis=-1)


def kernel(x_prompt, x_sample, cache_k, cache_v, state_gla_fwd, state_gla_bwd, c, c_ctx, w_ada, b_ada, w_in, q_norm, k_norm, gla_wa_fwd, gla_ba_fwd, gla_wa_bwd, gla_ba_bwd, gla_norm, w_out, ln1_g, ln1_b, ln2_g, ln2_b, w_router, router_bias, exp_w1, exp_w3, exp_w2, sh_w1, sh_w3, sh_w2):
    n_ctx_b, ctx_seq, _ = x_prompt.shape
    n_lat_b, lat_seq, _ = x_sample.shape
    n_ctx = n_ctx_b * ctx_seq
    n_lat = n_lat_b * lat_seq
    n = n_ctx + n_lat
    l = 0

    x_c = x_prompt.reshape(n_ctx, D_MODEL)
    x_l = x_sample.reshape(n_lat, D_MODEL)

    c_rows = jnp.zeros((SUBLANES, D_MODEL), F32).at[0].set(c_ctx).at[1:1 + n_lat_b].set(c)
    mod = _modulation(c_rows, w_ada[l], b_ada[l][None, :])
    mod3 = mod.reshape(SUBLANES, 1, 6 * D_MODEL)

    wi = w_in[l]
    o_q, o_k, o_v, o_gq, o_gk, o_gv, o_gg, o_rf, o_rb, o_end = np.cumsum(
        [0, ATT_WIDTH, KV_HEADS * HEAD_DIM, KV_HEADS * HEAD_DIM, GLA_KW, GLA_KW, GLA_WIDTH, GLA_WIDTH,
         GLA_GATE_RANK, GLA_GATE_RANK])
    w_tok = jnp.concatenate([
        _dup_heads(wi[:, o_k:o_v]), wi[:, o_v:o_gq], wi[:, o_gq:o_gk],
        wi[:, o_gv:o_gg], wi[:, o_gg:o_rf], wi[:, o_rf:o_end],
        jnp.zeros((D_MODEL, LANES - 2 * GLA_GATE_RANK), F32)], axis=1).astype(BF16)
    w_tr = jnp.concatenate([wi[:, o_q:o_k], wi[:, o_v:o_gq], wi[:, o_gk:o_gv], wi[:, o_rf:o_end]],
                           axis=1).T.astype(BF16)
    rep = LANES // HEAD_DIM
    qn = q_norm[l][:, None]
    kn = jnp.tile(k_norm[l], rep)[None, :]
    seg = jnp.asarray(np.kron(np.eye(rep), np.ones((HEAD_DIM, HEAD_DIM))), BF16)
    wa = jnp.zeros((LANES, 2 * GLA_KW), F32)
    wa = wa.at[0:GLA_GATE_RANK, 0:GLA_KW].set(gla_wa_fwd[l])
    wa = wa.at[GLA_GATE_RANK:2 * GLA_GATE_RANK, GLA_KW:].set(gla_wa_bwd[l])
    ba = jnp.concatenate([gla_ba_fwd[l], gla_ba_bwd[l]])[None, :]
    wat = wa[0:2 * GLA_GATE_RANK, :].T
    bat = ba.T
    cos_t, sa_t, sb_t, cos_tr, sin_tr = _rope_tables(lat_seq, TOK_TILE)

    (qt, k_dup, vt, k32, v32, gq, gv, gg, la, gkt, lat) = _in_projection(
        x_c, x_l, mod3, w_tok, w_tr, qn, kn, cos_t, sa_t, sb_t, cos_tr, sin_tr, seg, wa, ba, wat, bat,
        lat_seq // TOK_TILE)

    ck = _dup_heads(cache_k[:, l].reshape(n_lat_b, -1, KV_HEADS * HEAD_DIM)).astype(BF16)
    cvt = cache_v[:, l].reshape(n_lat_b, -1, KV_HEADS * HEAD_DIM).transpose(0, 2, 1).astype(BF16)
    att_c = _attention(qt, k_dup, vt, None, 0, n_ctx_b, ctx_seq)
    att_l = _attention(qt, k_dup, vt, (ck, cvt), n_ctx, n_lat_b, lat_seq)

    gconst, levels_of = _gla_constants()
    to_dev = lambda t: (jnp.asarray(t[0], BF16), jnp.asarray(t[1], BF16), jnp.asarray(t[2], F32))
    bd = jnp.asarray(np.kron(np.eye(GLA_HEADS), np.ones((GLA_DK, GLA_DV))), BF16)
    vbd = jnp.asarray(np.kron(np.eye(GLA_HEADS), np.ones((GLA_CHUNK, GLA_DV))), BF16)
    consts = ((to_dev(gconst["f"]), to_dev(gconst["b"])), levels_of, bd, vbd)
    s_zero = jnp.zeros((n_ctx_b, GLA_HEADS, GLA_DK, GLA_DV), F32)
    of_c, ob_c, sf_new, sb_new = _gla(gq, la, gkt, lat, gv, s_zero, s_zero, consts, 0, n_ctx_b, ctx_seq)
    of_l, ob_l, _, _ = _gla(gq, la, gkt, lat, gv, state_gla_fwd[:, l], state_gla_bwd[:, l], consts,
                            n_ctx, n_lat_b, lat_seq)

    sw13 = jnp.concatenate([sh_w1[l], sh_w3[l]], axis=1).astype(BF16)
    base, u2_rows, logits_t = _out_projection(
        att_c, att_l, of_c, of_l, ob_c, ob_l, gg, x_c, x_l, mod3, w_out[l].astype(BF16),
        gla_norm[l][None, :], ln1_g[l][None, :], ln1_b[l][None, :], w_router[l].T.astype(BF16), sw13,
        sh_w2[l].astype(BF16), lat_seq // TOK_TILE)

    upper = jnp.asarray(np.triu(np.ones((TOK_TILE, TOK_TILE)), 1), BF16)
    idx_t, w_t, pos_t, counts = _route(logits_t, router_bias[l][:, None], upper)
    n_blocks = n * TOP_K // MOE_ROWS + N_EXPERTS
    n_blocks_pad = -(-n_blocks // LANES) * LANES
    lower = jnp.asarray(np.tril(np.ones((N_EXPERTS, N_EXPERTS)), -1), F32)
    dest_t, bexp, bval, nused = _destinations(counts, lower, idx_t, pos_t, n_blocks_pad)
    dest_flat = dest_t.T.reshape(-1)
    w_flat = w_t.T.reshape(-1)

    row_src = _invert_rows(dest_flat, n_blocks * MOE_ROWS)
    yt = _moe_experts(bexp.reshape(-1), bval.reshape(-1), nused.reshape(-1)[0:1], u2_rows, row_src,
                      exp_w1[l], exp_w3[l], exp_w2[l], n)
    y_c, y_l = _combine(w_flat, yt, base, mod3, ln2_g[l][None, :], ln2_b[l][None, :], n_ctx, lat_seq)

    y_prompt = y_c.reshape(n_ctx_b, ctx_seq, D_MODEL)
    y_sample = y_l.reshape(n_lat_b, lat_seq, D_MODEL)
    new_cache_k = k32.reshape(n_ctx_b, 1, ctx_seq, KV_HEADS, HEAD_DIM)
    new_cache_v = v32.reshape(n_ctx_b, 1, ctx_seq, KV_HEADS, HEAD_DIM)
    return (y_prompt, y_sample, new_cache_k, new_cache_v, sf_new[:, None], sb_new[:, None])
```

```python
import functools

import numpy as np
import jax
import jax.numpy as jnp
from jax import lax
from jax.experimental import pallas as pl
from jax.experimental.pallas import tpu as pltpu
from jax.experimental.pallas import tpu_sc as plsc

F32 = jnp.float32
BF16 = jnp.bfloat16
I32 = jnp.int32

D_MODEL = 1024
GRID_W = 64
HEAD_DIM = 64
N_HEADS = 8
KV_HEADS = 2
ATT_WIDTH = N_HEADS * HEAD_DIM
ATT_SCALE = HEAD_DIM ** -0.5
LOG2_E = 1.4426950408889634
ROPE_AXIS_DIM = HEAD_DIM // 2
ROPE_THETA = 10000.0
GLA_HEADS = 4
GLA_DK = 64
GLA_DV = 128
GLA_WIDTH = GLA_HEADS * GLA_DV
GLA_KW = GLA_HEADS * GLA_DK
GLA_GATE_RANK = 16
GLA_TAU = 16.0
N_EXPERTS = 256
TOP_K = 8
EXPERT_FF = 256
SHARED_FF = 256
ROUTED_SCALE = 2.5
DEPTH = 1
ALPHA = (2.0 * DEPTH) ** 0.25
EPS = 1e-6

LANES = 128
SUBLANES = 8
VMEM_BYTES = 64 * 1024 * 1024
VMEM_LIMIT = VMEM_BYTES - 8 * 1024 * 1024
PACK_CHUNKS = D_MODEL // (2 * LANES)
PACK_CHUNKS_LOG2 = PACK_CHUNKS.bit_length() - 1
HIGH_HALF = 0xFFFF0000
U32 = jnp.uint32

TOK_TILE = 512
ATT_TQ = 128
GLA_CHUNK = 128
GLA_LEVELS = ((32, 128), (8, 32), (2, 8), (1, 2))
MOE_ROWS = 256
MOE_VMEM_LIMIT = VMEM_BYTES - 2 * 1024 * 1024
TOP_K_LOG2 = TOP_K.bit_length() - 1
SRC_GROUP = 4
COMB_TILE = 256
COMB_UNROLL = 8
HIGHEST = lax.Precision.HIGHEST


def _cparams(n_axes):
    return pltpu.CompilerParams(dimension_semantics=("arbitrary",) * n_axes,
                                vmem_limit_bytes=VMEM_LIMIT)


def _silu(x):
    return x * jax.nn.sigmoid(x)


def _log_sigmoid(x):
    return jnp.minimum(x, 0.0) - jnp.log(1.0 + jnp.exp(-jnp.abs(x)))


def _dot_split(a, b):
    a_hi = a.astype(BF16)
    b_hi = b.astype(BF16)
    a_lo = (a - a_hi.astype(F32)).astype(BF16)
    b_lo = (b - b_hi.astype(F32)).astype(BF16)
    dot = functools.partial(jnp.dot, preferred_element_type=F32)
    return dot(a_hi, b_hi) + dot(a_lo, b_hi) + dot(a_hi, b_lo)


def _layer_norm(z, g, b):
    mu = jnp.mean(z, axis=-1, keepdims=True)
    zc = z - mu
    var = jnp.mean(zc * zc, axis=-1, keepdims=True)
    return zc * lax.rsqrt(var + EPS) * g + b


def _mod_kernel(c_ref, w_ref, b_ref, o_ref):
    s = _silu(c_ref[...]).astype(BF16)
    o_ref[...] = jnp.dot(s, w_ref[...].astype(BF16), preferred_element_type=F32) + b_ref[...]


def _modulation(c_rows, w_ada, b_ada):
    n_cols = w_ada.shape[1]
    tn = 512
    return pl.pallas_call(
        _mod_kernel,
        grid=(n_cols // tn,),
        in_specs=[pl.BlockSpec((SUBLANES, D_MODEL), lambda j: (0, 0)),
                  pl.BlockSpec((D_MODEL, tn), lambda j: (0, j)),
                  pl.BlockSpec((1, tn), lambda j: (0, j))],
        out_specs=pl.BlockSpec((SUBLANES, tn), lambda j: (0, j)),
        out_shape=jax.ShapeDtypeStruct((SUBLANES, n_cols), F32),
        compiler_params=_cparams(1),
        name="modulation",
    )(c_rows, w_ada, b_ada)


_C_K = 0
_C_V = _C_K + 2 * LANES
_C_GQ = _C_V + KV_HEADS * HEAD_DIM
_C_GV = _C_GQ + GLA_KW
_C_GG = _C_GV + GLA_WIDTH
_C_RA = _C_GG + GLA_WIDTH
_C_END = _C_RA + LANES
_R_Q = 0
_R_V = _R_Q + ATT_WIDTH
_R_GK = _R_V + KV_HEADS * HEAD_DIM
_R_RA = _R_GK + GLA_KW
_R_END = _R_RA + 2 * GLA_GATE_RANK


def _inproj_kernel(xc_ref, xl_ref, mod_ref, w_ref, wt_ref, qn_ref, kn_ref, cos_ref, sa_ref, sb_ref,
                   cost_ref, sint_ref, seg_ref, wa_ref, ba_ref, wat_ref, bat_ref,
                   qt_ref, k_ref, vt_ref, k32_ref, v32_ref, gq_ref, gv_ref, gg_ref,
                   la_ref, gkt_ref, lat_ref, *, n_ctx_tiles):
    i = pl.program_id(0)
    m = mod_ref[...]
    shift1 = m[:, 0:D_MODEL]
    scale1 = m[:, D_MODEL:2 * D_MODEL]
    x = jnp.where(i < n_ctx_tiles, xc_ref[...], xl_ref[...])
    u = (x * (1.0 + scale1) + shift1).astype(BF16)

    cos = cos_ref[...]
    sin_a = sa_ref[...]
    sin_b = sb_ref[...]
    seg = seg_ref[...]
    lane = lax.broadcasted_iota(I32, (u.shape[0], LANES), 1)
    low = lane < HEAD_DIM

    def proj(c0, c1):
        return jnp.dot(u, w_ref[:, c0:c1], preferred_element_type=F32)

    def head_norm(blk, gain):
        ss = jnp.dot((blk * blk).astype(BF16), seg, preferred_element_type=F32) * (1.0 / HEAD_DIM)
        return blk * lax.rsqrt(ss + EPS) * gain

    def rope(blk):
        return (blk * cos + pltpu.roll(blk, LANES - ROPE_AXIS_DIM // 2, 1) * sin_a
                + pltpu.roll(blk, ROPE_AXIS_DIM // 2, 1) * sin_b)

    pk = proj(_C_K, _C_V)
    kn = [head_norm(pk[:, j * LANES:(j + 1) * LANES], kn_ref[...]) for j in range(KV_HEADS)]
    for j in range(KV_HEADS):
        k_ref[:, j * LANES:(j + 1) * LANES] = rope(kn[j]).astype(BF16)

    @pl.when(i < n_ctx_tiles)
    def _():
        k32_ref[...] = jnp.where(low, kn[0], kn[1])
        v32_ref[...] = proj(_C_V, _C_GQ)

    gq_ref[...] = proj(_C_GQ, _C_GV) * (GLA_DK ** -0.5)
    gv_ref[...] = proj(_C_GV, _C_GG).astype(BF16)
    gg_ref[...] = proj(_C_GG, _C_RA).astype(BF16)

    ra = proj(_C_RA, _C_END)
    pre = _dot_split(ra, wa_ref[...]) + ba_ref[...]
    la_ref[...] = _log_sigmoid(pre) * (1.0 / GLA_TAU)

    pt = lax.dot_general(wt_ref[...], u, (((1,), (1,)), ((), ())), preferred_element_type=F32)
    cos_t = cost_ref[...]
    sin_t = sint_ref[...]
    quarter = ROPE_AXIS_DIM // 2
    for h in range(N_HEADS):
        blk = pt[_R_Q + h * HEAD_DIM:_R_Q + (h + 1) * HEAD_DIM, :]
        ms = jnp.mean(blk * blk, axis=0, keepdims=True)
        qn = blk * lax.rsqrt(ms + EPS) * qn_ref[...]
        rot = jnp.concatenate([-qn[quarter:2 * quarter], qn[0:quarter],
                               -qn[3 * quarter:4 * quarter], qn[2 * quarter:3 * quarter]], axis=0)
        qt_ref[h * HEAD_DIM:(h + 1) * HEAD_DIM, :] = (
            (qn * cos_t + rot * sin_t) * (ATT_SCALE * LOG2_E)).astype(BF16)
    vt_ref[...] = pt[_R_V:_R_GK, :].astype(BF16)
    gkt_ref[...] = pt[_R_GK:_R_RA, :]
    rat = pt[_R_RA:_R_END, :]
    pre_t = _dot_split(wat_ref[...], rat) + bat_ref[...]
    lat_ref[...] = _log_sigmoid(pre_t) * (1.0 / GLA_TAU)


def _in_projection(x_c, x_l, mod3, w_tok, w_tr, qn, kn, cos_t, sa_t, sb_t, cos_tr, sin_tr, seg, wa, ba,
                   wat, bat, n_seq_tiles):
    n_ctx = x_c.shape[0]
    n = n_ctx + x_l.shape[0]
    tb = TOK_TILE
    n_ctx_tiles = n_ctx // tb
    n_tiles = n // tb
    n_rope_blocks = cos_t.shape[0] // tb - 1

    def mod_idx(i):
        return (jnp.where(i < n_ctx_tiles, 0, 1 + (i - n_ctx_tiles) // n_seq_tiles), 0, 0)

    def rope_blk(i):
        return jnp.where(i < n_ctx_tiles, n_rope_blocks, (i - n_ctx_tiles) % n_seq_tiles)

    def rope_idx(i):
        return (rope_blk(i), 0)

    def ctx_idx(i):
        return (jnp.minimum(i, n_ctx_tiles - 1), 0)

    tok = lambda w: pl.BlockSpec((tb, w), lambda i: (i, 0))
    full = lambda a: pl.BlockSpec(a.shape, lambda i: (0,) * a.ndim)
    tr = lambda r: pl.BlockSpec((r, tb), lambda i: (0, i))
    rope_tr = pl.BlockSpec((HEAD_DIM, tb), lambda i: (0, rope_blk(i)))
    out_shapes = (
        jax.ShapeDtypeStruct((ATT_WIDTH, n), BF16),
        jax.ShapeDtypeStruct((n, 2 * LANES), BF16),
        jax.ShapeDtypeStruct((KV_HEADS * HEAD_DIM, n), BF16),
        jax.ShapeDtypeStruct((n_ctx, LANES), F32),
        jax.ShapeDtypeStruct((n_ctx, LANES), F32),
        jax.ShapeDtypeStruct((n, GLA_KW), F32),
        jax.ShapeDtypeStruct((n, GLA_WIDTH), BF16),
        jax.ShapeDtypeStruct((n, GLA_WIDTH), BF16),
        jax.ShapeDtypeStruct((n, 2 * GLA_KW), F32),
        jax.ShapeDtypeStruct((GLA_KW, n), F32),
        jax.ShapeDtypeStruct((2 * GLA_KW, n), F32),
    )
    out_specs = (tr(ATT_WIDTH), tok(2 * LANES), tr(KV_HEADS * HEAD_DIM),
                 pl.BlockSpec((tb, LANES), ctx_idx), pl.BlockSpec((tb, LANES), ctx_idx),
                 tok(GLA_KW), tok(GLA_WIDTH), tok(GLA_WIDTH), tok(2 * GLA_KW),
                 tr(GLA_KW), tr(2 * GLA_KW))
    in_specs = [pl.BlockSpec((tb, D_MODEL), ctx_idx),
                pl.BlockSpec((tb, D_MODEL), lambda i: (jnp.maximum(i - n_ctx_tiles, 0), 0)),
                pl.BlockSpec((None, 1, mod3.shape[2]), mod_idx),
                full(w_tok), full(w_tr), full(qn), full(kn),
                pl.BlockSpec((tb, LANES), rope_idx), pl.BlockSpec((tb, LANES), rope_idx),
                pl.BlockSpec((tb, LANES), rope_idx), rope_tr, rope_tr,
                full(seg), full(wa), full(ba), full(wat), full(bat)]
    return pl.pallas_call(
        functools.partial(_inproj_kernel, n_ctx_tiles=n_ctx_tiles),
        grid=(n_tiles,), in_specs=in_specs, out_specs=out_specs, out_shape=out_shapes,
        compiler_params=_cparams(1), name="in_projection",
    )(x_c, x_l, mod3, w_tok, w_tr, qn, kn, cos_t, sa_t, sb_t, cos_tr, sin_tr, seg, wa, ba, wat, bat)


def _attention_kernel(*refs, n_kv_parts):
    qt_ref = refs[0]
    k_refs = refs[1:1 + n_kv_parts]
    vt_refs = refs[1 + n_kv_parts:1 + 2 * n_kv_parts]
    o_ref = refs[1 + 2 * n_kv_parts]
    tq = qt_ref.shape[1]
    group = N_HEADS // KV_HEADS
    scores = []
    for kv in range(KV_HEADS):
        heads = range(kv * group, (kv + 1) * group)
        q_grp = jnp.concatenate([qt_ref[h * HEAD_DIM:(h + 1) * HEAD_DIM, :] for h in heads], axis=1)
        rhs = jnp.concatenate([q_grp, jnp.zeros_like(q_grp)], axis=0)
        scores.append([jnp.dot(k[:, kv * LANES:(kv + 1) * LANES], rhs, preferred_element_type=F32)
                       for k in k_refs])
    for kv in range(KV_HEADS):
        heads = range(kv * group, (kv + 1) * group)
        s = scores[kv]
        mx = functools.reduce(jnp.maximum, [jnp.max(x, axis=0, keepdims=True) for x in s])
        pr = [jnp.exp2(x - mx) for x in s]
        den = functools.reduce(jnp.add, [jnp.sum(x, axis=0, keepdims=True) for x in pr])
        acc = functools.reduce(jnp.add, [
            jnp.dot(vt[kv * HEAD_DIM:(kv + 1) * HEAD_DIM, :], x.astype(BF16),
                    preferred_element_type=F32) for x, vt in zip(pr, vt_refs)])
        out = (acc / den).astype(BF16)
        for j, h in enumerate(heads):
            o_ref[h * HEAD_DIM:(h + 1) * HEAD_DIM, :] = out[:, j * tq:(j + 1) * tq]


def _attention(qt, k, vt, extra_kv, row0, n_batch, seq):
    tq = ATT_TQ
    n_q = seq // tq
    q_blk0 = row0 // tq
    kv_blk0 = row0 // seq
    in_specs = [pl.BlockSpec((ATT_WIDTH, tq), lambda b, i: (0, q_blk0 + b * n_q + i))]
    k_spec = pl.BlockSpec((seq, 2 * LANES), lambda b, i: (kv_blk0 + b, 0))
    vt_spec = pl.BlockSpec((KV_HEADS * HEAD_DIM, seq), lambda b, i: (0, kv_blk0 + b))
    args_k, args_v, specs_k, specs_v = [k], [vt], [k_spec], [vt_spec]
    if extra_kv is not None:
        ck, cvt = extra_kv
        args_k.append(ck)
        args_v.append(cvt)
        specs_k.append(pl.BlockSpec((None, ck.shape[1], 2 * LANES), lambda b, i: (b, 0, 0)))
        specs_v.append(pl.BlockSpec((None, KV_HEADS * HEAD_DIM, cvt.shape[2]), lambda b, i: (b, 0, 0)))
    return pl.pallas_call(
        functools.partial(_attention_kernel, n_kv_parts=len(args_k)),
        grid=(n_batch, n_q),
        in_specs=in_specs + specs_k + specs_v,
        out_specs=pl.BlockSpec((ATT_WIDTH, tq), lambda b, i: (0, b * n_q + i)),
        out_shape=jax.ShapeDtypeStruct((ATT_WIDTH, n_batch * seq), BF16),
        compiler_params=_cparams(2), name="attention",
    )(qt, *args_k, *args_v)


def _gla_constants():
    c = GLA_CHUNK
    idx = np.arange(c)
    q_mats, k_mats, masks, levels_of = [], [], [], []
    for li, (s, p) in enumerate(GLA_LEVELS):
        start = (idx // s) * s
        end = start + s - 1
        k_mats.append(((idx[None, :] > idx[:, None]) & (idx[None, :] <= end[:, None])))
        for d in range(p // s - 1):
            lo = np.maximum(start - d * s, 0)
            q_mats.append((idx[None, :] >= lo[:, None]) & (idx[None, :] <= idx[:, None]))
            masks.append((idx[:, None] // p == idx[None, :] // p)
                         & (idx[:, None] // s - idx[None, :] // s - 1 == d))
            levels_of.append(li)
    masks.append(np.eye(c, dtype=bool))
    levels_of.append(len(GLA_LEVELS) - 1)
    q_mats.append(idx[None, :] <= idx[:, None])
    k_mats = k_mats[:-1]
    k_mats.append(idx[None, :] > idx[:, None])
    k_mats.append(np.ones((c, c), bool))
    out = {}
    for name, flip in (("f", False), ("b", True)):
        f = (lambda a: a[::-1, ::-1]) if flip else (lambda a: a)
        lq = np.concatenate([f(a) for a in q_mats], axis=0).astype(np.float32)
        lkt = np.concatenate([f(a).T for a in k_mats], axis=1).astype(np.float32)
        mk = np.stack([np.tile(f(a), (1, GLA_HEADS)) for a in masks]).astype(np.float32)
        out[name] = (np.concatenate([lq, lq], axis=1), np.concatenate([lkt, lkt], axis=0), mk)
    return out, tuple(levels_of)


def _gla_direction(q, g, gkt, gt, v, lq2, lkt2, masks_ref, bd, vbd, s_ref, levels_of):
    c = GLA_CHUNK
    n_var = len(levels_of)
    n_lev = len(GLA_LEVELS)
    g_hi = g.astype(BF16)
    g_lo = (g - g_hi.astype(F32)).astype(BF16)
    fq = jnp.dot(lq2, jnp.concatenate([g_hi, g_lo], axis=0), preferred_element_type=F32)
    gt_hi = gt.astype(BF16)
    gt_lo = (gt - gt_hi.astype(F32)).astype(BF16)
    fk = jnp.dot(jnp.concatenate([gt_hi, gt_lo], axis=1), lkt2, preferred_element_type=F32)

    def key_factor(f):
        return gkt * jnp.exp(fk[:, f * c:(f + 1) * c])

    q_var = [(q * jnp.exp(fq[vi * c:(vi + 1) * c, :])).astype(BF16) for vi in range(n_var - 1)]
    q_var.append(q.astype(BF16))
    a = jnp.zeros((c, GLA_HEADS * c), F32)
    for li in range(n_lev):
        kt = (key_factor(li) if li < n_lev - 1 else gkt).astype(BF16)
        xt = jnp.concatenate([kt] * GLA_HEADS, axis=1) * bd
        vis = [vi for vi in range(n_var) if levels_of[vi] == li]
        res = jnp.dot(jnp.concatenate([q_var[vi] for vi in vis], axis=0), xt,
                      preferred_element_type=F32)
        for r, vi in enumerate(vis):
            a = a + masks_ref[vi] * res[r * c:(r + 1) * c, :]
    q_in = (q * jnp.exp(fq[(n_var - 1) * c:n_var * c, :])).astype(BF16)
    state = s_ref[...]
    v_bd = jnp.concatenate([v] * GLA_HEADS, axis=0) * vbd
    o = (jnp.dot(q_in, state.astype(BF16), preferred_element_type=F32)
         + jnp.dot(a.astype(BF16), v_bd, preferred_element_type=F32))
    k_out = key_factor(n_lev - 1).astype(BF16)
    e_tot = jnp.exp(fk[:, n_lev * c:(n_lev + 1) * c])
    upd = jnp.dot(k_out, v, preferred_element_type=F32)
    s_ref[...] = (state * jnp.concatenate([e_tot] * (GLA_WIDTH // c), axis=1)
                  + upd * bd.astype(F32))
    return o


def _gla_kernel(gq_f, la_f, gkt_f, lat_f, gv_f, gq_b, la_b, gkt_b, lat_b, gv_b,
                s0f_ref, s0b_ref, lq2f, lkt2f, mkf, lq2b, lkt2b, mkb, bd_ref, vbd_ref,
                of_ref, ob_ref, sf_ref, sb_ref, st_f, st_b, *, levels_of):
    n = pl.program_id(1)

    @pl.when(n == 0)
    def _():
        st_f[...] = jnp.zeros_like(st_f)
        st_b[...] = jnp.zeros_like(st_b)
        for h in range(GLA_HEADS):
            rows = slice(h * GLA_DK, (h + 1) * GLA_DK)
            cols = slice(h * GLA_DV, (h + 1) * GLA_DV)
            st_f[rows, cols] = s0f_ref[h]
            st_b[rows, cols] = s0b_ref[h]

    bd = bd_ref[...]
    vbd = vbd_ref[...]
    of_ref[...] = _gla_direction(gq_f[...], la_f[...], gkt_f[...], lat_f[...], gv_f[...],
                                 lq2f[...], lkt2f[...], mkf, bd, vbd, st_f, levels_of)
    ob_ref[...] = _gla_direction(gq_b[...], la_b[...], gkt_b[...], lat_b[...], gv_b[...],
                                 lq2b[...], lkt2b[...], mkb, bd, vbd, st_b, levels_of)

    @pl.when(n == pl.num_programs(1) - 1)
    def _():
        for h in range(GLA_HEADS):
            rows = slice(h * GLA_DK, (h + 1) * GLA_DK)
            cols = slice(h * GLA_DV, (h + 1) * GLA_DV)
            sf_ref[h] = st_f[rows, cols]
            sb_ref[h] = st_b[rows, cols]


def _gla(gq, la, gkt, lat, gv, s0f, s0b, consts, row0, n_batch, seq):
    (cf, cb), levels_of, bd, vbd = consts
    c = GLA_CHUNK
    nc = seq // c
    blk0 = row0 // c
    n_la_blocks_b = 1
    fwd = lambda b, n: blk0 + b * nc + n
    bwd = lambda b, n: blk0 + b * nc + (nc - 1 - n)

    def tok(w, which, col=0):
        return pl.BlockSpec((c, w), lambda b, n: (which(b, n), col))

    def tr(r, which, row=0):
        return pl.BlockSpec((r, c), lambda b, n: (row, which(b, n)))

    full = lambda a: pl.BlockSpec(a.shape, lambda b, n: (0,) * a.ndim)
    st_spec = pl.BlockSpec((None, GLA_HEADS, GLA_DK, GLA_DV), lambda b, n: (b, 0, 0, 0))
    in_specs = [tok(GLA_KW, fwd), tok(GLA_KW, fwd, 0), tr(GLA_KW, fwd), tr(GLA_KW, fwd, 0),
                tok(GLA_WIDTH, fwd),
                tok(GLA_KW, bwd), tok(GLA_KW, bwd, n_la_blocks_b), tr(GLA_KW, bwd),
                tr(GLA_KW, bwd, 1), tok(GLA_WIDTH, bwd),
                st_spec, st_spec,
                full(cf[0]), full(cf[1]), full(cf[2]), full(cb[0]), full(cb[1]), full(cb[2]),
                full(bd), full(vbd)]
    out_specs = (pl.BlockSpec((c, GLA_WIDTH), lambda b, n: (b * nc + n, 0)),
                 pl.BlockSpec((c, GLA_WIDTH), lambda b, n: (b * nc + (nc - 1 - n), 0)),
                 st_spec, st_spec)
    out_shape = (jax.ShapeDtypeStruct((n_batch * seq, GLA_WIDTH), F32),
                 jax.ShapeDtypeStruct((n_batch * seq, GLA_WIDTH), F32),
                 jax.ShapeDtypeStruct((n_batch, GLA_HEADS, GLA_DK, GLA_DV), F32),
                 jax.ShapeDtypeStruct((n_batch, GLA_HEADS, GLA_DK, GLA_DV), F32))
    return pl.pallas_call(
        functools.partial(_gla_kernel, levels_of=levels_of),
        grid=(n_batch, nc), in_specs=in_specs, out_specs=out_specs, out_shape=out_shape,
        scratch_shapes=[pltpu.VMEM((GLA_KW, GLA_WIDTH), F32), pltpu.VMEM((GLA_KW, GLA_WIDTH), F32)],
        compiler_params=_cparams(2), name="gla",
    )(gq, la, gkt, lat, gv, gq, la, gkt, lat, gv, s0f, s0b,
      cf[0], cf[1], cf[2], cb[0], cb[1], cb[2], bd, vbd)


def _outproj_kernel(attc_ref, attl_ref, ofc_ref, ofl_ref, obc_ref, obl_ref, gg_ref, xc_ref, xl_ref,
                    mod_ref, wo_ref, gn_ref, l1g_ref, l1b_ref, wrt_ref, sw13_ref, sw2_ref,
                    base_ref, u2_ref, lg_ref, *, n_ctx_tiles):
    is_ctx = pl.program_id(0) < n_ctx_tiles
    pick = lambda a_ref, b_ref: jnp.where(is_ctx, a_ref[...], b_ref[...])
    m = mod_ref[...]
    gate1 = m[:, 2 * D_MODEL:3 * D_MODEL]
    shift2 = m[:, 3 * D_MODEL:4 * D_MODEL]
    scale2 = m[:, 4 * D_MODEL:5 * D_MODEL]
    gate2 = m[:, 5 * D_MODEL:6 * D_MODEL]
    og = pick(ofc_ref, ofl_ref) + pick(obc_ref, obl_ref)
    gg = gg_ref[...].astype(F32)
    parts = []
    for h in range(GLA_HEADS):
        blk = og[:, h * GLA_DV:(h + 1) * GLA_DV]
        ms = jnp.mean(blk * blk, axis=-1, keepdims=True)
        nb = blk * lax.rsqrt(ms + EPS) * gn_ref[...]
        parts.append((nb * _silu(gg[:, h * GLA_DV:(h + 1) * GLA_DV])).astype(BF16))
    att_t = pick(attc_ref, attl_ref)
    hmix = (lax.dot_general(att_t, wo_ref[0:ATT_WIDTH, :], (((0,), (0,)), ((), ())),
                            preferred_element_type=F32)
            + jnp.dot(jnp.concatenate(parts, axis=1), wo_ref[ATT_WIDTH:, :],
                      preferred_element_type=F32))
    x1 = _layer_norm(ALPHA * pick(xc_ref, xl_ref) + gate1 * hmix, l1g_ref[...], l1b_ref[...])
    u2 = x1 * (1.0 + scale2) + shift2
    u2b = u2.astype(BF16)
    lg_ref[...] = lax.dot_general(wrt_ref[...], u2b, (((1,), (1,)), ((), ())),
                                  preferred_element_type=F32)
    ab = jnp.dot(u2b, sw13_ref[...], preferred_element_type=F32)
    hid = (_silu(ab[:, 0:SHARED_FF]) * ab[:, SHARED_FF:2 * SHARED_FF]).astype(BF16)
    shared = jnp.dot(hid, sw2_ref[...], preferred_element_type=F32)
    base_ref[...] = ALPHA * x1 + gate2 * shared
    _pack_rows(u2_ref, u2)


def _out_projection(att_c, att_l, of_c, of_l, ob_c, ob_l, gg, x_c, x_l, mod3, wo, gn, l1g, l1b, wrt,
                    sw13, sw2, n_seq_tiles):
    n_ctx = x_c.shape[0]
    n = n_ctx + x_l.shape[0]
    tb = TOK_TILE
    n_ctx_tiles = n_ctx // tb

    def mod_idx(i):
        return (jnp.where(i < n_ctx_tiles, 0, 1 + (i - n_ctx_tiles) // n_seq_tiles), 0, 0)

    ctx_blk = lambda i: jnp.minimum(i, n_ctx_tiles - 1)
    lat_blk = lambda i: jnp.maximum(i - n_ctx_tiles, 0)
    tok = lambda w: pl.BlockSpec((tb, w), lambda i: (i, 0))
    tok_c = lambda w: pl.BlockSpec((tb, w), lambda i: (ctx_blk(i), 0))
    tok_l = lambda w: pl.BlockSpec((tb, w), lambda i: (lat_blk(i), 0))
    full = lambda a: pl.BlockSpec(a.shape, lambda i: (0,) * a.ndim)
    return pl.pallas_call(
        functools.partial(_outproj_kernel, n_ctx_tiles=n_ctx_tiles),
        grid=(n // tb,),
        in_specs=[pl.BlockSpec((ATT_WIDTH, tb), lambda i: (0, ctx_blk(i))),
                  pl.BlockSpec((ATT_WIDTH, tb), lambda i: (0, lat_blk(i))),
                  tok_c(GLA_WIDTH), tok_l(GLA_WIDTH), tok_c(GLA_WIDTH), tok_l(GLA_WIDTH),
                  tok(GLA_WIDTH), tok_c(D_MODEL), tok_l(D_MODEL),
                  pl.BlockSpec((None, 1, mod3.shape[2]), mod_idx),
                  full(wo), full(gn), full(l1g), full(l1b), full(wrt), full(sw13), full(sw2)],
        out_specs=(tok(D_MODEL),
                   pl.BlockSpec((tb * PACK_CHUNKS, LANES), lambda i: (i, 0)),
                   pl.BlockSpec((N_EXPERTS, tb), lambda i: (0, i))),
        out_shape=(jax.ShapeDtypeStruct((n, D_MODEL), F32),
                   jax.ShapeDtypeStruct((n * PACK_CHUNKS, LANES), U32),
                   jax.ShapeDtypeStruct((N_EXPERTS, n), F32)),
        compiler_params=_cparams(1), name="out_projection",
    )(att_c, att_l, of_c, of_l, ob_c, ob_l, gg, x_c, x_l, mod3, wo, gn, l1g, l1b, wrt, sw13, sw2)


def _route_kernel(lg_ref, bias_ref, upper_ref, idx_ref, w_ref, pos_ref, cnt_ref, run_ref):
    i = pl.program_id(0)

    @pl.when(i == 0)
    def _():
        run_ref[...] = jnp.zeros_like(run_ref)

    s = jax.nn.sigmoid(lg_ref[...])
    work = s + bias_ref[...]
    rows = lax.broadcasted_iota(I32, s.shape, 0).astype(F32)
    sel = jnp.zeros(s.shape, F32)
    idxs, vals = [], []
    for _ in range(TOP_K):
        mx = jnp.max(work, axis=0, keepdims=True)
        idx = jnp.min(jnp.where(work == mx, rows, float(N_EXPERTS)), axis=0, keepdims=True)
        hit = rows == idx
        vals.append(jnp.sum(jnp.where(hit, s, 0.0), axis=0, keepdims=True))
        idxs.append(idx)
        sel = jnp.where(hit, 1.0, sel)
        work = jnp.where(hit, -jnp.inf, work)
    den = functools.reduce(jnp.add, vals)
    rank = jnp.dot(sel.astype(BF16), upper_ref[...], preferred_element_type=F32) + run_ref[:, 0:1]
    for k in range(TOP_K):
        idx_ref[k:k + 1, :] = idxs[k].astype(I32)
        w_ref[k:k + 1, :] = vals[k] / den * ROUTED_SCALE
        pos_ref[k:k + 1, :] = jnp.sum(jnp.where(rows == idxs[k], rank, 0.0), axis=0,
                                      keepdims=True).astype(I32)
    run_ref[...] = run_ref[...] + jnp.sum(sel, axis=1, keepdims=True)
    cnt_ref[...] = run_ref[...]


def _route(logits_t, bias_col, upper):
    n = logits_t.shape[1]
    tt = TOK_TILE
    row = lambda dt: jax.ShapeDtypeStruct((TOP_K, n), dt)
    blk = pl.BlockSpec((TOP_K, tt), lambda i: (0, i))
    return pl.pallas_call(
        _route_kernel,
        grid=(n // tt,),
        in_specs=[pl.BlockSpec((N_EXPERTS, tt), lambda i: (0, i)),
                  pl.BlockSpec((N_EXPERTS, 1), lambda i: (0, 0)),
                  pl.BlockSpec((tt, tt), lambda i: (0, 0))],
        out_specs=(blk, blk, blk, pl.BlockSpec((N_EXPERTS, LANES), lambda i: (0, 0))),
        out_shape=(row(I32), row(F32), row(I32), jax.ShapeDtypeStruct((N_EXPERTS, LANES), F32)),
        scratch_shapes=[pltpu.VMEM((N_EXPERTS, LANES), F32)],
        compiler_params=_cparams(1), name="route",
    )(logits_t, bias_col, upper)


def _dest_kernel(cnt_ref, lower_ref, idx_ref, pos_ref, dest_ref, bexp_ref, bval_ref, nused_ref):
    cnt = cnt_ref[...]
    nblk = jnp.floor((cnt + (MOE_ROWS - 1)) * (1.0 / MOE_ROWS))
    bstart = jnp.dot(lower_ref[...], nblk, precision=HIGHEST, preferred_element_type=F32)
    bend = bstart + nblk
    pstart = bstart[:, 0:1] * MOE_ROWS
    rows = lax.broadcasted_iota(I32, (N_EXPERTS, idx_ref.shape[1]), 0)
    for k in range(TOP_K):
        hit = rows == idx_ref[k:k + 1, :]
        dest_ref[k:k + 1, :] = (jnp.sum(jnp.where(hit, pstart, 0.0), axis=0, keepdims=True)
                                .astype(I32) + pos_ref[k:k + 1, :])

    @pl.when(pl.program_id(0) == 0)
    def _():
        nb = bexp_ref.shape[1]
        bid = lax.broadcasted_iota(I32, (N_EXPERTS, nb), 1).astype(F32)
        inside = jnp.logical_and(bid >= bstart[:, 0:1], bid < bend[:, 0:1])
        erow = lax.broadcasted_iota(I32, (N_EXPERTS, nb), 0).astype(F32)
        bexp_ref[...] = jnp.sum(jnp.where(inside, erow, 0.0), axis=0, keepdims=True).astype(I32)
        valid = jnp.clip(cnt[:, 0:1] - (bid - bstart[:, 0:1]) * MOE_ROWS, 0.0, float(MOE_ROWS))
        bval_ref[...] = jnp.sum(jnp.where(inside, valid, 0.0), axis=0, keepdims=True).astype(I32)
        nused_ref[...] = jnp.max(bend, axis=0, keepdims=True).astype(I32)


def _destinations(counts, lower, idx_t, pos_t, n_blocks_pad):
    n = idx_t.shape[1]
    tt = TOK_TILE
    blk = pl.BlockSpec((TOP_K, tt), lambda i: (0, i))
    one = lambda w: pl.BlockSpec((1, w), lambda i: (0, 0))
    return pl.pallas_call(
        _dest_kernel,
        grid=(n // tt,),
        in_specs=[pl.BlockSpec((N_EXPERTS, LANES), lambda i: (0, 0)),
                  pl.BlockSpec((N_EXPERTS, N_EXPERTS), lambda i: (0, 0)), blk, blk],
        out_specs=(blk, one(n_blocks_pad), one(n_blocks_pad), one(LANES)),
        out_shape=(jax.ShapeDtypeStruct((TOP_K, n), I32),
                   jax.ShapeDtypeStruct((1, n_blocks_pad), I32),
                   jax.ShapeDtypeStruct((1, n_blocks_pad), I32),
                   jax.ShapeDtypeStruct((1, LANES), I32)),
        compiler_params=_cparams(1), name="destinations",
    )(counts, lower, idx_t, pos_t)


SC_WINDOW = 128
SC_WINDOWS_PER_STEP = 8


def _invert_rows(dest, ids, n_rows):
    m = dest.size
    mesh = plsc.VectorSubcoreMesh(core_axis_name="core", subcore_axis_name="subcore")

    @functools.partial(pl.kernel, out_type=jax.ShapeDtypeStruct((n_rows,), I32), mesh=mesh,
                       scratch_types=[])
    def invert(val_hbm, idx_hbm, out_hbm):
        def body(val_vmem, idx_vmem):
            for j in range(SC_WINDOWS_PER_STEP):
                pltpu.sync_copy(val_vmem.at[j], out_hbm.at[idx_vmem.at[j]])

        blk = pl.BlockSpec((SC_WINDOWS_PER_STEP, SC_WINDOW), lambda i: (i, 0))
        pltpu.emit_pipeline(
            body, grid=(m // (SC_WINDOW * SC_WINDOWS_PER_STEP),),
            in_specs=[blk, blk], out_specs=[], core_axis_name=("core", "subcore"),
            dimension_semantics=(pltpu.PARALLEL,),
        )(val_hbm, idx_hbm)

    shape = (m // SC_WINDOW, SC_WINDOW)
    return invert(jnp.asarray(ids.reshape(shape), I32), dest.reshape(shape))


def _pack_rows(ref, x, row0=0):
    bits = pltpu.bitcast(x.astype(BF16).astype(F32), U32)
    for s in range(PACK_CHUNKS):
        lo = bits[:, (2 * s) * LANES:(2 * s + 1) * LANES] >> 16
        hi = bits[:, (2 * s + 1) * LANES:(2 * s + 2) * LANES] & jnp.uint32(HIGH_HALF)
        ref[pl.ds(row0 + s, x.shape[0], stride=PACK_CHUNKS), :] = lo | hi


def _unpack_rows(ref, n_rows, row0=0):
    parts = []
    for s in range(PACK_CHUNKS):
        w = ref[pl.ds(row0 + s, n_rows, stride=PACK_CHUNKS), :]
        parts.append(pltpu.bitcast(w << 16, F32))
        parts.append(pltpu.bitcast(w & jnp.uint32(HIGH_HALF), F32))
    return jnp.concatenate(parts, axis=1).astype(BF16)


def _moe_kernel(bexp_ref, bval_ref, nused_ref, u2p_hbm, src_hbm, w1_hbm, w3_hbm, w2_hbm, yt_hbm,
                u2p_vmem, w1_f, w3_f, w2_f, w13_s, w2_s, xbuf, ybuf, src_smem,
                sem_in, sem_src, sem_w, sem_out, *, n_tokens):
    n_used = nused_ref[0]
    br = MOE_ROWS
    grp = SRC_GROUP * br
    trash0 = n_tokens * TOP_K

    def src_copy(g):
        return pltpu.make_async_copy(src_hbm.at[pl.ds(g * grp, grp)],
                                     src_smem.at[pl.ds(lax.rem(g, 2) * grp, grp)], sem_src)

    def out_wait(slot):
        pltpu.make_async_copy(ybuf.at[pl.ds(slot * br * PACK_CHUNKS, br * PACK_CHUNKS)],
                              yt_hbm.at[pl.ds(0, br * PACK_CHUNKS)], sem_out.at[slot]).wait()

    def src_base(blk):
        return lax.rem(blk // SRC_GROUP, 2) * grp + lax.rem(blk, SRC_GROUP) * br

    def scatter_row(blk_slot, sbase, valid, r, priority=0):
        dst = jnp.where(r < valid, src_smem[sbase + r], trash0 + blk_slot * br + r)
        pltpu.make_async_copy(
            ybuf.at[pl.ds(pl.multiple_of((blk_slot * br + r) * PACK_CHUNKS, PACK_CHUNKS), PACK_CHUNKS)],
            yt_hbm.at[pl.ds(pl.multiple_of(dst * PACK_CHUNKS, PACK_CHUNKS), PACK_CHUNKS)],
            sem_out.at[blk_slot]).start(priority=priority)

    def gather_row(xslot, sbase, r):
        row = lax.shift_right_logical(src_smem[sbase + r], TOP_K_LOG2 - PACK_CHUNKS_LOG2)
        row = jnp.minimum(row & (int(jnp.iinfo(I32).max) - (PACK_CHUNKS - 1)),
                          (n_tokens - 1) * PACK_CHUNKS)
        dst = pl.multiple_of((xslot * br + r) * PACK_CHUNKS, PACK_CHUNKS)
        xbuf[pl.ds(dst, PACK_CHUNKS), :] = u2p_vmem[pl.ds(pl.multiple_of(row, PACK_CHUNKS), PACK_CHUNKS), :]

    def weight_copies(e, wslot):
        return [pltpu.make_async_copy(src.at[e], dst.at[wslot], sem_w.at[wslot])
                for src, dst in ((w1_hbm, w1_f), (w3_hbm, w3_f), (w2_hbm, w2_f))]

    cp = pltpu.make_async_copy(u2p_hbm, u2p_vmem, sem_in)
    cp.start()
    src_copy(0).start()
    for wcp in weight_copies(bexp_ref[0], 0):
        wcp.start()
    ybuf[...] = jnp.zeros_like(ybuf)
    cp.wait()
    src_copy(0).wait()
    lax.fori_loop(0, br, lambda r, c: (gather_row(0, 0, r), c)[1], 0)

    def block(b, wslot):
        g = b // SRC_GROUP
        phase = lax.rem(b, SRC_GROUP)

        more = (g + 1) * SRC_GROUP < n_used

        @pl.when(jnp.logical_and(phase == 1, more))
        def _():
            src_copy(g + 1).start()

        @pl.when(jnp.logical_and(phase == SRC_GROUP - 1, more))
        def _():
            src_copy(g + 1).wait()

        e = bexp_ref[b]
        prev = bexp_ref[jnp.maximum(b - 1, 0)]

        @pl.when(jnp.logical_or(b == 0, e != prev))
        def _():
            for wcp in weight_copies(e, wslot):
                wcp.wait()
            w13_s[:, 0:EXPERT_FF] = w1_f[wslot].astype(BF16)
            w13_s[:, EXPERT_FF:2 * EXPERT_FF] = w3_f[wslot].astype(BF16)
            w2_s[...] = w2_f[wslot].astype(BF16)
            nxt = lax.while_loop(
                lambda j: jnp.logical_and(j < n_used, bexp_ref[jnp.minimum(j, n_used - 1)] == e),
                lambda j: j + 1, b + 1)

            @pl.when(nxt < n_used)
            def _():
                for wcp in weight_copies(bexp_ref[nxt], 1 - wslot):
                    wcp.start()

        switch = jnp.logical_and(b + 1 < n_used, bexp_ref[b + 1] != e)

        valid = bval_ref[b]
        sbase = src_base(b)
        slot = lax.rem(b, 2)

        pb = jnp.maximum(b - 1, 0)
        p_valid = jnp.where(b > 0, bval_ref[pb], 0)
        p_base = src_base(pb)
        n_base = src_base(b + 1)
        for r in range(br):
            gather_row(1 - slot, n_base, r)
            scatter_row(1 - slot, p_base, p_valid, r, priority=r % 2)

        x = _unpack_rows(xbuf, br, slot * (br * PACK_CHUNKS))
        rows = lax.broadcasted_iota(I32, x.shape, 0)
        x = jnp.where(rows < valid, x, jnp.zeros_like(x))
        ab = jnp.dot(x, w13_s[...], preferred_element_type=F32)
        hid = (_silu(ab[:, 0:EXPERT_FF]) * ab[:, EXPERT_FF:2 * EXPERT_FF]).astype(BF16)
        y = jnp.dot(hid, w2_s[...], preferred_element_type=F32)

        @pl.when(b >= 1)
        def _():
            out_wait(slot)

        _pack_rows(ybuf, y, slot * (br * PACK_CHUNKS))
        return jnp.where(switch, 1 - wslot, wslot)

    lax.fori_loop(0, n_used, block, 0)

    last = n_used - 1
    l_slot = lax.rem(last, 2)
    l_base = src_base(last)
    l_valid = bval_ref[last]
    lax.fori_loop(0, br, lambda r, c: (scatter_row(l_slot, l_base, l_valid, r), c)[1], 0)
    out_wait(1 - l_slot)
    out_wait(l_slot)


def _moe_experts(bexp, bval, nused, u2p, row_src, w1, w3, w2, n_tokens):
    br = MOE_ROWS
    any_spec = pl.BlockSpec(memory_space=pl.ANY)
    grid_spec = pltpu.PrefetchScalarGridSpec(
        num_scalar_prefetch=3, grid=(1,),
        in_specs=[any_spec] * 5,
        out_specs=any_spec,
        scratch_shapes=[pltpu.VMEM(u2p.shape, U32),
                        pltpu.VMEM((2, D_MODEL, EXPERT_FF), F32),
                        pltpu.VMEM((2, D_MODEL, EXPERT_FF), F32),
                        pltpu.VMEM((2, EXPERT_FF, D_MODEL), F32),
                        pltpu.VMEM((D_MODEL, 2 * EXPERT_FF), BF16),
                        pltpu.VMEM((EXPERT_FF, D_MODEL), BF16),
                        pltpu.VMEM((2 * PACK_CHUNKS * br, LANES), U32),
                        pltpu.VMEM((2 * br * PACK_CHUNKS, LANES), U32),
                        pltpu.SMEM((2 * SRC_GROUP * br,), I32),
                        pltpu.SemaphoreType.DMA, pltpu.SemaphoreType.DMA,
                        pltpu.SemaphoreType.DMA((2,)), pltpu.SemaphoreType.DMA((2,))])
    n_out_tiles = n_tokens * TOP_K + 2 * br
    return pl.pallas_call(
        functools.partial(_moe_kernel, n_tokens=n_tokens), grid_spec=grid_spec,
        out_shape=jax.ShapeDtypeStruct((n_out_tiles * PACK_CHUNKS, LANES), U32),
        compiler_params=pltpu.CompilerParams(dimension_semantics=("arbitrary",),
                                             vmem_limit_bytes=MOE_VMEM_LIMIT),
        name="moe_experts",
    )(bexp, bval, nused, u2p, row_src, w1, w3, w2)


def _combine_kernel(w_hbm, yt_ref, base_ref, mod_ref, g_ref, b_ref, yc_ref, yl_ref,
                    w_smem, acc_lo, acc_hi, sem_w, *, n_ctx_tiles):
    i = pl.program_id(0)
    n_steps = pl.num_programs(0)
    n_tok = acc_lo.shape[0] // PACK_CHUNKS
    n_idx = n_tok * TOP_K

    def w_copy(tile):
        return pltpu.make_async_copy(w_hbm.at[pl.ds(tile * n_idx, n_idx)],
                                     w_smem.at[pl.ds(lax.rem(tile, 2) * n_idx, n_idx)], sem_w)

    @pl.when(i == 0)
    def _():
        w_copy(i).start()

    w_copy(i).wait()

    @pl.when(i + 1 < n_steps)
    def _():
        w_copy(i + 1).start()

    wbase = lax.rem(i, 2) * n_idx

    per_tile = SUBLANES // PACK_CHUNKS
    first = lax.broadcasted_iota(I32, (SUBLANES, LANES), 0) < PACK_CHUNKS

    def reduce_token(t):
        lo = hi = None
        for m in range(TOP_K // per_tile):
            j = t * TOP_K + m * per_tile
            words = yt_ref[pl.ds(pl.multiple_of(j * PACK_CHUNKS, SUBLANES), SUBLANES), :]
            wgt = jnp.where(first, w_smem[wbase + j], w_smem[wbase + j + 1])
            t_lo = wgt * pltpu.bitcast(words << 16, F32)
            t_hi = wgt * pltpu.bitcast(words & jnp.uint32(HIGH_HALF), F32)
            lo = t_lo if lo is None else lo + t_lo
            hi = t_hi if hi is None else hi + t_hi
        row = pl.multiple_of(t * PACK_CHUNKS, PACK_CHUNKS)
        acc_lo[pl.ds(row, PACK_CHUNKS), :] = lo[0:PACK_CHUNKS] + lo[PACK_CHUNKS:SUBLANES]
        acc_hi[pl.ds(row, PACK_CHUNKS), :] = hi[0:PACK_CHUNKS] + hi[PACK_CHUNKS:SUBLANES]

    def reduce_group(i, carry):
        for u in range(COMB_UNROLL):
            reduce_token(i * COMB_UNROLL + u)
        return carry

    lax.fori_loop(0, n_tok // COMB_UNROLL, reduce_group, 0)
    parts = []
    for s in range(PACK_CHUNKS):
        parts.append(acc_lo[pl.ds(s, n_tok, stride=PACK_CHUNKS), :])
        parts.append(acc_hi[pl.ds(s, n_tok, stride=PACK_CHUNKS), :])
    moe = jnp.concatenate(parts, axis=1)
    gate2 = mod_ref[:, 5 * D_MODEL:6 * D_MODEL]
    y = _layer_norm(base_ref[...] + gate2 * moe, g_ref[...], b_ref[...])

    @pl.when(i < n_ctx_tiles)
    def _():
        yc_ref[...] = y

    @pl.when(i >= n_ctx_tiles)
    def _():
        yl_ref[...] = y


def _combine(w_flat, yt, base, mod3, l2g, l2b, n_ctx, seq_tokens):
    n = base.shape[0]
    tc = COMB_TILE
    n_ctx_tiles = n_ctx // tc
    n_seq_tiles = seq_tokens // tc

    def mod_idx(i):
        return (jnp.where(i < n_ctx_tiles, 0, 1 + (i - n_ctx_tiles) // n_seq_tiles), 0, 0)

    full = lambda a: pl.BlockSpec(a.shape, lambda i: (0,) * a.ndim)
    return pl.pallas_call(
        functools.partial(_combine_kernel, n_ctx_tiles=n_ctx_tiles),
        grid=(n // tc,),
        in_specs=[pl.BlockSpec(memory_space=pl.ANY),
                  pl.BlockSpec((tc * TOP_K * PACK_CHUNKS, LANES), lambda i: (i, 0)),
                  pl.BlockSpec((tc, D_MODEL), lambda i: (i, 0)),
                  pl.BlockSpec((None, 1, mod3.shape[2]), mod_idx), full(l2g), full(l2b)],
        out_specs=(pl.BlockSpec((tc, D_MODEL), lambda i: (jnp.minimum(i, n_ctx_tiles - 1), 0)),
                   pl.BlockSpec((tc, D_MODEL), lambda i: (jnp.maximum(i - n_ctx_tiles, 0), 0))),
        out_shape=(jax.ShapeDtypeStruct((n_ctx, D_MODEL), F32),
                   jax.ShapeDtypeStruct((n - n_ctx, D_MODEL), F32)),
        scratch_shapes=[pltpu.SMEM((2 * tc * TOP_K,), F32),
                        pltpu.VMEM((tc * PACK_CHUNKS, LANES), F32),
                        pltpu.VMEM((tc * PACK_CHUNKS, LANES), F32),
                        pltpu.SemaphoreType.DMA],
        compiler_params=_cparams(1), name="combine",
    )(w_flat, yt, base, mod3, l2g, l2b)


def _rope_tables(n_tok, tile):
    f32 = np.float32
    rows = n_tok // GRID_W
    row_idx = np.repeat(np.arange(rows, dtype=f32), GRID_W)
    col_idx = np.tile(np.arange(GRID_W, dtype=f32), rows)
    inv_freq = (1.0 / (ROPE_THETA ** (np.arange(0, ROPE_AXIS_DIM, 2, dtype=f32) / ROPE_AXIS_DIM))).astype(f32)
    ang_r = row_idx[:, None] * inv_freq[None, :]
    ang_c = col_idx[:, None] * inv_freq[None, :]
    ang = np.concatenate([ang_r, ang_r, ang_c, ang_c], axis=-1)
    cos, sin = np.cos(ang), np.sin(ang)
    quarter = (np.arange(HEAD_DIM) // (ROPE_AXIS_DIM // 2)) % 2
    sin_a = np.where(quarter == 0, -sin, 0.0)
    sin_b = np.where(quarter == 1, sin, 0.0)
    rep = LANES // HEAD_DIM
    ident = lambda v: np.full((tile, LANES), v, f32)
    cos_t = np.concatenate([np.tile(cos, (1, rep)), ident(1.0)], axis=0)
    sa_t = np.concatenate([np.tile(sin_a, (1, rep)), ident(0.0)], axis=0)
    sb_t = np.concatenate([np.tile(sin_b, (1, rep)), ident(0.0)], axis=0)
    ident_tr = lambda v: np.full((HEAD_DIM, tile), v, f32)
    cos_tr = np.concatenate([cos.T, ident_tr(1.0)], axis=1)
    sin_tr = np.concatenate([sin.T, ident_tr(0.0)], axis=1)
    return tuple(jnp.asarray(t, F32) for t in (cos_t, sa_t, sb_t, cos_tr, sin_tr))


def _dup_heads(a):
    parts = []
    for h in range(KV_HEADS):
        blk = a[..., h * HEAD_DIM:(h + 1) * HEAD_DIM]
        parts += [blk] * (LANES // HEAD_DIM)
    return jnp.concatenate(parts, axis=-1)


def kernel(x_prompt, x_sample, cache_k, cache_v, state_gla_fwd, state_gla_bwd, c, c_ctx, w_ada, b_ada, w_in, q_norm, k_norm, gla_wa_fwd, gla_ba_fwd, gla_wa_bwd, gla_ba_bwd, gla_norm, w_out, ln1_g, ln1_b, ln2_g, ln2_b, w_router, router_bias, exp_w1, exp_w3, exp_w2, sh_w1, sh_w3, sh_w2):
    n_ctx_b, ctx_seq, _ = x_prompt.shape
    n_lat_b, lat_seq, _ = x_sample.shape
    n_ctx = n_ctx_b * ctx_seq
    n_lat = n_lat_b * lat_seq
    n = n_ctx + n_lat
    l = 0

    x_c = x_prompt.reshape(n_ctx, D_MODEL)
    x_l = x_sample.reshape(n_lat, D_MODEL)

    c_rows = jnp.zeros((SUBLANES, D_MODEL), F32).at[0].set(c_ctx).at[1:1 + n_lat_b].set(c)
    mod = _modulation(c_rows, w_ada[l], b_ada[l][None, :])
    mod3 = mod.reshape(SUBLANES, 1, 6 * D_MODEL)

    wi = w_in[l]
    o_q, o_k, o_v, o_gq, o_gk, o_gv, o_gg, o_rf, o_rb, o_end = np.cumsum(
        [0, ATT_WIDTH, KV_HEADS * HEAD_DIM, KV_HEADS * HEAD_DIM, GLA_KW, GLA_KW, GLA_WIDTH, GLA_WIDTH,
         GLA_GATE_RANK, GLA_GATE_RANK])
    w_tok = jnp.concatenate([
        _dup_heads(wi[:, o_k:o_v]), wi[:, o_v:o_gq], wi[:, o_gq:o_gk],
        wi[:, o_gv:o_gg], wi[:, o_gg:o_rf], wi[:, o_rf:o_end],
        jnp.zeros((D_MODEL, LANES - 2 * GLA_GATE_RANK), F32)], axis=1).astype(BF16)
    w_tr = jnp.concatenate([wi[:, o_q:o_k], wi[:, o_v:o_gq], wi[:, o_gk:o_gv], wi[:, o_rf:o_end]],
                           axis=1).T.astype(BF16)
    rep = LANES // HEAD_DIM
    qn = q_norm[l][:, None]
    kn = jnp.tile(k_norm[l], rep)[None, :]
    seg = jnp.asarray(np.kron(np.eye(rep), np.ones((HEAD_DIM, HEAD_DIM))), BF16)
    wa = jnp.zeros((LANES, 2 * GLA_KW), F32)
    wa = wa.at[0:GLA_GATE_RANK, 0:GLA_KW].set(gla_wa_fwd[l])
    wa = wa.at[GLA_GATE_RANK:2 * GLA_GATE_RANK, GLA_KW:].set(gla_wa_bwd[l])
    ba = jnp.concatenate([gla_ba_fwd[l], gla_ba_bwd[l]])[None, :]
    wat = wa[0:2 * GLA_GATE_RANK, :].T
    bat = ba.T
    cos_t, sa_t, sb_t, cos_tr, sin_tr = _rope_tables(lat_seq, TOK_TILE)

    (qt, k_dup, vt, k32, v32, gq, gv, gg, la, gkt, lat) = _in_projection(
        x_c, x_l, mod3, w_tok, w_tr, qn, kn, cos_t, sa_t, sb_t, cos_tr, sin_tr, seg, wa, ba, wat, bat,
        lat_seq // TOK_TILE)

    ck = _dup_heads(cache_k[:, l].reshape(n_lat_b, -1, KV_HEADS * HEAD_DIM)).astype(BF16)
    cvt = cache_v[:, l].reshape(n_lat_b, -1, KV_HEADS * HEAD_DIM).transpose(0, 2, 1).astype(BF16)
    att_c = _attention(qt, k_dup, vt, None, 0, n_ctx_b, ctx_seq)
    att_l = _attention(qt, k_dup, vt, (ck, cvt), n_ctx, n_lat_b, lat_seq)

    gconst, levels_of = _gla_constants()
    to_dev = lambda t: (jnp.asarray(t[0], BF16), jnp.asarray(t[1], BF16), jnp.asarray(t[2], F32))
    bd = jnp.asarray(np.kron(np.eye(GLA_HEADS), np.ones((GLA_DK, GLA_DV))), BF16)
    vbd = jnp.asarray(np.kron(np.eye(GLA_HEADS), np.ones((GLA_CHUNK, GLA_DV))), BF16)
    consts = ((to_dev(gconst["f"]), to_dev(gconst["b"])), levels_of, bd, vbd)
    s_zero = jnp.zeros((n_ctx_b, GLA_HEADS, GLA_DK, GLA_DV), F32)
    of_c, ob_c, sf_new, sb_new = _gla(gq, la, gkt, lat, gv, s_zero, s_zero, consts, 0, n_ctx_b, ctx_seq)
    of_l, ob_l, _, _ = _gla(gq, la, gkt, lat, gv, state_gla_fwd[:, l], state_gla_bwd[:, l], consts,
                            n_ctx, n_lat_b, lat_seq)

    sw13 = jnp.concatenate([sh_w1[l], sh_w3[l]], axis=1).astype(BF16)
    base, u2_rows, logits_t = _out_projection(
        att_c, att_l, of_c, of_l, ob_c, ob_l, gg, x_c, x_l, mod3, w_out[l].astype(BF16),
        gla_norm[l][None, :], ln1_g[l][None, :], ln1_b[l][None, :], w_router[l].T.astype(BF16), sw13,
        sh_w2[l].astype(BF16), lat_seq // TOK_TILE)

    upper = jnp.asarray(np.triu(np.ones((TOK_TILE, TOK_TILE)), 1), BF16)
    idx_t, w_t, pos_t, counts = _route(logits_t, router_bias[l][:, None], upper)
    n_blocks = n * TOP_K // MOE_ROWS + N_EXPERTS
    n_blocks_pad = -(-n_blocks // LANES) * LANES
    lower = jnp.asarray(np.tril(np.ones((N_EXPERTS, N_EXPERTS)), -1), F32)
    dest_t, bexp, bval, nused = _destinations(counts, lower, idx_t, pos_t, n_blocks_pad)
    w_flat = w_t.T.reshape(-1)

    ids = np.arange(n, dtype=np.int32)[None, :] * TOP_K + np.arange(TOP_K, dtype=np.int32)[:, None]
    row_src = _invert_rows(dest_t, ids, n_blocks * MOE_ROWS)
    yt = _moe_experts(bexp.reshape(-1), bval.reshape(-1), nused.reshape(-1)[0:1], u2_rows, row_src,
                      exp_w1[l], exp_w3[l], exp_w2[l], n)
    y_c, y_l = _combine(w_flat, yt, base, mod3, ln2_g[l][None, :], ln2_b[l][None, :], n_ctx, lat_seq)

    y_prompt = y_c.reshape(n_ctx_b, ctx_seq, D_MODEL)
    y_sample = y_l.reshape(n_lat_b, lat_seq, D_MODEL)
    new_cache_k = k32.reshape(n_ctx_b, 1, ctx_seq, KV_HEADS, HEAD_DIM)
    new_cache_v = v32.reshape(n_ctx_b, 1, ctx_seq, KV_HEADS, HEAD_DIM)
    return (y_prompt, y_sample, new_cache_k, new_cache_v, sf_new[:, None], sb_new[:, None])
```

```python
import functools

import numpy as np
import jax
import jax.numpy as jnp
from jax import lax
from jax.experimental import pallas as pl
from jax.experimental.pallas import tpu as pltpu
from jax.experimental.pallas import tpu_sc as plsc

F32 = jnp.float32
BF16 = jnp.bfloat16
I32 = jnp.int32

D_MODEL = 1024
GRID_W = 64
HEAD_DIM = 64
N_HEADS = 8
KV_HEADS = 2
ATT_WIDTH = N_HEADS * HEAD_DIM
ATT_SCALE = HEAD_DIM ** -0.5
LOG2_E = 1.4426950408889634
ROPE_AXIS_DIM = HEAD_DIM // 2
ROPE_THETA = 10000.0
GLA_HEADS = 4
GLA_DK = 64
GLA_DV = 128
GLA_WIDTH = GLA_HEADS * GLA_DV
GLA_KW = GLA_HEADS * GLA_DK
GLA_GATE_RANK = 16
GLA_TAU = 16.0
N_EXPERTS = 256
TOP_K = 8
EXPERT_FF = 256
SHARED_FF = 256
ROUTED_SCALE = 2.5
DEPTH = 1
ALPHA = (2.0 * DEPTH) ** 0.25
EPS = 1e-6

LANES = 128
SUBLANES = 8
VMEM_BYTES = 64 * 1024 * 1024
VMEM_LIMIT = VMEM_BYTES - 8 * 1024 * 1024
PACK_CHUNKS = D_MODEL // (2 * LANES)
PACK_CHUNKS_LOG2 = PACK_CHUNKS.bit_length() - 1
HIGH_HALF = 0xFFFF0000
U32 = jnp.uint32

TOK_TILE = 512
ATT_TQ = 256
GLA_CHUNK = 128
GLA_LEVELS = ((32, 128), (8, 32), (2, 8), (1, 2))
MOE_ROWS = 256
MOE_VMEM_LIMIT = VMEM_BYTES - 2 * 1024 * 1024
TOP_K_LOG2 = TOP_K.bit_length() - 1
SRC_GROUP = 4
COMB_TILE = 256
COMB_UNROLL = 8
HIGHEST = lax.Precision.HIGHEST


def _cparams(n_axes):
    return pltpu.CompilerParams(dimension_semantics=("arbitrary",) * n_axes,
                                vmem_limit_bytes=VMEM_LIMIT)


def _silu(x):
    return x * jax.nn.sigmoid(x)


def _log_sigmoid(x):
    return jnp.minimum(x, 0.0) - jnp.log(1.0 + jnp.exp(-jnp.abs(x)))


def _dot_split(a, b):
    a_hi = a.astype(BF16)
    b_hi = b.astype(BF16)
    a_lo = (a - a_hi.astype(F32)).astype(BF16)
    b_lo = (b - b_hi.astype(F32)).astype(BF16)
    dot = functools.partial(jnp.dot, preferred_element_type=F32)
    return dot(a_hi, b_hi) + dot(a_lo, b_hi) + dot(a_hi, b_lo)


def _layer_norm(z, g, b):
    mu = jnp.mean(z, axis=-1, keepdims=True)
    zc = z - mu
    var = jnp.mean(zc * zc, axis=-1, keepdims=True)
    return zc * lax.rsqrt(var + EPS) * g + b


def _mod_kernel(c_ref, w_ref, b_ref, o_ref):
    s = _silu(c_ref[...]).astype(BF16)
    o_ref[...] = jnp.dot(s, w_ref[...].astype(BF16), preferred_element_type=F32) + b_ref[...]


def _modulation(c_rows, w_ada, b_ada):
    n_cols = w_ada.shape[1]
    tn = 512
    return pl.pallas_call(
        _mod_kernel,
        grid=(n_cols // tn,),
        in_specs=[pl.BlockSpec((SUBLANES, D_MODEL), lambda j: (0, 0)),
                  pl.BlockSpec((D_MODEL, tn), lambda j: (0, j)),
                  pl.BlockSpec((1, tn), lambda j: (0, j))],
        out_specs=pl.BlockSpec((SUBLANES, tn), lambda j: (0, j)),
        out_shape=jax.ShapeDtypeStruct((SUBLANES, n_cols), F32),
        compiler_params=_cparams(1),
        name="modulation",
    )(c_rows, w_ada, b_ada)


_C_K = 0
_C_V = _C_K + 2 * LANES
_C_GQ = _C_V + KV_HEADS * HEAD_DIM
_C_GV = _C_GQ + GLA_KW
_C_GG = _C_GV + GLA_WIDTH
_C_RA = _C_GG + GLA_WIDTH
_C_END = _C_RA + LANES
_R_Q = 0
_R_V = _R_Q + ATT_WIDTH
_R_GK = _R_V + KV_HEADS * HEAD_DIM
_R_RA = _R_GK + GLA_KW
_R_END = _R_RA + 2 * GLA_GATE_RANK


def _inproj_kernel(xc_ref, xl_ref, mod_ref, w_ref, wt_ref, qn_ref, kn_ref, cos_ref, sa_ref, sb_ref,
                   cost_ref, sint_ref, seg_ref, wa_ref, ba_ref, wat_ref, bat_ref,
                   qt_ref, k_ref, vt_ref, k32_ref, v32_ref, gq_ref, gv_ref, gg_ref,
                   la_ref, gkt_ref, lat_ref, *, n_ctx_tiles):
    i = pl.program_id(0)
    m = mod_ref[...]
    shift1 = m[:, 0:D_MODEL]
    scale1 = m[:, D_MODEL:2 * D_MODEL]
    x = jnp.where(i < n_ctx_tiles, xc_ref[...], xl_ref[...])
    u = (x * (1.0 + scale1) + shift1).astype(BF16)

    cos = cos_ref[...]
    sin_a = sa_ref[...]
    sin_b = sb_ref[...]
    seg = seg_ref[...]
    lane = lax.broadcasted_iota(I32, (u.shape[0], LANES), 1)
    low = lane < HEAD_DIM

    def proj(c0, c1):
        return jnp.dot(u, w_ref[:, c0:c1], preferred_element_type=F32)

    def head_norm(blk, gain):
        ss = jnp.dot((blk * blk).astype(BF16), seg, preferred_element_type=F32) * (1.0 / HEAD_DIM)
        return blk * lax.rsqrt(ss + EPS) * gain

    def rope(blk):
        return (blk * cos + pltpu.roll(blk, LANES - ROPE_AXIS_DIM // 2, 1) * sin_a
                + pltpu.roll(blk, ROPE_AXIS_DIM // 2, 1) * sin_b)

    pk = proj(_C_K, _C_V)
    kn = [head_norm(pk[:, j * LANES:(j + 1) * LANES], kn_ref[...]) for j in range(KV_HEADS)]
    for j in range(KV_HEADS):
        k_ref[:, j * LANES:(j + 1) * LANES] = rope(kn[j]).astype(BF16)

    @pl.when(i < n_ctx_tiles)
    def _():
        k32_ref[...] = jnp.where(low, kn[0], kn[1])
        v32_ref[...] = proj(_C_V, _C_GQ)

    gq_ref[...] = proj(_C_GQ, _C_GV) * (GLA_DK ** -0.5)
    gv_ref[...] = proj(_C_GV, _C_GG).astype(BF16)
    gg_ref[...] = proj(_C_GG, _C_RA).astype(BF16)

    ra = proj(_C_RA, _C_END)
    pre = _dot_split(ra, wa_ref[...]) + ba_ref[...]
    la_ref[...] = _log_sigmoid(pre) * (1.0 / GLA_TAU)

    pt = lax.dot_general(wt_ref[...], u, (((1,), (1,)), ((), ())), preferred_element_type=F32)
    cos_t = cost_ref[...]
    sin_t = sint_ref[...]
    quarter = ROPE_AXIS_DIM // 2
    for h in range(N_HEADS):
        blk = pt[_R_Q + h * HEAD_DIM:_R_Q + (h + 1) * HEAD_DIM, :]
        ms = jnp.mean(blk * blk, axis=0, keepdims=True)
        qn = blk * lax.rsqrt(ms + EPS) * qn_ref[...]
        rot = jnp.concatenate([-qn[quarter:2 * quarter], qn[0:quarter],
                               -qn[3 * quarter:4 * quarter], qn[2 * quarter:3 * quarter]], axis=0)
        qt_ref[h * HEAD_DIM:(h + 1) * HEAD_DIM, :] = (
            (qn * cos_t + rot * sin_t) * (ATT_SCALE * LOG2_E)).astype(BF16)
    vt_ref[...] = pt[_R_V:_R_GK, :].astype(BF16)
    gkt_ref[...] = pt[_R_GK:_R_RA, :]
    rat = pt[_R_RA:_R_END, :]
    pre_t = _dot_split(wat_ref[...], rat) + bat_ref[...]
    lat_ref[...] = _log_sigmoid(pre_t) * (1.0 / GLA_TAU)


def _in_projection(x_c, x_l, mod3, w_tok, w_tr, qn, kn, cos_t, sa_t, sb_t, cos_tr, sin_tr, seg, wa, ba,
                   wat, bat, n_seq_tiles):
    n_ctx = x_c.shape[0]
    n = n_ctx + x_l.shape[0]
    tb = TOK_TILE
    n_ctx_tiles = n_ctx // tb
    n_tiles = n // tb
    n_rope_blocks = cos_t.shape[0] // tb - 1

    def mod_idx(i):
        return (jnp.where(i < n_ctx_tiles, 0, 1 + (i - n_ctx_tiles) // n_seq_tiles), 0, 0)

    def rope_blk(i):
        return jnp.where(i < n_ctx_tiles, n_rope_blocks, (i - n_ctx_tiles) % n_seq_tiles)

    def rope_idx(i):
        return (rope_blk(i), 0)

    def ctx_idx(i):
        return (jnp.minimum(i, n_ctx_tiles - 1), 0)

    tok = lambda w: pl.BlockSpec((tb, w), lambda i: (i, 0))
    full = lambda a: pl.BlockSpec(a.shape, lambda i: (0,) * a.ndim)
    tr = lambda r: pl.BlockSpec((r, tb), lambda i: (0, i))
    rope_tr = pl.BlockSpec((HEAD_DIM, tb), lambda i: (0, rope_blk(i)))
    out_shapes = (
        jax.ShapeDtypeStruct((ATT_WIDTH, n), BF16),
        jax.ShapeDtypeStruct((n, 2 * LANES), BF16),
        jax.ShapeDtypeStruct((KV_HEADS * HEAD_DIM, n), BF16),
        jax.ShapeDtypeStruct((n_ctx, LANES), F32),
        jax.ShapeDtypeStruct((n_ctx, LANES), F32),
        jax.ShapeDtypeStruct((n, GLA_KW), F32),
        jax.ShapeDtypeStruct((n, GLA_WIDTH), BF16),
        jax.ShapeDtypeStruct((n, GLA_WIDTH), BF16),
        jax.ShapeDtypeStruct((n, 2 * GLA_KW), F32),
        jax.ShapeDtypeStruct((GLA_KW, n), F32),
        jax.ShapeDtypeStruct((2 * GLA_KW, n), F32),
    )
    out_specs = (tr(ATT_WIDTH), tok(2 * LANES), tr(KV_HEADS * HEAD_DIM),
                 pl.BlockSpec((tb, LANES), ctx_idx), pl.BlockSpec((tb, LANES), ctx_idx),
                 tok(GLA_KW), tok(GLA_WIDTH), tok(GLA_WIDTH), tok(2 * GLA_KW),
                 tr(GLA_KW), tr(2 * GLA_KW))
    in_specs = [pl.BlockSpec((tb, D_MODEL), ctx_idx),
                pl.BlockSpec((tb, D_MODEL), lambda i: (jnp.maximum(i - n_ctx_tiles, 0), 0)),
                pl.BlockSpec((None, 1, mod3.shape[2]), mod_idx),
                full(w_tok), full(w_tr), full(qn), full(kn),
                pl.BlockSpec((tb, LANES), rope_idx), pl.BlockSpec((tb, LANES), rope_idx),
                pl.BlockSpec((tb, LANES), rope_idx), rope_tr, rope_tr,
                full(seg), full(wa), full(ba), full(wat), full(bat)]
    return pl.pallas_call(
        functools.partial(_inproj_kernel, n_ctx_tiles=n_ctx_tiles),
        grid=(n_tiles,), in_specs=in_specs, out_specs=out_specs, out_shape=out_shapes,
        compiler_params=_cparams(1), name="in_projection",
    )(x_c, x_l, mod3, w_tok, w_tr, qn, kn, cos_t, sa_t, sb_t, cos_tr, sin_tr, seg, wa, ba, wat, bat)


def _attention_kernel(*refs, n_kv_parts):
    qt_ref = refs[0]
    k_refs = refs[1:1 + n_kv_parts]
    vt_refs = refs[1 + n_kv_parts:1 + 2 * n_kv_parts]
    o_ref = refs[1 + 2 * n_kv_parts]
    tq = qt_ref.shape[1]
    group = N_HEADS // KV_HEADS
    for kv in range(KV_HEADS):
        heads = range(kv * group, (kv + 1) * group)
        q_grp = jnp.concatenate([qt_ref[h * HEAD_DIM:(h + 1) * HEAD_DIM, :] for h in heads], axis=1)
        rhs = jnp.concatenate([q_grp, jnp.zeros_like(q_grp)], axis=0)
        s = [jnp.dot(k[:, kv * LANES:(kv + 1) * LANES], rhs, preferred_element_type=F32)
             for k in k_refs]
        mx = functools.reduce(jnp.maximum, [jnp.max(x, axis=0, keepdims=True) for x in s])
        pr = [jnp.exp2(x - mx) for x in s]
        den = functools.reduce(jnp.add, [jnp.sum(x, axis=0, keepdims=True) for x in pr])
        acc = functools.reduce(jnp.add, [
            jnp.dot(vt[kv * HEAD_DIM:(kv + 1) * HEAD_DIM, :], x.astype(BF16),
                    preferred_element_type=F32) for x, vt in zip(pr, vt_refs)])
        out = (acc / den).astype(BF16)
        for j, h in enumerate(heads):
            o_ref[h * HEAD_DIM:(h + 1) * HEAD_DIM, :] = out[:, j * tq:(j + 1) * tq]


def _attention(qt, k, vt, extra_kv, row0, n_batch, seq):
    tq = ATT_TQ
    n_q = seq // tq
    q_blk0 = row0 // tq
    kv_blk0 = row0 // seq
    in_specs = [pl.BlockSpec((ATT_WIDTH, tq), lambda b, i: (0, q_blk0 + b * n_q + i))]
    k_spec = pl.BlockSpec((seq, 2 * LANES), lambda b, i: (kv_blk0 + b, 0))
    vt_spec = pl.BlockSpec((KV_HEADS * HEAD_DIM, seq), lambda b, i: (0, kv_blk0 + b))
    args_k, args_v, specs_k, specs_v = [k], [vt], [k_spec], [vt_spec]
    if extra_kv is not None:
        ck, cvt = extra_kv
        args_k.append(ck)
        args_v.append(cvt)
        specs_k.append(pl.BlockSpec((None, ck.shape[1], 2 * LANES), lambda b, i: (b, 0, 0)))
        specs_v.append(pl.BlockSpec((None, KV_HEADS * HEAD_DIM, cvt.shape[2]), lambda b, i: (b, 0, 0)))
    return pl.pallas_call(
        functools.partial(_attention_kernel, n_kv_parts=len(args_k)),
        grid=(n_batch, n_q),
        in_specs=in_specs + specs_k + specs_v,
        out_specs=pl.BlockSpec((ATT_WIDTH, tq), lambda b, i: (0, b * n_q + i)),
        out_shape=jax.ShapeDtypeStruct((ATT_WIDTH, n_batch * seq), BF16),
        compiler_params=_cparams(2), name="attention",
    )(qt, *args_k, *args_v)


def _gla_constants():
    c = GLA_CHUNK
    idx = np.arange(c)
    q_mats, k_mats, masks, levels_of = [], [], [], []
    for li, (s, p) in enumerate(GLA_LEVELS):
        start = (idx // s) * s
        end = start + s - 1
        k_mats.append(((idx[None, :] > idx[:, None]) & (idx[None, :] <= end[:, None])))
        for d in range(p // s - 1):
            lo = np.maximum(start - d * s, 0)
            q_mats.append((idx[None, :] >= lo[:, None]) & (idx[None, :] <= idx[:, None]))
            masks.append((idx[:, None] // p == idx[None, :] // p)
                         & (idx[:, None] // s - idx[None, :] // s - 1 == d))
            levels_of.append(li)
    masks.append(np.eye(c, dtype=bool))
    levels_of.append(len(GLA_LEVELS) - 1)
    q_mats.append(idx[None, :] <= idx[:, None])
    k_mats = k_mats[:-1]
    k_mats.append(idx[None, :] > idx[:, None])
    k_mats.append(np.ones((c, c), bool))
    out = {}
    for name, flip in (("f", False), ("b", True)):
        f = (lambda a: a[::-1, ::-1]) if flip else (lambda a: a)
        lq = np.concatenate([f(a) for a in q_mats], axis=0).astype(np.float32)
        lkt = np.concatenate([f(a).T for a in k_mats], axis=1).astype(np.float32)
        mk = np.stack([np.tile(f(a), (1, GLA_HEADS)) for a in masks]).astype(np.float32)
        out[name] = (np.concatenate([lq, lq], axis=1), np.concatenate([lkt, lkt], axis=0), mk)
    return out, tuple(levels_of)


def _gla_direction(q, g, gkt, gt, v, lq2, lkt2, masks_ref, bd, vbd, s_ref, levels_of):
    c = GLA_CHUNK
    n_var = len(levels_of)
    n_lev = len(GLA_LEVELS)
    g_hi = g.astype(BF16)
    g_lo = (g - g_hi.astype(F32)).astype(BF16)
    fq = jnp.dot(lq2, jnp.concatenate([g_hi, g_lo], axis=0), preferred_element_type=F32)
    gt_hi = gt.astype(BF16)
    gt_lo = (gt - gt_hi.astype(F32)).astype(BF16)
    fk = jnp.dot(jnp.concatenate([gt_hi, gt_lo], axis=1), lkt2, preferred_element_type=F32)

    def key_factor(f):
        return gkt * jnp.exp(fk[:, f * c:(f + 1) * c])

    q_var = [(q * jnp.exp(fq[vi * c:(vi + 1) * c, :])).astype(BF16) for vi in range(n_var - 1)]
    q_var.append(q.astype(BF16))
    a = jnp.zeros((c, GLA_HEADS * c), F32)
    for li in range(n_lev):
        kt = (key_factor(li) if li < n_lev - 1 else gkt).astype(BF16)
        xt = jnp.concatenate([kt] * GLA_HEADS, axis=1) * bd
        vis = [vi for vi in range(n_var) if levels_of[vi] == li]
        res = jnp.dot(jnp.concatenate([q_var[vi] for vi in vis], axis=0), xt,
                      preferred_element_type=F32)
        for r, vi in enumerate(vis):
            a = a + masks_ref[vi] * res[r * c:(r + 1) * c, :]
    q_in = (q * jnp.exp(fq[(n_var - 1) * c:n_var * c, :])).astype(BF16)
    state = s_ref[...]
    v_bd = jnp.concatenate([v] * GLA_HEADS, axis=0) * vbd
    o = (jnp.dot(q_in, state.astype(BF16), preferred_element_type=F32)
         + jnp.dot(a.astype(BF16), v_bd, preferred_element_type=F32))
    k_out = key_factor(n_lev - 1).astype(BF16)
    e_tot = jnp.exp(fk[:, n_lev * c:(n_lev + 1) * c])
    upd = jnp.dot(k_out, v, preferred_element_type=F32)
    s_ref[...] = (state * jnp.concatenate([e_tot] * (GLA_WIDTH // c), axis=1)
                  + upd * bd.astype(F32))
    return o


def _gla_kernel(gq_f, la_f, gkt_f, lat_f, gv_f, gq_b, la_b, gkt_b, lat_b, gv_b,
                s0f_ref, s0b_ref, lq2f, lkt2f, mkf, lq2b, lkt2b, mkb, bd_ref, vbd_ref,
                of_ref, ob_ref, sf_ref, sb_ref, st_f, st_b, *, levels_of):
    n = pl.program_id(1)

    @pl.when(n == 0)
    def _():
        st_f[...] = jnp.zeros_like(st_f)
        st_b[...] = jnp.zeros_like(st_b)
        for h in range(GLA_HEADS):
            rows = slice(h * GLA_DK, (h + 1) * GLA_DK)
            cols = slice(h * GLA_DV, (h + 1) * GLA_DV)
            st_f[rows, cols] = s0f_ref[h]
            st_b[rows, cols] = s0b_ref[h]

    bd = bd_ref[...]
    vbd = vbd_ref[...]
    of_ref[...] = _gla_direction(gq_f[...], la_f[...], gkt_f[...], lat_f[...], gv_f[...],
                                 lq2f[...], lkt2f[...], mkf, bd, vbd, st_f, levels_of)
    ob_ref[...] = _gla_direction(gq_b[...], la_b[...], gkt_b[...], lat_b[...], gv_b[...],
                                 lq2b[...], lkt2b[...], mkb, bd, vbd, st_b, levels_of)

    @pl.when(n == pl.num_programs(1) - 1)
    def _():
        for h in range(GLA_HEADS):
            rows = slice(h * GLA_DK, (h + 1) * GLA_DK)
            cols = slice(h * GLA_DV, (h + 1) * GLA_DV)
            sf_ref[h] = st_f[rows, cols]
            sb_ref[h] = st_b[rows, cols]


def _gla(gq, la, gkt, lat, gv, s0f, s0b, consts, row0, n_batch, seq):
    (cf, cb), levels_of, bd, vbd = consts
    c = GLA_CHUNK
    nc = seq // c
    blk0 = row0 // c
    n_la_blocks_b = 1
    fwd = lambda b, n: blk0 + b * nc + n
    bwd = lambda b, n: blk0 + b * nc + (nc - 1 - n)

    def tok(w, which, col=0):
        return pl.BlockSpec((c, w), lambda b, n: (which(b, n), col))

    def tr(r, which, row=0):
        return pl.BlockSpec((r, c), lambda b, n: (row, which(b, n)))

    full = lambda a: pl.BlockSpec(a.shape, lambda b, n: (0,) * a.ndim)
    st_spec = pl.BlockSpec((None, GLA_HEADS, GLA_DK, GLA_DV), lambda b, n: (b, 0, 0, 0))
    in_specs = [tok(GLA_KW, fwd), tok(GLA_KW, fwd, 0), tr(GLA_KW, fwd), tr(GLA_KW, fwd, 0),
                tok(GLA_WIDTH, fwd),
                tok(GLA_KW, bwd), tok(GLA_KW, bwd, n_la_blocks_b), tr(GLA_KW, bwd),
                tr(GLA_KW, bwd, 1), tok(GLA_WIDTH, bwd),
                st_spec, st_spec,
                full(cf[0]), full(cf[1]), full(cf[2]), full(cb[0]), full(cb[1]), full(cb[2]),
                full(bd), full(vbd)]
    out_specs = (pl.BlockSpec((c, GLA_WIDTH), lambda b, n: (b * nc + n, 0)),
                 pl.BlockSpec((c, GLA_WIDTH), lambda b, n: (b * nc + (nc - 1 - n), 0)),
                 st_spec, st_spec)
    out_shape = (jax.ShapeDtypeStruct((n_batch * seq, GLA_WIDTH), F32),
                 jax.ShapeDtypeStruct((n_batch * seq, GLA_WIDTH), F32),
                 jax.ShapeDtypeStruct((n_batch, GLA_HEADS, GLA_DK, GLA_DV), F32),
                 jax.ShapeDtypeStruct((n_batch, GLA_HEADS, GLA_DK, GLA_DV), F32))
    return pl.pallas_call(
        functools.partial(_gla_kernel, levels_of=levels_of),
        grid=(n_batch, nc), in_specs=in_specs, out_specs=out_specs, out_shape=out_shape,
        scratch_shapes=[pltpu.VMEM((GLA_KW, GLA_WIDTH), F32), pltpu.VMEM((GLA_KW, GLA_WIDTH), F32)],
        compiler_params=_cparams(2), name="gla",
    )(gq, la, gkt, lat, gv, gq, la, gkt, lat, gv, s0f, s0b,
      cf[0], cf[1], cf[2], cb[0], cb[1], cb[2], bd, vbd)


def _outproj_kernel(attc_ref, attl_ref, ofc_ref, ofl_ref, obc_ref, obl_ref, gg_ref, xc_ref, xl_ref,
                    mod_ref, wo_ref, gn_ref, l1g_ref, l1b_ref, wrt_ref, sw13_ref, sw2_ref,
                    base_ref, u2_ref, lg_ref, *, n_ctx_tiles):
    is_ctx = pl.program_id(0) < n_ctx_tiles
    pick = lambda a_ref, b_ref: jnp.where(is_ctx, a_ref[...], b_ref[...])
    m = mod_ref[...]
    gate1 = m[:, 2 * D_MODEL:3 * D_MODEL]
    shift2 = m[:, 3 * D_MODEL:4 * D_MODEL]
    scale2 = m[:, 4 * D_MODEL:5 * D_MODEL]
    gate2 = m[:, 5 * D_MODEL:6 * D_MODEL]
    og = pick(ofc_ref, ofl_ref) + pick(obc_ref, obl_ref)
    gg = gg_ref[...].astype(F32)
    parts = []
    for h in range(GLA_HEADS):
        blk = og[:, h * GLA_DV:(h + 1) * GLA_DV]
        ms = jnp.mean(blk * blk, axis=-1, keepdims=True)
        nb = blk * lax.rsqrt(ms + EPS) * gn_ref[...]
        parts.append((nb * _silu(gg[:, h * GLA_DV:(h + 1) * GLA_DV])).astype(BF16))
    att_t = pick(attc_ref, attl_ref)
    hmix = (lax.dot_general(att_t, wo_ref[0:ATT_WIDTH, :], (((0,), (0,)), ((), ())),
                            preferred_element_type=F32)
            + jnp.dot(jnp.concatenate(parts, axis=1), wo_ref[ATT_WIDTH:, :],
                      preferred_element_type=F32))
    x1 = _layer_norm(ALPHA * pick(xc_ref, xl_ref) + gate1 * hmix, l1g_ref[...], l1b_ref[...])
    u2 = x1 * (1.0 + scale2) + shift2
    u2b = u2.astype(BF16)
    lg_ref[...] = lax.dot_general(wrt_ref[...], u2b, (((1,), (1,)), ((), ())),
                                  preferred_element_type=F32)
    ab = jnp.dot(u2b, sw13_ref[...], preferred_element_type=F32)
    hid = (_silu(ab[:, 0:SHARED_FF]) * ab[:, SHARED_FF:2 * SHARED_FF]).astype(BF16)
    shared = jnp.dot(hid, sw2_ref[...], preferred_element_type=F32)
    base_ref[...] = ALPHA * x1 + gate2 * shared
    _pack_rows(u2_ref, u2)


def _out_projection(att_c, att_l, of_c, of_l, ob_c, ob_l, gg, x_c, x_l, mod3, wo, gn, l1g, l1b, wrt,
                    sw13, sw2, n_seq_tiles):
    n_ctx = x_c.shape[0]
    n = n_ctx + x_l.shape[0]
    tb = TOK_TILE
    n_ctx_tiles = n_ctx // tb

    def mod_idx(i):
        return (jnp.where(i < n_ctx_tiles, 0, 1 + (i - n_ctx_tiles) // n_seq_tiles), 0, 0)

    ctx_blk = lambda i: jnp.minimum(i, n_ctx_tiles - 1)
    lat_blk = lambda i: jnp.maximum(i - n_ctx_tiles, 0)
    tok = lambda w: pl.BlockSpec((tb, w), lambda i: (i, 0))
    tok_c = lambda w: pl.BlockSpec((tb, w), lambda i: (ctx_blk(i), 0))
    tok_l = lambda w: pl.BlockSpec((tb, w), lambda i: (lat_blk(i), 0))
    full = lambda a: pl.BlockSpec(a.shape, lambda i: (0,) * a.ndim)
    return pl.pallas_call(
        functools.partial(_outproj_kernel, n_ctx_tiles=n_ctx_tiles),
        grid=(n // tb,),
        in_specs=[pl.BlockSpec((ATT_WIDTH, tb), lambda i: (0, ctx_blk(i))),
                  pl.BlockSpec((ATT_WIDTH, tb), lambda i: (0, lat_blk(i))),
                  tok_c(GLA_WIDTH), tok_l(GLA_WIDTH), tok_c(GLA_WIDTH), tok_l(GLA_WIDTH),
                  tok(GLA_WIDTH), tok_c(D_MODEL), tok_l(D_MODEL),
                  pl.BlockSpec((None, 1, mod3.shape[2]), mod_idx),
                  full(wo), full(gn), full(l1g), full(l1b), full(wrt), full(sw13), full(sw2)],
        out_specs=(tok(D_MODEL),
                   pl.BlockSpec((tb * PACK_CHUNKS, LANES), lambda i: (i, 0)),
                   pl.BlockSpec((N_EXPERTS, tb), lambda i: (0, i))),
        out_shape=(jax.ShapeDtypeStruct((n, D_MODEL), F32),
                   jax.ShapeDtypeStruct((n * PACK_CHUNKS, LANES), U32),
                   jax.ShapeDtypeStruct((N_EXPERTS, n), F32)),
        compiler_params=_cparams(1), name="out_projection",
    )(att_c, att_l, of_c, of_l, ob_c, ob_l, gg, x_c, x_l, mod3, wo, gn, l1g, l1b, wrt, sw13, sw2)


def _route_kernel(lg_ref, bias_ref, upper_ref, idx_ref, w_ref, pos_ref, cnt_ref, run_ref):
    i = pl.program_id(0)

    @pl.when(i == 0)
    def _():
        run_ref[...] = jnp.zeros_like(run_ref)

    s = jax.nn.sigmoid(lg_ref[...])
    work = s + bias_ref[...]
    rows = lax.broadcasted_iota(I32, s.shape, 0).astype(F32)
    sel = jnp.zeros(s.shape, F32)
    idxs, vals = [], []
    for _ in range(TOP_K):
        mx = jnp.max(work, axis=0, keepdims=True)
        idx = jnp.min(jnp.where(work == mx, rows, float(N_EXPERTS)), axis=0, keepdims=True)
        hit = rows == idx
        vals.append(jnp.sum(jnp.where(hit, s, 0.0), axis=0, keepdims=True))
        idxs.append(idx)
        sel = jnp.where(hit, 1.0, sel)
        work = jnp.where(hit, -jnp.inf, work)
    den = functools.reduce(jnp.add, vals)
    rank = jnp.dot(sel.astype(BF16), upper_ref[...], preferred_element_type=F32) + run_ref[:, 0:1]
    for k in range(TOP_K):
        idx_ref[k:k + 1, :] = idxs[k].astype(I32)
        w_ref[k:k + 1, :] = vals[k] / den * ROUTED_SCALE
        pos_ref[k:k + 1, :] = jnp.sum(jnp.where(rows == idxs[k], rank, 0.0), axis=0,
                                      keepdims=True).astype(I32)
    run_ref[...] = run_ref[...] + jnp.sum(sel, axis=1, keepdims=True)
    cnt_ref[...] = run_ref[...]


def _route(logits_t, bias_col, upper):
    n = logits_t.shape[1]
    tt = TOK_TILE
    row = lambda dt: jax.ShapeDtypeStruct((TOP_K, n), dt)
    blk = pl.BlockSpec((TOP_K, tt), lambda i: (0, i))
    return pl.pallas_call(
        _route_kernel,
        grid=(n // tt,),
        in_specs=[pl.BlockSpec((N_EXPERTS, tt), lambda i: (0, i)),
                  pl.BlockSpec((N_EXPERTS, 1), lambda i: (0, 0)),
                  pl.BlockSpec((tt, tt), lambda i: (0, 0))],
        out_specs=(blk, blk, blk, pl.BlockSpec((N_EXPERTS, LANES), lambda i: (0, 0))),
        out_shape=(row(I32), row(F32), row(I32), jax.ShapeDtypeStruct((N_EXPERTS, LANES), F32)),
        scratch_shapes=[pltpu.VMEM((N_EXPERTS, LANES), F32)],
        compiler_params=_cparams(1), name="route",
    )(logits_t, bias_col, upper)


def _dest_kernel(cnt_ref, lower_ref, idx_ref, pos_ref, dest_ref, bexp_ref, bval_ref, nused_ref):
    cnt = cnt_ref[...]
    nblk = jnp.floor((cnt + (MOE_ROWS - 1)) * (1.0 / MOE_ROWS))
    bstart = jnp.dot(lower_ref[...], nblk, precision=HIGHEST, preferred_element_type=F32)
    bend = bstart + nblk
    pstart = bstart[:, 0:1] * MOE_ROWS
    rows = lax.broadcasted_iota(I32, (N_EXPERTS, idx_ref.shape[1]), 0)
    for k in range(TOP_K):
        hit = rows == idx_ref[k:k + 1, :]
        dest_ref[k:k + 1, :] = (jnp.sum(jnp.where(hit, pstart, 0.0), axis=0, keepdims=True)
                                .astype(I32) + pos_ref[k:k + 1, :])

    @pl.when(pl.program_id(0) == 0)
    def _():
        nb = bexp_ref.shape[1]
        bid = lax.broadcasted_iota(I32, (N_EXPERTS, nb), 1).astype(F32)
        inside = jnp.logical_and(bid >= bstart[:, 0:1], bid < bend[:, 0:1])
        erow = lax.broadcasted_iota(I32, (N_EXPERTS, nb), 0).astype(F32)
        bexp_ref[...] = jnp.sum(jnp.where(inside, erow, 0.0), axis=0, keepdims=True).astype(I32)
        valid = jnp.clip(cnt[:, 0:1] - (bid - bstart[:, 0:1]) * MOE_ROWS, 0.0, float(MOE_ROWS))
        bval_ref[...] = jnp.sum(jnp.where(inside, valid, 0.0), axis=0, keepdims=True).astype(I32)
        nused_ref[...] = jnp.max(bend, axis=0, keepdims=True).astype(I32)


def _destinations(counts, lower, idx_t, pos_t, n_blocks_pad):
    n = idx_t.shape[1]
    tt = TOK_TILE
    blk = pl.BlockSpec((TOP_K, tt), lambda i: (0, i))
    one = lambda w: pl.BlockSpec((1, w), lambda i: (0, 0))
    return pl.pallas_call(
        _dest_kernel,
        grid=(n // tt,),
        in_specs=[pl.BlockSpec((N_EXPERTS, LANES), lambda i: (0, 0)),
                  pl.BlockSpec((N_EXPERTS, N_EXPERTS), lambda i: (0, 0)), blk, blk],
        out_specs=(blk, one(n_blocks_pad), one(n_blocks_pad), one(LANES)),
        out_shape=(jax.ShapeDtypeStruct((TOP_K, n), I32),
                   jax.ShapeDtypeStruct((1, n_blocks_pad), I32),
                   jax.ShapeDtypeStruct((1, n_blocks_pad), I32),
                   jax.ShapeDtypeStruct((1, LANES), I32)),
        compiler_params=_cparams(1), name="destinations",
    )(counts, lower, idx_t, pos_t)


SC_WINDOW = 128
SC_WINDOWS_PER_STEP = 8


def _invert_rows(dest, ids, n_rows):
    m = dest.size
    mesh = plsc.VectorSubcoreMesh(core_axis_name="core", subcore_axis_name="subcore")

    @functools.partial(pl.kernel, out_type=jax.ShapeDtypeStruct((n_rows,), I32), mesh=mesh,
                       scratch_types=[])
    def invert(val_hbm, idx_hbm, out_hbm):
        def body(val_vmem, idx_vmem):
            for j in range(SC_WINDOWS_PER_STEP):
                pltpu.sync_copy(val_vmem.at[j], out_hbm.at[idx_vmem.at[j]])

        blk = pl.BlockSpec((SC_WINDOWS_PER_STEP, SC_WINDOW), lambda i: (i, 0))
        pltpu.emit_pipeline(
            body, grid=(m // (SC_WINDOW * SC_WINDOWS_PER_STEP),),
            in_specs=[blk, blk], out_specs=[], core_axis_name=("core", "subcore"),
            dimension_semantics=(pltpu.PARALLEL,),
        )(val_hbm, idx_hbm)

    shape = (m // SC_WINDOW, SC_WINDOW)
    return invert(jnp.asarray(ids.reshape(shape), I32), dest.reshape(shape))


def _pack_rows(ref, x, row0=0):
    bits = pltpu.bitcast(x.astype(BF16).astype(F32), U32)
    for s in range(PACK_CHUNKS):
        lo = bits[:, (2 * s) * LANES:(2 * s + 1) * LANES] >> 16
        hi = bits[:, (2 * s + 1) * LANES:(2 * s + 2) * LANES] & jnp.uint32(HIGH_HALF)
        ref[pl.ds(row0 + s, x.shape[0], stride=PACK_CHUNKS), :] = lo | hi


def _unpack_rows(ref, n_rows, row0=0):
    parts = []
    for s in range(PACK_CHUNKS):
        w = ref[pl.ds(row0 + s, n_rows, stride=PACK_CHUNKS), :]
        parts.append(pltpu.bitcast(w << 16, F32))
        parts.append(pltpu.bitcast(w & jnp.uint32(HIGH_HALF), F32))
    return jnp.concatenate(parts, axis=1).astype(BF16)


def _moe_kernel(bexp_ref, bval_ref, nused_ref, u2p_hbm, src_hbm, w1_hbm, w3_hbm, w2_hbm, yt_hbm,
                u2p_vmem, w1_f, w3_f, w2_f, w13_s, w2_s, xbuf, ybuf, src_smem,
                sem_in, sem_src, sem_w, sem_out, *, n_tokens):
    n_used = nused_ref[0]
    br = MOE_ROWS
    grp = SRC_GROUP * br
    trash0 = n_tokens * TOP_K

    def src_copy(g):
        return pltpu.make_async_copy(src_hbm.at[pl.ds(g * grp, grp)],
                                     src_smem.at[pl.ds(lax.rem(g, 2) * grp, grp)], sem_src)

    def out_wait(slot):
        pltpu.make_async_copy(ybuf.at[pl.ds(slot * br * PACK_CHUNKS, br * PACK_CHUNKS)],
                              yt_hbm.at[pl.ds(0, br * PACK_CHUNKS)], sem_out.at[slot]).wait()

    def src_base(blk):
        return lax.rem(blk // SRC_GROUP, 2) * grp + lax.rem(blk, SRC_GROUP) * br

    def scatter_row(blk_slot, sbase, valid, r, priority=0):
        dst = jnp.where(r < valid, src_smem[sbase + r], trash0 + blk_slot * br + r)
        pltpu.make_async_copy(
            ybuf.at[pl.ds(pl.multiple_of((blk_slot * br + r) * PACK_CHUNKS, PACK_CHUNKS), PACK_CHUNKS)],
            yt_hbm.at[pl.ds(pl.multiple_of(dst * PACK_CHUNKS, PACK_CHUNKS), PACK_CHUNKS)],
            sem_out.at[blk_slot]).start(priority=priority)

    def gather_row(xslot, sbase, r):
        row = lax.shift_right_logical(src_smem[sbase + r], TOP_K_LOG2 - PACK_CHUNKS_LOG2)
        row = jnp.minimum(row & (int(jnp.iinfo(I32).max) - (PACK_CHUNKS - 1)),
                          (n_tokens - 1) * PACK_CHUNKS)
        dst = pl.multiple_of((xslot * br + r) * PACK_CHUNKS, PACK_CHUNKS)
        xbuf[pl.ds(dst, PACK_CHUNKS), :] = u2p_vmem[pl.ds(pl.multiple_of(row, PACK_CHUNKS), PACK_CHUNKS), :]

    def weight_copies(e, wslot):
        return [pltpu.make_async_copy(src.at[e], dst.at[wslot], sem_w.at[wslot])
                for src, dst in ((w1_hbm, w1_f), (w3_hbm, w3_f), (w2_hbm, w2_f))]

    cp = pltpu.make_async_copy(u2p_hbm, u2p_vmem, sem_in)
    cp.start()
    src_copy(0).start()
    for wcp in weight_copies(bexp_ref[0], 0):
        wcp.start()
    ybuf[...] = jnp.zeros_like(ybuf)
    cp.wait()
    src_copy(0).wait()
    lax.fori_loop(0, br, lambda r, c: (gather_row(0, 0, r), c)[1], 0)

    def block(b, wslot):
        g = b // SRC_GROUP
        phase = lax.rem(b, SRC_GROUP)

        more = (g + 1) * SRC_GROUP < n_used

        @pl.when(jnp.logical_and(phase == 1, more))
        def _():
            src_copy(g + 1).start()

        @pl.when(jnp.logical_and(phase == SRC_GROUP - 1, more))
        def _():
            src_copy(g + 1).wait()

        e = bexp_ref[b]
        prev = bexp_ref[jnp.maximum(b - 1, 0)]

        @pl.when(jnp.logical_or(b == 0, e != prev))
        def _():
            for wcp in weight_copies(e, wslot):
                wcp.wait()
            w13_s[:, 0:EXPERT_FF] = w1_f[wslot].astype(BF16)
            w13_s[:, EXPERT_FF:2 * EXPERT_FF] = w3_f[wslot].astype(BF16)
            w2_s[...] = w2_f[wslot].astype(BF16)
            nxt = lax.while_loop(
                lambda j: jnp.logical_and(j < n_used, bexp_ref[jnp.minimum(j, n_used - 1)] == e),
                lambda j: j + 1, b + 1)

            @pl.when(nxt < n_used)
            def _():
                for wcp in weight_copies(bexp_ref[nxt], 1 - wslot):
                    wcp.start()

        switch = jnp.logical_and(b + 1 < n_used, bexp_ref[b + 1] != e)

        valid = bval_ref[b]
        sbase = src_base(b)
        slot = lax.rem(b, 2)

        pb = jnp.maximum(b - 1, 0)
        p_valid = jnp.where(b > 0, bval_ref[pb], 0)
        p_base = src_base(pb)
        n_base = src_base(b + 1)
        for r in range(br):
            gather_row(1 - slot, n_base, r)
            scatter_row(1 - slot, p_base, p_valid, r, priority=r % 2)

        x = _unpack_rows(xbuf, br, slot * (br * PACK_CHUNKS))
        rows = lax.broadcasted_iota(I32, x.shape, 0)
        x = jnp.where(rows < valid, x, jnp.zeros_like(x))
        ab = jnp.dot(x, w13_s[...], preferred_element_type=F32)
        hid = (_silu(ab[:, 0:EXPERT_FF]) * ab[:, EXPERT_FF:2 * EXPERT_FF]).astype(BF16)
        y = jnp.dot(hid, w2_s[...], preferred_element_type=F32)

        @pl.when(b >= 1)
        def _():
            out_wait(slot)

        _pack_rows(ybuf, y, slot * (br * PACK_CHUNKS))
        return jnp.where(switch, 1 - wslot, wslot)

    lax.fori_loop(0, n_used, block, 0)

    last = n_used - 1
    l_slot = lax.rem(last, 2)
    l_base = src_base(last)
    l_valid = bval_ref[last]
    lax.fori_loop(0, br, lambda r, c: (scatter_row(l_slot, l_base, l_valid, r), c)[1], 0)
    out_wait(1 - l_slot)
    out_wait(l_slot)


def _moe_experts(bexp, bval, nused, u2p, row_src, w1, w3, w2, n_tokens):
    br = MOE_ROWS
    any_spec = pl.BlockSpec(memory_space=pl.ANY)
    grid_spec = pltpu.PrefetchScalarGridSpec(
        num_scalar_prefetch=3, grid=(1,),
        in_specs=[any_spec] * 5,
        out_specs=any_spec,
        scratch_shapes=[pltpu.VMEM(u2p.shape, U32),
                        pltpu.VMEM((2, D_MODEL, EXPERT_FF), F32),
                        pltpu.VMEM((2, D_MODEL, EXPERT_FF), F32),
                        pltpu.VMEM((2, EXPERT_FF, D_MODEL), F32),
                        pltpu.VMEM((D_MODEL, 2 * EXPERT_FF), BF16),
                        pltpu.VMEM((EXPERT_FF, D_MODEL), BF16),
                        pltpu.VMEM((2 * PACK_CHUNKS * br, LANES), U32),
                        pltpu.VMEM((2 * br * PACK_CHUNKS, LANES), U32),
                        pltpu.SMEM((2 * SRC_GROUP * br,), I32),
                        pltpu.SemaphoreType.DMA, pltpu.SemaphoreType.DMA,
                        pltpu.SemaphoreType.DMA((2,)), pltpu.SemaphoreType.DMA((2,))])
    n_out_tiles = n_tokens * TOP_K + 2 * br
    return pl.pallas_call(
        functools.partial(_moe_kernel, n_tokens=n_tokens), grid_spec=grid_spec,
        out_shape=jax.ShapeDtypeStruct((n_out_tiles * PACK_CHUNKS, LANES), U32),
        compiler_params=pltpu.CompilerParams(dimension_semantics=("arbitrary",),
                                             vmem_limit_bytes=MOE_VMEM_LIMIT),
        name="moe_experts",
    )(bexp, bval, nused, u2p, row_src, w1, w3, w2)


def _combine_kernel(w_hbm, yt_ref, base_ref, mod_ref, g_ref, b_ref, yc_ref, yl_ref,
                    w_smem, acc_lo, acc_hi, sem_w, *, n_ctx_tiles):
    i = pl.program_id(0)
    n_steps = pl.num_programs(0)
    n_tok = acc_lo.shape[0] // PACK_CHUNKS
    n_idx = n_tok * TOP_K

    def w_copy(tile):
        return pltpu.make_async_copy(w_hbm.at[pl.ds(tile * n_idx, n_idx)],
                                     w_smem.at[pl.ds(lax.rem(tile, 2) * n_idx, n_idx)], sem_w)

    @pl.when(i == 0)
    def _():
        w_copy(i).start()

    w_copy(i).wait()

    @pl.when(i + 1 < n_steps)
    def _():
        w_copy(i + 1).start()

    wbase = lax.rem(i, 2) * n_idx

    per_tile = SUBLANES // PACK_CHUNKS
    first = lax.broadcasted_iota(I32, (SUBLANES, LANES), 0) < PACK_CHUNKS

    def reduce_token(t):
        lo = hi = None
        for m in range(TOP_K // per_tile):
            j = t * TOP_K + m * per_tile
            words = yt_ref[pl.ds(pl.multiple_of(j * PACK_CHUNKS, SUBLANES), SUBLANES), :]
            wgt = jnp.where(first, w_smem[wbase + j], w_smem[wbase + j + 1])
            t_lo = wgt * pltpu.bitcast(words << 16, F32)
            t_hi = wgt * pltpu.bitcast(words & jnp.uint32(HIGH_HALF), F32)
            lo = t_lo if lo is None else lo + t_lo
            hi = t_hi if hi is None else hi + t_hi
        row = pl.multiple_of(t * PACK_CHUNKS, PACK_CHUNKS)
        acc_lo[pl.ds(row, PACK_CHUNKS), :] = lo[0:PACK_CHUNKS] + lo[PACK_CHUNKS:SUBLANES]
        acc_hi[pl.ds(row, PACK_CHUNKS), :] = hi[0:PACK_CHUNKS] + hi[PACK_CHUNKS:SUBLANES]

    def reduce_group(i, carry):
        for u in range(COMB_UNROLL):
            reduce_token(i * COMB_UNROLL + u)
        return carry

    lax.fori_loop(0, n_tok // COMB_UNROLL, reduce_group, 0)
    parts = []
    for s in range(PACK_CHUNKS):
        parts.append(acc_lo[pl.ds(s, n_tok, stride=PACK_CHUNKS), :])
        parts.append(acc_hi[pl.ds(s, n_tok, stride=PACK_CHUNKS), :])
    moe = jnp.concatenate(parts, axis=1)
    gate2 = mod_ref[:, 5 * D_MODEL:6 * D_MODEL]
    y = _layer_norm(base_ref[...] + gate2 * moe, g_ref[...], b_ref[...])

    @pl.when(i < n_ctx_tiles)
    def _():
        yc_ref[...] = y

    @pl.when(i >= n_ctx_tiles)
    def _():
        yl_ref[...] = y


def _combine(w_flat, yt, base, mod3, l2g, l2b, n_ctx, seq_tokens):
    n = base.shape[0]
    tc = COMB_TILE
    n_ctx_tiles = n_ctx // tc
    n_seq_tiles = seq_tokens // tc

    def mod_idx(i):
        return (jnp.where(i < n_ctx_tiles, 0, 1 + (i - n_ctx_tiles) // n_seq_tiles), 0, 0)

    full = lambda a: pl.BlockSpec(a.shape, lambda i: (0,) * a.ndim)
    return pl.pallas_call(
        functools.partial(_combine_kernel, n_ctx_tiles=n_ctx_tiles),
        grid=(n // tc,),
        in_specs=[pl.BlockSpec(memory_space=pl.ANY),
                  pl.BlockSpec((tc * TOP_K * PACK_CHUNKS, LANES), lambda i: (i, 0)),
                  pl.BlockSpec((tc, D_MODEL), lambda i: (i, 0)),
                  pl.BlockSpec((None, 1, mod3.shape[2]), mod_idx), full(l2g), full(l2b)],
        out_specs=(pl.BlockSpec((tc, D_MODEL), lambda i: (jnp.minimum(i, n_ctx_tiles - 1), 0)),
                   pl.BlockSpec((tc, D_MODEL), lambda i: (jnp.maximum(i - n_ctx_tiles, 0), 0))),
        out_shape=(jax.ShapeDtypeStruct((n_ctx, D_MODEL), F32),
                   jax.ShapeDtypeStruct((n - n_ctx, D_MODEL), F32)),
        scratch_shapes=[pltpu.SMEM((2 * tc * TOP_K,), F32),
                        pltpu.VMEM((tc * PACK_CHUNKS, LANES), F32),
                        pltpu.VMEM((tc * PACK_CHUNKS, LANES), F32),
                        pltpu.SemaphoreType.DMA],
        compiler_params=_cparams(1), name="combine",
    )(w_flat, yt, base, mod3, l2g, l2b)


def _rope_tables(n_tok, tile):
    f32 = np.float32
    rows = n_tok // GRID_W
    row_idx = np.repeat(np.arange(rows, dtype=f32), GRID_W)
    col_idx = np.tile(np.arange(GRID_W, dtype=f32), rows)
    inv_freq = (1.0 / (ROPE_THETA ** (np.arange(0, ROPE_AXIS_DIM, 2, dtype=f32) / ROPE_AXIS_DIM))).astype(f32)
    ang_r = row_idx[:, None] * inv_freq[None, :]
    ang_c = col_idx[:, None] * inv_freq[None, :]
    ang = np.concatenate([ang_r, ang_r, ang_c, ang_c], axis=-1)
    cos, sin = np.cos(ang), np.sin(ang)
    quarter = (np.arange(HEAD_DIM) // (ROPE_AXIS_DIM // 2)) % 2
    sin_a = np.where(quarter == 0, -sin, 0.0)
    sin_b = np.where(quarter == 1, sin, 0.0)
    rep = LANES // HEAD_DIM
    ident = lambda v: np.full((tile, LANES), v, f32)
    cos_t = np.concatenate([np.tile(cos, (1, rep)), ident(1.0)], axis=0)
    sa_t = np.concatenate([np.tile(sin_a, (1, rep)), ident(0.0)], axis=0)
    sb_t = np.concatenate([np.tile(sin_b, (1, rep)), ident(0.0)], axis=0)
    ident_tr = lambda v: np.full((HEAD_DIM, tile), v, f32)
    cos_tr = np.concatenate([cos.T, ident_tr(1.0)], axis=1)
    sin_tr = np.concatenate([sin.T, ident_tr(0.0)], axis=1)
    return tuple(jnp.asarray(t, F32) for t in (cos_t, sa_t, sb_t, cos_tr, sin_tr))


def _dup_heads(a):
    parts = []
    for h in range(KV_HEADS):
        blk = a[..., h * HEAD_DIM:(h + 1) * HEAD_DIM]
        parts += [blk] * (LANES // HEAD_DIM)
    return jnp.concatenate(parts, axis=-1)


def kernel(x_prompt, x_sample, cache_k, cache_v, state_gla_fwd, state_gla_bwd, c, c_ctx, w_ada, b_ada, w_in, q_norm, k_norm, gla_wa_fwd, gla_ba_fwd, gla_wa_bwd, gla_ba_bwd, gla_norm, w_out, ln1_g, ln1_b, ln2_g, ln2_b, w_router, router_bias, exp_w1, exp_w3, exp_w2, sh_w1, sh_w3, sh_w2):
    n_ctx_b, ctx_seq, _ = x_prompt.shape
    n_lat_b, lat_seq, _ = x_sample.shape
    n_ctx = n_ctx_b * ctx_seq
    n_lat = n_lat_b * lat_seq
    n = n_ctx + n_lat
    l = 0

    x_c = x_prompt.reshape(n_ctx, D_MODEL)
    x_l = x_sample.reshape(n_lat, D_MODEL)

    c_rows = jnp.zeros((SUBLANES, D_MODEL), F32).at[0].set(c_ctx).at[1:1 + n_lat_b].set(c)
    mod = _modulation(c_rows, w_ada[l], b_ada[l][None, :])
    mod3 = mod.reshape(SUBLANES, 1, 6 * D_MODEL)

    wi = w_in[l]
    o_q, o_k, o_v, o_gq, o_gk, o_gv, o_gg, o_rf, o_rb, o_end = np.cumsum(
        [0, ATT_WIDTH, KV_HEADS * HEAD_DIM, KV_HEADS * HEAD_DIM, GLA_KW, GLA_KW, GLA_WIDTH, GLA_WIDTH,
         GLA_GATE_RANK, GLA_GATE_RANK])
    w_tok = jnp.concatenate([
        _dup_heads(wi[:, o_k:o_v]), wi[:, o_v:o_gq], wi[:, o_gq:o_gk],
        wi[:, o_gv:o_gg], wi[:, o_gg:o_rf], wi[:, o_rf:o_end],
        jnp.zeros((D_MODEL, LANES - 2 * GLA_GATE_RANK), F32)], axis=1).astype(BF16)
    w_tr = jnp.concatenate([wi[:, o_q:o_k], wi[:, o_v:o_gq], wi[:, o_gk:o_gv], wi[:, o_rf:o_end]],
                           axis=1).T.astype(BF16)
    rep = LANES // HEAD_DIM
    qn = q_norm[l][:, None]
    kn = jnp.tile(k_norm[l], rep)[None, :]
    seg = jnp.asarray(np.kron(np.eye(rep), np.ones((HEAD_DIM, HEAD_DIM))), BF16)
    wa = jnp.zeros((LANES, 2 * GLA_KW), F32)
    wa = wa.at[0:GLA_GATE_RANK, 0:GLA_KW].set(gla_wa_fwd[l])
    wa = wa.at[GLA_GATE_RANK:2 * GLA_GATE_RANK, GLA_KW:].set(gla_wa_bwd[l])
    ba = jnp.concatenate([gla_ba_fwd[l], gla_ba_bwd[l]])[None, :]
    wat = wa[0:2 * GLA_GATE_RANK, :].T
    bat = ba.T
    cos_t, sa_t, sb_t, cos_tr, sin_tr = _rope_tables(lat_seq, TOK_TILE)

    (qt, k_dup, vt, k32, v32, gq, gv, gg, la, gkt, lat) = _in_projection(
        x_c, x_l, mod3, w_tok, w_tr, qn, kn, cos_t, sa_t, sb_t, cos_tr, sin_tr, seg, wa, ba, wat, bat,
        lat_seq // TOK_TILE)

    ck = _dup_heads(cache_k[:, l].reshape(n_lat_b, -1, KV_HEADS * HEAD_DIM)).astype(BF16)
    cvt = cache_v[:, l].reshape(n_lat_b, -1, KV_HEADS * HEAD_DIM).transpose(0, 2, 1).astype(BF16)
    att_c = _attention(qt, k_dup, vt, None, 0, n_ctx_b, ctx_seq)
    att_l = _attention(qt, k_dup, vt, (ck, cvt), n_ctx, n_lat_b, lat_seq)

    gconst, levels_of = _gla_constants()
    to_dev = lambda t: (jnp.asarray(t[0], BF16), jnp.asarray(t[1], BF16), jnp.asarray(t[2], F32))
    bd = jnp.asarray(np.kron(np.eye(GLA_HEADS), np.ones((GLA_DK, GLA_DV))), BF16)
    vbd = jnp.asarray(np.kron(np.eye(GLA_HEADS), np.ones((GLA_CHUNK, GLA_DV))), BF16)
    consts = ((to_dev(gconst["f"]), to_dev(gconst["b"])), levels_of, bd, vbd)
    s_zero = jnp.zeros((n_ctx_b, GLA_HEADS, GLA_DK, GLA_DV), F32)
    of_c, ob_c, sf_new, sb_new = _gla(gq, la, gkt, lat, gv, s_zero, s_zero, consts, 0, n_ctx_b, ctx_seq)
    of_l, ob_l, _, _ = _gla(gq, la, gkt, lat, gv, state_gla_fwd[:, l], state_gla_bwd[:, l], consts,
                            n_ctx, n_lat_b, lat_seq)

    sw13 = jnp.concatenate([sh_w1[l], sh_w3[l]], axis=1).astype(BF16)
    base, u2_rows, logits_t = _out_projection(
        att_c, att_l, of_c, of_l, ob_c, ob_l, gg, x_c, x_l, mod3, w_out[l].astype(BF16),
        gla_norm[l][None, :], ln1_g[l][None, :], ln1_b[l][None, :], w_router[l].T.astype(BF16), sw13,
        sh_w2[l].astype(BF16), lat_seq // TOK_TILE)

    upper = jnp.asarray(np.triu(np.ones((TOK_TILE, TOK_TILE)), 1), BF16)
    idx_t, w_t, pos_t, counts = _route(logits_t, router_bias[l][:, None], upper)
    n_blocks = n * TOP_K // MOE_ROWS + N_EXPERTS
    n_blocks_pad = -(-n_blocks // LANES) * LANES
    lower = jnp.asarray(np.tril(np.ones((N_EXPERTS, N_EXPERTS)), -1), F32)
    dest_t, bexp, bval, nused = _destinations(counts, lower, idx_t, pos_t, n_blocks_pad)
    w_flat = w_t.T.reshape(-1)

    ids = np.arange(n, dtype=np.int32)[None, :] * TOP_K + np.arange(TOP_K, dtype=np.int32)[:, None]
    row_src = _invert_rows(dest_t, ids, n_blocks * MOE_ROWS)
    yt = _moe_experts(bexp.reshape(-1), bval.reshape(-1), nused.reshape(-1)[0:1], u2_rows, row_src,
                      exp_w1[l], exp_w3[l], exp_w2[l], n)
    y_c, y_l = _combine(w_flat, yt, base, mod3, ln2_g[l][None, :], ln2_b[l][None, :], n_ctx, lat_seq)

    y_prompt = y_c.reshape(n_ctx_b, ctx_seq, D_MODEL)
    y_sample = y_l.reshape(n_lat_b, lat_seq, D_MODEL)
    new_cache_k = k32.reshape(n_ctx_b, 1, ctx_seq, KV_HEADS, HEAD_DIM)
    new_cache_v = v32.reshape(n_ctx_b, 1, ctx_seq, KV_HEADS, HEAD_DIM)
    return (y_prompt, y_sample, new_cache_k, new_cache_v, sf_new[:, None], sb_new[:, None])
```

```python
import functools

import numpy as np
import jax
import jax.numpy as jnp
from jax import lax
from jax.experimental import pallas as pl
from jax.experimental.pallas import tpu as pltpu
from jax.experimental.pallas import tpu_sc as plsc

F32 = jnp.float32
BF16 = jnp.bfloat16
I32 = jnp.int32

D_MODEL = 1024
GRID_W = 64
HEAD_DIM = 64
N_HEADS = 8
KV_HEADS = 2
ATT_WIDTH = N_HEADS * HEAD_DIM
ATT_SCALE = HEAD_DIM ** -0.5
LOG2_E = 1.4426950408889634
ROPE_AXIS_DIM = HEAD_DIM // 2
ROPE_THETA = 10000.0
GLA_HEADS = 4
GLA_DK = 64
GLA_DV = 128
GLA_WIDTH = GLA_HEADS * GLA_DV
GLA_KW = GLA_HEADS * GLA_DK
GLA_GATE_RANK = 16
GLA_TAU = 16.0
N_EXPERTS = 256
TOP_K = 8
EXPERT_FF = 256
SHARED_FF = 256
ROUTED_SCALE = 2.5
DEPTH = 1
ALPHA = (2.0 * DEPTH) ** 0.25
EPS = 1e-6

LANES = 128
SUBLANES = 8
VMEM_BYTES = 64 * 1024 * 1024
VMEM_LIMIT = VMEM_BYTES - 8 * 1024 * 1024
PACK_CHUNKS = D_MODEL // (2 * LANES)
PACK_CHUNKS_LOG2 = PACK_CHUNKS.bit_length() - 1
HIGH_HALF = 0xFFFF0000
U32 = jnp.uint32

TOK_TILE = 512
ATT_TQ = 256
GLA_CHUNK = 128
GLA_LEVELS = ((32, 128), (8, 32), (2, 8), (1, 2))
MOE_ROWS = 256
MOE_VMEM_LIMIT = VMEM_BYTES - 2 * 1024 * 1024
TOP_K_LOG2 = TOP_K.bit_length() - 1
SRC_GROUP = 4
COMB_TILE = 256
COMB_UNROLL = 8
HIGHEST = lax.Precision.HIGHEST


def _cparams(n_axes):
    return pltpu.CompilerParams(dimension_semantics=("arbitrary",) * n_axes,
                                vmem_limit_bytes=VMEM_LIMIT)


def _silu(x):
    return x * jax.nn.sigmoid(x)


def _log_sigmoid(x):
    return jnp.minimum(x, 0.0) - jnp.log(1.0 + jnp.exp(-jnp.abs(x)))


def _dot_split(a, b):
    a_hi = a.astype(BF16)
    b_hi = b.astype(BF16)
    a_lo = (a - a_hi.astype(F32)).astype(BF16)
    b_lo = (b - b_hi.astype(F32)).astype(BF16)
    dot = functools.partial(jnp.dot, preferred_element_type=F32)
    return dot(a_hi, b_hi) + dot(a_lo, b_hi) + dot(a_hi, b_lo)


def _layer_norm(z, g, b):
    mu = jnp.mean(z, axis=-1, keepdims=True)
    zc = z - mu
    var = jnp.mean(zc * zc, axis=-1, keepdims=True)
    return zc * lax.rsqrt(var + EPS) * g + b


def _mod_kernel(c_ref, w_ref, b_ref, o_ref):
    s = _silu(c_ref[...]).astype(BF16)
    o_ref[...] = jnp.dot(s, w_ref[...].astype(BF16), preferred_element_type=F32) + b_ref[...]


def _modulation(c_rows, w_ada, b_ada):
    n_cols = w_ada.shape[1]
    tn = 512
    return pl.pallas_call(
        _mod_kernel,
        grid=(n_cols // tn,),
        in_specs=[pl.BlockSpec((SUBLANES, D_MODEL), lambda j: (0, 0)),
                  pl.BlockSpec((D_MODEL, tn), lambda j: (0, j)),
                  pl.BlockSpec((1, tn), lambda j: (0, j))],
        out_specs=pl.BlockSpec((SUBLANES, tn), lambda j: (0, j)),
        out_shape=jax.ShapeDtypeStruct((SUBLANES, n_cols), F32),
        compiler_params=_cparams(1),
        name="modulation",
    )(c_rows, w_ada, b_ada)


_C_K = 0
_C_V = _C_K + 2 * LANES
_C_GQ = _C_V + KV_HEADS * HEAD_DIM
_C_GV = _C_GQ + GLA_KW
_C_GG = _C_GV + GLA_WIDTH
_C_RA = _C_GG + GLA_WIDTH
_C_END = _C_RA + LANES
_R_Q = 0
_R_V = _R_Q + ATT_WIDTH
_R_GK = _R_V + KV_HEADS * HEAD_DIM
_R_RA = _R_GK + GLA_KW
_R_END = _R_RA + 2 * GLA_GATE_RANK


def _inproj_kernel(xc_ref, xl_ref, mod_ref, w_ref, wt_ref, qn_ref, kn_ref, cos_ref, sa_ref, sb_ref,
                   cost_ref, sint_ref, seg_ref, wa_ref, ba_ref, wat_ref, bat_ref,
                   qt_ref, k_ref, vt_ref, k32_ref, v32_ref, gq_ref, gv_ref, gg_ref,
                   la_ref, gkt_ref, lat_ref, *, n_ctx_tiles):
    i = pl.program_id(0)
    m = mod_ref[...]
    shift1 = m[:, 0:D_MODEL]
    scale1 = m[:, D_MODEL:2 * D_MODEL]
    x = jnp.where(i < n_ctx_tiles, xc_ref[...], xl_ref[...])
    u = (x * (1.0 + scale1) + shift1).astype(BF16)

    cos = cos_ref[...]
    sin_a = sa_ref[...]
    sin_b = sb_ref[...]
    seg = seg_ref[...]
    lane = lax.broadcasted_iota(I32, (u.shape[0], LANES), 1)
    low = lane < HEAD_DIM

    def proj(c0, c1):
        return jnp.dot(u, w_ref[:, c0:c1], preferred_element_type=F32)

    def head_norm(blk, gain):
        ss = jnp.dot((blk * blk).astype(BF16), seg, preferred_element_type=F32) * (1.0 / HEAD_DIM)
        return blk * lax.rsqrt(ss + EPS) * gain

    def rope(blk):
        return (blk * cos + pltpu.roll(blk, LANES - ROPE_AXIS_DIM // 2, 1) * sin_a
                + pltpu.roll(blk, ROPE_AXIS_DIM // 2, 1) * sin_b)

    pk = proj(_C_K, _C_V)
    kn = [head_norm(pk[:, j * LANES:(j + 1) * LANES], kn_ref[...]) for j in range(KV_HEADS)]
    for j in range(KV_HEADS):
        k_ref[:, j * LANES:(j + 1) * LANES] = rope(kn[j]).astype(BF16)

    @pl.when(i < n_ctx_tiles)
    def _():
        k32_ref[...] = jnp.where(low, kn[0], kn[1])
        v32_ref[...] = proj(_C_V, _C_GQ)

    gq_ref[...] = proj(_C_GQ, _C_GV) * (GLA_DK ** -0.5)
    gv_ref[...] = proj(_C_GV, _C_GG).astype(BF16)
    gg_ref[...] = proj(_C_GG, _C_RA).astype(BF16)

    ra = proj(_C_RA, _C_END)
    pre = _dot_split(ra, wa_ref[...]) + ba_ref[...]
    la_ref[...] = _log_sigmoid(pre) * (1.0 / GLA_TAU)

    pt = lax.dot_general(wt_ref[...], u, (((1,), (1,)), ((), ())), preferred_element_type=F32)
    cos_t = cost_ref[...]
    sin_t = sint_ref[...]
    quarter = ROPE_AXIS_DIM // 2
    for h in range(N_HEADS):
        blk = pt[_R_Q + h * HEAD_DIM:_R_Q + (h + 1) * HEAD_DIM, :]
        ms = jnp.mean(blk * blk, axis=0, keepdims=True)
        qn = blk * lax.rsqrt(ms + EPS) * qn_ref[...]
        rot = jnp.concatenate([-qn[quarter:2 * quarter], qn[0:quarter],
                               -qn[3 * quarter:4 * quarter], qn[2 * quarter:3 * quarter]], axis=0)
        qt_ref[h * HEAD_DIM:(h + 1) * HEAD_DIM, :] = (
            (qn * cos_t + rot * sin_t) * (ATT_SCALE * LOG2_E)).astype(BF16)
    vt_ref[...] = pt[_R_V:_R_GK, :].astype(BF16)
    gkt_ref[...] = pt[_R_GK:_R_RA, :]
    rat = pt[_R_RA:_R_END, :]
    pre_t = _dot_split(wat_ref[...], rat) + bat_ref[...]
    lat_ref[...] = _log_sigmoid(pre_t) * (1.0 / GLA_TAU)


def _in_projection(x_c, x_l, mod3, w_tok, w_tr, qn, kn, cos_t, sa_t, sb_t, cos_tr, sin_tr, seg, wa, ba,
                   wat, bat, n_seq_tiles):
    n_ctx = x_c.shape[0]
    n = n_ctx + x_l.shape[0]
    tb = TOK_TILE
    n_ctx_tiles = n_ctx // tb
    n_tiles = n // tb
    n_rope_blocks = cos_t.shape[0] // tb - 1

    def mod_idx(i):
        return (jnp.where(i < n_ctx_tiles, 0, 1 + (i - n_ctx_tiles) // n_seq_tiles), 0, 0)

    def rope_blk(i):
        return jnp.where(i < n_ctx_tiles, n_rope_blocks, (i - n_ctx_tiles) % n_seq_tiles)

    def rope_idx(i):
        return (rope_blk(i), 0)

    def ctx_idx(i):
        return (jnp.minimum(i, n_ctx_tiles - 1), 0)

    tok = lambda w: pl.BlockSpec((tb, w), lambda i: (i, 0))
    full = lambda a: pl.BlockSpec(a.shape, lambda i: (0,) * a.ndim)
    tr = lambda r: pl.BlockSpec((r, tb), lambda i: (0, i))
    rope_tr = pl.BlockSpec((HEAD_DIM, tb), lambda i: (0, rope_blk(i)))
    out_shapes = (
        jax.ShapeDtypeStruct((ATT_WIDTH, n), BF16),
        jax.ShapeDtypeStruct((n, 2 * LANES), BF16),
        jax.ShapeDtypeStruct((KV_HEADS * HEAD_DIM, n), BF16),
        jax.ShapeDtypeStruct((n_ctx, LANES), F32),
        jax.ShapeDtypeStruct((n_ctx, LANES), F32),
        jax.ShapeDtypeStruct((n, GLA_KW), F32),
        jax.ShapeDtypeStruct((n, GLA_WIDTH), BF16),
        jax.ShapeDtypeStruct((n, GLA_WIDTH), BF16),
        jax.ShapeDtypeStruct((n, 2 * GLA_KW), F32),
        jax.ShapeDtypeStruct((GLA_KW, n), F32),
        jax.ShapeDtypeStruct((2 * GLA_KW, n), F32),
    )
    out_specs = (tr(ATT_WIDTH), tok(2 * LANES), tr(KV_HEADS * HEAD_DIM),
                 pl.BlockSpec((tb, LANES), ctx_idx), pl.BlockSpec((tb, LANES), ctx_idx),
                 tok(GLA_KW), tok(GLA_WIDTH), tok(GLA_WIDTH), tok(2 * GLA_KW),
                 tr(GLA_KW), tr(2 * GLA_KW))
    in_specs = [pl.BlockSpec((tb, D_MODEL), ctx_idx),
                pl.BlockSpec((tb, D_MODEL), lambda i: (jnp.maximum(i - n_ctx_tiles, 0), 0)),
                pl.BlockSpec((None, 1, mod3.shape[2]), mod_idx),
                full(w_tok), full(w_tr), full(qn), full(kn),
                pl.BlockSpec((tb, LANES), rope_idx), pl.BlockSpec((tb, LANES), rope_idx),
                pl.BlockSpec((tb, LANES), rope_idx), rope_tr, rope_tr,
                full(seg), full(wa), full(ba), full(wat), full(bat)]
    return pl.pallas_call(
        functools.partial(_inproj_kernel, n_ctx_tiles=n_ctx_tiles),
        grid=(n_tiles,), in_specs=in_specs, out_specs=out_specs, out_shape=out_shapes,
        compiler_params=_cparams(1), name="in_projection",
    )(x_c, x_l, mod3, w_tok, w_tr, qn, kn, cos_t, sa_t, sb_t, cos_tr, sin_tr, seg, wa, ba, wat, bat)


def _attention_kernel(*refs, n_kv_parts):
    qt_ref = refs[0]
    k_refs = refs[1:1 + n_kv_parts]
    vt_refs = refs[1 + n_kv_parts:1 + 2 * n_kv_parts]
    o_ref = refs[1 + 2 * n_kv_parts]
    tq = qt_ref.shape[1]
    group = N_HEADS // KV_HEADS
    for kv in range(KV_HEADS):
        heads = range(kv * group, (kv + 1) * group)
        q_grp = jnp.concatenate([qt_ref[h * HEAD_DIM:(h + 1) * HEAD_DIM, :] for h in heads], axis=1)
        rhs = jnp.concatenate([q_grp, jnp.zeros_like(q_grp)], axis=0)
        s = [jnp.dot(k[:, kv * LANES:(kv + 1) * LANES], rhs, preferred_element_type=F32)
             for k in k_refs]
        mx = functools.reduce(jnp.maximum, [jnp.max(x, axis=0, keepdims=True) for x in s])
        pr = [jnp.exp2(x - mx) for x in s]
        den = functools.reduce(jnp.add, [jnp.sum(x, axis=0, keepdims=True) for x in pr])
        acc = functools.reduce(jnp.add, [
            jnp.dot(vt[kv * HEAD_DIM:(kv + 1) * HEAD_DIM, :], x.astype(BF16),
                    preferred_element_type=F32) for x, vt in zip(pr, vt_refs)])
        out = (acc / den).astype(BF16)
        for j, h in enumerate(heads):
            o_ref[h * HEAD_DIM:(h + 1) * HEAD_DIM, :] = out[:, j * tq:(j + 1) * tq]


def _attention(qt, k, vt, extra_kv, row0, n_batch, seq):
    tq = ATT_TQ
    n_q = seq // tq
    q_blk0 = row0 // tq
    kv_blk0 = row0 // seq
    in_specs = [pl.BlockSpec((ATT_WIDTH, tq), lambda b, i: (0, q_blk0 + b * n_q + i))]
    k_spec = pl.BlockSpec((seq, 2 * LANES), lambda b, i: (kv_blk0 + b, 0))
    vt_spec = pl.BlockSpec((KV_HEADS * HEAD_DIM, seq), lambda b, i: (0, kv_blk0 + b))
    args_k, args_v, specs_k, specs_v = [k], [vt], [k_spec], [vt_spec]
    if extra_kv is not None:
        ck, cvt = extra_kv
        args_k.append(ck)
        args_v.append(cvt)
        specs_k.append(pl.BlockSpec((None, ck.shape[1], 2 * LANES), lambda b, i: (b, 0, 0)))
        specs_v.append(pl.BlockSpec((None, KV_HEADS * HEAD_DIM, cvt.shape[2]), lambda b, i: (b, 0, 0)))
    return pl.pallas_call(
        functools.partial(_attention_kernel, n_kv_parts=len(args_k)),
        grid=(n_batch, n_q),
        in_specs=in_specs + specs_k + specs_v,
        out_specs=pl.BlockSpec((ATT_WIDTH, tq), lambda b, i: (0, b * n_q + i)),
        out_shape=jax.ShapeDtypeStruct((ATT_WIDTH, n_batch * seq), BF16),
        compiler_params=_cparams(2), name="attention",
    )(qt, *args_k, *args_v)


def _gla_constants():
    c = GLA_CHUNK
    idx = np.arange(c)
    q_mats, k_mats, masks, levels_of = [], [], [], []
    for li, (s, p) in enumerate(GLA_LEVELS):
        start = (idx // s) * s
        end = start + s - 1
        k_mats.append(((idx[None, :] > idx[:, None]) & (idx[None, :] <= end[:, None])))
        for d in range(p // s - 1):
            lo = np.maximum(start - d * s, 0)
            q_mats.append((idx[None, :] >= lo[:, None]) & (idx[None, :] <= idx[:, None]))
            masks.append((idx[:, None] // p == idx[None, :] // p)
                         & (idx[:, None] // s - idx[None, :] // s - 1 == d))
            levels_of.append(li)
    masks.append(np.eye(c, dtype=bool))
    levels_of.append(len(GLA_LEVELS) - 1)
    q_mats.append(idx[None, :] <= idx[:, None])
    k_mats = k_mats[:-1]
    k_mats.append(idx[None, :] > idx[:, None])
    k_mats.append(np.ones((c, c), bool))
    out = {}
    for name, flip in (("f", False), ("b", True)):
        f = (lambda a: a[::-1, ::-1]) if flip else (lambda a: a)
        lq = np.concatenate([f(a) for a in q_mats], axis=0).astype(np.float32)
        lkt = np.concatenate([f(a).T for a in k_mats], axis=1).astype(np.float32)
        mk = np.stack([np.tile(f(a), (1, GLA_HEADS)) for a in masks]).astype(np.float32)
        out[name] = (np.concatenate([lq, lq], axis=1), np.concatenate([lkt, lkt], axis=0), mk)
    return out, tuple(levels_of)


def _gla_direction(q, g, gkt, gt, v, lq2, lkt2, masks_ref, bd, vbd, s_ref, levels_of):
    c = GLA_CHUNK
    n_var = len(levels_of)
    n_lev = len(GLA_LEVELS)
    g_hi = g.astype(BF16)
    g_lo = (g - g_hi.astype(F32)).astype(BF16)
    fq = jnp.dot(lq2, jnp.concatenate([g_hi, g_lo], axis=0), preferred_element_type=F32)
    gt_hi = gt.astype(BF16)
    gt_lo = (gt - gt_hi.astype(F32)).astype(BF16)
    fk = jnp.dot(jnp.concatenate([gt_hi, gt_lo], axis=1), lkt2, preferred_element_type=F32)

    def key_factor(f):
        return gkt * jnp.exp(fk[:, f * c:(f + 1) * c])

    q_var = [(q * jnp.exp(fq[vi * c:(vi + 1) * c, :])).astype(BF16) for vi in range(n_var - 1)]
    q_var.append(q.astype(BF16))
    a = jnp.zeros((c, GLA_HEADS * c), F32)
    for li in range(n_lev):
        kt = (key_factor(li) if li < n_lev - 1 else gkt).astype(BF16)
        xt = jnp.concatenate([kt] * GLA_HEADS, axis=1) * bd
        vis = [vi for vi in range(n_var) if levels_of[vi] == li]
        res = jnp.dot(jnp.concatenate([q_var[vi] for vi in vis], axis=0), xt,
                      preferred_element_type=F32)
        for r, vi in enumerate(vis):
            a = a + masks_ref[vi] * res[r * c:(r + 1) * c, :]
    q_in = (q * jnp.exp(fq[(n_var - 1) * c:n_var * c, :])).astype(BF16)
    state = s_ref[...]
    v_bd = jnp.concatenate([v] * GLA_HEADS, axis=0) * vbd
    o = (jnp.dot(q_in, state.astype(BF16), preferred_element_type=F32)
         + jnp.dot(a.astype(BF16), v_bd, preferred_element_type=F32))
    k_out = key_factor(n_lev - 1).astype(BF16)
    e_tot = jnp.exp(fk[:, n_lev * c:(n_lev + 1) * c])
    upd = jnp.dot(k_out, v, preferred_element_type=F32)
    s_ref[...] = (state * jnp.concatenate([e_tot] * (GLA_WIDTH // c), axis=1)
                  + upd * bd.astype(F32))
    return o


def _gla_kernel(gq_f, la_f, gkt_f, lat_f, gv_f, gq_b, la_b, gkt_b, lat_b, gv_b,
                s0f_ref, s0b_ref, lq2f, lkt2f, mkf, lq2b, lkt2b, mkb, bd_ref, vbd_ref,
                of_ref, ob_ref, sf_ref, sb_ref, st_f, st_b, *, levels_of):
    n = pl.program_id(1)

    @pl.when(n == 0)
    def _():
        st_f[...] = jnp.zeros_like(st_f)
        st_b[...] = jnp.zeros_like(st_b)
        for h in range(GLA_HEADS):
            rows = slice(h * GLA_DK, (h + 1) * GLA_DK)
            cols = slice(h * GLA_DV, (h + 1) * GLA_DV)
            st_f[rows, cols] = s0f_ref[h]
            st_b[rows, cols] = s0b_ref[h]

    bd = bd_ref[...]
    vbd = vbd_ref[...]
    of_ref[...] = _gla_direction(gq_f[...], la_f[...], gkt_f[...], lat_f[...], gv_f[...],
                                 lq2f[...], lkt2f[...], mkf, bd, vbd, st_f, levels_of)
    ob_ref[...] = _gla_direction(gq_b[...], la_b[...], gkt_b[...], lat_b[...], gv_b[...],
                                 lq2b[...], lkt2b[...], mkb, bd, vbd, st_b, levels_of)

    @pl.when(n == pl.num_programs(1) - 1)
    def _():
        for h in range(GLA_HEADS):
            rows = slice(h * GLA_DK, (h + 1) * GLA_DK)
            cols = slice(h * GLA_DV, (h + 1) * GLA_DV)
            sf_ref[h] = st_f[rows, cols]
            sb_ref[h] = st_b[rows, cols]


def _gla(gq, la, gkt, lat, gv, s0f, s0b, consts, row0, n_batch, seq):
    (cf, cb), levels_of, bd, vbd = consts
    c = GLA_CHUNK
    nc = seq // c
    blk0 = row0 // c
    n_la_blocks_b = 1
    fwd = lambda b, n: blk0 + b * nc + n
    bwd = lambda b, n: blk0 + b * nc + (nc - 1 - n)

    def tok(w, which, col=0):
        return pl.BlockSpec((c, w), lambda b, n: (which(b, n), col))

    def tr(r, which, row=0):
        return pl.BlockSpec((r, c), lambda b, n: (row, which(b, n)))

    full = lambda a: pl.BlockSpec(a.shape, lambda b, n: (0,) * a.ndim)
    st_spec = pl.BlockSpec((None, GLA_HEADS, GLA_DK, GLA_DV), lambda b, n: (b, 0, 0, 0))
    in_specs = [tok(GLA_KW, fwd), tok(GLA_KW, fwd, 0), tr(GLA_KW, fwd), tr(GLA_KW, fwd, 0),
                tok(GLA_WIDTH, fwd),
                tok(GLA_KW, bwd), tok(GLA_KW, bwd, n_la_blocks_b), tr(GLA_KW, bwd),
                tr(GLA_KW, bwd, 1), tok(GLA_WIDTH, bwd),
                st_spec, st_spec,
                full(cf[0]), full(cf[1]), full(cf[2]), full(cb[0]), full(cb[1]), full(cb[2]),
                full(bd), full(vbd)]
    out_specs = (pl.BlockSpec((c, GLA_WIDTH), lambda b, n: (b * nc + n, 0)),
                 pl.BlockSpec((c, GLA_WIDTH), lambda b, n: (b * nc + (nc - 1 - n), 0)),
                 st_spec, st_spec)
    out_shape = (jax.ShapeDtypeStruct((n_batch * seq, GLA_WIDTH), F32),
                 jax.ShapeDtypeStruct((n_batch * seq, GLA_WIDTH), F32),
                 jax.ShapeDtypeStruct((n_batch, GLA_HEADS, GLA_DK, GLA_DV), F32),
                 jax.ShapeDtypeStruct((n_batch, GLA_HEADS, GLA_DK, GLA_DV), F32))
    return pl.pallas_call(
        functools.partial(_gla_kernel, levels_of=levels_of),
        grid=(n_batch, nc), in_specs=in_specs, out_specs=out_specs, out_shape=out_shape,
        scratch_shapes=[pltpu.VMEM((GLA_KW, GLA_WIDTH), F32), pltpu.VMEM((GLA_KW, GLA_WIDTH), F32)],
        compiler_params=_cparams(2), name="gla",
    )(gq, la, gkt, lat, gv, gq, la, gkt, lat, gv, s0f, s0b,
      cf[0], cf[1], cf[2], cb[0], cb[1], cb[2], bd, vbd)


def _outproj_kernel(attc_ref, attl_ref, ofc_ref, ofl_ref, obc_ref, obl_ref, gg_ref, xc_ref, xl_ref,
                    mod_ref, wo_ref, gn_ref, l1g_ref, l1b_ref, wrt_ref, sw13_ref, sw2_ref,
                    base_ref, u2_ref, lg_ref, *, n_ctx_tiles):
    is_ctx = pl.program_id(0) < n_ctx_tiles
    pick = lambda a_ref, b_ref: jnp.where(is_ctx, a_ref[...], b_ref[...])
    m = mod_ref[...]
    gate1 = m[:, 2 * D_MODEL:3 * D_MODEL]
    shift2 = m[:, 3 * D_MODEL:4 * D_MODEL]
    scale2 = m[:, 4 * D_MODEL:5 * D_MODEL]
    gate2 = m[:, 5 * D_MODEL:6 * D_MODEL]
    og = pick(ofc_ref, ofl_ref) + pick(obc_ref, obl_ref)
    gg = gg_ref[...].astype(F32)
    parts = []
    for h in range(GLA_HEADS):
        blk = og[:, h * GLA_DV:(h + 1) * GLA_DV]
        ms = jnp.mean(blk * blk, axis=-1, keepdims=True)
        nb = blk * lax.rsqrt(ms + EPS) * gn_ref[...]
        parts.append((nb * _silu(gg[:, h * GLA_DV:(h + 1) * GLA_DV])).astype(BF16))
    att_t = pick(attc_ref, attl_ref)
    hmix = (lax.dot_general(att_t, wo_ref[0:ATT_WIDTH, :], (((0,), (0,)), ((), ())),
                            preferred_element_type=F32)
            + jnp.dot(jnp.concatenate(parts, axis=1), wo_ref[ATT_WIDTH:, :],
                      preferred_element_type=F32))
    x1 = _layer_norm(ALPHA * pick(xc_ref, xl_ref) + gate1 * hmix, l1g_ref[...], l1b_ref[...])
    u2 = x1 * (1.0 + scale2) + shift2
    u2b = u2.astype(BF16)
    lg_ref[...] = lax.dot_general(wrt_ref[...], u2b, (((1,), (1,)), ((), ())),
                                  preferred_element_type=F32)
    ab = jnp.dot(u2b, sw13_ref[...], preferred_element_type=F32)
    hid = (_silu(ab[:, 0:SHARED_FF]) * ab[:, SHARED_FF:2 * SHARED_FF]).astype(BF16)
    shared = jnp.dot(hid, sw2_ref[...], preferred_element_type=F32)
    base_ref[...] = ALPHA * x1 + gate2 * shared
    _pack_rows(u2_ref, u2)


def _out_projection(att_c, att_l, of_c, of_l, ob_c, ob_l, gg, x_c, x_l, mod3, wo, gn, l1g, l1b, wrt,
                    sw13, sw2, n_seq_tiles):
    n_ctx = x_c.shape[0]
    n = n_ctx + x_l.shape[0]
    tb = TOK_TILE
    n_ctx_tiles = n_ctx // tb

    def mod_idx(i):
        return (jnp.where(i < n_ctx_tiles, 0, 1 + (i - n_ctx_tiles) // n_seq_tiles), 0, 0)

    ctx_blk = lambda i: jnp.minimum(i, n_ctx_tiles - 1)
    lat_blk = lambda i: jnp.maximum(i - n_ctx_tiles, 0)
    tok = lambda w: pl.BlockSpec((tb, w), lambda i: (i, 0))
    tok_c = lambda w: pl.BlockSpec((tb, w), lambda i: (ctx_blk(i), 0))
    tok_l = lambda w: pl.BlockSpec((tb, w), lambda i: (lat_blk(i), 0))
    full = lambda a: pl.BlockSpec(a.shape, lambda i: (0,) * a.ndim)
    return pl.pallas_call(
        functools.partial(_outproj_kernel, n_ctx_tiles=n_ctx_tiles),
        grid=(n // tb,),
        in_specs=[pl.BlockSpec((ATT_WIDTH, tb), lambda i: (0, ctx_blk(i))),
                  pl.BlockSpec((ATT_WIDTH, tb), lambda i: (0, lat_blk(i))),
                  tok_c(GLA_WIDTH), tok_l(GLA_WIDTH), tok_c(GLA_WIDTH), tok_l(GLA_WIDTH),
                  tok(GLA_WIDTH), tok_c(D_MODEL), tok_l(D_MODEL),
                  pl.BlockSpec((None, 1, mod3.shape[2]), mod_idx),
                  full(wo), full(gn), full(l1g), full(l1b), full(wrt), full(sw13), full(sw2)],
        out_specs=(tok(D_MODEL),
                   pl.BlockSpec((tb * PACK_CHUNKS, LANES), lambda i: (i, 0)),
                   pl.BlockSpec((N_EXPERTS, tb), lambda i: (0, i))),
        out_shape=(jax.ShapeDtypeStruct((n, D_MODEL), F32),
                   jax.ShapeDtypeStruct((n * PACK_CHUNKS, LANES), U32),
                   jax.ShapeDtypeStruct((N_EXPERTS, n), F32)),
        compiler_params=_cparams(1), name="out_projection",
    )(att_c, att_l, of_c, of_l, ob_c, ob_l, gg, x_c, x_l, mod3, wo, gn, l1g, l1b, wrt, sw13, sw2)


def _route_kernel(lg_ref, bias_ref, upper_ref, idx_ref, w_ref, pos_ref, cnt_ref, run_ref):
    i = pl.program_id(0)

    @pl.when(i == 0)
    def _():
        run_ref[...] = jnp.zeros_like(run_ref)

    s = jax.nn.sigmoid(lg_ref[...])
    work = s + bias_ref[...]
    rows = lax.broadcasted_iota(I32, s.shape, 0).astype(F32)
    sel = jnp.zeros(s.shape, F32)
    idxs, vals = [], []
    for _ in range(TOP_K):
        mx = jnp.max(work, axis=0, keepdims=True)
        idx = jnp.min(jnp.where(work == mx, rows, float(N_EXPERTS)), axis=0, keepdims=True)
        hit = rows == idx
        vals.append(jnp.sum(jnp.where(hit, s, 0.0), axis=0, keepdims=True))
        idxs.append(idx)
        sel = jnp.where(hit, 1.0, sel)
        work = jnp.where(hit, -jnp.inf, work)
    den = functools.reduce(jnp.add, vals)
    rank = jnp.dot(sel.astype(BF16), upper_ref[...], preferred_element_type=F32) + run_ref[:, 0:1]
    for k in range(TOP_K):
        idx_ref[k:k + 1, :] = idxs[k].astype(I32)
        w_ref[k:k + 1, :] = vals[k] / den * ROUTED_SCALE
        pos_ref[k:k + 1, :] = jnp.sum(jnp.where(rows == idxs[k], rank, 0.0), axis=0,
                                      keepdims=True).astype(I32)
    run_ref[...] = run_ref[...] + jnp.sum(sel, axis=1, keepdims=True)
    cnt_ref[...] = run_ref[...]


def _route(logits_t, bias_col, upper):
    n = logits_t.shape[1]
    tt = TOK_TILE
    row = lambda dt: jax.ShapeDtypeStruct((TOP_K, n), dt)
    blk = pl.BlockSpec((TOP_K, tt), lambda i: (0, i))
    return pl.pallas_call(
        _route_kernel,
        grid=(n // tt,),
        in_specs=[pl.BlockSpec((N_EXPERTS, tt), lambda i: (0, i)),
                  pl.BlockSpec((N_EXPERTS, 1), lambda i: (0, 0)),
                  pl.BlockSpec((tt, tt), lambda i: (0, 0))],
        out_specs=(blk, blk, blk, pl.BlockSpec((N_EXPERTS, LANES), lambda i: (0, 0))),
        out_shape=(row(I32), row(F32), row(I32), jax.ShapeDtypeStruct((N_EXPERTS, LANES), F32)),
        scratch_shapes=[pltpu.VMEM((N_EXPERTS, LANES), F32)],
        compiler_params=_cparams(1), name="route",
    )(logits_t, bias_col, upper)


def _dest_kernel(cnt_ref, lower_ref, idx_ref, pos_ref, dest_ref, bexp_ref, bval_ref, nused_ref):
    cnt = cnt_ref[...]
    nblk = jnp.floor((cnt + (MOE_ROWS - 1)) * (1.0 / MOE_ROWS))
    bstart = jnp.dot(lower_ref[...], nblk, precision=HIGHEST, preferred_element_type=F32)
    bend = bstart + nblk
    pstart = bstart[:, 0:1] * MOE_ROWS
    rows = lax.broadcasted_iota(I32, (N_EXPERTS, idx_ref.shape[1]), 0)
    for k in range(TOP_K):
        hit = rows == idx_ref[k:k + 1, :]
        dest_ref[k:k + 1, :] = (jnp.sum(jnp.where(hit, pstart, 0.0), axis=0, keepdims=True)
                                .astype(I32) + pos_ref[k:k + 1, :])

    @pl.when(pl.program_id(0) == 0)
    def _():
        nb = bexp_ref.shape[1]
        bid = lax.broadcasted_iota(I32, (N_EXPERTS, nb), 1).astype(F32)
        inside = jnp.logical_and(bid >= bstart[:, 0:1], bid < bend[:, 0:1])
        erow = lax.broadcasted_iota(I32, (N_EXPERTS, nb), 0).astype(F32)
        bexp_ref[...] = jnp.sum(jnp.where(inside, erow, 0.0), axis=0, keepdims=True).astype(I32)
        valid = jnp.clip(cnt[:, 0:1] - (bid - bstart[:, 0:1]) * MOE_ROWS, 0.0, float(MOE_ROWS))
        bval_ref[...] = jnp.sum(jnp.where(inside, valid, 0.0), axis=0, keepdims=True).astype(I32)
        nused_ref[...] = jnp.max(bend, axis=0, keepdims=True).astype(I32)


def _destinations(counts, lower, idx_t, pos_t, n_blocks_pad):
    n = idx_t.shape[1]
    tt = TOK_TILE
    blk = pl.BlockSpec((TOP_K, tt), lambda i: (0, i))
    one = lambda w: pl.BlockSpec((1, w), lambda i: (0, 0))
    return pl.pallas_call(
        _dest_kernel,
        grid=(n // tt,),
        in_specs=[pl.BlockSpec((N_EXPERTS, LANES), lambda i: (0, 0)),
                  pl.BlockSpec((N_EXPERTS, N_EXPERTS), lambda i: (0, 0)), blk, blk],
        out_specs=(blk, one(n_blocks_pad), one(n_blocks_pad), one(LANES)),
        out_shape=(jax.ShapeDtypeStruct((TOP_K, n), I32),
                   jax.ShapeDtypeStruct((1, n_blocks_pad), I32),
                   jax.ShapeDtypeStruct((1, n_blocks_pad), I32),
                   jax.ShapeDtypeStruct((1, LANES), I32)),
        compiler_params=_cparams(1), name="destinations",
    )(counts, lower, idx_t, pos_t)


SC_WINDOW = 128
SC_WINDOWS_PER_STEP = 8


def _invert_rows(dest, ids, n_rows):
    m = dest.size
    mesh = plsc.VectorSubcoreMesh(core_axis_name="core", subcore_axis_name="subcore")

    @functools.partial(pl.kernel, out_type=jax.ShapeDtypeStruct((n_rows,), I32), mesh=mesh,
                       scratch_types=[])
    def invert(val_hbm, idx_hbm, out_hbm):
        def body(val_vmem, idx_vmem):
            for j in range(SC_WINDOWS_PER_STEP):
                pltpu.sync_copy(val_vmem.at[j], out_hbm.at[idx_vmem.at[j]])

        blk = pl.BlockSpec((SC_WINDOWS_PER_STEP, SC_WINDOW), lambda i: (i, 0))
        pltpu.emit_pipeline(
            body, grid=(m // (SC_WINDOW * SC_WINDOWS_PER_STEP),),
            in_specs=[blk, blk], out_specs=[], core_axis_name=("core", "subcore"),
            dimension_semantics=(pltpu.PARALLEL,),
        )(val_hbm, idx_hbm)

    shape = (m // SC_WINDOW, SC_WINDOW)
    return invert(jnp.asarray(ids.reshape(shape), I32), dest.reshape(shape))


def _pack_rows(ref, x, row0=0):
    bits = pltpu.bitcast(x.astype(BF16).astype(F32), U32)
    for s in range(PACK_CHUNKS):
        lo = bits[:, (2 * s) * LANES:(2 * s + 1) * LANES] >> 16
        hi = bits[:, (2 * s + 1) * LANES:(2 * s + 2) * LANES] & jnp.uint32(HIGH_HALF)
        ref[pl.ds(row0 + s, x.shape[0], stride=PACK_CHUNKS), :] = lo | hi


def _unpack_rows(ref, n_rows, row0=0):
    parts = []
    for s in range(PACK_CHUNKS):
        w = ref[pl.ds(row0 + s, n_rows, stride=PACK_CHUNKS), :]
        parts.append(pltpu.bitcast(w << 16, F32))
        parts.append(pltpu.bitcast(w & jnp.uint32(HIGH_HALF), F32))
    return jnp.concatenate(parts, axis=1).astype(BF16)


def _moe_kernel(bexp_ref, bval_ref, nused_ref, u2p_hbm, gat_hbm, dst_hbm, w1_hbm, w3_hbm, w2_hbm,
                yt_hbm, u2p_vmem, w1_f, w3_f, w2_f, w13_s, w2_s, xbuf, ybuf, gat_smem, dst_smem,
                sem_in, sem_src, sem_w, sem_out):
    n_used = nused_ref[0]
    br = MOE_ROWS
    grp = SRC_GROUP * br

    def src_copies(g):
        window = pl.ds(g * grp, grp)
        ring = pl.ds(lax.rem(g, 2) * grp, grp)
        return (pltpu.make_async_copy(gat_hbm.at[window], gat_smem.at[ring], sem_src),
                pltpu.make_async_copy(dst_hbm.at[window], dst_smem.at[ring], sem_src))

    def out_wait(slot):
        pltpu.make_async_copy(ybuf.at[pl.ds(slot * br * PACK_CHUNKS, br * PACK_CHUNKS)],
                              yt_hbm.at[pl.ds(0, br * PACK_CHUNKS)], sem_out.at[slot]).wait()

    def src_base(blk):
        return lax.rem(blk // SRC_GROUP, 2) * grp + lax.rem(blk, SRC_GROUP) * br

    def scatter_row(blk_slot, sbase, r, priority=0):
        pltpu.make_async_copy(
            ybuf.at[pl.ds(pl.multiple_of((blk_slot * br + r) * PACK_CHUNKS, PACK_CHUNKS), PACK_CHUNKS)],
            yt_hbm.at[pl.ds(pl.multiple_of(dst_smem[sbase + r], PACK_CHUNKS), PACK_CHUNKS)],
            sem_out.at[blk_slot]).start(priority=priority)

    def gather_row(xslot, sbase, r):
        dst = pl.multiple_of((xslot * br + r) * PACK_CHUNKS, PACK_CHUNKS)
        row = pl.multiple_of(gat_smem[sbase + r], PACK_CHUNKS)
        xbuf[pl.ds(dst, PACK_CHUNKS), :] = u2p_vmem[pl.ds(row, PACK_CHUNKS), :]

    def weight_copies(e, wslot):
        return [pltpu.make_async_copy(src.at[e], dst.at[wslot], sem_w.at[wslot])
                for src, dst in ((w1_hbm, w1_f), (w3_hbm, w3_f), (w2_hbm, w2_f))]

    cp = pltpu.make_async_copy(u2p_hbm, u2p_vmem, sem_in)
    cp.start()
    for scp in src_copies(0):
        scp.start()
    for wcp in weight_copies(bexp_ref[0], 0):
        wcp.start()
    ybuf[...] = jnp.zeros_like(ybuf)
    cp.wait()
    for scp in src_copies(0):
        scp.wait()
    lax.fori_loop(0, br, lambda r, c: (gather_row(0, 0, r), c)[1], 0)

    def block(b, wslot):
        g = b // SRC_GROUP
        phase = lax.rem(b, SRC_GROUP)

        more = (g + 1) * SRC_GROUP < n_used

        @pl.when(jnp.logical_and(phase == 1, more))
        def _():
            for scp in src_copies(g + 1):
                scp.start()

        @pl.when(jnp.logical_and(phase == SRC_GROUP - 1, more))
        def _():
            for scp in src_copies(g + 1):
                scp.wait()

        e = bexp_ref[b]
        prev = bexp_ref[jnp.maximum(b - 1, 0)]

        @pl.when(jnp.logical_or(b == 0, e != prev))
        def _():
            for wcp in weight_copies(e, wslot):
                wcp.wait()
            w13_s[:, 0:EXPERT_FF] = w1_f[wslot].astype(BF16)
            w13_s[:, EXPERT_FF:2 * EXPERT_FF] = w3_f[wslot].astype(BF16)
            w2_s[...] = w2_f[wslot].astype(BF16)
            nxt = lax.while_loop(
                lambda j: jnp.logical_and(j < n_used, bexp_ref[jnp.minimum(j, n_used - 1)] == e),
                lambda j: j + 1, b + 1)

            @pl.when(nxt < n_used)
            def _():
                for wcp in weight_copies(bexp_ref[nxt], 1 - wslot):
                    wcp.start()

        switch = jnp.logical_and(b + 1 < n_used, bexp_ref[b + 1] != e)

        valid = bval_ref[b]
        slot = lax.rem(b, 2)

        p_base = src_base(b)
        n_base = src_base(b + 1)
        for r in range(br):
            gather_row(1 - slot, n_base, r)
            scatter_row(1 - slot, p_base, r, priority=r % 2)

        x = _unpack_rows(xbuf, br, slot * (br * PACK_CHUNKS))
        rows = lax.broadcasted_iota(I32, x.shape, 0)
        x = jnp.where(rows < valid, x, jnp.zeros_like(x))
        ab = jnp.dot(x, w13_s[...], preferred_element_type=F32)
        hid = (_silu(ab[:, 0:EXPERT_FF]) * ab[:, EXPERT_FF:2 * EXPERT_FF]).astype(BF16)
        y = jnp.dot(hid, w2_s[...], preferred_element_type=F32)

        @pl.when(b >= 1)
        def _():
            out_wait(slot)

        _pack_rows(ybuf, y, slot * (br * PACK_CHUNKS))
        return jnp.where(switch, 1 - wslot, wslot)

    lax.fori_loop(0, n_used, block, 0)

    last = n_used - 1
    l_slot = lax.rem(last, 2)
    for scp in src_copies(n_used // SRC_GROUP):
        scp.start()
    for scp in src_copies(n_used // SRC_GROUP):
        scp.wait()
    l_base = src_base(n_used)
    lax.fori_loop(0, br, lambda r, c: (scatter_row(l_slot, l_base, r), c)[1], 0)
    out_wait(1 - l_slot)
    out_wait(l_slot)


def _row_tables_kernel(src_ref, srcp_ref, bvalp_ref, gat_ref, dst_ref, *, n_tokens):
    src = src_ref[...]
    row = lax.shift_right_logical(src, TOP_K_LOG2 - PACK_CHUNKS_LOG2)
    row = row & (int(jnp.iinfo(I32).max) - (PACK_CHUNKS - 1))
    gat_ref[...] = jnp.minimum(row, (n_tokens - 1) * PACK_CHUNKS)
    blk = lax.broadcasted_iota(I32, srcp_ref.shape, 0)
    lane = lax.broadcasted_iota(I32, srcp_ref.shape, 1)
    spare = n_tokens * TOP_K + ((blk + 1) & 1) * MOE_ROWS + lane
    dst_ref[...] = jnp.where(lane < bvalp_ref[...], srcp_ref[...], spare) * PACK_CHUNKS


def _row_tables(row_src, bval, n_blocks, n_tokens):
    n_tab = -(-(n_blocks + 1) // SRC_GROUP) * SRC_GROUP
    src2 = row_src.reshape(n_blocks, MOE_ROWS)
    pad = ((1, n_tab - n_blocks - 1), (0, 0))
    src_tab = jnp.pad(src2, ((0, n_tab - n_blocks), (0, 0)))
    srcp = jnp.pad(src2, pad)
    bvalp = jnp.pad(bval.reshape(-1)[:n_blocks, None], pad)
    full = lambda a: pl.BlockSpec(a.shape, lambda i: (0,) * a.ndim)
    tab = jax.ShapeDtypeStruct((n_tab, MOE_ROWS), I32)
    gat, dst = pl.pallas_call(
        functools.partial(_row_tables_kernel, n_tokens=n_tokens),
        grid=(1,), in_specs=[full(src_tab), full(srcp), full(bvalp)],
        out_specs=(full(src_tab), full(src_tab)), out_shape=(tab, tab),
        compiler_params=_cparams(1), name="row_tables",
    )(src_tab, srcp, bvalp)
    return gat.reshape(-1), dst.reshape(-1)


def _moe_experts(bexp, bval, nused, u2p, gat, dst, w1, w3, w2, n_tokens):
    br = MOE_ROWS
    any_spec = pl.BlockSpec(memory_space=pl.ANY)
    grid_spec = pltpu.PrefetchScalarGridSpec(
        num_scalar_prefetch=3, grid=(1,),
        in_specs=[any_spec] * 6,
        out_specs=any_spec,
        scratch_shapes=[pltpu.VMEM(u2p.shape, U32),
                        pltpu.VMEM((2, D_MODEL, EXPERT_FF), F32),
                        pltpu.VMEM((2, D_MODEL, EXPERT_FF), F32),
                        pltpu.VMEM((2, EXPERT_FF, D_MODEL), F32),
                        pltpu.VMEM((D_MODEL, 2 * EXPERT_FF), BF16),
                        pltpu.VMEM((EXPERT_FF, D_MODEL), BF16),
                        pltpu.VMEM((2 * PACK_CHUNKS * br, LANES), U32),
                        pltpu.VMEM((2 * br * PACK_CHUNKS, LANES), U32),
                        pltpu.SMEM((2 * SRC_GROUP * br,), I32),
                        pltpu.SMEM((2 * SRC_GROUP * br,), I32),
                        pltpu.SemaphoreType.DMA, pltpu.SemaphoreType.DMA,
                        pltpu.SemaphoreType.DMA((2,)), pltpu.SemaphoreType.DMA((2,))])
    n_out_tiles = n_tokens * TOP_K + 2 * br
    return pl.pallas_call(
        _moe_kernel, grid_spec=grid_spec,
        out_shape=jax.ShapeDtypeStruct((n_out_tiles * PACK_CHUNKS, LANES), U32),
        compiler_params=pltpu.CompilerParams(dimension_semantics=("arbitrary",),
                                             vmem_limit_bytes=MOE_VMEM_LIMIT),
        name="moe_experts",
    )(bexp, bval, nused, u2p, gat, dst, w1, w3, w2)


def _combine_kernel(w_hbm, yt_ref, base_ref, mod_ref, g_ref, b_ref, yc_ref, yl_ref,
                    w_smem, acc_lo, acc_hi, sem_w, *, n_ctx_tiles):
    i = pl.program_id(0)
    n_steps = pl.num_programs(0)
    n_tok = acc_lo.shape[0] // PACK_CHUNKS
    n_idx = n_tok * TOP_K

    def w_copy(tile):
        return pltpu.make_async_copy(w_hbm.at[pl.ds(tile * n_idx, n_idx)],
                                     w_smem.at[pl.ds(lax.rem(tile, 2) * n_idx, n_idx)], sem_w)

    @pl.when(i == 0)
    def _():
        w_copy(i).start()

    w_copy(i).wait()

    @pl.when(i + 1 < n_steps)
    def _():
        w_copy(i + 1).start()

    wbase = lax.rem(i, 2) * n_idx

    per_tile = SUBLANES // PACK_CHUNKS
    first = lax.broadcasted_iota(I32, (SUBLANES, LANES), 0) < PACK_CHUNKS

    def reduce_token(t):
        lo = hi = None
        for m in range(TOP_K // per_tile):
            j = t * TOP_K + m * per_tile
            words = yt_ref[pl.ds(pl.multiple_of(j * PACK_CHUNKS, SUBLANES), SUBLANES), :]
            wgt = jnp.where(first, w_smem[wbase + j], w_smem[wbase + j + 1])
            t_lo = wgt * pltpu.bitcast(words << 16, F32)
            t_hi = wgt * pltpu.bitcast(words & jnp.uint32(HIGH_HALF), F32)
            lo = t_lo if lo is None else lo + t_lo
            hi = t_hi if hi is None else hi + t_hi
        row = pl.multiple_of(t * PACK_CHUNKS, PACK_CHUNKS)
        acc_lo[pl.ds(row, PACK_CHUNKS), :] = lo[0:PACK_CHUNKS] + lo[PACK_CHUNKS:SUBLANES]
        acc_hi[pl.ds(row, PACK_CHUNKS), :] = hi[0:PACK_CHUNKS] + hi[PACK_CHUNKS:SUBLANES]

    def reduce_group(i, carry):
        for u in range(COMB_UNROLL):
            reduce_token(i * COMB_UNROLL + u)
        return carry

    lax.fori_loop(0, n_tok // COMB_UNROLL, reduce_group, 0)
    parts = []
    for s in range(PACK_CHUNKS):
        parts.append(acc_lo[pl.ds(s, n_tok, stride=PACK_CHUNKS), :])
        parts.append(acc_hi[pl.ds(s, n_tok, stride=PACK_CHUNKS), :])
    moe = jnp.concatenate(parts, axis=1)
    gate2 = mod_ref[:, 5 * D_MODEL:6 * D_MODEL]
    y = _layer_norm(base_ref[...] + gate2 * moe, g_ref[...], b_ref[...])

    @pl.when(i < n_ctx_tiles)
    def _():
        yc_ref[...] = y

    @pl.when(i >= n_ctx_tiles)
    def _():
        yl_ref[...] = y


def _combine(w_flat, yt, base, mod3, l2g, l2b, n_ctx, seq_tokens):
    n = base.shape[0]
    tc = COMB_TILE
    n_ctx_tiles = n_ctx // tc
    n_seq_tiles = seq_tokens // tc

    def mod_idx(i):
        return (jnp.where(i < n_ctx_tiles, 0, 1 + (i - n_ctx_tiles) // n_seq_tiles), 0, 0)

    full = lambda a: pl.BlockSpec(a.shape, lambda i: (0,) * a.ndim)
    return pl.pallas_call(
        functools.partial(_combine_kernel, n_ctx_tiles=n_ctx_tiles),
        grid=(n // tc,),
        in_specs=[pl.BlockSpec(memory_space=pl.ANY),
                  pl.BlockSpec((tc * TOP_K * PACK_CHUNKS, LANES), lambda i: (i, 0)),
                  pl.BlockSpec((tc, D_MODEL), lambda i: (i, 0)),
                  pl.BlockSpec((None, 1, mod3.shape[2]), mod_idx), full(l2g), full(l2b)],
        out_specs=(pl.BlockSpec((tc, D_MODEL), lambda i: (jnp.minimum(i, n_ctx_tiles - 1), 0)),
                   pl.BlockSpec((tc, D_MODEL), lambda i: (jnp.maximum(i - n_ctx_tiles, 0), 0))),
        out_shape=(jax.ShapeDtypeStruct((n_ctx, D_MODEL), F32),
                   jax.ShapeDtypeStruct((n - n_ctx, D_MODEL), F32)),
        scratch_shapes=[pltpu.SMEM((2 * tc * TOP_K,), F32),
                        pltpu.VMEM((tc * PACK_CHUNKS, LANES), F32),
                        pltpu.VMEM((tc * PACK_CHUNKS, LANES), F32),
                        pltpu.SemaphoreType.DMA],
        compiler_params=_cparams(1), name="combine",
    )(w_flat, yt, base, mod3, l2g, l2b)


def _rope_tables(n_tok, tile):
    f32 = np.float32
    rows = n_tok // GRID_W
    row_idx = np.repeat(np.arange(rows, dtype=f32), GRID_W)
    col_idx = np.tile(np.arange(GRID_W, dtype=f32), rows)
    inv_freq = (1.0 / (ROPE_THETA ** (np.arange(0, ROPE_AXIS_DIM, 2, dtype=f32) / ROPE_AXIS_DIM))).astype(f32)
    ang_r = row_idx[:, None] * inv_freq[None, :]
    ang_c = col_idx[:, None] * inv_freq[None, :]
    ang = np.concatenate([ang_r, ang_r, ang_c, ang_c], axis=-1)
    cos, sin = np.cos(ang), np.sin(ang)
    quarter = (np.arange(HEAD_DIM) // (ROPE_AXIS_DIM // 2)) % 2
    sin_a = np.where(quarter == 0, -sin, 0.0)
    sin_b = np.where(quarter == 1, sin, 0.0)
    rep = LANES // HEAD_DIM
    ident = lambda v: np.full((tile, LANES), v, f32)
    cos_t = np.concatenate([np.tile(cos, (1, rep)), ident(1.0)], axis=0)
    sa_t = np.concatenate([np.tile(sin_a, (1, rep)), ident(0.0)], axis=0)
    sb_t = np.concatenate([np.tile(sin_b, (1, rep)), ident(0.0)], axis=0)
    ident_tr = lambda v: np.full((HEAD_DIM, tile), v, f32)
    cos_tr = np.concatenate([cos.T, ident_tr(1.0)], axis=1)
    sin_tr = np.concatenate([sin.T, ident_tr(0.0)], axis=1)
    return tuple(jnp.asarray(t, F32) for t in (cos_t, sa_t, sb_t, cos_tr, sin_tr))


def _dup_heads(a):
    parts = []
    for h in range(KV_HEADS):
        blk = a[..., h * HEAD_DIM:(h + 1) * HEAD_DIM]
        parts += [blk] * (LANES // HEAD_DIM)
    return jnp.concatenate(parts, axis=-1)


def kernel(x_prompt, x_sample, cache_k, cache_v, state_gla_fwd, state_gla_bwd, c, c_ctx, w_ada, b_ada, w_in, q_norm, k_norm, gla_wa_fwd, gla_ba_fwd, gla_wa_bwd, gla_ba_bwd, gla_norm, w_out, ln1_g, ln1_b, ln2_g, ln2_b, w_router, router_bias, exp_w1, exp_w3, exp_w2, sh_w1, sh_w3, sh_w2):
    n_ctx_b, ctx_seq, _ = x_prompt.shape
    n_lat_b, lat_seq, _ = x_sample.shape
    n_ctx = n_ctx_b * ctx_seq
    n_lat = n_lat_b * lat_seq
    n = n_ctx + n_lat
    l = 0

    x_c = x_prompt.reshape(n_ctx, D_MODEL)
    x_l = x_sample.reshape(n_lat, D_MODEL)

    c_rows = jnp.zeros((SUBLANES, D_MODEL), F32).at[0].set(c_ctx).at[1:1 + n_lat_b].set(c)
    mod = _modulation(c_rows, w_ada[l], b_ada[l][None, :])
    mod3 = mod.reshape(SUBLANES, 1, 6 * D_MODEL)

    wi = w_in[l]
    o_q, o_k, o_v, o_gq, o_gk, o_gv, o_gg, o_rf, o_rb, o_end = np.cumsum(
        [0, ATT_WIDTH, KV_HEADS * HEAD_DIM, KV_HEADS * HEAD_DIM, GLA_KW, GLA_KW, GLA_WIDTH, GLA_WIDTH,
         GLA_GATE_RANK, GLA_GATE_RANK])
    w_tok = jnp.concatenate([
        _dup_heads(wi[:, o_k:o_v]), wi[:, o_v:o_gq], wi[:, o_gq:o_gk],
        wi[:, o_gv:o_gg], wi[:, o_gg:o_rf], wi[:, o_rf:o_end],
        jnp.zeros((D_MODEL, LANES - 2 * GLA_GATE_RANK), F32)], axis=1).astype(BF16)
    w_tr = jnp.concatenate([wi[:, o_q:o_k], wi[:, o_v:o_gq], wi[:, o_gk:o_gv], wi[:, o_rf:o_end]],
                           axis=1).T.astype(BF16)
    rep = LANES // HEAD_DIM
    qn = q_norm[l][:, None]
    kn = jnp.tile(k_norm[l], rep)[None, :]
    seg = jnp.asarray(np.kron(np.eye(rep), np.ones((HEAD_DIM, HEAD_DIM))), BF16)
    wa = jnp.zeros((LANES, 2 * GLA_KW), F32)
    wa = wa.at[0:GLA_GATE_RANK, 0:GLA_KW].set(gla_wa_fwd[l])
    wa = wa.at[GLA_GATE_RANK:2 * GLA_GATE_RANK, GLA_KW:].set(gla_wa_bwd[l])
    ba = jnp.concatenate([gla_ba_fwd[l], gla_ba_bwd[l]])[None, :]
    wat = wa[0:2 * GLA_GATE_RANK, :].T
    bat = ba.T
    cos_t, sa_t, sb_t, cos_tr, sin_tr = _rope_tables(lat_seq, TOK_TILE)

    (qt, k_dup, vt, k32, v32, gq, gv, gg, la, gkt, lat) = _in_projection(
        x_c, x_l, mod3, w_tok, w_tr, qn, kn, cos_t, sa_t, sb_t, cos_tr, sin_tr, seg, wa, ba, wat, bat,
        lat_seq // TOK_TILE)

    ck = _dup_heads(cache_k[:, l].reshape(n_lat_b, -1, KV_HEADS * HEAD_DIM)).astype(BF16)
    cvt = cache_v[:, l].reshape(n_lat_b, -1, KV_HEADS * HEAD_DIM).transpose(0, 2, 1).astype(BF16)
    att_c = _attention(qt, k_dup, vt, None, 0, n_ctx_b, ctx_seq)
    att_l = _attention(qt, k_dup, vt, (ck, cvt), n_ctx, n_lat_b, lat_seq)

    gconst, levels_of = _gla_constants()
    to_dev = lambda t: (jnp.asarray(t[0], BF16), jnp.asarray(t[1], BF16), jnp.asarray(t[2], F32))
    bd = jnp.asarray(np.kron(np.eye(GLA_HEADS), np.ones((GLA_DK, GLA_DV))), BF16)
    vbd = jnp.asarray(np.kron(np.eye(GLA_HEADS), np.ones((GLA_CHUNK, GLA_DV))), BF16)
    consts = ((to_dev(gconst["f"]), to_dev(gconst["b"])), levels_of, bd, vbd)
    s_zero = jnp.zeros((n_ctx_b, GLA_HEADS, GLA_DK, GLA_DV), F32)
    of_c, ob_c, sf_new, sb_new = _gla(gq, la, gkt, lat, gv, s_zero, s_zero, consts, 0, n_ctx_b, ctx_seq)
    of_l, ob_l, _, _ = _gla(gq, la, gkt, lat, gv, state_gla_fwd[:, l], state_gla_bwd[:, l], consts,
                            n_ctx, n_lat_b, lat_seq)

    sw13 = jnp.concatenate([sh_w1[l], sh_w3[l]], axis=1).astype(BF16)
    base, u2_rows, logits_t = _out_projection(
        att_c, att_l, of_c, of_l, ob_c, ob_l, gg, x_c, x_l, mod3, w_out[l].astype(BF16),
        gla_norm[l][None, :], ln1_g[l][None, :], ln1_b[l][None, :], w_router[l].T.astype(BF16), sw13,
        sh_w2[l].astype(BF16), lat_seq // TOK_TILE)

    upper = jnp.asarray(np.triu(np.ones((TOK_TILE, TOK_TILE)), 1), BF16)
    idx_t, w_t, pos_t, counts = _route(logits_t, router_bias[l][:, None], upper)
    n_blocks = n * TOP_K // MOE_ROWS + N_EXPERTS
    n_blocks_pad = -(-n_blocks // LANES) * LANES
    lower = jnp.asarray(np.tril(np.ones((N_EXPERTS, N_EXPERTS)), -1), F32)
    dest_t, bexp, bval, nused = _destinations(counts, lower, idx_t, pos_t, n_blocks_pad)
    w_flat = w_t.T.reshape(-1)

    ids = np.arange(n, dtype=np.int32)[None, :] * TOP_K + np.arange(TOP_K, dtype=np.int32)[:, None]
    row_src = _invert_rows(dest_t, ids, n_blocks * MOE_ROWS)
    gat, dst = _row_tables(row_src, bval, n_blocks, n)
    yt = _moe_experts(bexp.reshape(-1), bval.reshape(-1), nused.reshape(-1)[0:1], u2_rows, gat, dst,
                      exp_w1[l], exp_w3[l], exp_w2[l], n)
    y_c, y_l = _combine(w_flat, yt, base, mod3, ln2_g[l][None, :], ln2_b[l][None, :], n_ctx, lat_seq)

    y_prompt = y_c.reshape(n_ctx_b, ctx_seq, D_MODEL)
    y_sample = y_l.reshape(n_lat_b, lat_seq, D_MODEL)
    new_cache_k = k32.reshape(n_ctx_b, 1, ctx_seq, KV_HEADS, HEAD_DIM)
    new_cache_v = v32.reshape(n_ctx_b, 1, ctx_seq, KV_HEADS, HEAD_DIM)
    return (y_prompt, y_sample, new_cache_k, new_cache_v, sf_new[:, None], sb_new[:, None])
```

```python
import functools

import numpy as np
import jax
import jax.numpy as jnp
from jax import lax
from jax.experimental import pallas as pl
from jax.experimental.pallas import tpu as pltpu
from jax.experimental.pallas import tpu_sc as plsc

F32 = jnp.float32
BF16 = jnp.bfloat16
I32 = jnp.int32

D_MODEL = 1024
GRID_W = 64
HEAD_DIM = 64
N_HEADS = 8
KV_HEADS = 2
ATT_WIDTH = N_HEADS * HEAD_DIM
ATT_SCALE = HEAD_DIM ** -0.5
LOG2_E = 1.4426950408889634
ROPE_AXIS_DIM = HEAD_DIM // 2
ROPE_THETA = 10000.0
GLA_HEADS = 4
GLA_DK = 64
GLA_DV = 128
GLA_WIDTH = GLA_HEADS * GLA_DV
GLA_KW = GLA_HEADS * GLA_DK
GLA_GATE_RANK = 16
GLA_TAU = 16.0
N_EXPERTS = 256
TOP_K = 8
EXPERT_FF = 256
SHARED_FF = 256
ROUTED_SCALE = 2.5
DEPTH = 1
ALPHA = (2.0 * DEPTH) ** 0.25
EPS = 1e-6

LANES = 128
SUBLANES = 8
VMEM_BYTES = 64 * 1024 * 1024
VMEM_LIMIT = VMEM_BYTES - 8 * 1024 * 1024
PACK_CHUNKS = D_MODEL // (2 * LANES)
PACK_CHUNKS_LOG2 = PACK_CHUNKS.bit_length() - 1
HIGH_HALF = 0xFFFF0000
U32 = jnp.uint32

TOK_TILE = 512
ATT_TQ = 256
GLA_CHUNK = 128
GLA_LEVELS = ((32, 128), (8, 32), (2, 8), (1, 2))
MOE_ROWS = 256
MOE_VMEM_LIMIT = VMEM_BYTES - 2 * 1024 * 1024
TOP_K_LOG2 = TOP_K.bit_length() - 1
SMEM_SLICE_WORDS = 1024
SRC_GROUP = SMEM_SLICE_WORDS // MOE_ROWS
COMB_TILE = 256
COMB_UNROLL = 8
HIGHEST = lax.Precision.HIGHEST


def _cparams(n_axes):
    return pltpu.CompilerParams(dimension_semantics=("arbitrary",) * n_axes,
                                vmem_limit_bytes=VMEM_LIMIT)


def _silu(x):
    return x * jax.nn.sigmoid(x)


def _log_sigmoid(x):
    return jnp.minimum(x, 0.0) - jnp.log(1.0 + jnp.exp(-jnp.abs(x)))


def _dot_split(a, b):
    a_hi = a.astype(BF16)
    b_hi = b.astype(BF16)
    a_lo = (a - a_hi.astype(F32)).astype(BF16)
    b_lo = (b - b_hi.astype(F32)).astype(BF16)
    dot = functools.partial(jnp.dot, preferred_element_type=F32)
    return dot(a_hi, b_hi) + dot(a_lo, b_hi) + dot(a_hi, b_lo)


def _layer_norm(z, g, b):
    mu = jnp.mean(z, axis=-1, keepdims=True)
    zc = z - mu
    var = jnp.mean(zc * zc, axis=-1, keepdims=True)
    return zc * lax.rsqrt(var + EPS) * g + b


def _mod_kernel(c_ref, w_ref, b_ref, o_ref):
    s = _silu(c_ref[...]).astype(BF16)
    o_ref[...] = jnp.dot(s, w_ref[...].astype(BF16), preferred_element_type=F32) + b_ref[...]


def _modulation(c_rows, w_ada, b_ada):
    n_cols = w_ada.shape[1]
    tn = 512
    return pl.pallas_call(
        _mod_kernel,
        grid=(n_cols // tn,),
        in_specs=[pl.BlockSpec((SUBLANES, D_MODEL), lambda j: (0, 0)),
                  pl.BlockSpec((D_MODEL, tn), lambda j: (0, j)),
                  pl.BlockSpec((1, tn), lambda j: (0, j))],
        out_specs=pl.BlockSpec((SUBLANES, tn), lambda j: (0, j)),
        out_shape=jax.ShapeDtypeStruct((SUBLANES, n_cols), F32),
        compiler_params=_cparams(1),
        name="modulation",
    )(c_rows, w_ada, b_ada)


_C_K = 0
_C_V = _C_K + 2 * LANES
_C_GQ = _C_V + KV_HEADS * HEAD_DIM
_C_GV = _C_GQ + GLA_KW
_C_GG = _C_GV + GLA_WIDTH
_C_RA = _C_GG + GLA_WIDTH
_C_END = _C_RA + LANES
_R_Q = 0
_R_V = _R_Q + ATT_WIDTH
_R_GK = _R_V + KV_HEADS * HEAD_DIM
_R_RA = _R_GK + GLA_KW
_R_END = _R_RA + 2 * GLA_GATE_RANK


def _inproj_kernel(xc_ref, xl_ref, mod_ref, w_ref, wt_ref, qn_ref, kn_ref, cos_ref, sa_ref, sb_ref,
                   cost_ref, sint_ref, seg_ref, wa_ref, ba_ref, wat_ref, bat_ref,
                   qt_ref, k_ref, vt_ref, k32_ref, v32_ref, gq_ref, gv_ref, gg_ref,
                   la_ref, gkt_ref, lat_ref, *, n_ctx_tiles):
    i = pl.program_id(0)
    m = mod_ref[...]
    shift1 = m[:, 0:D_MODEL]
    scale1 = m[:, D_MODEL:2 * D_MODEL]
    x = jnp.where(i < n_ctx_tiles, xc_ref[...], xl_ref[...])
    u = (x * (1.0 + scale1) + shift1).astype(BF16)

    cos = cos_ref[...]
    sin_a = sa_ref[...]
    sin_b = sb_ref[...]
    seg = seg_ref[...]
    lane = lax.broadcasted_iota(I32, (u.shape[0], LANES), 1)
    low = lane < HEAD_DIM

    def proj(c0, c1):
        return jnp.dot(u, w_ref[:, c0:c1], preferred_element_type=F32)

    def head_norm(blk, gain):
        ss = jnp.dot((blk * blk).astype(BF16), seg, preferred_element_type=F32) * (1.0 / HEAD_DIM)
        return blk * lax.rsqrt(ss + EPS) * gain

    def rope(blk):
        return (blk * cos + pltpu.roll(blk, LANES - ROPE_AXIS_DIM // 2, 1) * sin_a
                + pltpu.roll(blk, ROPE_AXIS_DIM // 2, 1) * sin_b)

    pk = proj(_C_K, _C_V)
    kn = [head_norm(pk[:, j * LANES:(j + 1) * LANES], kn_ref[...]) for j in range(KV_HEADS)]
    for j in range(KV_HEADS):
        k_ref[:, j * LANES:(j + 1) * LANES] = rope(kn[j]).astype(BF16)

    @pl.when(i < n_ctx_tiles)
    def _():
        k32_ref[...] = jnp.where(low, kn[0], kn[1])
        v32_ref[...] = proj(_C_V, _C_GQ)

    gq_ref[...] = proj(_C_GQ, _C_GV) * (GLA_DK ** -0.5)
    gv_ref[...] = proj(_C_GV, _C_GG).astype(BF16)
    gg_ref[...] = proj(_C_GG, _C_RA).astype(BF16)

    ra = proj(_C_RA, _C_END)
    pre = _dot_split(ra, wa_ref[...]) + ba_ref[...]
    la_ref[...] = _log_sigmoid(pre) * (1.0 / GLA_TAU)

    pt = lax.dot_general(wt_ref[...], u, (((1,), (1,)), ((), ())), preferred_element_type=F32)
    cos_t = cost_ref[...]
    sin_t = sint_ref[...]
    quarter = ROPE_AXIS_DIM // 2
    for h in range(N_HEADS):
        blk = pt[_R_Q + h * HEAD_DIM:_R_Q + (h + 1) * HEAD_DIM, :]
        ms = jnp.mean(blk * blk, axis=0, keepdims=True)
        qn = blk * lax.rsqrt(ms + EPS) * qn_ref[...]
        rot = jnp.concatenate([-qn[quarter:2 * quarter], qn[0:quarter],
                               -qn[3 * quarter:4 * quarter], qn[2 * quarter:3 * quarter]], axis=0)
        qt_ref[h * HEAD_DIM:(h + 1) * HEAD_DIM, :] = (
            (qn * cos_t + rot * sin_t) * (ATT_SCALE * LOG2_E)).astype(BF16)
    vt_ref[...] = pt[_R_V:_R_GK, :].astype(BF16)
    gkt_ref[...] = pt[_R_GK:_R_RA, :]
    rat = pt[_R_RA:_R_END, :]
    pre_t = _dot_split(wat_ref[...], rat) + bat_ref[...]
    lat_ref[...] = _log_sigmoid(pre_t) * (1.0 / GLA_TAU)


def _in_projection(x_c, x_l, mod3, w_tok, w_tr, qn, kn, cos_t, sa_t, sb_t, cos_tr, sin_tr, seg, wa, ba,
                   wat, bat, n_seq_tiles):
    n_ctx = x_c.shape[0]
    n = n_ctx + x_l.shape[0]
    tb = TOK_TILE
    n_ctx_tiles = n_ctx // tb
    n_tiles = n // tb
    n_rope_blocks = cos_t.shape[0] // tb - 1

    def mod_idx(i):
        return (jnp.where(i < n_ctx_tiles, 0, 1 + (i - n_ctx_tiles) // n_seq_tiles), 0, 0)

    def rope_blk(i):
        return jnp.where(i < n_ctx_tiles, n_rope_blocks, (i - n_ctx_tiles) % n_seq_tiles)

    def rope_idx(i):
        return (rope_blk(i), 0)

    def ctx_idx(i):
        return (jnp.minimum(i, n_ctx_tiles - 1), 0)

    tok = lambda w: pl.BlockSpec((tb, w), lambda i: (i, 0))
    full = lambda a: pl.BlockSpec(a.shape, lambda i: (0,) * a.ndim)
    tr = lambda r: pl.BlockSpec((r, tb), lambda i: (0, i))
    rope_tr = pl.BlockSpec((HEAD_DIM, tb), lambda i: (0, rope_blk(i)))
    out_shapes = (
        jax.ShapeDtypeStruct((ATT_WIDTH, n), BF16),
        jax.ShapeDtypeStruct((n, 2 * LANES), BF16),
        jax.ShapeDtypeStruct((KV_HEADS * HEAD_DIM, n), BF16),
        jax.ShapeDtypeStruct((n_ctx, LANES), F32),
        jax.ShapeDtypeStruct((n_ctx, LANES), F32),
        jax.ShapeDtypeStruct((n, GLA_KW), F32),
        jax.ShapeDtypeStruct((n, GLA_WIDTH), BF16),
        jax.ShapeDtypeStruct((n, GLA_WIDTH), BF16),
        jax.ShapeDtypeStruct((n, 2 * GLA_KW), F32),
        jax.ShapeDtypeStruct((GLA_KW, n), F32),
        jax.ShapeDtypeStruct((2 * GLA_KW, n), F32),
    )
    out_specs = (tr(ATT_WIDTH), tok(2 * LANES), tr(KV_HEADS * HEAD_DIM),
                 pl.BlockSpec((tb, LANES), ctx_idx), pl.BlockSpec((tb, LANES), ctx_idx),
                 tok(GLA_KW), tok(GLA_WIDTH), tok(GLA_WIDTH), tok(2 * GLA_KW),
                 tr(GLA_KW), tr(2 * GLA_KW))
    in_specs = [pl.BlockSpec((tb, D_MODEL), ctx_idx),
                pl.BlockSpec((tb, D_MODEL), lambda i: (jnp.maximum(i - n_ctx_tiles, 0), 0)),
                pl.BlockSpec((None, 1, mod3.shape[2]), mod_idx),
                full(w_tok), full(w_tr), full(qn), full(kn),
                pl.BlockSpec((tb, LANES), rope_idx), pl.BlockSpec((tb, LANES), rope_idx),
                pl.BlockSpec((tb, LANES), rope_idx), rope_tr, rope_tr,
                full(seg), full(wa), full(ba), full(wat), full(bat)]
    return pl.pallas_call(
        functools.partial(_inproj_kernel, n_ctx_tiles=n_ctx_tiles),
        grid=(n_tiles,), in_specs=in_specs, out_specs=out_specs, out_shape=out_shapes,
        compiler_params=_cparams(1), name="in_projection",
    )(x_c, x_l, mod3, w_tok, w_tr, qn, kn, cos_t, sa_t, sb_t, cos_tr, sin_tr, seg, wa, ba, wat, bat)


def _attention_kernel(*refs, n_kv_parts):
    qt_ref = refs[0]
    k_refs = refs[1:1 + n_kv_parts]
    vt_refs = refs[1 + n_kv_parts:1 + 2 * n_kv_parts]
    o_ref = refs[1 + 2 * n_kv_parts]
    tq = qt_ref.shape[1]
    group = N_HEADS // KV_HEADS
    for kv in range(KV_HEADS):
        heads = range(kv * group, (kv + 1) * group)
        q_grp = jnp.concatenate([qt_ref[h * HEAD_DIM:(h + 1) * HEAD_DIM, :] for h in heads], axis=1)
        rhs = jnp.concatenate([q_grp, jnp.zeros_like(q_grp)], axis=0)
        s = [jnp.dot(k[:, kv * LANES:(kv + 1) * LANES], rhs, preferred_element_type=F32)
             for k in k_refs]
        mx = functools.reduce(jnp.maximum, [jnp.max(x, axis=0, keepdims=True) for x in s])
        pr = [jnp.exp2(x - mx) for x in s]
        den = functools.reduce(jnp.add, [jnp.sum(x, axis=0, keepdims=True) for x in pr])
        acc = functools.reduce(jnp.add, [
            jnp.dot(vt[kv * HEAD_DIM:(kv + 1) * HEAD_DIM, :], x.astype(BF16),
                    preferred_element_type=F32) for x, vt in zip(pr, vt_refs)])
        out = (acc / den).astype(BF16)
        for j, h in enumerate(heads):
            o_ref[h * HEAD_DIM:(h + 1) * HEAD_DIM, :] = out[:, j * tq:(j + 1) * tq]


def _attention(qt, k, vt, extra_kv, row0, n_batch, seq):
    tq = ATT_TQ
    n_q = seq // tq
    q_blk0 = row0 // tq
    kv_blk0 = row0 // seq
    in_specs = [pl.BlockSpec((ATT_WIDTH, tq), lambda b, i: (0, q_blk0 + b * n_q + i))]
    k_spec = pl.BlockSpec((seq, 2 * LANES), lambda b, i: (kv_blk0 + b, 0))
    vt_spec = pl.BlockSpec((KV_HEADS * HEAD_DIM, seq), lambda b, i: (0, kv_blk0 + b))
    args_k, args_v, specs_k, specs_v = [k], [vt], [k_spec], [vt_spec]
    if extra_kv is not None:
        ck, cvt = extra_kv
        args_k.append(ck)
        args_v.append(cvt)
        specs_k.append(pl.BlockSpec((None, ck.shape[1], 2 * LANES), lambda b, i: (b, 0, 0)))
        specs_v.append(pl.BlockSpec((None, KV_HEADS * HEAD_DIM, cvt.shape[2]), lambda b, i: (b, 0, 0)))
    return pl.pallas_call(
        functools.partial(_attention_kernel, n_kv_parts=len(args_k)),
        grid=(n_batch, n_q),
        in_specs=in_specs + specs_k + specs_v,
        out_specs=pl.BlockSpec((ATT_WIDTH, tq), lambda b, i: (0, b * n_q + i)),
        out_shape=jax.ShapeDtypeStruct((ATT_WIDTH, n_batch * seq), BF16),
        compiler_params=_cparams(2), name="attention",
    )(qt, *args_k, *args_v)


def _gla_constants():
    c = GLA_CHUNK
    idx = np.arange(c)
    q_mats, k_mats, masks, levels_of = [], [], [], []
    for li, (s, p) in enumerate(GLA_LEVELS):
        start = (idx // s) * s
        end = start + s - 1
        k_mats.append(((idx[None, :] > idx[:, None]) & (idx[None, :] <= end[:, None])))
        for d in range(p // s - 1):
            lo = np.maximum(start - d * s, 0)
            q_mats.append((idx[None, :] >= lo[:, None]) & (idx[None, :] <= idx[:, None]))
            masks.append((idx[:, None] // p == idx[None, :] // p)
                         & (idx[:, None] // s - idx[None, :] // s - 1 == d))
            levels_of.append(li)
    masks.append(np.eye(c, dtype=bool))
    levels_of.append(len(GLA_LEVELS) - 1)
    q_mats.append(idx[None, :] <= idx[:, None])
    k_mats = k_mats[:-1]
    k_mats.append(idx[None, :] > idx[:, None])
    k_mats.append(np.ones((c, c), bool))
    out = {}
    for name, flip in (("f", False), ("b", True)):
        f = (lambda a: a[::-1, ::-1]) if flip else (lambda a: a)
        lq = np.concatenate([f(a) for a in q_mats], axis=0).astype(np.float32)
        lkt = np.concatenate([f(a).T for a in k_mats], axis=1).astype(np.float32)
        mk = np.stack([np.tile(f(a), (1, GLA_HEADS)) for a in masks]).astype(np.float32)
        out[name] = (np.concatenate([lq, lq], axis=1), np.concatenate([lkt, lkt], axis=0), mk)
    return out, tuple(levels_of)


def _gla_direction(q, g, gkt, gt, v, lq2, lkt2, masks_ref, bd, vbd, s_ref, levels_of):
    c = GLA_CHUNK
    n_var = len(levels_of)
    n_lev = len(GLA_LEVELS)
    g_hi = g.astype(BF16)
    g_lo = (g - g_hi.astype(F32)).astype(BF16)
    fq = jnp.dot(lq2, jnp.concatenate([g_hi, g_lo], axis=0), preferred_element_type=F32)
    gt_hi = gt.astype(BF16)
    gt_lo = (gt - gt_hi.astype(F32)).astype(BF16)
    fk = jnp.dot(jnp.concatenate([gt_hi, gt_lo], axis=1), lkt2, preferred_element_type=F32)

    def key_factor(f):
        return gkt * jnp.exp(fk[:, f * c:(f + 1) * c])

    q_var = [(q * jnp.exp(fq[vi * c:(vi + 1) * c, :])).astype(BF16) for vi in range(n_var - 1)]
    q_var.append(q.astype(BF16))
    a = jnp.zeros((c, GLA_HEADS * c), F32)
    for li in range(n_lev):
        kt = (key_factor(li) if li < n_lev - 1 else gkt).astype(BF16)
        xt = jnp.concatenate([kt] * GLA_HEADS, axis=1) * bd
        vis = [vi for vi in range(n_var) if levels_of[vi] == li]
        res = jnp.dot(jnp.concatenate([q_var[vi] for vi in vis], axis=0), xt,
                      preferred_element_type=F32)
        for r, vi in enumerate(vis):
            a = a + masks_ref[vi] * res[r * c:(r + 1) * c, :]
    q_in = (q * jnp.exp(fq[(n_var - 1) * c:n_var * c, :])).astype(BF16)
    state = s_ref[...]
    v_bd = jnp.concatenate([v] * GLA_HEADS, axis=0) * vbd
    o = (jnp.dot(q_in, state.astype(BF16), preferred_element_type=F32)
         + jnp.dot(a.astype(BF16), v_bd, preferred_element_type=F32))
    k_out = key_factor(n_lev - 1).astype(BF16)
    e_tot = jnp.exp(fk[:, n_lev * c:(n_lev + 1) * c])
    upd = jnp.dot(k_out, v, preferred_element_type=F32)
    s_ref[...] = (state * jnp.concatenate([e_tot] * (GLA_WIDTH // c), axis=1)
                  + upd * bd.astype(F32))
    return o


def _gla_kernel(gq_f, la_f, gkt_f, lat_f, gv_f, gq_b, la_b, gkt_b, lat_b, gv_b,
                s0f_ref, s0b_ref, lq2f, lkt2f, mkf, lq2b, lkt2b, mkb, bd_ref, vbd_ref,
                of_ref, ob_ref, sf_ref, sb_ref, st_f, st_b, *, levels_of):
    n = pl.program_id(1)

    @pl.when(n == 0)
    def _():
        st_f[...] = jnp.zeros_like(st_f)
        st_b[...] = jnp.zeros_like(st_b)
        for h in range(GLA_HEADS):
            rows = slice(h * GLA_DK, (h + 1) * GLA_DK)
            cols = slice(h * GLA_DV, (h + 1) * GLA_DV)
            st_f[rows, cols] = s0f_ref[h]
            st_b[rows, cols] = s0b_ref[h]

    bd = bd_ref[...]
    vbd = vbd_ref[...]
    of_ref[...] = _gla_direction(gq_f[...], la_f[...], gkt_f[...], lat_f[...], gv_f[...],
                                 lq2f[...], lkt2f[...], mkf, bd, vbd, st_f, levels_of)
    ob_ref[...] = _gla_direction(gq_b[...], la_b[...], gkt_b[...], lat_b[...], gv_b[...],
                                 lq2b[...], lkt2b[...], mkb, bd, vbd, st_b, levels_of)

    @pl.when(n == pl.num_programs(1) - 1)
    def _():
        for h in range(GLA_HEADS):
            rows = slice(h * GLA_DK, (h + 1) * GLA_DK)
            cols = slice(h * GLA_DV, (h + 1) * GLA_DV)
            sf_ref[h] = st_f[rows, cols]
            sb_ref[h] = st_b[rows, cols]


def _gla(gq, la, gkt, lat, gv, s0f, s0b, consts, row0, n_batch, seq):
    (cf, cb), levels_of, bd, vbd = consts
    c = GLA_CHUNK
    nc = seq // c
    blk0 = row0 // c
    n_la_blocks_b = 1
    fwd = lambda b, n: blk0 + b * nc + n
    bwd = lambda b, n: blk0 + b * nc + (nc - 1 - n)

    def tok(w, which, col=0):
        return pl.BlockSpec((c, w), lambda b, n: (which(b, n), col))

    def tr(r, which, row=0):
        return pl.BlockSpec((r, c), lambda b, n: (row, which(b, n)))

    full = lambda a: pl.BlockSpec(a.shape, lambda b, n: (0,) * a.ndim)
    st_spec = pl.BlockSpec((None, GLA_HEADS, GLA_DK, GLA_DV), lambda b, n: (b, 0, 0, 0))
    in_specs = [tok(GLA_KW, fwd), tok(GLA_KW, fwd, 0), tr(GLA_KW, fwd), tr(GLA_KW, fwd, 0),
                tok(GLA_WIDTH, fwd),
                tok(GLA_KW, bwd), tok(GLA_KW, bwd, n_la_blocks_b), tr(GLA_KW, bwd),
                tr(GLA_KW, bwd, 1), tok(GLA_WIDTH, bwd),
                st_spec, st_spec,
                full(cf[0]), full(cf[1]), full(cf[2]), full(cb[0]), full(cb[1]), full(cb[2]),
                full(bd), full(vbd)]
    out_specs = (pl.BlockSpec((c, GLA_WIDTH), lambda b, n: (b * nc + n, 0)),
                 pl.BlockSpec((c, GLA_WIDTH), lambda b, n: (b * nc + (nc - 1 - n), 0)),
                 st_spec, st_spec)
    out_shape = (jax.ShapeDtypeStruct((n_batch * seq, GLA_WIDTH), F32),
                 jax.ShapeDtypeStruct((n_batch * seq, GLA_WIDTH), F32),
                 jax.ShapeDtypeStruct((n_batch, GLA_HEADS, GLA_DK, GLA_DV), F32),
                 jax.ShapeDtypeStruct((n_batch, GLA_HEADS, GLA_DK, GLA_DV), F32))
    return pl.pallas_call(
        functools.partial(_gla_kernel, levels_of=levels_of),
        grid=(n_batch, nc), in_specs=in_specs, out_specs=out_specs, out_shape=out_shape,
        scratch_shapes=[pltpu.VMEM((GLA_KW, GLA_WIDTH), F32), pltpu.VMEM((GLA_KW, GLA_WIDTH), F32)],
        compiler_params=_cparams(2), name="gla",
    )(gq, la, gkt, lat, gv, gq, la, gkt, lat, gv, s0f, s0b,
      cf[0], cf[1], cf[2], cb[0], cb[1], cb[2], bd, vbd)


def _outproj_kernel(attc_ref, attl_ref, ofc_ref, ofl_ref, obc_ref, obl_ref, gg_ref, xc_ref, xl_ref,
                    mod_ref, wo_ref, gn_ref, l1g_ref, l1b_ref, wrt_ref, sw13_ref, sw2_ref,
                    base_ref, u2_ref, lg_ref, *, n_ctx_tiles):
    is_ctx = pl.program_id(0) < n_ctx_tiles
    pick = lambda a_ref, b_ref: jnp.where(is_ctx, a_ref[...], b_ref[...])
    m = mod_ref[...]
    gate1 = m[:, 2 * D_MODEL:3 * D_MODEL]
    shift2 = m[:, 3 * D_MODEL:4 * D_MODEL]
    scale2 = m[:, 4 * D_MODEL:5 * D_MODEL]
    gate2 = m[:, 5 * D_MODEL:6 * D_MODEL]
    og = pick(ofc_ref, ofl_ref) + pick(obc_ref, obl_ref)
    gg = gg_ref[...].astype(F32)
    parts = []
    for h in range(GLA_HEADS):
        blk = og[:, h * GLA_DV:(h + 1) * GLA_DV]
        ms = jnp.mean(blk * blk, axis=-1, keepdims=True)
        nb = blk * lax.rsqrt(ms + EPS) * gn_ref[...]
        parts.append((nb * _silu(gg[:, h * GLA_DV:(h + 1) * GLA_DV])).astype(BF16))
    att_t = pick(attc_ref, attl_ref)
    hmix = (lax.dot_general(att_t, wo_ref[0:ATT_WIDTH, :], (((0,), (0,)), ((), ())),
                            preferred_element_type=F32)
            + jnp.dot(jnp.concatenate(parts, axis=1), wo_ref[ATT_WIDTH:, :],
                      preferred_element_type=F32))
    x1 = _layer_norm(ALPHA * pick(xc_ref, xl_ref) + gate1 * hmix, l1g_ref[...], l1b_ref[...])
    u2 = x1 * (1.0 + scale2) + shift2
    u2b = u2.astype(BF16)
    lg_ref[...] = lax.dot_general(wrt_ref[...], u2b, (((1,), (1,)), ((), ())),
                                  preferred_element_type=F32)
    ab = jnp.dot(u2b, sw13_ref[...], preferred_element_type=F32)
    hid = (_silu(ab[:, 0:SHARED_FF]) * ab[:, SHARED_FF:2 * SHARED_FF]).astype(BF16)
    shared = jnp.dot(hid, sw2_ref[...], preferred_element_type=F32)
    base_ref[...] = ALPHA * x1 + gate2 * shared
    _pack_rows(u2_ref, u2)


def _out_projection(att_c, att_l, of_c, of_l, ob_c, ob_l, gg, x_c, x_l, mod3, wo, gn, l1g, l1b, wrt,
                    sw13, sw2, n_seq_tiles):
    n_ctx = x_c.shape[0]
    n = n_ctx + x_l.shape[0]
    tb = TOK_TILE
    n_ctx_tiles = n_ctx // tb

    def mod_idx(i):
        return (jnp.where(i < n_ctx_tiles, 0, 1 + (i - n_ctx_tiles) // n_seq_tiles), 0, 0)

    ctx_blk = lambda i: jnp.minimum(i, n_ctx_tiles - 1)
    lat_blk = lambda i: jnp.maximum(i - n_ctx_tiles, 0)
    tok = lambda w: pl.BlockSpec((tb, w), lambda i: (i, 0))
    tok_c = lambda w: pl.BlockSpec((tb, w), lambda i: (ctx_blk(i), 0))
    tok_l = lambda w: pl.BlockSpec((tb, w), lambda i: (lat_blk(i), 0))
    full = lambda a: pl.BlockSpec(a.shape, lambda i: (0,) * a.ndim)
    return pl.pallas_call(
        functools.partial(_outproj_kernel, n_ctx_tiles=n_ctx_tiles),
        grid=(n // tb,),
        in_specs=[pl.BlockSpec((ATT_WIDTH, tb), lambda i: (0, ctx_blk(i))),
                  pl.BlockSpec((ATT_WIDTH, tb), lambda i: (0, lat_blk(i))),
                  tok_c(GLA_WIDTH), tok_l(GLA_WIDTH), tok_c(GLA_WIDTH), tok_l(GLA_WIDTH),
                  tok(GLA_WIDTH), tok_c(D_MODEL), tok_l(D_MODEL),
                  pl.BlockSpec((None, 1, mod3.shape[2]), mod_idx),
                  full(wo), full(gn), full(l1g), full(l1b), full(wrt), full(sw13), full(sw2)],
        out_specs=(tok(D_MODEL),
                   pl.BlockSpec((tb * PACK_CHUNKS, LANES), lambda i: (i, 0)),
                   pl.BlockSpec((N_EXPERTS, tb), lambda i: (0, i))),
        out_shape=(jax.ShapeDtypeStruct((n, D_MODEL), F32),
                   jax.ShapeDtypeStruct((n * PACK_CHUNKS, LANES), U32),
                   jax.ShapeDtypeStruct((N_EXPERTS, n), F32)),
        compiler_params=_cparams(1), name="out_projection",
    )(att_c, att_l, of_c, of_l, ob_c, ob_l, gg, x_c, x_l, mod3, wo, gn, l1g, l1b, wrt, sw13, sw2)


def _route_kernel(lg_ref, bias_ref, upper_ref, idx_ref, w_ref, pos_ref, cnt_ref, run_ref):
    i = pl.program_id(0)

    @pl.when(i == 0)
    def _():
        run_ref[...] = jnp.zeros_like(run_ref)

    s = jax.nn.sigmoid(lg_ref[...])
    work = s + bias_ref[...]
    rows = lax.broadcasted_iota(I32, s.shape, 0).astype(F32)
    sel = jnp.zeros(s.shape, F32)
    idxs, vals = [], []
    for _ in range(TOP_K):
        mx = jnp.max(work, axis=0, keepdims=True)
        idx = jnp.min(jnp.where(work == mx, rows, float(N_EXPERTS)), axis=0, keepdims=True)
        hit = rows == idx
        vals.append(jnp.sum(jnp.where(hit, s, 0.0), axis=0, keepdims=True))
        idxs.append(idx)
        sel = jnp.where(hit, 1.0, sel)
        work = jnp.where(hit, -jnp.inf, work)
    den = functools.reduce(jnp.add, vals)
    rank = jnp.dot(sel.astype(BF16), upper_ref[...], preferred_element_type=F32) + run_ref[:, 0:1]
    for k in range(TOP_K):
        idx_ref[k:k + 1, :] = idxs[k].astype(I32)
        w_ref[k:k + 1, :] = vals[k] / den * ROUTED_SCALE
        pos_ref[k:k + 1, :] = jnp.sum(jnp.where(rows == idxs[k], rank, 0.0), axis=0,
                                      keepdims=True).astype(I32)
    run_ref[...] = run_ref[...] + jnp.sum(sel, axis=1, keepdims=True)
    cnt_ref[...] = run_ref[...]


def _route(logits_t, bias_col, upper):
    n = logits_t.shape[1]
    tt = TOK_TILE
    row = lambda dt: jax.ShapeDtypeStruct((TOP_K, n), dt)
    blk = pl.BlockSpec((TOP_K, tt), lambda i: (0, i))
    return pl.pallas_call(
        _route_kernel,
        grid=(n // tt,),
        in_specs=[pl.BlockSpec((N_EXPERTS, tt), lambda i: (0, i)),
                  pl.BlockSpec((N_EXPERTS, 1), lambda i: (0, 0)),
                  pl.BlockSpec((tt, tt), lambda i: (0, 0))],
        out_specs=(blk, blk, blk, pl.BlockSpec((N_EXPERTS, LANES), lambda i: (0, 0))),
        out_shape=(row(I32), row(F32), row(I32), jax.ShapeDtypeStruct((N_EXPERTS, LANES), F32)),
        scratch_shapes=[pltpu.VMEM((N_EXPERTS, LANES), F32)],
        compiler_params=_cparams(1), name="route",
    )(logits_t, bias_col, upper)


def _dest_kernel(cnt_ref, lower_ref, idx_ref, pos_ref, dest_ref, bexp_ref, bval_ref, nused_ref):
    cnt = cnt_ref[...]
    nblk = jnp.floor((cnt + (MOE_ROWS - 1)) * (1.0 / MOE_ROWS))
    bstart = jnp.dot(lower_ref[...], nblk, precision=HIGHEST, preferred_element_type=F32)
    bend = bstart + nblk
    pstart = bstart[:, 0:1] * MOE_ROWS
    rows = lax.broadcasted_iota(I32, (N_EXPERTS, idx_ref.shape[1]), 0)
    for k in range(TOP_K):
        hit = rows == idx_ref[k:k + 1, :]
        dest_ref[k:k + 1, :] = (jnp.sum(jnp.where(hit, pstart, 0.0), axis=0, keepdims=True)
                                .astype(I32) + pos_ref[k:k + 1, :])

    @pl.when(pl.program_id(0) == 0)
    def _():
        nb = bexp_ref.shape[1]
        bid = lax.broadcasted_iota(I32, (N_EXPERTS, nb), 1).astype(F32)
        inside = jnp.logical_and(bid >= bstart[:, 0:1], bid < bend[:, 0:1])
        erow = lax.broadcasted_iota(I32, (N_EXPERTS, nb), 0).astype(F32)
        bexp_ref[...] = jnp.sum(jnp.where(inside, erow, 0.0), axis=0, keepdims=True).astype(I32)
        valid = jnp.clip(cnt[:, 0:1] - (bid - bstart[:, 0:1]) * MOE_ROWS, 0.0, float(MOE_ROWS))
        bval_ref[...] = jnp.sum(jnp.where(inside, valid, 0.0), axis=0, keepdims=True).astype(I32)
        nused_ref[...] = jnp.max(bend, axis=0, keepdims=True).astype(I32)


def _destinations(counts, lower, idx_t, pos_t, n_blocks_pad):
    n = idx_t.shape[1]
    tt = TOK_TILE
    blk = pl.BlockSpec((TOP_K, tt), lambda i: (0, i))
    one = lambda w: pl.BlockSpec((1, w), lambda i: (0, 0))
    return pl.pallas_call(
        _dest_kernel,
        grid=(n // tt,),
        in_specs=[pl.BlockSpec((N_EXPERTS, LANES), lambda i: (0, 0)),
                  pl.BlockSpec((N_EXPERTS, N_EXPERTS), lambda i: (0, 0)), blk, blk],
        out_specs=(blk, one(n_blocks_pad), one(n_blocks_pad), one(LANES)),
        out_shape=(jax.ShapeDtypeStruct((TOP_K, n), I32),
                   jax.ShapeDtypeStruct((1, n_blocks_pad), I32),
                   jax.ShapeDtypeStruct((1, n_blocks_pad), I32),
                   jax.ShapeDtypeStruct((1, LANES), I32)),
        compiler_params=_cparams(1), name="destinations",
    )(counts, lower, idx_t, pos_t)


SC_WINDOW = 128
SC_WINDOWS_PER_STEP = 8


def _invert_rows(dest, ids, n_rows):
    m = dest.size
    mesh = plsc.VectorSubcoreMesh(core_axis_name="core", subcore_axis_name="subcore")

    @functools.partial(pl.kernel, out_type=jax.ShapeDtypeStruct((n_rows,), I32), mesh=mesh,
                       scratch_types=[])
    def invert(val_hbm, idx_hbm, out_hbm):
        def body(val_vmem, idx_vmem):
            for j in range(SC_WINDOWS_PER_STEP):
                pltpu.sync_copy(val_vmem.at[j], out_hbm.at[idx_vmem.at[j]])

        blk = pl.BlockSpec((SC_WINDOWS_PER_STEP, SC_WINDOW), lambda i: (i, 0))
        pltpu.emit_pipeline(
            body, grid=(m // (SC_WINDOW * SC_WINDOWS_PER_STEP),),
            in_specs=[blk, blk], out_specs=[], core_axis_name=("core", "subcore"),
            dimension_semantics=(pltpu.PARALLEL,),
        )(val_hbm, idx_hbm)

    shape = (m // SC_WINDOW, SC_WINDOW)
    return invert(jnp.asarray(ids.reshape(shape), I32), dest.reshape(shape))


def _pack_rows(ref, x, row0=0):
    bits = pltpu.bitcast(x.astype(BF16).astype(F32), U32)
    for s in range(PACK_CHUNKS):
        lo = bits[:, (2 * s) * LANES:(2 * s + 1) * LANES] >> 16
        hi = bits[:, (2 * s + 1) * LANES:(2 * s + 2) * LANES] & jnp.uint32(HIGH_HALF)
        ref[pl.ds(row0 + s, x.shape[0], stride=PACK_CHUNKS), :] = lo | hi


def _unpack_rows(ref, n_rows, row0=0):
    parts = []
    for s in range(PACK_CHUNKS):
        w = ref[pl.ds(row0 + s, n_rows, stride=PACK_CHUNKS), :]
        parts.append(pltpu.bitcast(w << 16, F32))
        parts.append(pltpu.bitcast(w & jnp.uint32(HIGH_HALF), F32))
    return jnp.concatenate(parts, axis=1).astype(BF16)


def _moe_kernel(bexp_ref, bval_ref, nused_ref, u2p_hbm, gat_hbm, dst_hbm, w1_hbm, w3_hbm, w2_hbm,
                yt_hbm, u2p_vmem, w1_f, w3_f, w2_f, w13_s, w2_s, xbuf, ybuf, gat_smem, dst_smem,
                sem_in, sem_src, sem_w, sem_out):
    n_used = nused_ref[0]
    br = MOE_ROWS
    grp = SRC_GROUP * br

    def src_copies(g):
        window = pl.ds(g * grp, grp)
        ring = pl.ds(lax.rem(g, 2) * grp, grp)
        return (pltpu.make_async_copy(gat_hbm.at[window], gat_smem.at[ring], sem_src),
                pltpu.make_async_copy(dst_hbm.at[window], dst_smem.at[ring], sem_src))

    def out_wait(slot):
        pltpu.make_async_copy(ybuf.at[pl.ds(slot * br * PACK_CHUNKS, br * PACK_CHUNKS)],
                              yt_hbm.at[pl.ds(0, br * PACK_CHUNKS)], sem_out.at[slot]).wait()

    def src_base(blk):
        return lax.rem(blk // SRC_GROUP, 2) * grp + lax.rem(blk, SRC_GROUP) * br

    def scatter_row(blk_slot, sbase, r, priority=0):
        pltpu.make_async_copy(
            ybuf.at[pl.ds(pl.multiple_of((blk_slot * br + r) * PACK_CHUNKS, PACK_CHUNKS), PACK_CHUNKS)],
            yt_hbm.at[pl.ds(pl.multiple_of(dst_smem[sbase + r], PACK_CHUNKS), PACK_CHUNKS)],
            sem_out.at[blk_slot]).start(priority=priority)

    def gather_row(xslot, sbase, r):
        dst = pl.multiple_of((xslot * br + r) * PACK_CHUNKS, PACK_CHUNKS)
        row = pl.multiple_of(gat_smem[sbase + r], PACK_CHUNKS)
        xbuf[pl.ds(dst, PACK_CHUNKS), :] = u2p_vmem[pl.ds(row, PACK_CHUNKS), :]

    def weight_copies(e, wslot):
        return [pltpu.make_async_copy(src.at[e], dst.at[wslot], sem_w.at[wslot])
                for src, dst in ((w1_hbm, w1_f), (w3_hbm, w3_f), (w2_hbm, w2_f))]

    cp = pltpu.make_async_copy(u2p_hbm, u2p_vmem, sem_in)
    cp.start()
    for scp in src_copies(0):
        scp.start()
    for wcp in weight_copies(bexp_ref[0], 0):
        wcp.start()
    ybuf[...] = jnp.zeros_like(ybuf)
    cp.wait()
    for scp in src_copies(0):
        scp.wait()
    lax.fori_loop(0, br, lambda r, c: (gather_row(0, 0, r), c)[1], 0)

    def block(b, wslot):
        g = b // SRC_GROUP
        phase = lax.rem(b, SRC_GROUP)

        more = (g + 1) * SRC_GROUP < n_used

        @pl.when(jnp.logical_and(phase == 1, more))
        def _():
            for scp in src_copies(g + 1):
                scp.start()

        @pl.when(jnp.logical_and(phase == SRC_GROUP - 1, more))
        def _():
            for scp in src_copies(g + 1):
                scp.wait()

        e = bexp_ref[b]
        prev = bexp_ref[jnp.maximum(b - 1, 0)]

        @pl.when(jnp.logical_or(b == 0, e != prev))
        def _():
            for wcp in weight_copies(e, wslot):
                wcp.wait()
            w13_s[:, 0:EXPERT_FF] = w1_f[wslot].astype(BF16)
            w13_s[:, EXPERT_FF:2 * EXPERT_FF] = w3_f[wslot].astype(BF16)
            w2_s[...] = w2_f[wslot].astype(BF16)
            nxt = lax.while_loop(
                lambda j: jnp.logical_and(j < n_used, bexp_ref[jnp.minimum(j, n_used - 1)] == e),
                lambda j: j + 1, b + 1)

            @pl.when(nxt < n_used)
            def _():
                for wcp in weight_copies(bexp_ref[nxt], 1 - wslot):
                    wcp.start()

        switch = jnp.logical_and(b + 1 < n_used, bexp_ref[b + 1] != e)

        valid = bval_ref[b]
        slot = lax.rem(b, 2)

        p_base = src_base(b)
        n_base = src_base(b + 1)
        for r in range(br):
            gather_row(1 - slot, n_base, r)
            scatter_row(1 - slot, p_base, r, priority=r % 2)

        x = _unpack_rows(xbuf, br, slot * (br * PACK_CHUNKS))
        rows = lax.broadcasted_iota(I32, x.shape, 0)
        x = jnp.where(rows < valid, x, jnp.zeros_like(x))
        ab = jnp.dot(x, w13_s[...], preferred_element_type=F32)
        hid = (_silu(ab[:, 0:EXPERT_FF]) * ab[:, EXPERT_FF:2 * EXPERT_FF]).astype(BF16)
        y = jnp.dot(hid, w2_s[...], preferred_element_type=F32)

        @pl.when(b >= 1)
        def _():
            out_wait(slot)

        _pack_rows(ybuf, y, slot * (br * PACK_CHUNKS))
        return jnp.where(switch, 1 - wslot, wslot)

    lax.fori_loop(0, n_used, block, 0)

    last = n_used - 1
    l_slot = lax.rem(last, 2)
    for scp in src_copies(n_used // SRC_GROUP):
        scp.start()
    for scp in src_copies(n_used // SRC_GROUP):
        scp.wait()
    l_base = src_base(n_used)
    lax.fori_loop(0, br, lambda r, c: (scatter_row(l_slot, l_base, r), c)[1], 0)
    out_wait(1 - l_slot)
    out_wait(l_slot)


def _row_tables_kernel(src_ref, srcp_ref, bvalp_ref, gat_ref, dst_ref, *, n_tokens):
    src = src_ref[...]
    row = lax.shift_right_logical(src, TOP_K_LOG2 - PACK_CHUNKS_LOG2)
    row = row & (int(jnp.iinfo(I32).max) - (PACK_CHUNKS - 1))
    gat_ref[...] = jnp.minimum(row, (n_tokens - 1) * PACK_CHUNKS)
    blk = lax.broadcasted_iota(I32, srcp_ref.shape, 0)
    lane = lax.broadcasted_iota(I32, srcp_ref.shape, 1)
    spare = n_tokens * TOP_K + ((blk + 1) & 1) * MOE_ROWS + lane
    dst_ref[...] = jnp.where(lane < bvalp_ref[...], srcp_ref[...], spare) * PACK_CHUNKS


def _row_tables(row_src, bval, n_blocks, n_tokens):
    n_tab = -(-(n_blocks + 1) // SRC_GROUP) * SRC_GROUP
    src2 = row_src.reshape(n_blocks, MOE_ROWS)
    pad = ((1, n_tab - n_blocks - 1), (0, 0))
    src_tab = jnp.pad(src2, ((0, n_tab - n_blocks), (0, 0)))
    srcp = jnp.pad(src2, pad)
    bvalp = jnp.pad(bval.reshape(-1)[:n_blocks, None], pad)
    full = lambda a: pl.BlockSpec(a.shape, lambda i: (0,) * a.ndim)
    tab = jax.ShapeDtypeStruct((n_tab, MOE_ROWS), I32)
    gat, dst = pl.pallas_call(
        functools.partial(_row_tables_kernel, n_tokens=n_tokens),
        grid=(1,), in_specs=[full(src_tab), full(srcp), full(bvalp)],
        out_specs=(full(src_tab), full(src_tab)), out_shape=(tab, tab),
        compiler_params=_cparams(1), name="row_tables",
    )(src_tab, srcp, bvalp)
    return gat.reshape(-1), dst.reshape(-1)


def _moe_experts(bexp, bval, nused, u2p, gat, dst, w1, w3, w2, n_tokens):
    br = MOE_ROWS
    any_spec = pl.BlockSpec(memory_space=pl.ANY)
    grid_spec = pltpu.PrefetchScalarGridSpec(
        num_scalar_prefetch=3, grid=(1,),
        in_specs=[any_spec] * 6,
        out_specs=any_spec,
        scratch_shapes=[pltpu.VMEM(u2p.shape, U32),
                        pltpu.VMEM((2, D_MODEL, EXPERT_FF), F32),
                        pltpu.VMEM((2, D_MODEL, EXPERT_FF), F32),
                        pltpu.VMEM((2, EXPERT_FF, D_MODEL), F32),
                        pltpu.VMEM((D_MODEL, 2 * EXPERT_FF), BF16),
                        pltpu.VMEM((EXPERT_FF, D_MODEL), BF16),
                        pltpu.VMEM((2 * PACK_CHUNKS * br, LANES), U32),
                        pltpu.VMEM((2 * br * PACK_CHUNKS, LANES), U32),
                        pltpu.SMEM((2 * SRC_GROUP * br,), I32),
                        pltpu.SMEM((2 * SRC_GROUP * br,), I32),
                        pltpu.SemaphoreType.DMA, pltpu.SemaphoreType.DMA,
                        pltpu.SemaphoreType.DMA((2,)), pltpu.SemaphoreType.DMA((2,))])
    n_out_tiles = n_tokens * TOP_K + 2 * br
    return pl.pallas_call(
        _moe_kernel, grid_spec=grid_spec,
        out_shape=jax.ShapeDtypeStruct((n_out_tiles * PACK_CHUNKS, LANES), U32),
        compiler_params=pltpu.CompilerParams(dimension_semantics=("arbitrary",),
                                             vmem_limit_bytes=MOE_VMEM_LIMIT),
        name="moe_experts",
    )(bexp, bval, nused, u2p, gat, dst, w1, w3, w2)


def _combine_kernel(w_hbm, yt_ref, base_ref, mod_ref, g_ref, b_ref, yc_ref, yl_ref,
                    w_smem, acc_lo, acc_hi, sem_w, *, n_ctx_tiles):
    i = pl.program_id(0)
    n_steps = pl.num_programs(0)
    n_tok = acc_lo.shape[0] // PACK_CHUNKS
    n_idx = n_tok * TOP_K

    def w_copy(tile):
        return pltpu.make_async_copy(w_hbm.at[pl.ds(tile * n_idx, n_idx)],
                                     w_smem.at[pl.ds(lax.rem(tile, 2) * n_idx, n_idx)], sem_w)

    @pl.when(i == 0)
    def _():
        w_copy(i).start()

    w_copy(i).wait()

    @pl.when(i + 1 < n_steps)
    def _():
        w_copy(i + 1).start()

    wbase = lax.rem(i, 2) * n_idx

    per_tile = SUBLANES // PACK_CHUNKS
    first = lax.broadcasted_iota(I32, (SUBLANES, LANES), 0) < PACK_CHUNKS

    def reduce_token(t):
        lo = hi = None
        for m in range(TOP_K // per_tile):
            j = t * TOP_K + m * per_tile
            words = yt_ref[pl.ds(pl.multiple_of(j * PACK_CHUNKS, SUBLANES), SUBLANES), :]
            wgt = jnp.where(first, w_smem[wbase + j], w_smem[wbase + j + 1])
            t_lo = wgt * pltpu.bitcast(words << 16, F32)
            t_hi = wgt * pltpu.bitcast(words & jnp.uint32(HIGH_HALF), F32)
            lo = t_lo if lo is None else lo + t_lo
            hi = t_hi if hi is None else hi + t_hi
        row = pl.multiple_of(t * PACK_CHUNKS, PACK_CHUNKS)
        acc_lo[pl.ds(row, PACK_CHUNKS), :] = lo[0:PACK_CHUNKS] + lo[PACK_CHUNKS:SUBLANES]
        acc_hi[pl.ds(row, PACK_CHUNKS), :] = hi[0:PACK_CHUNKS] + hi[PACK_CHUNKS:SUBLANES]

    def reduce_group(i, carry):
        for u in range(COMB_UNROLL):
            reduce_token(i * COMB_UNROLL + u)
        return carry

    lax.fori_loop(0, n_tok // COMB_UNROLL, reduce_group, 0)
    parts = []
    for s in range(PACK_CHUNKS):
        parts.append(acc_lo[pl.ds(s, n_tok, stride=PACK_CHUNKS), :])
        parts.append(acc_hi[pl.ds(s, n_tok, stride=PACK_CHUNKS), :])
    moe = jnp.concatenate(parts, axis=1)
    gate2 = mod_ref[:, 5 * D_MODEL:6 * D_MODEL]
    y = _layer_norm(base_ref[...] + gate2 * moe, g_ref[...], b_ref[...])

    @pl.when(i < n_ctx_tiles)
    def _():
        yc_ref[...] = y

    @pl.when(i >= n_ctx_tiles)
    def _():
        yl_ref[...] = y


def _combine(w_flat, yt, base, mod3, l2g, l2b, n_ctx, seq_tokens):
    n = base.shape[0]
    tc = COMB_TILE
    n_ctx_tiles = n_ctx // tc
    n_seq_tiles = seq_tokens // tc

    def mod_idx(i):
        return (jnp.where(i < n_ctx_tiles, 0, 1 + (i - n_ctx_tiles) // n_seq_tiles), 0, 0)

    full = lambda a: pl.BlockSpec(a.shape, lambda i: (0,) * a.ndim)
    return pl.pallas_call(
        functools.partial(_combine_kernel, n_ctx_tiles=n_ctx_tiles),
        grid=(n // tc,),
        in_specs=[pl.BlockSpec(memory_space=pl.ANY),
                  pl.BlockSpec((tc * TOP_K * PACK_CHUNKS, LANES), lambda i: (i, 0)),
                  pl.BlockSpec((tc, D_MODEL), lambda i: (i, 0)),
                  pl.BlockSpec((None, 1, mod3.shape[2]), mod_idx), full(l2g), full(l2b)],
        out_specs=(pl.BlockSpec((tc, D_MODEL), lambda i: (jnp.minimum(i, n_ctx_tiles - 1), 0)),
                   pl.BlockSpec((tc, D_MODEL), lambda i: (jnp.maximum(i - n_ctx_tiles, 0), 0))),
        out_shape=(jax.ShapeDtypeStruct((n_ctx, D_MODEL), F32),
                   jax.ShapeDtypeStruct((n - n_ctx, D_MODEL), F32)),
        scratch_shapes=[pltpu.SMEM((2 * tc * TOP_K,), F32),
                        pltpu.VMEM((tc * PACK_CHUNKS, LANES), F32),
                        pltpu.VMEM((tc * PACK_CHUNKS, LANES), F32),
                        pltpu.SemaphoreType.DMA],
        compiler_params=_cparams(1), name="combine",
    )(w_flat, yt, base, mod3, l2g, l2b)


def _rope_tables(n_tok, tile):
    f32 = np.float32
    rows = n_tok // GRID_W
    row_idx = np.repeat(np.arange(rows, dtype=f32), GRID_W)
    col_idx = np.tile(np.arange(GRID_W, dtype=f32), rows)
    inv_freq = (1.0 / (ROPE_THETA ** (np.arange(0, ROPE_AXIS_DIM, 2, dtype=f32) / ROPE_AXIS_DIM))).astype(f32)
    ang_r = row_idx[:, None] * inv_freq[None, :]
    ang_c = col_idx[:, None] * inv_freq[None, :]
    ang = np.concatenate([ang_r, ang_r, ang_c, ang_c], axis=-1)
    cos, sin = np.cos(ang), np.sin(ang)
    quarter = (np.arange(HEAD_DIM) // (ROPE_AXIS_DIM // 2)) % 2
    sin_a = np.where(quarter == 0, -sin, 0.0)
    sin_b = np.where(quarter == 1, sin, 0.0)
    rep = LANES // HEAD_DIM
    ident = lambda v: np.full((tile, LANES), v, f32)
    cos_t = np.concatenate([np.tile(cos, (1, rep)), ident(1.0)], axis=0)
    sa_t = np.concatenate([np.tile(sin_a, (1, rep)), ident(0.0)], axis=0)
    sb_t = np.concatenate([np.tile(sin_b, (1, rep)), ident(0.0)], axis=0)
    ident_tr = lambda v: np.full((HEAD_DIM, tile), v, f32)
    cos_tr = np.concatenate([cos.T, ident_tr(1.0)], axis=1)
    sin_tr = np.concatenate([sin.T, ident_tr(0.0)], axis=1)
    return tuple(jnp.asarray(t, F32) for t in (cos_t, sa_t, sb_t, cos_tr, sin_tr))


def _dup_heads(a):
    parts = []
    for h in range(KV_HEADS):
        blk = a[..., h * HEAD_DIM:(h + 1) * HEAD_DIM]
        parts += [blk] * (LANES // HEAD_DIM)
    return jnp.concatenate(parts, axis=-1)


def kernel(x_prompt, x_sample, cache_k, cache_v, state_gla_fwd, state_gla_bwd, c, c_ctx, w_ada, b_ada, w_in, q_norm, k_norm, gla_wa_fwd, gla_ba_fwd, gla_wa_bwd, gla_ba_bwd, gla_norm, w_out, ln1_g, ln1_b, ln2_g, ln2_b, w_router, router_bias, exp_w1, exp_w3, exp_w2, sh_w1, sh_w3, sh_w2):
    n_ctx_b, ctx_seq, _ = x_prompt.shape
    n_lat_b, lat_seq, _ = x_sample.shape
    n_ctx = n_ctx_b * ctx_seq
    n_lat = n_lat_b * lat_seq
    n = n_ctx + n_lat
    l = 0

    x_c = x_prompt.reshape(n_ctx, D_MODEL)
    x_l = x_sample.reshape(n_lat, D_MODEL)

    c_rows = jnp.zeros((SUBLANES, D_MODEL), F32).at[0].set(c_ctx).at[1:1 + n_lat_b].set(c)
    mod = _modulation(c_rows, w_ada[l], b_ada[l][None, :])
    mod3 = mod.reshape(SUBLANES, 1, 6 * D_MODEL)

    wi = w_in[l]
    o_q, o_k, o_v, o_gq, o_gk, o_gv, o_gg, o_rf, o_rb, o_end = np.cumsum(
        [0, ATT_WIDTH, KV_HEADS * HEAD_DIM, KV_HEADS * HEAD_DIM, GLA_KW, GLA_KW, GLA_WIDTH, GLA_WIDTH,
         GLA_GATE_RANK, GLA_GATE_RANK])
    w_tok = jnp.concatenate([
        _dup_heads(wi[:, o_k:o_v]), wi[:, o_v:o_gq], wi[:, o_gq:o_gk],
        wi[:, o_gv:o_gg], wi[:, o_gg:o_rf], wi[:, o_rf:o_end],
        jnp.zeros((D_MODEL, LANES - 2 * GLA_GATE_RANK), F32)], axis=1).astype(BF16)
    w_tr = jnp.concatenate([wi[:, o_q:o_k], wi[:, o_v:o_gq], wi[:, o_gk:o_gv], wi[:, o_rf:o_end]],
                           axis=1).T.astype(BF16)
    rep = LANES // HEAD_DIM
    qn = q_norm[l][:, None]
    kn = jnp.tile(k_norm[l], rep)[None, :]
    seg = jnp.asarray(np.kron(np.eye(rep), np.ones((HEAD_DIM, HEAD_DIM))), BF16)
    wa = jnp.zeros((LANES, 2 * GLA_KW), F32)
    wa = wa.at[0:GLA_GATE_RANK, 0:GLA_KW].set(gla_wa_fwd[l])
    wa = wa.at[GLA_GATE_RANK:2 * GLA_GATE_RANK, GLA_KW:].set(gla_wa_bwd[l])
    ba = jnp.concatenate([gla_ba_fwd[l], gla_ba_bwd[l]])[None, :]
    wat = wa[0:2 * GLA_GATE_RANK, :].T
    bat = ba.T
    cos_t, sa_t, sb_t, cos_tr, sin_tr = _rope_tables(lat_seq, TOK_TILE)

    (qt, k_dup, vt, k32, v32, gq, gv, gg, la, gkt, lat) = _in_projection(
        x_c, x_l, mod3, w_tok, w_tr, qn, kn, cos_t, sa_t, sb_t, cos_tr, sin_tr, seg, wa, ba, wat, bat,
        lat_seq // TOK_TILE)

    ck = _dup_heads(cache_k[:, l].reshape(n_lat_b, -1, KV_HEADS * HEAD_DIM)).astype(BF16)
    cvt = cache_v[:, l].reshape(n_lat_b, -1, KV_HEADS * HEAD_DIM).transpose(0, 2, 1).astype(BF16)
    att_c = _attention(qt, k_dup, vt, None, 0, n_ctx_b, ctx_seq)
    att_l = _attention(qt, k_dup, vt, (ck, cvt), n_ctx, n_lat_b, lat_seq)

    gconst, levels_of = _gla_constants()
    to_dev = lambda t: (jnp.asarray(t[0], BF16), jnp.asarray(t[1], BF16), jnp.asarray(t[2], F32))
    bd = jnp.asarray(np.kron(np.eye(GLA_HEADS), np.ones((GLA_DK, GLA_DV))), BF16)
    vbd = jnp.asarray(np.kron(np.eye(GLA_HEADS), np.ones((GLA_CHUNK, GLA_DV))), BF16)
    consts = ((to_dev(gconst["f"]), to_dev(gconst["b"])), levels_of, bd, vbd)
    s_zero = jnp.zeros((n_ctx_b, GLA_HEADS, GLA_DK, GLA_DV), F32)
    of_c, ob_c, sf_new, sb_new = _gla(gq, la, gkt, lat, gv, s_zero, s_zero, consts, 0, n_ctx_b, ctx_seq)
    of_l, ob_l, _, _ = _gla(gq, la, gkt, lat, gv, state_gla_fwd[:, l], state_gla_bwd[:, l], consts,
                            n_ctx, n_lat_b, lat_seq)

    sw13 = jnp.concatenate([sh_w1[l], sh_w3[l]], axis=1).astype(BF16)
    base, u2_rows, logits_t = _out_projection(
        att_c, att_l, of_c, of_l, ob_c, ob_l, gg, x_c, x_l, mod3, w_out[l].astype(BF16),
        gla_norm[l][None, :], ln1_g[l][None, :], ln1_b[l][None, :], w_router[l].T.astype(BF16), sw13,
        sh_w2[l].astype(BF16), lat_seq // TOK_TILE)

    upper = jnp.asarray(np.triu(np.ones((TOK_TILE, TOK_TILE)), 1), BF16)
    idx_t, w_t, pos_t, counts = _route(logits_t, router_bias[l][:, None], upper)
    n_blocks = n * TOP_K // MOE_ROWS + N_EXPERTS
    n_blocks_pad = -(-(n_blocks + 1) // LANES) * LANES
    lower = jnp.asarray(np.tril(np.ones((N_EXPERTS, N_EXPERTS)), -1), F32)
    dest_t, bexp, bval, nused = _destinations(counts, lower, idx_t, pos_t, n_blocks_pad)
    w_flat = w_t.T.reshape(-1)

    ids = np.arange(n, dtype=np.int32)[None, :] * TOP_K + np.arange(TOP_K, dtype=np.int32)[:, None]
    row_src = _invert_rows(dest_t, ids, n_blocks * MOE_ROWS)
    gat, dst = _row_tables(row_src, bval, n_blocks, n)
    yt = _moe_experts(bexp.reshape(-1), bval.reshape(-1), nused.reshape(-1)[0:1], u2_rows, gat, dst,
                      exp_w1[l], exp_w3[l], exp_w2[l], n)
    y_c, y_l = _combine(w_flat, yt, base, mod3, ln2_g[l][None, :], ln2_b[l][None, :], n_ctx, lat_seq)

    y_prompt = y_c.reshape(n_ctx_b, ctx_seq, D_MODEL)
    y_sample = y_l.reshape(n_lat_b, lat_seq, D_MODEL)
    new_cache_k = k32.reshape(n_ctx_b, 1, ctx_seq, KV_HEADS, HEAD_DIM)
    new_cache_v = v32.reshape(n_ctx_b, 1, ctx_seq, KV_HEADS, HEAD_DIM)
    return (y_prompt, y_sample, new_cache_k, new_cache_v, sf_new[:, None], sb_new[:, None])
```

```python
import functools

import numpy as np
import jax
import jax.numpy as jnp
from jax import lax
from jax.experimental import pallas as pl
from jax.experimental.pallas import tpu as pltpu
from jax.experimental.pallas import tpu_sc as plsc

F32 = jnp.float32
BF16 = jnp.bfloat16
I32 = jnp.int32

D_MODEL = 1024
GRID_W = 64
HEAD_DIM = 64
N_HEADS = 8
KV_HEADS = 2
ATT_WIDTH = N_HEADS * HEAD_DIM
ATT_SCALE = HEAD_DIM ** -0.5
LOG2_E = 1.4426950408889634
ROPE_AXIS_DIM = HEAD_DIM // 2
ROPE_THETA = 10000.0
GLA_HEADS = 4
GLA_DK = 64
GLA_DV = 128
GLA_WIDTH = GLA_HEADS * GLA_DV
GLA_KW = GLA_HEADS * GLA_DK
GLA_GATE_RANK = 16
GLA_TAU = 16.0
N_EXPERTS = 256
TOP_K = 8
EXPERT_FF = 256
SHARED_FF = 256
ROUTED_SCALE = 2.5
DEPTH = 1
ALPHA = (2.0 * DEPTH) ** 0.25
EPS = 1e-6

LANES = 128
SUBLANES = 8
VMEM_BYTES = 64 * 1024 * 1024
VMEM_LIMIT = VMEM_BYTES - 8 * 1024 * 1024
PACK_CHUNKS = D_MODEL // (2 * LANES)
PACK_CHUNKS_LOG2 = PACK_CHUNKS.bit_length() - 1
HIGH_HALF = 0xFFFF0000
U32 = jnp.uint32

TOK_TILE = 512
ATT_TQ = 256
GLA_CHUNK = 128
GLA_LEVELS = ((32, 128), (8, 32), (2, 8), (1, 2))
MOE_ROWS = 256
MOE_VMEM_LIMIT = VMEM_BYTES - 2 * 1024 * 1024
TOP_K_LOG2 = TOP_K.bit_length() - 1
SMEM_SLICE_WORDS = 1024
SRC_GROUP = SMEM_SLICE_WORDS // MOE_ROWS
COMB_TILE = 256
COMB_UNROLL = 8
HIGHEST = lax.Precision.HIGHEST


def _cparams(n_axes):
    return pltpu.CompilerParams(dimension_semantics=("arbitrary",) * n_axes,
                                vmem_limit_bytes=VMEM_LIMIT)


def _silu(x):
    return x * jax.nn.sigmoid(x)


def _log_sigmoid(x):
    return jnp.minimum(x, 0.0) - jnp.log(1.0 + jnp.exp(-jnp.abs(x)))


def _dot_split(a, b):
    a_hi = a.astype(BF16)
    b_hi = b.astype(BF16)
    a_lo = (a - a_hi.astype(F32)).astype(BF16)
    b_lo = (b - b_hi.astype(F32)).astype(BF16)
    dot = functools.partial(jnp.dot, preferred_element_type=F32)
    return dot(a_hi, b_hi) + dot(a_lo, b_hi) + dot(a_hi, b_lo)


def _layer_norm(z, g, b):
    mu = jnp.mean(z, axis=-1, keepdims=True)
    zc = z - mu
    var = jnp.mean(zc * zc, axis=-1, keepdims=True)
    return zc * lax.rsqrt(var + EPS) * g + b


def _mod_kernel(c_ref, w_ref, b_ref, o_ref):
    s = _silu(c_ref[...]).astype(BF16)
    o_ref[...] = jnp.dot(s, w_ref[...].astype(BF16), preferred_element_type=F32) + b_ref[...]


def _modulation(c_rows, w_ada, b_ada):
    n_cols = w_ada.shape[1]
    tn = 512
    return pl.pallas_call(
        _mod_kernel,
        grid=(n_cols // tn,),
        in_specs=[pl.BlockSpec((SUBLANES, D_MODEL), lambda j: (0, 0)),
                  pl.BlockSpec((D_MODEL, tn), lambda j: (0, j)),
                  pl.BlockSpec((1, tn), lambda j: (0, j))],
        out_specs=pl.BlockSpec((SUBLANES, tn), lambda j: (0, j)),
        out_shape=jax.ShapeDtypeStruct((SUBLANES, n_cols), F32),
        compiler_params=_cparams(1),
        name="modulation",
    )(c_rows, w_ada, b_ada)


_C_K = 0
_C_V = _C_K + 2 * LANES
_C_GQ = _C_V + KV_HEADS * HEAD_DIM
_C_GV = _C_GQ + GLA_KW
_C_GG = _C_GV + GLA_WIDTH
_C_RA = _C_GG + GLA_WIDTH
_C_END = _C_RA + LANES
_R_Q = 0
_R_V = _R_Q + ATT_WIDTH
_R_GK = _R_V + KV_HEADS * HEAD_DIM
_R_RA = _R_GK + GLA_KW
_R_END = _R_RA + 2 * GLA_GATE_RANK


def _inproj_kernel(xc_ref, xl_ref, mod_ref, w_ref, wt_ref, qn_ref, kn_ref, cos_ref, sa_ref, sb_ref,
                   cost_ref, sint_ref, seg_ref, wa_ref, ba_ref, wat_ref, bat_ref,
                   qt_ref, k_ref, vt_ref, k32_ref, v32_ref, gq_ref, gv_ref, gg_ref,
                   la_ref, gkt_ref, lat_ref, *, n_ctx_tiles):
    i = pl.program_id(0)
    m = mod_ref[...]
    shift1 = m[:, 0:D_MODEL]
    scale1 = m[:, D_MODEL:2 * D_MODEL]
    x = jnp.where(i < n_ctx_tiles, xc_ref[...], xl_ref[...])
    u = (x * (1.0 + scale1) + shift1).astype(BF16)

    cos = cos_ref[...]
    sin_a = sa_ref[...]
    sin_b = sb_ref[...]
    seg = seg_ref[...]
    lane = lax.broadcasted_iota(I32, (u.shape[0], LANES), 1)
    low = lane < HEAD_DIM

    def proj(c0, c1):
        return jnp.dot(u, w_ref[:, c0:c1], preferred_element_type=F32)

    def head_norm(blk, gain):
        ss = jnp.dot((blk * blk).astype(BF16), seg, preferred_element_type=F32) * (1.0 / HEAD_DIM)
        return blk * lax.rsqrt(ss + EPS) * gain

    def rope(blk):
        return (blk * cos + pltpu.roll(blk, LANES - ROPE_AXIS_DIM // 2, 1) * sin_a
                + pltpu.roll(blk, ROPE_AXIS_DIM // 2, 1) * sin_b)

    pk = proj(_C_K, _C_V)
    kn = [head_norm(pk[:, j * LANES:(j + 1) * LANES], kn_ref[...]) for j in range(KV_HEADS)]
    for j in range(KV_HEADS):
        k_ref[:, j * LANES:(j + 1) * LANES] = rope(kn[j]).astype(BF16)

    @pl.when(i < n_ctx_tiles)
    def _():
        k32_ref[...] = jnp.where(low, kn[0], kn[1])
        v32_ref[...] = proj(_C_V, _C_GQ)

    gq_ref[...] = proj(_C_GQ, _C_GV) * (GLA_DK ** -0.5)
    gv_ref[...] = proj(_C_GV, _C_GG).astype(BF16)
    gg_ref[...] = proj(_C_GG, _C_RA).astype(BF16)

    ra = proj(_C_RA, _C_END)
    pre = _dot_split(ra, wa_ref[...]) + ba_ref[...]
    la_ref[...] = _log_sigmoid(pre) * (1.0 / GLA_TAU)

    pt = lax.dot_general(wt_ref[...], u, (((1,), (1,)), ((), ())), preferred_element_type=F32)
    cos_t = cost_ref[...]
    sin_t = sint_ref[...]
    quarter = ROPE_AXIS_DIM // 2
    for h in range(N_HEADS):
        blk = pt[_R_Q + h * HEAD_DIM:_R_Q + (h + 1) * HEAD_DIM, :]
        ms = jnp.mean(blk * blk, axis=0, keepdims=True)
        qn = blk * lax.rsqrt(ms + EPS) * qn_ref[...]
        rot = jnp.concatenate([-qn[quarter:2 * quarter], qn[0:quarter],
                               -qn[3 * quarter:4 * quarter], qn[2 * quarter:3 * quarter]], axis=0)
        qt_ref[h * HEAD_DIM:(h + 1) * HEAD_DIM, :] = (
            (qn * cos_t + rot * sin_t) * (ATT_SCALE * LOG2_E)).astype(BF16)
    vt_ref[...] = pt[_R_V:_R_GK, :].astype(BF16)
    gkt_ref[...] = pt[_R_GK:_R_RA, :]
    rat = pt[_R_RA:_R_END, :]
    pre_t = _dot_split(wat_ref[...], rat) + bat_ref[...]
    lat_ref[...] = _log_sigmoid(pre_t) * (1.0 / GLA_TAU)


def _in_projection(x_c, x_l, mod3, w_tok, w_tr, qn, kn, cos_t, sa_t, sb_t, cos_tr, sin_tr, seg, wa, ba,
                   wat, bat, n_seq_tiles):
    n_ctx = x_c.shape[0]
    n = n_ctx + x_l.shape[0]
    tb = TOK_TILE
    n_ctx_tiles = n_ctx // tb
    n_tiles = n // tb
    n_rope_blocks = cos_t.shape[0] // tb - 1

    def mod_idx(i):
        return (jnp.where(i < n_ctx_tiles, 0, 1 + (i - n_ctx_tiles) // n_seq_tiles), 0, 0)

    def rope_blk(i):
        return jnp.where(i < n_ctx_tiles, n_rope_blocks, (i - n_ctx_tiles) % n_seq_tiles)

    def rope_idx(i):
        return (rope_blk(i), 0)

    def ctx_idx(i):
        return (jnp.minimum(i, n_ctx_tiles - 1), 0)

    tok = lambda w: pl.BlockSpec((tb, w), lambda i: (i, 0))
    full = lambda a: pl.BlockSpec(a.shape, lambda i: (0,) * a.ndim)
    tr = lambda r: pl.BlockSpec((r, tb), lambda i: (0, i))
    rope_tr = pl.BlockSpec((HEAD_DIM, tb), lambda i: (0, rope_blk(i)))
    out_shapes = (
        jax.ShapeDtypeStruct((ATT_WIDTH, n), BF16),
        jax.ShapeDtypeStruct((n, 2 * LANES), BF16),
        jax.ShapeDtypeStruct((KV_HEADS * HEAD_DIM, n), BF16),
        jax.ShapeDtypeStruct((n_ctx, LANES), F32),
        jax.ShapeDtypeStruct((n_ctx, LANES), F32),
        jax.ShapeDtypeStruct((n, GLA_KW), F32),
        jax.ShapeDtypeStruct((n, GLA_WIDTH), BF16),
        jax.ShapeDtypeStruct((n, GLA_WIDTH), BF16),
        jax.ShapeDtypeStruct((n, 2 * GLA_KW), F32),
        jax.ShapeDtypeStruct((GLA_KW, n), F32),
        jax.ShapeDtypeStruct((2 * GLA_KW, n), F32),
    )
    out_specs = (tr(ATT_WIDTH), tok(2 * LANES), tr(KV_HEADS * HEAD_DIM),
                 pl.BlockSpec((tb, LANES), ctx_idx), pl.BlockSpec((tb, LANES), ctx_idx),
                 tok(GLA_KW), tok(GLA_WIDTH), tok(GLA_WIDTH), tok(2 * GLA_KW),
                 tr(GLA_KW), tr(2 * GLA_KW))
    in_specs = [pl.BlockSpec((tb, D_MODEL), ctx_idx),
                pl.BlockSpec((tb, D_MODEL), lambda i: (jnp.maximum(i - n_ctx_tiles, 0), 0)),
                pl.BlockSpec((None, 1, mod3.shape[2]), mod_idx),
                full(w_tok), full(w_tr), full(qn), full(kn),
                pl.BlockSpec((tb, LANES), rope_idx), pl.BlockSpec((tb, LANES), rope_idx),
                pl.BlockSpec((tb, LANES), rope_idx), rope_tr, rope_tr,
                full(seg), full(wa), full(ba), full(wat), full(bat)]
    return pl.pallas_call(
        functools.partial(_inproj_kernel, n_ctx_tiles=n_ctx_tiles),
        grid=(n_tiles,), in_specs=in_specs, out_specs=out_specs, out_shape=out_shapes,
        compiler_params=_cparams(1), name="in_projection",
    )(x_c, x_l, mod3, w_tok, w_tr, qn, kn, cos_t, sa_t, sb_t, cos_tr, sin_tr, seg, wa, ba, wat, bat)


def _attention_kernel(*refs, n_kv_parts):
    qt_ref = refs[0]
    k_refs = refs[1:1 + n_kv_parts]
    vt_refs = refs[1 + n_kv_parts:1 + 2 * n_kv_parts]
    o_ref = refs[1 + 2 * n_kv_parts]
    tq = qt_ref.shape[1]
    group = N_HEADS // KV_HEADS
    for kv in range(KV_HEADS):
        heads = range(kv * group, (kv + 1) * group)
        q_grp = jnp.concatenate([qt_ref[h * HEAD_DIM:(h + 1) * HEAD_DIM, :] for h in heads], axis=1)
        rhs = jnp.concatenate([q_grp, jnp.zeros_like(q_grp)], axis=0)
        s = [jnp.dot(k[:, kv * LANES:(kv + 1) * LANES], rhs, preferred_element_type=F32)
             for k in k_refs]
        mx = functools.reduce(jnp.maximum, [jnp.max(x, axis=0, keepdims=True) for x in s])
        pr = [jnp.exp2(x - mx) for x in s]
        den = functools.reduce(jnp.add, [jnp.sum(x, axis=0, keepdims=True) for x in pr])
        acc = functools.reduce(jnp.add, [
            jnp.dot(vt[kv * HEAD_DIM:(kv + 1) * HEAD_DIM, :], x.astype(BF16),
                    preferred_element_type=F32) for x, vt in zip(pr, vt_refs)])
        out = (acc / den).astype(BF16)
        for j, h in enumerate(heads):
            o_ref[h * HEAD_DIM:(h + 1) * HEAD_DIM, :] = out[:, j * tq:(j + 1) * tq]


def _attention(qt, k, vt, extra_kv, row0, n_batch, seq):
    tq = ATT_TQ
    n_q = seq // tq
    q_blk0 = row0 // tq
    kv_blk0 = row0 // seq
    in_specs = [pl.BlockSpec((ATT_WIDTH, tq), lambda b, i: (0, q_blk0 + b * n_q + i))]
    k_spec = pl.BlockSpec((seq, 2 * LANES), lambda b, i: (kv_blk0 + b, 0))
    vt_spec = pl.BlockSpec((KV_HEADS * HEAD_DIM, seq), lambda b, i: (0, kv_blk0 + b))
    args_k, args_v, specs_k, specs_v = [k], [vt], [k_spec], [vt_spec]
    if extra_kv is not None:
        ck, cvt = extra_kv
        args_k.append(ck)
        args_v.append(cvt)
        specs_k.append(pl.BlockSpec((None, ck.shape[1], 2 * LANES), lambda b, i: (b, 0, 0)))
        specs_v.append(pl.BlockSpec((None, KV_HEADS * HEAD_DIM, cvt.shape[2]), lambda b, i: (b, 0, 0)))
    return pl.pallas_call(
        functools.partial(_attention_kernel, n_kv_parts=len(args_k)),
        grid=(n_batch, n_q),
        in_specs=in_specs + specs_k + specs_v,
        out_specs=pl.BlockSpec((ATT_WIDTH, tq), lambda b, i: (0, b * n_q + i)),
        out_shape=jax.ShapeDtypeStruct((ATT_WIDTH, n_batch * seq), BF16),
        compiler_params=_cparams(2), name="attention",
    )(qt, *args_k, *args_v)


def _gla_constants():
    c = GLA_CHUNK
    idx = np.arange(c)
    q_mats, k_mats, masks, levels_of = [], [], [], []
    for li, (s, p) in enumerate(GLA_LEVELS):
        start = (idx // s) * s
        end = start + s - 1
        k_mats.append(((idx[None, :] > idx[:, None]) & (idx[None, :] <= end[:, None])))
        for d in range(p // s - 1):
            lo = np.maximum(start - d * s, 0)
            q_mats.append((idx[None, :] >= lo[:, None]) & (idx[None, :] <= idx[:, None]))
            masks.append((idx[:, None] // p == idx[None, :] // p)
                         & (idx[:, None] // s - idx[None, :] // s - 1 == d))
            levels_of.append(li)
    masks.append(np.eye(c, dtype=bool))
    levels_of.append(len(GLA_LEVELS) - 1)
    q_mats.append(idx[None, :] <= idx[:, None])
    k_mats = k_mats[:-1]
    k_mats.append(idx[None, :] > idx[:, None])
    k_mats.append(np.ones((c, c), bool))
    out = {}
    for name, flip in (("f", False), ("b", True)):
        f = (lambda a: a[::-1, ::-1]) if flip else (lambda a: a)
        lq = np.concatenate([f(a) for a in q_mats], axis=0).astype(np.float32)
        lkt = np.concatenate([f(a).T for a in k_mats], axis=1).astype(np.float32)
        mk = np.stack([np.tile(f(a), (1, GLA_HEADS)) for a in masks]).astype(np.float32)
        out[name] = (np.concatenate([lq, lq], axis=1), np.concatenate([lkt, lkt], axis=0), mk)
    return out, tuple(levels_of)


def _gla_direction(q, g, gkt, gt, v, lq2, lkt2, masks_ref, bd, vbd, s_ref, levels_of):
    c = GLA_CHUNK
    n_var = len(levels_of)
    n_lev = len(GLA_LEVELS)
    g_hi = g.astype(BF16)
    g_lo = (g - g_hi.astype(F32)).astype(BF16)
    fq = jnp.dot(lq2, jnp.concatenate([g_hi, g_lo], axis=0), preferred_element_type=F32)
    gt_hi = gt.astype(BF16)
    gt_lo = (gt - gt_hi.astype(F32)).astype(BF16)
    fk = jnp.dot(jnp.concatenate([gt_hi, gt_lo], axis=1), lkt2, preferred_element_type=F32)

    def key_factor(f):
        return gkt * jnp.exp(fk[:, f * c:(f + 1) * c])

    q_var = [(q * jnp.exp(fq[vi * c:(vi + 1) * c, :])).astype(BF16) for vi in range(n_var - 1)]
    q_var.append(q.astype(BF16))
    a = jnp.zeros((c, GLA_HEADS * c), F32)
    for li in range(n_lev):
        kt = (key_factor(li) if li < n_lev - 1 else gkt).astype(BF16)
        xt = jnp.concatenate([kt] * GLA_HEADS, axis=1) * bd
        vis = [vi for vi in range(n_var) if levels_of[vi] == li]
        res = jnp.dot(jnp.concatenate([q_var[vi] for vi in vis], axis=0), xt,
                      preferred_element_type=F32)
        for r, vi in enumerate(vis):
            a = a + masks_ref[vi] * res[r * c:(r + 1) * c, :]
    q_in = (q * jnp.exp(fq[(n_var - 1) * c:n_var * c, :])).astype(BF16)
    state = s_ref[...]
    v_bd = jnp.concatenate([v] * GLA_HEADS, axis=0) * vbd
    o = (jnp.dot(q_in, state.astype(BF16), preferred_element_type=F32)
         + jnp.dot(a.astype(BF16), v_bd, preferred_element_type=F32))
    k_out = key_factor(n_lev - 1).astype(BF16)
    e_tot = jnp.exp(fk[:, n_lev * c:(n_lev + 1) * c])
    upd = jnp.dot(k_out, v, preferred_element_type=F32)
    s_ref[...] = (state * jnp.concatenate([e_tot] * (GLA_WIDTH // c), axis=1)
                  + upd * bd.astype(F32))
    return o


def _gla_kernel(gq_f, la_f, gkt_f, lat_f, gv_f, gq_b, la_b, gkt_b, lat_b, gv_b,
                s0f_ref, s0b_ref, lq2f, lkt2f, mkf, lq2b, lkt2b, mkb, bd_ref, vbd_ref,
                of_ref, ob_ref, sf_ref, sb_ref, st_f, st_b, *, levels_of):
    n = pl.program_id(1)

    @pl.when(n == 0)
    def _():
        st_f[...] = jnp.zeros_like(st_f)
        st_b[...] = jnp.zeros_like(st_b)
        for h in range(GLA_HEADS):
            rows = slice(h * GLA_DK, (h + 1) * GLA_DK)
            cols = slice(h * GLA_DV, (h + 1) * GLA_DV)
            st_f[rows, cols] = s0f_ref[h]
            st_b[rows, cols] = s0b_ref[h]

    bd = bd_ref[...]
    vbd = vbd_ref[...]
    of_ref[...] = _gla_direction(gq_f[...], la_f[...], gkt_f[...], lat_f[...], gv_f[...],
                                 lq2f[...], lkt2f[...], mkf, bd, vbd, st_f, levels_of)
    ob_ref[...] = _gla_direction(gq_b[...], la_b[...], gkt_b[...], lat_b[...], gv_b[...],
                                 lq2b[...], lkt2b[...], mkb, bd, vbd, st_b, levels_of)

    @pl.when(n == pl.num_programs(1) - 1)
    def _():
        for h in range(GLA_HEADS):
            rows = slice(h * GLA_DK, (h + 1) * GLA_DK)
            cols = slice(h * GLA_DV, (h + 1) * GLA_DV)
            sf_ref[h] = st_f[rows, cols]
            sb_ref[h] = st_b[rows, cols]


def _gla(gq, la, gkt, lat, gv, s0f, s0b, consts, row0, n_batch, seq):
    (cf, cb), levels_of, bd, vbd = consts
    c = GLA_CHUNK
    nc = seq // c
    blk0 = row0 // c
    n_la_blocks_b = 1
    fwd = lambda b, n: blk0 + b * nc + n
    bwd = lambda b, n: blk0 + b * nc + (nc - 1 - n)

    def tok(w, which, col=0):
        return pl.BlockSpec((c, w), lambda b, n: (which(b, n), col))

    def tr(r, which, row=0):
        return pl.BlockSpec((r, c), lambda b, n: (row, which(b, n)))

    full = lambda a: pl.BlockSpec(a.shape, lambda b, n: (0,) * a.ndim)
    st_spec = pl.BlockSpec((None, GLA_HEADS, GLA_DK, GLA_DV), lambda b, n: (b, 0, 0, 0))
    in_specs = [tok(GLA_KW, fwd), tok(GLA_KW, fwd, 0), tr(GLA_KW, fwd), tr(GLA_KW, fwd, 0),
                tok(GLA_WIDTH, fwd),
                tok(GLA_KW, bwd), tok(GLA_KW, bwd, n_la_blocks_b), tr(GLA_KW, bwd),
                tr(GLA_KW, bwd, 1), tok(GLA_WIDTH, bwd),
                st_spec, st_spec,
                full(cf[0]), full(cf[1]), full(cf[2]), full(cb[0]), full(cb[1]), full(cb[2]),
                full(bd), full(vbd)]
    out_specs = (pl.BlockSpec((c, GLA_WIDTH), lambda b, n: (b * nc + n, 0)),
                 pl.BlockSpec((c, GLA_WIDTH), lambda b, n: (b * nc + (nc - 1 - n), 0)),
                 st_spec, st_spec)
    out_shape = (jax.ShapeDtypeStruct((n_batch * seq, GLA_WIDTH), F32),
                 jax.ShapeDtypeStruct((n_batch * seq, GLA_WIDTH), F32),
                 jax.ShapeDtypeStruct((n_batch, GLA_HEADS, GLA_DK, GLA_DV), F32),
                 jax.ShapeDtypeStruct((n_batch, GLA_HEADS, GLA_DK, GLA_DV), F32))
    return pl.pallas_call(
        functools.partial(_gla_kernel, levels_of=levels_of),
        grid=(n_batch, nc), in_specs=in_specs, out_specs=out_specs, out_shape=out_shape,
        scratch_shapes=[pltpu.VMEM((GLA_KW, GLA_WIDTH), F32), pltpu.VMEM((GLA_KW, GLA_WIDTH), F32)],
        compiler_params=_cparams(2), name="gla",
    )(gq, la, gkt, lat, gv, gq, la, gkt, lat, gv, s0f, s0b,
      cf[0], cf[1], cf[2], cb[0], cb[1], cb[2], bd, vbd)


def _outproj_kernel(attc_ref, attl_ref, ofc_ref, ofl_ref, obc_ref, obl_ref, gg_ref, xc_ref, xl_ref,
                    mod_ref, wo_ref, gn_ref, l1g_ref, l1b_ref, wrt_ref,
                    base_ref, u2_ref, lg_ref, *, n_ctx_tiles):
    is_ctx = pl.program_id(0) < n_ctx_tiles
    pick = lambda a_ref, b_ref: jnp.where(is_ctx, a_ref[...], b_ref[...])
    m = mod_ref[...]
    gate1 = m[:, 2 * D_MODEL:3 * D_MODEL]
    shift2 = m[:, 3 * D_MODEL:4 * D_MODEL]
    scale2 = m[:, 4 * D_MODEL:5 * D_MODEL]
    og = pick(ofc_ref, ofl_ref) + pick(obc_ref, obl_ref)
    gg = gg_ref[...].astype(F32)
    parts = []
    for h in range(GLA_HEADS):
        blk = og[:, h * GLA_DV:(h + 1) * GLA_DV]
        ms = jnp.mean(blk * blk, axis=-1, keepdims=True)
        nb = blk * lax.rsqrt(ms + EPS) * gn_ref[...]
        parts.append((nb * _silu(gg[:, h * GLA_DV:(h + 1) * GLA_DV])).astype(BF16))
    att_t = pick(attc_ref, attl_ref)
    hmix = (lax.dot_general(att_t, wo_ref[0:ATT_WIDTH, :], (((0,), (0,)), ((), ())),
                            preferred_element_type=F32)
            + jnp.dot(jnp.concatenate(parts, axis=1), wo_ref[ATT_WIDTH:, :],
                      preferred_element_type=F32))
    x1 = _layer_norm(ALPHA * pick(xc_ref, xl_ref) + gate1 * hmix, l1g_ref[...], l1b_ref[...])
    u2 = x1 * (1.0 + scale2) + shift2
    u2b = u2.astype(BF16)
    lg_ref[...] = lax.dot_general(wrt_ref[...], u2b, (((1,), (1,)), ((), ())),
                                  preferred_element_type=F32)
    base_ref[...] = ALPHA * x1
    _pack_rows(u2_ref, u2)


def _out_projection(att_c, att_l, of_c, of_l, ob_c, ob_l, gg, x_c, x_l, mod3, wo, gn, l1g, l1b, wrt,
                    n_seq_tiles):
    n_ctx = x_c.shape[0]
    n = n_ctx + x_l.shape[0]
    tb = TOK_TILE
    n_ctx_tiles = n_ctx // tb

    def mod_idx(i):
        return (jnp.where(i < n_ctx_tiles, 0, 1 + (i - n_ctx_tiles) // n_seq_tiles), 0, 0)

    ctx_blk = lambda i: jnp.minimum(i, n_ctx_tiles - 1)
    lat_blk = lambda i: jnp.maximum(i - n_ctx_tiles, 0)
    tok = lambda w: pl.BlockSpec((tb, w), lambda i: (i, 0))
    tok_c = lambda w: pl.BlockSpec((tb, w), lambda i: (ctx_blk(i), 0))
    tok_l = lambda w: pl.BlockSpec((tb, w), lambda i: (lat_blk(i), 0))
    full = lambda a: pl.BlockSpec(a.shape, lambda i: (0,) * a.ndim)
    return pl.pallas_call(
        functools.partial(_outproj_kernel, n_ctx_tiles=n_ctx_tiles),
        grid=(n // tb,),
        in_specs=[pl.BlockSpec((ATT_WIDTH, tb), lambda i: (0, ctx_blk(i))),
                  pl.BlockSpec((ATT_WIDTH, tb), lambda i: (0, lat_blk(i))),
                  tok_c(GLA_WIDTH), tok_l(GLA_WIDTH), tok_c(GLA_WIDTH), tok_l(GLA_WIDTH),
                  tok(GLA_WIDTH), tok_c(D_MODEL), tok_l(D_MODEL),
                  pl.BlockSpec((None, 1, mod3.shape[2]), mod_idx),
                  full(wo), full(gn), full(l1g), full(l1b), full(wrt)],
        out_specs=(tok(D_MODEL),
                   pl.BlockSpec((tb * PACK_CHUNKS, LANES), lambda i: (i, 0)),
                   pl.BlockSpec((N_EXPERTS, tb), lambda i: (0, i))),
        out_shape=(jax.ShapeDtypeStruct((n, D_MODEL), F32),
                   jax.ShapeDtypeStruct((n * PACK_CHUNKS, LANES), U32),
                   jax.ShapeDtypeStruct((N_EXPERTS, n), F32)),
        compiler_params=_cparams(1), name="out_projection",
    )(att_c, att_l, of_c, of_l, ob_c, ob_l, gg, x_c, x_l, mod3, wo, gn, l1g, l1b, wrt)


def _shared_kernel(u2_ref, base_ref, mod_ref, sw13_ref, sw2_ref, out_ref):
    gate2 = mod_ref[:, 5 * D_MODEL:6 * D_MODEL]
    u2b = _unpack_rows(u2_ref, out_ref.shape[0])
    ab = jnp.dot(u2b, sw13_ref[...], preferred_element_type=F32)
    hid = (_silu(ab[:, 0:SHARED_FF]) * ab[:, SHARED_FF:2 * SHARED_FF]).astype(BF16)
    shared = jnp.dot(hid, sw2_ref[...], preferred_element_type=F32)
    out_ref[...] = base_ref[...] + gate2 * shared


def _shared_expert(u2p, base, mod3, sw13, sw2, n_ctx, n_seq_tiles):
    n = base.shape[0]
    tb = TOK_TILE
    n_ctx_tiles = n_ctx // tb

    def mod_idx(i):
        return (jnp.where(i < n_ctx_tiles, 0, 1 + (i - n_ctx_tiles) // n_seq_tiles), 0, 0)

    full = lambda a: pl.BlockSpec(a.shape, lambda i: (0,) * a.ndim)
    tok = pl.BlockSpec((tb, D_MODEL), lambda i: (i, 0))
    return pl.pallas_call(
        _shared_kernel,
        grid=(n // tb,),
        in_specs=[pl.BlockSpec((tb * PACK_CHUNKS, LANES), lambda i: (i, 0)), tok,
                  pl.BlockSpec((None, 1, mod3.shape[2]), mod_idx), full(sw13), full(sw2)],
        out_specs=tok,
        out_shape=jax.ShapeDtypeStruct(base.shape, F32),
        input_output_aliases={1: 0},
        compiler_params=_cparams(1), name="shared_expert",
    )(u2p, base, mod3, sw13, sw2)


def _route_kernel(lg_ref, bias_ref, upper_ref, idx_ref, w_ref, pos_ref, cnt_ref, run_ref):
    i = pl.program_id(0)

    @pl.when(i == 0)
    def _():
        run_ref[...] = jnp.zeros_like(run_ref)

    s = jax.nn.sigmoid(lg_ref[...])
    work = s + bias_ref[...]
    rows = lax.broadcasted_iota(I32, s.shape, 0).astype(F32)
    sel = jnp.zeros(s.shape, F32)
    idxs, vals = [], []
    for _ in range(TOP_K):
        mx = jnp.max(work, axis=0, keepdims=True)
        idx = jnp.min(jnp.where(work == mx, rows, float(N_EXPERTS)), axis=0, keepdims=True)
        hit = rows == idx
        vals.append(jnp.sum(jnp.where(hit, s, 0.0), axis=0, keepdims=True))
        idxs.append(idx)
        sel = jnp.where(hit, 1.0, sel)
        work = jnp.where(hit, -jnp.inf, work)
    den = functools.reduce(jnp.add, vals)
    rank = jnp.dot(sel.astype(BF16), upper_ref[...], preferred_element_type=F32) + run_ref[:, 0:1]
    for k in range(TOP_K):
        idx_ref[k:k + 1, :] = idxs[k].astype(I32)
        w_ref[k:k + 1, :] = vals[k] / den * ROUTED_SCALE
        pos_ref[k:k + 1, :] = jnp.sum(jnp.where(rows == idxs[k], rank, 0.0), axis=0,
                                      keepdims=True).astype(I32)
    run_ref[...] = run_ref[...] + jnp.sum(sel, axis=1, keepdims=True)
    cnt_ref[...] = run_ref[...]


def _route(logits_t, bias_col, upper):
    n = logits_t.shape[1]
    tt = TOK_TILE
    row = lambda dt: jax.ShapeDtypeStruct((TOP_K, n), dt)
    blk = pl.BlockSpec((TOP_K, tt), lambda i: (0, i))
    return pl.pallas_call(
        _route_kernel,
        grid=(n // tt,),
        in_specs=[pl.BlockSpec((N_EXPERTS, tt), lambda i: (0, i)),
                  pl.BlockSpec((N_EXPERTS, 1), lambda i: (0, 0)),
                  pl.BlockSpec((tt, tt), lambda i: (0, 0))],
        out_specs=(blk, blk, blk, pl.BlockSpec((N_EXPERTS, LANES), lambda i: (0, 0))),
        out_shape=(row(I32), row(F32), row(I32), jax.ShapeDtypeStruct((N_EXPERTS, LANES), F32)),
        scratch_shapes=[pltpu.VMEM((N_EXPERTS, LANES), F32)],
        compiler_params=_cparams(1), name="route",
    )(logits_t, bias_col, upper)


def _dest_kernel(cnt_ref, lower_ref, idx_ref, pos_ref, dest_ref, bexp_ref, bval_ref, nused_ref):
    cnt = cnt_ref[...]
    nblk = jnp.floor((cnt + (MOE_ROWS - 1)) * (1.0 / MOE_ROWS))
    bstart = jnp.dot(lower_ref[...], nblk, precision=HIGHEST, preferred_element_type=F32)
    bend = bstart + nblk
    pstart = bstart[:, 0:1] * MOE_ROWS
    rows = lax.broadcasted_iota(I32, (N_EXPERTS, idx_ref.shape[1]), 0)
    for k in range(TOP_K):
        hit = rows == idx_ref[k:k + 1, :]
        dest_ref[k:k + 1, :] = (jnp.sum(jnp.where(hit, pstart, 0.0), axis=0, keepdims=True)
                                .astype(I32) + pos_ref[k:k + 1, :])

    @pl.when(pl.program_id(0) == 0)
    def _():
        nb = bexp_ref.shape[1]
        bid = lax.broadcasted_iota(I32, (N_EXPERTS, nb), 1).astype(F32)
        inside = jnp.logical_and(bid >= bstart[:, 0:1], bid < bend[:, 0:1])
        erow = lax.broadcasted_iota(I32, (N_EXPERTS, nb), 0).astype(F32)
        bexp_ref[...] = jnp.sum(jnp.where(inside, erow, 0.0), axis=0, keepdims=True).astype(I32)
        valid = jnp.clip(cnt[:, 0:1] - (bid - bstart[:, 0:1]) * MOE_ROWS, 0.0, float(MOE_ROWS))
        bval_ref[...] = jnp.sum(jnp.where(inside, valid, 0.0), axis=0, keepdims=True).astype(I32)
        nused_ref[...] = jnp.max(bend, axis=0, keepdims=True).astype(I32)


def _destinations(counts, lower, idx_t, pos_t, n_blocks_pad):
    n = idx_t.shape[1]
    tt = TOK_TILE
    blk = pl.BlockSpec((TOP_K, tt), lambda i: (0, i))
    one = lambda w: pl.BlockSpec((1, w), lambda i: (0, 0))
    return pl.pallas_call(
        _dest_kernel,
        grid=(n // tt,),
        in_specs=[pl.BlockSpec((N_EXPERTS, LANES), lambda i: (0, 0)),
                  pl.BlockSpec((N_EXPERTS, N_EXPERTS), lambda i: (0, 0)), blk, blk],
        out_specs=(blk, one(n_blocks_pad), one(n_blocks_pad), one(LANES)),
        out_shape=(jax.ShapeDtypeStruct((TOP_K, n), I32),
                   jax.ShapeDtypeStruct((1, n_blocks_pad), I32),
                   jax.ShapeDtypeStruct((1, n_blocks_pad), I32),
                   jax.ShapeDtypeStruct((1, LANES), I32)),
        compiler_params=_cparams(1), name="destinations",
    )(counts, lower, idx_t, pos_t)


SC_WINDOW = 128
SC_WINDOWS_PER_STEP = 8


def _invert_rows(dest, ids, n_rows):
    m = dest.size
    mesh = plsc.VectorSubcoreMesh(core_axis_name="core", subcore_axis_name="subcore")

    @functools.partial(pl.kernel, out_type=jax.ShapeDtypeStruct((n_rows,), I32), mesh=mesh,
                       scratch_types=[])
    def invert(val_hbm, idx_hbm, out_hbm):
        def body(val_vmem, idx_vmem):
            for j in range(SC_WINDOWS_PER_STEP):
                pltpu.sync_copy(val_vmem.at[j], out_hbm.at[idx_vmem.at[j]])

        blk = pl.BlockSpec((SC_WINDOWS_PER_STEP, SC_WINDOW), lambda i: (i, 0))
        pltpu.emit_pipeline(
            body, grid=(m // (SC_WINDOW * SC_WINDOWS_PER_STEP),),
            in_specs=[blk, blk], out_specs=[], core_axis_name=("core", "subcore"),
            dimension_semantics=(pltpu.PARALLEL,),
        )(val_hbm, idx_hbm)

    shape = (m // SC_WINDOW, SC_WINDOW)
    return invert(jnp.asarray(ids.reshape(shape), I32), dest.reshape(shape))


def _pack_rows(ref, x, row0=0):
    bits = pltpu.bitcast(x.astype(BF16).astype(F32), U32)
    for s in range(PACK_CHUNKS):
        lo = bits[:, (2 * s) * LANES:(2 * s + 1) * LANES] >> 16
        hi = bits[:, (2 * s + 1) * LANES:(2 * s + 2) * LANES] & jnp.uint32(HIGH_HALF)
        ref[pl.ds(row0 + s, x.shape[0], stride=PACK_CHUNKS), :] = lo | hi


def _unpack_rows(ref, n_rows, row0=0):
    parts = []
    for s in range(PACK_CHUNKS):
        w = ref[pl.ds(row0 + s, n_rows, stride=PACK_CHUNKS), :]
        parts.append(pltpu.bitcast(w << 16, F32))
        parts.append(pltpu.bitcast(w & jnp.uint32(HIGH_HALF), F32))
    return jnp.concatenate(parts, axis=1).astype(BF16)


def _moe_kernel(bexp_ref, bval_ref, nused_ref, u2p_hbm, gat_hbm, dst_hbm, w1_hbm, w3_hbm, w2_hbm,
                yt_hbm, u2p_vmem, w1_f, w3_f, w2_f, w13_s, w2_s, xbuf, ybuf, gat_smem, dst_smem,
                sem_in, sem_src, sem_w, sem_out):
    n_used = nused_ref[0]
    br = MOE_ROWS
    grp = SRC_GROUP * br

    def src_copies(g):
        window = pl.ds(g * grp, grp)
        ring = pl.ds(lax.rem(g, 2) * grp, grp)
        return (pltpu.make_async_copy(gat_hbm.at[window], gat_smem.at[ring], sem_src),
                pltpu.make_async_copy(dst_hbm.at[window], dst_smem.at[ring], sem_src))

    def out_wait(slot):
        pltpu.make_async_copy(ybuf.at[pl.ds(slot * br * PACK_CHUNKS, br * PACK_CHUNKS)],
                              yt_hbm.at[pl.ds(0, br * PACK_CHUNKS)], sem_out.at[slot]).wait()

    def src_base(blk):
        return lax.rem(blk // SRC_GROUP, 2) * grp + lax.rem(blk, SRC_GROUP) * br

    def scatter_row(blk_slot, sbase, r, priority=0):
        pltpu.make_async_copy(
            ybuf.at[pl.ds(pl.multiple_of((blk_slot * br + r) * PACK_CHUNKS, PACK_CHUNKS), PACK_CHUNKS)],
            yt_hbm.at[pl.ds(pl.multiple_of(dst_smem[sbase + r], PACK_CHUNKS), PACK_CHUNKS)],
            sem_out.at[blk_slot]).start(priority=priority)

    def gather_row(xslot, sbase, r):
        dst = pl.multiple_of((xslot * br + r) * PACK_CHUNKS, PACK_CHUNKS)
        row = pl.multiple_of(gat_smem[sbase + r], PACK_CHUNKS)
        xbuf[pl.ds(dst, PACK_CHUNKS), :] = u2p_vmem[pl.ds(row, PACK_CHUNKS), :]

    def weight_copies(e, wslot):
        return [pltpu.make_async_copy(src.at[e], dst.at[wslot], sem_w.at[wslot])
                for src, dst in ((w1_hbm, w1_f), (w3_hbm, w3_f), (w2_hbm, w2_f))]

    cp = pltpu.make_async_copy(u2p_hbm, u2p_vmem, sem_in)
    cp.start()
    for scp in src_copies(0):
        scp.start()
    for wcp in weight_copies(bexp_ref[0], 0):
        wcp.start()
    ybuf[...] = jnp.zeros_like(ybuf)
    cp.wait()
    for scp in src_copies(0):
        scp.wait()
    lax.fori_loop(0, br, lambda r, c: (gather_row(0, 0, r), c)[1], 0)

    def block(b, wslot):
        g = b // SRC_GROUP
        phase = lax.rem(b, SRC_GROUP)

        more = (g + 1) * SRC_GROUP < n_used

        @pl.when(jnp.logical_and(phase == 1, more))
        def _():
            for scp in src_copies(g + 1):
                scp.start()

        @pl.when(jnp.logical_and(phase == SRC_GROUP - 1, more))
        def _():
            for scp in src_copies(g + 1):
                scp.wait()

        e = bexp_ref[b]
        prev = bexp_ref[jnp.maximum(b - 1, 0)]

        @pl.when(jnp.logical_or(b == 0, e != prev))
        def _():
            for wcp in weight_copies(e, wslot):
                wcp.wait()
            w13_s[:, 0:EXPERT_FF] = w1_f[wslot].astype(BF16)
            w13_s[:, EXPERT_FF:2 * EXPERT_FF] = w3_f[wslot].astype(BF16)
            w2_s[...] = w2_f[wslot].astype(BF16)
            nxt = lax.while_loop(
                lambda j: jnp.logical_and(j < n_used, bexp_ref[jnp.minimum(j, n_used - 1)] == e),
                lambda j: j + 1, b + 1)

            @pl.when(nxt < n_used)
            def _():
                for wcp in weight_copies(bexp_ref[nxt], 1 - wslot):
                    wcp.start()

        switch = jnp.logical_and(b + 1 < n_used, bexp_ref[b + 1] != e)

        valid = bval_ref[b]
        slot = lax.rem(b, 2)

        p_base = src_base(b)
        n_base = src_base(b + 1)
        for r in range(br):
            gather_row(1 - slot, n_base, r)
            scatter_row(1 - slot, p_base, r, priority=r % 2)

        x = _unpack_rows(xbuf, br, slot * (br * PACK_CHUNKS))
        rows = lax.broadcasted_iota(I32, x.shape, 0)
        x = jnp.where(rows < valid, x, jnp.zeros_like(x))
        ab = jnp.dot(x, w13_s[...], preferred_element_type=F32)
        hid = (_silu(ab[:, 0:EXPERT_FF]) * ab[:, EXPERT_FF:2 * EXPERT_FF]).astype(BF16)
        y = jnp.dot(hid, w2_s[...], preferred_element_type=F32)

        @pl.when(b >= 1)
        def _():
            out_wait(slot)

        _pack_rows(ybuf, y, slot * (br * PACK_CHUNKS))
        return jnp.where(switch, 1 - wslot, wslot)

    lax.fori_loop(0, n_used, block, 0)

    last = n_used - 1
    l_slot = lax.rem(last, 2)
    for scp in src_copies(n_used // SRC_GROUP):
        scp.start()
    for scp in src_copies(n_used // SRC_GROUP):
        scp.wait()
    l_base = src_base(n_used)
    lax.fori_loop(0, br, lambda r, c: (scatter_row(l_slot, l_base, r), c)[1], 0)
    out_wait(1 - l_slot)
    out_wait(l_slot)


def _row_tables_kernel(src_ref, srcp_ref, bvalp_ref, gat_ref, dst_ref, *, n_tokens):
    src = src_ref[...]
    row = lax.shift_right_logical(src, TOP_K_LOG2 - PACK_CHUNKS_LOG2)
    row = row & (int(jnp.iinfo(I32).max) - (PACK_CHUNKS - 1))
    gat_ref[...] = jnp.minimum(row, (n_tokens - 1) * PACK_CHUNKS)
    blk = lax.broadcasted_iota(I32, srcp_ref.shape, 0)
    lane = lax.broadcasted_iota(I32, srcp_ref.shape, 1)
    spare = n_tokens * TOP_K + ((blk + 1) & 1) * MOE_ROWS + lane
    dst_ref[...] = jnp.where(lane < bvalp_ref[...], srcp_ref[...], spare) * PACK_CHUNKS


def _row_tables(row_src, bval, n_blocks, n_tokens):
    n_tab = -(-(n_blocks + 1) // SRC_GROUP) * SRC_GROUP
    src2 = row_src.reshape(n_blocks, MOE_ROWS)
    pad = ((1, n_tab - n_blocks - 1), (0, 0))
    src_tab = jnp.pad(src2, ((0, n_tab - n_blocks), (0, 0)))
    srcp = jnp.pad(src2, pad)
    bvalp = jnp.pad(bval.reshape(-1)[:n_blocks, None], pad)
    full = lambda a: pl.BlockSpec(a.shape, lambda i: (0,) * a.ndim)
    tab = jax.ShapeDtypeStruct((n_tab, MOE_ROWS), I32)
    gat, dst = pl.pallas_call(
        functools.partial(_row_tables_kernel, n_tokens=n_tokens),
        grid=(1,), in_specs=[full(src_tab), full(srcp), full(bvalp)],
        out_specs=(full(src_tab), full(src_tab)), out_shape=(tab, tab),
        compiler_params=_cparams(1), name="row_tables",
    )(src_tab, srcp, bvalp)
    return gat.reshape(-1), dst.reshape(-1)


def _moe_experts(bexp, bval, nused, u2p, gat, dst, w1, w3, w2, n_tokens):
    br = MOE_ROWS
    any_spec = pl.BlockSpec(memory_space=pl.ANY)
    grid_spec = pltpu.PrefetchScalarGridSpec(
        num_scalar_prefetch=3, grid=(1,),
        in_specs=[any_spec] * 6,
        out_specs=any_spec,
        scratch_shapes=[pltpu.VMEM(u2p.shape, U32),
                        pltpu.VMEM((2, D_MODEL, EXPERT_FF), F32),
                        pltpu.VMEM((2, D_MODEL, EXPERT_FF), F32),
                        pltpu.VMEM((2, EXPERT_FF, D_MODEL), F32),
                        pltpu.VMEM((D_MODEL, 2 * EXPERT_FF), BF16),
                        pltpu.VMEM((EXPERT_FF, D_MODEL), BF16),
                        pltpu.VMEM((2 * PACK_CHUNKS * br, LANES), U32),
                        pltpu.VMEM((2 * br * PACK_CHUNKS, LANES), U32),
                        pltpu.SMEM((2 * SRC_GROUP * br,), I32),
                        pltpu.SMEM((2 * SRC_GROUP * br,), I32),
                        pltpu.SemaphoreType.DMA, pltpu.SemaphoreType.DMA,
                        pltpu.SemaphoreType.DMA((2,)), pltpu.SemaphoreType.DMA((2,))])
    n_out_tiles = n_tokens * TOP_K + 2 * br
    return pl.pallas_call(
        _moe_kernel, grid_spec=grid_spec,
        out_shape=jax.ShapeDtypeStruct((n_out_tiles * PACK_CHUNKS, LANES), U32),
        compiler_params=pltpu.CompilerParams(dimension_semantics=("arbitrary",),
                                             vmem_limit_bytes=MOE_VMEM_LIMIT),
        name="moe_experts",
    )(bexp, bval, nused, u2p, gat, dst, w1, w3, w2)


def _combine_kernel(w_hbm, yt_ref, base_ref, mod_ref, g_ref, b_ref, yc_ref, yl_ref,
                    w_smem, acc_lo, acc_hi, sem_w, *, n_ctx_tiles):
    i = pl.program_id(0)
    n_steps = pl.num_programs(0)
    n_tok = acc_lo.shape[0] // PACK_CHUNKS
    n_idx = n_tok * TOP_K

    def w_copy(tile):
        return pltpu.make_async_copy(w_hbm.at[pl.ds(tile * n_idx, n_idx)],
                                     w_smem.at[pl.ds(lax.rem(tile, 2) * n_idx, n_idx)], sem_w)

    @pl.when(i == 0)
    def _():
        w_copy(i).start()

    w_copy(i).wait()

    @pl.when(i + 1 < n_steps)
    def _():
        w_copy(i + 1).start()

    wbase = lax.rem(i, 2) * n_idx

    per_tile = SUBLANES // PACK_CHUNKS
    first = lax.broadcasted_iota(I32, (SUBLANES, LANES), 0) < PACK_CHUNKS

    def reduce_token(t):
        lo = hi = None
        for m in range(TOP_K // per_tile):
            j = t * TOP_K + m * per_tile
            words = yt_ref[pl.ds(pl.multiple_of(j * PACK_CHUNKS, SUBLANES), SUBLANES), :]
            wgt = jnp.where(first, w_smem[wbase + j], w_smem[wbase + j + 1])
            t_lo = wgt * pltpu.bitcast(words << 16, F32)
            t_hi = wgt * pltpu.bitcast(words & jnp.uint32(HIGH_HALF), F32)
            lo = t_lo if lo is None else lo + t_lo
            hi = t_hi if hi is None else hi + t_hi
        row = pl.multiple_of(t * PACK_CHUNKS, PACK_CHUNKS)
        acc_lo[pl.ds(row, PACK_CHUNKS), :] = lo[0:PACK_CHUNKS] + lo[PACK_CHUNKS:SUBLANES]
        acc_hi[pl.ds(row, PACK_CHUNKS), :] = hi[0:PACK_CHUNKS] + hi[PACK_CHUNKS:SUBLANES]

    def reduce_group(i, carry):
        for u in range(COMB_UNROLL):
            reduce_token(i * COMB_UNROLL + u)
        return carry

    lax.fori_loop(0, n_tok // COMB_UNROLL, reduce_group, 0)
    parts = []
    for s in range(PACK_CHUNKS):
        parts.append(acc_lo[pl.ds(s, n_tok, stride=PACK_CHUNKS), :])
        parts.append(acc_hi[pl.ds(s, n_tok, stride=PACK_CHUNKS), :])
    moe = jnp.concatenate(parts, axis=1)
    gate2 = mod_ref[:, 5 * D_MODEL:6 * D_MODEL]
    y = _layer_norm(base_ref[...] + gate2 * moe, g_ref[...], b_ref[...])

    @pl.when(i < n_ctx_tiles)
    def _():
        yc_ref[...] = y

    @pl.when(i >= n_ctx_tiles)
    def _():
        yl_ref[...] = y


def _combine(w_flat, yt, base, mod3, l2g, l2b, n_ctx, seq_tokens):
    n = base.shape[0]
    tc = COMB_TILE
    n_ctx_tiles = n_ctx // tc
    n_seq_tiles = seq_tokens // tc

    def mod_idx(i):
        return (jnp.where(i < n_ctx_tiles, 0, 1 + (i - n_ctx_tiles) // n_seq_tiles), 0, 0)

    full = lambda a: pl.BlockSpec(a.shape, lambda i: (0,) * a.ndim)
    return pl.pallas_call(
        functools.partial(_combine_kernel, n_ctx_tiles=n_ctx_tiles),
        grid=(n // tc,),
        in_specs=[pl.BlockSpec(memory_space=pl.ANY),
                  pl.BlockSpec((tc * TOP_K * PACK_CHUNKS, LANES), lambda i: (i, 0)),
                  pl.BlockSpec((tc, D_MODEL), lambda i: (i, 0)),
                  pl.BlockSpec((None, 1, mod3.shape[2]), mod_idx), full(l2g), full(l2b)],
        out_specs=(pl.BlockSpec((tc, D_MODEL), lambda i: (jnp.minimum(i, n_ctx_tiles - 1), 0)),
                   pl.BlockSpec((tc, D_MODEL), lambda i: (jnp.maximum(i - n_ctx_tiles, 0), 0))),
        out_shape=(jax.ShapeDtypeStruct((n_ctx, D_MODEL), F32),
                   jax.ShapeDtypeStruct((n - n_ctx, D_MODEL), F32)),
        scratch_shapes=[pltpu.SMEM((2 * tc * TOP_K,), F32),
                        pltpu.VMEM((tc * PACK_CHUNKS, LANES), F32),
                        pltpu.VMEM((tc * PACK_CHUNKS, LANES), F32),
                        pltpu.SemaphoreType.DMA],
        compiler_params=_cparams(1), name="combine",
    )(w_flat, yt, base, mod3, l2g, l2b)


def _rope_tables(n_tok, tile):
    f32 = np.float32
    rows = n_tok // GRID_W
    row_idx = np.repeat(np.arange(rows, dtype=f32), GRID_W)
    col_idx = np.tile(np.arange(GRID_W, dtype=f32), rows)
    inv_freq = (1.0 / (ROPE_THETA ** (np.arange(0, ROPE_AXIS_DIM, 2, dtype=f32) / ROPE_AXIS_DIM))).astype(f32)
    ang_r = row_idx[:, None] * inv_freq[None, :]
    ang_c = col_idx[:, None] * inv_freq[None, :]
    ang = np.concatenate([ang_r, ang_r, ang_c, ang_c], axis=-1)
    cos, sin = np.cos(ang), np.sin(ang)
    quarter = (np.arange(HEAD_DIM) // (ROPE_AXIS_DIM // 2)) % 2
    sin_a = np.where(quarter == 0, -sin, 0.0)
    sin_b = np.where(quarter == 1, sin, 0.0)
    rep = LANES // HEAD_DIM
    ident = lambda v: np.full((tile, LANES), v, f32)
    cos_t = np.concatenate([np.tile(cos, (1, rep)), ident(1.0)], axis=0)
    sa_t = np.concatenate([np.tile(sin_a, (1, rep)), ident(0.0)], axis=0)
    sb_t = np.concatenate([np.tile(sin_b, (1, rep)), ident(0.0)], axis=0)
    ident_tr = lambda v: np.full((HEAD_DIM, tile), v, f32)
    cos_tr = np.concatenate([cos.T, ident_tr(1.0)], axis=1)
    sin_tr = np.concatenate([sin.T, ident_tr(0.0)], axis=1)
    return tuple(jnp.asarray(t, F32) for t in (cos_t, sa_t, sb_t, cos_tr, sin_tr))


def _dup_heads(a):
    parts = []
    for h in range(KV_HEADS):
        blk = a[..., h * HEAD_DIM:(h + 1) * HEAD_DIM]
        parts += [blk] * (LANES // HEAD_DIM)
    return jnp.concatenate(parts, axis=-1)


def kernel(x_prompt, x_sample, cache_k, cache_v, state_gla_fwd, state_gla_bwd, c, c_ctx, w_ada, b_ada, w_in, q_norm, k_norm, gla_wa_fwd, gla_ba_fwd, gla_wa_bwd, gla_ba_bwd, gla_norm, w_out, ln1_g, ln1_b, ln2_g, ln2_b, w_router, router_bias, exp_w1, exp_w3, exp_w2, sh_w1, sh_w3, sh_w2):
    n_ctx_b, ctx_seq, _ = x_prompt.shape
    n_lat_b, lat_seq, _ = x_sample.shape
    n_ctx = n_ctx_b * ctx_seq
    n_lat = n_lat_b * lat_seq
    n = n_ctx + n_lat
    l = 0

    x_c = x_prompt.reshape(n_ctx, D_MODEL)
    x_l = x_sample.reshape(n_lat, D_MODEL)

    c_rows = jnp.zeros((SUBLANES, D_MODEL), F32).at[0].set(c_ctx).at[1:1 + n_lat_b].set(c)
    mod = _modulation(c_rows, w_ada[l], b_ada[l][None, :])
    mod3 = mod.reshape(SUBLANES, 1, 6 * D_MODEL)

    wi = w_in[l]
    o_q, o_k, o_v, o_gq, o_gk, o_gv, o_gg, o_rf, o_rb, o_end = np.cumsum(
        [0, ATT_WIDTH, KV_HEADS * HEAD_DIM, KV_HEADS * HEAD_DIM, GLA_KW, GLA_KW, GLA_WIDTH, GLA_WIDTH,
         GLA_GATE_RANK, GLA_GATE_RANK])
    w_tok = jnp.concatenate([
        _dup_heads(wi[:, o_k:o_v]), wi[:, o_v:o_gq], wi[:, o_gq:o_gk],
        wi[:, o_gv:o_gg], wi[:, o_gg:o_rf], wi[:, o_rf:o_end],
        jnp.zeros((D_MODEL, LANES - 2 * GLA_GATE_RANK), F32)], axis=1).astype(BF16)
    w_tr = jnp.concatenate([wi[:, o_q:o_k], wi[:, o_v:o_gq], wi[:, o_gk:o_gv], wi[:, o_rf:o_end]],
                           axis=1).T.astype(BF16)
    rep = LANES // HEAD_DIM
    qn = q_norm[l][:, None]
    kn = jnp.tile(k_norm[l], rep)[None, :]
    seg = jnp.asarray(np.kron(np.eye(rep), np.ones((HEAD_DIM, HEAD_DIM))), BF16)
    wa = jnp.zeros((LANES, 2 * GLA_KW), F32)
    wa = wa.at[0:GLA_GATE_RANK, 0:GLA_KW].set(gla_wa_fwd[l])
    wa = wa.at[GLA_GATE_RANK:2 * GLA_GATE_RANK, GLA_KW:].set(gla_wa_bwd[l])
    ba = jnp.concatenate([gla_ba_fwd[l], gla_ba_bwd[l]])[None, :]
    wat = wa[0:2 * GLA_GATE_RANK, :].T
    bat = ba.T
    cos_t, sa_t, sb_t, cos_tr, sin_tr = _rope_tables(lat_seq, TOK_TILE)

    (qt, k_dup, vt, k32, v32, gq, gv, gg, la, gkt, lat) = _in_projection(
        x_c, x_l, mod3, w_tok, w_tr, qn, kn, cos_t, sa_t, sb_t, cos_tr, sin_tr, seg, wa, ba, wat, bat,
        lat_seq // TOK_TILE)

    ck = _dup_heads(cache_k[:, l].reshape(n_lat_b, -1, KV_HEADS * HEAD_DIM)).astype(BF16)
    cvt = cache_v[:, l].reshape(n_lat_b, -1, KV_HEADS * HEAD_DIM).transpose(0, 2, 1).astype(BF16)
    att_c = _attention(qt, k_dup, vt, None, 0, n_ctx_b, ctx_seq)
    att_l = _attention(qt, k_dup, vt, (ck, cvt), n_ctx, n_lat_b, lat_seq)

    gconst, levels_of = _gla_constants()
    to_dev = lambda t: (jnp.asarray(t[0], BF16), jnp.asarray(t[1], BF16), jnp.asarray(t[2], F32))
    bd = jnp.asarray(np.kron(np.eye(GLA_HEADS), np.ones((GLA_DK, GLA_DV))), BF16)
    vbd = jnp.asarray(np.kron(np.eye(GLA_HEADS), np.ones((GLA_CHUNK, GLA_DV))), BF16)
    consts = ((to_dev(gconst["f"]), to_dev(gconst["b"])), levels_of, bd, vbd)
    s_zero = jnp.zeros((n_ctx_b, GLA_HEADS, GLA_DK, GLA_DV), F32)
    of_c, ob_c, sf_new, sb_new = _gla(gq, la, gkt, lat, gv, s_zero, s_zero, consts, 0, n_ctx_b, ctx_seq)
    of_l, ob_l, _, _ = _gla(gq, la, gkt, lat, gv, state_gla_fwd[:, l], state_gla_bwd[:, l], consts,
                            n_ctx, n_lat_b, lat_seq)

    base, u2_rows, logits_t = _out_projection(
        att_c, att_l, of_c, of_l, ob_c, ob_l, gg, x_c, x_l, mod3, w_out[l].astype(BF16),
        gla_norm[l][None, :], ln1_g[l][None, :], ln1_b[l][None, :], w_router[l].T.astype(BF16),
        lat_seq // TOK_TILE)

    upper = jnp.asarray(np.triu(np.ones((TOK_TILE, TOK_TILE)), 1), BF16)
    idx_t, w_t, pos_t, counts = _route(logits_t, router_bias[l][:, None], upper)
    n_blocks = n * TOP_K // MOE_ROWS + N_EXPERTS
    n_blocks_pad = -(-(n_blocks + 1) // LANES) * LANES
    lower = jnp.asarray(np.tril(np.ones((N_EXPERTS, N_EXPERTS)), -1), F32)
    dest_t, bexp, bval, nused = _destinations(counts, lower, idx_t, pos_t, n_blocks_pad)
    w_flat = w_t.T.reshape(-1)

    ids = np.arange(n, dtype=np.int32)[None, :] * TOP_K + np.arange(TOP_K, dtype=np.int32)[:, None]
    row_src = _invert_rows(dest_t, ids, n_blocks * MOE_ROWS)
    sw13 = jnp.concatenate([sh_w1[l], sh_w3[l]], axis=1).astype(BF16)
    base = _shared_expert(u2_rows, base, mod3, sw13, sh_w2[l].astype(BF16), n_ctx, lat_seq // TOK_TILE)
    gat, dst = _row_tables(row_src, bval, n_blocks, n)
    yt = _moe_experts(bexp.reshape(-1), bval.reshape(-1), nused.reshape(-1)[0:1], u2_rows, gat, dst,
                      exp_w1[l], exp_w3[l], exp_w2[l], n)
    y_c, y_l = _combine(w_flat, yt, base, mod3, ln2_g[l][None, :], ln2_b[l][None, :], n_ctx, lat_seq)

    y_prompt = y_c.reshape(n_ctx_b, ctx_seq, D_MODEL)
    y_sample = y_l.reshape(n_lat_b, lat_seq, D_MODEL)
    new_cache_k = k32.reshape(n_ctx_b, 1, ctx_seq, KV_HEADS, HEAD_DIM)
    new_cache_v = v32.reshape(n_ctx_b, 1, ctx_seq, KV_HEADS, HEAD_DIM)
    return (y_prompt, y_sample, new_cache_k, new_cache_v, sf_new[:, None], sb_new[:, None])
```

```python
import functools

import numpy as np
import jax
import jax.numpy as jnp
from jax import lax
from jax.experimental import pallas as pl
from jax.experimental.pallas import tpu as pltpu
from jax.experimental.pallas import tpu_sc as plsc

F32 = jnp.float32
BF16 = jnp.bfloat16
I32 = jnp.int32

D_MODEL = 1024
GRID_W = 64
HEAD_DIM = 64
N_HEADS = 8
KV_HEADS = 2
ATT_WIDTH = N_HEADS * HEAD_DIM
ATT_SCALE = HEAD_DIM ** -0.5
LOG2_E = 1.4426950408889634
ROPE_AXIS_DIM = HEAD_DIM // 2
ROPE_THETA = 10000.0
GLA_HEADS = 4
GLA_DK = 64
GLA_DV = 128
GLA_WIDTH = GLA_HEADS * GLA_DV
GLA_KW = GLA_HEADS * GLA_DK
GLA_GATE_RANK = 16
GLA_TAU = 16.0
N_EXPERTS = 256
TOP_K = 8
EXPERT_FF = 256
SHARED_FF = 256
ROUTED_SCALE = 2.5
DEPTH = 1
ALPHA = (2.0 * DEPTH) ** 0.25
EPS = 1e-6

LANES = 128
SUBLANES = 8
VMEM_BYTES = 64 * 1024 * 1024
VMEM_LIMIT = VMEM_BYTES - 8 * 1024 * 1024
PACK_CHUNKS = D_MODEL // (2 * LANES)
PACK_CHUNKS_LOG2 = PACK_CHUNKS.bit_length() - 1
HIGH_HALF = 0xFFFF0000
U32 = jnp.uint32

TOK_TILE = 512
ATT_TQ = 256
GLA_CHUNK = 128
GLA_LEVELS = ((32, 128), (8, 32), (2, 8), (1, 2))
MOE_ROWS = 256
MOE_VMEM_LIMIT = VMEM_BYTES - 2 * 1024 * 1024
TOP_K_LOG2 = TOP_K.bit_length() - 1
SMEM_SLICE_WORDS = 1024
SRC_GROUP = SMEM_SLICE_WORDS // MOE_ROWS
COMB_TILE = 512
COMB_UNROLL = 8
HIGHEST = lax.Precision.HIGHEST


def _cparams(n_axes):
    return pltpu.CompilerParams(dimension_semantics=("arbitrary",) * n_axes,
                                vmem_limit_bytes=VMEM_LIMIT)


def _silu(x):
    return x * jax.nn.sigmoid(x)


def _log_sigmoid(x):
    return jnp.minimum(x, 0.0) - jnp.log(1.0 + jnp.exp(-jnp.abs(x)))


def _dot_split(a, b):
    a_hi = a.astype(BF16)
    b_hi = b.astype(BF16)
    a_lo = (a - a_hi.astype(F32)).astype(BF16)
    b_lo = (b - b_hi.astype(F32)).astype(BF16)
    dot = functools.partial(jnp.dot, preferred_element_type=F32)
    return dot(a_hi, b_hi) + dot(a_lo, b_hi) + dot(a_hi, b_lo)


def _layer_norm(z, g, b):
    mu = jnp.mean(z, axis=-1, keepdims=True)
    zc = z - mu
    var = jnp.mean(zc * zc, axis=-1, keepdims=True)
    return zc * lax.rsqrt(var + EPS) * g + b


def _mod_kernel(c_ref, w_ref, b_ref, o_ref):
    s = _silu(c_ref[...]).astype(BF16)
    o_ref[...] = jnp.dot(s, w_ref[...].astype(BF16), preferred_element_type=F32) + b_ref[...]


def _modulation(c_rows, w_ada, b_ada):
    n_cols = w_ada.shape[1]
    tn = 512
    return pl.pallas_call(
        _mod_kernel,
        grid=(n_cols // tn,),
        in_specs=[pl.BlockSpec((SUBLANES, D_MODEL), lambda j: (0, 0)),
                  pl.BlockSpec((D_MODEL, tn), lambda j: (0, j)),
                  pl.BlockSpec((1, tn), lambda j: (0, j))],
        out_specs=pl.BlockSpec((SUBLANES, tn), lambda j: (0, j)),
        out_shape=jax.ShapeDtypeStruct((SUBLANES, n_cols), F32),
        compiler_params=_cparams(1),
        name="modulation",
    )(c_rows, w_ada, b_ada)


_C_K = 0
_C_V = _C_K + 2 * LANES
_C_GQ = _C_V + KV_HEADS * HEAD_DIM
_C_GV = _C_GQ + GLA_KW
_C_GG = _C_GV + GLA_WIDTH
_C_RA = _C_GG + GLA_WIDTH
_C_END = _C_RA + LANES
_R_Q = 0
_R_V = _R_Q + ATT_WIDTH
_R_GK = _R_V + KV_HEADS * HEAD_DIM
_R_RA = _R_GK + GLA_KW
_R_END = _R_RA + 2 * GLA_GATE_RANK


def _inproj_kernel(xc_ref, xl_ref, mod_ref, w_ref, wt_ref, qn_ref, kn_ref, cos_ref, sa_ref, sb_ref,
                   cost_ref, sint_ref, seg_ref, wa_ref, ba_ref, wat_ref, bat_ref,
                   qt_ref, k_ref, vt_ref, k32_ref, v32_ref, gq_ref, gv_ref, gg_ref,
                   la_ref, gkt_ref, lat_ref, *, n_ctx_tiles):
    i = pl.program_id(0)
    m = mod_ref[...]
    shift1 = m[:, 0:D_MODEL]
    scale1 = m[:, D_MODEL:2 * D_MODEL]
    x = jnp.where(i < n_ctx_tiles, xc_ref[...], xl_ref[...])
    u = (x * (1.0 + scale1) + shift1).astype(BF16)

    cos = cos_ref[...]
    sin_a = sa_ref[...]
    sin_b = sb_ref[...]
    seg = seg_ref[...]
    lane = lax.broadcasted_iota(I32, (u.shape[0], LANES), 1)
    low = lane < HEAD_DIM

    def proj(c0, c1):
        return jnp.dot(u, w_ref[:, c0:c1], preferred_element_type=F32)

    def head_norm(blk, gain):
        ss = jnp.dot((blk * blk).astype(BF16), seg, preferred_element_type=F32) * (1.0 / HEAD_DIM)
        return blk * lax.rsqrt(ss + EPS) * gain

    def rope(blk):
        return (blk * cos + pltpu.roll(blk, LANES - ROPE_AXIS_DIM // 2, 1) * sin_a
                + pltpu.roll(blk, ROPE_AXIS_DIM // 2, 1) * sin_b)

    pk = proj(_C_K, _C_V)
    kn = [head_norm(pk[:, j * LANES:(j + 1) * LANES], kn_ref[...]) for j in range(KV_HEADS)]
    for j in range(KV_HEADS):
        k_ref[:, j * LANES:(j + 1) * LANES] = rope(kn[j]).astype(BF16)

    @pl.when(i < n_ctx_tiles)
    def _():
        k32_ref[...] = jnp.where(low, kn[0], kn[1])
        v32_ref[...] = proj(_C_V, _C_GQ)

    gq_ref[...] = proj(_C_GQ, _C_GV) * (GLA_DK ** -0.5)
    gv_ref[...] = proj(_C_GV, _C_GG).astype(BF16)
    gg_ref[...] = proj(_C_GG, _C_RA).astype(BF16)

    ra = proj(_C_RA, _C_END)
    pre = _dot_split(ra, wa_ref[...]) + ba_ref[...]
    la_ref[...] = _log_sigmoid(pre) * (1.0 / GLA_TAU)

    pt = lax.dot_general(wt_ref[...], u, (((1,), (1,)), ((), ())), preferred_element_type=F32)
    cos_t = cost_ref[...]
    sin_t = sint_ref[...]
    quarter = ROPE_AXIS_DIM // 2
    for h in range(N_HEADS):
        blk = pt[_R_Q + h * HEAD_DIM:_R_Q + (h + 1) * HEAD_DIM, :]
        ms = jnp.mean(blk * blk, axis=0, keepdims=True)
        qn = blk * lax.rsqrt(ms + EPS) * qn_ref[...]
        rot = jnp.concatenate([-qn[quarter:2 * quarter], qn[0:quarter],
                               -qn[3 * quarter:4 * quarter], qn[2 * quarter:3 * quarter]], axis=0)
        qt_ref[h * HEAD_DIM:(h + 1) * HEAD_DIM, :] = (
            (qn * cos_t + rot * sin_t) * (ATT_SCALE * LOG2_E)).astype(BF16)
    vt_ref[...] = pt[_R_V:_R_GK, :].astype(BF16)
    gkt_ref[...] = pt[_R_GK:_R_RA, :]
    rat = pt[_R_RA:_R_END, :]
    pre_t = _dot_split(wat_ref[...], rat) + bat_ref[...]
    lat_ref[...] = _log_sigmoid(pre_t) * (1.0 / GLA_TAU)


def _in_projection(x_c, x_l, mod3, w_tok, w_tr, qn, kn, cos_t, sa_t, sb_t, cos_tr, sin_tr, seg, wa, ba,
                   wat, bat, n_seq_tiles):
    n_ctx = x_c.shape[0]
    n = n_ctx + x_l.shape[0]
    tb = TOK_TILE
    n_ctx_tiles = n_ctx // tb
    n_tiles = n // tb
    n_rope_blocks = cos_t.shape[0] // tb - 1

    def mod_idx(i):
        return (jnp.where(i < n_ctx_tiles, 0, 1 + (i - n_ctx_tiles) // n_seq_tiles), 0, 0)

    def rope_blk(i):
        return jnp.where(i < n_ctx_tiles, n_rope_blocks, (i - n_ctx_tiles) % n_seq_tiles)

    def rope_idx(i):
        return (rope_blk(i), 0)

    def ctx_idx(i):
        return (jnp.minimum(i, n_ctx_tiles - 1), 0)

    tok = lambda w: pl.BlockSpec((tb, w), lambda i: (i, 0))
    full = lambda a: pl.BlockSpec(a.shape, lambda i: (0,) * a.ndim)
    tr = lambda r: pl.BlockSpec((r, tb), lambda i: (0, i))
    rope_tr = pl.BlockSpec((HEAD_DIM, tb), lambda i: (0, rope_blk(i)))
    out_shapes = (
        jax.ShapeDtypeStruct((ATT_WIDTH, n), BF16),
        jax.ShapeDtypeStruct((n, 2 * LANES), BF16),
        jax.ShapeDtypeStruct((KV_HEADS * HEAD_DIM, n), BF16),
        jax.ShapeDtypeStruct((n_ctx, LANES), F32),
        jax.ShapeDtypeStruct((n_ctx, LANES), F32),
        jax.ShapeDtypeStruct((n, GLA_KW), F32),
        jax.ShapeDtypeStruct((n, GLA_WIDTH), BF16),
        jax.ShapeDtypeStruct((n, GLA_WIDTH), BF16),
        jax.ShapeDtypeStruct((n, 2 * GLA_KW), F32),
        jax.ShapeDtypeStruct((GLA_KW, n), F32),
        jax.ShapeDtypeStruct((2 * GLA_KW, n), F32),
    )
    out_specs = (tr(ATT_WIDTH), tok(2 * LANES), tr(KV_HEADS * HEAD_DIM),
                 pl.BlockSpec((tb, LANES), ctx_idx), pl.BlockSpec((tb, LANES), ctx_idx),
                 tok(GLA_KW), tok(GLA_WIDTH), tok(GLA_WIDTH), tok(2 * GLA_KW),
                 tr(GLA_KW), tr(2 * GLA_KW))
    in_specs = [pl.BlockSpec((tb, D_MODEL), ctx_idx),
                pl.BlockSpec((tb, D_MODEL), lambda i: (jnp.maximum(i - n_ctx_tiles, 0), 0)),
                pl.BlockSpec((None, 1, mod3.shape[2]), mod_idx),
                full(w_tok), full(w_tr), full(qn), full(kn),
                pl.BlockSpec((tb, LANES), rope_idx), pl.BlockSpec((tb, LANES), rope_idx),
                pl.BlockSpec((tb, LANES), rope_idx), rope_tr, rope_tr,
                full(seg), full(wa), full(ba), full(wat), full(bat)]
    return pl.pallas_call(
        functools.partial(_inproj_kernel, n_ctx_tiles=n_ctx_tiles),
        grid=(n_tiles,), in_specs=in_specs, out_specs=out_specs, out_shape=out_shapes,
        compiler_params=_cparams(1), name="in_projection",
    )(x_c, x_l, mod3, w_tok, w_tr, qn, kn, cos_t, sa_t, sb_t, cos_tr, sin_tr, seg, wa, ba, wat, bat)


def _attention_kernel(*refs, n_kv_parts):
    qt_ref = refs[0]
    k_refs = refs[1:1 + n_kv_parts]
    vt_refs = refs[1 + n_kv_parts:1 + 2 * n_kv_parts]
    o_ref = refs[1 + 2 * n_kv_parts]
    tq = qt_ref.shape[1]
    group = N_HEADS // KV_HEADS
    for kv in range(KV_HEADS):
        heads = range(kv * group, (kv + 1) * group)
        q_grp = jnp.concatenate([qt_ref[h * HEAD_DIM:(h + 1) * HEAD_DIM, :] for h in heads], axis=1)
        rhs = jnp.concatenate([q_grp, jnp.zeros_like(q_grp)], axis=0)
        s = [jnp.dot(k[:, kv * LANES:(kv + 1) * LANES], rhs, preferred_element_type=F32)
             for k in k_refs]
        mx = functools.reduce(jnp.maximum, [jnp.max(x, axis=0, keepdims=True) for x in s])
        pr = [jnp.exp2(x - mx) for x in s]
        den = functools.reduce(jnp.add, [jnp.sum(x, axis=0, keepdims=True) for x in pr])
        acc = functools.reduce(jnp.add, [
            jnp.dot(vt[kv * HEAD_DIM:(kv + 1) * HEAD_DIM, :], x.astype(BF16),
                    preferred_element_type=F32) for x, vt in zip(pr, vt_refs)])
        out = (acc / den).astype(BF16)
        for j, h in enumerate(heads):
            o_ref[h * HEAD_DIM:(h + 1) * HEAD_DIM, :] = out[:, j * tq:(j + 1) * tq]


def _attention(qt, k, vt, extra_kv, row0, n_batch, seq):
    tq = ATT_TQ
    n_q = seq // tq
    q_blk0 = row0 // tq
    kv_blk0 = row0 // seq
    in_specs = [pl.BlockSpec((ATT_WIDTH, tq), lambda b, i: (0, q_blk0 + b * n_q + i))]
    k_spec = pl.BlockSpec((seq, 2 * LANES), lambda b, i: (kv_blk0 + b, 0))
    vt_spec = pl.BlockSpec((KV_HEADS * HEAD_DIM, seq), lambda b, i: (0, kv_blk0 + b))
    args_k, args_v, specs_k, specs_v = [k], [vt], [k_spec], [vt_spec]
    if extra_kv is not None:
        ck, cvt = extra_kv
        args_k.append(ck)
        args_v.append(cvt)
        specs_k.append(pl.BlockSpec((None, ck.shape[1], 2 * LANES), lambda b, i: (b, 0, 0)))
        specs_v.append(pl.BlockSpec((None, KV_HEADS * HEAD_DIM, cvt.shape[2]), lambda b, i: (b, 0, 0)))
    return pl.pallas_call(
        functools.partial(_attention_kernel, n_kv_parts=len(args_k)),
        grid=(n_batch, n_q),
        in_specs=in_specs + specs_k + specs_v,
        out_specs=pl.BlockSpec((ATT_WIDTH, tq), lambda b, i: (0, b * n_q + i)),
        out_shape=jax.ShapeDtypeStruct((ATT_WIDTH, n_batch * seq), BF16),
        compiler_params=_cparams(2), name="attention",
    )(qt, *args_k, *args_v)


def _gla_constants():
    c = GLA_CHUNK
    idx = np.arange(c)
    q_mats, k_mats, masks, levels_of = [], [], [], []
    for li, (s, p) in enumerate(GLA_LEVELS):
        start = (idx // s) * s
        end = start + s - 1
        k_mats.append(((idx[None, :] > idx[:, None]) & (idx[None, :] <= end[:, None])))
        for d in range(p // s - 1):
            lo = np.maximum(start - d * s, 0)
            q_mats.append((idx[None, :] >= lo[:, None]) & (idx[None, :] <= idx[:, None]))
            masks.append((idx[:, None] // p == idx[None, :] // p)
                         & (idx[:, None] // s - idx[None, :] // s - 1 == d))
            levels_of.append(li)
    masks.append(np.eye(c, dtype=bool))
    levels_of.append(len(GLA_LEVELS) - 1)
    q_mats.append(idx[None, :] <= idx[:, None])
    k_mats = k_mats[:-1]
    k_mats.append(idx[None, :] > idx[:, None])
    k_mats.append(np.ones((c, c), bool))
    out = {}
    for name, flip in (("f", False), ("b", True)):
        f = (lambda a: a[::-1, ::-1]) if flip else (lambda a: a)
        lq = np.concatenate([f(a) for a in q_mats], axis=0).astype(np.float32)
        lkt = np.concatenate([f(a).T for a in k_mats], axis=1).astype(np.float32)
        mk = np.stack([np.tile(f(a), (1, GLA_HEADS)) for a in masks]).astype(np.float32)
        out[name] = (np.concatenate([lq, lq], axis=1), np.concatenate([lkt, lkt], axis=0), mk)
    return out, tuple(levels_of)


def _gla_direction(q, g, gkt, gt, v, lq2, lkt2, masks_ref, bd, vbd, s_ref, levels_of):
    c = GLA_CHUNK
    n_var = len(levels_of)
    n_lev = len(GLA_LEVELS)
    g_hi = g.astype(BF16)
    g_lo = (g - g_hi.astype(F32)).astype(BF16)
    fq = jnp.dot(lq2, jnp.concatenate([g_hi, g_lo], axis=0), preferred_element_type=F32)
    gt_hi = gt.astype(BF16)
    gt_lo = (gt - gt_hi.astype(F32)).astype(BF16)
    fk = jnp.dot(jnp.concatenate([gt_hi, gt_lo], axis=1), lkt2, preferred_element_type=F32)

    def key_factor(f):
        return gkt * jnp.exp(fk[:, f * c:(f + 1) * c])

    q_var = [(q * jnp.exp(fq[vi * c:(vi + 1) * c, :])).astype(BF16) for vi in range(n_var - 1)]
    q_var.append(q.astype(BF16))
    a = jnp.zeros((c, GLA_HEADS * c), F32)
    for li in range(n_lev):
        kt = (key_factor(li) if li < n_lev - 1 else gkt).astype(BF16)
        xt = jnp.concatenate([kt] * GLA_HEADS, axis=1) * bd
        vis = [vi for vi in range(n_var) if levels_of[vi] == li]
        res = jnp.dot(jnp.concatenate([q_var[vi] for vi in vis], axis=0), xt,
                      preferred_element_type=F32)
        for r, vi in enumerate(vis):
            a = a + masks_ref[vi] * res[r * c:(r + 1) * c, :]
    q_in = (q * jnp.exp(fq[(n_var - 1) * c:n_var * c, :])).astype(BF16)
    state = s_ref[...]
    v_bd = jnp.concatenate([v] * GLA_HEADS, axis=0) * vbd
    o = (jnp.dot(q_in, state.astype(BF16), preferred_element_type=F32)
         + jnp.dot(a.astype(BF16), v_bd, preferred_element_type=F32))
    k_out = key_factor(n_lev - 1).astype(BF16)
    e_tot = jnp.exp(fk[:, n_lev * c:(n_lev + 1) * c])
    upd = jnp.dot(k_out, v, preferred_element_type=F32)
    s_ref[...] = (state * jnp.concatenate([e_tot] * (GLA_WIDTH // c), axis=1)
                  + upd * bd.astype(F32))
    return o


def _gla_kernel(gq_f, la_f, gkt_f, lat_f, gv_f, gq_b, la_b, gkt_b, lat_b, gv_b,
                s0f_ref, s0b_ref, lq2f, lkt2f, mkf, lq2b, lkt2b, mkb, bd_ref, vbd_ref,
                of_ref, ob_ref, sf_ref, sb_ref, st_f, st_b, *, levels_of):
    n = pl.program_id(1)

    @pl.when(n == 0)
    def _():
        st_f[...] = jnp.zeros_like(st_f)
        st_b[...] = jnp.zeros_like(st_b)
        for h in range(GLA_HEADS):
            rows = slice(h * GLA_DK, (h + 1) * GLA_DK)
            cols = slice(h * GLA_DV, (h + 1) * GLA_DV)
            st_f[rows, cols] = s0f_ref[h]
            st_b[rows, cols] = s0b_ref[h]

    bd = bd_ref[...]
    vbd = vbd_ref[...]
    of_ref[...] = _gla_direction(gq_f[...], la_f[...], gkt_f[...], lat_f[...], gv_f[...],
                                 lq2f[...], lkt2f[...], mkf, bd, vbd, st_f, levels_of)
    ob_ref[...] = _gla_direction(gq_b[...], la_b[...], gkt_b[...], lat_b[...], gv_b[...],
                                 lq2b[...], lkt2b[...], mkb, bd, vbd, st_b, levels_of)

    @pl.when(n == pl.num_programs(1) - 1)
    def _():
        for h in range(GLA_HEADS):
            rows = slice(h * GLA_DK, (h + 1) * GLA_DK)
            cols = slice(h * GLA_DV, (h + 1) * GLA_DV)
            sf_ref[h] = st_f[rows, cols]
            sb_ref[h] = st_b[rows, cols]


def _gla(gq, la, gkt, lat, gv, s0f, s0b, consts, row0, n_batch, seq):
    (cf, cb), levels_of, bd, vbd = consts
    c = GLA_CHUNK
    nc = seq // c
    blk0 = row0 // c
    n_la_blocks_b = 1
    fwd = lambda b, n: blk0 + b * nc + n
    bwd = lambda b, n: blk0 + b * nc + (nc - 1 - n)

    def tok(w, which, col=0):
        return pl.BlockSpec((c, w), lambda b, n: (which(b, n), col))

    def tr(r, which, row=0):
        return pl.BlockSpec((r, c), lambda b, n: (row, which(b, n)))

    full = lambda a: pl.BlockSpec(a.shape, lambda b, n: (0,) * a.ndim)
    st_spec = pl.BlockSpec((None, GLA_HEADS, GLA_DK, GLA_DV), lambda b, n: (b, 0, 0, 0))
    in_specs = [tok(GLA_KW, fwd), tok(GLA_KW, fwd, 0), tr(GLA_KW, fwd), tr(GLA_KW, fwd, 0),
                tok(GLA_WIDTH, fwd),
                tok(GLA_KW, bwd), tok(GLA_KW, bwd, n_la_blocks_b), tr(GLA_KW, bwd),
                tr(GLA_KW, bwd, 1), tok(GLA_WIDTH, bwd),
                st_spec, st_spec,
                full(cf[0]), full(cf[1]), full(cf[2]), full(cb[0]), full(cb[1]), full(cb[2]),
                full(bd), full(vbd)]
    out_specs = (pl.BlockSpec((c, GLA_WIDTH), lambda b, n: (b * nc + n, 0)),
                 pl.BlockSpec((c, GLA_WIDTH), lambda b, n: (b * nc + (nc - 1 - n), 0)),
                 st_spec, st_spec)
    out_shape = (jax.ShapeDtypeStruct((n_batch * seq, GLA_WIDTH), F32),
                 jax.ShapeDtypeStruct((n_batch * seq, GLA_WIDTH), F32),
                 jax.ShapeDtypeStruct((n_batch, GLA_HEADS, GLA_DK, GLA_DV), F32),
                 jax.ShapeDtypeStruct((n_batch, GLA_HEADS, GLA_DK, GLA_DV), F32))
    return pl.pallas_call(
        functools.partial(_gla_kernel, levels_of=levels_of),
        grid=(n_batch, nc), in_specs=in_specs, out_specs=out_specs, out_shape=out_shape,
        scratch_shapes=[pltpu.VMEM((GLA_KW, GLA_WIDTH), F32), pltpu.VMEM((GLA_KW, GLA_WIDTH), F32)],
        compiler_params=_cparams(2), name="gla",
    )(gq, la, gkt, lat, gv, gq, la, gkt, lat, gv, s0f, s0b,
      cf[0], cf[1], cf[2], cb[0], cb[1], cb[2], bd, vbd)


def _outproj_kernel(attc_ref, attl_ref, ofc_ref, ofl_ref, obc_ref, obl_ref, gg_ref, xc_ref, xl_ref,
                    mod_ref, wo_ref, gn_ref, l1g_ref, l1b_ref, wrt_ref, sw13_ref, sw2_ref,
                    base_ref, u2_ref, lg_ref, *, n_ctx_tiles):
    is_ctx = pl.program_id(0) < n_ctx_tiles
    pick = lambda a_ref, b_ref: jnp.where(is_ctx, a_ref[...], b_ref[...])
    m = mod_ref[...]
    gate1 = m[:, 2 * D_MODEL:3 * D_MODEL]
    shift2 = m[:, 3 * D_MODEL:4 * D_MODEL]
    scale2 = m[:, 4 * D_MODEL:5 * D_MODEL]
    gate2 = m[:, 5 * D_MODEL:6 * D_MODEL]
    og = pick(ofc_ref, ofl_ref) + pick(obc_ref, obl_ref)
    gg = gg_ref[...].astype(F32)
    parts = []
    for h in range(GLA_HEADS):
        blk = og[:, h * GLA_DV:(h + 1) * GLA_DV]
        ms = jnp.mean(blk * blk, axis=-1, keepdims=True)
        nb = blk * lax.rsqrt(ms + EPS) * gn_ref[...]
        parts.append((nb * _silu(gg[:, h * GLA_DV:(h + 1) * GLA_DV])).astype(BF16))
    att_t = pick(attc_ref, attl_ref)
    hmix = (lax.dot_general(att_t, wo_ref[0:ATT_WIDTH, :], (((0,), (0,)), ((), ())),
                            preferred_element_type=F32)
            + jnp.dot(jnp.concatenate(parts, axis=1), wo_ref[ATT_WIDTH:, :],
                      preferred_element_type=F32))
    x1 = _layer_norm(ALPHA * pick(xc_ref, xl_ref) + gate1 * hmix, l1g_ref[...], l1b_ref[...])
    u2 = x1 * (1.0 + scale2) + shift2
    u2b = u2.astype(BF16)
    lg_ref[...] = lax.dot_general(wrt_ref[...], u2b, (((1,), (1,)), ((), ())),
                                  preferred_element_type=F32)
    ab = jnp.dot(u2b, sw13_ref[...], preferred_element_type=F32)
    hid = (_silu(ab[:, 0:SHARED_FF]) * ab[:, SHARED_FF:2 * SHARED_FF]).astype(BF16)
    shared = jnp.dot(hid, sw2_ref[...], preferred_element_type=F32)
    base_ref[...] = ALPHA * x1 + gate2 * shared
    _pack_rows(u2_ref, u2)


def _out_projection(att_c, att_l, of_c, of_l, ob_c, ob_l, gg, x_c, x_l, mod3, wo, gn, l1g, l1b, wrt,
                    sw13, sw2, n_seq_tiles):
    n_ctx = x_c.shape[0]
    n = n_ctx + x_l.shape[0]
    tb = TOK_TILE
    n_ctx_tiles = n_ctx // tb

    def mod_idx(i):
        return (jnp.where(i < n_ctx_tiles, 0, 1 + (i - n_ctx_tiles) // n_seq_tiles), 0, 0)

    ctx_blk = lambda i: jnp.minimum(i, n_ctx_tiles - 1)
    lat_blk = lambda i: jnp.maximum(i - n_ctx_tiles, 0)
    tok = lambda w: pl.BlockSpec((tb, w), lambda i: (i, 0))
    tok_c = lambda w: pl.BlockSpec((tb, w), lambda i: (ctx_blk(i), 0))
    tok_l = lambda w: pl.BlockSpec((tb, w), lambda i: (lat_blk(i), 0))
    full = lambda a: pl.BlockSpec(a.shape, lambda i: (0,) * a.ndim)
    return pl.pallas_call(
        functools.partial(_outproj_kernel, n_ctx_tiles=n_ctx_tiles),
        grid=(n // tb,),
        in_specs=[pl.BlockSpec((ATT_WIDTH, tb), lambda i: (0, ctx_blk(i))),
                  pl.BlockSpec((ATT_WIDTH, tb), lambda i: (0, lat_blk(i))),
                  tok_c(GLA_WIDTH), tok_l(GLA_WIDTH), tok_c(GLA_WIDTH), tok_l(GLA_WIDTH),
                  tok(GLA_WIDTH), tok_c(D_MODEL), tok_l(D_MODEL),
                  pl.BlockSpec((None, 1, mod3.shape[2]), mod_idx),
                  full(wo), full(gn), full(l1g), full(l1b), full(wrt), full(sw13), full(sw2)],
        out_specs=(tok(D_MODEL),
                   pl.BlockSpec((tb * PACK_CHUNKS, LANES), lambda i: (i, 0)),
                   pl.BlockSpec((N_EXPERTS, tb), lambda i: (0, i))),
        out_shape=(jax.ShapeDtypeStruct((n, D_MODEL), F32),
                   jax.ShapeDtypeStruct((n * PACK_CHUNKS, LANES), U32),
                   jax.ShapeDtypeStruct((N_EXPERTS, n), F32)),
        compiler_params=_cparams(1), name="out_projection",
    )(att_c, att_l, of_c, of_l, ob_c, ob_l, gg, x_c, x_l, mod3, wo, gn, l1g, l1b, wrt, sw13, sw2)


def _route_kernel(lg_ref, bias_ref, upper_ref, idx_ref, w_ref, pos_ref, cnt_ref, run_ref):
    i = pl.program_id(0)

    @pl.when(i == 0)
    def _():
        run_ref[...] = jnp.zeros_like(run_ref)

    s = jax.nn.sigmoid(lg_ref[...])
    work = s + bias_ref[...]
    rows = lax.broadcasted_iota(I32, s.shape, 0).astype(F32)
    sel = jnp.zeros(s.shape, F32)
    idxs, vals = [], []
    for _ in range(TOP_K):
        mx = jnp.max(work, axis=0, keepdims=True)
        idx = jnp.min(jnp.where(work == mx, rows, float(N_EXPERTS)), axis=0, keepdims=True)
        hit = rows == idx
        vals.append(jnp.sum(jnp.where(hit, s, 0.0), axis=0, keepdims=True))
        idxs.append(idx)
        sel = jnp.where(hit, 1.0, sel)
        work = jnp.where(hit, -jnp.inf, work)
    den = functools.reduce(jnp.add, vals)
    rank = jnp.dot(sel.astype(BF16), upper_ref[...], preferred_element_type=F32) + run_ref[:, 0:1]
    for k in range(TOP_K):
        idx_ref[k:k + 1, :] = idxs[k].astype(I32)
        w_ref[k:k + 1, :] = vals[k] / den * ROUTED_SCALE
        pos_ref[k:k + 1, :] = jnp.sum(jnp.where(rows == idxs[k], rank, 0.0), axis=0,
                                      keepdims=True).astype(I32)
    run_ref[...] = run_ref[...] + jnp.sum(sel, axis=1, keepdims=True)
    cnt_ref[...] = run_ref[...]


def _route(logits_t, bias_col, upper):
    n = logits_t.shape[1]
    tt = TOK_TILE
    row = lambda dt: jax.ShapeDtypeStruct((TOP_K, n), dt)
    blk = pl.BlockSpec((TOP_K, tt), lambda i: (0, i))
    return pl.pallas_call(
        _route_kernel,
        grid=(n // tt,),
        in_specs=[pl.BlockSpec((N_EXPERTS, tt), lambda i: (0, i)),
                  pl.BlockSpec((N_EXPERTS, 1), lambda i: (0, 0)),
                  pl.BlockSpec((tt, tt), lambda i: (0, 0))],
        out_specs=(blk, blk, blk, pl.BlockSpec((N_EXPERTS, LANES), lambda i: (0, 0))),
        out_shape=(row(I32), row(F32), row(I32), jax.ShapeDtypeStruct((N_EXPERTS, LANES), F32)),
        scratch_shapes=[pltpu.VMEM((N_EXPERTS, LANES), F32)],
        compiler_params=_cparams(1), name="route",
    )(logits_t, bias_col, upper)


def _dest_kernel(cnt_ref, lower_ref, idx_ref, pos_ref, dest_ref, bexp_ref, bval_ref, nused_ref):
    cnt = cnt_ref[...]
    nblk = jnp.floor((cnt + (MOE_ROWS - 1)) * (1.0 / MOE_ROWS))
    bstart = jnp.dot(lower_ref[...], nblk, precision=HIGHEST, preferred_element_type=F32)
    bend = bstart + nblk
    pstart = bstart[:, 0:1] * MOE_ROWS
    rows = lax.broadcasted_iota(I32, (N_EXPERTS, idx_ref.shape[1]), 0)
    for k in range(TOP_K):
        hit = rows == idx_ref[k:k + 1, :]
        dest_ref[k:k + 1, :] = (jnp.sum(jnp.where(hit, pstart, 0.0), axis=0, keepdims=True)
                                .astype(I32) + pos_ref[k:k + 1, :])

    @pl.when(pl.program_id(0) == 0)
    def _():
        nb = bexp_ref.shape[1]
        bid = lax.broadcasted_iota(I32, (N_EXPERTS, nb), 1).astype(F32)
        inside = jnp.logical_and(bid >= bstart[:, 0:1], bid < bend[:, 0:1])
        erow = lax.broadcasted_iota(I32, (N_EXPERTS, nb), 0).astype(F32)
        bexp_ref[...] = jnp.sum(jnp.where(inside, erow, 0.0), axis=0, keepdims=True).astype(I32)
        valid = jnp.clip(cnt[:, 0:1] - (bid - bstart[:, 0:1]) * MOE_ROWS, 0.0, float(MOE_ROWS))
        bval_ref[...] = jnp.sum(jnp.where(inside, valid, 0.0), axis=0, keepdims=True).astype(I32)
        nused_ref[...] = jnp.max(bend, axis=0, keepdims=True).astype(I32)


def _destinations(counts, lower, idx_t, pos_t, n_blocks_pad):
    n = idx_t.shape[1]
    tt = TOK_TILE
    blk = pl.BlockSpec((TOP_K, tt), lambda i: (0, i))
    one = lambda w: pl.BlockSpec((1, w), lambda i: (0, 0))
    return pl.pallas_call(
        _dest_kernel,
        grid=(n // tt,),
        in_specs=[pl.BlockSpec((N_EXPERTS, LANES), lambda i: (0, 0)),
                  pl.BlockSpec((N_EXPERTS, N_EXPERTS), lambda i: (0, 0)), blk, blk],
        out_specs=(blk, one(n_blocks_pad), one(n_blocks_pad), one(LANES)),
        out_shape=(jax.ShapeDtypeStruct((TOP_K, n), I32),
                   jax.ShapeDtypeStruct((1, n_blocks_pad), I32),
                   jax.ShapeDtypeStruct((1, n_blocks_pad), I32),
                   jax.ShapeDtypeStruct((1, LANES), I32)),
        compiler_params=_cparams(1), name="destinations",
    )(counts, lower, idx_t, pos_t)


SC_WINDOW = 128
SC_WINDOWS_PER_STEP = 8


def _invert_rows(dest, ids, n_rows):
    m = dest.size
    mesh = plsc.VectorSubcoreMesh(core_axis_name="core", subcore_axis_name="subcore")

    @functools.partial(pl.kernel, out_type=jax.ShapeDtypeStruct((n_rows,), I32), mesh=mesh,
                       scratch_types=[])
    def invert(val_hbm, idx_hbm, out_hbm):
        def body(val_vmem, idx_vmem):
            for j in range(SC_WINDOWS_PER_STEP):
                pltpu.sync_copy(val_vmem.at[j], out_hbm.at[idx_vmem.at[j]])

        blk = pl.BlockSpec((SC_WINDOWS_PER_STEP, SC_WINDOW), lambda i: (i, 0))
        pltpu.emit_pipeline(
            body, grid=(m // (SC_WINDOW * SC_WINDOWS_PER_STEP),),
            in_specs=[blk, blk], out_specs=[], core_axis_name=("core", "subcore"),
            dimension_semantics=(pltpu.PARALLEL,),
        )(val_hbm, idx_hbm)

    shape = (m // SC_WINDOW, SC_WINDOW)
    return invert(jnp.asarray(ids.reshape(shape), I32), dest.reshape(shape))


def _pack_rows(ref, x, row0=0):
    bits = pltpu.bitcast(x.astype(BF16).astype(F32), U32)
    for s in range(PACK_CHUNKS):
        lo = bits[:, (2 * s) * LANES:(2 * s + 1) * LANES] >> 16
        hi = bits[:, (2 * s + 1) * LANES:(2 * s + 2) * LANES] & jnp.uint32(HIGH_HALF)
        ref[pl.ds(row0 + s, x.shape[0], stride=PACK_CHUNKS), :] = lo | hi


def _unpack_rows(ref, n_rows, row0=0):
    parts = []
    for s in range(PACK_CHUNKS):
        w = ref[pl.ds(row0 + s, n_rows, stride=PACK_CHUNKS), :]
        parts.append(pltpu.bitcast(w << 16, F32))
        parts.append(pltpu.bitcast(w & jnp.uint32(HIGH_HALF), F32))
    return jnp.concatenate(parts, axis=1).astype(BF16)


def _moe_kernel(bexp_ref, bval_ref, nused_ref, u2p_hbm, gat_hbm, dst_hbm, w1_hbm, w3_hbm, w2_hbm,
                yt_hbm, u2p_vmem, w1_f, w3_f, w2_f, w13_s, w2_s, xbuf, ybuf, gat_smem, dst_smem,
                sem_in, sem_src, sem_w, sem_out):
    n_used = nused_ref[0]
    br = MOE_ROWS
    grp = SRC_GROUP * br

    def src_copies(g):
        window = pl.ds(g * grp, grp)
        ring = pl.ds(lax.rem(g, 2) * grp, grp)
        return (pltpu.make_async_copy(gat_hbm.at[window], gat_smem.at[ring], sem_src),
                pltpu.make_async_copy(dst_hbm.at[window], dst_smem.at[ring], sem_src))

    def out_wait(slot):
        pltpu.make_async_copy(ybuf.at[pl.ds(slot * br * PACK_CHUNKS, br * PACK_CHUNKS)],
                              yt_hbm.at[pl.ds(0, br * PACK_CHUNKS)], sem_out.at[slot]).wait()

    def src_base(blk):
        return lax.rem(blk // SRC_GROUP, 2) * grp + lax.rem(blk, SRC_GROUP) * br

    def scatter_row(blk_slot, sbase, r, priority=0):
        pltpu.make_async_copy(
            ybuf.at[pl.ds(pl.multiple_of((blk_slot * br + r) * PACK_CHUNKS, PACK_CHUNKS), PACK_CHUNKS)],
            yt_hbm.at[pl.ds(pl.multiple_of(dst_smem[sbase + r], PACK_CHUNKS), PACK_CHUNKS)],
            sem_out.at[blk_slot]).start(priority=priority)

    def gather_row(xslot, sbase, r):
        dst = pl.multiple_of((xslot * br + r) * PACK_CHUNKS, PACK_CHUNKS)
        row = pl.multiple_of(gat_smem[sbase + r], PACK_CHUNKS)
        xbuf[pl.ds(dst, PACK_CHUNKS), :] = u2p_vmem[pl.ds(row, PACK_CHUNKS), :]

    def weight_copies(e, wslot):
        return [pltpu.make_async_copy(src.at[e], dst.at[wslot], sem_w.at[wslot])
                for src, dst in ((w1_hbm, w1_f), (w3_hbm, w3_f), (w2_hbm, w2_f))]

    cp = pltpu.make_async_copy(u2p_hbm, u2p_vmem, sem_in)
    cp.start()
    for scp in src_copies(0):
        scp.start()
    for wcp in weight_copies(bexp_ref[0], 0):
        wcp.start()
    ybuf[...] = jnp.zeros_like(ybuf)
    cp.wait()
    for scp in src_copies(0):
        scp.wait()
    lax.fori_loop(0, br, lambda r, c: (gather_row(0, 0, r), c)[1], 0)

    def block(b, wslot):
        g = b // SRC_GROUP
        phase = lax.rem(b, SRC_GROUP)

        more = (g + 1) * SRC_GROUP < n_used

        @pl.when(jnp.logical_and(phase == 1, more))
        def _():
            for scp in src_copies(g + 1):
                scp.start()

        @pl.when(jnp.logical_and(phase == SRC_GROUP - 1, more))
        def _():
            for scp in src_copies(g + 1):
                scp.wait()

        e = bexp_ref[b]
        prev = bexp_ref[jnp.maximum(b - 1, 0)]

        @pl.when(jnp.logical_or(b == 0, e != prev))
        def _():
            for wcp in weight_copies(e, wslot):
                wcp.wait()
            w13_s[:, 0:EXPERT_FF] = w1_f[wslot].astype(BF16)
            w13_s[:, EXPERT_FF:2 * EXPERT_FF] = w3_f[wslot].astype(BF16)
            w2_s[...] = w2_f[wslot].astype(BF16)
            nxt = lax.while_loop(
                lambda j: jnp.logical_and(j < n_used, bexp_ref[jnp.minimum(j, n_used - 1)] == e),
                lambda j: j + 1, b + 1)

            @pl.when(nxt < n_used)
            def _():
                for wcp in weight_copies(bexp_ref[nxt], 1 - wslot):
                    wcp.start()

        switch = jnp.logical_and(b + 1 < n_used, bexp_ref[b + 1] != e)

        valid = bval_ref[b]
        slot = lax.rem(b, 2)

        p_base = src_base(b)
        n_base = src_base(b + 1)
        for r in range(br):
            gather_row(1 - slot, n_base, r)
            scatter_row(1 - slot, p_base, r, priority=r % 2)

        x = _unpack_rows(xbuf, br, slot * (br * PACK_CHUNKS))
        rows = lax.broadcasted_iota(I32, x.shape, 0)
        x = jnp.where(rows < valid, x, jnp.zeros_like(x))
        ab = jnp.dot(x, w13_s[...], preferred_element_type=F32)
        hid = (_silu(ab[:, 0:EXPERT_FF]) * ab[:, EXPERT_FF:2 * EXPERT_FF]).astype(BF16)
        y = jnp.dot(hid, w2_s[...], preferred_element_type=F32)

        @pl.when(b >= 1)
        def _():
            out_wait(slot)

        _pack_rows(ybuf, y, slot * (br * PACK_CHUNKS))
        return jnp.where(switch, 1 - wslot, wslot)

    lax.fori_loop(0, n_used, block, 0)

    last = n_used - 1
    l_slot = lax.rem(last, 2)
    for scp in src_copies(n_used // SRC_GROUP):
        scp.start()
    for scp in src_copies(n_used // SRC_GROUP):
        scp.wait()
    l_base = src_base(n_used)
    lax.fori_loop(0, br, lambda r, c: (scatter_row(l_slot, l_base, r), c)[1], 0)
    out_wait(1 - l_slot)
    out_wait(l_slot)


def _row_tables_kernel(src_ref, srcp_ref, bvalp_ref, gat_ref, dst_ref, *, n_tokens):
    src = src_ref[...]
    row = lax.shift_right_logical(src, TOP_K_LOG2 - PACK_CHUNKS_LOG2)
    row = row & (int(jnp.iinfo(I32).max) - (PACK_CHUNKS - 1))
    gat_ref[...] = jnp.minimum(row, (n_tokens - 1) * PACK_CHUNKS)
    blk = lax.broadcasted_iota(I32, srcp_ref.shape, 0)
    lane = lax.broadcasted_iota(I32, srcp_ref.shape, 1)
    spare = n_tokens * TOP_K + ((blk + 1) & 1) * MOE_ROWS + lane
    dst_ref[...] = jnp.where(lane < bvalp_ref[...], srcp_ref[...], spare) * PACK_CHUNKS


def _row_tables(row_src, bval, n_blocks, n_tokens):
    n_tab = -(-(n_blocks + 1) // SRC_GROUP) * SRC_GROUP
    src2 = row_src.reshape(n_blocks, MOE_ROWS)
    pad = ((1, n_tab - n_blocks - 1), (0, 0))
    src_tab = jnp.pad(src2, ((0, n_tab - n_blocks), (0, 0)))
    srcp = jnp.pad(src2, pad)
    bvalp = jnp.pad(bval.reshape(-1)[:n_blocks, None], pad)
    full = lambda a: pl.BlockSpec(a.shape, lambda i: (0,) * a.ndim)
    tab = jax.ShapeDtypeStruct((n_tab, MOE_ROWS), I32)
    gat, dst = pl.pallas_call(
        functools.partial(_row_tables_kernel, n_tokens=n_tokens),
        grid=(1,), in_specs=[full(src_tab), full(srcp), full(bvalp)],
        out_specs=(full(src_tab), full(src_tab)), out_shape=(tab, tab),
        compiler_params=_cparams(1), name="row_tables",
    )(src_tab, srcp, bvalp)
    return gat.reshape(-1), dst.reshape(-1)


def _moe_experts(bexp, bval, nused, u2p, gat, dst, w1, w3, w2, n_tokens):
    br = MOE_ROWS
    any_spec = pl.BlockSpec(memory_space=pl.ANY)
    grid_spec = pltpu.PrefetchScalarGridSpec(
        num_scalar_prefetch=3, grid=(1,),
        in_specs=[any_spec] * 6,
        out_specs=any_spec,
        scratch_shapes=[pltpu.VMEM(u2p.shape, U32),
                        pltpu.VMEM((2, D_MODEL, EXPERT_FF), F32),
                        pltpu.VMEM((2, D_MODEL, EXPERT_FF), F32),
                        pltpu.VMEM((2, EXPERT_FF, D_MODEL), F32),
                        pltpu.VMEM((D_MODEL, 2 * EXPERT_FF), BF16),
                        pltpu.VMEM((EXPERT_FF, D_MODEL), BF16),
                        pltpu.VMEM((2 * PACK_CHUNKS * br, LANES), U32),
                        pltpu.VMEM((2 * br * PACK_CHUNKS, LANES), U32),
                        pltpu.SMEM((2 * SRC_GROUP * br,), I32),
                        pltpu.SMEM((2 * SRC_GROUP * br,), I32),
                        pltpu.SemaphoreType.DMA, pltpu.SemaphoreType.DMA,
                        pltpu.SemaphoreType.DMA((2,)), pltpu.SemaphoreType.DMA((2,))])
    n_out_tiles = n_tokens * TOP_K + 2 * br
    return pl.pallas_call(
        _moe_kernel, grid_spec=grid_spec,
        out_shape=jax.ShapeDtypeStruct((n_out_tiles * PACK_CHUNKS, LANES), U32),
        compiler_params=pltpu.CompilerParams(dimension_semantics=("arbitrary",),
                                             vmem_limit_bytes=MOE_VMEM_LIMIT),
        name="moe_experts",
    )(bexp, bval, nused, u2p, gat, dst, w1, w3, w2)


def _combine_kernel(w_hbm, yt_ref, base_ref, mod_ref, g_ref, b_ref, yc_ref, yl_ref,
                    w_smem, acc_lo, acc_hi, sem_w, *, n_ctx_tiles):
    i = pl.program_id(0)
    n_steps = pl.num_programs(0)
    n_tok = acc_lo.shape[0] // PACK_CHUNKS
    n_idx = n_tok * TOP_K

    def w_copy(tile):
        return pltpu.make_async_copy(w_hbm.at[pl.ds(tile * n_idx, n_idx)],
                                     w_smem.at[pl.ds(lax.rem(tile, 2) * n_idx, n_idx)], sem_w)

    @pl.when(i == 0)
    def _():
        w_copy(i).start()

    w_copy(i).wait()

    @pl.when(i + 1 < n_steps)
    def _():
        w_copy(i + 1).start()

    wbase = lax.rem(i, 2) * n_idx

    per_tile = SUBLANES // PACK_CHUNKS
    first = lax.broadcasted_iota(I32, (SUBLANES, LANES), 0) < PACK_CHUNKS

    def reduce_token(t):
        lo = hi = None
        for m in range(TOP_K // per_tile):
            j = t * TOP_K + m * per_tile
            words = yt_ref[pl.ds(pl.multiple_of(j * PACK_CHUNKS, SUBLANES), SUBLANES), :]
            wgt = jnp.where(first, w_smem[wbase + j], w_smem[wbase + j + 1])
            t_lo = wgt * pltpu.bitcast(words << 16, F32)
            t_hi = wgt * pltpu.bitcast(words & jnp.uint32(HIGH_HALF), F32)
            lo = t_lo if lo is None else lo + t_lo
            hi = t_hi if hi is None else hi + t_hi
        row = pl.multiple_of(t * PACK_CHUNKS, PACK_CHUNKS)
        acc_lo[pl.ds(row, PACK_CHUNKS), :] = lo[0:PACK_CHUNKS] + lo[PACK_CHUNKS:SUBLANES]
        acc_hi[pl.ds(row, PACK_CHUNKS), :] = hi[0:PACK_CHUNKS] + hi[PACK_CHUNKS:SUBLANES]

    def reduce_group(i, carry):
        for u in range(COMB_UNROLL):
            reduce_token(i * COMB_UNROLL + u)
        return carry

    lax.fori_loop(0, n_tok // COMB_UNROLL, reduce_group, 0)
    parts = []
    for s in range(PACK_CHUNKS):
        parts.append(acc_lo[pl.ds(s, n_tok, stride=PACK_CHUNKS), :])
        parts.append(acc_hi[pl.ds(s, n_tok, stride=PACK_CHUNKS), :])
    moe = jnp.concatenate(parts, axis=1)
    gate2 = mod_ref[:, 5 * D_MODEL:6 * D_MODEL]
    y = _layer_norm(base_ref[...] + gate2 * moe, g_ref[...], b_ref[...])

    @pl.when(i < n_ctx_tiles)
    def _():
        yc_ref[...] = y

    @pl.when(i >= n_ctx_tiles)
    def _():
        yl_ref[...] = y


def _combine(w_flat, yt, base, mod3, l2g, l2b, n_ctx, seq_tokens):
    n = base.shape[0]
    tc = COMB_TILE
    n_ctx_tiles = n_ctx // tc
    n_seq_tiles = seq_tokens // tc

    def mod_idx(i):
        return (jnp.where(i < n_ctx_tiles, 0, 1 + (i - n_ctx_tiles) // n_seq_tiles), 0, 0)

    full = lambda a: pl.BlockSpec(a.shape, lambda i: (0,) * a.ndim)
    return pl.pallas_call(
        functools.partial(_combine_kernel, n_ctx_tiles=n_ctx_tiles),
        grid=(n // tc,),
        in_specs=[pl.BlockSpec(memory_space=pl.ANY),
                  pl.BlockSpec((tc * TOP_K * PACK_CHUNKS, LANES), lambda i: (i, 0)),
                  pl.BlockSpec((tc, D_MODEL), lambda i: (i, 0)),
                  pl.BlockSpec((None, 1, mod3.shape[2]), mod_idx), full(l2g), full(l2b)],
        out_specs=(pl.BlockSpec((tc, D_MODEL), lambda i: (jnp.minimum(i, n_ctx_tiles - 1), 0)),
                   pl.BlockSpec((tc, D_MODEL), lambda i: (jnp.maximum(i - n_ctx_tiles, 0), 0))),
        out_shape=(jax.ShapeDtypeStruct((n_ctx, D_MODEL), F32),
                   jax.ShapeDtypeStruct((n - n_ctx, D_MODEL), F32)),
        scratch_shapes=[pltpu.SMEM((2 * tc * TOP_K,), F32),
                        pltpu.VMEM((tc * PACK_CHUNKS, LANES), F32),
                        pltpu.VMEM((tc * PACK_CHUNKS, LANES), F32),
                        pltpu.SemaphoreType.DMA],
        compiler_params=_cparams(1), name="combine",
    )(w_flat, yt, base, mod3, l2g, l2b)


def _rope_tables(n_tok, tile):
    f32 = np.float32
    rows = n_tok // GRID_W
    row_idx = np.repeat(np.arange(rows, dtype=f32), GRID_W)
    col_idx = np.tile(np.arange(GRID_W, dtype=f32), rows)
    inv_freq = (1.0 / (ROPE_THETA ** (np.arange(0, ROPE_AXIS_DIM, 2, dtype=f32) / ROPE_AXIS_DIM))).astype(f32)
    ang_r = row_idx[:, None] * inv_freq[None, :]
    ang_c = col_idx[:, None] * inv_freq[None, :]
    ang = np.concatenate([ang_r, ang_r, ang_c, ang_c], axis=-1)
    cos, sin = np.cos(ang), np.sin(ang)
    quarter = (np.arange(HEAD_DIM) // (ROPE_AXIS_DIM // 2)) % 2
    sin_a = np.where(quarter == 0, -sin, 0.0)
    sin_b = np.where(quarter == 1, sin, 0.0)
    rep = LANES // HEAD_DIM
    ident = lambda v: np.full((tile, LANES), v, f32)
    cos_t = np.concatenate([np.tile(cos, (1, rep)), ident(1.0)], axis=0)
    sa_t = np.concatenate([np.tile(sin_a, (1, rep)), ident(0.0)], axis=0)
    sb_t = np.concatenate([np.tile(sin_b, (1, rep)), ident(0.0)], axis=0)
    ident_tr = lambda v: np.full((HEAD_DIM, tile), v, f32)
    cos_tr = np.concatenate([cos.T, ident_tr(1.0)], axis=1)
    sin_tr = np.concatenate([sin.T, ident_tr(0.0)], axis=1)
    return tuple(jnp.asarray(t, F32) for t in (cos_t, sa_t, sb_t, cos_tr, sin_tr))


def _dup_heads(a):
    parts = []
    for h in range(KV_HEADS):
        blk = a[..., h * HEAD_DIM:(h + 1) * HEAD_DIM]
        parts += [blk] * (LANES // HEAD_DIM)
    return jnp.concatenate(parts, axis=-1)


def kernel(x_prompt, x_sample, cache_k, cache_v, state_gla_fwd, state_gla_bwd, c, c_ctx, w_ada, b_ada, w_in, q_norm, k_norm, gla_wa_fwd, gla_ba_fwd, gla_wa_bwd, gla_ba_bwd, gla_norm, w_out, ln1_g, ln1_b, ln2_g, ln2_b, w_router, router_bias, exp_w1, exp_w3, exp_w2, sh_w1, sh_w3, sh_w2):
    n_ctx_b, ctx_seq, _ = x_prompt.shape
    n_lat_b, lat_seq, _ = x_sample.shape
    n_ctx = n_ctx_b * ctx_seq
    n_lat = n_lat_b * lat_seq
    n = n_ctx + n_lat
    l = 0

    x_c = x_prompt.reshape(n_ctx, D_MODEL)
    x_l = x_sample.reshape(n_lat, D_MODEL)

    c_rows = jnp.zeros((SUBLANES, D_MODEL), F32).at[0].set(c_ctx).at[1:1 + n_lat_b].set(c)
    mod = _modulation(c_rows, w_ada[l], b_ada[l][None, :])
    mod3 = mod.reshape(SUBLANES, 1, 6 * D_MODEL)

    wi = w_in[l]
    o_q, o_k, o_v, o_gq, o_gk, o_gv, o_gg, o_rf, o_rb, o_end = np.cumsum(
        [0, ATT_WIDTH, KV_HEADS * HEAD_DIM, KV_HEADS * HEAD_DIM, GLA_KW, GLA_KW, GLA_WIDTH, GLA_WIDTH,
         GLA_GATE_RANK, GLA_GATE_RANK])
    w_tok = jnp.concatenate([
        _dup_heads(wi[:, o_k:o_v]), wi[:, o_v:o_gq], wi[:, o_gq:o_gk],
        wi[:, o_gv:o_gg], wi[:, o_gg:o_rf], wi[:, o_rf:o_end],
        jnp.zeros((D_MODEL, LANES - 2 * GLA_GATE_RANK), F32)], axis=1).astype(BF16)
    w_tr = jnp.concatenate([wi[:, o_q:o_k], wi[:, o_v:o_gq], wi[:, o_gk:o_gv], wi[:, o_rf:o_end]],
                           axis=1).T.astype(BF16)
    rep = LANES // HEAD_DIM
    qn = q_norm[l][:, None]
    kn = jnp.tile(k_norm[l], rep)[None, :]
    seg = jnp.asarray(np.kron(np.eye(rep), np.ones((HEAD_DIM, HEAD_DIM))), BF16)
    wa = jnp.zeros((LANES, 2 * GLA_KW), F32)
    wa = wa.at[0:GLA_GATE_RANK, 0:GLA_KW].set(gla_wa_fwd[l])
    wa = wa.at[GLA_GATE_RANK:2 * GLA_GATE_RANK, GLA_KW:].set(gla_wa_bwd[l])
    ba = jnp.concatenate([gla_ba_fwd[l], gla_ba_bwd[l]])[None, :]
    wat = wa[0:2 * GLA_GATE_RANK, :].T
    bat = ba.T
    cos_t, sa_t, sb_t, cos_tr, sin_tr = _rope_tables(lat_seq, TOK_TILE)

    (qt, k_dup, vt, k32, v32, gq, gv, gg, la, gkt, lat) = _in_projection(
        x_c, x_l, mod3, w_tok, w_tr, qn, kn, cos_t, sa_t, sb_t, cos_tr, sin_tr, seg, wa, ba, wat, bat,
        lat_seq // TOK_TILE)

    ck = _dup_heads(cache_k[:, l].reshape(n_lat_b, -1, KV_HEADS * HEAD_DIM)).astype(BF16)
    cvt = cache_v[:, l].reshape(n_lat_b, -1, KV_HEADS * HEAD_DIM).transpose(0, 2, 1).astype(BF16)
    att_c = _attention(qt, k_dup, vt, None, 0, n_ctx_b, ctx_seq)
    att_l = _attention(qt, k_dup, vt, (ck, cvt), n_ctx, n_lat_b, lat_seq)

    gconst, levels_of = _gla_constants()
    to_dev = lambda t: (jnp.asarray(t[0], BF16), jnp.asarray(t[1], BF16), jnp.asarray(t[2], F32))
    bd = jnp.asarray(np.kron(np.eye(GLA_HEADS), np.ones((GLA_DK, GLA_DV))), BF16)
    vbd = jnp.asarray(np.kron(np.eye(GLA_HEADS), np.ones((GLA_CHUNK, GLA_DV))), BF16)
    consts = ((to_dev(gconst["f"]), to_dev(gconst["b"])), levels_of, bd, vbd)
    s_zero = jnp.zeros((n_ctx_b, GLA_HEADS, GLA_DK, GLA_DV), F32)
    of_c, ob_c, sf_new, sb_new = _gla(gq, la, gkt, lat, gv, s_zero, s_zero, consts, 0, n_ctx_b, ctx_seq)
    of_l, ob_l, _, _ = _gla(gq, la, gkt, lat, gv, state_gla_fwd[:, l], state_gla_bwd[:, l], consts,
                            n_ctx, n_lat_b, lat_seq)

    sw13 = jnp.concatenate([sh_w1[l], sh_w3[l]], axis=1).astype(BF16)
    base, u2_rows, logits_t = _out_projection(
        att_c, att_l, of_c, of_l, ob_c, ob_l, gg, x_c, x_l, mod3, w_out[l].astype(BF16),
        gla_norm[l][None, :], ln1_g[l][None, :], ln1_b[l][None, :], w_router[l].T.astype(BF16), sw13,
        sh_w2[l].astype(BF16), lat_seq // TOK_TILE)

    upper = jnp.asarray(np.triu(np.ones((TOK_TILE, TOK_TILE)), 1), BF16)
    idx_t, w_t, pos_t, counts = _route(logits_t, router_bias[l][:, None], upper)
    n_blocks = n * TOP_K // MOE_ROWS + N_EXPERTS
    n_blocks_pad = -(-(n_blocks + 1) // LANES) * LANES
    lower = jnp.asarray(np.tril(np.ones((N_EXPERTS, N_EXPERTS)), -1), F32)
    dest_t, bexp, bval, nused = _destinations(counts, lower, idx_t, pos_t, n_blocks_pad)
    w_flat = w_t.T.reshape(-1)

    ids = np.arange(n, dtype=np.int32)[None, :] * TOP_K + np.arange(TOP_K, dtype=np.int32)[:, None]
    row_src = _invert_rows(dest_t, ids, n_blocks * MOE_ROWS)
    gat, dst = _row_tables(row_src, bval, n_blocks, n)
    yt = _moe_experts(bexp.reshape(-1), bval.reshape(-1), nused.reshape(-1)[0:1], u2_rows, gat, dst,
                      exp_w1[l], exp_w3[l], exp_w2[l], n)
    y_c, y_l = _combine(w_flat, yt, base, mod3, ln2_g[l][None, :], ln2_b[l][None, :], n_ctx, lat_seq)

    y_prompt = y_c.reshape(n_ctx_b, ctx_seq, D_MODEL)
    y_sample = y_l.reshape(n_lat_b, lat_seq, D_MODEL)
    new_cache_k = k32.reshape(n_ctx_b, 1, ctx_seq, KV_HEADS, HEAD_DIM)
    new_cache_v = v32.reshape(n_ctx_b, 1, ctx_seq, KV_HEADS, HEAD_DIM)
    return (y_prompt, y_sample, new_cache_k, new_cache_v, sf_new[:, None], sb_new[:, None])
```
